```python
import jax
import jax.numpy as jnp
from jax import lax
import numpy as np

D_MODEL = 2048
BATCH = 8
SEQ = 4096
DEPTH = 2

PLE_DIM = 256
NORM_EPS = 1e-6
ROPE_THETA = 10000.0

GDN_HEADS = 8
GDN_DK = 128
GDN_DV = 128
GDN_CONV = 4
GDN_CHUNK = 64
MLA_HEADS = 8
MLA_Q_RANK = 512
MLA_KV_RANK = 512
MLA_NOPE = 128
MLA_ROPE = 64
MLA_V = 128
MLA_BLOCK = 128
RET_HEADS = 8
RET_DK = D_MODEL // RET_HEADS
RET_DV = 2 * RET_DK
RET_CHUNK = 64
D_FF = 5632
FFN_CONV = 3

L0_SPLITS = (GDN_HEADS * GDN_DK, GDN_HEADS * GDN_DK, GDN_HEADS * GDN_DV, GDN_HEADS * GDN_DV,
             GDN_HEADS, GDN_HEADS, MLA_Q_RANK, MLA_KV_RANK, MLA_ROPE)
L0_IN = sum(L0_SPLITS)
L0_MIX = GDN_HEADS * GDN_DV + MLA_HEADS * MLA_V
L1_SPLITS = (RET_HEADS * RET_DK, RET_HEADS * RET_DK, RET_HEADS * RET_DV, RET_HEADS * RET_DV)
L1_IN = sum(L1_SPLITS)
L1_MIX = RET_HEADS * RET_DV

kernel_name = 'hybrid_gdn_mla_retnet_block'


def rmsnorm(x, g):
    xf = x.astype(jnp.float32)
    y = xf * lax.rsqrt(jnp.mean(xf * xf, axis=-1, keepdims=True) + NORM_EPS)
    return (y * g.astype(jnp.float32)).astype(x.dtype)


def split_cols(x, sizes):
    idx = [int(i) for i in np.cumsum(sizes)[:-1]]
    return jnp.split(x, idx, axis=-1)


def to_heads(x, n_heads):
    b, s, _ = x.shape
    return x.reshape(b, s, n_heads, -1).transpose(0, 2, 1, 3)


def from_heads(x):
    b, h, s, d = x.shape
    return x.transpose(0, 2, 1, 3).reshape(b, s, h * d)


def causal_dwconv(x, w):
    width, s = w.shape[0], x.shape[1]
    xp = jnp.pad(x, ((0, 0), (width - 1, 0), (0, 0)))
    y = xp[:, 0:s] * w[0]
    for j in range(1, width):
        y = y + xp[:, j:j + s] * w[j]
    return y


def rope_tables(positions, dim):
    inv_freq = ROPE_THETA ** (-jnp.arange(0, dim, 2, dtype=jnp.float32) / dim)
    ang = positions.astype(jnp.float32)[..., None] * inv_freq
    return jnp.cos(ang), jnp.sin(ang)


def apply_rope(x, cos, sin):
    x1, x2 = jnp.split(x, 2, axis=-1)
    return jnp.concatenate([x1 * cos - x2 * sin, x2 * cos + x1 * sin], axis=-1).astype(x.dtype)


def l2norm(x):
    return x * lax.rsqrt(jnp.sum(x * x, axis=-1, keepdims=True) + NORM_EPS)


def chunk(t, c):
    b, h, s = t.shape[:3]
    return t.reshape(b, h, s // c, c, *t.shape[3:])


def gated_deltanet(q, k, v, z, a, b, conv_w, A_log, dt_bias, norm_w):
    f32 = jnp.float32
    dtype = q.dtype
    bsz, s, _ = q.shape
    c = GDN_CHUNK
    qkv = jax.nn.silu(causal_dwconv(jnp.concatenate([q, k, v], axis=-1), conv_w)).astype(f32)
    q, k, v = split_cols(qkv, L0_SPLITS[:3])
    q = l2norm(to_heads(q, GDN_HEADS)) * GDN_DK ** -0.5
    k = l2norm(to_heads(k, GDN_HEADS))
    v = to_heads(v, GDN_HEADS)
    beta = jax.nn.sigmoid(b.astype(f32)).transpose(0, 2, 1)
    g = (-jnp.exp(A_log.astype(f32)) * jax.nn.softplus(a.astype(f32) + dt_bias.astype(f32))
         ).transpose(0, 2, 1)
    qc, kc, vc = chunk(q, c), chunk(k, c), chunk(v, c)
    bc = chunk(beta, c)[..., None]
    G = jnp.cumsum(chunk(g, c), axis=-1)
    incl = jnp.tril(jnp.ones((c, c), dtype=bool))
    strict = jnp.tril(jnp.ones((c, c), dtype=bool), -1)
    gamma = jnp.exp(jnp.where(incl, G[..., :, None] - G[..., None, :], -jnp.inf))
    kb = kc * bc
    A = jnp.where(strict, jnp.einsum('bhncd,bhnsd->bhncs', kb, kc) * gamma, 0.0)
    M = A + jnp.eye(c, dtype=f32)
    rhs = jnp.concatenate([vc * bc, kb * jnp.exp(G)[..., None]], axis=-1)
    sol = lax.linalg.triangular_solve(M, rhs, left_side=True, lower=True, unit_diagonal=True)
    u, w = sol[..., :GDN_DV], sol[..., GDN_DV:]
    attn = jnp.einsum('bhncd,bhnsd->bhncs', qc, kc) * gamma
    q_dec = qc * jnp.exp(G)[..., None]
    g_last = G[..., -1]
    k_dec = kc * jnp.exp(g_last[..., None] - G)[..., None]

    def step(state, xs):
        u_n, w_n, qd_n, kd_n, attn_n, gl_n = xs
        v_new = u_n - jnp.einsum('bhcd,bhde->bhce', w_n, state)
        o_n = jnp.einsum('bhcd,bhde->bhce', qd_n, state) + jnp.einsum('bhcs,bhse->bhce', attn_n, v_new)
        state = state * jnp.exp(gl_n)[..., None, None] + jnp.einsum('bhcd,bhce->bhde', kd_n, v_new)
        return state, o_n

    xs = tuple(jnp.moveaxis(t, 2, 0) for t in (u, w, q_dec, k_dec, attn, g_last))
    state0 = jnp.zeros((bsz, GDN_HEADS, GDN_DK, GDN_DV), f32)
    _, o = lax.scan(step, state0, xs)
    o = jnp.moveaxis(o, 0, 2).reshape(bsz, GDN_HEADS, s, GDN_DV)
    o = rmsnorm(o, norm_w) * jax.nn.silu(to_heads(z.astype(f32), GDN_HEADS))
    return from_heads(o).astype(dtype)


def mla(c_q, c_kv, k_rope, cos, sin, q_norm, w_uq, kv_norm, w_ukv):
    bsz, s, _ = c_q.shape
    q = to_heads(rmsnorm(c_q, q_norm) @ w_uq, MLA_HEADS)
    q_nope = q[..., :MLA_NOPE]
    q_pe = apply_rope(q[..., MLA_NOPE:], cos[:, None], sin[:, None])
    kv = to_heads(rmsnorm(c_kv, kv_norm) @ w_ukv, MLA_HEADS)
    k_nope, v = kv[..., :MLA_NOPE], kv[..., MLA_NOPE:]
    k_pe = apply_rope(k_rope, cos, sin)
    scale = (MLA_NOPE + MLA_ROPE) ** -0.5
    nb = s // MLA_BLOCK
    qn_b = q_nope.reshape(bsz, MLA_HEADS, nb, MLA_BLOCK, MLA_NOPE).transpose(2, 0, 1, 3, 4)
    qp_b = q_pe.reshape(bsz, MLA_HEADS, nb, MLA_BLOCK, MLA_ROPE).transpose(2, 0, 1, 3, 4)
    key_idx = jnp.arange(s)

    def block(args):
        i, qn, qp = args
        sc = (jnp.einsum('bhqd,bhkd->bhqk', qn, k_nope)
              + jnp.einsum('bhqr,bkr->bhqk', qp, k_pe)).astype(jnp.float32) * scale
        q_idx = i * MLA_BLOCK + jnp.arange(MLA_BLOCK)
        sc = jnp.where(key_idx[None, :] <= q_idx[:, None], sc, -jnp.inf)
        pr = jax.nn.softmax(sc, axis=-1).astype(v.dtype)
        return jnp.einsum('bhqk,bhkd->bhqd', pr, v)

    o = lax.map(block, (jnp.arange(nb), qn_b, qp_b))
    o = o.transpose(1, 2, 0, 3, 4).reshape(bsz, MLA_HEADS, s, MLA_V)
    return from_heads(o)


def retention(q, k, v, cos, sin, norm_w):
    f32 = jnp.float32
    dtype = q.dtype
    bsz, s, _ = q.shape
    c = RET_CHUNK
    q = apply_rope(to_heads(q.astype(f32), RET_HEADS), cos[:, None], sin[:, None])
    k = apply_rope(to_heads(k.astype(f32), RET_HEADS), cos[:, None], sin[:, None]) * RET_DK ** -0.5
    v = to_heads(v.astype(f32), RET_HEADS)
    log_gamma = jnp.log1p(-jnp.power(2.0, -5.0 - jnp.arange(RET_HEADS, dtype=f32)))
    pos = jnp.arange(c, dtype=f32)
    incl = jnp.tril(jnp.ones((c, c), dtype=bool))
    decay = jnp.exp(jnp.where(incl, (pos[:, None] - pos[None, :]) * log_gamma[:, None, None], -jnp.inf))
    qc, kc, vc = chunk(q, c), chunk(k, c), chunk(v, c)
    inner = jnp.einsum('bhncs,bhnse->bhnce', jnp.einsum('bhncd,bhnsd->bhncs', qc, kc) * decay[:, None], vc)
    xi = jnp.exp((pos + 1.0) * log_gamma[:, None])
    zeta = jnp.exp((c - 1.0 - pos) * log_gamma[:, None])
    gamma_c = jnp.exp(c * log_gamma)
    q_dec = qc * xi[:, None, :, None]
    k_dec = kc * zeta[:, None, :, None]

    def step(state, xs):
        qd_n, kd_n, v_n = xs
        cross = jnp.einsum('bhcd,bhde->bhce', qd_n, state)
        state = state * gamma_c[:, None, None] + jnp.einsum('bhcd,bhce->bhde', kd_n, v_n)
        return state, cross

    xs = tuple(jnp.moveaxis(t, 2, 0) for t in (q_dec, k_dec, vc))
    state0 = jnp.zeros((bsz, RET_HEADS, RET_DK, RET_DV), f32)
    _, cross = lax.scan(step, state0, xs)
    o = (inner + jnp.moveaxis(cross, 0, 2)).reshape(bsz, RET_HEADS, s, RET_DV)
    mu = jnp.mean(o, axis=-1, keepdims=True)
    var = jnp.mean(jnp.square(o - mu), axis=-1, keepdims=True)
    o = (o - mu) * lax.rsqrt(var + NORM_EPS)
    return (from_heads(o) * norm_w.astype(f32)).astype(dtype)


def even_mixer(hn, cos, sin, w_in, gdn_conv, gdn_A_log, gdn_dt_bias, gdn_norm,
               q_norm, w_uq, kv_norm, w_ukv, w_out):
    q, k, v, z, a, b, c_q, c_kv, k_rope = split_cols(hn @ w_in, L0_SPLITS)
    y_a = gated_deltanet(q, k, v, z, a, b, gdn_conv, gdn_A_log, gdn_dt_bias, gdn_norm)
    y_b = mla(c_q, c_kv, k_rope, cos, sin, q_norm, w_uq, kv_norm, w_ukv)
    return jnp.concatenate([y_a, y_b], axis=-1) @ w_out


def odd_mixer(hn, cos, sin, w_in, ret_norm, w_out):
    q, k, v, g = split_cols(hn @ w_in, L1_SPLITS)
    y = retention(q, k, v, cos, sin, ret_norm)
    return (jax.nn.silu(g) * y) @ w_out


def conv_ffn(hn, w_up, conv_w, conv_b, w_down):
    u = causal_dwconv(hn @ w_up, conv_w) + conv_b
    gate, up = jnp.split(u, 2, axis=-1)
    return (jax.nn.silu(gate) * up) @ w_down


def ple_term(h, p_i, w_proj, gate_norm, w_gate):
    return (p_i @ w_proj) * jax.nn.sigmoid(rmsnorm(h, gate_norm) @ w_gate)


def _fwd_setup_inputs(seed: int = 0) -> dict:
    key = jax.random.key(seed)
    keys = jax.random.split(key, 64)
    counter = [0]

    def nk():
        kk = keys[counter[0]]
        counter[0] += 1
        return kk

    def dense(fan_in, fan_out):
        return jax.random.normal(nk(), (fan_in, fan_out), jnp.float32) * fan_in ** -0.5

    def gain(n):
        return 1.0 + 0.02 * jax.random.normal(nk(), (n,), jnp.float32)

    x = jax.random.normal(nk(), (BATCH, SEQ, D_MODEL), jnp.float32)
    p = jax.random.normal(nk(), (DEPTH, BATCH, SEQ, PLE_DIM), jnp.float32)
    offset = jax.random.randint(nk(), (BATCH, 1), 0, 1024, jnp.int32)
    positions = offset + jnp.arange(SEQ, dtype=jnp.int32)[None, :]

    gdn_ch = 2 * GDN_HEADS * GDN_DK + GDN_HEADS * GDN_DV
    dt = jnp.exp(jax.random.uniform(nk(), (GDN_HEADS,), jnp.float32, np.log(1e-3), np.log(1e-1)))

    def ffn_params():
        return (gain(D_MODEL), dense(D_MODEL, 2 * D_FF),
                jax.random.normal(nk(), (FFN_CONV, 2 * D_FF), jnp.float32) * FFN_CONV ** -0.5,
                0.01 * jax.random.normal(nk(), (2 * D_FF,), jnp.float32),
                dense(D_FF, D_MODEL))

    def ple_params():
        return (dense(PLE_DIM, D_MODEL), gain(D_MODEL), dense(D_MODEL, D_MODEL))

    out = {'x': x, 'p': p, 'positions': positions}
    out['l0_attn_norm'] = gain(D_MODEL)
    out['l0_w_in'] = dense(D_MODEL, L0_IN)
    out['l0_gdn_conv'] = jax.random.normal(nk(), (GDN_CONV, gdn_ch), jnp.float32) * GDN_CONV ** -0.5
    out['l0_gdn_A_log'] = jnp.log(jax.random.uniform(nk(), (GDN_HEADS,), jnp.float32, 1.0, 16.0))
    out['l0_gdn_dt_bias'] = dt + jnp.log(-jnp.expm1(-dt))
    out['l0_gdn_norm'] = gain(GDN_DV)
    out['l0_mla_q_norm'] = gain(MLA_Q_RANK)
    out['l0_mla_w_uq'] = dense(MLA_Q_RANK, MLA_HEADS * (MLA_NOPE + MLA_ROPE))
    out['l0_mla_kv_norm'] = gain(MLA_KV_RANK)
    out['l0_mla_w_ukv'] = dense(MLA_KV_RANK, MLA_HEADS * (MLA_NOPE + MLA_V))
    out['l0_w_out'] = dense(L0_MIX, D_MODEL)
    (out['l0_ffn_norm'], out['l0_ffn_w_up'], out['l0_ffn_conv_w'], out['l0_ffn_conv_b'],
     out['l0_ffn_w_down']) = ffn_params()
    out['l0_ple_proj'], out['l0_ple_gate_norm'], out['l0_ple_gate'] = ple_params()
    out['l1_attn_norm'] = gain(D_MODEL)
    out['l1_w_in'] = dense(D_MODEL, L1_IN)
    out['l1_ret_norm'] = gain(L1_MIX)
    out['l1_w_out'] = dense(L1_MIX, D_MODEL)
    (out['l1_ffn_norm'], out['l1_ffn_w_up'], out['l1_ffn_conv_w'], out['l1_ffn_conv_b'],
     out['l1_ffn_w_down']) = ffn_params()
    out['l1_ple_proj'], out['l1_ple_gate_norm'], out['l1_ple_gate'] = ple_params()
    out['final_norm'] = gain(D_MODEL)
    return out


def _fwd_reference(x, p, positions,
              l0_attn_norm, l0_w_in, l0_gdn_conv, l0_gdn_A_log, l0_gdn_dt_bias, l0_gdn_norm,
              l0_mla_q_norm, l0_mla_w_uq, l0_mla_kv_norm, l0_mla_w_ukv, l0_w_out,
              l0_ffn_norm, l0_ffn_w_up, l0_ffn_conv_w, l0_ffn_conv_b, l0_ffn_w_down,
              l0_ple_proj, l0_ple_gate_norm, l0_ple_gate,
              l1_attn_norm, l1_w_in, l1_ret_norm, l1_w_out,
              l1_ffn_norm, l1_ffn_w_up, l1_ffn_conv_w, l1_ffn_conv_b, l1_ffn_w_down,
              l1_ple_proj, l1_ple_gate_norm, l1_ple_gate,
              final_norm):
    cos_mla, sin_mla = rope_tables(positions, MLA_ROPE)
    cos_ret, sin_ret = rope_tables(positions, RET_DK)
    mixers = (
        lambda hn: even_mixer(hn, cos_mla, sin_mla, l0_w_in, l0_gdn_conv, l0_gdn_A_log, l0_gdn_dt_bias,
                              l0_gdn_norm, l0_mla_q_norm, l0_mla_w_uq, l0_mla_kv_norm, l0_mla_w_ukv, l0_w_out),
        lambda hn: odd_mixer(hn, cos_ret, sin_ret, l1_w_in, l1_ret_norm, l1_w_out),
    )
    attn_norms = (l0_attn_norm, l1_attn_norm)
    ffn_params = ((l0_ffn_norm, l0_ffn_w_up, l0_ffn_conv_w, l0_ffn_conv_b, l0_ffn_w_down),
                  (l1_ffn_norm, l1_ffn_w_up, l1_ffn_conv_w, l1_ffn_conv_b, l1_ffn_w_down))
    ple_params = ((l0_ple_proj, l0_ple_gate_norm, l0_ple_gate),
                  (l1_ple_proj, l1_ple_gate_norm, l1_ple_gate))
    h = x
    for i in range(DEPTH):
        h = h + mixers[i](rmsnorm(h, attn_norms[i]))
        ffn_norm, w_up, conv_w, conv_b, w_down = ffn_params[i]
        h = h + conv_ffn(rmsnorm(h, ffn_norm), w_up, conv_w, conv_b, w_down)
        w_proj, gate_norm, w_gate = ple_params[i]
        h = h + ple_term(h, p[i], w_proj, gate_norm, w_gate)
    return rmsnorm(h, final_norm)


import jax as _jax
import jax.numpy as _jnp

TWIN_FORMAT = 'train_step'
FWD_PARAMS = ['x', 'p', 'positions', 'l0_attn_norm', 'l0_w_in', 'l0_gdn_conv', 'l0_gdn_A_log', 'l0_gdn_dt_bias', 'l0_gdn_norm', 'l0_mla_q_norm', 'l0_mla_w_uq', 'l0_mla_kv_norm', 'l0_mla_w_ukv', 'l0_w_out', 'l0_ffn_norm', 'l0_ffn_w_up', 'l0_ffn_conv_w', 'l0_ffn_conv_b', 'l0_ffn_w_down', 'l0_ple_proj', 'l0_ple_gate_norm', 'l0_ple_gate', 'l1_attn_norm', 'l1_w_in', 'l1_ret_norm', 'l1_w_out', 'l1_ffn_norm', 'l1_ffn_w_up', 'l1_ffn_conv_w', 'l1_ffn_conv_b', 'l1_ffn_w_down', 'l1_ple_proj', 'l1_ple_gate_norm', 'l1_ple_gate', 'final_norm']
TWIN_WEIGHTS = ['l0_attn_norm', 'l0_w_in', 'l0_gdn_conv', 'l0_gdn_A_log', 'l0_gdn_dt_bias', 'l0_gdn_norm', 'l0_mla_q_norm', 'l0_mla_w_uq', 'l0_mla_kv_norm', 'l0_mla_w_ukv', 'l0_w_out', 'l0_ffn_norm', 'l0_ffn_w_up', 'l0_ffn_conv_w', 'l0_ffn_conv_b', 'l0_ffn_w_down', 'l0_ple_proj', 'l0_ple_gate_norm', 'l0_ple_gate', 'l1_attn_norm', 'l1_w_in', 'l1_ret_norm', 'l1_w_out', 'l1_ffn_norm', 'l1_ffn_w_up', 'l1_ffn_conv_w', 'l1_ffn_conv_b', 'l1_ffn_w_down', 'l1_ple_proj', 'l1_ple_gate_norm', 'l1_ple_gate', 'final_norm']
TWIN_DIFF_INPUT = 'x'
TWIN_INPUTS = ['x', 'p', 'positions', 'l0_attn_norm', 'l0_w_in', 'l0_gdn_conv', 'l0_gdn_A_log', 'l0_gdn_dt_bias', 'l0_gdn_norm', 'l0_mla_q_norm', 'l0_mla_w_uq', 'l0_mla_kv_norm', 'l0_mla_w_ukv', 'l0_w_out', 'l0_ffn_norm', 'l0_ffn_w_up', 'l0_ffn_conv_w', 'l0_ffn_conv_b', 'l0_ffn_w_down', 'l0_ple_proj', 'l0_ple_gate_norm', 'l0_ple_gate', 'l1_attn_norm', 'l1_w_in', 'l1_ret_norm', 'l1_w_out', 'l1_ffn_norm', 'l1_ffn_w_up', 'l1_ffn_conv_w', 'l1_ffn_conv_b', 'l1_ffn_w_down', 'l1_ple_proj', 'l1_ple_gate_norm', 'l1_ple_gate', 'final_norm', 'loss_target', 'm_l0_attn_norm', 'm_l0_w_in', 'm_l0_gdn_conv', 'm_l0_gdn_A_log', 'm_l0_gdn_dt_bias', 'm_l0_gdn_norm', 'm_l0_mla_q_norm', 'm_l0_mla_w_uq', 'm_l0_mla_kv_norm', 'm_l0_mla_w_ukv', 'm_l0_w_out', 'm_l0_ffn_norm', 'm_l0_ffn_w_up', 'm_l0_ffn_conv_w', 'm_l0_ffn_conv_b', 'm_l0_ffn_w_down', 'm_l0_ple_proj', 'm_l0_ple_gate_norm', 'm_l0_ple_gate', 'm_l1_attn_norm', 'm_l1_w_in', 'm_l1_ret_norm', 'm_l1_w_out', 'm_l1_ffn_norm', 'm_l1_ffn_w_up', 'm_l1_ffn_conv_w', 'm_l1_ffn_conv_b', 'm_l1_ffn_w_down', 'm_l1_ple_proj', 'm_l1_ple_gate_norm', 'm_l1_ple_gate', 'm_final_norm', 'v_l0_attn_norm', 'v_l0_w_in', 'v_l0_gdn_conv', 'v_l0_gdn_A_log', 'v_l0_gdn_dt_bias', 'v_l0_gdn_norm', 'v_l0_mla_q_norm', 'v_l0_mla_w_uq', 'v_l0_mla_kv_norm', 'v_l0_mla_w_ukv', 'v_l0_w_out', 'v_l0_ffn_norm', 'v_l0_ffn_w_up', 'v_l0_ffn_conv_w', 'v_l0_ffn_conv_b', 'v_l0_ffn_w_down', 'v_l0_ple_proj', 'v_l0_ple_gate_norm', 'v_l0_ple_gate', 'v_l1_attn_norm', 'v_l1_w_in', 'v_l1_ret_norm', 'v_l1_w_out', 'v_l1_ffn_norm', 'v_l1_ffn_w_up', 'v_l1_ffn_conv_w', 'v_l1_ffn_conv_b', 'v_l1_ffn_w_down', 'v_l1_ple_proj', 'v_l1_ple_gate_norm', 'v_l1_ple_gate', 'v_final_norm']
TWIN_OUTPUTS = ['loss', 'grad_x', 'grad_l0_attn_norm', 'grad_l0_w_in', 'grad_l0_gdn_conv', 'grad_l0_gdn_A_log', 'grad_l0_gdn_dt_bias', 'grad_l0_gdn_norm', 'grad_l0_mla_q_norm', 'grad_l0_mla_w_uq', 'grad_l0_mla_kv_norm', 'grad_l0_mla_w_ukv', 'grad_l0_w_out', 'grad_l0_ffn_norm', 'grad_l0_ffn_w_up', 'grad_l0_ffn_conv_w', 'grad_l0_ffn_conv_b', 'grad_l0_ffn_w_down', 'grad_l0_ple_proj', 'grad_l0_ple_gate_norm', 'grad_l0_ple_gate', 'grad_l1_attn_norm', 'grad_l1_w_in', 'grad_l1_ret_norm', 'grad_l1_w_out', 'grad_l1_ffn_norm', 'grad_l1_ffn_w_up', 'grad_l1_ffn_conv_w', 'grad_l1_ffn_conv_b', 'grad_l1_ffn_w_down', 'grad_l1_ple_proj', 'grad_l1_ple_gate_norm', 'grad_l1_ple_gate', 'grad_final_norm', 'delta_l0_attn_norm', 'delta_l0_w_in', 'delta_l0_gdn_conv', 'delta_l0_gdn_A_log', 'delta_l0_gdn_dt_bias', 'delta_l0_gdn_norm', 'delta_l0_mla_q_norm', 'delta_l0_mla_w_uq', 'delta_l0_mla_kv_norm', 'delta_l0_mla_w_ukv', 'delta_l0_w_out', 'delta_l0_ffn_norm', 'delta_l0_ffn_w_up', 'delta_l0_ffn_conv_w', 'delta_l0_ffn_conv_b', 'delta_l0_ffn_w_down', 'delta_l0_ple_proj', 'delta_l0_ple_gate_norm', 'delta_l0_ple_gate', 'delta_l1_attn_norm', 'delta_l1_w_in', 'delta_l1_ret_norm', 'delta_l1_w_out', 'delta_l1_ffn_norm', 'delta_l1_ffn_w_up', 'delta_l1_ffn_conv_w', 'delta_l1_ffn_conv_b', 'delta_l1_ffn_w_down', 'delta_l1_ple_proj', 'delta_l1_ple_gate_norm', 'delta_l1_ple_gate', 'delta_final_norm', 'new_m_l0_attn_norm', 'new_m_l0_w_in', 'new_m_l0_gdn_conv', 'new_m_l0_gdn_A_log', 'new_m_l0_gdn_dt_bias', 'new_m_l0_gdn_norm', 'new_m_l0_mla_q_norm', 'new_m_l0_mla_w_uq', 'new_m_l0_mla_kv_norm', 'new_m_l0_mla_w_ukv', 'new_m_l0_w_out', 'new_m_l0_ffn_norm', 'new_m_l0_ffn_w_up', 'new_m_l0_ffn_conv_w', 'new_m_l0_ffn_conv_b', 'new_m_l0_ffn_w_down', 'new_m_l0_ple_proj', 'new_m_l0_ple_gate_norm', 'new_m_l0_ple_gate', 'new_m_l1_attn_norm', 'new_m_l1_w_in', 'new_m_l1_ret_norm', 'new_m_l1_w_out', 'new_m_l1_ffn_norm', 'new_m_l1_ffn_w_up', 'new_m_l1_ffn_conv_w', 'new_m_l1_ffn_conv_b', 'new_m_l1_ffn_w_down', 'new_m_l1_ple_proj', 'new_m_l1_ple_gate_norm', 'new_m_l1_ple_gate', 'new_m_final_norm', 'new_v_l0_attn_norm', 'new_v_l0_w_in', 'new_v_l0_gdn_conv', 'new_v_l0_gdn_A_log', 'new_v_l0_gdn_dt_bias', 'new_v_l0_gdn_norm', 'new_v_l0_mla_q_norm', 'new_v_l0_mla_w_uq', 'new_v_l0_mla_kv_norm', 'new_v_l0_mla_w_ukv', 'new_v_l0_w_out', 'new_v_l0_ffn_norm', 'new_v_l0_ffn_w_up', 'new_v_l0_ffn_conv_w', 'new_v_l0_ffn_conv_b', 'new_v_l0_ffn_w_down', 'new_v_l0_ple_proj', 'new_v_l0_ple_gate_norm', 'new_v_l0_ple_gate', 'new_v_l1_attn_norm', 'new_v_l1_w_in', 'new_v_l1_ret_norm', 'new_v_l1_w_out', 'new_v_l1_ffn_norm', 'new_v_l1_ffn_w_up', 'new_v_l1_ffn_conv_w', 'new_v_l1_ffn_conv_b', 'new_v_l1_ffn_w_down', 'new_v_l1_ple_proj', 'new_v_l1_ple_gate_norm', 'new_v_l1_ple_gate', 'new_v_final_norm']
TWIN_LEAF_KINDS = {'loss': 'loss', 'grad_x': 'grad_x', 'grad_l0_attn_norm': 'grad_w', 'grad_l0_w_in': 'grad_w', 'grad_l0_gdn_conv': 'grad_w', 'grad_l0_gdn_A_log': 'grad_w', 'grad_l0_gdn_dt_bias': 'grad_w', 'grad_l0_gdn_norm': 'grad_w', 'grad_l0_mla_q_norm': 'grad_w', 'grad_l0_mla_w_uq': 'grad_w', 'grad_l0_mla_kv_norm': 'grad_w', 'grad_l0_mla_w_ukv': 'grad_w', 'grad_l0_w_out': 'grad_w', 'grad_l0_ffn_norm': 'grad_w', 'grad_l0_ffn_w_up': 'grad_w', 'grad_l0_ffn_conv_w': 'grad_w', 'grad_l0_ffn_conv_b': 'grad_w', 'grad_l0_ffn_w_down': 'grad_w', 'grad_l0_ple_proj': 'grad_w', 'grad_l0_ple_gate_norm': 'grad_w', 'grad_l0_ple_gate': 'grad_w', 'grad_l1_attn_norm': 'grad_w', 'grad_l1_w_in': 'grad_w', 'grad_l1_ret_norm': 'grad_w', 'grad_l1_w_out': 'grad_w', 'grad_l1_ffn_norm': 'grad_w', 'grad_l1_ffn_w_up': 'grad_w', 'grad_l1_ffn_conv_w': 'grad_w', 'grad_l1_ffn_conv_b': 'grad_w', 'grad_l1_ffn_w_down': 'grad_w', 'grad_l1_ple_proj': 'grad_w', 'grad_l1_ple_gate_norm': 'grad_w', 'grad_l1_ple_gate': 'grad_w', 'grad_final_norm': 'grad_w', 'delta_l0_attn_norm': 'delta_w', 'delta_l0_w_in': 'delta_w', 'delta_l0_gdn_conv': 'delta_w', 'delta_l0_gdn_A_log': 'delta_w', 'delta_l0_gdn_dt_bias': 'delta_w', 'delta_l0_gdn_norm': 'delta_w', 'delta_l0_mla_q_norm': 'delta_w', 'delta_l0_mla_w_uq': 'delta_w', 'delta_l0_mla_kv_norm': 'delta_w', 'delta_l0_mla_w_ukv': 'delta_w', 'delta_l0_w_out': 'delta_w', 'delta_l0_ffn_norm': 'delta_w', 'delta_l0_ffn_w_up': 'delta_w', 'delta_l0_ffn_conv_w': 'delta_w', 'delta_l0_ffn_conv_b': 'delta_w', 'delta_l0_ffn_w_down': 'delta_w', 'delta_l0_ple_proj': 'delta_w', 'delta_l0_ple_gate_norm': 'delta_w', 'delta_l0_ple_gate': 'delta_w', 'delta_l1_attn_norm': 'delta_w', 'delta_l1_w_in': 'delta_w', 'delta_l1_ret_norm': 'delta_w', 'delta_l1_w_out': 'delta_w', 'delta_l1_ffn_norm': 'delta_w', 'delta_l1_ffn_w_up': 'delta_w', 'delta_l1_ffn_conv_w': 'delta_w', 'delta_l1_ffn_conv_b': 'delta_w', 'delta_l1_ffn_w_down': 'delta_w', 'delta_l1_ple_proj': 'delta_w', 'delta_l1_ple_gate_norm': 'delta_w', 'delta_l1_ple_gate': 'delta_w', 'delta_final_norm': 'delta_w', 'new_m_l0_attn_norm': 'new_m', 'new_m_l0_w_in': 'new_m', 'new_m_l0_gdn_conv': 'new_m', 'new_m_l0_gdn_A_log': 'new_m', 'new_m_l0_gdn_dt_bias': 'new_m', 'new_m_l0_gdn_norm': 'new_m', 'new_m_l0_mla_q_norm': 'new_m', 'new_m_l0_mla_w_uq': 'new_m', 'new_m_l0_mla_kv_norm': 'new_m', 'new_m_l0_mla_w_ukv': 'new_m', 'new_m_l0_w_out': 'new_m', 'new_m_l0_ffn_norm': 'new_m', 'new_m_l0_ffn_w_up': 'new_m', 'new_m_l0_ffn_conv_w': 'new_m', 'new_m_l0_ffn_conv_b': 'new_m', 'new_m_l0_ffn_w_down': 'new_m', 'new_m_l0_ple_proj': 'new_m', 'new_m_l0_ple_gate_norm': 'new_m', 'new_m_l0_ple_gate': 'new_m', 'new_m_l1_attn_norm': 'new_m', 'new_m_l1_w_in': 'new_m', 'new_m_l1_ret_norm': 'new_m', 'new_m_l1_w_out': 'new_m', 'new_m_l1_ffn_norm': 'new_m', 'new_m_l1_ffn_w_up': 'new_m', 'new_m_l1_ffn_conv_w': 'new_m', 'new_m_l1_ffn_conv_b': 'new_m', 'new_m_l1_ffn_w_down': 'new_m', 'new_m_l1_ple_proj': 'new_m', 'new_m_l1_ple_gate_norm': 'new_m', 'new_m_l1_ple_gate': 'new_m', 'new_m_final_norm': 'new_m', 'new_v_l0_attn_norm': 'new_v', 'new_v_l0_w_in': 'new_v', 'new_v_l0_gdn_conv': 'new_v', 'new_v_l0_gdn_A_log': 'new_v', 'new_v_l0_gdn_dt_bias': 'new_v', 'new_v_l0_gdn_norm': 'new_v', 'new_v_l0_mla_q_norm': 'new_v', 'new_v_l0_mla_w_uq': 'new_v', 'new_v_l0_mla_kv_norm': 'new_v', 'new_v_l0_mla_w_ukv': 'new_v', 'new_v_l0_w_out': 'new_v', 'new_v_l0_ffn_norm': 'new_v', 'new_v_l0_ffn_w_up': 'new_v', 'new_v_l0_ffn_conv_w': 'new_v', 'new_v_l0_ffn_conv_b': 'new_v', 'new_v_l0_ffn_w_down': 'new_v', 'new_v_l0_ple_proj': 'new_v', 'new_v_l0_ple_gate_norm': 'new_v', 'new_v_l0_ple_gate': 'new_v', 'new_v_l1_attn_norm': 'new_v', 'new_v_l1_w_in': 'new_v', 'new_v_l1_ret_norm': 'new_v', 'new_v_l1_w_out': 'new_v', 'new_v_l1_ffn_norm': 'new_v', 'new_v_l1_ffn_w_up': 'new_v', 'new_v_l1_ffn_conv_w': 'new_v', 'new_v_l1_ffn_conv_b': 'new_v', 'new_v_l1_ffn_w_down': 'new_v', 'new_v_l1_ple_proj': 'new_v', 'new_v_l1_ple_gate_norm': 'new_v', 'new_v_l1_ple_gate': 'new_v', 'new_v_final_norm': 'new_v'}


def _forward(args):
    return _fwd_reference(*[args[k] for k in FWD_PARAMS])


def _output_shape():
    def fwd():
        inp = _fwd_setup_inputs(0)
        return _fwd_reference(*[inp[k] for k in FWD_PARAMS])
    out = _jax.eval_shape(fwd)
    return out.shape, out.dtype

N_MICROBATCH = 1
ADAM_LR = 0.001
ADAM_B1 = 0.9
ADAM_B2 = 0.999
ADAM_EPS = 1e-08
ADAM_WD = 0.01
ADAM_STEP = 10
PER_EXAMPLE_BATCH_AXIS = {'x': 0, 'p': 1, 'positions': 0, 'loss_target': 0}
SHARED_INPUTS = []
_WEIGHT_DTYPES = {'l0_attn_norm': _jnp.float32, 'l0_w_in': _jnp.float32, 'l0_gdn_conv': _jnp.float32, 'l0_gdn_A_log': _jnp.float32, 'l0_gdn_dt_bias': _jnp.float32, 'l0_gdn_norm': _jnp.float32, 'l0_mla_q_norm': _jnp.float32, 'l0_mla_w_uq': _jnp.float32, 'l0_mla_kv_norm': _jnp.float32, 'l0_mla_w_ukv': _jnp.float32, 'l0_w_out': _jnp.float32, 'l0_ffn_norm': _jnp.float32, 'l0_ffn_w_up': _jnp.float32, 'l0_ffn_conv_w': _jnp.float32, 'l0_ffn_conv_b': _jnp.float32, 'l0_ffn_w_down': _jnp.float32, 'l0_ple_proj': _jnp.float32, 'l0_ple_gate_norm': _jnp.float32, 'l0_ple_gate': _jnp.float32, 'l1_attn_norm': _jnp.float32, 'l1_w_in': _jnp.float32, 'l1_ret_norm': _jnp.float32, 'l1_w_out': _jnp.float32, 'l1_ffn_norm': _jnp.float32, 'l1_ffn_w_up': _jnp.float32, 'l1_ffn_conv_w': _jnp.float32, 'l1_ffn_conv_b': _jnp.float32, 'l1_ffn_w_down': _jnp.float32, 'l1_ple_proj': _jnp.float32, 'l1_ple_gate_norm': _jnp.float32, 'l1_ple_gate': _jnp.float32, 'final_norm': _jnp.float32}
MOMENT_SCALE = {'l0_attn_norm': 7.641993e-02, 'l0_w_in': 4.863256e-02, 'l0_gdn_conv': 4.691660e-02, 'l0_gdn_A_log': 2.292170e-01, 'l0_gdn_dt_bias': 2.166924e-01, 'l0_gdn_norm': 1.922831e-01, 'l0_mla_q_norm': 2.818814e-02, 'l0_mla_w_uq': 1.610909e-02, 'l0_mla_kv_norm': 4.286704e-02, 'l0_mla_w_ukv': 2.132732e-02, 'l0_w_out': 4.715411e-02, 'l0_ffn_norm': 7.444091e-02, 'l0_ffn_w_up': 3.110231e-02, 'l0_ffn_conv_w': 3.076011e-02, 'l0_ffn_conv_b': 3.046936e-02, 'l0_ffn_w_down': 5.078289e-02, 'l0_ple_proj': 4.453132e-02, 'l0_ple_gate_norm': 1.725023e-02, 'l0_ple_gate': 1.726280e-02, 'l1_attn_norm': 8.235009e-02, 'l1_w_in': 3.026952e-02, 'l1_ret_norm': 2.631492e-02, 'l1_w_out': 3.671473e-02, 'l1_ffn_norm': 4.675512e-02, 'l1_ffn_w_up': 2.004231e-02, 'l1_ffn_conv_w': 2.010569e-02, 'l1_ffn_conv_b': 1.945300e-02, 'l1_ffn_w_down': 3.282985e-02, 'l1_ple_proj': 2.834826e-02, 'l1_ple_gate_norm': 1.134629e-02, 'l1_ple_gate': 1.106924e-02, 'final_norm': 1.598224e+01}


def _to_microbatches(a, axis):
    t = _jnp.moveaxis(a, axis, 0)
    t = t.reshape((N_MICROBATCH, t.shape[0] // N_MICROBATCH) + t.shape[1:])
    return _jnp.moveaxis(t, 1, axis + 1)


def setup_inputs(seed: int = 0) -> dict:
    inp = _fwd_setup_inputs(seed)
    key = _jax.random.fold_in(_jax.random.key(seed), 7919)
    shape, _ = _output_shape()
    out = dict(inp)
    out["loss_target"] = _jax.random.normal(_jax.random.fold_in(key, 0), shape, _jnp.float32)
    for i, name in enumerate(TWIN_WEIGHTS):
        w = inp[name].astype(_jnp.float32)
        if MOMENT_SCALE is None:
            s = _jnp.sqrt(_jnp.mean(_jnp.square(w)) + 1e-30)
        else:
            s = MOMENT_SCALE[name]
        km, kv = _jax.random.split(_jax.random.fold_in(key, i + 1))
        out[name] = w
        out["m_" + name] = s * _jax.random.normal(km, w.shape, _jnp.float32)
        out["v_" + name] = (s * s) * _jax.random.uniform(kv, w.shape, _jnp.float32, 0.5, 1.5)
    if N_MICROBATCH > 1:
        for name, axis in PER_EXAMPLE_BATCH_AXIS.items():
            out[name] = _to_microbatches(out[name], axis)
    return {'x': out['x'], 'p': out['p'], 'positions': out['positions'], 'l0_attn_norm': out['l0_attn_norm'], 'l0_w_in': out['l0_w_in'], 'l0_gdn_conv': out['l0_gdn_conv'], 'l0_gdn_A_log': out['l0_gdn_A_log'], 'l0_gdn_dt_bias': out['l0_gdn_dt_bias'], 'l0_gdn_norm': out['l0_gdn_norm'], 'l0_mla_q_norm': out['l0_mla_q_norm'], 'l0_mla_w_uq': out['l0_mla_w_uq'], 'l0_mla_kv_norm': out['l0_mla_kv_norm'], 'l0_mla_w_ukv': out['l0_mla_w_ukv'], 'l0_w_out': out['l0_w_out'], 'l0_ffn_norm': out['l0_ffn_norm'], 'l0_ffn_w_up': out['l0_ffn_w_up'], 'l0_ffn_conv_w': out['l0_ffn_conv_w'], 'l0_ffn_conv_b': out['l0_ffn_conv_b'], 'l0_ffn_w_down': out['l0_ffn_w_down'], 'l0_ple_proj': out['l0_ple_proj'], 'l0_ple_gate_norm': out['l0_ple_gate_norm'], 'l0_ple_gate': out['l0_ple_gate'], 'l1_attn_norm': out['l1_attn_norm'], 'l1_w_in': out['l1_w_in'], 'l1_ret_norm': out['l1_ret_norm'], 'l1_w_out': out['l1_w_out'], 'l1_ffn_norm': out['l1_ffn_norm'], 'l1_ffn_w_up': out['l1_ffn_w_up'], 'l1_ffn_conv_w': out['l1_ffn_conv_w'], 'l1_ffn_conv_b': out['l1_ffn_conv_b'], 'l1_ffn_w_down': out['l1_ffn_w_down'], 'l1_ple_proj': out['l1_ple_proj'], 'l1_ple_gate_norm': out['l1_ple_gate_norm'], 'l1_ple_gate': out['l1_ple_gate'], 'final_norm': out['final_norm'], 'loss_target': out['loss_target'], 'm_l0_attn_norm': out['m_l0_attn_norm'], 'm_l0_w_in': out['m_l0_w_in'], 'm_l0_gdn_conv': out['m_l0_gdn_conv'], 'm_l0_gdn_A_log': out['m_l0_gdn_A_log'], 'm_l0_gdn_dt_bias': out['m_l0_gdn_dt_bias'], 'm_l0_gdn_norm': out['m_l0_gdn_norm'], 'm_l0_mla_q_norm': out['m_l0_mla_q_norm'], 'm_l0_mla_w_uq': out['m_l0_mla_w_uq'], 'm_l0_mla_kv_norm': out['m_l0_mla_kv_norm'], 'm_l0_mla_w_ukv': out['m_l0_mla_w_ukv'], 'm_l0_w_out': out['m_l0_w_out'], 'm_l0_ffn_norm': out['m_l0_ffn_norm'], 'm_l0_ffn_w_up': out['m_l0_ffn_w_up'], 'm_l0_ffn_conv_w': out['m_l0_ffn_conv_w'], 'm_l0_ffn_conv_b': out['m_l0_ffn_conv_b'], 'm_l0_ffn_w_down': out['m_l0_ffn_w_down'], 'm_l0_ple_proj': out['m_l0_ple_proj'], 'm_l0_ple_gate_norm': out['m_l0_ple_gate_norm'], 'm_l0_ple_gate': out['m_l0_ple_gate'], 'm_l1_attn_norm': out['m_l1_attn_norm'], 'm_l1_w_in': out['m_l1_w_in'], 'm_l1_ret_norm': out['m_l1_ret_norm'], 'm_l1_w_out': out['m_l1_w_out'], 'm_l1_ffn_norm': out['m_l1_ffn_norm'], 'm_l1_ffn_w_up': out['m_l1_ffn_w_up'], 'm_l1_ffn_conv_w': out['m_l1_ffn_conv_w'], 'm_l1_ffn_conv_b': out['m_l1_ffn_conv_b'], 'm_l1_ffn_w_down': out['m_l1_ffn_w_down'], 'm_l1_ple_proj': out['m_l1_ple_proj'], 'm_l1_ple_gate_norm': out['m_l1_ple_gate_norm'], 'm_l1_ple_gate': out['m_l1_ple_gate'], 'm_final_norm': out['m_final_norm'], 'v_l0_attn_norm': out['v_l0_attn_norm'], 'v_l0_w_in': out['v_l0_w_in'], 'v_l0_gdn_conv': out['v_l0_gdn_conv'], 'v_l0_gdn_A_log': out['v_l0_gdn_A_log'], 'v_l0_gdn_dt_bias': out['v_l0_gdn_dt_bias'], 'v_l0_gdn_norm': out['v_l0_gdn_norm'], 'v_l0_mla_q_norm': out['v_l0_mla_q_norm'], 'v_l0_mla_w_uq': out['v_l0_mla_w_uq'], 'v_l0_mla_kv_norm': out['v_l0_mla_kv_norm'], 'v_l0_mla_w_ukv': out['v_l0_mla_w_ukv'], 'v_l0_w_out': out['v_l0_w_out'], 'v_l0_ffn_norm': out['v_l0_ffn_norm'], 'v_l0_ffn_w_up': out['v_l0_ffn_w_up'], 'v_l0_ffn_conv_w': out['v_l0_ffn_conv_w'], 'v_l0_ffn_conv_b': out['v_l0_ffn_conv_b'], 'v_l0_ffn_w_down': out['v_l0_ffn_w_down'], 'v_l0_ple_proj': out['v_l0_ple_proj'], 'v_l0_ple_gate_norm': out['v_l0_ple_gate_norm'], 'v_l0_ple_gate': out['v_l0_ple_gate'], 'v_l1_attn_norm': out['v_l1_attn_norm'], 'v_l1_w_in': out['v_l1_w_in'], 'v_l1_ret_norm': out['v_l1_ret_norm'], 'v_l1_w_out': out['v_l1_w_out'], 'v_l1_ffn_norm': out['v_l1_ffn_norm'], 'v_l1_ffn_w_up': out['v_l1_ffn_w_up'], 'v_l1_ffn_conv_w': out['v_l1_ffn_conv_w'], 'v_l1_ffn_conv_b': out['v_l1_ffn_conv_b'], 'v_l1_ffn_w_down': out['v_l1_ffn_w_down'], 'v_l1_ple_proj': out['v_l1_ple_proj'], 'v_l1_ple_gate_norm': out['v_l1_ple_gate_norm'], 'v_l1_ple_gate': out['v_l1_ple_gate'], 'v_final_norm': out['v_final_norm']}


def _loss(weights, diff, rest, loss_target):
    with _jax.named_scope("forward"):
        args = {**rest, TWIN_DIFF_INPUT: diff, **{k: w.astype(_WEIGHT_DTYPES[k]) for k, w in weights.items()}}
        y = _forward(args)
    with _jax.named_scope("loss_head"):
        err = _jnp.square(y.astype(_jnp.float32) - loss_target)
        return 0.5 * _jnp.sum(_jnp.mean(err, axis=-1)) if err.ndim else 0.5 * err


def _adamw(w, g, m, v):
    m = ADAM_B1 * m + (1.0 - ADAM_B1) * g
    v = ADAM_B2 * v + (1.0 - ADAM_B2) * _jnp.square(g)
    m_hat = m / (1.0 - ADAM_B1 ** ADAM_STEP)
    v_hat = v / (1.0 - ADAM_B2 ** ADAM_STEP)
    delta = -ADAM_LR * (m_hat / (_jnp.sqrt(v_hat) + ADAM_EPS) + ADAM_WD * w)
    return delta, m, v


def reference(x, p, positions, l0_attn_norm, l0_w_in, l0_gdn_conv, l0_gdn_A_log, l0_gdn_dt_bias, l0_gdn_norm, l0_mla_q_norm, l0_mla_w_uq, l0_mla_kv_norm, l0_mla_w_ukv, l0_w_out, l0_ffn_norm, l0_ffn_w_up, l0_ffn_conv_w, l0_ffn_conv_b, l0_ffn_w_down, l0_ple_proj, l0_ple_gate_norm, l0_ple_gate, l1_attn_norm, l1_w_in, l1_ret_norm, l1_w_out, l1_ffn_norm, l1_ffn_w_up, l1_ffn_conv_w, l1_ffn_conv_b, l1_ffn_w_down, l1_ple_proj, l1_ple_gate_norm, l1_ple_gate, final_norm, loss_target, m_l0_attn_norm, m_l0_w_in, m_l0_gdn_conv, m_l0_gdn_A_log, m_l0_gdn_dt_bias, m_l0_gdn_norm, m_l0_mla_q_norm, m_l0_mla_w_uq, m_l0_mla_kv_norm, m_l0_mla_w_ukv, m_l0_w_out, m_l0_ffn_norm, m_l0_ffn_w_up, m_l0_ffn_conv_w, m_l0_ffn_conv_b, m_l0_ffn_w_down, m_l0_ple_proj, m_l0_ple_gate_norm, m_l0_ple_gate, m_l1_attn_norm, m_l1_w_in, m_l1_ret_norm, m_l1_w_out, m_l1_ffn_norm, m_l1_ffn_w_up, m_l1_ffn_conv_w, m_l1_ffn_conv_b, m_l1_ffn_w_down, m_l1_ple_proj, m_l1_ple_gate_norm, m_l1_ple_gate, m_final_norm, v_l0_attn_norm, v_l0_w_in, v_l0_gdn_conv, v_l0_gdn_A_log, v_l0_gdn_dt_bias, v_l0_gdn_norm, v_l0_mla_q_norm, v_l0_mla_w_uq, v_l0_mla_kv_norm, v_l0_mla_w_ukv, v_l0_w_out, v_l0_ffn_norm, v_l0_ffn_w_up, v_l0_ffn_conv_w, v_l0_ffn_conv_b, v_l0_ffn_w_down, v_l0_ple_proj, v_l0_ple_gate_norm, v_l0_ple_gate, v_l1_attn_norm, v_l1_w_in, v_l1_ret_norm, v_l1_w_out, v_l1_ffn_norm, v_l1_ffn_w_up, v_l1_ffn_conv_w, v_l1_ffn_conv_b, v_l1_ffn_w_down, v_l1_ple_proj, v_l1_ple_gate_norm, v_l1_ple_gate, v_final_norm):
    given = dict(x=x, p=p, positions=positions, l0_attn_norm=l0_attn_norm, l0_w_in=l0_w_in, l0_gdn_conv=l0_gdn_conv, l0_gdn_A_log=l0_gdn_A_log, l0_gdn_dt_bias=l0_gdn_dt_bias, l0_gdn_norm=l0_gdn_norm, l0_mla_q_norm=l0_mla_q_norm, l0_mla_w_uq=l0_mla_w_uq, l0_mla_kv_norm=l0_mla_kv_norm, l0_mla_w_ukv=l0_mla_w_ukv, l0_w_out=l0_w_out, l0_ffn_norm=l0_ffn_norm, l0_ffn_w_up=l0_ffn_w_up, l0_ffn_conv_w=l0_ffn_conv_w, l0_ffn_conv_b=l0_ffn_conv_b, l0_ffn_w_down=l0_ffn_w_down, l0_ple_proj=l0_ple_proj, l0_ple_gate_norm=l0_ple_gate_norm, l0_ple_gate=l0_ple_gate, l1_attn_norm=l1_attn_norm, l1_w_in=l1_w_in, l1_ret_norm=l1_ret_norm, l1_w_out=l1_w_out, l1_ffn_norm=l1_ffn_norm, l1_ffn_w_up=l1_ffn_w_up, l1_ffn_conv_w=l1_ffn_conv_w, l1_ffn_conv_b=l1_ffn_conv_b, l1_ffn_w_down=l1_ffn_w_down, l1_ple_proj=l1_ple_proj, l1_ple_gate_norm=l1_ple_gate_norm, l1_ple_gate=l1_ple_gate, final_norm=final_norm, loss_target=loss_target, m_l0_attn_norm=m_l0_attn_norm, m_l0_w_in=m_l0_w_in, m_l0_gdn_conv=m_l0_gdn_conv, m_l0_gdn_A_log=m_l0_gdn_A_log, m_l0_gdn_dt_bias=m_l0_gdn_dt_bias, m_l0_gdn_norm=m_l0_gdn_norm, m_l0_mla_q_norm=m_l0_mla_q_norm, m_l0_mla_w_uq=m_l0_mla_w_uq, m_l0_mla_kv_norm=m_l0_mla_kv_norm, m_l0_mla_w_ukv=m_l0_mla_w_ukv, m_l0_w_out=m_l0_w_out, m_l0_ffn_norm=m_l0_ffn_norm, m_l0_ffn_w_up=m_l0_ffn_w_up, m_l0_ffn_conv_w=m_l0_ffn_conv_w, m_l0_ffn_conv_b=m_l0_ffn_conv_b, m_l0_ffn_w_down=m_l0_ffn_w_down, m_l0_ple_proj=m_l0_ple_proj, m_l0_ple_gate_norm=m_l0_ple_gate_norm, m_l0_ple_gate=m_l0_ple_gate, m_l1_attn_norm=m_l1_attn_norm, m_l1_w_in=m_l1_w_in, m_l1_ret_norm=m_l1_ret_norm, m_l1_w_out=m_l1_w_out, m_l1_ffn_norm=m_l1_ffn_norm, m_l1_ffn_w_up=m_l1_ffn_w_up, m_l1_ffn_conv_w=m_l1_ffn_conv_w, m_l1_ffn_conv_b=m_l1_ffn_conv_b, m_l1_ffn_w_down=m_l1_ffn_w_down, m_l1_ple_proj=m_l1_ple_proj, m_l1_ple_gate_norm=m_l1_ple_gate_norm, m_l1_ple_gate=m_l1_ple_gate, m_final_norm=m_final_norm, v_l0_attn_norm=v_l0_attn_norm, v_l0_w_in=v_l0_w_in, v_l0_gdn_conv=v_l0_gdn_conv, v_l0_gdn_A_log=v_l0_gdn_A_log, v_l0_gdn_dt_bias=v_l0_gdn_dt_bias, v_l0_gdn_norm=v_l0_gdn_norm, v_l0_mla_q_norm=v_l0_mla_q_norm, v_l0_mla_w_uq=v_l0_mla_w_uq, v_l0_mla_kv_norm=v_l0_mla_kv_norm, v_l0_mla_w_ukv=v_l0_mla_w_ukv, v_l0_w_out=v_l0_w_out, v_l0_ffn_norm=v_l0_ffn_norm, v_l0_ffn_w_up=v_l0_ffn_w_up, v_l0_ffn_conv_w=v_l0_ffn_conv_w, v_l0_ffn_conv_b=v_l0_ffn_conv_b, v_l0_ffn_w_down=v_l0_ffn_w_down, v_l0_ple_proj=v_l0_ple_proj, v_l0_ple_gate_norm=v_l0_ple_gate_norm, v_l0_ple_gate=v_l0_ple_gate, v_l1_attn_norm=v_l1_attn_norm, v_l1_w_in=v_l1_w_in, v_l1_ret_norm=v_l1_ret_norm, v_l1_w_out=v_l1_w_out, v_l1_ffn_norm=v_l1_ffn_norm, v_l1_ffn_w_up=v_l1_ffn_w_up, v_l1_ffn_conv_w=v_l1_ffn_conv_w, v_l1_ffn_conv_b=v_l1_ffn_conv_b, v_l1_ffn_w_down=v_l1_ffn_w_down, v_l1_ple_proj=v_l1_ple_proj, v_l1_ple_gate_norm=v_l1_ple_gate_norm, v_l1_ple_gate=v_l1_ple_gate, v_final_norm=v_final_norm)
    weights = {n: given[n] for n in TWIN_WEIGHTS}
    shared = {n: given[n] for n in SHARED_INPUTS}
    per_example = {n: given[n] for n in ['x', 'p', 'positions']}
    grad_fn = _jax.value_and_grad(_loss, argnums=(0, 1))

    def one_microbatch(ex, loss_target):
        ex = dict(ex)
        diff = ex.pop(TWIN_DIFF_INPUT)
        return grad_fn(weights, diff, {**shared, **ex}, loss_target)

    if N_MICROBATCH == 1:
        loss, (grad_w, grad_x) = one_microbatch(per_example, given["loss_target"])
    else:
        def body(carry, xs):
            loss_sum, grad_sum = carry
            l_k, (gw_k, gx_k) = one_microbatch(xs[0], xs[1])
            with _jax.named_scope("update"):
                return (loss_sum + l_k, _jax.tree.map(_jnp.add, grad_sum, gw_k)), gx_k

        init = (_jnp.zeros((), _jnp.float32), _jax.tree.map(_jnp.zeros_like, weights))
        (loss, grad_w), grad_x = _jax.lax.scan(body, init, (per_example, given["loss_target"]))
    with _jax.named_scope("update"):
        delta_w, new_m, new_v = {}, {}, {}
        for n in TWIN_WEIGHTS:
            delta_w[n], new_m[n], new_v[n] = _adamw(weights[n], grad_w[n], given["m_" + n], given["v_" + n])
    return (loss, grad_x, *[grad_w[n] for n in TWIN_WEIGHTS], *[delta_w[n] for n in TWIN_WEIGHTS],
            *[new_m[n] for n in TWIN_WEIGHTS], *[new_v[n] for n in TWIN_WEIGHTS])
```

```python
import functools
import math

import numpy as np
import jax
import jax.numpy as jnp
from jax import lax
from jax.experimental import pallas as pl
from jax.experimental.pallas import tpu as pltpu

F32, BF16 = jnp.float32, jnp.bfloat16
HI = lax.Precision.HIGHEST
MESH = pl.DeviceIdType.MESH

NORM_EPS = 1e-6
ROPE_THETA = 10000.0
D_MODEL = 2048
PLE_DIM = 256
GDN_HEADS, GDN_DK, GDN_DV, GDN_CONV = 8, 128, 128, 4
MLA_HEADS, MLA_Q_RANK, MLA_KV_RANK, MLA_NOPE, MLA_ROPE, MLA_V = 8, 512, 512, 128, 64, 128
RET_HEADS, RET_DK, RET_DV = 8, 256, 512
D_FF, FFN_CONV = 5632, 3
ADAM_LR, ADAM_B1, ADAM_B2, ADAM_EPS, ADAM_WD, ADAM_STEP = 0.001, 0.9, 0.999, 1e-08, 0.01, 10

LANES = 128
SUBLANES = 8
CHUNK = 128
N_CHIPS = 4
VMEM_LIMIT_MB = 56

ZIN_QKV, ZIN_Z, ZIN_CQ, ZIN_CKV, ZIN_KR, ZIN_AB, ZIN_W = 0, 3072, 4096, 4608, 5120, 5248, 5376


def _pcall(body, *, name, out_shape, grid=(), in_specs=None, out_specs=None, scratch_shapes=(), dims=None,
           aliases=None):
    params = dict(vmem_limit_bytes=VMEM_LIMIT_MB << 20)
    if dims is not None:
        params["dimension_semantics"] = dims
    kw = {}
    if aliases:
        kw["input_output_aliases"] = aliases
    return pl.pallas_call(body, name=name, out_shape=out_shape, grid=grid, in_specs=in_specs, out_specs=out_specs,
                          scratch_shapes=list(scratch_shapes), compiler_params=pltpu.CompilerParams(**params), **kw)


def _tile(n, target, mult=LANES):
    best = None
    for t in range(mult, min(n, target) + 1, mult):
        if n % t == 0:
            best = t
    return best or n


_DN = {"nn": (((1,), (0,)), ((), ())), "nt": (((1,), (1,)), ((), ())), "tn": (((0,), (0,)), ((), ()))}


def matmul(a, b, mode, out_dtype, *, name, add=None, tm=1024, tn=1024, tk=512):
    if mode == "nn":
        (M, K), (K2, N) = a.shape, b.shape
    elif mode == "nt":
        (M, K), (N, K2) = a.shape, b.shape
    else:
        (K, M), (K2, N) = a.shape, b.shape
    assert K == K2, (name, a.shape, b.shape)
    tm, tn, tk = _tile(M, tm), _tile(N, tn), _tile(K, tk)
    nk = K // tk
    dn = _DN[mode]
    has_add = add is not None

    def body(*refs):
        if has_add:
            a_ref, b_ref, add_ref, o_ref, acc = refs
        else:
            a_ref, b_ref, o_ref, acc = refs
        k = pl.program_id(2)

        @pl.when(k == 0)
        def _():
            acc[...] = jnp.zeros_like(acc)

        acc[...] += lax.dot_general(a_ref[...].astype(BF16), b_ref[...].astype(BF16), dn,
                                    preferred_element_type=F32)

        @pl.when(k == nk - 1)
        def _():
            r = acc[...]
            if has_add:
                r = r + add_ref[...]
            o_ref[...] = r.astype(out_dtype)

    if mode == "tn":
        a_spec = pl.BlockSpec((tk, tm), lambda i, j, k: (k, i))
    else:
        a_spec = pl.BlockSpec((tm, tk), lambda i, j, k: (i, k))
    if mode == "nt":
        b_spec = pl.BlockSpec((tn, tk), lambda i, j, k: (j, k))
    else:
        b_spec = pl.BlockSpec((tk, tn), lambda i, j, k: (k, j))
    in_specs = [a_spec, b_spec]
    args = [a, b]
    if has_add:
        in_specs.append(pl.BlockSpec((tm, tn), lambda i, j, k: (i, j)))
        args.append(add)
    return _pcall(body, name=name, out_shape=jax.ShapeDtypeStruct((M, N), out_dtype),
                  grid=(M // tm, N // tn, nk), in_specs=in_specs,
                  out_specs=pl.BlockSpec((tm, tn), lambda i, j, k: (i, j)),
                  scratch_shapes=[pltpu.VMEM((tm, tn), F32)],
                  dims=("parallel", "parallel", "arbitrary"))(*args)


def _row_spec(tr, w, c):
    return pl.BlockSpec((tr, w), lambda i: (i, c))


def _full_spec(arr):
    return pl.BlockSpec(arr.shape, lambda i: (0,) * arr.ndim)


def rowwise(fn, rows, params, nd_rows, outs, *, name, tr=256):
    S = rows[0][0].shape[0]
    tr = min(tr, S)
    n_in = len(rows) + len(params) + len(nd_rows)

    def body(*refs):
        res = fn(*[x[...] for x in refs[:n_in]])
        for o_ref, v in zip(refs[n_in:], res):
            o_ref[...] = v.astype(o_ref.dtype)

    return _pcall(body, name=name, grid=(S // tr,),
                  in_specs=([_row_spec(tr, w, c) for (_, w, c) in rows] + [_full_spec(q) for q in params]
                            + [_row_spec(tr, w, c) for (_, w, c) in nd_rows]),
                  out_specs=[_row_spec(tr, w, 0) for (w, _) in outs],
                  out_shape=[jax.ShapeDtypeStruct((S, w), dt) for (w, dt) in outs],
                  dims=("parallel",))(*[r[0] for r in rows], *params, *[r[0] for r in nd_rows])


def rowwise_bwd(fn, rows, params, nd_rows, cts, d_dtypes, *, name, adds=None, tr=256):
    S = rows[0][0].shape[0]
    tr = min(tr, S)
    n_r, n_p, n_n, n_c = len(rows), len(params), len(nd_rows), len(cts)
    adds = adds or [None] * n_r
    add_list = [a for a in adds if a is not None]
    n_a = len(add_list)

    def body(*refs):
        it = iter(refs)
        r = [next(it)[...] for _ in range(n_r)]
        p = [next(it)[...] for _ in range(n_p)]
        nd = [next(it)[...] for _ in range(n_n)]
        c = [next(it)[...] for _ in range(n_c)]
        ad = [next(it)[...] for _ in range(n_a)]
        d_row_refs = [next(it) for _ in range(n_r)]
        d_par_refs = [next(it) for _ in range(n_p)]
        outs, vjp = jax.vjp(lambda *dp: fn(*dp, *nd), *r, *p)
        g = vjp(tuple(ci.astype(o.dtype) for ci, o in zip(c, outs)))
        ai = 0
        for k in range(n_r):
            gk = g[k].astype(F32)
            if adds[k] is not None:
                gk = gk + ad[ai].astype(F32)
                ai += 1
            d_row_refs[k][...] = gk.astype(d_row_refs[k].dtype)

        @pl.when(pl.program_id(0) == 0)
        def _():
            for ref in d_par_refs:
                ref[...] = jnp.zeros_like(ref)

        for k in range(n_p):
            d_par_refs[k][...] += g[n_r + k].astype(F32)

    in_specs = ([_row_spec(tr, w, c) for (_, w, c) in rows] + [_full_spec(q) for q in params]
                + [_row_spec(tr, w, c) for (_, w, c) in nd_rows] + [_row_spec(tr, w, c) for (_, w, c) in cts]
                + [_row_spec(tr, w, c) for (_, w, c) in add_list])
    out_specs = [_row_spec(tr, w, 0) for (_, w, _) in rows] + [_full_spec(q) for q in params]
    out_shape = ([jax.ShapeDtypeStruct((S, w), dt) for (_, w, _), dt in zip(rows, d_dtypes)]
                 + [jax.ShapeDtypeStruct(q.shape, F32) for q in params])
    res = _pcall(body, name=name, grid=(S // tr,), in_specs=in_specs, out_specs=out_specs, out_shape=out_shape,
                 dims=("arbitrary",))(*[r[0] for r in rows], *params, *[r[0] for r in nd_rows],
                                      *[r[0] for r in cts], *[r[0] for r in add_list])
    return res[:n_r], res[n_r:]


def _rms(x, g):
    x = x.astype(F32)
    return x * lax.rsqrt(jnp.mean(x * x, axis=-1, keepdims=True) + NORM_EPS) * g


def _fn_rms(x, g):
    return (_rms(x, g),)


def _sigmoid(x):
    return 1.0 / (1.0 + jnp.exp(-x))


def _silu(x):
    return x * _sigmoid(x)


def _softplus(x):
    return jnp.maximum(x, 0.0) + jnp.log(1.0 + jnp.exp(-jnp.abs(x)))


def _fn_ple(h, pp, gl):
    return (h.astype(F32) + pp.astype(F32) * _sigmoid(gl.astype(F32)),)


def _fn_ple_terms(pp, gl):
    return (pp.astype(F32) * _sigmoid(gl.astype(F32)),)


def _rot_half_matrix():
    half = MLA_ROPE // 2
    r = lax.broadcasted_iota(jnp.int32, (LANES, LANES), 0)
    c = lax.broadcasted_iota(jnp.int32, (LANES, LANES), 1)
    plus = (c == r + half) & (r < half)
    minus = (r == c + half) & (c < half)
    return jnp.where(plus, 1.0, 0.0) - jnp.where(minus, 1.0, 0.0)


def _rope_pad(x, cosp, sinp):
    return x * cosp + jnp.dot(x, _rot_half_matrix(), precision=HI, preferred_element_type=F32) * sinp


def _fn_mla_pre(cq, ckv, kr, qn_w, kvn_w, cosp, sinp):
    return (_rms(cq, qn_w), _rms(ckv, kvn_w), _rope_pad(kr.astype(F32), cosp, sinp))


def _fn_rope_q(q, cosp, sinp):
    q = q.astype(F32)
    parts = []
    for h in range(MLA_HEADS):
        base = 2 * LANES * h
        parts.append(q[:, base:base + LANES])
        parts.append(_rope_pad(q[:, base + LANES:base + 2 * LANES], cosp, sinp))
    return (jnp.concatenate(parts, axis=1),)


def _fn_ret_gate(g, on, w):
    return (_silu(g.astype(F32)) * (on.astype(F32) * w),)


def _shift_down(cur, halo, s):
    if s == 0:
        return cur
    r = pltpu.roll(cur, s, 0)
    hs = pltpu.roll(halo, s, 0)
    row = lax.broadcasted_iota(jnp.int32, hs.shape, 0)
    first = jnp.where(row < s, hs, r[:SUBLANES])
    return jnp.concatenate([first, r[SUBLANES:]], axis=0)


def _shift_up(cur, halo, s):
    if s == 0:
        return cur
    n = cur.shape[0]
    r = pltpu.roll(cur, n - s, 0)
    hs = pltpu.roll(halo, SUBLANES - s, 0)
    row = lax.broadcasted_iota(jnp.int32, hs.shape, 0)
    last = jnp.where(row >= SUBLANES - s, hs, r[n - SUBLANES:])
    return jnp.concatenate([r[:n - SUBLANES], last], axis=0)


def _prev_halo_spec(tr, tw, col):
    return pl.BlockSpec((SUBLANES, tw), lambda c, i: (jnp.maximum(i * (tr // SUBLANES) - 1, 0), col(c)))


def _conv_taps(cur, halo, w_ref, width):
    taps = [_shift_down(cur, halo, width - 1 - j) for j in range(width)]
    y = taps[0] * w_ref[0:1, :]
    for j in range(1, width):
        y = y + taps[j] * w_ref[j:j + 1, :]
    return y, taps


def gdn_conv_fwd(zin, w, *, tr=512, tw=512):
    S = zin.shape[0]
    tr = min(tr, S)
    width, C = w.shape

    def body(cur_ref, halo_ref, w_ref, o_ref):
        i = pl.program_id(1)
        halo = halo_ref[...] * (i > 0).astype(F32)
        y, _ = _conv_taps(cur_ref[...], halo, w_ref, width)
        o_ref[...] = _silu(y)

    return _pcall(body, name="gdn_conv_fwd", grid=(C // tw, S // tr),
                  in_specs=[pl.BlockSpec((tr, tw), lambda c, i: (i, c)), _prev_halo_spec(tr, tw, lambda c: c),
                            pl.BlockSpec((width, tw), lambda c, i: (0, c))],
                  out_specs=pl.BlockSpec((tr, tw), lambda c, i: (i, c)),
                  out_shape=jax.ShapeDtypeStruct((S, C), F32), dims=("parallel", "arbitrary"))(zin, zin, w)


def gdn_conv_bwd(zin, w, dy, *, tr=512, tw=512):
    S = zin.shape[0]
    tr = min(tr, S)
    width, C = w.shape

    def body(cur_ref, halo_ref, w_ref, dy_ref, da_ref, dw_ref):
        i = pl.program_id(1)
        halo = halo_ref[...] * (i > 0).astype(F32)
        y, taps = _conv_taps(cur_ref[...], halo, w_ref, width)
        sg = _sigmoid(y)
        da = dy_ref[...] * (sg * (1.0 + y * (1.0 - sg)))
        da_ref[...] = da

        @pl.when(i == 0)
        def _():
            dw_ref[...] = jnp.zeros_like(dw_ref)

        for j in range(width):
            dw_ref[j:j + 1, :] += jnp.sum(da * taps[j], axis=0, keepdims=True)

    return _pcall(body, name="gdn_conv_bwd", grid=(C // tw, S // tr),
                  in_specs=[pl.BlockSpec((tr, tw), lambda c, i: (i, c)), _prev_halo_spec(tr, tw, lambda c: c),
                            pl.BlockSpec((width, tw), lambda c, i: (0, c)),
                            pl.BlockSpec((tr, tw), lambda c, i: (i, c))],
                  out_specs=[pl.BlockSpec((tr, tw), lambda c, i: (i, c)),
                             pl.BlockSpec((width, tw), lambda c, i: (0, c))],
                  out_shape=[jax.ShapeDtypeStruct((S, C), F32), jax.ShapeDtypeStruct((width, C), F32)],
                  dims=("parallel", "arbitrary"))(zin, zin, w, dy)


def conv_transpose(dy, w, out_dtype, *, name, tr=512, tw=512):
    S, C = dy.shape
    tr = min(tr, S)
    width = w.shape[0]
    n_i = S // tr

    def body(cur_ref, halo_ref, w_ref, o_ref):
        i = pl.program_id(1)
        halo = halo_ref[...] * (i < n_i - 1).astype(F32)
        cur = cur_ref[...]
        acc = cur * w_ref[width - 1:width, :]
        for s in range(1, width):
            acc = acc + _shift_up(cur, halo, s) * w_ref[width - 1 - s:width - s, :]
        o_ref[...] = acc.astype(out_dtype)

    nxt = pl.BlockSpec((SUBLANES, tw), lambda c, i: (jnp.minimum((i + 1) * (tr // SUBLANES), S // SUBLANES - 1), c))
    return _pcall(body, name=name, grid=(C // tw, n_i),
                  in_specs=[pl.BlockSpec((tr, tw), lambda c, i: (i, c)), nxt,
                            pl.BlockSpec((width, tw), lambda c, i: (0, c))],
                  out_specs=pl.BlockSpec((tr, tw), lambda c, i: (i, c)),
                  out_shape=jax.ShapeDtypeStruct((S, C), out_dtype), dims=("parallel", "arbitrary"))(dy, dy, w)


def ffn_conv_fwd(u, w, b, *, tr=512, tw=512):
    S, C2 = u.shape
    tr = min(tr, S)
    width = w.shape[0]
    half = C2 // 2
    nc = half // tw

    def body(g_ref, gh_ref, u_ref, uh_ref, wg_ref, wu_ref, bg_ref, bu_ref, o_ref):
        i = pl.program_id(1)
        live = (i > 0).astype(F32)
        yg, _ = _conv_taps(g_ref[...], gh_ref[...] * live, wg_ref, width)
        yu, _ = _conv_taps(u_ref[...], uh_ref[...] * live, wu_ref, width)
        o_ref[...] = (_silu(yg + bg_ref[...]) * (yu + bu_ref[...])).astype(o_ref.dtype)

    return _pcall(body, name="ffn_conv_fwd", grid=(nc, S // tr),
                  in_specs=[pl.BlockSpec((tr, tw), lambda c, i: (i, c)), _prev_halo_spec(tr, tw, lambda c: c),
                            pl.BlockSpec((tr, tw), lambda c, i: (i, c + nc)),
                            _prev_halo_spec(tr, tw, lambda c: c + nc),
                            pl.BlockSpec((width, tw), lambda c, i: (0, c)),
                            pl.BlockSpec((width, tw), lambda c, i: (0, c + nc)),
                            pl.BlockSpec((1, tw), lambda c, i: (0, c)), pl.BlockSpec((1, tw), lambda c, i: (0, c + nc))],
                  out_specs=pl.BlockSpec((tr, tw), lambda c, i: (i, c)),
                  out_shape=jax.ShapeDtypeStruct((S, half), BF16),
                  dims=("parallel", "arbitrary"))(u, u, u, u, w, w, b, b)


def ffn_conv_bwd(u, w, b, df, *, tr=512, tw=512):
    S, C2 = u.shape
    tr = min(tr, S)
    width = w.shape[0]
    half = C2 // 2
    nc = half // tw

    def body(g_ref, gh_ref, u_ref, uh_ref, wg_ref, wu_ref, bg_ref, bu_ref, df_ref, dc_ref, dw_ref, db_ref):
        t, i = pl.program_id(0), pl.program_id(2)
        live = (i > 0).astype(F32)
        yg, gt = _conv_taps(g_ref[...], gh_ref[...] * live, wg_ref, width)
        yu, ut = _conv_taps(u_ref[...], uh_ref[...] * live, wu_ref, width)
        yg = yg + bg_ref[...]
        yu = yu + bu_ref[...]
        sg = _sigmoid(yg)
        dfv = df_ref[...].astype(F32)
        is_gate = t == 0
        dc = jnp.where(is_gate, dfv * yu * (sg * (1.0 + yg * (1.0 - sg))), dfv * (yg * sg))
        dc_ref[...] = dc

        @pl.when(i == 0)
        def _():
            dw_ref[...] = jnp.zeros_like(dw_ref)
            db_ref[...] = jnp.zeros_like(db_ref)

        db_ref[...] += jnp.sum(dc, axis=0, keepdims=True)
        for j in range(width):
            tap = jnp.where(is_gate, gt[j], ut[j])
            dw_ref[j:j + 1, :] += jnp.sum(dc * tap, axis=0, keepdims=True)

    def prev(col):
        return pl.BlockSpec((SUBLANES, tw), lambda t, c, i: (jnp.maximum(i * (tr // SUBLANES) - 1, 0), col(c)))

    return _pcall(body, name="ffn_conv_bwd", grid=(2, nc, S // tr),
                  in_specs=[pl.BlockSpec((tr, tw), lambda t, c, i: (i, c)), prev(lambda c: c),
                            pl.BlockSpec((tr, tw), lambda t, c, i: (i, c + nc)), prev(lambda c: c + nc),
                            pl.BlockSpec((width, tw), lambda t, c, i: (0, c)),
                            pl.BlockSpec((width, tw), lambda t, c, i: (0, c + nc)),
                            pl.BlockSpec((1, tw), lambda t, c, i: (0, c)),
                            pl.BlockSpec((1, tw), lambda t, c, i: (0, c + nc)),
                            pl.BlockSpec((tr, tw), lambda t, c, i: (i, c))],
                  out_specs=[pl.BlockSpec((tr, tw), lambda t, c, i: (i, t * nc + c)),
                             pl.BlockSpec((width, tw), lambda t, c, i: (0, t * nc + c)),
                             pl.BlockSpec((1, tw), lambda t, c, i: (0, t * nc + c))],
                  out_shape=[jax.ShapeDtypeStruct((S, C2), F32), jax.ShapeDtypeStruct((width, C2), F32),
                             jax.ShapeDtypeStruct((1, C2), F32)],
                  dims=("parallel", "parallel", "arbitrary"))(u, u, u, u, w, w, b, b, df)


def _bdot(a, b, dn=_DN["nn"]):
    return lax.dot_general(a.astype(BF16), b.astype(BF16), dn, preferred_element_type=F32)


def _hdot(a, b, dn=_DN["nn"]):
    return lax.dot_general(a, b, dn, precision=HI, preferred_element_type=F32)


def _tri_masks(n):
    r = lax.broadcasted_iota(jnp.int32, (n, n), 0)
    c = lax.broadcasted_iota(jnp.int32, (n, n), 1)
    return r >= c, r > c


def _gdn_chunk(q, k, v, z, ab, a_row, dt_row, norm_w, state, sel_a, sel_b):
    C = q.shape[0]
    incl, strict = _tri_masks(C)
    lower = jnp.where(incl, 1.0, 0.0)
    qn = q * lax.rsqrt(jnp.sum(q * q, axis=-1, keepdims=True) + NORM_EPS) * (GDN_DK ** -0.5)
    kn = k * lax.rsqrt(jnp.sum(k * k, axis=-1, keepdims=True) + NORM_EPS)
    g = jnp.sum(-jnp.exp(a_row) * _softplus(ab + dt_row) * sel_a, axis=-1, keepdims=True)
    beta = jnp.sum(_sigmoid(ab) * sel_b, axis=-1, keepdims=True)
    gb = jnp.broadcast_to(g, (C, C))
    g_col = _hdot(lower, gb)
    g_row = _hdot(gb, jnp.where(strict, 0.0, 1.0), _DN["tn"])
    g_last = jnp.sum(gb, axis=0, keepdims=True)
    gamma = jnp.where(incl, jnp.exp(jnp.where(incl, g_col - g_row, 0.0)), 0.0)
    e_col = jnp.exp(g_col)
    kb = kn * beta
    a_mat = jnp.where(strict, _bdot(kb, kn, _DN["nt"]) * gamma, 0.0)
    x = jnp.concatenate([v * beta, kb * e_col], axis=1)
    pw = -a_mat
    steps = int(math.log2(C))
    for it in range(steps):
        x = x + _hdot(pw, x)
        if it < steps - 1:
            pw = _hdot(pw, pw)
    u, w = x[:, :GDN_DV], x[:, GDN_DV:]
    attn = _bdot(qn, kn, _DN["nt"]) * gamma
    q_dec = qn * e_col
    k_dec = kn * jnp.exp(g_last - g_col)
    v_new = u - _bdot(w, state)
    o = _bdot(q_dec, state) + _bdot(attn, v_new)
    state_new = state * jnp.exp(jnp.broadcast_to(g_last, state.shape)) + _bdot(k_dec, v_new, _DN["tn"])
    y = _rms(o, norm_w) * _silu(z)
    return y, state_new


def _head_selectors(h):
    lane = lax.broadcasted_iota(jnp.int32, (1, LANES), 1)
    return jnp.where(lane == h, 1.0, 0.0), jnp.where(lane == h + GDN_HEADS, 1.0, 0.0)


def _gdn_in_specs(rev, nc):
    def n_(n):
        return nc - 1 - n if rev else n
    H = GDN_HEADS
    blk = lambda off: pl.BlockSpec((CHUNK, LANES), lambda n, h: (n_(n), off + h))
    row = pl.BlockSpec((1, LANES), lambda n, h: (0, 0))
    return n_, [blk(0), blk(H), blk(2 * H), blk(ZIN_Z // LANES),
                pl.BlockSpec((CHUNK, LANES), lambda n, h: (n_(n), ZIN_AB // LANES)), row, row, row]


def gdn_fwd(qkv, zin, a_row, dt_row, norm_w):
    S = qkv.shape[0]
    nc = S // CHUNK
    H = GDN_HEADS
    _, in_specs = _gdn_in_specs(False, nc)

    def body(q_ref, k_ref, v_ref, z_ref, ab_ref, a_ref, dt_ref, nw_ref, y_ref, st_ref, state):
        n, h = pl.program_id(0), pl.program_id(1)

        @pl.when(n == 0)
        def _():
            state[h] = jnp.zeros((GDN_DK, GDN_DV), F32)

        st = state[h]
        st_ref[...] = st
        sel_a, sel_b = _head_selectors(h)
        y, st_new = _gdn_chunk(q_ref[...], k_ref[...], v_ref[...], z_ref[...], ab_ref[...], a_ref[...],
                               dt_ref[...], nw_ref[...], st, sel_a, sel_b)
        y_ref[...] = y.astype(y_ref.dtype)
        state[h] = st_new

    return _pcall(body, name="gdn_fwd", grid=(nc, H), in_specs=in_specs,
                  out_specs=[pl.BlockSpec((CHUNK, LANES), lambda n, h: (n, h)),
                             pl.BlockSpec((None, None, GDN_DK, GDN_DV), lambda n, h: (h, n, 0, 0))],
                  out_shape=[jax.ShapeDtypeStruct((S, H * GDN_DV), BF16),
                             jax.ShapeDtypeStruct((H, nc, GDN_DK, GDN_DV), F32)],
                  scratch_shapes=[pltpu.VMEM((H, GDN_DK, GDN_DV), F32)],
                  dims=("arbitrary", "arbitrary"))(qkv, qkv, qkv, zin, zin, a_row, dt_row, norm_w)


def gdn_bwd(qkv, zin, a_row, dt_row, norm_w, states, dy, dy_col0):
    S = qkv.shape[0]
    nc = S // CHUNK
    H = GDN_HEADS
    n_, in_specs = _gdn_in_specs(True, nc)
    in_specs = in_specs + [pl.BlockSpec((None, None, GDN_DK, GDN_DV), lambda n, h: (h, n_(n), 0, 0)),
                           pl.BlockSpec((CHUNK, LANES), lambda n, h: (n_(n), dy_col0 + h))]

    def body(q_ref, k_ref, v_ref, z_ref, ab_ref, a_ref, dt_ref, nw_ref, st_ref, dy_ref,
             dq_ref, dk_ref, dv_ref, dz_ref, dab_ref, da_ref, ddt_ref, dnw_ref, dstate):
        n, h = pl.program_id(0), pl.program_id(1)

        @pl.when(n == 0)
        def _():
            dstate[h] = jnp.zeros((GDN_DK, GDN_DV), F32)

        @pl.when((n == 0) & (h == 0))
        def _():
            da_ref[...] = jnp.zeros_like(da_ref)
            ddt_ref[...] = jnp.zeros_like(ddt_ref)
            dnw_ref[...] = jnp.zeros_like(dnw_ref)

        sel_a, sel_b = _head_selectors(h)
        _, vjp = jax.vjp(lambda *a: _gdn_chunk(*a, sel_a, sel_b), q_ref[...], k_ref[...], v_ref[...], z_ref[...],
                         ab_ref[...], a_ref[...], dt_ref[...], nw_ref[...], st_ref[...])
        dq, dk, dv, dz, dab, da, ddt, dnw, dst = vjp((dy_ref[...].astype(F32), dstate[h]))
        dq_ref[...] = dq
        dk_ref[...] = dk
        dv_ref[...] = dv
        dz_ref[...] = dz.astype(dz_ref.dtype)

        @pl.when(h == 0)
        def _():
            dab_ref[...] = jnp.zeros_like(dab_ref)

        dab_ref[...] += dab
        da_ref[...] += da
        ddt_ref[...] += ddt
        dnw_ref[...] += dnw
        dstate[h] = dst

    blk = pl.BlockSpec((CHUNK, LANES), lambda n, h: (n_(n), h))
    row = pl.BlockSpec((1, LANES), lambda n, h: (0, 0))
    wide = jax.ShapeDtypeStruct((S, H * LANES), F32)
    return _pcall(body, name="gdn_bwd", grid=(nc, H), in_specs=in_specs,
                  out_specs=[blk, blk, blk, blk, pl.BlockSpec((CHUNK, LANES), lambda n, h: (n_(n), 0)), row, row, row],
                  out_shape=[wide, wide, wide, jax.ShapeDtypeStruct((S, H * LANES), BF16),
                             jax.ShapeDtypeStruct((S, LANES), F32)] + [jax.ShapeDtypeStruct((1, LANES), F32)] * 3,
                  scratch_shapes=[pltpu.VMEM((H, GDN_DK, GDN_DV), F32)],
                  dims=("arbitrary", "arbitrary"))(qkv, qkv, qkv, zin, zin, a_row, dt_row, norm_w, states, dy)


def _rope_full(x, cos, sin):
    x1, x2 = x[:, :RET_DK // 2], x[:, RET_DK // 2:]
    return jnp.concatenate([x1 * cos - x2 * sin, x2 * cos + x1 * sin], axis=1)


def _ret_chunk(q, k, v, cos, sin, lg, state):
    C = q.shape[0]
    incl, _ = _tri_masks(C)
    qr = _rope_full(q, cos, sin)
    kr = _rope_full(k, cos, sin) * (RET_DK ** -0.5)
    r = lax.broadcasted_iota(jnp.int32, (C, C), 0)
    c = lax.broadcasted_iota(jnp.int32, (C, C), 1)
    dist = jnp.where(incl, (r - c).astype(F32), 0.0)
    decay = jnp.where(incl, jnp.exp(dist * lg), 0.0)
    pos = lax.broadcasted_iota(jnp.int32, (C, 1), 0).astype(F32)
    lg1 = lg[:, :1]
    xi = jnp.exp((pos + 1.0) * lg1)
    zeta = jnp.exp((C - 1.0 - pos) * lg1)
    inner = _bdot(_bdot(qr, kr, _DN["nt"]) * decay, v)
    cross = _bdot(qr * xi, state)
    state_new = state * jnp.exp(C * lg1) + _bdot(kr * zeta, v, _DN["tn"])
    o = inner + cross
    mu = jnp.mean(o, axis=-1, keepdims=True)
    var = jnp.mean(jnp.square(o - mu), axis=-1, keepdims=True)
    return (o - mu) * lax.rsqrt(var + NORM_EPS), state_new


def _ret_log_gamma():
    lg = np.log1p(-np.power(2.0, -5.0 - np.arange(RET_HEADS, dtype=np.float64))).astype(np.float32)
    return jnp.asarray(np.broadcast_to(lg[:, None, None], (RET_HEADS, 1, LANES)).copy())


def _ret_in_specs(rev, nc):
    def n_(n):
        return nc - 1 - n if rev else n
    H = RET_HEADS
    return n_, [pl.BlockSpec((CHUNK, RET_DK), lambda n, h: (n_(n), h)),
                pl.BlockSpec((CHUNK, RET_DK), lambda n, h: (n_(n), H + h)),
                pl.BlockSpec((CHUNK, RET_DV), lambda n, h: (n_(n), 2 * H * RET_DK // RET_DV + h)),
                pl.BlockSpec((CHUNK, LANES), lambda n, h: (n_(n), 0)),
                pl.BlockSpec((CHUNK, LANES), lambda n, h: (n_(n), 0)),
                pl.BlockSpec((None, 1, LANES), lambda n, h: (h, 0, 0))]


def ret_fwd(zz, cos, sin):
    S = zz.shape[0]
    nc = S // CHUNK
    H = RET_HEADS
    _, in_specs = _ret_in_specs(False, nc)

    def body(q_ref, k_ref, v_ref, cos_ref, sin_ref, lg_ref, o_ref, st_ref, state):
        n, h = pl.program_id(0), pl.program_id(1)

        @pl.when(n == 0)
        def _():
            state[h] = jnp.zeros((RET_DK, RET_DV), F32)

        st = state[h]
        st_ref[...] = st
        o, st_new = _ret_chunk(q_ref[...], k_ref[...], v_ref[...], cos_ref[...], sin_ref[...], lg_ref[...], st)
        o_ref[...] = o
        state[h] = st_new

    return _pcall(body, name="ret_fwd", grid=(nc, H), in_specs=in_specs,
                  out_specs=[pl.BlockSpec((CHUNK, RET_DV), lambda n, h: (n, h)),
                             pl.BlockSpec((None, None, RET_DK, RET_DV), lambda n, h: (h, n, 0, 0))],
                  out_shape=[jax.ShapeDtypeStruct((S, H * RET_DV), F32),
                             jax.ShapeDtypeStruct((H, nc, RET_DK, RET_DV), F32)],
                  scratch_shapes=[pltpu.VMEM((H, RET_DK, RET_DV), F32)],
                  dims=("arbitrary", "arbitrary"))(zz, zz, zz, cos, sin, _ret_log_gamma())


def ret_bwd(zz, cos, sin, states, do):
    S = zz.shape[0]
    nc = S // CHUNK
    H = RET_HEADS
    n_, in_specs = _ret_in_specs(True, nc)
    in_specs = in_specs + [pl.BlockSpec((None, None, RET_DK, RET_DV), lambda n, h: (h, n_(n), 0, 0)),
                           pl.BlockSpec((CHUNK, RET_DV), lambda n, h: (n_(n), h))]

    def body(q_ref, k_ref, v_ref, cos_ref, sin_ref, lg_ref, st_ref, do_ref, dq_ref, dk_ref, dv_ref, dstate):
        n, h = pl.program_id(0), pl.program_id(1)

        @pl.when(n == 0)
        def _():
            dstate[h] = jnp.zeros((RET_DK, RET_DV), F32)

        cos, sin, lg = cos_ref[...], sin_ref[...], lg_ref[...]
        _, vjp = jax.vjp(lambda q, k, v, st: _ret_chunk(q, k, v, cos, sin, lg, st),
                         q_ref[...], k_ref[...], v_ref[...], st_ref[...])
        dq, dk, dv, dst = vjp((do_ref[...], dstate[h]))
        dq_ref[...] = dq.astype(dq_ref.dtype)
        dk_ref[...] = dk.astype(dk_ref.dtype)
        dv_ref[...] = dv.astype(dv_ref.dtype)
        dstate[h] = dst

    return _pcall(body, name="ret_bwd", grid=(nc, H), in_specs=in_specs,
                  out_specs=[pl.BlockSpec((CHUNK, RET_DK), lambda n, h: (n_(n), h)),
                             pl.BlockSpec((CHUNK, RET_DK), lambda n, h: (n_(n), h)),
                             pl.BlockSpec((CHUNK, RET_DV), lambda n, h: (n_(n), h))],
                  out_shape=[jax.ShapeDtypeStruct((S, H * RET_DK), BF16), jax.ShapeDtypeStruct((S, H * RET_DK), BF16),
                             jax.ShapeDtypeStruct((S, H * RET_DV), BF16)],
                  scratch_shapes=[pltpu.VMEM((H, RET_DK, RET_DV), F32)],
                  dims=("arbitrary", "arbitrary"))(zz, zz, zz, cos, sin, _ret_log_gamma(), states, do)


MLA_SCALE = (MLA_NOPE + MLA_ROPE) ** -0.5
NEG = -1e30


def _mla_scores(q, kn, kpe, i, j, t):
    s = (lax.dot_general(q[:, :LANES], kn, _DN["nt"], preferred_element_type=F32)
         + lax.dot_general(q[:, LANES:], kpe, _DN["nt"], preferred_element_type=F32)) * MLA_SCALE
    row = i * t + lax.broadcasted_iota(jnp.int32, s.shape, 0)
    col = j * t + lax.broadcasted_iota(jnp.int32, s.shape, 1)
    return s, col <= row


def flash_fwd(qr, kv, kpe, *, t=512):
    S = qr.shape[0]
    t = min(t, S)
    nb = S // t
    H = MLA_HEADS

    def body(q_ref, kn_ref, v_ref, kpe_ref, o_ref, lse_ref, m_s, l_s, acc):
        i, j = pl.program_id(1), pl.program_id(2)

        @pl.when(j == 0)
        def _():
            m_s[...] = jnp.full_like(m_s, NEG)
            l_s[...] = jnp.zeros_like(l_s)
            acc[...] = jnp.zeros_like(acc)

        @pl.when(j <= i)
        def _():
            s, ok = _mla_scores(q_ref[...], kn_ref[...], kpe_ref[...], i, j, t)
            s = jnp.where(ok, s, NEG)
            m_new = jnp.maximum(m_s[...], jnp.max(s, axis=-1, keepdims=True))
            p = jnp.where(ok, jnp.exp(s - m_new), 0.0)
            alpha = jnp.exp(m_s[...] - m_new)
            l_s[...] = alpha * l_s[...] + jnp.sum(p, axis=-1, keepdims=True)
            acc[...] = alpha * acc[...] + _bdot(p, v_ref[...])
            m_s[...] = m_new

        @pl.when(j == nb - 1)
        def _():
            o_ref[...] = (acc[...] / l_s[...]).astype(o_ref.dtype)
            lse_ref[...] = m_s[...] + jnp.log(l_s[...])

    kmap = lambda off: (lambda h, i, j: (jnp.minimum(j, i), off + h))
    return _pcall(body, name="mla_flash_fwd", grid=(H, nb, nb),
                  in_specs=[pl.BlockSpec((t, 2 * LANES), lambda h, i, j: (i, h)),
                            pl.BlockSpec((t, LANES), kmap(0)), pl.BlockSpec((t, LANES), kmap(H)),
                            pl.BlockSpec((t, LANES), lambda h, i, j: (jnp.minimum(j, i), 0))],
                  out_specs=[pl.BlockSpec((t, LANES), lambda h, i, j: (i, h)),
                             pl.BlockSpec((None, t, 1), lambda h, i, j: (h, i, 0))],
                  out_shape=[jax.ShapeDtypeStruct((S, H * MLA_V), BF16), jax.ShapeDtypeStruct((H, S, 1), F32)],
                  scratch_shapes=[pltpu.VMEM((t, 1), F32), pltpu.VMEM((t, 1), F32), pltpu.VMEM((t, MLA_V), F32)],
                  dims=("parallel", "parallel", "arbitrary"))(qr, kv, kv, kpe)


def _mla_p_ds(q, kn, v, kpe, do, o, lse, i, j, t):
    s, ok = _mla_scores(q, kn, kpe, i, j, t)
    p = jnp.where(ok, jnp.exp(jnp.where(ok, s, 0.0) - lse), 0.0)
    dof = do.astype(F32)
    delta = jnp.sum(dof * o.astype(F32), axis=-1, keepdims=True)
    dp = lax.dot_general(do.astype(BF16), v, _DN["nt"], preferred_element_type=F32)
    ds = p * (dp - delta) * MLA_SCALE
    return p, ds


def flash_bwd_dq(qr, kv, kpe, o, lse, dy, dy_col0, *, t=512):
    S = qr.shape[0]
    t = min(t, S)
    nb = S // t
    H = MLA_HEADS

    def body(q_ref, kn_ref, v_ref, kpe_ref, o_ref, lse_ref, do_ref, dq_ref, acc):
        i, j = pl.program_id(1), pl.program_id(2)

        @pl.when(j == 0)
        def _():
            acc[...] = jnp.zeros_like(acc)

        @pl.when(j <= i)
        def _():
            _, ds = _mla_p_ds(q_ref[...], kn_ref[...], v_ref[...], kpe_ref[...], do_ref[...], o_ref[...],
                              lse_ref[...], i, j, t)
            acc[...] += jnp.concatenate([_bdot(ds, kn_ref[...]), _bdot(ds, kpe_ref[...])], axis=1)

        @pl.when(j == nb - 1)
        def _():
            dq_ref[...] = acc[...]

    kmap = lambda off: (lambda h, i, j: (jnp.minimum(j, i), off + h))
    return _pcall(body, name="mla_flash_dq", grid=(H, nb, nb),
                  in_specs=[pl.BlockSpec((t, 2 * LANES), lambda h, i, j: (i, h)),
                            pl.BlockSpec((t, LANES), kmap(0)), pl.BlockSpec((t, LANES), kmap(H)),
                            pl.BlockSpec((t, LANES), lambda h, i, j: (jnp.minimum(j, i), 0)),
                            pl.BlockSpec((t, LANES), lambda h, i, j: (i, h)),
                            pl.BlockSpec((None, t, 1), lambda h, i, j: (h, i, 0)),
                            pl.BlockSpec((t, LANES), lambda h, i, j: (i, dy_col0 + h))],
                  out_specs=pl.BlockSpec((t, 2 * LANES), lambda h, i, j: (i, h)),
                  out_shape=jax.ShapeDtypeStruct((S, H * 2 * LANES), F32),
                  scratch_shapes=[pltpu.VMEM((t, 2 * LANES), F32)],
                  dims=("parallel", "parallel", "arbitrary"))(qr, kv, kv, kpe, o, lse, dy)


def flash_bwd_dkv(qr, kv, kpe, o, lse, dy, dy_col0, *, t=512):
    S = qr.shape[0]
    t = min(t, S)
    nb = S // t
    H = MLA_HEADS

    def body(q_ref, kn_ref, v_ref, kpe_ref, o_ref, lse_ref, do_ref, dkn_ref, dv_ref, dkpe_ref, akn, av):
        j, h, i = pl.program_id(0), pl.program_id(1), pl.program_id(2)

        @pl.when(i == 0)
        def _():
            akn[...] = jnp.zeros_like(akn)
            av[...] = jnp.zeros_like(av)

        @pl.when((i == 0) & (h == 0))
        def _():
            dkpe_ref[...] = jnp.zeros_like(dkpe_ref)

        @pl.when(i >= j)
        def _():
            q = q_ref[...]
            p, ds = _mla_p_ds(q, kn_ref[...], v_ref[...], kpe_ref[...], do_ref[...], o_ref[...], lse_ref[...], i, j, t)
            av[...] += _bdot(p, do_ref[...], _DN["tn"])
            akn[...] += _bdot(ds, q[:, :LANES], _DN["tn"])
            dkpe_ref[...] += _bdot(ds, q[:, LANES:], _DN["tn"])

        @pl.when(i == nb - 1)
        def _():
            dkn_ref[...] = akn[...].astype(dkn_ref.dtype)
            dv_ref[...] = av[...].astype(dv_ref.dtype)

    qmap = lambda off: (lambda j, h, i: (jnp.maximum(i, j), off + h))
    dkn, dv, dkpe = _pcall(
        body, name="mla_flash_dkv", grid=(nb, H, nb),
        in_specs=[pl.BlockSpec((t, 2 * LANES), qmap(0)),
                  pl.BlockSpec((t, LANES), lambda j, h, i: (j, h)), pl.BlockSpec((t, LANES), lambda j, h, i: (j, H + h)),
                  pl.BlockSpec((t, LANES), lambda j, h, i: (j, 0)),
                  pl.BlockSpec((t, LANES), qmap(0)),
                  pl.BlockSpec((None, t, 1), lambda j, h, i: (h, jnp.maximum(i, j), 0)),
                  pl.BlockSpec((t, LANES), qmap(dy_col0))],
        out_specs=[pl.BlockSpec((t, LANES), lambda j, h, i: (j, h)), pl.BlockSpec((t, LANES), lambda j, h, i: (j, h)),
                   pl.BlockSpec((t, LANES), lambda j, h, i: (j, 0))],
        out_shape=[jax.ShapeDtypeStruct((S, H * LANES), BF16), jax.ShapeDtypeStruct((S, H * LANES), BF16),
                   jax.ShapeDtypeStruct((S, LANES), F32)],
        scratch_shapes=[pltpu.VMEM((t, LANES), F32), pltpu.VMEM((t, LANES), F32)],
        dims=("arbitrary", "arbitrary", "arbitrary"))(qr, kv, kv, kpe, o, lse, dy)
    return jnp.concatenate([dkn, dv], axis=1), dkpe


def loss_head(h, target, g, *, tr=256):
    S, D = h.shape
    tr = min(tr, S)

    def body(h_ref, t_ref, g_ref, loss_ref, dh_ref, dg_ref):
        tgt = t_ref[...]

        def f(hh, gg):
            err = jnp.square(_rms(hh, gg) - tgt)
            per_row = jnp.sum(err, axis=-1, keepdims=True) * (0.5 / D)
            return jnp.sum(per_row, axis=0, keepdims=True)

        val, vjp = jax.vjp(f, h_ref[...], g_ref[...])
        dh, dg = vjp(jnp.ones((1, 1), F32))
        dh_ref[...] = dh

        @pl.when(pl.program_id(0) == 0)
        def _():
            loss_ref[...] = jnp.zeros_like(loss_ref)
            dg_ref[...] = jnp.zeros_like(dg_ref)

        loss_ref[...] += jnp.broadcast_to(val, loss_ref.shape)
        dg_ref[...] += dg

    return _pcall(body, name="loss_head", grid=(S // tr,),
                  in_specs=[_row_spec(tr, D, 0), _row_spec(tr, D, 0), _full_spec(g)],
                  out_specs=[pl.BlockSpec((1, LANES), lambda i: (0, 0)), _row_spec(tr, D, 0), _full_spec(g)],
                  out_shape=[jax.ShapeDtypeStruct((1, LANES), F32), jax.ShapeDtypeStruct((S, D), F32),
                             jax.ShapeDtypeStruct(g.shape, F32)],
                  dims=("arbitrary",))(h, target, g)


def _rope_tables(positions, dim):
    inv_freq = ROPE_THETA ** (-jnp.arange(0, dim, 2, dtype=F32) / dim)
    ang = positions.astype(F32)[:, None] * inv_freq
    return jnp.cos(ang), jnp.sin(ang)


def _pad_cols(w, n):
    return jnp.pad(w, ((0, 0), (0, n - w.shape[1])))


def _prep_w_in0(w):
    return jnp.concatenate([w[:, :4096], w[:, 4112:5136], _pad_cols(w[:, 5136:5200], LANES),
                            _pad_cols(w[:, 4096:4112], LANES)], axis=1)


def _unprep_w_in0(g):
    return jnp.concatenate([g[:, :4096], g[:, ZIN_AB:ZIN_AB + 16], g[:, ZIN_CQ:ZIN_KR], g[:, ZIN_KR:ZIN_KR + MLA_ROPE]],
                           axis=1)


def _prep_w_uq(w):
    w = w.reshape(MLA_Q_RANK, MLA_HEADS, MLA_NOPE + MLA_ROPE)
    w = jnp.pad(w, ((0, 0), (0, 0), (0, 2 * LANES - MLA_NOPE - MLA_ROPE)))
    return w.reshape(MLA_Q_RANK, MLA_HEADS * 2 * LANES)


def _unprep_w_uq(g):
    g = g.reshape(MLA_Q_RANK, MLA_HEADS, 2 * LANES)[:, :, :MLA_NOPE + MLA_ROPE]
    return g.reshape(MLA_Q_RANK, MLA_HEADS * (MLA_NOPE + MLA_ROPE))


def _prep_w_ukv(w):
    w = w.reshape(MLA_KV_RANK, MLA_HEADS, 2, LANES)
    return jnp.transpose(w, (0, 2, 1, 3)).reshape(MLA_KV_RANK, 2 * MLA_HEADS * LANES)


def _unprep_w_ukv(g):
    g = g.reshape(MLA_KV_RANK, 2, MLA_HEADS, LANES)
    return jnp.transpose(g, (0, 2, 1, 3)).reshape(MLA_KV_RANK, 2 * MLA_HEADS * LANES)


def _row(v, n=None):
    v = v.reshape(1, -1).astype(F32)
    return v if n is None else _pad_cols(v, n)


def _ffn_fwd(h, norm_g, w_up, conv_w, conv_b, w_down, tag):
    (hn,) = rowwise(_fn_rms, [(h, D_MODEL, 0)], [norm_g], [], [(D_MODEL, BF16)], name=f"{tag}_ffn_norm")
    u = matmul(hn, w_up, "nn", F32, name=f"{tag}_ffn_up")
    f = ffn_conv_fwd(u, conv_w, conv_b)
    h_out = matmul(f, w_down, "nn", F32, add=h, name=f"{tag}_ffn_down")
    return h_out, (hn, u, f)


def _ffn_bwd(dh, h, norm_g, w_up, conv_w, conv_b, w_down, saved, tag):
    hn, u, f = saved
    df = matmul(dh, w_down, "nt", BF16, name=f"{tag}_ffn_down_dx")
    g_down = matmul(f, dh, "tn", BF16, name=f"{tag}_ffn_down_dw")
    dc, g_conv_w, g_conv_b = ffn_conv_bwd(u, conv_w, conv_b, df)
    du = conv_transpose(dc, conv_w, BF16, name=f"{tag}_ffn_conv_dx")
    g_up = matmul(hn, du, "tn", BF16, name=f"{tag}_ffn_up_dw")
    dhn = matmul(du, w_up, "nt", F32, name=f"{tag}_ffn_up_dx")
    (dh_in,), (g_norm,) = rowwise_bwd(_fn_rms, [(h, D_MODEL, 0)], [norm_g], [], [(dhn, D_MODEL, 0)], [F32],
                                      adds=[(dh, D_MODEL, 0)], name=f"{tag}_ffn_norm_bwd")
    return dh_in, dict(ffn_norm=g_norm, ffn_w_up=g_up, ffn_conv_w=g_conv_w, ffn_conv_b=g_conv_b, ffn_w_down=g_down)


def _ple_fwd(h, p_i, w_proj, gate_g, w_gate, tag):
    (hg,) = rowwise(_fn_rms, [(h, D_MODEL, 0)], [gate_g], [], [(D_MODEL, BF16)], name=f"{tag}_ple_norm")
    gl = matmul(hg, w_gate, "nn", F32, name=f"{tag}_ple_gate")
    pp = matmul(p_i, w_proj, "nn", F32, name=f"{tag}_ple_proj")
    (h_out,) = rowwise(_fn_ple, [(h, D_MODEL, 0), (pp, D_MODEL, 0), (gl, D_MODEL, 0)], [], [], [(D_MODEL, F32)],
                       name=f"{tag}_ple_add")
    return h_out, (hg, gl, pp)


def _ple_bwd(dh, h, p_i, w_proj, gate_g, w_gate, saved, tag):
    hg, gl, pp = saved
    (dpp, dgl), _ = rowwise_bwd(_fn_ple_terms, [(pp, D_MODEL, 0), (gl, D_MODEL, 0)], [], [], [(dh, D_MODEL, 0)],
                                [BF16, BF16], name=f"{tag}_ple_add_bwd")
    g_proj = matmul(p_i, dpp, "tn", BF16, name=f"{tag}_ple_proj_dw")
    g_gate = matmul(hg, dgl, "tn", BF16, name=f"{tag}_ple_gate_dw")
    dhg = matmul(dgl, w_gate, "nt", F32, name=f"{tag}_ple_gate_dx")
    (dh_in,), (g_norm,) = rowwise_bwd(_fn_rms, [(h, D_MODEL, 0)], [gate_g], [], [(dhg, D_MODEL, 0)], [F32],
                                      adds=[(dh, D_MODEL, 0)], name=f"{tag}_ple_norm_bwd")
    return dh_in, dict(ple_proj=g_proj, ple_gate_norm=g_norm, ple_gate=g_gate)


def local_step(x, p, positions, target, W):
    S = x.shape[0]
    G = {}
    p0, p1 = p[0].astype(BF16), p[1].astype(BF16)

    cm, sm = _rope_tables(positions, MLA_ROPE)
    zeros = jnp.zeros((S, LANES - MLA_ROPE), F32)
    cosp = jnp.concatenate([cm, cm, zeros], axis=1)
    sinp = jnp.concatenate([sm, sm, zeros], axis=1)
    cr, sr = _rope_tables(positions, RET_DK)

    w_in0 = _prep_w_in0(W["l0_w_in"])
    w_uq = _prep_w_uq(W["l0_mla_w_uq"])
    w_ukv = _prep_w_ukv(W["l0_mla_w_ukv"])
    a_row = _row(W["l0_gdn_A_log"], LANES)
    dt_row = _row(W["l0_gdn_dt_bias"], LANES)
    gdn_nw = _row(W["l0_gdn_norm"])
    n = {k: _row(W[k]) for k in ("l0_attn_norm", "l0_mla_q_norm", "l0_mla_kv_norm", "l0_ffn_norm",
                                 "l0_ple_gate_norm", "l1_attn_norm", "l1_ret_norm", "l1_ffn_norm",
                                 "l1_ple_gate_norm", "final_norm", "l0_ffn_conv_b", "l1_ffn_conv_b")}

    (hn0,) = rowwise(_fn_rms, [(x, D_MODEL, 0)], [n["l0_attn_norm"]], [], [(D_MODEL, BF16)], name="l0_attn_norm")
    zin = matmul(hn0, w_in0, "nn", F32, name="l0_w_in", tn=768)
    qkv = gdn_conv_fwd(zin, W["l0_gdn_conv"])
    y_a, gdn_states = gdn_fwd(qkv, zin, a_row, dt_row, gdn_nw)
    mla_rows = [(zin, MLA_Q_RANK, ZIN_CQ // MLA_Q_RANK), (zin, MLA_KV_RANK, ZIN_CKV // MLA_KV_RANK),
                (zin, LANES, ZIN_KR // LANES)]
    mla_nd = [(cosp, LANES, 0), (sinp, LANES, 0)]
    cqn, ckvn, kpe = rowwise(_fn_mla_pre, mla_rows, [n["l0_mla_q_norm"], n["l0_mla_kv_norm"]], mla_nd,
                             [(MLA_Q_RANK, BF16), (MLA_KV_RANK, BF16), (LANES, BF16)], name="mla_pre")
    q_lin = matmul(cqn, w_uq, "nn", F32, name="mla_w_uq")
    kv = matmul(ckvn, w_ukv, "nn", BF16, name="mla_w_ukv")
    (qr,) = rowwise(_fn_rope_q, [(q_lin, 2048, 0)], [], mla_nd, [(2048, BF16)],
                    name="mla_rope_q")
    y_b, lse = flash_fwd(qr, kv, kpe)
    y_ab = jnp.concatenate([y_a, y_b], axis=1)
    h1 = matmul(y_ab, W["l0_w_out"], "nn", F32, add=x, name="l0_w_out")
    h2, ffn0 = _ffn_fwd(h1, n["l0_ffn_norm"], W["l0_ffn_w_up"], W["l0_ffn_conv_w"], n["l0_ffn_conv_b"],
                        W["l0_ffn_w_down"], "l0")
    h3, ple0 = _ple_fwd(h2, p0, W["l0_ple_proj"], n["l0_ple_gate_norm"], W["l0_ple_gate"], "l0")

    (hn1,) = rowwise(_fn_rms, [(h3, D_MODEL, 0)], [n["l1_attn_norm"]], [], [(D_MODEL, BF16)], name="l1_attn_norm")
    zz = matmul(hn1, W["l1_w_in"], "nn", F32, name="l1_w_in")
    o_ret, ret_states = ret_fwd(zz, cr, sr)
    gate_rows = [(zz, 4096, 2), (o_ret, 4096, 0)]
    (yg,) = rowwise(_fn_ret_gate, gate_rows, [n["l1_ret_norm"]], [], [(4096, BF16)], name="ret_gate")
    h4 = matmul(yg, W["l1_w_out"], "nn", F32, add=h3, name="l1_w_out")
    h5, ffn1 = _ffn_fwd(h4, n["l1_ffn_norm"], W["l1_ffn_w_up"], W["l1_ffn_conv_w"], n["l1_ffn_conv_b"],
                        W["l1_ffn_w_down"], "l1")
    h6, ple1 = _ple_fwd(h5, p1, W["l1_ple_proj"], n["l1_ple_gate_norm"], W["l1_ple_gate"], "l1")

    loss_vec, dh, G["final_norm"] = loss_head(h6, target, n["final_norm"])

    dh, g = _ple_bwd(dh, h5, p1, W["l1_ple_proj"], n["l1_ple_gate_norm"], W["l1_ple_gate"], ple1, "l1")
    G.update({"l1_" + k: v for k, v in g.items()})
    dh, g = _ffn_bwd(dh, h4, n["l1_ffn_norm"], W["l1_ffn_w_up"], W["l1_ffn_conv_w"], n["l1_ffn_conv_b"],
                     W["l1_ffn_w_down"], ffn1, "l1")
    G.update({"l1_" + k: v for k, v in g.items()})

    dyg = matmul(dh, W["l1_w_out"], "nt", F32, name="l1_w_out_dx")
    G["l1_w_out"] = matmul(yg, dh, "tn", BF16, name="l1_w_out_dw")
    (dg, do_ret), (G["l1_ret_norm"],) = rowwise_bwd(_fn_ret_gate, gate_rows, [n["l1_ret_norm"]], [],
                                                   [(dyg, 4096, 0)], [BF16, F32], name="ret_gate_bwd")
    dq, dk, dv = ret_bwd(zz, cr, sr, ret_states, do_ret)
    dzz = jnp.concatenate([dq, dk, dv, dg], axis=1)
    G["l1_w_in"] = matmul(hn1, dzz, "tn", BF16, name="l1_w_in_dw")
    dhn = matmul(dzz, W["l1_w_in"], "nt", F32, name="l1_w_in_dx")
    (dh,), (G["l1_attn_norm"],) = rowwise_bwd(_fn_rms, [(h3, D_MODEL, 0)], [n["l1_attn_norm"]], [],
                                             [(dhn, D_MODEL, 0)], [F32], adds=[(dh, D_MODEL, 0)],
                                             name="l1_attn_norm_bwd")

    dh, g = _ple_bwd(dh, h2, p0, W["l0_ple_proj"], n["l0_ple_gate_norm"], W["l0_ple_gate"], ple0, "l0")
    G.update({"l0_" + k: v for k, v in g.items()})
    dh, g = _ffn_bwd(dh, h1, n["l0_ffn_norm"], W["l0_ffn_w_up"], W["l0_ffn_conv_w"], n["l0_ffn_conv_b"],
                     W["l0_ffn_w_down"], ffn0, "l0")
    G.update({"l0_" + k: v for k, v in g.items()})

    dy_ab = matmul(dh, W["l0_w_out"], "nt", F32, name="l0_w_out_dx")
    G["l0_w_out"] = matmul(y_ab, dh, "tn", BF16, name="l0_w_out_dw")
    dq, dk, dv, dz, dab, g_a, g_dt, G["l0_gdn_norm"] = gdn_bwd(qkv, zin, a_row, dt_row, gdn_nw, gdn_states, dy_ab, 0)
    G["l0_gdn_A_log"], G["l0_gdn_dt_bias"] = g_a[:, :GDN_HEADS], g_dt[:, :GDN_HEADS]
    dpre, G["l0_gdn_conv"] = gdn_conv_bwd(zin, W["l0_gdn_conv"], jnp.concatenate([dq, dk, dv], axis=1))
    dqkv = conv_transpose(dpre, W["l0_gdn_conv"], BF16, name="gdn_conv_dx")
    dqr = flash_bwd_dq(qr, kv, kpe, y_b, lse, dy_ab, MLA_HEADS)
    dkv, dkpe = flash_bwd_dkv(qr, kv, kpe, y_b, lse, dy_ab, MLA_HEADS)
    (dq_lin,), _ = rowwise_bwd(_fn_rope_q, [(q_lin, 2048, 0)], [], mla_nd, [(dqr, 2048, 0)], [BF16],
                               name="mla_rope_q_bwd")
    G["l0_mla_w_uq"] = _unprep_w_uq(matmul(cqn, dq_lin, "tn", BF16, name="mla_w_uq_dw"))
    dcqn = matmul(dq_lin, w_uq, "nt", F32, name="mla_w_uq_dx")
    G["l0_mla_w_ukv"] = _unprep_w_ukv(matmul(ckvn, dkv, "tn", BF16, name="mla_w_ukv_dw"))
    dckvn = matmul(dkv, w_ukv, "nt", F32, name="mla_w_ukv_dx")
    (dcq, dckv, dkr), (G["l0_mla_q_norm"], G["l0_mla_kv_norm"]) = rowwise_bwd(
        _fn_mla_pre, mla_rows, [n["l0_mla_q_norm"], n["l0_mla_kv_norm"]], mla_nd,
        [(dcqn, MLA_Q_RANK, 0), (dckvn, MLA_KV_RANK, 0), (dkpe, LANES, 0)], [BF16, BF16, BF16], name="mla_pre_bwd")
    dzin = jnp.concatenate([dqkv, dz, dcq, dckv, dkr, dab.astype(BF16)], axis=1)
    G["l0_w_in"] = _unprep_w_in0(matmul(hn0, dzin, "tn", BF16, name="l0_w_in_dw", tn=768))
    dhn = matmul(dzin, w_in0, "nt", F32, name="l0_w_in_dx")
    (grad_x,), (G["l0_attn_norm"],) = rowwise_bwd(_fn_rms, [(x, D_MODEL, 0)], [n["l0_attn_norm"]], [],
                                                 [(dhn, D_MODEL, 0)], [F32], adds=[(dh, D_MODEL, 0)],
                                                 name="l0_attn_norm_bwd")
    return loss_vec[0, 0], grad_x, G


HBM = pl.BlockSpec(memory_space=pltpu.HBM)
VMEM = pl.BlockSpec(memory_space=pltpu.VMEM)


def _place():
    return lax.axis_index("x"), lax.axis_index("y"), lax.axis_index("c")


def _other_chips(x, y):
    return [(1 - x, y), (x, 1 - y), (1 - x, 1 - y)]


def _comm_call(body, *, name, out_shape, in_specs, out_specs, scratch_shapes):
    return pl.pallas_call(body, name=name, out_shape=out_shape, in_specs=in_specs, out_specs=out_specs,
                          scratch_shapes=list(scratch_shapes),
                          compiler_params=pltpu.CompilerParams(vmem_limit_bytes=VMEM_LIMIT_MB << 20))


def all_gather_chips(flat):
    R = flat.shape[0]
    half = R // 2

    def body(in_ref, out_ref, send_sems, recv_sems, local_sem):
        x, y, c = _place()
        me, sibling = (x, y, c), (x, y, 1 - c)
        chips = _other_chips(x, y)

        def rows(chip, hc):
            return out_ref.at[2 * chip[0] + chip[1], pl.ds(hc * half, half), :]

        def copy(k, chip, hc, to, src=None):
            return pltpu.make_async_remote_copy(src_ref=rows(chip, hc) if src is None else src, dst_ref=rows(chip, hc),
                                                send_sem=send_sems.at[k], recv_sem=recv_sems.at[k],
                                                device_id=to, device_id_type=MESH)

        mine = pltpu.make_async_copy(in_ref, out_ref.at[2 * x + y], local_sem)
        mine.start()
        my_half = in_ref.at[pl.ds(c * half, half), :]
        first = [copy(k, (x, y), c, (*chip, c), src=my_half) for k, chip in enumerate(chips)]
        for cp in first:
            cp.start()
        passed = [copy(3 + k, chip, c, sibling) for k, chip in enumerate(chips)]
        for k, chip in enumerate(chips):
            copy(k, chip, c, me).wait_recv()
            passed[k].start()
        for k, chip in enumerate(chips):
            copy(3 + k, chip, 1 - c, me).wait_recv()
        for cp in first + passed:
            cp.wait_send()
        mine.wait()

    return _comm_call(body, name="all_gather_chips", out_shape=jax.ShapeDtypeStruct((N_CHIPS,) + flat.shape, flat.dtype),
                      in_specs=[HBM], out_specs=HBM,
                      scratch_shapes=[pltpu.SemaphoreType.DMA((6,)), pltpu.SemaphoreType.DMA((6,)),
                                      pltpu.SemaphoreType.DMA])(flat)


def pair_swap_halves(g):
    _, R, L = g.shape
    half = R // 2

    def body(g_ref, out_ref, send_sem, recv_sem):
        x, y, c = _place()
        cp = pltpu.make_async_remote_copy(src_ref=g_ref.at[:, pl.ds((1 - c) * half, half), :], dst_ref=out_ref,
                                          send_sem=send_sem, recv_sem=recv_sem, device_id=(x, y, 1 - c),
                                          device_id_type=MESH)
        cp.start()
        cp.wait()

    return _comm_call(body, name="pair_swap_halves", out_shape=jax.ShapeDtypeStruct((N_CHIPS, half, L), g.dtype),
                      in_specs=[HBM], out_specs=HBM,
                      scratch_shapes=[pltpu.SemaphoreType.DMA, pltpu.SemaphoreType.DMA])(g)


def scatter_chips(p):
    _, H, L = p.shape

    def body(p_ref, out_ref, send_sems, recv_sems):
        x, y, c = _place()
        copies = [pltpu.make_async_remote_copy(src_ref=p_ref.at[2 * chip[0] + chip[1]], dst_ref=out_ref.at[k],
                                               send_sem=send_sems.at[k], recv_sem=recv_sems.at[k],
                                               device_id=(*chip, c), device_id_type=MESH)
                  for k, chip in enumerate(_other_chips(x, y))]
        for cp in copies:
            cp.start()
        for cp in copies:
            cp.wait()

    return _comm_call(body, name="scatter_chips", out_shape=jax.ShapeDtypeStruct((3, H, L), p.dtype),
                      in_specs=[HBM], out_specs=HBM,
                      scratch_shapes=[pltpu.SemaphoreType.DMA((3,)), pltpu.SemaphoreType.DMA((3,))])(p)


def pair_join_halves(rh):
    H, L = rh.shape

    def body(r_ref, out_ref, send_sem, recv_sem, local_sem):
        x, y, c = _place()
        mine = pltpu.make_async_copy(r_ref, out_ref.at[pl.ds(c * H, H), :], local_sem)
        mine.start()
        cp = pltpu.make_async_remote_copy(src_ref=r_ref, dst_ref=out_ref.at[pl.ds(c * H, H), :], send_sem=send_sem,
                                          recv_sem=recv_sem, device_id=(x, y, 1 - c), device_id_type=MESH)
        cp.start()
        cp.wait()
        mine.wait()

    return _comm_call(body, name="pair_join_halves", out_shape=jax.ShapeDtypeStruct((2 * H, L), rh.dtype),
                      in_specs=[HBM], out_specs=HBM,
                      scratch_shapes=[pltpu.SemaphoreType.DMA, pltpu.SemaphoreType.DMA, pltpu.SemaphoreType.DMA])(rh)


def all_reduce_small(v, name):
    n, L = v.shape
    n_dev = 8

    def body(v_ref, out_ref, buf, send_sems, recv_sems):
        x, y, c = _place()
        me = 4 * x + 2 * y + c
        buf[me] = v_ref[...]

        def copy(k, slot, peer):
            return pltpu.make_async_remote_copy(src_ref=v_ref, dst_ref=buf.at[slot], send_sem=send_sems.at[k],
                                                recv_sem=recv_sems.at[slot],
                                                device_id=(peer // 4, (peer // 2) % 2, peer % 2), device_id_type=MESH)

        sends = [copy(k - 1, me, (me + k) % n_dev) for k in range(1, n_dev)]
        for cp in sends:
            cp.start()
        for k in range(1, n_dev):
            src = (me + k) % n_dev
            copy(0, src, src).wait_recv()
        for cp in sends:
            cp.wait_send()
        acc = buf[0]
        for s in range(1, n_dev):
            acc = acc + buf[s]
        out_ref[...] = acc

    return _comm_call(body, name=name, out_shape=jax.ShapeDtypeStruct((n, L), v.dtype), in_specs=[VMEM], out_specs=VMEM,
                      scratch_shapes=[pltpu.VMEM((n_dev, n, L), v.dtype), pltpu.SemaphoreType.DMA((n_dev - 1,)),
                                      pltpu.SemaphoreType.DMA((n_dev,))])(v)


def _rows_tile(n, row_bytes, budget=1 << 20):
    best = SUBLANES if n % SUBLANES == 0 else n
    for t in range(SUBLANES, n + 1, SUBLANES):
        if n % t == 0 and t * row_bytes <= budget:
            best = t
    return best


def pair_add(g, got, c):
    _, R, L = g.shape
    half = R // 2
    tb = _rows_tile(half, L * 4)
    nb = half // tb

    def body(c_ref, g_ref, got_ref, o_ref):
        o_ref[...] = (g_ref[...].astype(F32) + got_ref[...].astype(F32)).astype(o_ref.dtype)

    spec = pltpu.PrefetchScalarGridSpec(
        num_scalar_prefetch=1, grid=(N_CHIPS, nb),
        in_specs=[pl.BlockSpec((None, tb, L), lambda s, i, c_ref: (s, c_ref[0] * nb + i, 0)),
                  pl.BlockSpec((None, tb, L), lambda s, i, c_ref: (s, i, 0))],
        out_specs=pl.BlockSpec((None, tb, L), lambda s, i, c_ref: (s, i, 0)))
    return pl.pallas_call(body, name="rs_pair_add", grid_spec=spec,
                          out_shape=jax.ShapeDtypeStruct((N_CHIPS, half, L), BF16),
                          compiler_params=pltpu.CompilerParams(dimension_semantics=("parallel", "parallel")))(
        jnp.reshape(c, (1,)).astype(jnp.int32), g, got)


def chip_add(p, got, j):
    _, H, L = p.shape
    tb = _rows_tile(H, L * 4)

    def body(j_ref, p_ref, got_ref, o_ref):
        acc = p_ref[...].astype(F32)
        for k in range(3):
            acc = acc + got_ref[k].astype(F32)
        o_ref[...] = acc

    spec = pltpu.PrefetchScalarGridSpec(
        num_scalar_prefetch=1, grid=(H // tb,),
        in_specs=[pl.BlockSpec((None, tb, L), lambda i, j_ref: (j_ref[0], i, 0)),
                  pl.BlockSpec((3, tb, L), lambda i, j_ref: (0, i, 0))],
        out_specs=pl.BlockSpec((tb, L), lambda i, j_ref: (i, 0)))
    return pl.pallas_call(body, name="rs_chip_add", grid_spec=spec, out_shape=jax.ShapeDtypeStruct((H, L), F32),
                          compiler_params=pltpu.CompilerParams(dimension_semantics=("parallel",)))(
        jnp.reshape(j, (1,)).astype(jnp.int32), p, got)


def adamw(w, g, m, v, name):
    r, c = w.shape
    tr = _rows_tile(r, c * 4)

    def body(w_ref, g_ref, m_ref, v_ref, d_ref, m_out, v_out):
        gg = g_ref[...]
        m2 = ADAM_B1 * m_ref[...] + (1.0 - ADAM_B1) * gg
        v2 = ADAM_B2 * v_ref[...] + (1.0 - ADAM_B2) * jnp.square(gg)
        m_hat = m2 / (1.0 - ADAM_B1 ** ADAM_STEP)
        v_hat = v2 / (1.0 - ADAM_B2 ** ADAM_STEP)
        d_ref[...] = -ADAM_LR * (m_hat / (jnp.sqrt(v_hat) + ADAM_EPS) + ADAM_WD * w_ref[...])
        m_out[...] = m2
        v_out[...] = v2

    blk = pl.BlockSpec((tr, c), lambda i: (i, 0))
    return _pcall(body, name=name, grid=(r // tr,), in_specs=[blk] * 4, out_specs=[blk] * 3,
                  out_shape=[jax.ShapeDtypeStruct((r, c), F32)] * 3, dims=("parallel",))(w, g, m, v)


WEIGHTS = ["l0_attn_norm", "l0_w_in", "l0_gdn_conv", "l0_gdn_A_log", "l0_gdn_dt_bias", "l0_gdn_norm", "l0_mla_q_norm",
           "l0_mla_w_uq", "l0_mla_kv_norm", "l0_mla_w_ukv", "l0_w_out", "l0_ffn_norm", "l0_ffn_w_up", "l0_ffn_conv_w",
           "l0_ffn_conv_b", "l0_ffn_w_down", "l0_ple_proj", "l0_ple_gate_norm", "l0_ple_gate", "l1_attn_norm",
           "l1_w_in", "l1_ret_norm", "l1_w_out", "l1_ffn_norm", "l1_ffn_w_up", "l1_ffn_conv_w", "l1_ffn_conv_b",
           "l1_ffn_w_down", "l1_ple_proj", "l1_ple_gate_norm", "l1_ple_gate", "final_norm"]
COL_SHARDED = ["l0_w_in", "l0_mla_w_uq", "l0_mla_w_ukv", "l0_ffn_w_up", "l0_ple_proj", "l1_w_in", "l1_ffn_w_up",
               "l1_ple_proj"]
ROW_SHARDED = ["l0_w_out", "l0_ffn_w_down", "l0_ple_gate", "l1_w_out", "l1_ffn_w_down", "l1_ple_gate"]
BIG = [k for k in WEIGHTS if k in COL_SHARDED or k in ROW_SHARDED]
SMALL_SHARDED = ["l0_gdn_conv", "l0_ffn_conv_w", "l1_ffn_conv_w"]
SMALL = [k for k in WEIGHTS if k not in BIG]
FLAT_ALIGN = 2 * 16 * LANES


def _pad_flat(v, align):
    n = v.shape[-1]
    total = -(-n // align) * align
    pad = [(0, 0)] * (v.ndim - 1) + [(0, total - n)]
    return jnp.pad(v, pad)


def _as_rows(v):
    return v.reshape(v.shape[:-1] + (v.shape[-1] // LANES, LANES))


def _pack_shards(shards):
    return _as_rows(_pad_flat(jnp.concatenate([s.astype(BF16).reshape(-1) for s in shards]), FLAT_ALIGN))


def _unpack_full(gathered, shard_shapes):
    flat = gathered.reshape(N_CHIPS, -1)
    out, off = {}, 0
    for k in BIG:
        r, c = shard_shapes[k]
        piece = flat[:, off:off + r * c].reshape(N_CHIPS, r, c)
        off += r * c
        out[k] = (jnp.transpose(piece, (1, 0, 2)).reshape(r, N_CHIPS * c) if k in COL_SHARDED
                  else piece.reshape(N_CHIPS * r, c))
    return out


def _pack_grads(G, shard_shapes):
    parts = []
    for k in BIG:
        r, c = shard_shapes[k]
        g = G[k].astype(BF16)
        if k in COL_SHARDED:
            g = jnp.transpose(g.reshape(r, N_CHIPS, c), (1, 0, 2))
        parts.append(g.reshape(N_CHIPS, r * c))
    return _as_rows(_pad_flat(jnp.concatenate(parts, axis=1), FLAT_ALIGN))


def _pack_small(vals):
    return _as_rows(_pad_flat(jnp.concatenate([v.astype(F32).reshape(-1) for v in vals]), SUBLANES * LANES))


def _unpack_small(rows, shapes):
    flat = rows.reshape(-1)
    out, off = [], 0
    for shp in shapes:
        n = int(np.prod(shp))
        out.append(flat[off:off + n].reshape(shp))
        off += n
    return out


INPUTS = (["x", "p", "positions"] + WEIGHTS + ["loss_target"] + ["m_" + k for k in WEIGHTS]
          + ["v_" + k for k in WEIGHTS])


def kernel(
        x, p, positions, l0_attn_norm, l0_w_in, l0_gdn_conv, l0_gdn_A_log, l0_gdn_dt_bias, l0_gdn_norm, l0_mla_q_norm,
        l0_mla_w_uq, l0_mla_kv_norm, l0_mla_w_ukv, l0_w_out, l0_ffn_norm, l0_ffn_w_up, l0_ffn_conv_w, l0_ffn_conv_b,
        l0_ffn_w_down, l0_ple_proj, l0_ple_gate_norm, l0_ple_gate, l1_attn_norm, l1_w_in, l1_ret_norm, l1_w_out,
        l1_ffn_norm, l1_ffn_w_up, l1_ffn_conv_w, l1_ffn_conv_b, l1_ffn_w_down, l1_ple_proj, l1_ple_gate_norm,
        l1_ple_gate, final_norm, loss_target, m_l0_attn_norm, m_l0_w_in, m_l0_gdn_conv, m_l0_gdn_A_log,
        m_l0_gdn_dt_bias, m_l0_gdn_norm, m_l0_mla_q_norm, m_l0_mla_w_uq, m_l0_mla_kv_norm, m_l0_mla_w_ukv, m_l0_w_out,
        m_l0_ffn_norm, m_l0_ffn_w_up, m_l0_ffn_conv_w, m_l0_ffn_conv_b, m_l0_ffn_w_down, m_l0_ple_proj,
        m_l0_ple_gate_norm, m_l0_ple_gate, m_l1_attn_norm, m_l1_w_in, m_l1_ret_norm, m_l1_w_out, m_l1_ffn_norm,
        m_l1_ffn_w_up, m_l1_ffn_conv_w, m_l1_ffn_conv_b, m_l1_ffn_w_down, m_l1_ple_proj, m_l1_ple_gate_norm,
        m_l1_ple_gate, m_final_norm, v_l0_attn_norm, v_l0_w_in, v_l0_gdn_conv, v_l0_gdn_A_log, v_l0_gdn_dt_bias,
        v_l0_gdn_norm, v_l0_mla_q_norm, v_l0_mla_w_uq, v_l0_mla_kv_norm, v_l0_mla_w_ukv, v_l0_w_out, v_l0_ffn_norm,
        v_l0_ffn_w_up, v_l0_ffn_conv_w, v_l0_ffn_conv_b, v_l0_ffn_w_down, v_l0_ple_proj, v_l0_ple_gate_norm,
        v_l0_ple_gate, v_l1_attn_norm, v_l1_w_in, v_l1_ret_norm, v_l1_w_out, v_l1_ffn_norm, v_l1_ffn_w_up,
        v_l1_ffn_conv_w, v_l1_ffn_conv_b, v_l1_ffn_w_down, v_l1_ple_proj, v_l1_ple_gate_norm, v_l1_ple_gate,
        v_final_norm):
    given = locals()
    a = {k: given[k] for k in INPUTS}
    x_i, y_i, c_i = _place()
    chip = 2 * x_i + y_i
    shard_shapes = {k: a[k].shape for k in WEIGHTS}

    gathered = all_gather_chips(_pack_shards([a[k] for k in BIG]))
    W = _unpack_full(gathered, shard_shapes)
    placed = []
    for k in SMALL_SHARDED:
        r, c = shard_shapes[k]
        mine = jnp.where(c_i == 0, a[k], jnp.zeros_like(a[k]))
        placed.append(lax.dynamic_update_slice(jnp.zeros((r, N_CHIPS * c), F32), mine, (0, chip * c)))
    full_small = _unpack_small(all_reduce_small(_pack_small(placed), "gather_small_weights"),
                               [p_.shape for p_ in placed])
    for k in SMALL:
        W[k] = a[k]
    W.update(dict(zip(SMALL_SHARDED, full_small)))

    loss_part, grad_x, G = local_step(a["x"][0], a["p"][:, 0], a["positions"][0], a["loss_target"][0], W)
    loss = lax.psum(loss_part, ("x", "y", "c"))

    g_flat = _pack_grads(G, shard_shapes)
    pair_sum = pair_add(g_flat, pair_swap_halves(g_flat), c_i)
    reduced_half = chip_add(pair_sum, scatter_chips(pair_sum), chip)
    reduced = pair_join_halves(reduced_half).reshape(-1)

    grads, deltas, new_m, new_v = {}, {}, {}, {}
    off = 0
    for k in BIG:
        r, c = shard_shapes[k]
        grads[k] = reduced[off:off + r * c].reshape(r, c)
        off += r * c
        deltas[k], new_m[k], new_v[k] = adamw(a[k], grads[k], a["m_" + k], a["v_" + k], "adamw_" + k)

    small_full = [G[k].reshape(-1) for k in SMALL]
    summed = _unpack_small(all_reduce_small(_pack_small(small_full), "reduce_small_grads"),
                           [G[k].shape for k in SMALL])
    for k, g in zip(SMALL, summed):
        if k in SMALL_SHARDED:
            r, c = shard_shapes[k]
            g = lax.dynamic_slice(g.reshape(r, N_CHIPS * c), (0, chip * c), (r, c))
        grads[k] = g.reshape(shard_shapes[k])
    packed = [_pack_small([d[k] for k in SMALL]) for d in (
        {k: a[k] for k in SMALL}, grads, {k: a["m_" + k] for k in SMALL}, {k: a["v_" + k] for k in SMALL})]
    outs = adamw(*packed, "adamw_small")
    shapes = [shard_shapes[k] for k in SMALL]
    for d, rows in zip((deltas, new_m, new_v), outs):
        d.update(dict(zip(SMALL, _unpack_small(rows, shapes))))

    return (loss, grad_x[None], *[grads[k] for k in WEIGHTS], *[deltas[k] for k in WEIGHTS],
            *[new_m[k] for k in WEIGHTS], *[new_v[k] for k in WEIGHTS])
```

```python
import functools
import math

import numpy as np
import jax
import jax.numpy as jnp
from jax import lax
from jax.experimental import pallas as pl
from jax.experimental.pallas import tpu as pltpu

F32, BF16 = jnp.float32, jnp.bfloat16
HI = lax.Precision.HIGHEST
MESH = pl.DeviceIdType.MESH

NORM_EPS = 1e-6
ROPE_THETA = 10000.0
D_MODEL = 2048
PLE_DIM = 256
GDN_HEADS, GDN_DK, GDN_DV, GDN_CONV = 8, 128, 128, 4
MLA_HEADS, MLA_Q_RANK, MLA_KV_RANK, MLA_NOPE, MLA_ROPE, MLA_V = 8, 512, 512, 128, 64, 128
RET_HEADS, RET_DK, RET_DV = 8, 256, 512
D_FF, FFN_CONV = 5632, 3
ADAM_LR, ADAM_B1, ADAM_B2, ADAM_EPS, ADAM_WD, ADAM_STEP = 0.001, 0.9, 0.999, 1e-08, 0.01, 10

LANES = 128
SUBLANES = 8
CHUNK = 128
N_CHIPS = 4
VMEM_LIMIT_MB = 56

ZIN_QKV, ZIN_Z, ZIN_CQ, ZIN_CKV, ZIN_KR, ZIN_AB, ZIN_W = 0, 3072, 4096, 4608, 5120, 5248, 5376


def _pcall(body, *, name, out_shape, grid=(), in_specs=None, out_specs=None, scratch_shapes=(), dims=None,
           aliases=None):
    params = dict(vmem_limit_bytes=VMEM_LIMIT_MB << 20)
    if dims is not None:
        params["dimension_semantics"] = dims
    kw = {}
    if aliases:
        kw["input_output_aliases"] = aliases
    return pl.pallas_call(body, name=name, out_shape=out_shape, grid=grid, in_specs=in_specs, out_specs=out_specs,
                          scratch_shapes=list(scratch_shapes), compiler_params=pltpu.CompilerParams(**params), **kw)


def _tile(n, target, mult=LANES):
    best = None
    for t in range(mult, min(n, target) + 1, mult):
        if n % t == 0:
            best = t
    return best or n


_DN = {"nn": (((1,), (0,)), ((), ())), "nt": (((1,), (1,)), ((), ())), "tn": (((0,), (0,)), ((), ()))}


def matmul(a, b, mode, out_dtype, *, name, add=None, b_shards=1, out_shards=1, tm=512, tn=1024, tk=2048):
    bs = b.shape[-2:]
    if mode == "nn":
        (M, K), (K2, N) = a.shape, (bs[0], bs[1] * b_shards)
    elif mode == "nt":
        (M, K), (N, K2) = a.shape, (bs[0], bs[1] * b_shards)
    else:
        (K, M), (K2, N) = a.shape, bs
    assert K == K2, (name, a.shape, b.shape)
    n_sh = N // max(b_shards if mode == "nn" else 1, out_shards)
    k_sh = K // (b_shards if mode == "nt" else 1)
    tm, tn, tk = _tile(M, tm), _tile(n_sh, tn), _tile(k_sh, tk)
    nk = K // tk
    nbn, nbk = n_sh // tn, k_sh // tk
    dn = _DN[mode]
    has_add = add is not None
    a_bytes, b_bytes = a.size * a.dtype.itemsize, b.size * b.dtype.itemsize
    i_outer = nk > 1 or a_bytes + (M // tm) * b_bytes <= b_bytes + (N // tn) * a_bytes

    def ij(g0, g1):
        return (g0, g1) if i_outer else (g1, g0)

    def body(*refs):
        a_ref, b_ref = refs[:2]
        add_ref = refs[2] if has_add else None
        o_ref = refs[3 if has_add else 2]
        part = lax.dot_general(a_ref[...].astype(BF16), b_ref[...].astype(BF16), dn, preferred_element_type=F32)

        def finish(r):
            if has_add:
                r = r + add_ref[...]
            o_ref[...] = r.astype(out_dtype)

        if nk == 1:
            finish(part)
            return
        acc = refs[-1]
        k = pl.program_id(2)

        @pl.when(k == 0)
        def _():
            acc[...] = part

        @pl.when(k > 0)
        def _():
            acc[...] += part

        @pl.when(k == nk - 1)
        def _():
            finish(acc[...])

    def spec(block, fn):
        return pl.BlockSpec(block, lambda g0, g1, k: fn(*ij(g0, g1), k))

    if mode == "tn":
        a_spec = spec((tk, tm), lambda i, j, k: (k, i))
    else:
        a_spec = spec((tm, tk), lambda i, j, k: (i, k))
    if mode == "nt":
        if b_shards > 1:
            b_spec = spec((None, tn, tk), lambda i, j, k: (k // nbk, j, k % nbk))
        else:
            b_spec = spec((tn, tk), lambda i, j, k: (j, k))
    elif b_shards > 1:
        b_spec = spec((None, tk, tn), lambda i, j, k: (j // nbn, k, j % nbn))
    else:
        b_spec = spec((tk, tn), lambda i, j, k: (k, j))
    in_specs = [a_spec, b_spec]
    args = [a, b]
    if has_add:
        in_specs.append(spec((tm, tn), lambda i, j, k: (i, j)))
        args.append(add)
    if out_shards > 1:
        out_spec = spec((None, tm, tn), lambda i, j, k: (j // nbn, i, j % nbn))
        out_shape = jax.ShapeDtypeStruct((out_shards, M, n_sh), out_dtype)
    else:
        out_spec = spec((tm, tn), lambda i, j, k: (i, j))
        out_shape = jax.ShapeDtypeStruct((M, N), out_dtype)
    gi, gj = M // tm, N // tn
    return _pcall(body, name=name, out_shape=out_shape, grid=(gi, gj, nk) if i_outer else (gj, gi, nk),
                  in_specs=in_specs, out_specs=out_spec,
                  scratch_shapes=[pltpu.VMEM((tm, tn), F32)] if nk > 1 else [],
                  dims=("parallel", "parallel", "arbitrary"))(*args)


def _row_spec(tr, w, c):
    return pl.BlockSpec((tr, w), lambda i: (i, c))


def _full_spec(arr):
    return pl.BlockSpec(arr.shape, lambda i: (0,) * arr.ndim)


def rowwise(fn, rows, params, nd_rows, outs, *, name, tr=256):
    S = rows[0][0].shape[0]
    tr = min(tr, S)
    n_in = len(rows) + len(params) + len(nd_rows)

    def body(*refs):
        res = fn(*[x[...] for x in refs[:n_in]])
        for o_ref, v in zip(refs[n_in:], res):
            o_ref[...] = v.astype(o_ref.dtype)

    return _pcall(body, name=name, grid=(S // tr,),
                  in_specs=([_row_spec(tr, w, c) for (_, w, c) in rows] + [_full_spec(q) for q in params]
                            + [_row_spec(tr, w, c) for (_, w, c) in nd_rows]),
                  out_specs=[_row_spec(tr, w, 0) for (w, _) in outs],
                  out_shape=[jax.ShapeDtypeStruct((S, w), dt) for (w, dt) in outs],
                  dims=("parallel",))(*[r[0] for r in rows], *params, *[r[0] for r in nd_rows])


def rowwise_bwd(fn, rows, params, nd_rows, cts, d_dtypes, *, name, adds=None, tr=256):
    S = rows[0][0].shape[0]
    tr = min(tr, S)
    n_r, n_p, n_n, n_c = len(rows), len(params), len(nd_rows), len(cts)
    adds = adds or [None] * n_r
    add_list = [a for a in adds if a is not None]
    n_a = len(add_list)

    def body(*refs):
        it = iter(refs)
        r = [next(it)[...] for _ in range(n_r)]
        p = [next(it)[...] for _ in range(n_p)]
        nd = [next(it)[...] for _ in range(n_n)]
        c = [next(it)[...] for _ in range(n_c)]
        ad = [next(it)[...] for _ in range(n_a)]
        d_row_refs = [next(it) for _ in range(n_r)]
        d_par_refs = [next(it) for _ in range(n_p)]
        outs, vjp = jax.vjp(lambda *dp: fn(*dp, *nd), *r, *p)
        g = vjp(tuple(ci.astype(o.dtype) for ci, o in zip(c, outs)))
        ai = 0
        for k in range(n_r):
            gk = g[k].astype(F32)
            if adds[k] is not None:
                gk = gk + ad[ai].astype(F32)
                ai += 1
            d_row_refs[k][...] = gk.astype(d_row_refs[k].dtype)

        @pl.when(pl.program_id(0) == 0)
        def _():
            for ref in d_par_refs:
                ref[...] = jnp.zeros_like(ref)

        for k in range(n_p):
            d_par_refs[k][...] += g[n_r + k].astype(F32)

    in_specs = ([_row_spec(tr, w, c) for (_, w, c) in rows] + [_full_spec(q) for q in params]
                + [_row_spec(tr, w, c) for (_, w, c) in nd_rows] + [_row_spec(tr, w, c) for (_, w, c) in cts]
                + [_row_spec(tr, w, c) for (_, w, c) in add_list])
    out_specs = [_row_spec(tr, w, 0) for (_, w, _) in rows] + [_full_spec(q) for q in params]
    out_shape = ([jax.ShapeDtypeStruct((S, w), dt) for (_, w, _), dt in zip(rows, d_dtypes)]
                 + [jax.ShapeDtypeStruct(q.shape, F32) for q in params])
    res = _pcall(body, name=name, grid=(S // tr,), in_specs=in_specs, out_specs=out_specs, out_shape=out_shape,
                 dims=("arbitrary",))(*[r[0] for r in rows], *params, *[r[0] for r in nd_rows],
                                      *[r[0] for r in cts], *[r[0] for r in add_list])
    return res[:n_r], res[n_r:]


def _rms(x, g):
    x = x.astype(F32)
    return x * lax.rsqrt(jnp.mean(x * x, axis=-1, keepdims=True) + NORM_EPS) * g


def _fn_rms(x, g):
    return (_rms(x, g),)


def _sigmoid(x):
    return 1.0 / (1.0 + jnp.exp(-x))


def _silu(x):
    return x * _sigmoid(x)


def _softplus(x):
    return jnp.maximum(x, 0.0) + jnp.log(1.0 + jnp.exp(-jnp.abs(x)))


def _fn_ple(h, pp, gl):
    return (h.astype(F32) + pp.astype(F32) * _sigmoid(gl.astype(F32)),)


def _fn_ple_terms(pp, gl):
    return (pp.astype(F32) * _sigmoid(gl.astype(F32)),)


def _rot_half_matrix():
    half = MLA_ROPE // 2
    r = lax.broadcasted_iota(jnp.int32, (LANES, LANES), 0)
    c = lax.broadcasted_iota(jnp.int32, (LANES, LANES), 1)
    plus = (c == r + half) & (r < half)
    minus = (r == c + half) & (c < half)
    return jnp.where(plus, 1.0, 0.0) - jnp.where(minus, 1.0, 0.0)


def _rope_pad(x, cosp, sinp):
    return x * cosp + jnp.dot(x, _rot_half_matrix(), precision=HI, preferred_element_type=F32) * sinp


def _fn_mla_pre(cq, ckv, kr, qn_w, kvn_w, cosp, sinp):
    return (_rms(cq, qn_w), _rms(ckv, kvn_w), _rope_pad(kr.astype(F32), cosp, sinp))


def _fn_rope_q(q, cosp, sinp):
    q = q.astype(F32)
    parts = []
    for h in range(MLA_HEADS):
        base = 2 * LANES * h
        parts.append(q[:, base:base + LANES])
        parts.append(_rope_pad(q[:, base + LANES:base + 2 * LANES], cosp, sinp))
    return (jnp.concatenate(parts, axis=1),)


def _fn_ret_gate(g, on, w):
    return (_silu(g.astype(F32)) * (on.astype(F32) * w),)


def _shift_down(cur, halo, s):
    if s == 0:
        return cur
    r = pltpu.roll(cur, s, 0)
    hs = pltpu.roll(halo, s, 0)
    row = lax.broadcasted_iota(jnp.int32, hs.shape, 0)
    first = jnp.where(row < s, hs, r[:SUBLANES])
    return jnp.concatenate([first, r[SUBLANES:]], axis=0)


def _shift_up(cur, halo, s):
    if s == 0:
        return cur
    n = cur.shape[0]
    r = pltpu.roll(cur, n - s, 0)
    hs = pltpu.roll(halo, SUBLANES - s, 0)
    row = lax.broadcasted_iota(jnp.int32, hs.shape, 0)
    last = jnp.where(row >= SUBLANES - s, hs, r[n - SUBLANES:])
    return jnp.concatenate([r[:n - SUBLANES], last], axis=0)


def _prev_halo_spec(tr, tw, col):
    return pl.BlockSpec((SUBLANES, tw), lambda c, i: (jnp.maximum(i * (tr // SUBLANES) - 1, 0), col(c)))


def _conv_taps(cur, halo, w_ref, width):
    taps = [_shift_down(cur, halo, width - 1 - j) for j in range(width)]
    y = taps[0] * w_ref[0:1, :]
    for j in range(1, width):
        y = y + taps[j] * w_ref[j:j + 1, :]
    return y, taps


def gdn_conv_fwd(zin, w, *, tr=512, tw=512):
    S = zin.shape[0]
    tr = min(tr, S)
    width, C = w.shape

    def body(cur_ref, halo_ref, w_ref, o_ref):
        i = pl.program_id(1)
        halo = halo_ref[...] * (i > 0).astype(F32)
        y, _ = _conv_taps(cur_ref[...], halo, w_ref, width)
        o_ref[...] = _silu(y)

    return _pcall(body, name="gdn_conv_fwd", grid=(C // tw, S // tr),
                  in_specs=[pl.BlockSpec((tr, tw), lambda c, i: (i, c)), _prev_halo_spec(tr, tw, lambda c: c),
                            pl.BlockSpec((width, tw), lambda c, i: (0, c))],
                  out_specs=pl.BlockSpec((tr, tw), lambda c, i: (i, c)),
                  out_shape=jax.ShapeDtypeStruct((S, C), F32), dims=("parallel", "arbitrary"))(zin, zin, w)


def gdn_conv_bwd(zin, w, dy, *, tr=512, tw=512):
    S = zin.shape[0]
    tr = min(tr, S)
    width, C = w.shape

    def body(cur_ref, halo_ref, w_ref, dy_ref, da_ref, dw_ref):
        i = pl.program_id(1)
        halo = halo_ref[...] * (i > 0).astype(F32)
        y, taps = _conv_taps(cur_ref[...], halo, w_ref, width)
        sg = _sigmoid(y)
        da = dy_ref[...] * (sg * (1.0 + y * (1.0 - sg)))
        da_ref[...] = da

        @pl.when(i == 0)
        def _():
            dw_ref[...] = jnp.zeros_like(dw_ref)

        for j in range(width):
            dw_ref[j:j + 1, :] += jnp.sum(da * taps[j], axis=0, keepdims=True)

    return _pcall(body, name="gdn_conv_bwd", grid=(C // tw, S // tr),
                  in_specs=[pl.BlockSpec((tr, tw), lambda c, i: (i, c)), _prev_halo_spec(tr, tw, lambda c: c),
                            pl.BlockSpec((width, tw), lambda c, i: (0, c)),
                            pl.BlockSpec((tr, tw), lambda c, i: (i, c))],
                  out_specs=[pl.BlockSpec((tr, tw), lambda c, i: (i, c)),
                             pl.BlockSpec((width, tw), lambda c, i: (0, c))],
                  out_shape=[jax.ShapeDtypeStruct((S, C), F32), jax.ShapeDtypeStruct((width, C), F32)],
                  dims=("parallel", "arbitrary"))(zin, zin, w, dy)


def conv_transpose(dy, w, out_dtype, *, name, tr=512, tw=512):
    S, C = dy.shape
    tr = min(tr, S)
    width = w.shape[0]
    n_i = S // tr

    def body(cur_ref, halo_ref, w_ref, o_ref):
        i = pl.program_id(1)
        halo = halo_ref[...] * (i < n_i - 1).astype(F32)
        cur = cur_ref[...]
        acc = cur * w_ref[width - 1:width, :]
        for s in range(1, width):
            acc = acc + _shift_up(cur, halo, s) * w_ref[width - 1 - s:width - s, :]
        o_ref[...] = acc.astype(out_dtype)

    nxt = pl.BlockSpec((SUBLANES, tw), lambda c, i: (jnp.minimum((i + 1) * (tr // SUBLANES), S // SUBLANES - 1), c))
    return _pcall(body, name=name, grid=(C // tw, n_i),
                  in_specs=[pl.BlockSpec((tr, tw), lambda c, i: (i, c)), nxt,
                            pl.BlockSpec((width, tw), lambda c, i: (0, c))],
                  out_specs=pl.BlockSpec((tr, tw), lambda c, i: (i, c)),
                  out_shape=jax.ShapeDtypeStruct((S, C), out_dtype), dims=("parallel", "arbitrary"))(dy, dy, w)


def ffn_conv_fwd(u, w, b, *, tr=512, tw=512):
    S, C2 = u.shape
    tr = min(tr, S)
    width = w.shape[0]
    half = C2 // 2
    nc = half // tw

    def body(g_ref, gh_ref, u_ref, uh_ref, wg_ref, wu_ref, bg_ref, bu_ref, o_ref):
        i = pl.program_id(1)
        live = (i > 0).astype(F32)
        yg, _ = _conv_taps(g_ref[...], gh_ref[...] * live, wg_ref, width)
        yu, _ = _conv_taps(u_ref[...], uh_ref[...] * live, wu_ref, width)
        o_ref[...] = (_silu(yg + bg_ref[...]) * (yu + bu_ref[...])).astype(o_ref.dtype)

    return _pcall(body, name="ffn_conv_fwd", grid=(nc, S // tr),
                  in_specs=[pl.BlockSpec((tr, tw), lambda c, i: (i, c)), _prev_halo_spec(tr, tw, lambda c: c),
                            pl.BlockSpec((tr, tw), lambda c, i: (i, c + nc)),
                            _prev_halo_spec(tr, tw, lambda c: c + nc),
                            pl.BlockSpec((width, tw), lambda c, i: (0, c)),
                            pl.BlockSpec((width, tw), lambda c, i: (0, c + nc)),
                            pl.BlockSpec((1, tw), lambda c, i: (0, c)), pl.BlockSpec((1, tw), lambda c, i: (0, c + nc))],
                  out_specs=pl.BlockSpec((tr, tw), lambda c, i: (i, c)),
                  out_shape=jax.ShapeDtypeStruct((S, half), BF16),
                  dims=("parallel", "arbitrary"))(u, u, u, u, w, w, b, b)


def ffn_conv_bwd(u, w, b, df, *, tr=512, tw=512):
    S, C2 = u.shape
    tr = min(tr, S)
    width = w.shape[0]
    half = C2 // 2
    nc = half // tw

    def body(g_ref, gh_ref, u_ref, uh_ref, wg_ref, wu_ref, bg_ref, bu_ref, df_ref, dc_ref, dw_ref, db_ref):
        t, i = pl.program_id(0), pl.program_id(2)
        live = (i > 0).astype(F32)
        yg, gt = _conv_taps(g_ref[...], gh_ref[...] * live, wg_ref, width)
        yu, ut = _conv_taps(u_ref[...], uh_ref[...] * live, wu_ref, width)
        yg = yg + bg_ref[...]
        yu = yu + bu_ref[...]
        sg = _sigmoid(yg)
        dfv = df_ref[...].astype(F32)
        is_gate = t == 0
        dc = jnp.where(is_gate, dfv * yu * (sg * (1.0 + yg * (1.0 - sg))), dfv * (yg * sg))
        dc_ref[...] = dc

        @pl.when(i == 0)
        def _():
            dw_ref[...] = jnp.zeros_like(dw_ref)
            db_ref[...] = jnp.zeros_like(db_ref)

        db_ref[...] += jnp.sum(dc, axis=0, keepdims=True)
        for j in range(width):
            tap = jnp.where(is_gate, gt[j], ut[j])
            dw_ref[j:j + 1, :] += jnp.sum(dc * tap, axis=0, keepdims=True)

    def prev(col):
        return pl.BlockSpec((SUBLANES, tw), lambda t, c, i: (jnp.maximum(i * (tr // SUBLANES) - 1, 0), col(c)))

    return _pcall(body, name="ffn_conv_bwd", grid=(2, nc, S // tr),
                  in_specs=[pl.BlockSpec((tr, tw), lambda t, c, i: (i, c)), prev(lambda c: c),
                            pl.BlockSpec((tr, tw), lambda t, c, i: (i, c + nc)), prev(lambda c: c + nc),
                            pl.BlockSpec((width, tw), lambda t, c, i: (0, c)),
                            pl.BlockSpec((width, tw), lambda t, c, i: (0, c + nc)),
                            pl.BlockSpec((1, tw), lambda t, c, i: (0, c)),
                            pl.BlockSpec((1, tw), lambda t, c, i: (0, c + nc)),
                            pl.BlockSpec((tr, tw), lambda t, c, i: (i, c))],
                  out_specs=[pl.BlockSpec((tr, tw), lambda t, c, i: (i, t * nc + c)),
                             pl.BlockSpec((width, tw), lambda t, c, i: (0, t * nc + c)),
                             pl.BlockSpec((1, tw), lambda t, c, i: (0, t * nc + c))],
                  out_shape=[jax.ShapeDtypeStruct((S, C2), F32), jax.ShapeDtypeStruct((width, C2), F32),
                             jax.ShapeDtypeStruct((1, C2), F32)],
                  dims=("parallel", "parallel", "arbitrary"))(u, u, u, u, w, w, b, b, df)


_MODE_OF = {v: k for k, v in _DN.items()}


def _bf16_dot(a, b, mode):
    return lax.dot_general(a.astype(BF16), b.astype(BF16), _DN[mode], preferred_element_type=F32)


@functools.partial(jax.custom_vjp, nondiff_argnums=(2,))
def _bdot_mode(a, b, mode):
    return _bf16_dot(a, b, mode)


def _bdot_fwd(a, b, mode):
    return _bf16_dot(a, b, mode), (a, b)


def _bdot_bwd(mode, res, ct):
    a, b = res
    if mode == "nn":
        da, db = _bf16_dot(ct, b, "nt"), _bf16_dot(a, ct, "tn")
    elif mode == "nt":
        da, db = _bf16_dot(ct, b, "nn"), _bf16_dot(ct, a, "tn")
    else:
        da, db = _bf16_dot(b, ct, "nt"), _bf16_dot(a, ct, "nn")
    return da.astype(a.dtype), db.astype(b.dtype)


_bdot_mode.defvjp(_bdot_fwd, _bdot_bwd)


def _bdot(a, b, dn=_DN["nn"]):
    return _bdot_mode(a, b, _MODE_OF[dn])


def _hi_lo(x):
    hi = x.astype(BF16)
    return hi, (x - hi.astype(F32)).astype(BF16)


def _dot3_raw(a, b, mode):
    a1, a2 = _hi_lo(a)
    b1, b2 = _hi_lo(b)
    dot = lambda p, q: lax.dot_general(p, q, _DN[mode], preferred_element_type=F32)
    return dot(a1, b1) + (dot(a1, b2) + dot(a2, b1))


@functools.partial(jax.custom_vjp, nondiff_argnums=(2,))
def _dot3(a, b, mode="nn"):
    return _dot3_raw(a, b, mode)


def _dot3_fwd(a, b, mode):
    return _dot3_raw(a, b, mode), (a, b)


def _dot3_bwd(mode, res, ct):
    a, b = res
    if mode == "nn":
        return _dot3_raw(ct, b, "nt"), _dot3_raw(a, ct, "tn")
    if mode == "nt":
        return _dot3_raw(ct, b, "nn"), _dot3_raw(ct, a, "tn")
    return _dot3_raw(b, ct, "nt"), _dot3_raw(a, ct, "nn")


_dot3.defvjp(_dot3_fwd, _dot3_bwd)


@functools.partial(jax.custom_vjp, nondiff_argnums=(2,))
def _gdot(a, b, mode="nn"):
    return _bf16_dot(a, b, mode)


def _gdot_fwd(a, b, mode):
    return _bf16_dot(a, b, mode), (a.astype(BF16), b.astype(BF16))


def _ct_dot(p, q, mode, ct_first):
    ct, r = (p, q) if ct_first else (q, p)
    c1, c2 = _hi_lo(ct)
    dot = lambda c: lax.dot_general(*((c, r) if ct_first else (r, c)), _DN[mode], preferred_element_type=F32)
    return dot(c1) + dot(c2)


def _gdot_bwd(mode, res, ct):
    a, b = res
    if mode == "nn":
        return _ct_dot(ct, b, "nt", True), _ct_dot(a, ct, "tn", False)
    if mode == "nt":
        return _ct_dot(ct, b, "nn", True), _ct_dot(ct, a, "tn", True)
    return _ct_dot(b, ct, "nt", False), _ct_dot(a, ct, "nn", False)


_gdot.defvjp(_gdot_fwd, _gdot_bwd)


def _split_dot(ones, x):
    x1 = x.astype(BF16)
    r1 = x - x1.astype(F32)
    x2 = r1.astype(BF16)
    x3 = (r1 - x2.astype(F32)).astype(BF16)
    m = ones.astype(BF16)
    dot = lambda p: lax.dot_general(m, p, _DN["nn"], preferred_element_type=F32)
    return dot(x1) + dot(x2) + dot(x3)


@jax.custom_vjp
def _tri_cumsum(x, lower, upper):
    return _split_dot(lower, x)


def _tri_cumsum_fwd(x, lower, upper):
    return _split_dot(lower, x), (lower, upper)


def _tri_cumsum_bwd(res, ct):
    lower, upper = res
    return _split_dot(upper, ct), jnp.zeros_like(lower), jnp.zeros_like(upper)


_tri_cumsum.defvjp(_tri_cumsum_fwd, _tri_cumsum_bwd)


def _tri_masks(n):
    r = lax.broadcasted_iota(jnp.int32, (n, n), 0)
    c = lax.broadcasted_iota(jnp.int32, (n, n), 1)
    return r >= c, r > c


def _gdn_chunk(q, k, v, z, ab, a_row, dt_row, norm_w, state, sel_a, sel_b):
    C = q.shape[0]
    incl, strict = _tri_masks(C)
    lower = jnp.where(incl, 1.0, 0.0)
    qn = q * lax.rsqrt(jnp.sum(q * q, axis=-1, keepdims=True) + NORM_EPS) * (GDN_DK ** -0.5)
    kn = k * lax.rsqrt(jnp.sum(k * k, axis=-1, keepdims=True) + NORM_EPS)
    g = jnp.sum(-jnp.exp(a_row) * _softplus(ab + dt_row) * sel_a, axis=-1, keepdims=True)
    beta = jnp.sum(_sigmoid(ab) * sel_b, axis=-1, keepdims=True)
    gb = jnp.broadcast_to(g, (C, C))
    g_col = _tri_cumsum(gb, lower, jnp.where(strict, 0.0, 1.0))
    g_row = g_col.T
    g_last = jnp.sum(gb, axis=0, keepdims=True)
    gamma = jnp.where(incl, jnp.exp(jnp.where(incl, g_col - g_row, 0.0)), 0.0)
    e_col = jnp.exp(g_col)
    kb = kn * beta
    a_mat = jnp.where(strict, _gdot(kb, kn, "nt") * gamma, 0.0)
    x = jnp.concatenate([v * beta, kb * e_col], axis=1)
    pw = -a_mat
    steps = int(math.log2(C))
    for it in range(steps):
        x = x + _dot3(pw, x, "nn")
        if it < steps - 1:
            pw = _dot3(pw, pw, "nn")
    u, w = x[:, :GDN_DV], x[:, GDN_DV:]
    attn = _gdot(qn, kn, "nt") * gamma
    q_dec = qn * e_col
    k_dec = kn * jnp.exp(g_last - g_col)
    v_new = u - _gdot(w, state, "nn")
    o = _gdot(q_dec, state, "nn") + _gdot(attn, v_new, "nn")
    state_new = state * jnp.exp(jnp.broadcast_to(g_last, state.shape)) + _gdot(k_dec, v_new, "tn")
    y = _rms(o, norm_w) * _silu(z)
    return y, state_new


def _head_selectors(h):
    lane = lax.broadcasted_iota(jnp.int32, (1, LANES), 1)
    return jnp.where(lane == h, 1.0, 0.0), jnp.where(lane == h + GDN_HEADS, 1.0, 0.0)


def _gdn_in_specs(rev, nc):
    def n_(n):
        return nc - 1 - n if rev else n
    H = GDN_HEADS
    blk = lambda off: pl.BlockSpec((CHUNK, LANES), lambda n, h: (n_(n), off + h))
    row = pl.BlockSpec((1, LANES), lambda n, h: (0, 0))
    return n_, [blk(0), blk(H), blk(2 * H), blk(ZIN_Z // LANES),
                pl.BlockSpec((CHUNK, LANES), lambda n, h: (n_(n), ZIN_AB // LANES)), row, row, row]


def gdn_fwd(qkv, zin, a_row, dt_row, norm_w):
    S = qkv.shape[0]
    nc = S // CHUNK
    H = GDN_HEADS
    _, in_specs = _gdn_in_specs(False, nc)

    def body(q_ref, k_ref, v_ref, z_ref, ab_ref, a_ref, dt_ref, nw_ref, y_ref, st_ref, state):
        n, h = pl.program_id(0), pl.program_id(1)

        @pl.when(n == 0)
        def _():
            state[h] = jnp.zeros((GDN_DK, GDN_DV), F32)

        st = state[h]
        st_ref[...] = st
        sel_a, sel_b = _head_selectors(h)
        y, st_new = _gdn_chunk(q_ref[...], k_ref[...], v_ref[...], z_ref[...], ab_ref[...], a_ref[...],
                               dt_ref[...], nw_ref[...], st, sel_a, sel_b)
        y_ref[...] = y.astype(y_ref.dtype)
        state[h] = st_new

    return _pcall(body, name="gdn_fwd", grid=(nc, H), in_specs=in_specs,
                  out_specs=[pl.BlockSpec((CHUNK, LANES), lambda n, h: (n, h)),
                             pl.BlockSpec((None, None, GDN_DK, GDN_DV), lambda n, h: (h, n, 0, 0))],
                  out_shape=[jax.ShapeDtypeStruct((S, H * GDN_DV), BF16),
                             jax.ShapeDtypeStruct((H, nc, GDN_DK, GDN_DV), F32)],
                  scratch_shapes=[pltpu.VMEM((H, GDN_DK, GDN_DV), F32)],
                  dims=("arbitrary", "arbitrary"))(qkv, qkv, qkv, zin, zin, a_row, dt_row, norm_w)


def gdn_bwd(qkv, zin, a_row, dt_row, norm_w, states, dy, dy_col0):
    S = qkv.shape[0]
    nc = S // CHUNK
    H = GDN_HEADS
    n_, in_specs = _gdn_in_specs(True, nc)
    in_specs = in_specs + [pl.BlockSpec((None, None, GDN_DK, GDN_DV), lambda n, h: (h, n_(n), 0, 0)),
                           pl.BlockSpec((CHUNK, LANES), lambda n, h: (n_(n), dy_col0 + h))]

    def body(q_ref, k_ref, v_ref, z_ref, ab_ref, a_ref, dt_ref, nw_ref, st_ref, dy_ref,
             dq_ref, dk_ref, dv_ref, dz_ref, dab_ref, da_ref, ddt_ref, dnw_ref, dstate):
        n, h = pl.program_id(0), pl.program_id(1)

        @pl.when(n == 0)
        def _():
            dstate[h] = jnp.zeros((GDN_DK, GDN_DV), F32)

        @pl.when((n == 0) & (h == 0))
        def _():
            da_ref[...] = jnp.zeros_like(da_ref)
            ddt_ref[...] = jnp.zeros_like(ddt_ref)
            dnw_ref[...] = jnp.zeros_like(dnw_ref)

        sel_a, sel_b = _head_selectors(h)
        _, vjp = jax.vjp(lambda *a: _gdn_chunk(*a, sel_a, sel_b), q_ref[...], k_ref[...], v_ref[...], z_ref[...],
                         ab_ref[...], a_ref[...], dt_ref[...], nw_ref[...], st_ref[...])
        dq, dk, dv, dz, dab, da, ddt, dnw, dst = vjp((dy_ref[...].astype(F32), dstate[h]))
        dq_ref[...] = dq
        dk_ref[...] = dk
        dv_ref[...] = dv
        dz_ref[...] = dz.astype(dz_ref.dtype)

        @pl.when(h == 0)
        def _():
            dab_ref[...] = jnp.zeros_like(dab_ref)

        dab_ref[...] += dab
        da_ref[...] += da
        ddt_ref[...] += ddt
        dnw_ref[...] += dnw
        dstate[h] = dst

    blk = pl.BlockSpec((CHUNK, LANES), lambda n, h: (n_(n), h))
    row = pl.BlockSpec((1, LANES), lambda n, h: (0, 0))
    wide = jax.ShapeDtypeStruct((S, H * LANES), F32)
    return _pcall(body, name="gdn_bwd", grid=(nc, H), in_specs=in_specs,
                  out_specs=[blk, blk, blk, blk, pl.BlockSpec((CHUNK, LANES), lambda n, h: (n_(n), 0)), row, row, row],
                  out_shape=[wide, wide, wide, jax.ShapeDtypeStruct((S, H * LANES), BF16),
                             jax.ShapeDtypeStruct((S, LANES), F32)] + [jax.ShapeDtypeStruct((1, LANES), F32)] * 3,
                  scratch_shapes=[pltpu.VMEM((H, GDN_DK, GDN_DV), F32)],
                  dims=("arbitrary", "arbitrary"))(qkv, qkv, qkv, zin, zin, a_row, dt_row, norm_w, states, dy)


def _rope_full(x, cos, sin):
    x1, x2 = x[:, :RET_DK // 2], x[:, RET_DK // 2:]
    return jnp.concatenate([x1 * cos - x2 * sin, x2 * cos + x1 * sin], axis=1)


def _ret_chunk(q, k, v, cos, sin, lg, state):
    C = q.shape[0]
    incl, _ = _tri_masks(C)
    qr = _rope_full(q, cos, sin)
    kr = _rope_full(k, cos, sin) * (RET_DK ** -0.5)
    r = lax.broadcasted_iota(jnp.int32, (C, C), 0)
    c = lax.broadcasted_iota(jnp.int32, (C, C), 1)
    dist = jnp.where(incl, (r - c).astype(F32), 0.0)
    decay = jnp.where(incl, jnp.exp(dist * lg), 0.0)
    pos = lax.broadcasted_iota(jnp.int32, (C, 1), 0).astype(F32)
    lg1 = lg[:, :1]
    xi = jnp.exp((pos + 1.0) * lg1)
    zeta = jnp.exp((C - 1.0 - pos) * lg1)
    inner = _bdot(_bdot(qr, kr, _DN["nt"]) * decay, v)
    cross = _bdot(qr * xi, state)
    state_new = state * jnp.exp(C * lg1) + _bdot(kr * zeta, v, _DN["tn"])
    o = inner + cross
    mu = jnp.mean(o, axis=-1, keepdims=True)
    var = jnp.mean(jnp.square(o - mu), axis=-1, keepdims=True)
    return (o - mu) * lax.rsqrt(var + NORM_EPS), state_new


def _ret_log_gamma():
    lg = np.log1p(-np.power(2.0, -5.0 - np.arange(RET_HEADS, dtype=np.float64))).astype(np.float32)
    return jnp.asarray(np.broadcast_to(lg[:, None, None], (RET_HEADS, 1, LANES)).copy())


def _ret_in_specs(rev, nc):
    def n_(n):
        return nc - 1 - n if rev else n
    H = RET_HEADS
    return n_, [pl.BlockSpec((CHUNK, RET_DK), lambda n, h: (n_(n), h)),
                pl.BlockSpec((CHUNK, RET_DK), lambda n, h: (n_(n), H + h)),
                pl.BlockSpec((CHUNK, RET_DV), lambda n, h: (n_(n), 2 * H * RET_DK // RET_DV + h)),
                pl.BlockSpec((CHUNK, LANES), lambda n, h: (n_(n), 0)),
                pl.BlockSpec((CHUNK, LANES), lambda n, h: (n_(n), 0)),
                pl.BlockSpec((None, 1, LANES), lambda n, h: (h, 0, 0))]


def ret_fwd(zz, cos, sin):
    S = zz.shape[0]
    nc = S // CHUNK
    H = RET_HEADS
    _, in_specs = _ret_in_specs(False, nc)

    def body(q_ref, k_ref, v_ref, cos_ref, sin_ref, lg_ref, o_ref, st_ref, state):
        n, h = pl.program_id(0), pl.program_id(1)

        @pl.when(n == 0)
        def _():
            state[h] = jnp.zeros((RET_DK, RET_DV), F32)

        st = state[h]
        st_ref[...] = st
        o, st_new = _ret_chunk(q_ref[...], k_ref[...], v_ref[...], cos_ref[...], sin_ref[...], lg_ref[...], st)
        o_ref[...] = o
        state[h] = st_new

    return _pcall(body, name="ret_fwd", grid=(nc, H), in_specs=in_specs,
                  out_specs=[pl.BlockSpec((CHUNK, RET_DV), lambda n, h: (n, h)),
                             pl.BlockSpec((None, None, RET_DK, RET_DV), lambda n, h: (h, n, 0, 0))],
                  out_shape=[jax.ShapeDtypeStruct((S, H * RET_DV), F32),
                             jax.ShapeDtypeStruct((H, nc, RET_DK, RET_DV), F32)],
                  scratch_shapes=[pltpu.VMEM((H, RET_DK, RET_DV), F32)],
                  dims=("arbitrary", "arbitrary"))(zz, zz, zz, cos, sin, _ret_log_gamma())


def ret_bwd(zz, cos, sin, states, do):
    S = zz.shape[0]
    nc = S // CHUNK
    H = RET_HEADS
    n_, in_specs = _ret_in_specs(True, nc)
    in_specs = in_specs + [pl.BlockSpec((None, None, RET_DK, RET_DV), lambda n, h: (h, n_(n), 0, 0)),
                           pl.BlockSpec((CHUNK, RET_DV), lambda n, h: (n_(n), h))]

    def body(q_ref, k_ref, v_ref, cos_ref, sin_ref, lg_ref, st_ref, do_ref, dq_ref, dk_ref, dv_ref, dstate):
        n, h = pl.program_id(0), pl.program_id(1)

        @pl.when(n == 0)
        def _():
            dstate[h] = jnp.zeros((RET_DK, RET_DV), F32)

        cos, sin, lg = cos_ref[...], sin_ref[...], lg_ref[...]
        _, vjp = jax.vjp(lambda q, k, v, st: _ret_chunk(q, k, v, cos, sin, lg, st),
                         q_ref[...], k_ref[...], v_ref[...], st_ref[...])
        dq, dk, dv, dst = vjp((do_ref[...], dstate[h]))
        dq_ref[...] = dq.astype(dq_ref.dtype)
        dk_ref[...] = dk.astype(dk_ref.dtype)
        dv_ref[...] = dv.astype(dv_ref.dtype)
        dstate[h] = dst

    return _pcall(body, name="ret_bwd", grid=(nc, H), in_specs=in_specs,
                  out_specs=[pl.BlockSpec((CHUNK, RET_DK), lambda n, h: (n_(n), h)),
                             pl.BlockSpec((CHUNK, RET_DK), lambda n, h: (n_(n), h)),
                             pl.BlockSpec((CHUNK, RET_DV), lambda n, h: (n_(n), h))],
                  out_shape=[jax.ShapeDtypeStruct((S, H * RET_DK), BF16), jax.ShapeDtypeStruct((S, H * RET_DK), BF16),
                             jax.ShapeDtypeStruct((S, H * RET_DV), BF16)],
                  scratch_shapes=[pltpu.VMEM((H, RET_DK, RET_DV), F32)],
                  dims=("arbitrary", "arbitrary"))(zz, zz, zz, cos, sin, _ret_log_gamma(), states, do)


MLA_SCALE = (MLA_NOPE + MLA_ROPE) ** -0.5
NEG = -1e30


def _mla_scores(q, kn, kpe, i, j, t):
    s = (lax.dot_general(q[:, :LANES], kn, _DN["nt"], preferred_element_type=F32)
         + lax.dot_general(q[:, LANES:], kpe, _DN["nt"], preferred_element_type=F32)) * MLA_SCALE
    row = i * t + lax.broadcasted_iota(jnp.int32, s.shape, 0)
    col = j * t + lax.broadcasted_iota(jnp.int32, s.shape, 1)
    return s, col <= row


def flash_fwd(qr, kv, kpe, *, t=512):
    S = qr.shape[0]
    t = min(t, S)
    nb = S // t
    H = MLA_HEADS

    def body(q_ref, kn_ref, v_ref, kpe_ref, o_ref, lse_ref, m_s, l_s, acc):
        i, j = pl.program_id(1), pl.program_id(2)

        @pl.when(j == 0)
        def _():
            m_s[...] = jnp.full_like(m_s, NEG)
            l_s[...] = jnp.zeros_like(l_s)
            acc[...] = jnp.zeros_like(acc)

        @pl.when(j <= i)
        def _():
            s, ok = _mla_scores(q_ref[...], kn_ref[...], kpe_ref[...], i, j, t)
            s = jnp.where(ok, s, NEG)
            m_new = jnp.maximum(m_s[...], jnp.max(s, axis=-1, keepdims=True))
            p = jnp.where(ok, jnp.exp(s - m_new), 0.0)
            alpha = jnp.exp(m_s[...] - m_new)
            l_s[...] = alpha * l_s[...] + jnp.sum(p, axis=-1, keepdims=True)
            acc[...] = alpha * acc[...] + _bdot(p, v_ref[...])
            m_s[...] = m_new

        @pl.when(j == nb - 1)
        def _():
            o_ref[...] = (acc[...] / l_s[...]).astype(o_ref.dtype)
            lse_ref[...] = m_s[...] + jnp.log(l_s[...])

    kmap = lambda off: (lambda h, i, j: (jnp.minimum(j, i), off + h))
    return _pcall(body, name="mla_flash_fwd", grid=(H, nb, nb),
                  in_specs=[pl.BlockSpec((t, 2 * LANES), lambda h, i, j: (i, h)),
                            pl.BlockSpec((t, LANES), kmap(0)), pl.BlockSpec((t, LANES), kmap(H)),
                            pl.BlockSpec((t, LANES), lambda h, i, j: (jnp.minimum(j, i), 0))],
                  out_specs=[pl.BlockSpec((t, LANES), lambda h, i, j: (i, h)),
                             pl.BlockSpec((None, t, 1), lambda h, i, j: (h, i, 0))],
                  out_shape=[jax.ShapeDtypeStruct((S, H * MLA_V), BF16), jax.ShapeDtypeStruct((H, S, 1), F32)],
                  scratch_shapes=[pltpu.VMEM((t, 1), F32), pltpu.VMEM((t, 1), F32), pltpu.VMEM((t, MLA_V), F32)],
                  dims=("parallel", "parallel", "arbitrary"))(qr, kv, kv, kpe)


def _mla_p_ds(q, kn, v, kpe, do, o, lse, i, j, t):
    s, ok = _mla_scores(q, kn, kpe, i, j, t)
    p = jnp.where(ok, jnp.exp(jnp.where(ok, s, 0.0) - lse), 0.0)
    dof = do.astype(F32)
    delta = jnp.sum(dof * o.astype(F32), axis=-1, keepdims=True)
    dp = lax.dot_general(do.astype(BF16), v, _DN["nt"], preferred_element_type=F32)
    ds = p * (dp - delta) * MLA_SCALE
    return p, ds


def flash_bwd_dq(qr, kv, kpe, o, lse, dy, dy_col0, *, t=512):
    S = qr.shape[0]
    t = min(t, S)
    nb = S // t
    H = MLA_HEADS

    def body(q_ref, kn_ref, v_ref, kpe_ref, o_ref, lse_ref, do_ref, dq_ref, acc):
        i, j = pl.program_id(1), pl.program_id(2)

        @pl.when(j == 0)
        def _():
            acc[...] = jnp.zeros_like(acc)

        @pl.when(j <= i)
        def _():
            _, ds = _mla_p_ds(q_ref[...], kn_ref[...], v_ref[...], kpe_ref[...], do_ref[...], o_ref[...],
                              lse_ref[...], i, j, t)
            acc[...] += jnp.concatenate([_bdot(ds, kn_ref[...]), _bdot(ds, kpe_ref[...])], axis=1)

        @pl.when(j == nb - 1)
        def _():
            dq_ref[...] = acc[...]

    kmap = lambda off: (lambda h, i, j: (jnp.minimum(j, i), off + h))
    return _pcall(body, name="mla_flash_dq", grid=(H, nb, nb),
                  in_specs=[pl.BlockSpec((t, 2 * LANES), lambda h, i, j: (i, h)),
                            pl.BlockSpec((t, LANES), kmap(0)), pl.BlockSpec((t, LANES), kmap(H)),
                            pl.BlockSpec((t, LANES), lambda h, i, j: (jnp.minimum(j, i), 0)),
                            pl.BlockSpec((t, LANES), lambda h, i, j: (i, h)),
                            pl.BlockSpec((None, t, 1), lambda h, i, j: (h, i, 0)),
                            pl.BlockSpec((t, LANES), lambda h, i, j: (i, dy_col0 + h))],
                  out_specs=pl.BlockSpec((t, 2 * LANES), lambda h, i, j: (i, h)),
                  out_shape=jax.ShapeDtypeStruct((S, H * 2 * LANES), F32),
                  scratch_shapes=[pltpu.VMEM((t, 2 * LANES), F32)],
                  dims=("parallel", "parallel", "arbitrary"))(qr, kv, kv, kpe, o, lse, dy)


def flash_bwd_dkv(qr, kv, kpe, o, lse, dy, dy_col0, *, t=512):
    S = qr.shape[0]
    t = min(t, S)
    nb = S // t
    H = MLA_HEADS

    def body(q_ref, kn_ref, v_ref, kpe_ref, o_ref, lse_ref, do_ref, dkn_ref, dv_ref, dkpe_ref, akn, av):
        j, h, i = pl.program_id(0), pl.program_id(1), pl.program_id(2)

        @pl.when(i == 0)
        def _():
            akn[...] = jnp.zeros_like(akn)
            av[...] = jnp.zeros_like(av)

        @pl.when((i == 0) & (h == 0))
        def _():
            dkpe_ref[...] = jnp.zeros_like(dkpe_ref)

        @pl.when(i >= j)
        def _():
            q = q_ref[...]
            p, ds = _mla_p_ds(q, kn_ref[...], v_ref[...], kpe_ref[...], do_ref[...], o_ref[...], lse_ref[...], i, j, t)
            av[...] += _bdot(p, do_ref[...], _DN["tn"])
            akn[...] += _bdot(ds, q[:, :LANES], _DN["tn"])
            dkpe_ref[...] += _bdot(ds, q[:, LANES:], _DN["tn"])

        @pl.when(i == nb - 1)
        def _():
            dkn_ref[...] = akn[...].astype(dkn_ref.dtype)
            dv_ref[...] = av[...].astype(dv_ref.dtype)

    qmap = lambda off: (lambda j, h, i: (jnp.maximum(i, j), off + h))
    dkn, dv, dkpe = _pcall(
        body, name="mla_flash_dkv", grid=(nb, H, nb),
        in_specs=[pl.BlockSpec((t, 2 * LANES), qmap(0)),
                  pl.BlockSpec((t, LANES), lambda j, h, i: (j, h)), pl.BlockSpec((t, LANES), lambda j, h, i: (j, H + h)),
                  pl.BlockSpec((t, LANES), lambda j, h, i: (j, 0)),
                  pl.BlockSpec((t, LANES), qmap(0)),
                  pl.BlockSpec((None, t, 1), lambda j, h, i: (h, jnp.maximum(i, j), 0)),
                  pl.BlockSpec((t, LANES), qmap(dy_col0))],
        out_specs=[pl.BlockSpec((t, LANES), lambda j, h, i: (j, h)), pl.BlockSpec((t, LANES), lambda j, h, i: (j, h)),
                   pl.BlockSpec((t, LANES), lambda j, h, i: (j, 0))],
        out_shape=[jax.ShapeDtypeStruct((S, H * LANES), BF16), jax.ShapeDtypeStruct((S, H * LANES), BF16),
                   jax.ShapeDtypeStruct((S, LANES), F32)],
        scratch_shapes=[pltpu.VMEM((t, LANES), F32), pltpu.VMEM((t, LANES), F32)],
        dims=("arbitrary", "arbitrary", "arbitrary"))(qr, kv, kv, kpe, o, lse, dy)
    return jnp.concatenate([dkn, dv], axis=1), dkpe


def loss_head(h, target, g, *, tr=256):
    S, D = h.shape
    tr = min(tr, S)

    def body(h_ref, t_ref, g_ref, loss_ref, dh_ref, dg_ref):
        tgt = t_ref[...]

        def f(hh, gg):
            err = jnp.square(_rms(hh, gg) - tgt)
            per_row = jnp.sum(err, axis=-1, keepdims=True) * (0.5 / D)
            return jnp.sum(per_row, axis=0, keepdims=True)

        val, vjp = jax.vjp(f, h_ref[...], g_ref[...])
        dh, dg = vjp(jnp.ones((1, 1), F32))
        dh_ref[...] = dh

        @pl.when(pl.program_id(0) == 0)
        def _():
            loss_ref[...] = jnp.zeros_like(loss_ref)
            dg_ref[...] = jnp.zeros_like(dg_ref)

        loss_ref[...] += jnp.broadcast_to(val, loss_ref.shape)
        dg_ref[...] += dg

    return _pcall(body, name="loss_head", grid=(S // tr,),
                  in_specs=[_row_spec(tr, D, 0), _row_spec(tr, D, 0), _full_spec(g)],
                  out_specs=[pl.BlockSpec((1, LANES), lambda i: (0, 0)), _row_spec(tr, D, 0), _full_spec(g)],
                  out_shape=[jax.ShapeDtypeStruct((1, LANES), F32), jax.ShapeDtypeStruct((S, D), F32),
                             jax.ShapeDtypeStruct(g.shape, F32)],
                  dims=("arbitrary",))(h, target, g)


def _rope_tables(positions, dim):
    inv_freq = ROPE_THETA ** (-jnp.arange(0, dim, 2, dtype=F32) / dim)
    ang = positions.astype(F32)[:, None] * inv_freq
    return jnp.cos(ang), jnp.sin(ang)


def _pad_cols(w, n):
    return jnp.pad(w, ((0, 0), (0, n - w.shape[1])))


def _prep_w_in0(w):
    return jnp.concatenate([w[:, :4096], w[:, 4112:5136], _pad_cols(w[:, 5136:5200], LANES),
                            _pad_cols(w[:, 4096:4112], LANES)], axis=1)


def _unprep_w_in0(g):
    return jnp.concatenate([g[:, :4096], g[:, ZIN_AB:ZIN_AB + 16], g[:, ZIN_CQ:ZIN_KR], g[:, ZIN_KR:ZIN_KR + MLA_ROPE]],
                           axis=1)


def _prep_w_uq(w):
    w = w.reshape(MLA_Q_RANK, MLA_HEADS, MLA_NOPE + MLA_ROPE)
    w = jnp.pad(w, ((0, 0), (0, 0), (0, 2 * LANES - MLA_NOPE - MLA_ROPE)))
    return w.reshape(MLA_Q_RANK, MLA_HEADS * 2 * LANES)


def _unprep_w_uq(g):
    g = g.reshape(MLA_Q_RANK, MLA_HEADS, 2 * LANES)[:, :, :MLA_NOPE + MLA_ROPE]
    return g.reshape(MLA_Q_RANK, MLA_HEADS * (MLA_NOPE + MLA_ROPE))


def _prep_w_ukv(w):
    w = w.reshape(MLA_KV_RANK, MLA_HEADS, 2, LANES)
    return jnp.transpose(w, (0, 2, 1, 3)).reshape(MLA_KV_RANK, 2 * MLA_HEADS * LANES)


def _unprep_w_ukv(g):
    g = g.reshape(MLA_KV_RANK, 2, MLA_HEADS, LANES)
    return jnp.transpose(g, (0, 2, 1, 3)).reshape(MLA_KV_RANK, 2 * MLA_HEADS * LANES)


def _row(v, n=None):
    v = v.reshape(1, -1).astype(F32)
    return v if n is None else _pad_cols(v, n)


def _ffn_fwd(h, norm_g, w_up, conv_w, conv_b, w_down, tag):
    (hn,) = rowwise(_fn_rms, [(h, D_MODEL, 0)], [norm_g], [], [(D_MODEL, BF16)], name=f"{tag}_ffn_norm")
    u = matmul(hn, w_up, "nn", F32, b_shards=N_CHIPS, tn=1408, name=f"{tag}_ffn_up")
    f = ffn_conv_fwd(u, conv_w, conv_b)
    h_out = matmul(f, w_down, "nn", F32, add=h, name=f"{tag}_ffn_down")
    return h_out, (hn, u, f)


def _ffn_bwd(dh, h, norm_g, w_up, conv_w, conv_b, w_down, saved, tag):
    hn, u, f = saved
    df = matmul(dh, w_down, "nt", BF16, name=f"{tag}_ffn_down_dx")
    g_down = matmul(f, dh, "tn", BF16, name=f"{tag}_ffn_down_dw")
    dc, g_conv_w, g_conv_b = ffn_conv_bwd(u, conv_w, conv_b, df)
    du = conv_transpose(dc, conv_w, BF16, name=f"{tag}_ffn_conv_dx")
    g_up = matmul(hn, du, "tn", BF16, out_shards=N_CHIPS, tn=1408, name=f"{tag}_ffn_up_dw")
    dhn = matmul(du, w_up, "nt", F32, b_shards=N_CHIPS, name=f"{tag}_ffn_up_dx")
    (dh_in,), (g_norm,) = rowwise_bwd(_fn_rms, [(h, D_MODEL, 0)], [norm_g], [], [(dhn, D_MODEL, 0)], [F32],
                                      adds=[(dh, D_MODEL, 0)], name=f"{tag}_ffn_norm_bwd")
    return dh_in, dict(ffn_norm=g_norm, ffn_w_up=g_up, ffn_conv_w=g_conv_w, ffn_conv_b=g_conv_b, ffn_w_down=g_down)


def _ple_fwd(h, p_i, w_proj, gate_g, w_gate, tag):
    (hg,) = rowwise(_fn_rms, [(h, D_MODEL, 0)], [gate_g], [], [(D_MODEL, BF16)], name=f"{tag}_ple_norm")
    gl = matmul(hg, w_gate, "nn", F32, name=f"{tag}_ple_gate")
    pp = matmul(p_i, w_proj, "nn", F32, b_shards=N_CHIPS, name=f"{tag}_ple_proj")
    (h_out,) = rowwise(_fn_ple, [(h, D_MODEL, 0), (pp, D_MODEL, 0), (gl, D_MODEL, 0)], [], [], [(D_MODEL, F32)],
                       name=f"{tag}_ple_add")
    return h_out, (hg, gl, pp)


def _ple_bwd(dh, h, p_i, w_proj, gate_g, w_gate, saved, tag):
    hg, gl, pp = saved
    (dpp, dgl), _ = rowwise_bwd(_fn_ple_terms, [(pp, D_MODEL, 0), (gl, D_MODEL, 0)], [], [], [(dh, D_MODEL, 0)],
                                [BF16, BF16], name=f"{tag}_ple_add_bwd")
    g_proj = matmul(p_i, dpp, "tn", BF16, out_shards=N_CHIPS, name=f"{tag}_ple_proj_dw")
    g_gate = matmul(hg, dgl, "tn", BF16, name=f"{tag}_ple_gate_dw")
    dhg = matmul(dgl, w_gate, "nt", F32, name=f"{tag}_ple_gate_dx")
    (dh_in,), (g_norm,) = rowwise_bwd(_fn_rms, [(h, D_MODEL, 0)], [gate_g], [], [(dhg, D_MODEL, 0)], [F32],
                                      adds=[(dh, D_MODEL, 0)], name=f"{tag}_ple_norm_bwd")
    return dh_in, dict(ple_proj=g_proj, ple_gate_norm=g_norm, ple_gate=g_gate)


def local_step(x, p, positions, target, W):
    S = x.shape[0]
    G = {}
    p0, p1 = p[0].astype(BF16), p[1].astype(BF16)

    cm, sm = _rope_tables(positions, MLA_ROPE)
    zeros = jnp.zeros((S, LANES - MLA_ROPE), F32)
    cosp = jnp.concatenate([cm, cm, zeros], axis=1)
    sinp = jnp.concatenate([sm, sm, zeros], axis=1)
    cr, sr = _rope_tables(positions, RET_DK)

    w_in0 = _prep_w_in0(W["l0_w_in"])
    w_uq = _prep_w_uq(W["l0_mla_w_uq"])
    w_ukv = _prep_w_ukv(W["l0_mla_w_ukv"])
    a_row = _row(W["l0_gdn_A_log"], LANES)
    dt_row = _row(W["l0_gdn_dt_bias"], LANES)
    gdn_nw = _row(W["l0_gdn_norm"])
    n = {k: _row(W[k]) for k in ("l0_attn_norm", "l0_mla_q_norm", "l0_mla_kv_norm", "l0_ffn_norm",
                                 "l0_ple_gate_norm", "l1_attn_norm", "l1_ret_norm", "l1_ffn_norm",
                                 "l1_ple_gate_norm", "final_norm", "l0_ffn_conv_b", "l1_ffn_conv_b")}

    (hn0,) = rowwise(_fn_rms, [(x, D_MODEL, 0)], [n["l0_attn_norm"]], [], [(D_MODEL, BF16)], name="l0_attn_norm")
    zin = matmul(hn0, w_in0, "nn", F32, name="l0_w_in", tn=768)
    qkv = gdn_conv_fwd(zin, W["l0_gdn_conv"])
    y_a, gdn_states = gdn_fwd(qkv, zin, a_row, dt_row, gdn_nw)
    mla_rows = [(zin, MLA_Q_RANK, ZIN_CQ // MLA_Q_RANK), (zin, MLA_KV_RANK, ZIN_CKV // MLA_KV_RANK),
                (zin, LANES, ZIN_KR // LANES)]
    mla_nd = [(cosp, LANES, 0), (sinp, LANES, 0)]
    cqn, ckvn, kpe = rowwise(_fn_mla_pre, mla_rows, [n["l0_mla_q_norm"], n["l0_mla_kv_norm"]], mla_nd,
                             [(MLA_Q_RANK, BF16), (MLA_KV_RANK, BF16), (LANES, BF16)], name="mla_pre")
    q_lin = matmul(cqn, w_uq, "nn", F32, name="mla_w_uq")
    kv = matmul(ckvn, w_ukv, "nn", BF16, name="mla_w_ukv")
    (qr,) = rowwise(_fn_rope_q, [(q_lin, 2048, 0)], [], mla_nd, [(2048, BF16)],
                    name="mla_rope_q")
    y_b, lse = flash_fwd(qr, kv, kpe)
    y_ab = jnp.concatenate([y_a, y_b], axis=1)
    h1 = matmul(y_ab, W["l0_w_out"], "nn", F32, add=x, name="l0_w_out")
    h2, ffn0 = _ffn_fwd(h1, n["l0_ffn_norm"], W["l0_ffn_w_up"], W["l0_ffn_conv_w"], n["l0_ffn_conv_b"],
                        W["l0_ffn_w_down"], "l0")
    h3, ple0 = _ple_fwd(h2, p0, W["l0_ple_proj"], n["l0_ple_gate_norm"], W["l0_ple_gate"], "l0")

    (hn1,) = rowwise(_fn_rms, [(h3, D_MODEL, 0)], [n["l1_attn_norm"]], [], [(D_MODEL, BF16)], name="l1_attn_norm")
    zz = matmul(hn1, W["l1_w_in"], "nn", F32, b_shards=N_CHIPS, tn=1536, name="l1_w_in")
    o_ret, ret_states = ret_fwd(zz, cr, sr)
    gate_rows = [(zz, 4096, 2), (o_ret, 4096, 0)]
    (yg,) = rowwise(_fn_ret_gate, gate_rows, [n["l1_ret_norm"]], [], [(4096, BF16)], name="ret_gate")
    h4 = matmul(yg, W["l1_w_out"], "nn", F32, add=h3, name="l1_w_out")
    h5, ffn1 = _ffn_fwd(h4, n["l1_ffn_norm"], W["l1_ffn_w_up"], W["l1_ffn_conv_w"], n["l1_ffn_conv_b"],
                        W["l1_ffn_w_down"], "l1")
    h6, ple1 = _ple_fwd(h5, p1, W["l1_ple_proj"], n["l1_ple_gate_norm"], W["l1_ple_gate"], "l1")

    loss_vec, dh, G["final_norm"] = loss_head(h6, target, n["final_norm"])

    dh, g = _ple_bwd(dh, h5, p1, W["l1_ple_proj"], n["l1_ple_gate_norm"], W["l1_ple_gate"], ple1, "l1")
    G.update({"l1_" + k: v for k, v in g.items()})
    dh, g = _ffn_bwd(dh, h4, n["l1_ffn_norm"], W["l1_ffn_w_up"], W["l1_ffn_conv_w"], n["l1_ffn_conv_b"],
                     W["l1_ffn_w_down"], ffn1, "l1")
    G.update({"l1_" + k: v for k, v in g.items()})

    dyg = matmul(dh, W["l1_w_out"], "nt", F32, name="l1_w_out_dx")
    G["l1_w_out"] = matmul(yg, dh, "tn", BF16, name="l1_w_out_dw")
    (dg, do_ret), (G["l1_ret_norm"],) = rowwise_bwd(_fn_ret_gate, gate_rows, [n["l1_ret_norm"]], [],
                                                   [(dyg, 4096, 0)], [BF16, F32], name="ret_gate_bwd")
    dq, dk, dv = ret_bwd(zz, cr, sr, ret_states, do_ret)
    dzz = jnp.concatenate([dq, dk, dv, dg], axis=1)
    G["l1_w_in"] = matmul(hn1, dzz, "tn", BF16, out_shards=N_CHIPS, tn=1536, name="l1_w_in_dw")
    dhn = matmul(dzz, W["l1_w_in"], "nt", F32, b_shards=N_CHIPS, name="l1_w_in_dx")
    (dh,), (G["l1_attn_norm"],) = rowwise_bwd(_fn_rms, [(h3, D_MODEL, 0)], [n["l1_attn_norm"]], [],
                                             [(dhn, D_MODEL, 0)], [F32], adds=[(dh, D_MODEL, 0)],
                                             name="l1_attn_norm_bwd")

    dh, g = _ple_bwd(dh, h2, p0, W["l0_ple_proj"], n["l0_ple_gate_norm"], W["l0_ple_gate"], ple0, "l0")
    G.update({"l0_" + k: v for k, v in g.items()})
    dh, g = _ffn_bwd(dh, h1, n["l0_ffn_norm"], W["l0_ffn_w_up"], W["l0_ffn_conv_w"], n["l0_ffn_conv_b"],
                     W["l0_ffn_w_down"], ffn0, "l0")
    G.update({"l0_" + k: v for k, v in g.items()})

    dy_ab = matmul(dh, W["l0_w_out"], "nt", F32, name="l0_w_out_dx")
    G["l0_w_out"] = matmul(y_ab, dh, "tn", BF16, name="l0_w_out_dw")
    dq, dk, dv, dz, dab, g_a, g_dt, G["l0_gdn_norm"] = gdn_bwd(qkv, zin, a_row, dt_row, gdn_nw, gdn_states, dy_ab, 0)
    G["l0_gdn_A_log"], G["l0_gdn_dt_bias"] = g_a[:, :GDN_HEADS], g_dt[:, :GDN_HEADS]
    dpre, G["l0_gdn_conv"] = gdn_conv_bwd(zin, W["l0_gdn_conv"], jnp.concatenate([dq, dk, dv], axis=1))
    dqkv = conv_transpose(dpre, W["l0_gdn_conv"], BF16, name="gdn_conv_dx")
    dqr = flash_bwd_dq(qr, kv, kpe, y_b, lse, dy_ab, MLA_HEADS)
    dkv, dkpe = flash_bwd_dkv(qr, kv, kpe, y_b, lse, dy_ab, MLA_HEADS)
    (dq_lin,), _ = rowwise_bwd(_fn_rope_q, [(q_lin, 2048, 0)], [], mla_nd, [(dqr, 2048, 0)], [BF16],
                               name="mla_rope_q_bwd")
    G["l0_mla_w_uq"] = _unprep_w_uq(matmul(cqn, dq_lin, "tn", BF16, name="mla_w_uq_dw"))
    dcqn = matmul(dq_lin, w_uq, "nt", F32, name="mla_w_uq_dx")
    G["l0_mla_w_ukv"] = _unprep_w_ukv(matmul(ckvn, dkv, "tn", BF16, name="mla_w_ukv_dw"))
    dckvn = matmul(dkv, w_ukv, "nt", F32, name="mla_w_ukv_dx")
    (dcq, dckv, dkr), (G["l0_mla_q_norm"], G["l0_mla_kv_norm"]) = rowwise_bwd(
        _fn_mla_pre, mla_rows, [n["l0_mla_q_norm"], n["l0_mla_kv_norm"]], mla_nd,
        [(dcqn, MLA_Q_RANK, 0), (dckvn, MLA_KV_RANK, 0), (dkpe, LANES, 0)], [BF16, BF16, BF16], name="mla_pre_bwd")
    dzin = jnp.concatenate([dqkv, dz, dcq, dckv, dkr, dab.astype(BF16)], axis=1)
    G["l0_w_in"] = _unprep_w_in0(matmul(hn0, dzin, "tn", BF16, name="l0_w_in_dw", tn=768))
    dhn = matmul(dzin, w_in0, "nt", F32, name="l0_w_in_dx")
    (grad_x,), (G["l0_attn_norm"],) = rowwise_bwd(_fn_rms, [(x, D_MODEL, 0)], [n["l0_attn_norm"]], [],
                                                 [(dhn, D_MODEL, 0)], [F32], adds=[(dh, D_MODEL, 0)],
                                                 name="l0_attn_norm_bwd")
    return loss_vec[0, 0], grad_x, G


HBM = pl.BlockSpec(memory_space=pltpu.HBM)
VMEM = pl.BlockSpec(memory_space=pltpu.VMEM)


def _place():
    return lax.axis_index("x"), lax.axis_index("y"), lax.axis_index("c")


def _other_chips(x, y):
    return [(1 - x, y), (x, 1 - y), (1 - x, 1 - y)]


def _comm_call(body, *, name, out_shape, in_specs, out_specs, scratch_shapes):
    return pl.pallas_call(body, name=name, out_shape=out_shape, in_specs=in_specs, out_specs=out_specs,
                          scratch_shapes=list(scratch_shapes),
                          compiler_params=pltpu.CompilerParams(vmem_limit_bytes=VMEM_LIMIT_MB << 20))


def _inplace_comm_call(body, bufs, *, name, n_sems):
    n = len(bufs)
    return pl.pallas_call(body, name=name, out_shape=[jax.ShapeDtypeStruct(b.shape, b.dtype) for b in bufs],
                          in_specs=[HBM] * n, out_specs=[HBM] * n, input_output_aliases={i: i for i in range(n)},
                          scratch_shapes=[pltpu.SemaphoreType.DMA((n_sems,)), pltpu.SemaphoreType.DMA((n_sems,))],
                          compiler_params=pltpu.CompilerParams(vmem_limit_bytes=VMEM_LIMIT_MB << 20))(*bufs)


def all_gather_chips(bufs):
    n = len(bufs)

    def body(*refs):
        outs, send_sems, recv_sems = refs[n:2 * n], refs[2 * n], refs[2 * n + 1]
        x, y, c = _place()
        me, sibling = (x, y, c), (x, y, 1 - c)
        chips = _other_chips(x, y)

        def copy(w, k, chip, hc, to):
            half = outs[w].shape[1] // 2
            rows = outs[w].at[2 * chip[0] + chip[1], pl.ds(hc * half, half), :]
            return pltpu.make_async_remote_copy(src_ref=rows, dst_ref=rows, send_sem=send_sems.at[6 * w + k],
                                                recv_sem=recv_sems.at[6 * w + k], device_id=to, device_id_type=MESH)

        first = [[copy(w, k, (x, y), c, (*chip, c)) for k, chip in enumerate(chips)] for w in range(n)]
        passed = [[copy(w, 3 + k, chip, c, sibling) for k, chip in enumerate(chips)] for w in range(n)]
        for w in range(n):
            for cp in first[w]:
                cp.start()
        for w in range(n):
            for k, chip in enumerate(chips):
                copy(w, k, chip, c, me).wait_recv()
                passed[w][k].start()
        for w in range(n):
            for k, chip in enumerate(chips):
                copy(w, 3 + k, chip, 1 - c, me).wait_recv()
        for w in range(n):
            for cp in first[w] + passed[w]:
                cp.wait_send()

    return _inplace_comm_call(body, bufs, name="all_gather_chips", n_sems=6 * n)


def pair_swap_halves(gs):
    n = len(gs)

    def body(*refs):
        g_refs, o_refs, send_sems, recv_sems = refs[:n], refs[n:2 * n], refs[2 * n], refs[2 * n + 1]
        x, y, c = _place()
        copies = []
        for w in range(n):
            half = g_refs[w].shape[1] // 2
            copies.append(pltpu.make_async_remote_copy(
                src_ref=g_refs[w].at[:, pl.ds((1 - c) * half, half), :], dst_ref=o_refs[w], send_sem=send_sems.at[w],
                recv_sem=recv_sems.at[w], device_id=(x, y, 1 - c), device_id_type=MESH))
        for cp in copies:
            cp.start()
        for cp in copies:
            cp.wait()

    return _comm_call(body, name="pair_swap_halves",
                      out_shape=[jax.ShapeDtypeStruct((N_CHIPS, g.shape[1] // 2, g.shape[2]), g.dtype) for g in gs],
                      in_specs=[HBM] * n, out_specs=[HBM] * n,
                      scratch_shapes=[pltpu.SemaphoreType.DMA((n,)), pltpu.SemaphoreType.DMA((n,))])(*gs)


def scatter_chips(ps):
    n = len(ps)

    def body(*refs):
        p_refs, o_refs, send_sems, recv_sems = refs[:n], refs[n:2 * n], refs[2 * n], refs[2 * n + 1]
        x, y, c = _place()
        copies = [pltpu.make_async_remote_copy(src_ref=p_refs[w].at[2 * chip[0] + chip[1]], dst_ref=o_refs[w].at[k],
                                               send_sem=send_sems.at[3 * w + k], recv_sem=recv_sems.at[3 * w + k],
                                               device_id=(*chip, c), device_id_type=MESH)
                  for w in range(n) for k, chip in enumerate(_other_chips(x, y))]
        for cp in copies:
            cp.start()
        for cp in copies:
            cp.wait()

    return _comm_call(body, name="scatter_chips",
                      out_shape=[jax.ShapeDtypeStruct((3,) + p.shape[1:], p.dtype) for p in ps],
                      in_specs=[HBM] * n, out_specs=[HBM] * n,
                      scratch_shapes=[pltpu.SemaphoreType.DMA((3 * n,)), pltpu.SemaphoreType.DMA((3 * n,))])(*ps)


def pair_join_halves(rs):
    n = len(rs)

    def body(*refs):
        outs, send_sems, recv_sems = refs[n:2 * n], refs[2 * n], refs[2 * n + 1]
        x, y, c = _place()
        copies = []
        for w in range(n):
            half = outs[w].shape[0] // 2
            rows = outs[w].at[pl.ds(c * half, half), :]
            copies.append(pltpu.make_async_remote_copy(src_ref=rows, dst_ref=rows, send_sem=send_sems.at[w],
                                                       recv_sem=recv_sems.at[w], device_id=(x, y, 1 - c),
                                                       device_id_type=MESH))
        for cp in copies:
            cp.start()
        for cp in copies:
            cp.wait()

    return _inplace_comm_call(body, rs, name="pair_join_halves", n_sems=n)


def all_reduce_small(v, name):
    n, L = v.shape
    n_dev = 8

    def body(v_ref, out_ref, buf, send_sems, recv_sems):
        x, y, c = _place()
        me = 4 * x + 2 * y + c
        buf[me] = v_ref[...]

        def copy(k, slot, peer):
            return pltpu.make_async_remote_copy(src_ref=v_ref, dst_ref=buf.at[slot], send_sem=send_sems.at[k],
                                                recv_sem=recv_sems.at[slot],
                                                device_id=(peer // 4, (peer // 2) % 2, peer % 2), device_id_type=MESH)

        sends = [copy(k - 1, me, (me + k) % n_dev) for k in range(1, n_dev)]
        for cp in sends:
            cp.start()
        for k in range(1, n_dev):
            src = (me + k) % n_dev
            copy(0, src, src).wait_recv()
        for cp in sends:
            cp.wait_send()
        acc = buf[0]
        for s in range(1, n_dev):
            acc = acc + buf[s]
        out_ref[...] = acc

    return _comm_call(body, name=name, out_shape=jax.ShapeDtypeStruct((n, L), v.dtype), in_specs=[VMEM], out_specs=VMEM,
                      scratch_shapes=[pltpu.VMEM((n_dev, n, L), v.dtype), pltpu.SemaphoreType.DMA((n_dev - 1,)),
                                      pltpu.SemaphoreType.DMA((n_dev,))])(v)


BF16_ROWS = 16


def _rows_tile(n, row_bytes, budget=1 << 20, mult=SUBLANES):
    best = mult if n % mult == 0 else n
    for t in range(mult, n + 1, mult):
        if n % t == 0 and t * row_bytes <= budget:
            best = t
    return best


def _scalars(*vals):
    return jnp.stack([jnp.asarray(v, jnp.int32) for v in vals])


def cast_to_slot(w, chip, name):
    r, c = w.shape
    tb = _rows_tile(r, c * 4, mult=BF16_ROWS)

    def body(s_ref, w_ref, o_ref):
        o_ref[...] = w_ref[...].astype(BF16)

    spec = pltpu.PrefetchScalarGridSpec(
        num_scalar_prefetch=1, grid=(r // tb,), in_specs=[pl.BlockSpec((tb, c), lambda i, s: (i, 0))],
        out_specs=pl.BlockSpec((None, tb, c), lambda i, s: (s[0], i, 0)))
    return pl.pallas_call(body, name=name, grid_spec=spec, out_shape=jax.ShapeDtypeStruct((N_CHIPS, r, c), BF16),
                          compiler_params=pltpu.CompilerParams(dimension_semantics=("parallel",)))(_scalars(chip), w)


def pair_add(g, got, c, name):
    _, r, w = g.shape
    half = r // 2
    tb = _rows_tile(half, w * 4, mult=BF16_ROWS)
    nb = half // tb

    def body(c_ref, g_ref, got_ref, o_ref):
        o_ref[...] = (g_ref[...].astype(F32) + got_ref[...].astype(F32)).astype(o_ref.dtype)

    spec = pltpu.PrefetchScalarGridSpec(
        num_scalar_prefetch=1, grid=(N_CHIPS, nb),
        in_specs=[pl.BlockSpec((None, tb, w), lambda s, i, c_ref: (s, c_ref[0] * nb + i, 0)),
                  pl.BlockSpec((None, tb, w), lambda s, i, c_ref: (s, i, 0))],
        out_specs=pl.BlockSpec((None, tb, w), lambda s, i, c_ref: (s, i, 0)))
    return pl.pallas_call(body, name=name, grid_spec=spec, out_shape=jax.ShapeDtypeStruct((N_CHIPS, half, w), BF16),
                          compiler_params=pltpu.CompilerParams(dimension_semantics=("parallel", "parallel")))(
        _scalars(c), g, got)


def chip_add(p, got, chip, c, name):
    _, h, w = p.shape
    tb = _rows_tile(h, w * 4, mult=BF16_ROWS)
    nb = h // tb

    def body(s_ref, p_ref, got_ref, o_ref):
        acc = p_ref[...].astype(F32)
        for k in range(3):
            acc = acc + got_ref[k].astype(F32)
        o_ref[...] = acc

    spec = pltpu.PrefetchScalarGridSpec(
        num_scalar_prefetch=1, grid=(nb,),
        in_specs=[pl.BlockSpec((None, tb, w), lambda i, s: (s[0], i, 0)),
                  pl.BlockSpec((3, tb, w), lambda i, s: (0, i, 0))],
        out_specs=pl.BlockSpec((tb, w), lambda i, s: (s[1] * nb + i, 0)))
    return pl.pallas_call(body, name=name, grid_spec=spec, out_shape=jax.ShapeDtypeStruct((2 * h, w), F32),
                          compiler_params=pltpu.CompilerParams(dimension_semantics=("parallel",)))(
        _scalars(chip, c), p, got)


def adamw(w, g, m, v, name):
    r, c = w.shape
    tr = _rows_tile(r, c * 4)

    def body(w_ref, g_ref, m_ref, v_ref, d_ref, m_out, v_out):
        gg = g_ref[...]
        m2 = ADAM_B1 * m_ref[...] + (1.0 - ADAM_B1) * gg
        v2 = ADAM_B2 * v_ref[...] + (1.0 - ADAM_B2) * jnp.square(gg)
        m_hat = m2 / (1.0 - ADAM_B1 ** ADAM_STEP)
        v_hat = v2 / (1.0 - ADAM_B2 ** ADAM_STEP)
        d_ref[...] = -ADAM_LR * (m_hat / (jnp.sqrt(v_hat) + ADAM_EPS) + ADAM_WD * w_ref[...])
        m_out[...] = m2
        v_out[...] = v2

    blk = pl.BlockSpec((tr, c), lambda i: (i, 0))
    return _pcall(body, name=name, grid=(r // tr,), in_specs=[blk] * 4, out_specs=[blk] * 3,
                  out_shape=[jax.ShapeDtypeStruct((r, c), F32)] * 3, dims=("parallel",))(w, g, m, v)


WEIGHTS = ["l0_attn_norm", "l0_w_in", "l0_gdn_conv", "l0_gdn_A_log", "l0_gdn_dt_bias", "l0_gdn_norm", "l0_mla_q_norm",
           "l0_mla_w_uq", "l0_mla_kv_norm", "l0_mla_w_ukv", "l0_w_out", "l0_ffn_norm", "l0_ffn_w_up", "l0_ffn_conv_w",
           "l0_ffn_conv_b", "l0_ffn_w_down", "l0_ple_proj", "l0_ple_gate_norm", "l0_ple_gate", "l1_attn_norm",
           "l1_w_in", "l1_ret_norm", "l1_w_out", "l1_ffn_norm", "l1_ffn_w_up", "l1_ffn_conv_w", "l1_ffn_conv_b",
           "l1_ffn_w_down", "l1_ple_proj", "l1_ple_gate_norm", "l1_ple_gate", "final_norm"]
COL_SHARDED = ["l0_w_in", "l0_mla_w_uq", "l0_mla_w_ukv", "l0_ffn_w_up", "l0_ple_proj", "l1_w_in", "l1_ffn_w_up",
               "l1_ple_proj"]
ROW_SHARDED = ["l0_w_out", "l0_ffn_w_down", "l0_ple_gate", "l1_w_out", "l1_ffn_w_down", "l1_ple_gate"]
BIG = [k for k in WEIGHTS if k in COL_SHARDED or k in ROW_SHARDED]
SMALL_SHARDED = ["l0_gdn_conv", "l0_ffn_conv_w", "l1_ffn_conv_w"]
SMALL = [k for k in WEIGHTS if k not in BIG]
KEPT_SHARDED = ["l0_ffn_w_up", "l0_ple_proj", "l1_w_in", "l1_ffn_w_up", "l1_ple_proj"]
RELAID = [k for k in COL_SHARDED if k not in KEPT_SHARDED]


def _cols_to_full(s):
    j, k, n = s.shape
    return jnp.transpose(s, (1, 0, 2)).reshape(k, j * n)


def _full_to_cols(g):
    k, n4 = g.shape
    return jnp.transpose(g.reshape(k, N_CHIPS, n4 // N_CHIPS), (1, 0, 2))


def _pack_small(vals):
    flat = jnp.concatenate([v.astype(F32).reshape(-1) for v in vals])
    align = SUBLANES * LANES
    flat = jnp.pad(flat, (0, -flat.shape[0] % align))
    return flat.reshape(-1, LANES)


def _unpack_small(rows, shapes):
    flat = rows.reshape(-1)
    out, off = [], 0
    for shp in shapes:
        n = int(np.prod(shp))
        out.append(flat[off:off + n].reshape(shp))
        off += n
    return out


INPUTS = (["x", "p", "positions"] + WEIGHTS + ["loss_target"] + ["m_" + k for k in WEIGHTS]
          + ["v_" + k for k in WEIGHTS])


def kernel(
        x, p, positions, l0_attn_norm, l0_w_in, l0_gdn_conv, l0_gdn_A_log, l0_gdn_dt_bias, l0_gdn_norm, l0_mla_q_norm,
        l0_mla_w_uq, l0_mla_kv_norm, l0_mla_w_ukv, l0_w_out, l0_ffn_norm, l0_ffn_w_up, l0_ffn_conv_w, l0_ffn_conv_b,
        l0_ffn_w_down, l0_ple_proj, l0_ple_gate_norm, l0_ple_gate, l1_attn_norm, l1_w_in, l1_ret_norm, l1_w_out,
        l1_ffn_norm, l1_ffn_w_up, l1_ffn_conv_w, l1_ffn_conv_b, l1_ffn_w_down, l1_ple_proj, l1_ple_gate_norm,
        l1_ple_gate, final_norm, loss_target, m_l0_attn_norm, m_l0_w_in, m_l0_gdn_conv, m_l0_gdn_A_log,
        m_l0_gdn_dt_bias, m_l0_gdn_norm, m_l0_mla_q_norm, m_l0_mla_w_uq, m_l0_mla_kv_norm, m_l0_mla_w_ukv, m_l0_w_out,
        m_l0_ffn_norm, m_l0_ffn_w_up, m_l0_ffn_conv_w, m_l0_ffn_conv_b, m_l0_ffn_w_down, m_l0_ple_proj,
        m_l0_ple_gate_norm, m_l0_ple_gate, m_l1_attn_norm, m_l1_w_in, m_l1_ret_norm, m_l1_w_out, m_l1_ffn_norm,
        m_l1_ffn_w_up, m_l1_ffn_conv_w, m_l1_ffn_conv_b, m_l1_ffn_w_down, m_l1_ple_proj, m_l1_ple_gate_norm,
        m_l1_ple_gate, m_final_norm, v_l0_attn_norm, v_l0_w_in, v_l0_gdn_conv, v_l0_gdn_A_log, v_l0_gdn_dt_bias,
        v_l0_gdn_norm, v_l0_mla_q_norm, v_l0_mla_w_uq, v_l0_mla_kv_norm, v_l0_mla_w_ukv, v_l0_w_out, v_l0_ffn_norm,
        v_l0_ffn_w_up, v_l0_ffn_conv_w, v_l0_ffn_conv_b, v_l0_ffn_w_down, v_l0_ple_proj, v_l0_ple_gate_norm,
        v_l0_ple_gate, v_l1_attn_norm, v_l1_w_in, v_l1_ret_norm, v_l1_w_out, v_l1_ffn_norm, v_l1_ffn_w_up,
        v_l1_ffn_conv_w, v_l1_ffn_conv_b, v_l1_ffn_w_down, v_l1_ple_proj, v_l1_ple_gate_norm, v_l1_ple_gate,
        v_final_norm):
    given = locals()
    a = {k: given[k] for k in INPUTS}
    x_i, y_i, c_i = _place()
    chip = 2 * x_i + y_i
    shard_shapes = {k: a[k].shape for k in WEIGHTS}

    gathered = dict(zip(BIG, all_gather_chips([cast_to_slot(a[k], chip, "cast_" + k) for k in BIG])))
    W = {}
    for k in BIG:
        r, c = shard_shapes[k]
        if k in ROW_SHARDED:
            W[k] = gathered[k].reshape(N_CHIPS * r, c)
        elif k in KEPT_SHARDED:
            W[k] = gathered[k]
        else:
            W[k] = _cols_to_full(gathered[k])
    placed = []
    for k in SMALL_SHARDED:
        r, c = shard_shapes[k]
        mine = jnp.where(c_i == 0, a[k], jnp.zeros_like(a[k]))
        placed.append(lax.dynamic_update_slice(jnp.zeros((r, N_CHIPS * c), F32), mine, (0, chip * c)))
    full_small = _unpack_small(all_reduce_small(_pack_small(placed), "gather_small_weights"),
                               [p_.shape for p_ in placed])
    for k in SMALL:
        W[k] = a[k]
    W.update(dict(zip(SMALL_SHARDED, full_small)))

    loss_part, grad_x, G = local_step(a["x"][0], a["p"][:, 0], a["positions"][0], a["loss_target"][0], W)
    loss = lax.psum(loss_part, ("x", "y", "c"))

    by_chip = []
    for k in BIG:
        r, c = shard_shapes[k]
        if k in ROW_SHARDED:
            by_chip.append(G[k].reshape(N_CHIPS, r, c))
        elif k in KEPT_SHARDED:
            by_chip.append(G[k])
        else:
            by_chip.append(_full_to_cols(G[k]))
    swapped = pair_swap_halves(by_chip)
    pair_sums = [pair_add(g, got, c_i, "rs_pair_add_" + k) for k, g, got in zip(BIG, by_chip, swapped)]
    scattered = scatter_chips(pair_sums)
    halves = [chip_add(p_, got, chip, c_i, "rs_chip_add_" + k) for k, p_, got in zip(BIG, pair_sums, scattered)]
    grads = dict(zip(BIG, pair_join_halves(halves)))
    deltas, new_m, new_v = {}, {}, {}
    for k in BIG:
        deltas[k], new_m[k], new_v[k] = adamw(a[k], grads[k], a["m_" + k], a["v_" + k], "adamw_" + k)

    small_full = [G[k].reshape(-1) for k in SMALL]
    summed = _unpack_small(all_reduce_small(_pack_small(small_full), "reduce_small_grads"),
                           [G[k].shape for k in SMALL])
    for k, g in zip(SMALL, summed):
        if k in SMALL_SHARDED:
            r, c = shard_shapes[k]
            g = lax.dynamic_slice(g.reshape(r, N_CHIPS * c), (0, chip * c), (r, c))
        grads[k] = g.reshape(shard_shapes[k])
    packed = [_pack_small([d[k] for k in SMALL]) for d in (
        {k: a[k] for k in SMALL}, grads, {k: a["m_" + k] for k in SMALL}, {k: a["v_" + k] for k in SMALL})]
    outs = adamw(*packed, "adamw_small")
    shapes = [shard_shapes[k] for k in SMALL]
    for d, rows in zip((deltas, new_m, new_v), outs):
        d.update(dict(zip(SMALL, _unpack_small(rows, shapes))))

    return (loss, grad_x[None], *[grads[k] for k in WEIGHTS], *[deltas[k] for k in WEIGHTS],
            *[new_m[k] for k in WEIGHTS], *[new_v[k] for k in WEIGHTS])
```

```python
import functools
import math

import numpy as np
import jax
import jax.numpy as jnp
from jax import lax
from jax.experimental import pallas as pl
from jax.experimental.pallas import tpu as pltpu

F32, BF16 = jnp.float32, jnp.bfloat16
HI = lax.Precision.HIGHEST
MESH = pl.DeviceIdType.MESH

NORM_EPS = 1e-6
ROPE_THETA = 10000.0
D_MODEL = 2048
PLE_DIM = 256
GDN_HEADS, GDN_DK, GDN_DV, GDN_CONV = 8, 128, 128, 4
MLA_HEADS, MLA_Q_RANK, MLA_KV_RANK, MLA_NOPE, MLA_ROPE, MLA_V = 8, 512, 512, 128, 64, 128
RET_HEADS, RET_DK, RET_DV = 8, 256, 512
D_FF, FFN_CONV = 5632, 3
ADAM_LR, ADAM_B1, ADAM_B2, ADAM_EPS, ADAM_WD, ADAM_STEP = 0.001, 0.9, 0.999, 1e-08, 0.01, 10

LANES = 128
SUBLANES = 8
CHUNK = 128
N_CHIPS = 4
VMEM_LIMIT_MB = 56

ZIN_QKV, ZIN_Z, ZIN_CQ, ZIN_CKV, ZIN_KR, ZIN_AB, ZIN_W = 0, 3072, 4096, 4608, 5120, 5248, 5376


class Hosted:
    def __init__(self, inputs, out_shapes, aliases, n_sems, start, finish):
        self.inputs, self.out_shapes, self.aliases, self.n_sems = list(inputs), list(out_shapes), dict(aliases), n_sems
        self.start, self.finish = start, finish


def _pcall(body, *, name, out_shape, grid=(), in_specs=None, out_specs=None, scratch_shapes=(), dims=None,
           hosted=None):
    params = dict(vmem_limit_bytes=VMEM_LIMIT_MB << 20)
    if dims is not None:
        params["dimension_semantics"] = dims
    if hosted is None:
        return pl.pallas_call(body, name=name, out_shape=out_shape, grid=grid, in_specs=in_specs, out_specs=out_specs,
                              scratch_shapes=list(scratch_shapes), compiler_params=pltpu.CompilerParams(**params))
    single = not isinstance(out_shape, (list, tuple))
    out_shape = [out_shape] if single else list(out_shape)
    out_specs = [out_specs] if single else list(out_specs)
    n_in, n_out, n_scr = len(in_specs), len(out_shape), len(scratch_shapes)
    h_in, h_out = len(hosted.inputs), len(hosted.out_shapes)
    hbm = pl.BlockSpec(memory_space=pltpu.HBM)

    def hosting_body(*refs):
        ins, h_ins = refs[:n_in], refs[n_in:n_in + h_in]
        o0 = n_in + h_in
        outs, h_outs = refs[o0:o0 + n_out], refs[o0 + n_out:o0 + n_out + h_out]
        s0 = o0 + n_out + h_out
        scr, (send_sems, recv_sems) = refs[s0:s0 + n_scr], refs[s0 + n_scr:]
        ids = [pl.program_id(d) for d in range(len(grid))]
        first = functools.reduce(lambda u, v: u & v, [i == 0 for i in ids])
        last = functools.reduce(lambda u, v: u & v, [i == g - 1 for i, g in zip(ids, grid)])

        @pl.when(first)
        def _():
            hosted.start(h_ins, h_outs, send_sems, recv_sems)

        body(*ins, *outs, *scr)

        @pl.when(last)
        def _():
            hosted.finish(h_ins, h_outs, send_sems, recv_sems)

    params["dimension_semantics"] = ("arbitrary",) * len(grid)
    call = pl.pallas_call(
        hosting_body, name=name, out_shape=out_shape + hosted.out_shapes, grid=grid,
        in_specs=list(in_specs) + [hbm] * h_in, out_specs=out_specs + [hbm] * h_out,
        scratch_shapes=list(scratch_shapes) + [pltpu.SemaphoreType.DMA((hosted.n_sems,)),
                                               pltpu.SemaphoreType.DMA((hosted.n_sems,))],
        input_output_aliases={n_in + i: n_out + o for i, o in hosted.aliases.items()},
        compiler_params=pltpu.CompilerParams(**params))

    def run(*args):
        res = call(*args, *hosted.inputs)
        main = res[:n_out]
        return (main[0] if single else main), list(res[n_out:])

    return run


def _tile(n, target, mult=LANES):
    best = None
    for t in range(mult, min(n, target) + 1, mult):
        if n % t == 0:
            best = t
    return best or n


_DN = {"nn": (((1,), (0,)), ((), ())), "nt": (((1,), (1,)), ((), ())), "tn": (((0,), (0,)), ((), ()))}


def matmul(a, b, mode, out_dtype, *, name, add=None, b_shards=1, out_shards=1, tm=512, tn=1024, tk=2048):
    bs = b.shape[-2:]
    if mode == "nn":
        (M, K), (K2, N) = a.shape, (bs[0], bs[1] * b_shards)
    elif mode == "nt":
        (M, K), (N, K2) = a.shape, (bs[0], bs[1] * b_shards)
    else:
        (K, M), (K2, N) = a.shape, bs
    assert K == K2, (name, a.shape, b.shape)
    n_sh = N // max(b_shards if mode == "nn" else 1, out_shards)
    k_sh = K // (b_shards if mode == "nt" else 1)
    tm, tn, tk = _tile(M, tm), _tile(n_sh, tn), _tile(k_sh, tk)
    nk = K // tk
    nbn, nbk = n_sh // tn, k_sh // tk
    dn = _DN[mode]
    has_add = add is not None
    a_bytes, b_bytes = a.size * a.dtype.itemsize, b.size * b.dtype.itemsize
    i_outer = nk > 1 or a_bytes + (M // tm) * b_bytes <= b_bytes + (N // tn) * a_bytes

    def ij(g0, g1):
        return (g0, g1) if i_outer else (g1, g0)

    def body(*refs):
        a_ref, b_ref = refs[:2]
        add_ref = refs[2] if has_add else None
        o_ref = refs[3 if has_add else 2]
        part = lax.dot_general(a_ref[...].astype(BF16), b_ref[...].astype(BF16), dn, preferred_element_type=F32)

        def finish(r):
            if has_add:
                r = r + add_ref[...]
            o_ref[...] = r.astype(out_dtype)

        if nk == 1:
            finish(part)
            return
        acc = refs[-1]
        k = pl.program_id(2)

        @pl.when(k == 0)
        def _():
            acc[...] = part

        @pl.when(k > 0)
        def _():
            acc[...] += part

        @pl.when(k == nk - 1)
        def _():
            finish(acc[...])

    def spec(block, fn):
        return pl.BlockSpec(block, lambda g0, g1, k: fn(*ij(g0, g1), k))

    if mode == "tn":
        a_spec = spec((tk, tm), lambda i, j, k: (k, i))
    else:
        a_spec = spec((tm, tk), lambda i, j, k: (i, k))
    if mode == "nt":
        if b_shards > 1:
            b_spec = spec((None, tn, tk), lambda i, j, k: (k // nbk, j, k % nbk))
        else:
            b_spec = spec((tn, tk), lambda i, j, k: (j, k))
    elif b_shards > 1:
        b_spec = spec((None, tk, tn), lambda i, j, k: (j // nbn, k, j % nbn))
    else:
        b_spec = spec((tk, tn), lambda i, j, k: (k, j))
    in_specs = [a_spec, b_spec]
    args = [a, b]
    if has_add:
        in_specs.append(spec((tm, tn), lambda i, j, k: (i, j)))
        args.append(add)
    if out_shards > 1:
        out_spec = spec((None, tm, tn), lambda i, j, k: (j // nbn, i, j % nbn))
        out_shape = jax.ShapeDtypeStruct((out_shards, M, n_sh), out_dtype)
    else:
        out_spec = spec((tm, tn), lambda i, j, k: (i, j))
        out_shape = jax.ShapeDtypeStruct((M, N), out_dtype)
    gi, gj = M // tm, N // tn
    return _pcall(body, name=name, out_shape=out_shape, grid=(gi, gj, nk) if i_outer else (gj, gi, nk),
                  in_specs=in_specs, out_specs=out_spec,
                  scratch_shapes=[pltpu.VMEM((tm, tn), F32)] if nk > 1 else [],
                  dims=("parallel", "parallel", "arbitrary"))(*args)


def _row_spec(tr, w, c):
    return pl.BlockSpec((tr, w), lambda i: (i, c))


def _full_spec(arr):
    return pl.BlockSpec(arr.shape, lambda i: (0,) * arr.ndim)


def rowwise(fn, rows, params, nd_rows, outs, *, name, tr=256):
    S = rows[0][0].shape[0]
    tr = min(tr, S)
    n_in = len(rows) + len(params) + len(nd_rows)

    def body(*refs):
        res = fn(*[x[...] for x in refs[:n_in]])
        for o_ref, v in zip(refs[n_in:], res):
            o_ref[...] = v.astype(o_ref.dtype)

    return _pcall(body, name=name, grid=(S // tr,),
                  in_specs=([_row_spec(tr, w, c) for (_, w, c) in rows] + [_full_spec(q) for q in params]
                            + [_row_spec(tr, w, c) for (_, w, c) in nd_rows]),
                  out_specs=[_row_spec(tr, w, 0) for (w, _) in outs],
                  out_shape=[jax.ShapeDtypeStruct((S, w), dt) for (w, dt) in outs],
                  dims=("parallel",))(*[r[0] for r in rows], *params, *[r[0] for r in nd_rows])


def rowwise_bwd(fn, rows, params, nd_rows, cts, d_dtypes, *, name, adds=None, tr=256):
    S = rows[0][0].shape[0]
    tr = min(tr, S)
    n_r, n_p, n_n, n_c = len(rows), len(params), len(nd_rows), len(cts)
    adds = adds or [None] * n_r
    add_list = [a for a in adds if a is not None]
    n_a = len(add_list)

    def body(*refs):
        it = iter(refs)
        r = [next(it)[...] for _ in range(n_r)]
        p = [next(it)[...] for _ in range(n_p)]
        nd = [next(it)[...] for _ in range(n_n)]
        c = [next(it)[...] for _ in range(n_c)]
        ad = [next(it)[...] for _ in range(n_a)]
        d_row_refs = [next(it) for _ in range(n_r)]
        d_par_refs = [next(it) for _ in range(n_p)]
        outs, vjp = jax.vjp(lambda *dp: fn(*dp, *nd), *r, *p)
        g = vjp(tuple(ci.astype(o.dtype) for ci, o in zip(c, outs)))
        ai = 0
        for k in range(n_r):
            gk = g[k].astype(F32)
            if adds[k] is not None:
                gk = gk + ad[ai].astype(F32)
                ai += 1
            d_row_refs[k][...] = gk.astype(d_row_refs[k].dtype)

        @pl.when(pl.program_id(0) == 0)
        def _():
            for ref in d_par_refs:
                ref[...] = jnp.zeros_like(ref)

        for k in range(n_p):
            d_par_refs[k][...] += g[n_r + k].astype(F32)

    in_specs = ([_row_spec(tr, w, c) for (_, w, c) in rows] + [_full_spec(q) for q in params]
                + [_row_spec(tr, w, c) for (_, w, c) in nd_rows] + [_row_spec(tr, w, c) for (_, w, c) in cts]
                + [_row_spec(tr, w, c) for (_, w, c) in add_list])
    out_specs = [_row_spec(tr, w, 0) for (_, w, _) in rows] + [_full_spec(q) for q in params]
    out_shape = ([jax.ShapeDtypeStruct((S, w), dt) for (_, w, _), dt in zip(rows, d_dtypes)]
                 + [jax.ShapeDtypeStruct(q.shape, F32) for q in params])
    res = _pcall(body, name=name, grid=(S // tr,), in_specs=in_specs, out_specs=out_specs, out_shape=out_shape,
                 dims=("arbitrary",))(*[r[0] for r in rows], *params, *[r[0] for r in nd_rows],
                                      *[r[0] for r in cts], *[r[0] for r in add_list])
    return res[:n_r], res[n_r:]


def _rms(x, g):
    x = x.astype(F32)
    return x * lax.rsqrt(jnp.mean(x * x, axis=-1, keepdims=True) + NORM_EPS) * g


def _fn_rms(x, g):
    return (_rms(x, g),)


def _sigmoid(x):
    return 1.0 / (1.0 + jnp.exp(-x))


def _silu(x):
    return x * _sigmoid(x)


def _softplus(x):
    return jnp.maximum(x, 0.0) + jnp.log(1.0 + jnp.exp(-jnp.abs(x)))


def _fn_ple(h, pp, gl):
    return (h.astype(F32) + pp.astype(F32) * _sigmoid(gl.astype(F32)),)


def _fn_ple_terms(pp, gl):
    return (pp.astype(F32) * _sigmoid(gl.astype(F32)),)


def _rot_half_matrix():
    half = MLA_ROPE // 2
    r = lax.broadcasted_iota(jnp.int32, (LANES, LANES), 0)
    c = lax.broadcasted_iota(jnp.int32, (LANES, LANES), 1)
    plus = (c == r + half) & (r < half)
    minus = (r == c + half) & (c < half)
    return jnp.where(plus, 1.0, 0.0) - jnp.where(minus, 1.0, 0.0)


def _rope_pad(x, cosp, sinp):
    return x * cosp + jnp.dot(x, _rot_half_matrix(), precision=HI, preferred_element_type=F32) * sinp


def _fn_mla_pre(cq, ckv, kr, qn_w, kvn_w, cosp, sinp):
    return (_rms(cq, qn_w), _rms(ckv, kvn_w), _rope_pad(kr.astype(F32), cosp, sinp))


def _fn_rope_q(q, cosp, sinp):
    q = q.astype(F32)
    parts = []
    for h in range(MLA_HEADS):
        base = 2 * LANES * h
        parts.append(q[:, base:base + LANES])
        parts.append(_rope_pad(q[:, base + LANES:base + 2 * LANES], cosp, sinp))
    return (jnp.concatenate(parts, axis=1),)


def _fn_ret_gate(g, on, w):
    return (_silu(g.astype(F32)) * (on.astype(F32) * w),)


def _shift_down(cur, halo, s):
    if s == 0:
        return cur
    r = pltpu.roll(cur, s, 0)
    hs = pltpu.roll(halo, s, 0)
    row = lax.broadcasted_iota(jnp.int32, hs.shape, 0)
    first = jnp.where(row < s, hs, r[:SUBLANES])
    return jnp.concatenate([first, r[SUBLANES:]], axis=0)


def _shift_up(cur, halo, s):
    if s == 0:
        return cur
    n = cur.shape[0]
    r = pltpu.roll(cur, n - s, 0)
    hs = pltpu.roll(halo, SUBLANES - s, 0)
    row = lax.broadcasted_iota(jnp.int32, hs.shape, 0)
    last = jnp.where(row >= SUBLANES - s, hs, r[n - SUBLANES:])
    return jnp.concatenate([r[:n - SUBLANES], last], axis=0)


def _prev_halo_spec(tr, tw, col):
    return pl.BlockSpec((SUBLANES, tw), lambda c, i: (jnp.maximum(i * (tr // SUBLANES) - 1, 0), col(c)))


def _conv_taps(cur, halo, w_ref, width):
    taps = [_shift_down(cur, halo, width - 1 - j) for j in range(width)]
    y = taps[0] * w_ref[0:1, :]
    for j in range(1, width):
        y = y + taps[j] * w_ref[j:j + 1, :]
    return y, taps


def gdn_conv_fwd(zin, w, *, tr=512, tw=512):
    S = zin.shape[0]
    tr = min(tr, S)
    width, C = w.shape

    def body(cur_ref, halo_ref, w_ref, o_ref):
        i = pl.program_id(1)
        halo = halo_ref[...] * (i > 0).astype(F32)
        y, _ = _conv_taps(cur_ref[...], halo, w_ref, width)
        o_ref[...] = _silu(y)

    return _pcall(body, name="gdn_conv_fwd", grid=(C // tw, S // tr),
                  in_specs=[pl.BlockSpec((tr, tw), lambda c, i: (i, c)), _prev_halo_spec(tr, tw, lambda c: c),
                            pl.BlockSpec((width, tw), lambda c, i: (0, c))],
                  out_specs=pl.BlockSpec((tr, tw), lambda c, i: (i, c)),
                  out_shape=jax.ShapeDtypeStruct((S, C), F32), dims=("parallel", "arbitrary"))(zin, zin, w)


def gdn_conv_bwd(zin, w, dy, *, tr=512, tw=512):
    S = zin.shape[0]
    tr = min(tr, S)
    width, C = w.shape

    def body(cur_ref, halo_ref, w_ref, dy_ref, da_ref, dw_ref):
        i = pl.program_id(1)
        halo = halo_ref[...] * (i > 0).astype(F32)
        y, taps = _conv_taps(cur_ref[...], halo, w_ref, width)
        sg = _sigmoid(y)
        da = dy_ref[...] * (sg * (1.0 + y * (1.0 - sg)))
        da_ref[...] = da

        @pl.when(i == 0)
        def _():
            dw_ref[...] = jnp.zeros_like(dw_ref)

        for j in range(width):
            dw_ref[j:j + 1, :] += jnp.sum(da * taps[j], axis=0, keepdims=True)

    return _pcall(body, name="gdn_conv_bwd", grid=(C // tw, S // tr),
                  in_specs=[pl.BlockSpec((tr, tw), lambda c, i: (i, c)), _prev_halo_spec(tr, tw, lambda c: c),
                            pl.BlockSpec((width, tw), lambda c, i: (0, c)),
                            pl.BlockSpec((tr, tw), lambda c, i: (i, c))],
                  out_specs=[pl.BlockSpec((tr, tw), lambda c, i: (i, c)),
                             pl.BlockSpec((width, tw), lambda c, i: (0, c))],
                  out_shape=[jax.ShapeDtypeStruct((S, C), F32), jax.ShapeDtypeStruct((width, C), F32)],
                  dims=("parallel", "arbitrary"))(zin, zin, w, dy)


def conv_transpose(dy, w, out_dtype, *, name, tr=512, tw=512):
    if dy.ndim == 2:
        dy = dy[None]
    T, S, C = dy.shape
    tr = min(tr, S)
    width = w.shape[0]
    n_i, nc = S // tr, C // tw

    def body(cur_ref, halo_ref, w_ref, o_ref):
        i = pl.program_id(2)
        halo = halo_ref[...] * (i < n_i - 1).astype(F32)
        cur = cur_ref[...]
        acc = cur * w_ref[width - 1:width, :]
        for s in range(1, width):
            acc = acc + _shift_up(cur, halo, s) * w_ref[width - 1 - s:width - s, :]
        o_ref[...] = acc.astype(out_dtype)

    nxt = pl.BlockSpec((None, SUBLANES, tw),
                       lambda t, c, i: (t, jnp.minimum((i + 1) * (tr // SUBLANES), S // SUBLANES - 1), c))
    return _pcall(body, name=name, grid=(T, nc, n_i),
                  in_specs=[pl.BlockSpec((None, tr, tw), lambda t, c, i: (t, i, c)), nxt,
                            pl.BlockSpec((width, tw), lambda t, c, i: (0, t * nc + c))],
                  out_specs=pl.BlockSpec((tr, tw), lambda t, c, i: (i, t * nc + c)),
                  out_shape=jax.ShapeDtypeStruct((S, T * C), out_dtype),
                  dims=("parallel", "parallel", "arbitrary"))(dy, dy, w)


def ffn_conv_fwd(u, w, b, *, tr=512, tw=512):
    S, C2 = u.shape
    tr = min(tr, S)
    width = w.shape[0]
    half = C2 // 2
    nc = half // tw

    def body(g_ref, gh_ref, u_ref, uh_ref, wg_ref, wu_ref, bg_ref, bu_ref, o_ref):
        i = pl.program_id(1)
        live = (i > 0).astype(F32)
        yg, _ = _conv_taps(g_ref[...], gh_ref[...] * live, wg_ref, width)
        yu, _ = _conv_taps(u_ref[...], uh_ref[...] * live, wu_ref, width)
        o_ref[...] = (_silu(yg + bg_ref[...]) * (yu + bu_ref[...])).astype(o_ref.dtype)

    return _pcall(body, name="ffn_conv_fwd", grid=(nc, S // tr),
                  in_specs=[pl.BlockSpec((tr, tw), lambda c, i: (i, c)), _prev_halo_spec(tr, tw, lambda c: c),
                            pl.BlockSpec((tr, tw), lambda c, i: (i, c + nc)),
                            _prev_halo_spec(tr, tw, lambda c: c + nc),
                            pl.BlockSpec((width, tw), lambda c, i: (0, c)),
                            pl.BlockSpec((width, tw), lambda c, i: (0, c + nc)),
                            pl.BlockSpec((1, tw), lambda c, i: (0, c)), pl.BlockSpec((1, tw), lambda c, i: (0, c + nc))],
                  out_specs=pl.BlockSpec((tr, tw), lambda c, i: (i, c)),
                  out_shape=jax.ShapeDtypeStruct((S, half), BF16),
                  dims=("parallel", "arbitrary"))(u, u, u, u, w, w, b, b)


def ffn_conv_bwd(u, w, b, df, *, tr=512, tw=512):
    S, C2 = u.shape
    tr = min(tr, S)
    width = w.shape[0]
    half = C2 // 2
    nc = half // tw

    def body(g_ref, gh_ref, u_ref, uh_ref, wg_ref, wu_ref, bg_ref, bu_ref, df_ref, dc_ref, dw_ref, db_ref):
        i = pl.program_id(1)
        live = (i > 0).astype(F32)
        yg, gt = _conv_taps(g_ref[...], gh_ref[...] * live, wg_ref, width)
        yu, ut = _conv_taps(u_ref[...], uh_ref[...] * live, wu_ref, width)
        yg = yg + bg_ref[...]
        yu = yu + bu_ref[...]
        sg = _sigmoid(yg)
        dfv = df_ref[...].astype(F32)
        dcs = (dfv * yu * (sg * (1.0 + yg * (1.0 - sg))), dfv * (yg * sg))

        @pl.when(i == 0)
        def _():
            dw_ref[...] = jnp.zeros_like(dw_ref)
            db_ref[...] = jnp.zeros_like(db_ref)

        for t, (dc, taps) in enumerate(zip(dcs, (gt, ut))):
            dc_ref[t] = dc
            db_ref[t] += jnp.sum(dc, axis=0, keepdims=True)
            for j in range(width):
                dw_ref[t, j:j + 1, :] += jnp.sum(dc * taps[j], axis=0, keepdims=True)

    dc, dw, db = _pcall(
        body, name="ffn_conv_bwd", grid=(nc, S // tr),
        in_specs=[pl.BlockSpec((tr, tw), lambda c, i: (i, c)), _prev_halo_spec(tr, tw, lambda c: c),
                  pl.BlockSpec((tr, tw), lambda c, i: (i, c + nc)), _prev_halo_spec(tr, tw, lambda c: c + nc),
                  pl.BlockSpec((width, tw), lambda c, i: (0, c)), pl.BlockSpec((width, tw), lambda c, i: (0, c + nc)),
                  pl.BlockSpec((1, tw), lambda c, i: (0, c)), pl.BlockSpec((1, tw), lambda c, i: (0, c + nc)),
                  pl.BlockSpec((tr, tw), lambda c, i: (i, c))],
        out_specs=[pl.BlockSpec((2, tr, tw), lambda c, i: (0, i, c)), pl.BlockSpec((2, width, tw), lambda c, i: (0, 0, c)),
                   pl.BlockSpec((2, 1, tw), lambda c, i: (0, 0, c))],
        out_shape=[jax.ShapeDtypeStruct((2, S, half), F32), jax.ShapeDtypeStruct((2, width, half), F32),
                   jax.ShapeDtypeStruct((2, 1, half), F32)],
        dims=("parallel", "arbitrary"))(u, u, u, u, w, w, b, b, df)
    return dc, jnp.concatenate([dw[0], dw[1]], axis=1), jnp.concatenate([db[0], db[1]], axis=1)


_MODE_OF = {v: k for k, v in _DN.items()}


def _bf16_dot(a, b, mode):
    return lax.dot_general(a.astype(BF16), b.astype(BF16), _DN[mode], preferred_element_type=F32)


@functools.partial(jax.custom_vjp, nondiff_argnums=(2,))
def _bdot_mode(a, b, mode):
    return _bf16_dot(a, b, mode)


def _bdot_fwd(a, b, mode):
    return _bf16_dot(a, b, mode), (a, b)


def _bdot_bwd(mode, res, ct):
    a, b = res
    if mode == "nn":
        da, db = _bf16_dot(ct, b, "nt"), _bf16_dot(a, ct, "tn")
    elif mode == "nt":
        da, db = _bf16_dot(ct, b, "nn"), _bf16_dot(ct, a, "tn")
    else:
        da, db = _bf16_dot(b, ct, "nt"), _bf16_dot(a, ct, "nn")
    return da.astype(a.dtype), db.astype(b.dtype)


_bdot_mode.defvjp(_bdot_fwd, _bdot_bwd)


def _bdot(a, b, dn=_DN["nn"]):
    return _bdot_mode(a, b, _MODE_OF[dn])


def _hi_lo(x):
    hi = x.astype(BF16)
    return hi, (x - hi.astype(F32)).astype(BF16)


def _dot3_raw(a, b, mode):
    a1, a2 = _hi_lo(a)
    b1, b2 = _hi_lo(b)
    dot = lambda p, q: lax.dot_general(p, q, _DN[mode], preferred_element_type=F32)
    return dot(a1, b1) + (dot(a1, b2) + dot(a2, b1))


@functools.partial(jax.custom_vjp, nondiff_argnums=(2,))
def _dot3(a, b, mode="nn"):
    return _dot3_raw(a, b, mode)


def _dot3_fwd(a, b, mode):
    return _dot3_raw(a, b, mode), (a, b)


def _dot3_bwd(mode, res, ct):
    a, b = res
    if mode == "nn":
        return _dot3_raw(ct, b, "nt"), _dot3_raw(a, ct, "tn")
    if mode == "nt":
        return _dot3_raw(ct, b, "nn"), _dot3_raw(ct, a, "tn")
    return _dot3_raw(b, ct, "nt"), _dot3_raw(a, ct, "nn")


_dot3.defvjp(_dot3_fwd, _dot3_bwd)


@functools.partial(jax.custom_vjp, nondiff_argnums=(2,))
def _gdot(a, b, mode="nn"):
    return _bf16_dot(a, b, mode)


def _gdot_fwd(a, b, mode):
    return _bf16_dot(a, b, mode), (a.astype(BF16), b.astype(BF16))


def _ct_dot(p, q, mode, ct_first):
    ct, r = (p, q) if ct_first else (q, p)
    c1, c2 = _hi_lo(ct)
    dot = lambda c: lax.dot_general(*((c, r) if ct_first else (r, c)), _DN[mode], preferred_element_type=F32)
    return dot(c1) + dot(c2)


def _gdot_bwd(mode, res, ct):
    a, b = res
    if mode == "nn":
        return _ct_dot(ct, b, "nt", True), _ct_dot(a, ct, "tn", False)
    if mode == "nt":
        return _ct_dot(ct, b, "nn", True), _ct_dot(ct, a, "tn", True)
    return _ct_dot(b, ct, "nt", False), _ct_dot(a, ct, "nn", False)


_gdot.defvjp(_gdot_fwd, _gdot_bwd)


def _split_dot(ones, x):
    x1 = x.astype(BF16)
    r1 = x - x1.astype(F32)
    x2 = r1.astype(BF16)
    x3 = (r1 - x2.astype(F32)).astype(BF16)
    m = ones.astype(BF16)
    dot = lambda p: lax.dot_general(m, p, _DN["nn"], preferred_element_type=F32)
    return dot(x1) + dot(x2) + dot(x3)


@jax.custom_vjp
def _tri_cumsum(x, lower, upper):
    return _split_dot(lower, x)


def _tri_cumsum_fwd(x, lower, upper):
    return _split_dot(lower, x), (lower, upper)


def _tri_cumsum_bwd(res, ct):
    lower, upper = res
    return _split_dot(upper, ct), jnp.zeros_like(lower), jnp.zeros_like(upper)


_tri_cumsum.defvjp(_tri_cumsum_fwd, _tri_cumsum_bwd)


def _tri_masks(n):
    r = lax.broadcasted_iota(jnp.int32, (n, n), 0)
    c = lax.broadcasted_iota(jnp.int32, (n, n), 1)
    return r >= c, r > c


def _gdn_chunk(q, k, v, z, ab, a_row, dt_row, norm_w, state, sel_a, sel_b):
    C = q.shape[0]
    incl, strict = _tri_masks(C)
    lower = jnp.where(incl, 1.0, 0.0)
    qn = q * lax.rsqrt(jnp.sum(q * q, axis=-1, keepdims=True) + NORM_EPS) * (GDN_DK ** -0.5)
    kn = k * lax.rsqrt(jnp.sum(k * k, axis=-1, keepdims=True) + NORM_EPS)
    g = jnp.sum(-jnp.exp(a_row) * _softplus(ab + dt_row) * sel_a, axis=-1, keepdims=True)
    beta = jnp.sum(_sigmoid(ab) * sel_b, axis=-1, keepdims=True)
    gb = jnp.broadcast_to(g, (C, C))
    g_col = _tri_cumsum(gb, lower, jnp.where(strict, 0.0, 1.0))
    g_row = g_col.T
    g_last = jnp.sum(gb, axis=0, keepdims=True)
    gamma = jnp.where(incl, jnp.exp(jnp.where(incl, g_col - g_row, 0.0)), 0.0)
    e_col = jnp.exp(g_col)
    kb = kn * beta
    a_mat = jnp.where(strict, _gdot(kb, kn, "nt") * gamma, 0.0)
    x = jnp.concatenate([v * beta, kb * e_col], axis=1)
    pw = -a_mat
    steps = int(math.log2(C))
    for it in range(steps):
        x = x + _dot3(pw, x, "nn")
        if it < steps - 1:
            pw = _dot3(pw, pw, "nn")
    u, w = x[:, :GDN_DV], x[:, GDN_DV:]
    attn = _gdot(qn, kn, "nt") * gamma
    q_dec = qn * e_col
    k_dec = kn * jnp.exp(g_last - g_col)
    v_new = u - _gdot(w, state, "nn")
    o = _gdot(q_dec, state, "nn") + _gdot(attn, v_new, "nn")
    state_new = state * jnp.exp(jnp.broadcast_to(g_last, state.shape)) + _gdot(k_dec, v_new, "tn")
    y = _rms(o, norm_w) * _silu(z)
    return y, state_new


def _head_selectors(h):
    lane = lax.broadcasted_iota(jnp.int32, (1, LANES), 1)
    return jnp.where(lane == h, 1.0, 0.0), jnp.where(lane == h + GDN_HEADS, 1.0, 0.0)


def _gdn_in_specs(rev, nc):
    def n_(n):
        return nc - 1 - n if rev else n
    H = GDN_HEADS
    blk = lambda off: pl.BlockSpec((CHUNK, LANES), lambda n, h: (n_(n), off + h))
    row = pl.BlockSpec((1, LANES), lambda n, h: (0, 0))
    return n_, [blk(0), blk(H), blk(2 * H), blk(ZIN_Z // LANES),
                pl.BlockSpec((CHUNK, LANES), lambda n, h: (n_(n), ZIN_AB // LANES)), row, row, row]


def _hosting(call, hosted, *args):
    res = call(*args)
    return res if hosted is not None else (res, [])


def gdn_fwd(qkv, zin, a_row, dt_row, norm_w, hosted=None):
    S = qkv.shape[0]
    nc = S // CHUNK
    H = GDN_HEADS
    _, in_specs = _gdn_in_specs(False, nc)

    def body(q_ref, k_ref, v_ref, z_ref, ab_ref, a_ref, dt_ref, nw_ref, y_ref, st_ref, state):
        n, h = pl.program_id(0), pl.program_id(1)

        @pl.when(n == 0)
        def _():
            state[h] = jnp.zeros((GDN_DK, GDN_DV), F32)

        st = state[h]
        st_ref[...] = st
        sel_a, sel_b = _head_selectors(h)
        y, st_new = _gdn_chunk(q_ref[...], k_ref[...], v_ref[...], z_ref[...], ab_ref[...], a_ref[...],
                               dt_ref[...], nw_ref[...], st, sel_a, sel_b)
        y_ref[...] = y.astype(y_ref.dtype)
        state[h] = st_new

    call = _pcall(body, name="gdn_fwd", grid=(nc, H), in_specs=in_specs,
                  out_specs=[pl.BlockSpec((CHUNK, LANES), lambda n, h: (n, h)),
                             pl.BlockSpec((None, None, GDN_DK, GDN_DV), lambda n, h: (h, n, 0, 0))],
                  out_shape=[jax.ShapeDtypeStruct((S, H * GDN_DV), BF16),
                             jax.ShapeDtypeStruct((H, nc, GDN_DK, GDN_DV), F32)],
                  scratch_shapes=[pltpu.VMEM((H, GDN_DK, GDN_DV), F32)],
                  dims=("arbitrary", "arbitrary"), hosted=hosted)
    return _hosting(call, hosted, qkv, qkv, qkv, zin, zin, a_row, dt_row, norm_w)


def gdn_bwd(qkv, zin, a_row, dt_row, norm_w, states, dy, dy_col0, hosted=None):
    S = qkv.shape[0]
    nc = S // CHUNK
    H = GDN_HEADS
    n_, in_specs = _gdn_in_specs(True, nc)
    in_specs = in_specs + [pl.BlockSpec((None, None, GDN_DK, GDN_DV), lambda n, h: (h, n_(n), 0, 0)),
                           pl.BlockSpec((CHUNK, LANES), lambda n, h: (n_(n), dy_col0 + h))]

    def body(q_ref, k_ref, v_ref, z_ref, ab_ref, a_ref, dt_ref, nw_ref, st_ref, dy_ref,
             dq_ref, dk_ref, dv_ref, dz_ref, dab_ref, da_ref, ddt_ref, dnw_ref, dstate):
        n, h = pl.program_id(0), pl.program_id(1)

        @pl.when(n == 0)
        def _():
            dstate[h] = jnp.zeros((GDN_DK, GDN_DV), F32)

        @pl.when((n == 0) & (h == 0))
        def _():
            da_ref[...] = jnp.zeros_like(da_ref)
            ddt_ref[...] = jnp.zeros_like(ddt_ref)
            dnw_ref[...] = jnp.zeros_like(dnw_ref)

        sel_a, sel_b = _head_selectors(h)
        _, vjp = jax.vjp(lambda *a: _gdn_chunk(*a, sel_a, sel_b), q_ref[...], k_ref[...], v_ref[...], z_ref[...],
                         ab_ref[...], a_ref[...], dt_ref[...], nw_ref[...], st_ref[...])
        dq, dk, dv, dz, dab, da, ddt, dnw, dst = vjp((dy_ref[...].astype(F32), dstate[h]))
        dq_ref[...] = dq
        dk_ref[...] = dk
        dv_ref[...] = dv
        dz_ref[...] = dz.astype(dz_ref.dtype)

        @pl.when(h == 0)
        def _():
            dab_ref[...] = jnp.zeros_like(dab_ref)

        dab_ref[...] += dab
        da_ref[...] += da
        ddt_ref[...] += ddt
        dnw_ref[...] += dnw
        dstate[h] = dst

    blk = pl.BlockSpec((CHUNK, LANES), lambda n, h: (n_(n), h))
    row = pl.BlockSpec((1, LANES), lambda n, h: (0, 0))
    wide = jax.ShapeDtypeStruct((S, H * LANES), F32)
    call = _pcall(body, name="gdn_bwd", grid=(nc, H), in_specs=in_specs,
                  out_specs=[blk, blk, blk, blk, pl.BlockSpec((CHUNK, LANES), lambda n, h: (n_(n), 0)), row, row, row],
                  out_shape=[wide, wide, wide, jax.ShapeDtypeStruct((S, H * LANES), BF16),
                             jax.ShapeDtypeStruct((S, LANES), F32)] + [jax.ShapeDtypeStruct((1, LANES), F32)] * 3,
                  scratch_shapes=[pltpu.VMEM((H, GDN_DK, GDN_DV), F32)],
                  dims=("arbitrary", "arbitrary"), hosted=hosted)
    return _hosting(call, hosted, qkv, qkv, qkv, zin, zin, a_row, dt_row, norm_w, states, dy)


def _rope_full(x, cos, sin):
    x1, x2 = x[:, :RET_DK // 2], x[:, RET_DK // 2:]
    return jnp.concatenate([x1 * cos - x2 * sin, x2 * cos + x1 * sin], axis=1)


def _ret_chunk(q, k, v, cos, sin, lg, state):
    C = q.shape[0]
    incl, _ = _tri_masks(C)
    qr = _rope_full(q, cos, sin)
    kr = _rope_full(k, cos, sin) * (RET_DK ** -0.5)
    r = lax.broadcasted_iota(jnp.int32, (C, C), 0)
    c = lax.broadcasted_iota(jnp.int32, (C, C), 1)
    dist = jnp.where(incl, (r - c).astype(F32), 0.0)
    decay = jnp.where(incl, jnp.exp(dist * lg), 0.0)
    pos = lax.broadcasted_iota(jnp.int32, (C, 1), 0).astype(F32)
    lg1 = lg[:, :1]
    xi = jnp.exp((pos + 1.0) * lg1)
    zeta = jnp.exp((C - 1.0 - pos) * lg1)
    inner = _bdot(_bdot(qr, kr, _DN["nt"]) * decay, v)
    cross = _bdot(qr * xi, state)
    state_new = state * jnp.exp(C * lg1) + _bdot(kr * zeta, v, _DN["tn"])
    o = inner + cross
    mu = jnp.mean(o, axis=-1, keepdims=True)
    var = jnp.mean(jnp.square(o - mu), axis=-1, keepdims=True)
    return (o - mu) * lax.rsqrt(var + NORM_EPS), state_new


def _ret_log_gamma():
    lg = np.log1p(-np.power(2.0, -5.0 - np.arange(RET_HEADS, dtype=np.float64))).astype(np.float32)
    return jnp.asarray(np.broadcast_to(lg[:, None, None], (RET_HEADS, 1, LANES)).copy())


def _ret_in_specs(rev, nc):
    def n_(n):
        return nc - 1 - n if rev else n
    H = RET_HEADS
    return n_, [pl.BlockSpec((CHUNK, RET_DK), lambda n, h: (n_(n), h)),
                pl.BlockSpec((CHUNK, RET_DK), lambda n, h: (n_(n), H + h)),
                pl.BlockSpec((CHUNK, RET_DV), lambda n, h: (n_(n), 2 * H * RET_DK // RET_DV + h)),
                pl.BlockSpec((CHUNK, LANES), lambda n, h: (n_(n), 0)),
                pl.BlockSpec((CHUNK, LANES), lambda n, h: (n_(n), 0)),
                pl.BlockSpec((None, 1, LANES), lambda n, h: (h, 0, 0))]


def ret_fwd(zz, cos, sin):
    S = zz.shape[0]
    nc = S // CHUNK
    H = RET_HEADS
    _, in_specs = _ret_in_specs(False, nc)

    def body(q_ref, k_ref, v_ref, cos_ref, sin_ref, lg_ref, o_ref, st_ref, state):
        n, h = pl.program_id(0), pl.program_id(1)

        @pl.when(n == 0)
        def _():
            state[h] = jnp.zeros((RET_DK, RET_DV), F32)

        st = state[h]
        st_ref[...] = st
        o, st_new = _ret_chunk(q_ref[...], k_ref[...], v_ref[...], cos_ref[...], sin_ref[...], lg_ref[...], st)
        o_ref[...] = o
        state[h] = st_new

    return _pcall(body, name="ret_fwd", grid=(nc, H), in_specs=in_specs,
                  out_specs=[pl.BlockSpec((CHUNK, RET_DV), lambda n, h: (n, h)),
                             pl.BlockSpec((None, None, RET_DK, RET_DV), lambda n, h: (h, n, 0, 0))],
                  out_shape=[jax.ShapeDtypeStruct((S, H * RET_DV), F32),
                             jax.ShapeDtypeStruct((H, nc, RET_DK, RET_DV), F32)],
                  scratch_shapes=[pltpu.VMEM((H, RET_DK, RET_DV), F32)],
                  dims=("arbitrary", "arbitrary"))(zz, zz, zz, cos, sin, _ret_log_gamma())


def ret_bwd(zz, cos, sin, states, do):
    S = zz.shape[0]
    nc = S // CHUNK
    H = RET_HEADS
    n_, in_specs = _ret_in_specs(True, nc)
    in_specs = in_specs + [pl.BlockSpec((None, None, RET_DK, RET_DV), lambda n, h: (h, n_(n), 0, 0)),
                           pl.BlockSpec((CHUNK, RET_DV), lambda n, h: (n_(n), h))]

    def body(q_ref, k_ref, v_ref, cos_ref, sin_ref, lg_ref, st_ref, do_ref, dq_ref, dk_ref, dv_ref, dstate):
        n, h = pl.program_id(0), pl.program_id(1)

        @pl.when(n == 0)
        def _():
            dstate[h] = jnp.zeros((RET_DK, RET_DV), F32)

        cos, sin, lg = cos_ref[...], sin_ref[...], lg_ref[...]
        _, vjp = jax.vjp(lambda q, k, v, st: _ret_chunk(q, k, v, cos, sin, lg, st),
                         q_ref[...], k_ref[...], v_ref[...], st_ref[...])
        dq, dk, dv, dst = vjp((do_ref[...], dstate[h]))
        dq_ref[...] = dq.astype(dq_ref.dtype)
        dk_ref[...] = dk.astype(dk_ref.dtype)
        dv_ref[...] = dv.astype(dv_ref.dtype)
        dstate[h] = dst

    return _pcall(body, name="ret_bwd", grid=(nc, H), in_specs=in_specs,
                  out_specs=[pl.BlockSpec((CHUNK, RET_DK), lambda n, h: (n_(n), h)),
                             pl.BlockSpec((CHUNK, RET_DK), lambda n, h: (n_(n), h)),
                             pl.BlockSpec((CHUNK, RET_DV), lambda n, h: (n_(n), h))],
                  out_shape=[jax.ShapeDtypeStruct((S, H * RET_DK), BF16), jax.ShapeDtypeStruct((S, H * RET_DK), BF16),
                             jax.ShapeDtypeStruct((S, H * RET_DV), BF16)],
                  scratch_shapes=[pltpu.VMEM((H, RET_DK, RET_DV), F32)],
                  dims=("arbitrary", "arbitrary"))(zz, zz, zz, cos, sin, _ret_log_gamma(), states, do)


MLA_SCALE = (MLA_NOPE + MLA_ROPE) ** -0.5
NEG = -1e30


def _mla_scores(q, kn, kpe, i, j, t):
    s = (lax.dot_general(q[:, :LANES], kn, _DN["nt"], preferred_element_type=F32)
         + lax.dot_general(q[:, LANES:], kpe, _DN["nt"], preferred_element_type=F32)) * MLA_SCALE
    row = i * t + lax.broadcasted_iota(jnp.int32, s.shape, 0)
    col = j * t + lax.broadcasted_iota(jnp.int32, s.shape, 1)
    return s, col <= row


def flash_fwd(qr, kv, kpe, *, t=512, hosted=None):
    S = qr.shape[0]
    t = min(t, S)
    nb = S // t
    H = MLA_HEADS

    def body(q_ref, kn_ref, v_ref, kpe_ref, o_ref, lse_ref, m_s, l_s, acc):
        i, j = pl.program_id(1), pl.program_id(2)

        @pl.when(j == 0)
        def _():
            m_s[...] = jnp.full_like(m_s, NEG)
            l_s[...] = jnp.zeros_like(l_s)
            acc[...] = jnp.zeros_like(acc)

        @pl.when(j <= i)
        def _():
            s, ok = _mla_scores(q_ref[...], kn_ref[...], kpe_ref[...], i, j, t)
            s = jnp.where(ok, s, NEG)
            m_new = jnp.maximum(m_s[...], jnp.max(s, axis=-1, keepdims=True))
            p = jnp.where(ok, jnp.exp(s - m_new), 0.0)
            alpha = jnp.exp(m_s[...] - m_new)
            l_s[...] = alpha * l_s[...] + jnp.sum(p, axis=-1, keepdims=True)
            acc[...] = alpha * acc[...] + _bdot(p, v_ref[...])
            m_s[...] = m_new

        @pl.when(j == nb - 1)
        def _():
            o_ref[...] = (acc[...] / l_s[...]).astype(o_ref.dtype)
            lse_ref[...] = m_s[...] + jnp.log(l_s[...])

    kmap = lambda off: (lambda h, i, j: (jnp.minimum(j, i), off + h))
    call = _pcall(body, name="mla_flash_fwd", grid=(H, nb, nb),
                  in_specs=[pl.BlockSpec((t, 2 * LANES), lambda h, i, j: (i, h)),
                            pl.BlockSpec((t, LANES), kmap(0)), pl.BlockSpec((t, LANES), kmap(H)),
                            pl.BlockSpec((t, LANES), lambda h, i, j: (jnp.minimum(j, i), 0))],
                  out_specs=[pl.BlockSpec((t, LANES), lambda h, i, j: (i, h)),
                             pl.BlockSpec((None, t, 1), lambda h, i, j: (h, i, 0))],
                  out_shape=[jax.ShapeDtypeStruct((S, H * MLA_V), BF16), jax.ShapeDtypeStruct((H, S, 1), F32)],
                  scratch_shapes=[pltpu.VMEM((t, 1), F32), pltpu.VMEM((t, 1), F32), pltpu.VMEM((t, MLA_V), F32)],
                  dims=("parallel", "parallel", "arbitrary"), hosted=hosted)
    return _hosting(call, hosted, qr, kv, kv, kpe)


def _mla_p_ds(q, kn, v, kpe, do, o, lse, i, j, t):
    s, ok = _mla_scores(q, kn, kpe, i, j, t)
    p = jnp.where(ok, jnp.exp(jnp.where(ok, s, 0.0) - lse), 0.0)
    dof = do.astype(F32)
    delta = jnp.sum(dof * o.astype(F32), axis=-1, keepdims=True)
    dp = lax.dot_general(do.astype(BF16), v, _DN["nt"], preferred_element_type=F32)
    ds = p * (dp - delta) * MLA_SCALE
    return p, ds


def flash_bwd_dq(qr, kv, kpe, o, lse, dy, dy_col0, *, t=512):
    S = qr.shape[0]
    t = min(t, S)
    nb = S // t
    H = MLA_HEADS

    def body(q_ref, kn_ref, v_ref, kpe_ref, o_ref, lse_ref, do_ref, dq_ref, acc):
        i, j = pl.program_id(1), pl.program_id(2)

        @pl.when(j == 0)
        def _():
            acc[...] = jnp.zeros_like(acc)

        @pl.when(j <= i)
        def _():
            _, ds = _mla_p_ds(q_ref[...], kn_ref[...], v_ref[...], kpe_ref[...], do_ref[...], o_ref[...],
                              lse_ref[...], i, j, t)
            acc[...] += jnp.concatenate([_bdot(ds, kn_ref[...]), _bdot(ds, kpe_ref[...])], axis=1)

        @pl.when(j == nb - 1)
        def _():
            dq_ref[...] = acc[...]

    kmap = lambda off: (lambda h, i, j: (jnp.minimum(j, i), off + h))
    return _pcall(body, name="mla_flash_dq", grid=(H, nb, nb),
                  in_specs=[pl.BlockSpec((t, 2 * LANES), lambda h, i, j: (i, h)),
                            pl.BlockSpec((t, LANES), kmap(0)), pl.BlockSpec((t, LANES), kmap(H)),
                            pl.BlockSpec((t, LANES), lambda h, i, j: (jnp.minimum(j, i), 0)),
                            pl.BlockSpec((t, LANES), lambda h, i, j: (i, h)),
                            pl.BlockSpec((None, t, 1), lambda h, i, j: (h, i, 0)),
                            pl.BlockSpec((t, LANES), lambda h, i, j: (i, dy_col0 + h))],
                  out_specs=pl.BlockSpec((t, 2 * LANES), lambda h, i, j: (i, h)),
                  out_shape=jax.ShapeDtypeStruct((S, H * 2 * LANES), F32),
                  scratch_shapes=[pltpu.VMEM((t, 2 * LANES), F32)],
                  dims=("parallel", "parallel", "arbitrary"))(qr, kv, kv, kpe, o, lse, dy)


def flash_bwd_dkv(qr, kv, kpe, o, lse, dy, dy_col0, *, t=512, hosted=None):
    S = qr.shape[0]
    t = min(t, S)
    nb = S // t
    H = MLA_HEADS

    def body(q_ref, kn_ref, v_ref, kpe_ref, o_ref, lse_ref, do_ref, dkn_ref, dv_ref, dkpe_ref, akn, av):
        j, h, i = pl.program_id(0), pl.program_id(1), pl.program_id(2)

        @pl.when(i == 0)
        def _():
            akn[...] = jnp.zeros_like(akn)
            av[...] = jnp.zeros_like(av)

        @pl.when((i == 0) & (h == 0))
        def _():
            dkpe_ref[...] = jnp.zeros_like(dkpe_ref)

        @pl.when(i >= j)
        def _():
            q = q_ref[...]
            p, ds = _mla_p_ds(q, kn_ref[...], v_ref[...], kpe_ref[...], do_ref[...], o_ref[...], lse_ref[...], i, j, t)
            av[...] += _bdot(p, do_ref[...], _DN["tn"])
            akn[...] += _bdot(ds, q[:, :LANES], _DN["tn"])
            dkpe_ref[...] += _bdot(ds, q[:, LANES:], _DN["tn"])

        @pl.when(i == nb - 1)
        def _():
            dkn_ref[...] = akn[...].astype(dkn_ref.dtype)
            dv_ref[...] = av[...].astype(dv_ref.dtype)

    qmap = lambda off: (lambda j, h, i: (jnp.maximum(i, j), off + h))
    call = _pcall(
        body, name="mla_flash_dkv", grid=(nb, H, nb),
        in_specs=[pl.BlockSpec((t, 2 * LANES), qmap(0)),
                  pl.BlockSpec((t, LANES), lambda j, h, i: (j, h)), pl.BlockSpec((t, LANES), lambda j, h, i: (j, H + h)),
                  pl.BlockSpec((t, LANES), lambda j, h, i: (j, 0)),
                  pl.BlockSpec((t, LANES), qmap(0)),
                  pl.BlockSpec((None, t, 1), lambda j, h, i: (h, jnp.maximum(i, j), 0)),
                  pl.BlockSpec((t, LANES), qmap(dy_col0))],
        out_specs=[pl.BlockSpec((t, LANES), lambda j, h, i: (j, h)), pl.BlockSpec((t, LANES), lambda j, h, i: (j, h)),
                   pl.BlockSpec((t, LANES), lambda j, h, i: (j, 0))],
        out_shape=[jax.ShapeDtypeStruct((S, H * LANES), BF16), jax.ShapeDtypeStruct((S, H * LANES), BF16),
                   jax.ShapeDtypeStruct((S, LANES), F32)],
        scratch_shapes=[pltpu.VMEM((t, LANES), F32), pltpu.VMEM((t, LANES), F32)],
        dims=("arbitrary", "arbitrary", "arbitrary"), hosted=hosted)
    (dkn, dv, dkpe), extra = _hosting(call, hosted, qr, kv, kv, kpe, o, lse, dy)
    return (jnp.concatenate([dkn, dv], axis=1), dkpe), extra


def loss_head(h, target, g, *, tr=256):
    S, D = h.shape
    tr = min(tr, S)

    def body(h_ref, t_ref, g_ref, loss_ref, dh_ref, dg_ref):
        tgt = t_ref[...]

        def f(hh, gg):
            err = jnp.square(_rms(hh, gg) - tgt)
            per_row = jnp.sum(err, axis=-1, keepdims=True) * (0.5 / D)
            return jnp.sum(per_row, axis=0, keepdims=True)

        val, vjp = jax.vjp(f, h_ref[...], g_ref[...])
        dh, dg = vjp(jnp.ones((1, 1), F32))
        dh_ref[...] = dh

        @pl.when(pl.program_id(0) == 0)
        def _():
            loss_ref[...] = jnp.zeros_like(loss_ref)
            dg_ref[...] = jnp.zeros_like(dg_ref)

        loss_ref[...] += jnp.broadcast_to(val, loss_ref.shape)
        dg_ref[...] += dg

    return _pcall(body, name="loss_head", grid=(S // tr,),
                  in_specs=[_row_spec(tr, D, 0), _row_spec(tr, D, 0), _full_spec(g)],
                  out_specs=[pl.BlockSpec((1, LANES), lambda i: (0, 0)), _row_spec(tr, D, 0), _full_spec(g)],
                  out_shape=[jax.ShapeDtypeStruct((1, LANES), F32), jax.ShapeDtypeStruct((S, D), F32),
                             jax.ShapeDtypeStruct(g.shape, F32)],
                  dims=("arbitrary",))(h, target, g)


def _rope_tables(positions, dim):
    inv_freq = ROPE_THETA ** (-jnp.arange(0, dim, 2, dtype=F32) / dim)
    ang = positions.astype(F32)[:, None] * inv_freq
    return jnp.cos(ang), jnp.sin(ang)


def _pad_cols(w, n):
    return jnp.pad(w, ((0, 0), (0, n - w.shape[1])))


def _prep_w_in0(w):
    return jnp.concatenate([w[:, :4096], w[:, 4112:5136], _pad_cols(w[:, 5136:5200], LANES),
                            _pad_cols(w[:, 4096:4112], LANES)], axis=1)


def _unprep_w_in0(g):
    return jnp.concatenate([g[:, :4096], g[:, ZIN_AB:ZIN_AB + 16], g[:, ZIN_CQ:ZIN_KR], g[:, ZIN_KR:ZIN_KR + MLA_ROPE]],
                           axis=1)


def _prep_w_uq(w):
    w = w.reshape(MLA_Q_RANK, MLA_HEADS, MLA_NOPE + MLA_ROPE)
    w = jnp.pad(w, ((0, 0), (0, 0), (0, 2 * LANES - MLA_NOPE - MLA_ROPE)))
    return w.reshape(MLA_Q_RANK, MLA_HEADS * 2 * LANES)


def _unprep_w_uq(g):
    g = g.reshape(MLA_Q_RANK, MLA_HEADS, 2 * LANES)[:, :, :MLA_NOPE + MLA_ROPE]
    return g.reshape(MLA_Q_RANK, MLA_HEADS * (MLA_NOPE + MLA_ROPE))


def _prep_w_ukv(w):
    w = w.reshape(MLA_KV_RANK, MLA_HEADS, 2, LANES)
    return jnp.transpose(w, (0, 2, 1, 3)).reshape(MLA_KV_RANK, 2 * MLA_HEADS * LANES)


def _unprep_w_ukv(g):
    g = g.reshape(MLA_KV_RANK, 2, MLA_HEADS, LANES)
    return jnp.transpose(g, (0, 2, 1, 3)).reshape(MLA_KV_RANK, 2 * MLA_HEADS * LANES)


def _row(v, n=None):
    v = v.reshape(1, -1).astype(F32)
    return v if n is None else _pad_cols(v, n)


def _ffn_fwd(h, norm_g, w_up, conv_w, conv_b, w_down, tag):
    (hn,) = rowwise(_fn_rms, [(h, D_MODEL, 0)], [norm_g], [], [(D_MODEL, BF16)], name=f"{tag}_ffn_norm")
    u = matmul(hn, w_up, "nn", F32, b_shards=N_CHIPS, tn=1408, name=f"{tag}_ffn_up")
    f = ffn_conv_fwd(u, conv_w, conv_b)
    h_out = matmul(f, w_down, "nn", F32, add=h, name=f"{tag}_ffn_down")
    return h_out, (hn, u, f)


def _ffn_bwd(dh, h, norm_g, w_up, conv_w, conv_b, w_down, saved, tag):
    hn, u, f = saved
    df = matmul(dh, w_down, "nt", BF16, name=f"{tag}_ffn_down_dx")
    g_down = matmul(f, dh, "tn", BF16, name=f"{tag}_ffn_down_dw")
    dc, g_conv_w, g_conv_b = ffn_conv_bwd(u, conv_w, conv_b, df)
    du = conv_transpose(dc, conv_w, BF16, name=f"{tag}_ffn_conv_dx")
    g_up = matmul(hn, du, "tn", BF16, out_shards=N_CHIPS, tn=1408, name=f"{tag}_ffn_up_dw")
    dhn = matmul(du, w_up, "nt", F32, b_shards=N_CHIPS, name=f"{tag}_ffn_up_dx")
    (dh_in,), (g_norm,) = rowwise_bwd(_fn_rms, [(h, D_MODEL, 0)], [norm_g], [], [(dhn, D_MODEL, 0)], [F32],
                                      adds=[(dh, D_MODEL, 0)], name=f"{tag}_ffn_norm_bwd")
    return dh_in, dict(ffn_norm=g_norm, ffn_w_up=g_up, ffn_conv_w=g_conv_w, ffn_conv_b=g_conv_b, ffn_w_down=g_down)


def _ple_fwd(h, p_i, w_proj, gate_g, w_gate, tag):
    (hg,) = rowwise(_fn_rms, [(h, D_MODEL, 0)], [gate_g], [], [(D_MODEL, BF16)], name=f"{tag}_ple_norm")
    gl = matmul(hg, w_gate, "nn", F32, name=f"{tag}_ple_gate")
    pp = matmul(p_i, w_proj, "nn", F32, b_shards=N_CHIPS, name=f"{tag}_ple_proj")
    (h_out,) = rowwise(_fn_ple, [(h, D_MODEL, 0), (pp, D_MODEL, 0), (gl, D_MODEL, 0)], [], [], [(D_MODEL, F32)],
                       name=f"{tag}_ple_add")
    return h_out, (hg, gl, pp)


def _ple_bwd(dh, h, p_i, w_proj, gate_g, w_gate, saved, tag):
    hg, gl, pp = saved
    (dpp, dgl), _ = rowwise_bwd(_fn_ple_terms, [(pp, D_MODEL, 0), (gl, D_MODEL, 0)], [], [], [(dh, D_MODEL, 0)],
                                [BF16, BF16], name=f"{tag}_ple_add_bwd")
    g_proj = matmul(p_i, dpp, "tn", BF16, out_shards=N_CHIPS, name=f"{tag}_ple_proj_dw")
    g_gate = matmul(hg, dgl, "tn", BF16, name=f"{tag}_ple_gate_dw")
    dhg = matmul(dgl, w_gate, "nt", F32, name=f"{tag}_ple_gate_dx")
    (dh_in,), (g_norm,) = rowwise_bwd(_fn_rms, [(h, D_MODEL, 0)], [gate_g], [], [(dhg, D_MODEL, 0)], [F32],
                                      adds=[(dh, D_MODEL, 0)], name=f"{tag}_ple_norm_bwd")
    return dh_in, dict(ple_proj=g_proj, ple_gate_norm=g_norm, ple_gate=g_gate)


def local_step(x, p, positions, target, slots, W, ex=None):
    S = x.shape[0]
    G = {}
    p0, p1 = p[0].astype(BF16), p[1].astype(BF16)
    W = dict(W)

    def use(names, bufs):
        for k, b in zip(names, bufs):
            r, c = b.shape[1:]
            W[k] = b.reshape(N_CHIPS * r, c) if k in ROW_SHARDED else (b if k in KEPT_SHARDED else _cols_to_full(b))

    def by_chip(names):
        out = []
        for k in names:
            r, c = slots[k].shape[1:]
            out.append(G[k].reshape(N_CHIPS, r, c) if k in ROW_SHARDED
                       else (G[k] if k in KEPT_SHARDED else _full_to_cols(G[k])))
        return out

    first = [slots[k] for k in GATHER_FIRST]
    use(GATHER_FIRST, ex.gather(first) if ex else first)

    cm, sm = _rope_tables(positions, MLA_ROPE)
    zeros = jnp.zeros((S, LANES - MLA_ROPE), F32)
    cosp = jnp.concatenate([cm, cm, zeros], axis=1)
    sinp = jnp.concatenate([sm, sm, zeros], axis=1)
    cr, sr = _rope_tables(positions, RET_DK)

    w_in0 = _prep_w_in0(W["l0_w_in"])
    w_uq = _prep_w_uq(W["l0_mla_w_uq"])
    w_ukv = _prep_w_ukv(W["l0_mla_w_ukv"])
    a_row = _row(W["l0_gdn_A_log"], LANES)
    dt_row = _row(W["l0_gdn_dt_bias"], LANES)
    gdn_nw = _row(W["l0_gdn_norm"])
    n = {k: _row(W[k]) for k in ("l0_attn_norm", "l0_mla_q_norm", "l0_mla_kv_norm", "l0_ffn_norm",
                                 "l0_ple_gate_norm", "l1_attn_norm", "l1_ret_norm", "l1_ffn_norm",
                                 "l1_ple_gate_norm", "final_norm", "l0_ffn_conv_b", "l1_ffn_conv_b")}

    (hn0,) = rowwise(_fn_rms, [(x, D_MODEL, 0)], [n["l0_attn_norm"]], [], [(D_MODEL, BF16)], name="l0_attn_norm")
    zin = matmul(hn0, w_in0, "nn", F32, name="l0_w_in", tn=768)
    qkv = gdn_conv_fwd(zin, W["l0_gdn_conv"])
    layer0 = [slots[k] for k in GATHER_L0]
    (y_a, gdn_states), got = gdn_fwd(qkv, zin, a_row, dt_row, gdn_nw, hosted=hosted_gather(layer0) if ex else None)
    use(GATHER_L0, got if ex else layer0)
    mla_rows = [(zin, MLA_Q_RANK, ZIN_CQ // MLA_Q_RANK), (zin, MLA_KV_RANK, ZIN_CKV // MLA_KV_RANK),
                (zin, LANES, ZIN_KR // LANES)]
    mla_nd = [(cosp, LANES, 0), (sinp, LANES, 0)]
    cqn, ckvn, kpe = rowwise(_fn_mla_pre, mla_rows, [n["l0_mla_q_norm"], n["l0_mla_kv_norm"]], mla_nd,
                             [(MLA_Q_RANK, BF16), (MLA_KV_RANK, BF16), (LANES, BF16)], name="mla_pre")
    q_lin = matmul(cqn, w_uq, "nn", F32, name="mla_w_uq")
    kv = matmul(ckvn, w_ukv, "nn", BF16, name="mla_w_ukv")
    (qr,) = rowwise(_fn_rope_q, [(q_lin, 2048, 0)], [], mla_nd, [(2048, BF16)],
                    name="mla_rope_q")
    layer1 = [slots[k] for k in GATHER_L1]
    (y_b, lse), got = flash_fwd(qr, kv, kpe, hosted=hosted_gather(layer1) if ex else None)
    use(GATHER_L1, got if ex else layer1)
    y_ab = jnp.concatenate([y_a, y_b], axis=1)
    h1 = matmul(y_ab, W["l0_w_out"], "nn", F32, add=x, name="l0_w_out")
    h2, ffn0 = _ffn_fwd(h1, n["l0_ffn_norm"], W["l0_ffn_w_up"], W["l0_ffn_conv_w"], n["l0_ffn_conv_b"],
                        W["l0_ffn_w_down"], "l0")
    h3, ple0 = _ple_fwd(h2, p0, W["l0_ple_proj"], n["l0_ple_gate_norm"], W["l0_ple_gate"], "l0")

    (hn1,) = rowwise(_fn_rms, [(h3, D_MODEL, 0)], [n["l1_attn_norm"]], [], [(D_MODEL, BF16)], name="l1_attn_norm")
    zz = matmul(hn1, W["l1_w_in"], "nn", F32, b_shards=N_CHIPS, tn=1536, name="l1_w_in")
    o_ret, ret_states = ret_fwd(zz, cr, sr)
    gate_rows = [(zz, 4096, 2), (o_ret, 4096, 0)]
    (yg,) = rowwise(_fn_ret_gate, gate_rows, [n["l1_ret_norm"]], [], [(4096, BF16)], name="ret_gate")
    h4 = matmul(yg, W["l1_w_out"], "nn", F32, add=h3, name="l1_w_out")
    h5, ffn1 = _ffn_fwd(h4, n["l1_ffn_norm"], W["l1_ffn_w_up"], W["l1_ffn_conv_w"], n["l1_ffn_conv_b"],
                        W["l1_ffn_w_down"], "l1")
    h6, ple1 = _ple_fwd(h5, p1, W["l1_ple_proj"], n["l1_ple_gate_norm"], W["l1_ple_gate"], "l1")

    loss_vec, dh, G["final_norm"] = loss_head(h6, target, n["final_norm"])

    dh, g = _ple_bwd(dh, h5, p1, W["l1_ple_proj"], n["l1_ple_gate_norm"], W["l1_ple_gate"], ple1, "l1")
    G.update({"l1_" + k: v for k, v in g.items()})
    dh, g = _ffn_bwd(dh, h4, n["l1_ffn_norm"], W["l1_ffn_w_up"], W["l1_ffn_conv_w"], n["l1_ffn_conv_b"],
                     W["l1_ffn_w_down"], ffn1, "l1")
    G.update({"l1_" + k: v for k, v in g.items()})

    dyg = matmul(dh, W["l1_w_out"], "nt", F32, name="l1_w_out_dx")
    G["l1_w_out"] = matmul(yg, dh, "tn", BF16, name="l1_w_out_dw")
    (dg, do_ret), (G["l1_ret_norm"],) = rowwise_bwd(_fn_ret_gate, gate_rows, [n["l1_ret_norm"]], [],
                                                   [(dyg, 4096, 0)], [BF16, F32], name="ret_gate_bwd")
    dq, dk, dv = ret_bwd(zz, cr, sr, ret_states, do_ret)
    dzz = jnp.concatenate([dq, dk, dv, dg], axis=1)
    G["l1_w_in"] = matmul(hn1, dzz, "tn", BF16, out_shards=N_CHIPS, tn=1536, name="l1_w_in_dw")
    dhn = matmul(dzz, W["l1_w_in"], "nt", F32, b_shards=N_CHIPS, name="l1_w_in_dx")
    (dh,), (G["l1_attn_norm"],) = rowwise_bwd(_fn_rms, [(h3, D_MODEL, 0)], [n["l1_attn_norm"]], [],
                                             [(dhn, D_MODEL, 0)], [F32], adds=[(dh, D_MODEL, 0)],
                                             name="l1_attn_norm_bwd")

    dh, g = _ple_bwd(dh, h2, p0, W["l0_ple_proj"], n["l0_ple_gate_norm"], W["l0_ple_gate"], ple0, "l0")
    G.update({"l0_" + k: v for k, v in g.items()})
    dh, g = _ffn_bwd(dh, h1, n["l0_ffn_norm"], W["l0_ffn_w_up"], W["l0_ffn_conv_w"], n["l0_ffn_conv_b"],
                     W["l0_ffn_w_down"], ffn0, "l0")
    G.update({"l0_" + k: v for k, v in g.items()})

    dy_ab = matmul(dh, W["l0_w_out"], "nt", F32, name="l0_w_out_dx")
    G["l0_w_out"] = matmul(y_ab, dh, "tn", BF16, name="l0_w_out_dw")
    sums, landed = {}, {}
    if ex:
        sums.update(zip(REDUCE_L1, ex.pair_sums(REDUCE_L1, by_chip(REDUCE_L1))))
        sums.update(zip(REDUCE_L0, ex.pair_sums(REDUCE_L0, by_chip(REDUCE_L0))))
    (dq, dk, dv, dz, dab, g_a, g_dt, G["l0_gdn_norm"]), got = gdn_bwd(
        qkv, zin, a_row, dt_row, gdn_nw, gdn_states, dy_ab, 0,
        hosted=hosted_scatter([sums[k] for k in REDUCE_L1]) if ex else None)
    landed.update(zip(REDUCE_L1, got))
    G["l0_gdn_A_log"], G["l0_gdn_dt_bias"] = g_a[:, :GDN_HEADS], g_dt[:, :GDN_HEADS]
    dpre, G["l0_gdn_conv"] = gdn_conv_bwd(zin, W["l0_gdn_conv"], jnp.concatenate([dq, dk, dv], axis=1))
    dqkv = conv_transpose(dpre, W["l0_gdn_conv"], BF16, name="gdn_conv_dx")
    dqr = flash_bwd_dq(qr, kv, kpe, y_b, lse, dy_ab, MLA_HEADS)
    (dkv, dkpe), got = flash_bwd_dkv(qr, kv, kpe, y_b, lse, dy_ab, MLA_HEADS,
                                     hosted=hosted_scatter([sums[k] for k in REDUCE_L0]) if ex else None)
    landed.update(zip(REDUCE_L0, got))
    (dq_lin,), _ = rowwise_bwd(_fn_rope_q, [(q_lin, 2048, 0)], [], mla_nd, [(dqr, 2048, 0)], [BF16],
                               name="mla_rope_q_bwd")
    G["l0_mla_w_uq"] = _unprep_w_uq(matmul(cqn, dq_lin, "tn", BF16, name="mla_w_uq_dw"))
    dcqn = matmul(dq_lin, w_uq, "nt", F32, name="mla_w_uq_dx")
    G["l0_mla_w_ukv"] = _unprep_w_ukv(matmul(ckvn, dkv, "tn", BF16, name="mla_w_ukv_dw"))
    dckvn = matmul(dkv, w_ukv, "nt", F32, name="mla_w_ukv_dx")
    (dcq, dckv, dkr), (G["l0_mla_q_norm"], G["l0_mla_kv_norm"]) = rowwise_bwd(
        _fn_mla_pre, mla_rows, [n["l0_mla_q_norm"], n["l0_mla_kv_norm"]], mla_nd,
        [(dcqn, MLA_Q_RANK, 0), (dckvn, MLA_KV_RANK, 0), (dkpe, LANES, 0)], [BF16, BF16, BF16], name="mla_pre_bwd")
    dzin = jnp.concatenate([dqkv, dz, dcq, dckv, dkr, dab.astype(BF16)], axis=1)
    G["l0_w_in"] = _unprep_w_in0(matmul(hn0, dzin, "tn", BF16, name="l0_w_in_dw", tn=768))
    dhn = matmul(dzin, w_in0, "nt", F32, name="l0_w_in_dx")
    (grad_x,), (G["l0_attn_norm"],) = rowwise_bwd(_fn_rms, [(x, D_MODEL, 0)], [n["l0_attn_norm"]], [],
                                                 [(dhn, D_MODEL, 0)], [F32], adds=[(dh, D_MODEL, 0)],
                                                 name="l0_attn_norm_bwd")
    small = {k: G[k] for k in SMALL}
    if not ex:
        return loss_vec[0, 0], grad_x, dict(zip(BIG, by_chip(BIG))), small
    sums.update(zip(REDUCE_LAST, ex.pair_sums(REDUCE_LAST, by_chip(REDUCE_LAST))))
    landed.update(zip(REDUCE_LAST, scatter_chips([sums[k] for k in REDUCE_LAST])))
    return loss_vec[0, 0], grad_x, ex.finish(sums, landed), small


HBM = pl.BlockSpec(memory_space=pltpu.HBM)
VMEM = pl.BlockSpec(memory_space=pltpu.VMEM)


def _place():
    return lax.axis_index("x"), lax.axis_index("y"), lax.axis_index("c")


def _other_chips(x, y):
    return [(1 - x, y), (x, 1 - y), (1 - x, 1 - y)]


def _comm_call(body, *, name, out_shape, in_specs, out_specs, scratch_shapes):
    return pl.pallas_call(body, name=name, out_shape=out_shape, in_specs=in_specs, out_specs=out_specs,
                          scratch_shapes=list(scratch_shapes),
                          compiler_params=pltpu.CompilerParams(vmem_limit_bytes=VMEM_LIMIT_MB << 20))


def _inplace_comm_call(body, bufs, *, name, n_sems):
    n = len(bufs)
    return pl.pallas_call(body, name=name, out_shape=[jax.ShapeDtypeStruct(b.shape, b.dtype) for b in bufs],
                          in_specs=[HBM] * n, out_specs=[HBM] * n, input_output_aliases={i: i for i in range(n)},
                          scratch_shapes=[pltpu.SemaphoreType.DMA((n_sems,)), pltpu.SemaphoreType.DMA((n_sems,))],
                          compiler_params=pltpu.CompilerParams(vmem_limit_bytes=VMEM_LIMIT_MB << 20))(*bufs)


def all_gather_chips(bufs):
    n_sems, start, finish = _gather_phase(len(bufs))
    n = len(bufs)

    def body(*refs):
        outs, send_sems, recv_sems = refs[n:2 * n], refs[2 * n], refs[2 * n + 1]
        start(None, outs, send_sems, recv_sems)
        finish(None, outs, send_sems, recv_sems)

    return _inplace_comm_call(body, bufs, name="all_gather_chips", n_sems=n_sems)


def _gather_phase(n):
    def plan(outs, send_sems, recv_sems):
        x, y, c = _place()

        def copy(w, k, chip, hc, to):
            half = outs[w].shape[1] // 2
            rows = outs[w].at[2 * chip[0] + chip[1], pl.ds(hc * half, half), :]
            return pltpu.make_async_remote_copy(src_ref=rows, dst_ref=rows, send_sem=send_sems.at[6 * w + k],
                                                recv_sem=recv_sems.at[6 * w + k], device_id=to, device_id_type=MESH)

        first = [[copy(w, k, (x, y), c, (*chip, c)) for k, chip in enumerate(_other_chips(x, y))] for w in range(n)]
        passed = [[copy(w, 3 + k, chip, c, (x, y, 1 - c)) for k, chip in enumerate(_other_chips(x, y))]
                  for w in range(n)]
        return copy, first, passed, (x, y, c)

    def start(_, outs, send_sems, recv_sems):
        _, first, _, _ = plan(outs, send_sems, recv_sems)
        for w in range(n):
            for cp in first[w]:
                cp.start()

    def finish(_, outs, send_sems, recv_sems):
        copy, first, passed, (x, y, c) = plan(outs, send_sems, recv_sems)
        chips = _other_chips(x, y)
        for w in range(n):
            for k, chip in enumerate(chips):
                copy(w, k, chip, c, (x, y, c)).wait_recv()
                passed[w][k].start()
        for w in range(n):
            for k, chip in enumerate(chips):
                copy(w, 3 + k, chip, 1 - c, (x, y, c)).wait_recv()
        for w in range(n):
            for cp in first[w] + passed[w]:
                cp.wait_send()

    return 6 * n, start, finish


def hosted_gather(bufs):
    n_sems, start, finish = _gather_phase(len(bufs))
    return Hosted(bufs, [jax.ShapeDtypeStruct(b.shape, b.dtype) for b in bufs], {i: i for i in range(len(bufs))},
                  n_sems, start, finish)


def pair_swap_halves(gs):
    n = len(gs)

    def body(*refs):
        g_refs, o_refs, send_sems, recv_sems = refs[:n], refs[n:2 * n], refs[2 * n], refs[2 * n + 1]
        x, y, c = _place()
        copies = []
        for w in range(n):
            half = g_refs[w].shape[1] // 2
            copies.append(pltpu.make_async_remote_copy(
                src_ref=g_refs[w].at[:, pl.ds((1 - c) * half, half), :], dst_ref=o_refs[w], send_sem=send_sems.at[w],
                recv_sem=recv_sems.at[w], device_id=(x, y, 1 - c), device_id_type=MESH))
        for cp in copies:
            cp.start()
        for cp in copies:
            cp.wait()

    return _comm_call(body, name="pair_swap_halves",
                      out_shape=[jax.ShapeDtypeStruct((N_CHIPS, g.shape[1] // 2, g.shape[2]), g.dtype) for g in gs],
                      in_specs=[HBM] * n, out_specs=[HBM] * n,
                      scratch_shapes=[pltpu.SemaphoreType.DMA((n,)), pltpu.SemaphoreType.DMA((n,))])(*gs)


def scatter_chips(ps):
    n = len(ps)
    n_sems, start, finish = _scatter_phase(n)

    def body(*refs):
        p_refs, o_refs, send_sems, recv_sems = refs[:n], refs[n:2 * n], refs[2 * n], refs[2 * n + 1]
        start(p_refs, o_refs, send_sems, recv_sems)
        finish(p_refs, o_refs, send_sems, recv_sems)

    return _comm_call(body, name="scatter_chips", out_shape=_scatter_shapes(ps), in_specs=[HBM] * n, out_specs=[HBM] * n,
                      scratch_shapes=[pltpu.SemaphoreType.DMA((n_sems,)), pltpu.SemaphoreType.DMA((n_sems,))])(*ps)


def _scatter_shapes(ps):
    return [jax.ShapeDtypeStruct((3,) + p.shape[1:], p.dtype) for p in ps]


def _scatter_phase(n):
    def copies(p_refs, o_refs, send_sems, recv_sems):
        x, y, c = _place()
        return [pltpu.make_async_remote_copy(src_ref=p_refs[w].at[2 * chip[0] + chip[1]], dst_ref=o_refs[w].at[k],
                                             send_sem=send_sems.at[3 * w + k], recv_sem=recv_sems.at[3 * w + k],
                                             device_id=(*chip, c), device_id_type=MESH)
                for w in range(n) for k, chip in enumerate(_other_chips(x, y))]

    def start(*refs):
        for cp in copies(*refs):
            cp.start()

    def finish(*refs):
        for cp in copies(*refs):
            cp.wait()

    return 3 * n, start, finish


def hosted_scatter(ps):
    n_sems, start, finish = _scatter_phase(len(ps))
    return Hosted(ps, _scatter_shapes(ps), {}, n_sems, start, finish)


def pair_join_halves(rs):
    n = len(rs)

    def body(*refs):
        outs, send_sems, recv_sems = refs[n:2 * n], refs[2 * n], refs[2 * n + 1]
        x, y, c = _place()
        copies = []
        for w in range(n):
            half = outs[w].shape[0] // 2
            rows = outs[w].at[pl.ds(c * half, half), :]
            copies.append(pltpu.make_async_remote_copy(src_ref=rows, dst_ref=rows, send_sem=send_sems.at[w],
                                                       recv_sem=recv_sems.at[w], device_id=(x, y, 1 - c),
                                                       device_id_type=MESH))
        for cp in copies:
            cp.start()
        for cp in copies:
            cp.wait()

    return _inplace_comm_call(body, rs, name="pair_join_halves", n_sems=n)


def all_reduce_small(v, name):
    n, L = v.shape
    n_dev = 8

    def body(v_ref, out_ref, buf, send_sems, recv_sems):
        x, y, c = _place()
        me = 4 * x + 2 * y + c
        buf[me] = v_ref[...]

        def copy(k, slot, peer):
            return pltpu.make_async_remote_copy(src_ref=v_ref, dst_ref=buf.at[slot], send_sem=send_sems.at[k],
                                                recv_sem=recv_sems.at[slot],
                                                device_id=(peer // 4, (peer // 2) % 2, peer % 2), device_id_type=MESH)

        sends = [copy(k - 1, me, (me + k) % n_dev) for k in range(1, n_dev)]
        for cp in sends:
            cp.start()
        for k in range(1, n_dev):
            src = (me + k) % n_dev
            copy(0, src, src).wait_recv()
        for cp in sends:
            cp.wait_send()
        acc = buf[0]
        for s in range(1, n_dev):
            acc = acc + buf[s]
        out_ref[...] = acc

    return _comm_call(body, name=name, out_shape=jax.ShapeDtypeStruct((n, L), v.dtype), in_specs=[VMEM], out_specs=VMEM,
                      scratch_shapes=[pltpu.VMEM((n_dev, n, L), v.dtype), pltpu.SemaphoreType.DMA((n_dev - 1,)),
                                      pltpu.SemaphoreType.DMA((n_dev,))])(v)


BF16_ROWS = 16


def _rows_tile(n, row_bytes, budget=1 << 20, mult=SUBLANES):
    best = mult if n % mult == 0 else n
    for t in range(mult, n + 1, mult):
        if n % t == 0 and t * row_bytes <= budget:
            best = t
    return best


def _scalars(*vals):
    return jnp.stack([jnp.asarray(v, jnp.int32) for v in vals])


def cast_to_slot(w, chip, name):
    r, c = w.shape
    tb = _rows_tile(r, c * 4, mult=BF16_ROWS)

    def body(s_ref, w_ref, o_ref):
        o_ref[...] = w_ref[...].astype(BF16)

    spec = pltpu.PrefetchScalarGridSpec(
        num_scalar_prefetch=1, grid=(r // tb,), in_specs=[pl.BlockSpec((tb, c), lambda i, s: (i, 0))],
        out_specs=pl.BlockSpec((None, tb, c), lambda i, s: (s[0], i, 0)))
    return pl.pallas_call(body, name=name, grid_spec=spec, out_shape=jax.ShapeDtypeStruct((N_CHIPS, r, c), BF16),
                          compiler_params=pltpu.CompilerParams(dimension_semantics=("parallel",)))(_scalars(chip), w)


def pair_add(g, got, c, name):
    _, r, w = g.shape
    half = r // 2
    tb = _rows_tile(half, w * 4, mult=BF16_ROWS)
    nb = half // tb

    def body(c_ref, g_ref, got_ref, o_ref):
        o_ref[...] = (g_ref[...].astype(F32) + got_ref[...].astype(F32)).astype(o_ref.dtype)

    spec = pltpu.PrefetchScalarGridSpec(
        num_scalar_prefetch=1, grid=(N_CHIPS, nb),
        in_specs=[pl.BlockSpec((None, tb, w), lambda s, i, c_ref: (s, c_ref[0] * nb + i, 0)),
                  pl.BlockSpec((None, tb, w), lambda s, i, c_ref: (s, i, 0))],
        out_specs=pl.BlockSpec((None, tb, w), lambda s, i, c_ref: (s, i, 0)))
    return pl.pallas_call(body, name=name, grid_spec=spec, out_shape=jax.ShapeDtypeStruct((N_CHIPS, half, w), BF16),
                          compiler_params=pltpu.CompilerParams(dimension_semantics=("parallel", "parallel")))(
        _scalars(c), g, got)


def chip_add(p, got, chip, c, name):
    _, h, w = p.shape
    tb = _rows_tile(h, w * 4, mult=BF16_ROWS)
    nb = h // tb

    def body(s_ref, p_ref, got_ref, o_ref):
        acc = p_ref[...].astype(F32)
        for k in range(3):
            acc = acc + got_ref[k].astype(F32)
        o_ref[...] = acc

    spec = pltpu.PrefetchScalarGridSpec(
        num_scalar_prefetch=1, grid=(nb,),
        in_specs=[pl.BlockSpec((None, tb, w), lambda i, s: (s[0], i, 0)),
                  pl.BlockSpec((3, tb, w), lambda i, s: (0, i, 0))],
        out_specs=pl.BlockSpec((tb, w), lambda i, s: (s[1] * nb + i, 0)))
    return pl.pallas_call(body, name=name, grid_spec=spec, out_shape=jax.ShapeDtypeStruct((2 * h, w), F32),
                          compiler_params=pltpu.CompilerParams(dimension_semantics=("parallel",)))(
        _scalars(chip, c), p, got)


def adamw(w, g, m, v, name):
    r, c = w.shape
    tr = _rows_tile(r, c * 4)

    def body(w_ref, g_ref, m_ref, v_ref, d_ref, m_out, v_out):
        gg = g_ref[...]
        m2 = ADAM_B1 * m_ref[...] + (1.0 - ADAM_B1) * gg
        v2 = ADAM_B2 * v_ref[...] + (1.0 - ADAM_B2) * jnp.square(gg)
        m_hat = m2 / (1.0 - ADAM_B1 ** ADAM_STEP)
        v_hat = v2 / (1.0 - ADAM_B2 ** ADAM_STEP)
        d_ref[...] = -ADAM_LR * (m_hat / (jnp.sqrt(v_hat) + ADAM_EPS) + ADAM_WD * w_ref[...])
        m_out[...] = m2
        v_out[...] = v2

    blk = pl.BlockSpec((tr, c), lambda i: (i, 0))
    return _pcall(body, name=name, grid=(r // tr,), in_specs=[blk] * 4, out_specs=[blk] * 3,
                  out_shape=[jax.ShapeDtypeStruct((r, c), F32)] * 3, dims=("parallel",))(w, g, m, v)


WEIGHTS = ["l0_attn_norm", "l0_w_in", "l0_gdn_conv", "l0_gdn_A_log", "l0_gdn_dt_bias", "l0_gdn_norm", "l0_mla_q_norm",
           "l0_mla_w_uq", "l0_mla_kv_norm", "l0_mla_w_ukv", "l0_w_out", "l0_ffn_norm", "l0_ffn_w_up", "l0_ffn_conv_w",
           "l0_ffn_conv_b", "l0_ffn_w_down", "l0_ple_proj", "l0_ple_gate_norm", "l0_ple_gate", "l1_attn_norm",
           "l1_w_in", "l1_ret_norm", "l1_w_out", "l1_ffn_norm", "l1_ffn_w_up", "l1_ffn_conv_w", "l1_ffn_conv_b",
           "l1_ffn_w_down", "l1_ple_proj", "l1_ple_gate_norm", "l1_ple_gate", "final_norm"]
COL_SHARDED = ["l0_w_in", "l0_mla_w_uq", "l0_mla_w_ukv", "l0_ffn_w_up", "l0_ple_proj", "l1_w_in", "l1_ffn_w_up",
               "l1_ple_proj"]
ROW_SHARDED = ["l0_w_out", "l0_ffn_w_down", "l0_ple_gate", "l1_w_out", "l1_ffn_w_down", "l1_ple_gate"]
BIG = [k for k in WEIGHTS if k in COL_SHARDED or k in ROW_SHARDED]
SMALL_SHARDED = ["l0_gdn_conv", "l0_ffn_conv_w", "l1_ffn_conv_w"]
SMALL = [k for k in WEIGHTS if k not in BIG]
KEPT_SHARDED = ["l0_ffn_w_up", "l0_ple_proj", "l1_w_in", "l1_ffn_w_up", "l1_ple_proj"]
GATHER_FIRST = ["l0_w_in", "l0_mla_w_uq", "l0_mla_w_ukv", "l0_w_out"]
GATHER_L0 = ["l0_ffn_w_up", "l0_ffn_w_down", "l0_ple_proj", "l0_ple_gate"]
GATHER_L1 = [k for k in BIG if k.startswith("l1_")]
REDUCE_L1 = GATHER_L1
REDUCE_L0 = GATHER_L0 + ["l0_w_out"]
REDUCE_LAST = ["l0_w_in", "l0_mla_w_uq", "l0_mla_w_ukv"]


class Exchange:
    def __init__(self, chip, core):
        self.chip, self.core = chip, core

    def gather(self, bufs):
        return all_gather_chips(bufs)

    def pair_sums(self, names, grads):
        return [pair_add(g, got, self.core, "rs_pair_add_" + k)
                for k, g, got in zip(names, grads, pair_swap_halves(grads))]

    def finish(self, sums, landed):
        halves = [chip_add(sums[k], landed[k], self.chip, self.core, "rs_chip_add_" + k) for k in BIG]
        return dict(zip(BIG, pair_join_halves(halves)))


def _cols_to_full(s):
    j, k, n = s.shape
    return jnp.transpose(s, (1, 0, 2)).reshape(k, j * n)


def _full_to_cols(g):
    k, n4 = g.shape
    return jnp.transpose(g.reshape(k, N_CHIPS, n4 // N_CHIPS), (1, 0, 2))


def _pack_small(vals):
    flat = jnp.concatenate([v.astype(F32).reshape(-1) for v in vals])
    align = SUBLANES * LANES
    flat = jnp.pad(flat, (0, -flat.shape[0] % align))
    return flat.reshape(-1, LANES)


def _unpack_small(rows, shapes):
    flat = rows.reshape(-1)
    out, off = [], 0
    for shp in shapes:
        n = int(np.prod(shp))
        out.append(flat[off:off + n].reshape(shp))
        off += n
    return out


INPUTS = (["x", "p", "positions"] + WEIGHTS + ["loss_target"] + ["m_" + k for k in WEIGHTS]
          + ["v_" + k for k in WEIGHTS])


def kernel(
        x, p, positions, l0_attn_norm, l0_w_in, l0_gdn_conv, l0_gdn_A_log, l0_gdn_dt_bias, l0_gdn_norm, l0_mla_q_norm,
        l0_mla_w_uq, l0_mla_kv_norm, l0_mla_w_ukv, l0_w_out, l0_ffn_norm, l0_ffn_w_up, l0_ffn_conv_w, l0_ffn_conv_b,
        l0_ffn_w_down, l0_ple_proj, l0_ple_gate_norm, l0_ple_gate, l1_attn_norm, l1_w_in, l1_ret_norm, l1_w_out,
        l1_ffn_norm, l1_ffn_w_up, l1_ffn_conv_w, l1_ffn_conv_b, l1_ffn_w_down, l1_ple_proj, l1_ple_gate_norm,
        l1_ple_gate, final_norm, loss_target, m_l0_attn_norm, m_l0_w_in, m_l0_gdn_conv, m_l0_gdn_A_log,
        m_l0_gdn_dt_bias, m_l0_gdn_norm, m_l0_mla_q_norm, m_l0_mla_w_uq, m_l0_mla_kv_norm, m_l0_mla_w_ukv, m_l0_w_out,
        m_l0_ffn_norm, m_l0_ffn_w_up, m_l0_ffn_conv_w, m_l0_ffn_conv_b, m_l0_ffn_w_down, m_l0_ple_proj,
        m_l0_ple_gate_norm, m_l0_ple_gate, m_l1_attn_norm, m_l1_w_in, m_l1_ret_norm, m_l1_w_out, m_l1_ffn_norm,
        m_l1_ffn_w_up, m_l1_ffn_conv_w, m_l1_ffn_conv_b, m_l1_ffn_w_down, m_l1_ple_proj, m_l1_ple_gate_norm,
        m_l1_ple_gate, m_final_norm, v_l0_attn_norm, v_l0_w_in, v_l0_gdn_conv, v_l0_gdn_A_log, v_l0_gdn_dt_bias,
        v_l0_gdn_norm, v_l0_mla_q_norm, v_l0_mla_w_uq, v_l0_mla_kv_norm, v_l0_mla_w_ukv, v_l0_w_out, v_l0_ffn_norm,
        v_l0_ffn_w_up, v_l0_ffn_conv_w, v_l0_ffn_conv_b, v_l0_ffn_w_down, v_l0_ple_proj, v_l0_ple_gate_norm,
        v_l0_ple_gate, v_l1_attn_norm, v_l1_w_in, v_l1_ret_norm, v_l1_w_out, v_l1_ffn_norm, v_l1_ffn_w_up,
        v_l1_ffn_conv_w, v_l1_ffn_conv_b, v_l1_ffn_w_down, v_l1_ple_proj, v_l1_ple_gate_norm, v_l1_ple_gate,
        v_final_norm):
    given = locals()
    a = {k: given[k] for k in INPUTS}
    x_i, y_i, c_i = _place()
    chip = 2 * x_i + y_i
    shard_shapes = {k: a[k].shape for k in WEIGHTS}

    slots = {k: cast_to_slot(a[k], chip, "cast_" + k) for k in BIG}
    W = {}
    placed = []
    for k in SMALL_SHARDED:
        r, c = shard_shapes[k]
        mine = jnp.where(c_i == 0, a[k], jnp.zeros_like(a[k]))
        placed.append(lax.dynamic_update_slice(jnp.zeros((r, N_CHIPS * c), F32), mine, (0, chip * c)))
    full_small = _unpack_small(all_reduce_small(_pack_small(placed), "gather_small_weights"),
                               [p_.shape for p_ in placed])
    for k in SMALL:
        W[k] = a[k]
    W.update(dict(zip(SMALL_SHARDED, full_small)))

    loss_part, grad_x, grads, G = local_step(a["x"][0], a["p"][:, 0], a["positions"][0], a["loss_target"][0], slots, W,
                                             Exchange(chip, c_i))
    loss = lax.psum(loss_part, ("x", "y", "c"))
    deltas, new_m, new_v = {}, {}, {}
    for k in BIG:
        deltas[k], new_m[k], new_v[k] = adamw(a[k], grads[k], a["m_" + k], a["v_" + k], "adamw_" + k)

    small_full = [G[k].reshape(-1) for k in SMALL]
    summed = _unpack_small(all_reduce_small(_pack_small(small_full), "reduce_small_grads"),
                           [G[k].shape for k in SMALL])
    for k, g in zip(SMALL, summed):
        if k in SMALL_SHARDED:
            r, c = shard_shapes[k]
            g = lax.dynamic_slice(g.reshape(r, N_CHIPS * c), (0, chip * c), (r, c))
        grads[k] = g.reshape(shard_shapes[k])
    packed = [_pack_small([d[k] for k in SMALL]) for d in (
        {k: a[k] for k in SMALL}, grads, {k: a["m_" + k] for k in SMALL}, {k: a["v_" + k] for k in SMALL})]
    outs = adamw(*packed, "adamw_small")
    shapes = [shard_shapes[k] for k in SMALL]
    for d, rows in zip((deltas, new_m, new_v), outs):
        d.update(dict(zip(SMALL, _unpack_small(rows, shapes))))

    return (loss, grad_x[None], *[grads[k] for k in WEIGHTS], *[deltas[k] for k in WEIGHTS],
            *[new_m[k] for k in WEIGHTS], *[new_v[k] for k in WEIGHTS])
```

```python
import functools
import math

import numpy as np
import jax
import jax.numpy as jnp
from jax import lax
from jax.experimental import pallas as pl
from jax.experimental.pallas import tpu as pltpu

F32, BF16 = jnp.float32, jnp.bfloat16
HI = lax.Precision.HIGHEST
MESH = pl.DeviceIdType.MESH

NORM_EPS = 1e-6
ROPE_THETA = 10000.0
D_MODEL = 2048
PLE_DIM = 256
GDN_HEADS, GDN_DK, GDN_DV, GDN_CONV = 8, 128, 128, 4
MLA_HEADS, MLA_Q_RANK, MLA_KV_RANK, MLA_NOPE, MLA_ROPE, MLA_V = 8, 512, 512, 128, 64, 128
RET_HEADS, RET_DK, RET_DV = 8, 256, 512
D_FF, FFN_CONV = 5632, 3
ADAM_LR, ADAM_B1, ADAM_B2, ADAM_EPS, ADAM_WD, ADAM_STEP = 0.001, 0.9, 0.999, 1e-08, 0.01, 10

LANES = 128
SUBLANES = 8
CHUNK = 128
N_CHIPS = 4
VMEM_LIMIT_MB = 56

ZIN_QKV, ZIN_Z, ZIN_CQ, ZIN_CKV, ZIN_KR, ZIN_AB, ZIN_W = 0, 3072, 4096, 4608, 5120, 5248, 5376


class Hosted:
    def __init__(self, inputs, out_shapes, aliases, n_sems, start, finish):
        self.inputs, self.out_shapes, self.aliases, self.n_sems = list(inputs), list(out_shapes), dict(aliases), n_sems
        self.start, self.finish = start, finish


def _pcall(body, *, name, out_shape, grid=(), in_specs=None, out_specs=None, scratch_shapes=(), dims=None,
           hosted=None):
    params = dict(vmem_limit_bytes=VMEM_LIMIT_MB << 20)
    if dims is not None:
        params["dimension_semantics"] = dims
    if hosted is None:
        return pl.pallas_call(body, name=name, out_shape=out_shape, grid=grid, in_specs=in_specs, out_specs=out_specs,
                              scratch_shapes=list(scratch_shapes), compiler_params=pltpu.CompilerParams(**params))
    single = not isinstance(out_shape, (list, tuple))
    out_shape = [out_shape] if single else list(out_shape)
    out_specs = [out_specs] if single else list(out_specs)
    n_in, n_out, n_scr = len(in_specs), len(out_shape), len(scratch_shapes)
    h_in, h_out = len(hosted.inputs), len(hosted.out_shapes)
    hbm = pl.BlockSpec(memory_space=pltpu.HBM)

    def hosting_body(*refs):
        ins, h_ins = refs[:n_in], refs[n_in:n_in + h_in]
        o0 = n_in + h_in
        outs, h_outs = refs[o0:o0 + n_out], refs[o0 + n_out:o0 + n_out + h_out]
        s0 = o0 + n_out + h_out
        scr, (send_sems, recv_sems) = refs[s0:s0 + n_scr], refs[s0 + n_scr:]
        ids = [pl.program_id(d) for d in range(len(grid))]
        first = functools.reduce(lambda u, v: u & v, [i == 0 for i in ids])
        last = functools.reduce(lambda u, v: u & v, [i == g - 1 for i, g in zip(ids, grid)])

        @pl.when(first)
        def _():
            hosted.start(h_ins, h_outs, send_sems, recv_sems)

        body(*ins, *outs, *scr)

        @pl.when(last)
        def _():
            hosted.finish(h_ins, h_outs, send_sems, recv_sems)

    params["dimension_semantics"] = ("arbitrary",) * len(grid)
    call = pl.pallas_call(
        hosting_body, name=name, out_shape=out_shape + hosted.out_shapes, grid=grid,
        in_specs=list(in_specs) + [hbm] * h_in, out_specs=out_specs + [hbm] * h_out,
        scratch_shapes=list(scratch_shapes) + [pltpu.SemaphoreType.DMA((hosted.n_sems,)),
                                               pltpu.SemaphoreType.DMA((hosted.n_sems,))],
        input_output_aliases={n_in + i: n_out + o for i, o in hosted.aliases.items()},
        compiler_params=pltpu.CompilerParams(**params))

    def run(*args):
        res = call(*args, *hosted.inputs)
        main = res[:n_out]
        return (main[0] if single else main), list(res[n_out:])

    return run


def _tile(n, target, mult=LANES):
    best = None
    for t in range(mult, min(n, target) + 1, mult):
        if n % t == 0:
            best = t
    return best or n


_DN = {"nn": (((1,), (0,)), ((), ())), "nt": (((1,), (1,)), ((), ())), "tn": (((0,), (0,)), ((), ()))}


def matmul(a, b, mode, out_dtype, *, name, add=None, b_shards=1, out_shards=1, tm=512, tn=1024, tk=2048):
    bs = b.shape[-2:]
    if mode == "nn":
        (M, K), (K2, N) = a.shape, (bs[0], bs[1] * b_shards)
    elif mode == "nt":
        (M, K), (N, K2) = a.shape, (bs[0], bs[1] * b_shards)
    else:
        (K, M), (K2, N) = a.shape, bs
    assert K == K2, (name, a.shape, b.shape)
    n_sh = N // max(b_shards if mode == "nn" else 1, out_shards)
    k_sh = K // (b_shards if mode == "nt" else 1)
    tm, tn, tk = _tile(M, tm), _tile(n_sh, tn), _tile(k_sh, tk)
    nk = K // tk
    nbn, nbk = n_sh // tn, k_sh // tk
    dn = _DN[mode]
    has_add = add is not None
    a_bytes, b_bytes = a.size * a.dtype.itemsize, b.size * b.dtype.itemsize
    i_outer = nk > 1 or a_bytes + (M // tm) * b_bytes <= b_bytes + (N // tn) * a_bytes

    def ij(g0, g1):
        return (g0, g1) if i_outer else (g1, g0)

    def body(*refs):
        a_ref, b_ref = refs[:2]
        add_ref = refs[2] if has_add else None
        o_ref = refs[3 if has_add else 2]
        part = lax.dot_general(a_ref[...].astype(BF16), b_ref[...].astype(BF16), dn, preferred_element_type=F32)

        def finish(r):
            if has_add:
                r = r + add_ref[...]
            o_ref[...] = r.astype(out_dtype)

        if nk == 1:
            finish(part)
            return
        acc = refs[-1]
        k = pl.program_id(2)

        @pl.when(k == 0)
        def _():
            acc[...] = part

        @pl.when(k > 0)
        def _():
            acc[...] += part

        @pl.when(k == nk - 1)
        def _():
            finish(acc[...])

    def spec(block, fn):
        return pl.BlockSpec(block, lambda g0, g1, k: fn(*ij(g0, g1), k))

    if mode == "tn":
        a_spec = spec((tk, tm), lambda i, j, k: (k, i))
    else:
        a_spec = spec((tm, tk), lambda i, j, k: (i, k))
    if mode == "nt":
        if b_shards > 1:
            b_spec = spec((None, tn, tk), lambda i, j, k: (k // nbk, j, k % nbk))
        else:
            b_spec = spec((tn, tk), lambda i, j, k: (j, k))
    elif b_shards > 1:
        b_spec = spec((None, tk, tn), lambda i, j, k: (j // nbn, k, j % nbn))
    else:
        b_spec = spec((tk, tn), lambda i, j, k: (k, j))
    in_specs = [a_spec, b_spec]
    args = [a, b]
    if has_add:
        in_specs.append(spec((tm, tn), lambda i, j, k: (i, j)))
        args.append(add)
    if out_shards > 1:
        out_spec = spec((None, tm, tn), lambda i, j, k: (j // nbn, i, j % nbn))
        out_shape = jax.ShapeDtypeStruct((out_shards, M, n_sh), out_dtype)
    else:
        out_spec = spec((tm, tn), lambda i, j, k: (i, j))
        out_shape = jax.ShapeDtypeStruct((M, N), out_dtype)
    gi, gj = M // tm, N // tn
    return _pcall(body, name=name, out_shape=out_shape, grid=(gi, gj, nk) if i_outer else (gj, gi, nk),
                  in_specs=in_specs, out_specs=out_spec,
                  scratch_shapes=[pltpu.VMEM((tm, tn), F32)] if nk > 1 else [],
                  dims=("parallel", "parallel", "arbitrary"))(*args)


def _row_spec(tr, w, c):
    return pl.BlockSpec((tr, w), lambda i: (i, c))


def _full_spec(arr):
    return pl.BlockSpec(arr.shape, lambda i: (0,) * arr.ndim)


def rowwise(fn, rows, params, nd_rows, outs, *, name, tr=256):
    S = rows[0][0].shape[0]
    tr = min(tr, S)
    n_in = len(rows) + len(params) + len(nd_rows)

    def body(*refs):
        res = fn(*[x[...] for x in refs[:n_in]])
        for o_ref, v in zip(refs[n_in:], res):
            o_ref[...] = v.astype(o_ref.dtype)

    return _pcall(body, name=name, grid=(S // tr,),
                  in_specs=([_row_spec(tr, w, c) for (_, w, c) in rows] + [_full_spec(q) for q in params]
                            + [_row_spec(tr, w, c) for (_, w, c) in nd_rows]),
                  out_specs=[_row_spec(tr, w, 0) for (w, _) in outs],
                  out_shape=[jax.ShapeDtypeStruct((S, w), dt) for (w, dt) in outs],
                  dims=("parallel",))(*[r[0] for r in rows], *params, *[r[0] for r in nd_rows])


def rowwise_bwd(fn, rows, params, nd_rows, cts, d_dtypes, *, name, adds=None, tr=256):
    S = rows[0][0].shape[0]
    tr = min(tr, S)
    n_r, n_p, n_n, n_c = len(rows), len(params), len(nd_rows), len(cts)
    adds = adds or [None] * n_r
    add_list = [a for a in adds if a is not None]
    n_a = len(add_list)
    d_dtypes = [dt if isinstance(dt, (list, tuple)) else (dt,) for dt in d_dtypes]
    n_d = sum(len(dt) for dt in d_dtypes)

    def body(*refs):
        it = iter(refs)
        r = [next(it)[...] for _ in range(n_r)]
        p = [next(it)[...] for _ in range(n_p)]
        nd = [next(it)[...] for _ in range(n_n)]
        c = [next(it)[...] for _ in range(n_c)]
        ad = [next(it)[...] for _ in range(n_a)]
        d_row_refs = [[next(it) for _ in dts] for dts in d_dtypes]
        d_par_refs = [next(it) for _ in range(n_p)]
        outs, vjp = jax.vjp(lambda *dp: fn(*dp, *nd), *r, *p)
        g = vjp(tuple(ci.astype(o.dtype) for ci, o in zip(c, outs)))
        ai = 0
        for k in range(n_r):
            gk = g[k].astype(F32)
            if adds[k] is not None:
                gk = gk + ad[ai].astype(F32)
                ai += 1
            for ref in d_row_refs[k]:
                ref[...] = gk.astype(ref.dtype)

        @pl.when(pl.program_id(0) == 0)
        def _():
            for ref in d_par_refs:
                ref[...] = jnp.zeros_like(ref)

        for k in range(n_p):
            d_par_refs[k][...] += g[n_r + k].astype(F32)

    in_specs = ([_row_spec(tr, w, c) for (_, w, c) in rows] + [_full_spec(q) for q in params]
                + [_row_spec(tr, w, c) for (_, w, c) in nd_rows] + [_row_spec(tr, w, c) for (_, w, c) in cts]
                + [_row_spec(tr, w, c) for (_, w, c) in add_list])
    out_specs = ([_row_spec(tr, w, 0) for (_, w, _), dts in zip(rows, d_dtypes) for _ in dts]
                 + [_full_spec(q) for q in params])
    out_shape = ([jax.ShapeDtypeStruct((S, w), dt) for (_, w, _), dts in zip(rows, d_dtypes) for dt in dts]
                 + [jax.ShapeDtypeStruct(q.shape, F32) for q in params])
    res = _pcall(body, name=name, grid=(S // tr,), in_specs=in_specs, out_specs=out_specs, out_shape=out_shape,
                 dims=("arbitrary",))(*[r[0] for r in rows], *params, *[r[0] for r in nd_rows],
                                      *[r[0] for r in cts], *[r[0] for r in add_list])
    d_rows, i = [], 0
    for dts in d_dtypes:
        d_rows.append(res[i] if len(dts) == 1 else tuple(res[i:i + len(dts)]))
        i += len(dts)
    return d_rows, res[n_d:]


def _rms(x, g):
    x = x.astype(F32)
    return x * lax.rsqrt(jnp.mean(x * x, axis=-1, keepdims=True) + NORM_EPS) * g


def _fn_rms(x, g):
    return (_rms(x, g),)


def _sigmoid(x):
    return 1.0 / (1.0 + jnp.exp(-x))


def _silu(x):
    return x * _sigmoid(x)


def _softplus(x):
    return jnp.maximum(x, 0.0) + jnp.log(1.0 + jnp.exp(-jnp.abs(x)))


def _fn_ple(h, pp, gl):
    return (h.astype(F32) + pp.astype(F32) * _sigmoid(gl.astype(F32)),)


def _fn_ple_terms(pp, gl):
    return (pp.astype(F32) * _sigmoid(gl.astype(F32)),)


def _rot_half_matrix():
    half = MLA_ROPE // 2
    r = lax.broadcasted_iota(jnp.int32, (LANES, LANES), 0)
    c = lax.broadcasted_iota(jnp.int32, (LANES, LANES), 1)
    plus = (c == r + half) & (r < half)
    minus = (r == c + half) & (c < half)
    return jnp.where(plus, 1.0, 0.0) - jnp.where(minus, 1.0, 0.0)


def _rope_pad(x, cosp, sinp):
    return x * cosp + jnp.dot(x, _rot_half_matrix(), precision=HI, preferred_element_type=F32) * sinp


def _fn_mla_pre(cq, ckv, kr, qn_w, kvn_w, cosp, sinp):
    return (_rms(cq, qn_w), _rms(ckv, kvn_w), _rope_pad(kr.astype(F32), cosp, sinp))


def _fn_rope_q(q, cosp, sinp):
    q = q.astype(F32)
    parts = []
    for h in range(MLA_HEADS):
        base = 2 * LANES * h
        parts.append(q[:, base:base + LANES])
        parts.append(_rope_pad(q[:, base + LANES:base + 2 * LANES], cosp, sinp))
    return (jnp.concatenate(parts, axis=1),)


def _fn_ret_gate(g, on, w):
    return (_silu(g.astype(F32)) * (on.astype(F32) * w),)


def _shift_down(cur, halo, s):
    if s == 0:
        return cur
    r = pltpu.roll(cur, s, 0)
    hs = pltpu.roll(halo, s, 0)
    row = lax.broadcasted_iota(jnp.int32, hs.shape, 0)
    first = jnp.where(row < s, hs, r[:SUBLANES])
    return jnp.concatenate([first, r[SUBLANES:]], axis=0)


def _shift_up(cur, halo, s):
    if s == 0:
        return cur
    n = cur.shape[0]
    r = pltpu.roll(cur, n - s, 0)
    hs = pltpu.roll(halo, SUBLANES - s, 0)
    row = lax.broadcasted_iota(jnp.int32, hs.shape, 0)
    last = jnp.where(row >= SUBLANES - s, hs, r[n - SUBLANES:])
    return jnp.concatenate([r[:n - SUBLANES], last], axis=0)


def _prev_halo_spec(tr, tw, col):
    return pl.BlockSpec((SUBLANES, tw), lambda c, i: (jnp.maximum(i * (tr // SUBLANES) - 1, 0), col(c)))


def _conv_taps(cur, halo, w_ref, width):
    taps = [_shift_down(cur, halo, width - 1 - j) for j in range(width)]
    y = taps[0] * w_ref[0:1, :]
    for j in range(1, width):
        y = y + taps[j] * w_ref[j:j + 1, :]
    return y, taps


def gdn_conv_fwd(zin, w, *, tr=512, tw=512):
    S = zin.shape[0]
    tr = min(tr, S)
    width, C = w.shape

    def body(cur_ref, halo_ref, w_ref, o_ref):
        i = pl.program_id(1)
        halo = halo_ref[...] * (i > 0).astype(F32)
        y, _ = _conv_taps(cur_ref[...], halo, w_ref, width)
        o_ref[...] = _silu(y)

    return _pcall(body, name="gdn_conv_fwd", grid=(C // tw, S // tr),
                  in_specs=[pl.BlockSpec((tr, tw), lambda c, i: (i, c)), _prev_halo_spec(tr, tw, lambda c: c),
                            pl.BlockSpec((width, tw), lambda c, i: (0, c))],
                  out_specs=pl.BlockSpec((tr, tw), lambda c, i: (i, c)),
                  out_shape=jax.ShapeDtypeStruct((S, C), F32), dims=("parallel", "arbitrary"))(zin, zin, w)


def gdn_conv_bwd(zin, w, dy, *, tr=512, tw=512):
    S = zin.shape[0]
    tr = min(tr, S)
    width, C = w.shape

    def body(cur_ref, halo_ref, w_ref, dy_ref, da_ref, dw_ref):
        i = pl.program_id(1)
        halo = halo_ref[...] * (i > 0).astype(F32)
        y, taps = _conv_taps(cur_ref[...], halo, w_ref, width)
        sg = _sigmoid(y)
        da = dy_ref[...] * (sg * (1.0 + y * (1.0 - sg)))
        da_ref[...] = da

        @pl.when(i == 0)
        def _():
            dw_ref[...] = jnp.zeros_like(dw_ref)

        for j in range(width):
            dw_ref[j:j + 1, :] += jnp.sum(da * taps[j], axis=0, keepdims=True)

    return _pcall(body, name="gdn_conv_bwd", grid=(C // tw, S // tr),
                  in_specs=[pl.BlockSpec((tr, tw), lambda c, i: (i, c)), _prev_halo_spec(tr, tw, lambda c: c),
                            pl.BlockSpec((width, tw), lambda c, i: (0, c)),
                            pl.BlockSpec((tr, tw), lambda c, i: (i, c))],
                  out_specs=[pl.BlockSpec((tr, tw), lambda c, i: (i, c)),
                             pl.BlockSpec((width, tw), lambda c, i: (0, c))],
                  out_shape=[jax.ShapeDtypeStruct((S, C), F32), jax.ShapeDtypeStruct((width, C), F32)],
                  dims=("parallel", "arbitrary"))(zin, zin, w, dy)


def conv_transpose(dy, w, out_dtype, *, name, tr=512, tw=512):
    if dy.ndim == 2:
        dy = dy[None]
    T, S, C = dy.shape
    tr = min(tr, S)
    width = w.shape[0]
    n_i, nc = S // tr, C // tw

    def body(cur_ref, halo_ref, w_ref, o_ref):
        i = pl.program_id(2)
        halo = halo_ref[...] * (i < n_i - 1).astype(F32)
        cur = cur_ref[...]
        acc = cur * w_ref[width - 1:width, :]
        for s in range(1, width):
            acc = acc + _shift_up(cur, halo, s) * w_ref[width - 1 - s:width - s, :]
        o_ref[...] = acc.astype(out_dtype)

    nxt = pl.BlockSpec((None, SUBLANES, tw),
                       lambda t, c, i: (t, jnp.minimum((i + 1) * (tr // SUBLANES), S // SUBLANES - 1), c))
    return _pcall(body, name=name, grid=(T, nc, n_i),
                  in_specs=[pl.BlockSpec((None, tr, tw), lambda t, c, i: (t, i, c)), nxt,
                            pl.BlockSpec((width, tw), lambda t, c, i: (0, t * nc + c))],
                  out_specs=pl.BlockSpec((tr, tw), lambda t, c, i: (i, t * nc + c)),
                  out_shape=jax.ShapeDtypeStruct((S, T * C), out_dtype),
                  dims=("parallel", "parallel", "arbitrary"))(dy, dy, w)


def ffn_conv_fwd(u, w, b, *, tr=512, tw=512):
    S, C2 = u.shape
    tr = min(tr, S)
    width = w.shape[0]
    half = C2 // 2
    nc = half // tw

    def body(g_ref, gh_ref, u_ref, uh_ref, wg_ref, wu_ref, bg_ref, bu_ref, o_ref):
        i = pl.program_id(1)
        live = (i > 0).astype(F32)
        yg, _ = _conv_taps(g_ref[...], gh_ref[...] * live, wg_ref, width)
        yu, _ = _conv_taps(u_ref[...], uh_ref[...] * live, wu_ref, width)
        o_ref[...] = (_silu(yg + bg_ref[...]) * (yu + bu_ref[...])).astype(o_ref.dtype)

    return _pcall(body, name="ffn_conv_fwd", grid=(nc, S // tr),
                  in_specs=[pl.BlockSpec((tr, tw), lambda c, i: (i, c)), _prev_halo_spec(tr, tw, lambda c: c),
                            pl.BlockSpec((tr, tw), lambda c, i: (i, c + nc)),
                            _prev_halo_spec(tr, tw, lambda c: c + nc),
                            pl.BlockSpec((width, tw), lambda c, i: (0, c)),
                            pl.BlockSpec((width, tw), lambda c, i: (0, c + nc)),
                            pl.BlockSpec((1, tw), lambda c, i: (0, c)), pl.BlockSpec((1, tw), lambda c, i: (0, c + nc))],
                  out_specs=pl.BlockSpec((tr, tw), lambda c, i: (i, c)),
                  out_shape=jax.ShapeDtypeStruct((S, half), BF16),
                  dims=("parallel", "arbitrary"))(u, u, u, u, w, w, b, b)


def ffn_conv_bwd(u, w, b, df, *, tr=512, tw=512):
    S, C2 = u.shape
    tr = min(tr, S)
    width = w.shape[0]
    half = C2 // 2
    nc = half // tw

    def body(g_ref, gh_ref, u_ref, uh_ref, wg_ref, wu_ref, bg_ref, bu_ref, df_ref, dc_ref, dw_ref, db_ref):
        i = pl.program_id(1)
        live = (i > 0).astype(F32)
        yg, gt = _conv_taps(g_ref[...], gh_ref[...] * live, wg_ref, width)
        yu, ut = _conv_taps(u_ref[...], uh_ref[...] * live, wu_ref, width)
        yg = yg + bg_ref[...]
        yu = yu + bu_ref[...]
        sg = _sigmoid(yg)
        dfv = df_ref[...].astype(F32)
        dcs = (dfv * yu * (sg * (1.0 + yg * (1.0 - sg))), dfv * (yg * sg))

        @pl.when(i == 0)
        def _():
            dw_ref[...] = jnp.zeros_like(dw_ref)
            db_ref[...] = jnp.zeros_like(db_ref)

        for t, (dc, taps) in enumerate(zip(dcs, (gt, ut))):
            dc_ref[t] = dc
            db_ref[t] += jnp.sum(dc, axis=0, keepdims=True)
            for j in range(width):
                dw_ref[t, j:j + 1, :] += jnp.sum(dc * taps[j], axis=0, keepdims=True)

    dc, dw, db = _pcall(
        body, name="ffn_conv_bwd", grid=(nc, S // tr),
        in_specs=[pl.BlockSpec((tr, tw), lambda c, i: (i, c)), _prev_halo_spec(tr, tw, lambda c: c),
                  pl.BlockSpec((tr, tw), lambda c, i: (i, c + nc)), _prev_halo_spec(tr, tw, lambda c: c + nc),
                  pl.BlockSpec((width, tw), lambda c, i: (0, c)), pl.BlockSpec((width, tw), lambda c, i: (0, c + nc)),
                  pl.BlockSpec((1, tw), lambda c, i: (0, c)), pl.BlockSpec((1, tw), lambda c, i: (0, c + nc)),
                  pl.BlockSpec((tr, tw), lambda c, i: (i, c))],
        out_specs=[pl.BlockSpec((2, tr, tw), lambda c, i: (0, i, c)), pl.BlockSpec((2, width, tw), lambda c, i: (0, 0, c)),
                   pl.BlockSpec((2, 1, tw), lambda c, i: (0, 0, c))],
        out_shape=[jax.ShapeDtypeStruct((2, S, half), F32), jax.ShapeDtypeStruct((2, width, half), F32),
                   jax.ShapeDtypeStruct((2, 1, half), F32)],
        dims=("parallel", "arbitrary"))(u, u, u, u, w, w, b, b, df)
    return dc, jnp.concatenate([dw[0], dw[1]], axis=1), jnp.concatenate([db[0], db[1]], axis=1)


_MODE_OF = {v: k for k, v in _DN.items()}


def _bf16_dot(a, b, mode):
    return lax.dot_general(a.astype(BF16), b.astype(BF16), _DN[mode], preferred_element_type=F32)


@functools.partial(jax.custom_vjp, nondiff_argnums=(2,))
def _bdot_mode(a, b, mode):
    return _bf16_dot(a, b, mode)


def _bdot_fwd(a, b, mode):
    return _bf16_dot(a, b, mode), (a, b)


def _bdot_bwd(mode, res, ct):
    a, b = res
    if mode == "nn":
        da, db = _bf16_dot(ct, b, "nt"), _bf16_dot(a, ct, "tn")
    elif mode == "nt":
        da, db = _bf16_dot(ct, b, "nn"), _bf16_dot(ct, a, "tn")
    else:
        da, db = _bf16_dot(b, ct, "nt"), _bf16_dot(a, ct, "nn")
    return da.astype(a.dtype), db.astype(b.dtype)


_bdot_mode.defvjp(_bdot_fwd, _bdot_bwd)


def _bdot(a, b, dn=_DN["nn"]):
    return _bdot_mode(a, b, _MODE_OF[dn])


def _hi_lo(x):
    hi = x.astype(BF16)
    return hi, (x - hi.astype(F32)).astype(BF16)


def _dot3_raw(a, b, mode):
    a1, a2 = _hi_lo(a)
    b1, b2 = _hi_lo(b)
    dot = lambda p, q: lax.dot_general(p, q, _DN[mode], preferred_element_type=F32)
    return dot(a1, b1) + (dot(a1, b2) + dot(a2, b1))


@functools.partial(jax.custom_vjp, nondiff_argnums=(2,))
def _dot3(a, b, mode="nn"):
    return _dot3_raw(a, b, mode)


def _dot3_fwd(a, b, mode):
    return _dot3_raw(a, b, mode), (a, b)


def _dot3_bwd(mode, res, ct):
    a, b = res
    if mode == "nn":
        return _dot3_raw(ct, b, "nt"), _dot3_raw(a, ct, "tn")
    if mode == "nt":
        return _dot3_raw(ct, b, "nn"), _dot3_raw(ct, a, "tn")
    return _dot3_raw(b, ct, "nt"), _dot3_raw(a, ct, "nn")


_dot3.defvjp(_dot3_fwd, _dot3_bwd)


@functools.partial(jax.custom_vjp, nondiff_argnums=(2,))
def _gdot(a, b, mode="nn"):
    return _bf16_dot(a, b, mode)


def _gdot_fwd(a, b, mode):
    return _bf16_dot(a, b, mode), (a.astype(BF16), b.astype(BF16))


def _ct_dot(p, q, mode, ct_first):
    ct, r = (p, q) if ct_first else (q, p)
    c1, c2 = _hi_lo(ct)
    dot = lambda c: lax.dot_general(*((c, r) if ct_first else (r, c)), _DN[mode], preferred_element_type=F32)
    return dot(c1) + dot(c2)


def _gdot_bwd(mode, res, ct):
    a, b = res
    if mode == "nn":
        return _ct_dot(ct, b, "nt", True), _ct_dot(a, ct, "tn", False)
    if mode == "nt":
        return _ct_dot(ct, b, "nn", True), _ct_dot(ct, a, "tn", True)
    return _ct_dot(b, ct, "nt", False), _ct_dot(a, ct, "nn", False)


_gdot.defvjp(_gdot_fwd, _gdot_bwd)


def _split_dot(ones, x):
    x1 = x.astype(BF16)
    r1 = x - x1.astype(F32)
    x2 = r1.astype(BF16)
    x3 = (r1 - x2.astype(F32)).astype(BF16)
    m = ones.astype(BF16)
    dot = lambda p: lax.dot_general(m, p, _DN["nn"], preferred_element_type=F32)
    return dot(x1) + dot(x2) + dot(x3)


@jax.custom_vjp
def _tri_cumsum(x, lower, upper):
    return _split_dot(lower, x)


def _tri_cumsum_fwd(x, lower, upper):
    return _split_dot(lower, x), (lower, upper)


def _tri_cumsum_bwd(res, ct):
    lower, upper = res
    return _split_dot(upper, ct), jnp.zeros_like(lower), jnp.zeros_like(upper)


_tri_cumsum.defvjp(_tri_cumsum_fwd, _tri_cumsum_bwd)


def _tri_masks(n):
    r = lax.broadcasted_iota(jnp.int32, (n, n), 0)
    c = lax.broadcasted_iota(jnp.int32, (n, n), 1)
    return r >= c, r > c


def _gdn_chunk(q, k, v, z, ab, a_row, dt_row, norm_w, state, sel_a, sel_b):
    C = q.shape[0]
    incl, strict = _tri_masks(C)
    lower = jnp.where(incl, 1.0, 0.0)
    qn = q * lax.rsqrt(jnp.sum(q * q, axis=-1, keepdims=True) + NORM_EPS) * (GDN_DK ** -0.5)
    kn = k * lax.rsqrt(jnp.sum(k * k, axis=-1, keepdims=True) + NORM_EPS)
    g = jnp.sum(-jnp.exp(a_row) * _softplus(ab + dt_row) * sel_a, axis=-1, keepdims=True)
    beta = jnp.sum(_sigmoid(ab) * sel_b, axis=-1, keepdims=True)
    gb = jnp.broadcast_to(g, (C, C))
    g_col = _tri_cumsum(gb, lower, jnp.where(strict, 0.0, 1.0))
    g_row = g_col.T
    g_last = jnp.sum(gb, axis=0, keepdims=True)
    gamma = jnp.where(incl, jnp.exp(jnp.where(incl, g_col - g_row, 0.0)), 0.0)
    e_col = jnp.exp(g_col)
    kb = kn * beta
    a_mat = jnp.where(strict, _gdot(kb, kn, "nt") * gamma, 0.0)
    x = jnp.concatenate([v * beta, kb * e_col], axis=1)
    pw = -a_mat
    steps = int(math.log2(C))
    for it in range(steps):
        x = x + _dot3(pw, x, "nn")
        if it < steps - 1:
            pw = _dot3(pw, pw, "nn")
    u, w = x[:, :GDN_DV], x[:, GDN_DV:]
    attn = _gdot(qn, kn, "nt") * gamma
    q_dec = qn * e_col
    k_dec = kn * jnp.exp(g_last - g_col)
    v_new = u - _gdot(w, state, "nn")
    o = _gdot(q_dec, state, "nn") + _gdot(attn, v_new, "nn")
    state_new = state * jnp.exp(jnp.broadcast_to(g_last, state.shape)) + _gdot(k_dec, v_new, "tn")
    y = _rms(o, norm_w) * _silu(z)
    return y, state_new


def _head_selectors(h):
    lane = lax.broadcasted_iota(jnp.int32, (1, LANES), 1)
    return jnp.where(lane == h, 1.0, 0.0), jnp.where(lane == h + GDN_HEADS, 1.0, 0.0)


def _gdn_in_specs(rev, nc):
    def n_(n):
        return nc - 1 - n if rev else n
    H = GDN_HEADS
    blk = lambda off: pl.BlockSpec((CHUNK, LANES), lambda n, h: (n_(n), off + h))
    row = pl.BlockSpec((1, LANES), lambda n, h: (0, 0))
    return n_, [blk(0), blk(H), blk(2 * H), blk(ZIN_Z // LANES),
                pl.BlockSpec((CHUNK, LANES), lambda n, h: (n_(n), ZIN_AB // LANES)), row, row, row]


def _hosting(call, hosted, *args):
    res = call(*args)
    return res if hosted is not None else (res, [])


def gdn_fwd(qkv, zin, a_row, dt_row, norm_w, hosted=None):
    S = qkv.shape[0]
    nc = S // CHUNK
    H = GDN_HEADS
    _, in_specs = _gdn_in_specs(False, nc)

    def body(q_ref, k_ref, v_ref, z_ref, ab_ref, a_ref, dt_ref, nw_ref, y_ref, st_ref, state):
        n, h = pl.program_id(0), pl.program_id(1)

        @pl.when(n == 0)
        def _():
            state[h] = jnp.zeros((GDN_DK, GDN_DV), F32)

        st = state[h]
        st_ref[...] = st
        sel_a, sel_b = _head_selectors(h)
        y, st_new = _gdn_chunk(q_ref[...], k_ref[...], v_ref[...], z_ref[...], ab_ref[...], a_ref[...],
                               dt_ref[...], nw_ref[...], st, sel_a, sel_b)
        y_ref[...] = y.astype(y_ref.dtype)
        state[h] = st_new

    call = _pcall(body, name="gdn_fwd", grid=(nc, H), in_specs=in_specs,
                  out_specs=[pl.BlockSpec((CHUNK, LANES), lambda n, h: (n, h)),
                             pl.BlockSpec((None, None, GDN_DK, GDN_DV), lambda n, h: (h, n, 0, 0))],
                  out_shape=[jax.ShapeDtypeStruct((S, H * GDN_DV), BF16),
                             jax.ShapeDtypeStruct((H, nc, GDN_DK, GDN_DV), F32)],
                  scratch_shapes=[pltpu.VMEM((H, GDN_DK, GDN_DV), F32)],
                  dims=("arbitrary", "arbitrary"), hosted=hosted)
    return _hosting(call, hosted, qkv, qkv, qkv, zin, zin, a_row, dt_row, norm_w)


def gdn_bwd(qkv, zin, a_row, dt_row, norm_w, states, dy, dy_col0, hosted=None):
    S = qkv.shape[0]
    nc = S // CHUNK
    H = GDN_HEADS
    n_, in_specs = _gdn_in_specs(True, nc)
    in_specs = in_specs + [pl.BlockSpec((None, None, GDN_DK, GDN_DV), lambda n, h: (h, n_(n), 0, 0)),
                           pl.BlockSpec((CHUNK, LANES), lambda n, h: (n_(n), dy_col0 + h))]

    def body(q_ref, k_ref, v_ref, z_ref, ab_ref, a_ref, dt_ref, nw_ref, st_ref, dy_ref,
             dq_ref, dk_ref, dv_ref, dz_ref, dab_ref, da_ref, ddt_ref, dnw_ref, dstate):
        n, h = pl.program_id(0), pl.program_id(1)

        @pl.when(n == 0)
        def _():
            dstate[h] = jnp.zeros((GDN_DK, GDN_DV), F32)

        @pl.when((n == 0) & (h == 0))
        def _():
            da_ref[...] = jnp.zeros_like(da_ref)
            ddt_ref[...] = jnp.zeros_like(ddt_ref)
            dnw_ref[...] = jnp.zeros_like(dnw_ref)

        sel_a, sel_b = _head_selectors(h)
        _, vjp = jax.vjp(lambda *a: _gdn_chunk(*a, sel_a, sel_b), q_ref[...], k_ref[...], v_ref[...], z_ref[...],
                         ab_ref[...], a_ref[...], dt_ref[...], nw_ref[...], st_ref[...])
        dq, dk, dv, dz, dab, da, ddt, dnw, dst = vjp((dy_ref[...].astype(F32), dstate[h]))
        dq_ref[...] = dq
        dk_ref[...] = dk
        dv_ref[...] = dv
        dz_ref[...] = dz.astype(dz_ref.dtype)

        @pl.when(h == 0)
        def _():
            dab_ref[...] = jnp.zeros_like(dab_ref)

        dab_ref[...] += dab
        da_ref[...] += da
        ddt_ref[...] += ddt
        dnw_ref[...] += dnw
        dstate[h] = dst

    blk = pl.BlockSpec((CHUNK, LANES), lambda n, h: (n_(n), h))
    row = pl.BlockSpec((1, LANES), lambda n, h: (0, 0))
    wide = jax.ShapeDtypeStruct((S, H * LANES), F32)
    call = _pcall(body, name="gdn_bwd", grid=(nc, H), in_specs=in_specs,
                  out_specs=[blk, blk, blk, blk, pl.BlockSpec((CHUNK, LANES), lambda n, h: (n_(n), 0)), row, row, row],
                  out_shape=[wide, wide, wide, jax.ShapeDtypeStruct((S, H * LANES), BF16),
                             jax.ShapeDtypeStruct((S, LANES), F32)] + [jax.ShapeDtypeStruct((1, LANES), F32)] * 3,
                  scratch_shapes=[pltpu.VMEM((H, GDN_DK, GDN_DV), F32)],
                  dims=("arbitrary", "arbitrary"), hosted=hosted)
    return _hosting(call, hosted, qkv, qkv, qkv, zin, zin, a_row, dt_row, norm_w, states, dy)


def _rope_full(x, cos, sin):
    x1, x2 = x[:, :RET_DK // 2], x[:, RET_DK // 2:]
    return jnp.concatenate([x1 * cos - x2 * sin, x2 * cos + x1 * sin], axis=1)


def _ret_chunk(q, k, v, cos, sin, lg, state):
    C = q.shape[0]
    incl, _ = _tri_masks(C)
    qr = _rope_full(q, cos, sin)
    kr = _rope_full(k, cos, sin) * (RET_DK ** -0.5)
    r = lax.broadcasted_iota(jnp.int32, (C, C), 0)
    c = lax.broadcasted_iota(jnp.int32, (C, C), 1)
    dist = jnp.where(incl, (r - c).astype(F32), 0.0)
    decay = jnp.where(incl, jnp.exp(dist * lg), 0.0)
    pos = lax.broadcasted_iota(jnp.int32, (C, 1), 0).astype(F32)
    lg1 = lg[:, :1]
    xi = jnp.exp((pos + 1.0) * lg1)
    zeta = jnp.exp((C - 1.0 - pos) * lg1)
    inner = _bdot(_bdot(qr, kr, _DN["nt"]) * decay, v)
    cross = _bdot(qr * xi, state)
    state_new = state * jnp.exp(C * lg1) + _bdot(kr * zeta, v, _DN["tn"])
    o = inner + cross
    mu = jnp.mean(o, axis=-1, keepdims=True)
    var = jnp.mean(jnp.square(o - mu), axis=-1, keepdims=True)
    return (o - mu) * lax.rsqrt(var + NORM_EPS), state_new


def _ret_log_gamma():
    lg = np.log1p(-np.power(2.0, -5.0 - np.arange(RET_HEADS, dtype=np.float64))).astype(np.float32)
    return jnp.asarray(np.broadcast_to(lg[:, None, None], (RET_HEADS, 1, LANES)).copy())


def _ret_in_specs(rev, nc):
    def n_(n):
        return nc - 1 - n if rev else n
    H = RET_HEADS
    return n_, [pl.BlockSpec((CHUNK, RET_DK), lambda n, h: (n_(n), h)),
                pl.BlockSpec((CHUNK, RET_DK), lambda n, h: (n_(n), H + h)),
                pl.BlockSpec((CHUNK, RET_DV), lambda n, h: (n_(n), 2 * H * RET_DK // RET_DV + h)),
                pl.BlockSpec((CHUNK, LANES), lambda n, h: (n_(n), 0)),
                pl.BlockSpec((CHUNK, LANES), lambda n, h: (n_(n), 0)),
                pl.BlockSpec((None, 1, LANES), lambda n, h: (h, 0, 0))]


def ret_fwd(zz, cos, sin, hosted=None):
    S = zz.shape[0]
    nc = S // CHUNK
    H = RET_HEADS
    _, in_specs = _ret_in_specs(False, nc)

    def body(q_ref, k_ref, v_ref, cos_ref, sin_ref, lg_ref, o_ref, st_ref, state):
        n, h = pl.program_id(0), pl.program_id(1)

        @pl.when(n == 0)
        def _():
            state[h] = jnp.zeros((RET_DK, RET_DV), F32)

        st = state[h]
        st_ref[...] = st
        o, st_new = _ret_chunk(q_ref[...], k_ref[...], v_ref[...], cos_ref[...], sin_ref[...], lg_ref[...], st)
        o_ref[...] = o
        state[h] = st_new

    call = _pcall(body, name="ret_fwd", grid=(nc, H), in_specs=in_specs,
                  out_specs=[pl.BlockSpec((CHUNK, RET_DV), lambda n, h: (n, h)),
                             pl.BlockSpec((None, None, RET_DK, RET_DV), lambda n, h: (h, n, 0, 0))],
                  out_shape=[jax.ShapeDtypeStruct((S, H * RET_DV), F32),
                             jax.ShapeDtypeStruct((H, nc, RET_DK, RET_DV), F32)],
                  scratch_shapes=[pltpu.VMEM((H, RET_DK, RET_DV), F32)],
                  dims=("arbitrary", "arbitrary"), hosted=hosted)
    return _hosting(call, hosted, zz, zz, zz, cos, sin, _ret_log_gamma())


def ret_bwd(zz, cos, sin, states, do):
    S = zz.shape[0]
    nc = S // CHUNK
    H = RET_HEADS
    n_, in_specs = _ret_in_specs(True, nc)
    in_specs = in_specs + [pl.BlockSpec((None, None, RET_DK, RET_DV), lambda n, h: (h, n_(n), 0, 0)),
                           pl.BlockSpec((CHUNK, RET_DV), lambda n, h: (n_(n), h))]

    def body(q_ref, k_ref, v_ref, cos_ref, sin_ref, lg_ref, st_ref, do_ref, dq_ref, dk_ref, dv_ref, dstate):
        n, h = pl.program_id(0), pl.program_id(1)

        @pl.when(n == 0)
        def _():
            dstate[h] = jnp.zeros((RET_DK, RET_DV), F32)

        cos, sin, lg = cos_ref[...], sin_ref[...], lg_ref[...]
        _, vjp = jax.vjp(lambda q, k, v, st: _ret_chunk(q, k, v, cos, sin, lg, st),
                         q_ref[...], k_ref[...], v_ref[...], st_ref[...])
        dq, dk, dv, dst = vjp((do_ref[...], dstate[h]))
        dq_ref[...] = dq.astype(dq_ref.dtype)
        dk_ref[...] = dk.astype(dk_ref.dtype)
        dv_ref[...] = dv.astype(dv_ref.dtype)
        dstate[h] = dst

    return _pcall(body, name="ret_bwd", grid=(nc, H), in_specs=in_specs,
                  out_specs=[pl.BlockSpec((CHUNK, RET_DK), lambda n, h: (n_(n), h)),
                             pl.BlockSpec((CHUNK, RET_DK), lambda n, h: (n_(n), h)),
                             pl.BlockSpec((CHUNK, RET_DV), lambda n, h: (n_(n), h))],
                  out_shape=[jax.ShapeDtypeStruct((S, H * RET_DK), BF16), jax.ShapeDtypeStruct((S, H * RET_DK), BF16),
                             jax.ShapeDtypeStruct((S, H * RET_DV), BF16)],
                  scratch_shapes=[pltpu.VMEM((H, RET_DK, RET_DV), F32)],
                  dims=("arbitrary", "arbitrary"))(zz, zz, zz, cos, sin, _ret_log_gamma(), states, do)


MLA_SCALE = (MLA_NOPE + MLA_ROPE) ** -0.5
NEG = -1e30


def _mla_scores(q, kn, kpe, diagonal):
    s = (lax.dot_general(q[:, :LANES], kn, _DN["nt"], preferred_element_type=F32)
         + lax.dot_general(q[:, LANES:], kpe, _DN["nt"], preferred_element_type=F32)) * MLA_SCALE
    if diagonal:
        row = lax.broadcasted_iota(jnp.int32, s.shape, 0)
        col = lax.broadcasted_iota(jnp.int32, s.shape, 1)
        s = jnp.where(col <= row, s, NEG)
    return s


def _on_and_below_diagonal(i, j, step):
    @pl.when(j < i)
    def _():
        step(False)

    @pl.when(j == i)
    def _():
        step(True)


def flash_fwd(qr, kv, kpe, *, t=512, hosted=None):
    S = qr.shape[0]
    t = min(t, S)
    nb = S // t
    H = MLA_HEADS

    def body(q_ref, kn_ref, v_ref, kpe_ref, o_ref, lse_ref, m_s, l_s, acc):
        i, j = pl.program_id(1), pl.program_id(2)

        @pl.when(j == 0)
        def _():
            m_s[...] = jnp.full_like(m_s, NEG)
            l_s[...] = jnp.zeros_like(l_s)
            acc[...] = jnp.zeros_like(acc)

        def step(diagonal):
            s = _mla_scores(q_ref[...], kn_ref[...], kpe_ref[...], diagonal)
            m_new = jnp.maximum(m_s[...], jnp.max(s, axis=-1, keepdims=True))
            p = jnp.exp(s - m_new)
            alpha = jnp.exp(m_s[...] - m_new)
            l_s[...] = alpha * l_s[...] + jnp.sum(p, axis=-1, keepdims=True)
            acc[...] = alpha * acc[...] + _bdot(p, v_ref[...])
            m_s[...] = m_new

        _on_and_below_diagonal(i, j, step)

        @pl.when(j == nb - 1)
        def _():
            o_ref[...] = (acc[...] / l_s[...]).astype(o_ref.dtype)
            lse_ref[...] = m_s[...] + jnp.log(l_s[...])

    kmap = lambda off: (lambda h, i, j: (jnp.minimum(j, i), off + h))
    call = _pcall(body, name="mla_flash_fwd", grid=(H, nb, nb),
                  in_specs=[pl.BlockSpec((t, 2 * LANES), lambda h, i, j: (i, h)),
                            pl.BlockSpec((t, LANES), kmap(0)), pl.BlockSpec((t, LANES), kmap(H)),
                            pl.BlockSpec((t, LANES), lambda h, i, j: (jnp.minimum(j, i), 0))],
                  out_specs=[pl.BlockSpec((t, LANES), lambda h, i, j: (i, h)),
                             pl.BlockSpec((None, t, 1), lambda h, i, j: (h, i, 0))],
                  out_shape=[jax.ShapeDtypeStruct((S, H * MLA_V), BF16), jax.ShapeDtypeStruct((H, S, 1), F32)],
                  scratch_shapes=[pltpu.VMEM((t, 1), F32), pltpu.VMEM((t, 1), F32), pltpu.VMEM((t, MLA_V), F32)],
                  dims=("parallel", "parallel", "arbitrary"), hosted=hosted)
    return _hosting(call, hosted, qr, kv, kv, kpe)


def _mla_p_ds(q, kn, v, kpe, do, o, lse, diagonal):
    p = jnp.exp(_mla_scores(q, kn, kpe, diagonal) - lse)
    dof = do.astype(F32)
    delta = jnp.sum(dof * o.astype(F32), axis=-1, keepdims=True)
    dp = lax.dot_general(do.astype(BF16), v, _DN["nt"], preferred_element_type=F32)
    ds = p * (dp - delta) * MLA_SCALE
    return p, ds


def flash_bwd_dq(qr, kv, kpe, o, lse, dy, dy_col0, *, t=512):
    S = qr.shape[0]
    t = min(t, S)
    nb = S // t
    H = MLA_HEADS

    def body(q_ref, kn_ref, v_ref, kpe_ref, o_ref, lse_ref, do_ref, dq_ref, acc):
        i, j = pl.program_id(1), pl.program_id(2)

        @pl.when(j == 0)
        def _():
            acc[...] = jnp.zeros_like(acc)

        def step(diagonal):
            _, ds = _mla_p_ds(q_ref[...], kn_ref[...], v_ref[...], kpe_ref[...], do_ref[...], o_ref[...],
                              lse_ref[...], diagonal)
            acc[...] += jnp.concatenate([_bdot(ds, kn_ref[...]), _bdot(ds, kpe_ref[...])], axis=1)

        _on_and_below_diagonal(i, j, step)

        @pl.when(j == nb - 1)
        def _():
            dq_ref[...] = acc[...]

    kmap = lambda off: (lambda h, i, j: (jnp.minimum(j, i), off + h))
    return _pcall(body, name="mla_flash_dq", grid=(H, nb, nb),
                  in_specs=[pl.BlockSpec((t, 2 * LANES), lambda h, i, j: (i, h)),
                            pl.BlockSpec((t, LANES), kmap(0)), pl.BlockSpec((t, LANES), kmap(H)),
                            pl.BlockSpec((t, LANES), lambda h, i, j: (jnp.minimum(j, i), 0)),
                            pl.BlockSpec((t, LANES), lambda h, i, j: (i, h)),
                            pl.BlockSpec((None, t, 1), lambda h, i, j: (h, i, 0)),
                            pl.BlockSpec((t, LANES), lambda h, i, j: (i, dy_col0 + h))],
                  out_specs=pl.BlockSpec((t, 2 * LANES), lambda h, i, j: (i, h)),
                  out_shape=jax.ShapeDtypeStruct((S, H * 2 * LANES), F32),
                  scratch_shapes=[pltpu.VMEM((t, 2 * LANES), F32)],
                  dims=("parallel", "parallel", "arbitrary"))(qr, kv, kv, kpe, o, lse, dy)


def flash_bwd_dkv(qr, kv, kpe, o, lse, dy, dy_col0, *, t=512, hosted=None):
    S = qr.shape[0]
    t = min(t, S)
    nb = S // t
    H = MLA_HEADS

    def body(q_ref, kn_ref, v_ref, kpe_ref, o_ref, lse_ref, do_ref, dkn_ref, dv_ref, dkpe_ref, akn, av):
        j, h, i = pl.program_id(0), pl.program_id(1), pl.program_id(2)

        @pl.when(i == 0)
        def _():
            akn[...] = jnp.zeros_like(akn)
            av[...] = jnp.zeros_like(av)

        @pl.when((i == 0) & (h == 0))
        def _():
            dkpe_ref[...] = jnp.zeros_like(dkpe_ref)

        def step(diagonal):
            q = q_ref[...]
            p, ds = _mla_p_ds(q, kn_ref[...], v_ref[...], kpe_ref[...], do_ref[...], o_ref[...], lse_ref[...],
                              diagonal)
            av[...] += _bdot(p, do_ref[...], _DN["tn"])
            akn[...] += _bdot(ds, q[:, :LANES], _DN["tn"])
            dkpe_ref[...] += _bdot(ds, q[:, LANES:], _DN["tn"])

        _on_and_below_diagonal(i, j, step)

        @pl.when(i == nb - 1)
        def _():
            dkn_ref[...] = akn[...].astype(dkn_ref.dtype)
            dv_ref[...] = av[...].astype(dv_ref.dtype)

    qmap = lambda off: (lambda j, h, i: (jnp.maximum(i, j), off + h))
    call = _pcall(
        body, name="mla_flash_dkv", grid=(nb, H, nb),
        in_specs=[pl.BlockSpec((t, 2 * LANES), qmap(0)),
                  pl.BlockSpec((t, LANES), lambda j, h, i: (j, h)), pl.BlockSpec((t, LANES), lambda j, h, i: (j, H + h)),
                  pl.BlockSpec((t, LANES), lambda j, h, i: (j, 0)),
                  pl.BlockSpec((t, LANES), qmap(0)),
                  pl.BlockSpec((None, t, 1), lambda j, h, i: (h, jnp.maximum(i, j), 0)),
                  pl.BlockSpec((t, LANES), qmap(dy_col0))],
        out_specs=[pl.BlockSpec((t, LANES), lambda j, h, i: (j, h)), pl.BlockSpec((t, LANES), lambda j, h, i: (j, h)),
                   pl.BlockSpec((t, LANES), lambda j, h, i: (j, 0))],
        out_shape=[jax.ShapeDtypeStruct((S, H * LANES), BF16), jax.ShapeDtypeStruct((S, H * LANES), BF16),
                   jax.ShapeDtypeStruct((S, LANES), F32)],
        scratch_shapes=[pltpu.VMEM((t, LANES), F32), pltpu.VMEM((t, LANES), F32)],
        dims=("arbitrary", "arbitrary", "arbitrary"), hosted=hosted)
    (dkn, dv, dkpe), extra = _hosting(call, hosted, qr, kv, kv, kpe, o, lse, dy)
    return (jnp.concatenate([dkn, dv], axis=1), dkpe), extra


def loss_head(h, target, g, *, tr=256):
    S, D = h.shape
    tr = min(tr, S)

    def body(h_ref, t_ref, g_ref, loss_ref, dh_ref, dg_ref):
        tgt = t_ref[...]

        def f(hh, gg):
            err = jnp.square(_rms(hh, gg) - tgt)
            per_row = jnp.sum(err, axis=-1, keepdims=True) * (0.5 / D)
            return jnp.sum(per_row, axis=0, keepdims=True)

        val, vjp = jax.vjp(f, h_ref[...], g_ref[...])
        dh, dg = vjp(jnp.ones((1, 1), F32))
        dh_ref[...] = dh

        @pl.when(pl.program_id(0) == 0)
        def _():
            loss_ref[...] = jnp.zeros_like(loss_ref)
            dg_ref[...] = jnp.zeros_like(dg_ref)

        loss_ref[...] += jnp.broadcast_to(val, loss_ref.shape)
        dg_ref[...] += dg

    return _pcall(body, name="loss_head", grid=(S // tr,),
                  in_specs=[_row_spec(tr, D, 0), _row_spec(tr, D, 0), _full_spec(g)],
                  out_specs=[pl.BlockSpec((1, LANES), lambda i: (0, 0)), _row_spec(tr, D, 0), _full_spec(g)],
                  out_shape=[jax.ShapeDtypeStruct((1, LANES), F32), jax.ShapeDtypeStruct((S, D), F32),
                             jax.ShapeDtypeStruct(g.shape, F32)],
                  dims=("arbitrary",))(h, target, g)


def _rope_tables(positions, dim):
    inv_freq = ROPE_THETA ** (-jnp.arange(0, dim, 2, dtype=F32) / dim)
    ang = positions.astype(F32)[:, None] * inv_freq
    return jnp.cos(ang), jnp.sin(ang)


def _pad_cols(w, n):
    return jnp.pad(w, ((0, 0), (0, n - w.shape[1])))


def _prep_w_in0(w):
    return jnp.concatenate([w[:, :4096], w[:, 4112:5136], _pad_cols(w[:, 5136:5200], LANES),
                            _pad_cols(w[:, 4096:4112], LANES)], axis=1)


def _unprep_w_in0(g):
    return jnp.concatenate([g[:, :4096], g[:, ZIN_AB:ZIN_AB + 16], g[:, ZIN_CQ:ZIN_KR], g[:, ZIN_KR:ZIN_KR + MLA_ROPE]],
                           axis=1)


def _prep_w_uq(w):
    w = w.reshape(MLA_Q_RANK, MLA_HEADS, MLA_NOPE + MLA_ROPE)
    w = jnp.pad(w, ((0, 0), (0, 0), (0, 2 * LANES - MLA_NOPE - MLA_ROPE)))
    return w.reshape(MLA_Q_RANK, MLA_HEADS * 2 * LANES)


def _unprep_w_uq(g):
    g = g.reshape(MLA_Q_RANK, MLA_HEADS, 2 * LANES)[:, :, :MLA_NOPE + MLA_ROPE]
    return g.reshape(MLA_Q_RANK, MLA_HEADS * (MLA_NOPE + MLA_ROPE))


def _prep_w_ukv(w):
    w = w.reshape(MLA_KV_RANK, MLA_HEADS, 2, LANES)
    return jnp.transpose(w, (0, 2, 1, 3)).reshape(MLA_KV_RANK, 2 * MLA_HEADS * LANES)


def _unprep_w_ukv(g):
    g = g.reshape(MLA_KV_RANK, 2, MLA_HEADS, LANES)
    return jnp.transpose(g, (0, 2, 1, 3)).reshape(MLA_KV_RANK, 2 * MLA_HEADS * LANES)


def _row(v, n=None):
    v = v.reshape(1, -1).astype(F32)
    return v if n is None else _pad_cols(v, n)


def _ffn_fwd(h, norm_g, w_up, conv_w, conv_b, w_down, tag):
    (hn,) = rowwise(_fn_rms, [(h, D_MODEL, 0)], [norm_g], [], [(D_MODEL, BF16)], name=f"{tag}_ffn_norm")
    u = matmul(hn, w_up, "nn", F32, b_shards=N_CHIPS, **TILES["wide_nn"], name=f"{tag}_ffn_up")
    f = ffn_conv_fwd(u, conv_w, conv_b)
    h_out = matmul(f, w_down, "nn", F32, add=h, tm=512, tn=1024, tk=8192, name=f"{tag}_ffn_down")
    return h_out, (hn, u, f)


def _ffn_bwd(dh, dh16, h, norm_g, w_up, conv_w, conv_b, w_down, saved, tag):
    hn, u, f = saved
    df = matmul(dh16, w_down, "nt", BF16, tm=512, tn=2816, name=f"{tag}_ffn_down_dx")
    g_down = matmul(f, dh16, "tn", BF16, **TILES["dw"], name=f"{tag}_ffn_down_dw")
    dc, g_conv_w, g_conv_b = ffn_conv_bwd(u, conv_w, conv_b, df)
    du = conv_transpose(dc, conv_w, BF16, name=f"{tag}_ffn_conv_dx")
    g_up = matmul(hn, du, "tn", BF16, out_shards=N_CHIPS, tm=1024, tn=1408, tk=4096, name=f"{tag}_ffn_up_dw")
    dhn = matmul(du, w_up, "nt", F32, b_shards=N_CHIPS, tm=512, tn=2048, tk=2816, name=f"{tag}_ffn_up_dx")
    ((dh_in, dh_in16),), (g_norm,) = rowwise_bwd(_fn_rms, [(h, D_MODEL, 0)], [norm_g], [], [(dhn, D_MODEL, 0)],
                                                 [(F32, BF16)], adds=[(dh, D_MODEL, 0)], name=f"{tag}_ffn_norm_bwd")
    return dh_in, dh_in16, dict(ffn_norm=g_norm, ffn_w_up=g_up, ffn_conv_w=g_conv_w, ffn_conv_b=g_conv_b,
                                ffn_w_down=g_down)


TILES = {"wide_nn": dict(tm=512, tn=3072, tk=2048),
         "square": dict(tm=512, tn=2048, tk=2048),
         "dw": dict(tm=512, tn=2048, tk=4096)}


def _ple_fwd(h, p_i, w_proj, gate_g, w_gate, tag):
    (hg,) = rowwise(_fn_rms, [(h, D_MODEL, 0)], [gate_g], [], [(D_MODEL, BF16)], name=f"{tag}_ple_norm")
    gl = matmul(hg, w_gate, "nn", F32, **TILES["square"], name=f"{tag}_ple_gate")
    pp = matmul(p_i, w_proj, "nn", F32, b_shards=N_CHIPS, name=f"{tag}_ple_proj")
    (h_out,) = rowwise(_fn_ple, [(h, D_MODEL, 0), (pp, D_MODEL, 0), (gl, D_MODEL, 0)], [], [], [(D_MODEL, F32)],
                       name=f"{tag}_ple_add")
    return h_out, (hg, gl, pp)


def _ple_bwd(dh, h, p_i, w_proj, gate_g, w_gate, saved, tag):
    hg, gl, pp = saved
    (dpp, dgl), _ = rowwise_bwd(_fn_ple_terms, [(pp, D_MODEL, 0), (gl, D_MODEL, 0)], [], [], [(dh, D_MODEL, 0)],
                                [BF16, BF16], name=f"{tag}_ple_add_bwd")
    g_proj = matmul(p_i, dpp, "tn", BF16, out_shards=N_CHIPS, name=f"{tag}_ple_proj_dw")
    g_gate = matmul(hg, dgl, "tn", BF16, **TILES["dw"], name=f"{tag}_ple_gate_dw")
    dhg = matmul(dgl, w_gate, "nt", F32, **TILES["square"], name=f"{tag}_ple_gate_dx")
    ((dh_in, dh_in16),), (g_norm,) = rowwise_bwd(_fn_rms, [(h, D_MODEL, 0)], [gate_g], [], [(dhg, D_MODEL, 0)],
                                                 [(F32, BF16)], adds=[(dh, D_MODEL, 0)], name=f"{tag}_ple_norm_bwd")
    return dh_in, dh_in16, dict(ple_proj=g_proj, ple_gate_norm=g_norm, ple_gate=g_gate)


def local_step(x, p, positions, target, slots, W, ex=None):
    S = x.shape[0]
    G = {}
    p0, p1 = p[0].astype(BF16), p[1].astype(BF16)
    W = dict(W)

    def use(names, bufs):
        for k, b in zip(names, bufs):
            r, c = b.shape[1:]
            W[k] = b.reshape(N_CHIPS * r, c) if k in ROW_SHARDED else (b if k in KEPT_SHARDED else _cols_to_full(b))

    def by_chip(names):
        out = []
        for k in names:
            r, c = slots[k].shape[1:]
            out.append(G[k].reshape(N_CHIPS, r, c) if k in ROW_SHARDED
                       else (G[k] if k in KEPT_SHARDED else _full_to_cols(G[k])))
        return out

    first = [slots[k] for k in GATHER_FIRST]
    use(GATHER_FIRST, ex.gather(first) if ex else first)

    cm, sm = _rope_tables(positions, MLA_ROPE)
    zeros = jnp.zeros((S, LANES - MLA_ROPE), F32)
    cosp = jnp.concatenate([cm, cm, zeros], axis=1)
    sinp = jnp.concatenate([sm, sm, zeros], axis=1)
    cr, sr = _rope_tables(positions, RET_DK)

    w_in0 = _prep_w_in0(W["l0_w_in"])
    w_uq = _prep_w_uq(W["l0_mla_w_uq"])
    w_ukv = _prep_w_ukv(W["l0_mla_w_ukv"])
    a_row = _row(W["l0_gdn_A_log"], LANES)
    dt_row = _row(W["l0_gdn_dt_bias"], LANES)
    gdn_nw = _row(W["l0_gdn_norm"])
    n = {k: _row(W[k]) for k in ("l0_attn_norm", "l0_mla_q_norm", "l0_mla_kv_norm", "l0_ffn_norm",
                                 "l0_ple_gate_norm", "l1_attn_norm", "l1_ret_norm", "l1_ffn_norm",
                                 "l1_ple_gate_norm", "final_norm", "l0_ffn_conv_b", "l1_ffn_conv_b")}

    (hn0,) = rowwise(_fn_rms, [(x, D_MODEL, 0)], [n["l0_attn_norm"]], [], [(D_MODEL, BF16)], name="l0_attn_norm")
    zin = matmul(hn0, w_in0, "nn", F32, tm=512, tn=1792, name="l0_w_in")
    qkv = gdn_conv_fwd(zin, W["l0_gdn_conv"])
    layer0 = [slots[k] for k in GATHER_L0]
    (y_a, gdn_states), got = gdn_fwd(qkv, zin, a_row, dt_row, gdn_nw, hosted=hosted_gather(layer0) if ex else None)
    use(GATHER_L0, got if ex else layer0)
    mla_rows = [(zin, MLA_Q_RANK, ZIN_CQ // MLA_Q_RANK), (zin, MLA_KV_RANK, ZIN_CKV // MLA_KV_RANK),
                (zin, LANES, ZIN_KR // LANES)]
    mla_nd = [(cosp, LANES, 0), (sinp, LANES, 0)]
    cqn, ckvn, kpe = rowwise(_fn_mla_pre, mla_rows, [n["l0_mla_q_norm"], n["l0_mla_kv_norm"]], mla_nd,
                             [(MLA_Q_RANK, BF16), (MLA_KV_RANK, BF16), (LANES, BF16)], name="mla_pre")
    q_lin = matmul(cqn, w_uq, "nn", F32, name="mla_w_uq")
    kv = matmul(ckvn, w_ukv, "nn", BF16, name="mla_w_ukv")
    (qr,) = rowwise(_fn_rope_q, [(q_lin, 2048, 0)], [], mla_nd, [(2048, BF16)],
                    name="mla_rope_q")
    layer1 = [slots[k] for k in GATHER_L1]
    (y_b, lse), got = flash_fwd(qr, kv, kpe, hosted=hosted_gather(layer1) if ex else None)
    use(GATHER_L1, got if ex else layer1)
    y_ab = jnp.concatenate([y_a, y_b], axis=1)
    h1 = matmul(y_ab, W["l0_w_out"], "nn", F32, add=x, **TILES["square"], name="l0_w_out")
    h2, ffn0 = _ffn_fwd(h1, n["l0_ffn_norm"], W["l0_ffn_w_up"], W["l0_ffn_conv_w"], n["l0_ffn_conv_b"],
                        W["l0_ffn_w_down"], "l0")
    h3, ple0 = _ple_fwd(h2, p0, W["l0_ple_proj"], n["l0_ple_gate_norm"], W["l0_ple_gate"], "l0")

    (hn1,) = rowwise(_fn_rms, [(h3, D_MODEL, 0)], [n["l1_attn_norm"]], [], [(D_MODEL, BF16)], name="l1_attn_norm")
    zz = matmul(hn1, W["l1_w_in"], "nn", F32, b_shards=N_CHIPS, **TILES["wide_nn"], name="l1_w_in")
    late = [slots[k] for k in GATHER_RET]
    (o_ret, ret_states), got = ret_fwd(zz, cr, sr, hosted=hosted_gather(late) if ex else None)
    use(GATHER_RET, got if ex else late)
    gate_rows = [(zz, 4096, 2), (o_ret, 4096, 0)]
    (yg,) = rowwise(_fn_ret_gate, gate_rows, [n["l1_ret_norm"]], [], [(4096, BF16)], name="ret_gate")
    h4 = matmul(yg, W["l1_w_out"], "nn", F32, add=h3, tm=512, tn=2048, tk=4096, name="l1_w_out")
    h5, ffn1 = _ffn_fwd(h4, n["l1_ffn_norm"], W["l1_ffn_w_up"], W["l1_ffn_conv_w"], n["l1_ffn_conv_b"],
                        W["l1_ffn_w_down"], "l1")
    h6, ple1 = _ple_fwd(h5, p1, W["l1_ple_proj"], n["l1_ple_gate_norm"], W["l1_ple_gate"], "l1")

    loss_vec, dh, G["final_norm"] = loss_head(h6, target, n["final_norm"])

    dh, dh16, g = _ple_bwd(dh, h5, p1, W["l1_ple_proj"], n["l1_ple_gate_norm"], W["l1_ple_gate"], ple1, "l1")
    G.update({"l1_" + k: v for k, v in g.items()})
    dh, dh16, g = _ffn_bwd(dh, dh16, h4, n["l1_ffn_norm"], W["l1_ffn_w_up"], W["l1_ffn_conv_w"], n["l1_ffn_conv_b"],
                           W["l1_ffn_w_down"], ffn1, "l1")
    G.update({"l1_" + k: v for k, v in g.items()})

    dyg = matmul(dh16, W["l1_w_out"], "nt", F32, tm=512, tn=4096, name="l1_w_out_dx")
    G["l1_w_out"] = matmul(yg, dh16, "tn", BF16, **TILES["dw"], name="l1_w_out_dw")
    (dg, do_ret), (G["l1_ret_norm"],) = rowwise_bwd(_fn_ret_gate, gate_rows, [n["l1_ret_norm"]], [],
                                                   [(dyg, 4096, 0)], [BF16, F32], name="ret_gate_bwd")
    dq, dk, dv = ret_bwd(zz, cr, sr, ret_states, do_ret)
    dzz = jnp.concatenate([dq, dk, dv, dg], axis=1)
    G["l1_w_in"] = matmul(hn1, dzz, "tn", BF16, out_shards=N_CHIPS, tm=1024, tn=1536, tk=4096, name="l1_w_in_dw")
    dhn = matmul(dzz, W["l1_w_in"], "nt", F32, b_shards=N_CHIPS, tm=1024, tn=1024, tk=3072, name="l1_w_in_dx")
    (dh,), (G["l1_attn_norm"],) = rowwise_bwd(_fn_rms, [(h3, D_MODEL, 0)], [n["l1_attn_norm"]], [],
                                             [(dhn, D_MODEL, 0)], [F32], adds=[(dh, D_MODEL, 0)],
                                             name="l1_attn_norm_bwd")

    dh, dh16, g = _ple_bwd(dh, h2, p0, W["l0_ple_proj"], n["l0_ple_gate_norm"], W["l0_ple_gate"], ple0, "l0")
    G.update({"l0_" + k: v for k, v in g.items()})
    dh, dh16, g = _ffn_bwd(dh, dh16, h1, n["l0_ffn_norm"], W["l0_ffn_w_up"], W["l0_ffn_conv_w"], n["l0_ffn_conv_b"],
                           W["l0_ffn_w_down"], ffn0, "l0")
    G.update({"l0_" + k: v for k, v in g.items()})

    dy_ab = matmul(dh16, W["l0_w_out"], "nt", F32, **TILES["square"], name="l0_w_out_dx")
    G["l0_w_out"] = matmul(y_ab, dh16, "tn", BF16, **TILES["dw"], name="l0_w_out_dw")
    sums, landed = {}, {}
    if ex:
        sums.update(zip(REDUCE_L1, ex.pair_sums(REDUCE_L1, by_chip(REDUCE_L1))))
        sums.update(zip(REDUCE_L0, ex.pair_sums(REDUCE_L0, by_chip(REDUCE_L0))))
    (dq, dk, dv, dz, dab, g_a, g_dt, G["l0_gdn_norm"]), got = gdn_bwd(
        qkv, zin, a_row, dt_row, gdn_nw, gdn_states, dy_ab, 0,
        hosted=hosted_scatter([sums[k] for k in REDUCE_L1]) if ex else None)
    landed.update(zip(REDUCE_L1, got))
    G["l0_gdn_A_log"], G["l0_gdn_dt_bias"] = g_a[:, :GDN_HEADS], g_dt[:, :GDN_HEADS]
    dpre, G["l0_gdn_conv"] = gdn_conv_bwd(zin, W["l0_gdn_conv"], jnp.concatenate([dq, dk, dv], axis=1))
    dqkv = conv_transpose(dpre, W["l0_gdn_conv"], BF16, name="gdn_conv_dx")
    dqr = flash_bwd_dq(qr, kv, kpe, y_b, lse, dy_ab, MLA_HEADS)
    (dkv, dkpe), got = flash_bwd_dkv(qr, kv, kpe, y_b, lse, dy_ab, MLA_HEADS,
                                     hosted=hosted_scatter([sums[k] for k in REDUCE_L0]) if ex else None)
    landed.update(zip(REDUCE_L0, got))
    (dq_lin,), _ = rowwise_bwd(_fn_rope_q, [(q_lin, 2048, 0)], [], mla_nd, [(dqr, 2048, 0)], [BF16],
                               name="mla_rope_q_bwd")
    G["l0_mla_w_uq"] = _unprep_w_uq(matmul(cqn, dq_lin, "tn", BF16, name="mla_w_uq_dw"))
    dcqn = matmul(dq_lin, w_uq, "nt", F32, name="mla_w_uq_dx")
    G["l0_mla_w_ukv"] = _unprep_w_ukv(matmul(ckvn, dkv, "tn", BF16, name="mla_w_ukv_dw"))
    dckvn = matmul(dkv, w_ukv, "nt", F32, name="mla_w_ukv_dx")
    (dcq, dckv, dkr), (G["l0_mla_q_norm"], G["l0_mla_kv_norm"]) = rowwise_bwd(
        _fn_mla_pre, mla_rows, [n["l0_mla_q_norm"], n["l0_mla_kv_norm"]], mla_nd,
        [(dcqn, MLA_Q_RANK, 0), (dckvn, MLA_KV_RANK, 0), (dkpe, LANES, 0)], [BF16, BF16, BF16], name="mla_pre_bwd")
    dzin = jnp.concatenate([dqkv, dz, dcq, dckv, dkr, dab.astype(BF16)], axis=1)
    G["l0_w_in"] = _unprep_w_in0(matmul(hn0, dzin, "tn", BF16, tm=512, tn=1792, tk=4096, name="l0_w_in_dw"))
    dhn = matmul(dzin, w_in0, "nt", F32, tm=512, tn=2048, tk=5376, name="l0_w_in_dx")
    (grad_x,), (G["l0_attn_norm"],) = rowwise_bwd(_fn_rms, [(x, D_MODEL, 0)], [n["l0_attn_norm"]], [],
                                                 [(dhn, D_MODEL, 0)], [F32], adds=[(dh, D_MODEL, 0)],
                                                 name="l0_attn_norm_bwd")
    small = {k: G[k] for k in SMALL}
    if not ex:
        return loss_vec[0, 0], grad_x, dict(zip(BIG, by_chip(BIG))), small
    sums.update(zip(REDUCE_LAST, ex.pair_sums(REDUCE_LAST, by_chip(REDUCE_LAST))))
    landed.update(zip(REDUCE_LAST, scatter_chips([sums[k] for k in REDUCE_LAST])))
    return loss_vec[0, 0], grad_x, ex.finish(sums, landed), small


HBM = pl.BlockSpec(memory_space=pltpu.HBM)
VMEM = pl.BlockSpec(memory_space=pltpu.VMEM)


def _place():
    return lax.axis_index("x"), lax.axis_index("y"), lax.axis_index("c")


def _other_chips(x, y):
    return [(1 - x, y), (x, 1 - y), (1 - x, 1 - y)]


def _comm_call(body, *, name, out_shape, in_specs, out_specs, scratch_shapes):
    return pl.pallas_call(body, name=name, out_shape=out_shape, in_specs=in_specs, out_specs=out_specs,
                          scratch_shapes=list(scratch_shapes),
                          compiler_params=pltpu.CompilerParams(vmem_limit_bytes=VMEM_LIMIT_MB << 20))


def _inplace_comm_call(body, bufs, *, name, n_sems):
    n = len(bufs)
    return pl.pallas_call(body, name=name, out_shape=[jax.ShapeDtypeStruct(b.shape, b.dtype) for b in bufs],
                          in_specs=[HBM] * n, out_specs=[HBM] * n, input_output_aliases={i: i for i in range(n)},
                          scratch_shapes=[pltpu.SemaphoreType.DMA((n_sems,)), pltpu.SemaphoreType.DMA((n_sems,))],
                          compiler_params=pltpu.CompilerParams(vmem_limit_bytes=VMEM_LIMIT_MB << 20))(*bufs)


def all_gather_chips(bufs):
    n_sems, start, finish = _gather_phase(len(bufs))
    n = len(bufs)

    def body(*refs):
        outs, send_sems, recv_sems = refs[n:2 * n], refs[2 * n], refs[2 * n + 1]
        start(None, outs, send_sems, recv_sems)
        finish(None, outs, send_sems, recv_sems)

    return _inplace_comm_call(body, bufs, name="all_gather_chips", n_sems=n_sems)


def _gather_phase(n):
    def plan(outs, send_sems, recv_sems):
        x, y, c = _place()

        def copy(w, k, chip, hc, to):
            half = outs[w].shape[1] // 2
            rows = outs[w].at[2 * chip[0] + chip[1], pl.ds(hc * half, half), :]
            return pltpu.make_async_remote_copy(src_ref=rows, dst_ref=rows, send_sem=send_sems.at[6 * w + k],
                                                recv_sem=recv_sems.at[6 * w + k], device_id=to, device_id_type=MESH)

        first = [[copy(w, k, (x, y), c, (*chip, c)) for k, chip in enumerate(_other_chips(x, y))] for w in range(n)]
        passed = [[copy(w, 3 + k, chip, c, (x, y, 1 - c)) for k, chip in enumerate(_other_chips(x, y))]
                  for w in range(n)]
        return copy, first, passed, (x, y, c)

    def start(_, outs, send_sems, recv_sems):
        _, first, _, _ = plan(outs, send_sems, recv_sems)
        for w in range(n):
            for cp in first[w]:
                cp.start()

    def finish(_, outs, send_sems, recv_sems):
        copy, first, passed, (x, y, c) = plan(outs, send_sems, recv_sems)
        chips = _other_chips(x, y)
        for w in range(n):
            for k, chip in enumerate(chips):
                copy(w, k, chip, c, (x, y, c)).wait_recv()
                passed[w][k].start()
        for w in range(n):
            for k, chip in enumerate(chips):
                copy(w, 3 + k, chip, 1 - c, (x, y, c)).wait_recv()
        for w in range(n):
            for cp in first[w] + passed[w]:
                cp.wait_send()

    return 6 * n, start, finish


def hosted_gather(bufs):
    n_sems, start, finish = _gather_phase(len(bufs))
    return Hosted(bufs, [jax.ShapeDtypeStruct(b.shape, b.dtype) for b in bufs], {i: i for i in range(len(bufs))},
                  n_sems, start, finish)


def pair_swap_halves(gs):
    n = len(gs)

    def body(*refs):
        g_refs, o_refs, send_sems, recv_sems = refs[:n], refs[n:2 * n], refs[2 * n], refs[2 * n + 1]
        x, y, c = _place()
        copies = []
        for w in range(n):
            half = g_refs[w].shape[1] // 2
            copies.append(pltpu.make_async_remote_copy(
                src_ref=g_refs[w].at[:, pl.ds((1 - c) * half, half), :], dst_ref=o_refs[w], send_sem=send_sems.at[w],
                recv_sem=recv_sems.at[w], device_id=(x, y, 1 - c), device_id_type=MESH))
        for cp in copies:
            cp.start()
        for cp in copies:
            cp.wait()

    return _comm_call(body, name="pair_swap_halves",
                      out_shape=[jax.ShapeDtypeStruct((N_CHIPS, g.shape[1] // 2, g.shape[2]), g.dtype) for g in gs],
                      in_specs=[HBM] * n, out_specs=[HBM] * n,
                      scratch_shapes=[pltpu.SemaphoreType.DMA((n,)), pltpu.SemaphoreType.DMA((n,))])(*gs)


def scatter_chips(ps):
    n = len(ps)
    n_sems, start, finish = _scatter_phase(n)

    def body(*refs):
        p_refs, o_refs, send_sems, recv_sems = refs[:n], refs[n:2 * n], refs[2 * n], refs[2 * n + 1]
        start(p_refs, o_refs, send_sems, recv_sems)
        finish(p_refs, o_refs, send_sems, recv_sems)

    return _comm_call(body, name="scatter_chips", out_shape=_scatter_shapes(ps), in_specs=[HBM] * n, out_specs=[HBM] * n,
                      scratch_shapes=[pltpu.SemaphoreType.DMA((n_sems,)), pltpu.SemaphoreType.DMA((n_sems,))])(*ps)


def _scatter_shapes(ps):
    return [jax.ShapeDtypeStruct((3,) + p.shape[1:], p.dtype) for p in ps]


def _scatter_phase(n):
    def copies(p_refs, o_refs, send_sems, recv_sems):
        x, y, c = _place()
        return [pltpu.make_async_remote_copy(src_ref=p_refs[w].at[2 * chip[0] + chip[1]], dst_ref=o_refs[w].at[k],
                                             send_sem=send_sems.at[3 * w + k], recv_sem=recv_sems.at[3 * w + k],
                                             device_id=(*chip, c), device_id_type=MESH)
                for w in range(n) for k, chip in enumerate(_other_chips(x, y))]

    def start(*refs):
        for cp in copies(*refs):
            cp.start()

    def finish(*refs):
        for cp in copies(*refs):
            cp.wait()

    return 3 * n, start, finish


def hosted_scatter(ps):
    n_sems, start, finish = _scatter_phase(len(ps))
    return Hosted(ps, _scatter_shapes(ps), {}, n_sems, start, finish)


def pair_join_halves(rs):
    n = len(rs)

    def body(*refs):
        outs, send_sems, recv_sems = refs[n:2 * n], refs[2 * n], refs[2 * n + 1]
        x, y, c = _place()
        copies = []
        for w in range(n):
            half = outs[w].shape[0] // 2
            rows = outs[w].at[pl.ds(c * half, half), :]
            copies.append(pltpu.make_async_remote_copy(src_ref=rows, dst_ref=rows, send_sem=send_sems.at[w],
                                                       recv_sem=recv_sems.at[w], device_id=(x, y, 1 - c),
                                                       device_id_type=MESH))
        for cp in copies:
            cp.start()
        for cp in copies:
            cp.wait()

    return _inplace_comm_call(body, rs, name="pair_join_halves", n_sems=n)


def all_reduce_small(v, name):
    n, L = v.shape
    n_dev = 8

    def body(v_ref, out_ref, buf, send_sems, recv_sems):
        x, y, c = _place()
        me = 4 * x + 2 * y + c
        buf[me] = v_ref[...]

        def copy(k, slot, peer):
            return pltpu.make_async_remote_copy(src_ref=v_ref, dst_ref=buf.at[slot], send_sem=send_sems.at[k],
                                                recv_sem=recv_sems.at[slot],
                                                device_id=(peer // 4, (peer // 2) % 2, peer % 2), device_id_type=MESH)

        sends = [copy(k - 1, me, (me + k) % n_dev) for k in range(1, n_dev)]
        for cp in sends:
            cp.start()
        for k in range(1, n_dev):
            src = (me + k) % n_dev
            copy(0, src, src).wait_recv()
        for cp in sends:
            cp.wait_send()
        acc = buf[0]
        for s in range(1, n_dev):
            acc = acc + buf[s]
        out_ref[...] = acc

    return _comm_call(body, name=name, out_shape=jax.ShapeDtypeStruct((n, L), v.dtype), in_specs=[VMEM], out_specs=VMEM,
                      scratch_shapes=[pltpu.VMEM((n_dev, n, L), v.dtype), pltpu.SemaphoreType.DMA((n_dev - 1,)),
                                      pltpu.SemaphoreType.DMA((n_dev,))])(v)


BF16_ROWS = 16


def _rows_tile(n, row_bytes, budget=1 << 20, mult=SUBLANES):
    best = mult if n % mult == 0 else n
    for t in range(mult, n + 1, mult):
        if n % t == 0 and t * row_bytes <= budget:
            best = t
    return best


def _scalars(*vals):
    return jnp.stack([jnp.asarray(v, jnp.int32) for v in vals])


def cast_to_slot(w, chip, name):
    r, c = w.shape
    tb = _rows_tile(r, c * 4, mult=BF16_ROWS)

    def body(s_ref, w_ref, o_ref):
        o_ref[...] = w_ref[...].astype(BF16)

    spec = pltpu.PrefetchScalarGridSpec(
        num_scalar_prefetch=1, grid=(r // tb,), in_specs=[pl.BlockSpec((tb, c), lambda i, s: (i, 0))],
        out_specs=pl.BlockSpec((None, tb, c), lambda i, s: (s[0], i, 0)))
    return pl.pallas_call(body, name=name, grid_spec=spec, out_shape=jax.ShapeDtypeStruct((N_CHIPS, r, c), BF16),
                          compiler_params=pltpu.CompilerParams(dimension_semantics=("parallel",)))(_scalars(chip), w)


def pair_add(g, got, c, name):
    _, r, w = g.shape
    half = r // 2
    tb = _rows_tile(half, w * 4, mult=BF16_ROWS)
    nb = half // tb

    def body(c_ref, g_ref, got_ref, o_ref):
        o_ref[...] = (g_ref[...].astype(F32) + got_ref[...].astype(F32)).astype(o_ref.dtype)

    spec = pltpu.PrefetchScalarGridSpec(
        num_scalar_prefetch=1, grid=(N_CHIPS, nb),
        in_specs=[pl.BlockSpec((None, tb, w), lambda s, i, c_ref: (s, c_ref[0] * nb + i, 0)),
                  pl.BlockSpec((None, tb, w), lambda s, i, c_ref: (s, i, 0))],
        out_specs=pl.BlockSpec((None, tb, w), lambda s, i, c_ref: (s, i, 0)))
    return pl.pallas_call(body, name=name, grid_spec=spec, out_shape=jax.ShapeDtypeStruct((N_CHIPS, half, w), BF16),
                          compiler_params=pltpu.CompilerParams(dimension_semantics=("parallel", "parallel")))(
        _scalars(c), g, got)


def chip_add(p, got, chip, c, name):
    _, h, w = p.shape
    tb = _rows_tile(h, w * 4, mult=BF16_ROWS)
    nb = h // tb

    def body(s_ref, p_ref, got_ref, o_ref):
        acc = p_ref[...].astype(F32)
        for k in range(3):
            acc = acc + got_ref[k].astype(F32)
        o_ref[...] = acc

    spec = pltpu.PrefetchScalarGridSpec(
        num_scalar_prefetch=1, grid=(nb,),
        in_specs=[pl.BlockSpec((None, tb, w), lambda i, s: (s[0], i, 0)),
                  pl.BlockSpec((3, tb, w), lambda i, s: (0, i, 0))],
        out_specs=pl.BlockSpec((tb, w), lambda i, s: (s[1] * nb + i, 0)))
    return pl.pallas_call(body, name=name, grid_spec=spec, out_shape=jax.ShapeDtypeStruct((2 * h, w), F32),
                          compiler_params=pltpu.CompilerParams(dimension_semantics=("parallel",)))(
        _scalars(chip, c), p, got)


def adamw(w, g, m, v, name):
    r, c = w.shape
    tr = _rows_tile(r, c * 4)

    def body(w_ref, g_ref, m_ref, v_ref, d_ref, m_out, v_out):
        gg = g_ref[...]
        m2 = ADAM_B1 * m_ref[...] + (1.0 - ADAM_B1) * gg
        v2 = ADAM_B2 * v_ref[...] + (1.0 - ADAM_B2) * jnp.square(gg)
        m_hat = m2 / (1.0 - ADAM_B1 ** ADAM_STEP)
        v_hat = v2 / (1.0 - ADAM_B2 ** ADAM_STEP)
        d_ref[...] = -ADAM_LR * (m_hat / (jnp.sqrt(v_hat) + ADAM_EPS) + ADAM_WD * w_ref[...])
        m_out[...] = m2
        v_out[...] = v2

    blk = pl.BlockSpec((tr, c), lambda i: (i, 0))
    return _pcall(body, name=name, grid=(r // tr,), in_specs=[blk] * 4, out_specs=[blk] * 3,
                  out_shape=[jax.ShapeDtypeStruct((r, c), F32)] * 3, dims=("parallel",))(w, g, m, v)


WEIGHTS = ["l0_attn_norm", "l0_w_in", "l0_gdn_conv", "l0_gdn_A_log", "l0_gdn_dt_bias", "l0_gdn_norm", "l0_mla_q_norm",
           "l0_mla_w_uq", "l0_mla_kv_norm", "l0_mla_w_ukv", "l0_w_out", "l0_ffn_norm", "l0_ffn_w_up", "l0_ffn_conv_w",
           "l0_ffn_conv_b", "l0_ffn_w_down", "l0_ple_proj", "l0_ple_gate_norm", "l0_ple_gate", "l1_attn_norm",
           "l1_w_in", "l1_ret_norm", "l1_w_out", "l1_ffn_norm", "l1_ffn_w_up", "l1_ffn_conv_w", "l1_ffn_conv_b",
           "l1_ffn_w_down", "l1_ple_proj", "l1_ple_gate_norm", "l1_ple_gate", "final_norm"]
COL_SHARDED = ["l0_w_in", "l0_mla_w_uq", "l0_mla_w_ukv", "l0_ffn_w_up", "l0_ple_proj", "l1_w_in", "l1_ffn_w_up",
               "l1_ple_proj"]
ROW_SHARDED = ["l0_w_out", "l0_ffn_w_down", "l0_ple_gate", "l1_w_out", "l1_ffn_w_down", "l1_ple_gate"]
BIG = [k for k in WEIGHTS if k in COL_SHARDED or k in ROW_SHARDED]
SMALL_SHARDED = ["l0_gdn_conv", "l0_ffn_conv_w", "l1_ffn_conv_w"]
SMALL = [k for k in WEIGHTS if k not in BIG]
KEPT_SHARDED = ["l0_ffn_w_up", "l0_ple_proj", "l1_w_in", "l1_ffn_w_up", "l1_ple_proj"]
GATHER_FIRST = ["l0_w_in", "l0_mla_w_uq", "l0_mla_w_ukv", "l0_w_out"]
GATHER_L0 = ["l0_ffn_w_up", "l0_ffn_w_down", "l0_ple_proj", "l0_ple_gate", "l1_w_out"]
GATHER_L1 = ["l1_w_in", "l1_ffn_w_down", "l1_ple_proj", "l1_ple_gate"]
GATHER_RET = ["l1_ffn_w_up"]
REDUCE_L1 = [k for k in BIG if k.startswith("l1_")]
REDUCE_L0 = ["l0_ffn_w_up", "l0_ffn_w_down", "l0_ple_proj", "l0_ple_gate", "l0_w_out"]
REDUCE_LAST = ["l0_w_in", "l0_mla_w_uq", "l0_mla_w_ukv"]


class Exchange:
    def __init__(self, chip, core):
        self.chip, self.core = chip, core

    def gather(self, bufs):
        return all_gather_chips(bufs)

    def pair_sums(self, names, grads):
        return [pair_add(g, got, self.core, "rs_pair_add_" + k)
                for k, g, got in zip(names, grads, pair_swap_halves(grads))]

    def finish(self, sums, landed):
        halves = [chip_add(sums[k], landed[k], self.chip, self.core, "rs_chip_add_" + k) for k in BIG]
        return dict(zip(BIG, pair_join_halves(halves)))


def _cols_to_full(s):
    j, k, n = s.shape
    return jnp.transpose(s, (1, 0, 2)).reshape(k, j * n)


def _full_to_cols(g):
    k, n4 = g.shape
    return jnp.transpose(g.reshape(k, N_CHIPS, n4 // N_CHIPS), (1, 0, 2))


def _pack_small(vals):
    flat = jnp.concatenate([v.astype(F32).reshape(-1) for v in vals])
    align = SUBLANES * LANES
    flat = jnp.pad(flat, (0, -flat.shape[0] % align))
    return flat.reshape(-1, LANES)


def _unpack_small(rows, shapes):
    flat = rows.reshape(-1)
    out, off = [], 0
    for shp in shapes:
        n = int(np.prod(shp))
        out.append(flat[off:off + n].reshape(shp))
        off += n
    return out


INPUTS = (["x", "p", "positions"] + WEIGHTS + ["loss_target"] + ["m_" + k for k in WEIGHTS]
          + ["v_" + k for k in WEIGHTS])


def kernel(
        x, p, positions, l0_attn_norm, l0_w_in, l0_gdn_conv, l0_gdn_A_log, l0_gdn_dt_bias, l0_gdn_norm, l0_mla_q_norm,
        l0_mla_w_uq, l0_mla_kv_norm, l0_mla_w_ukv, l0_w_out, l0_ffn_norm, l0_ffn_w_up, l0_ffn_conv_w, l0_ffn_conv_b,
        l0_ffn_w_down, l0_ple_proj, l0_ple_gate_norm, l0_ple_gate, l1_attn_norm, l1_w_in, l1_ret_norm, l1_w_out,
        l1_ffn_norm, l1_ffn_w_up, l1_ffn_conv_w, l1_ffn_conv_b, l1_ffn_w_down, l1_ple_proj, l1_ple_gate_norm,
        l1_ple_gate, final_norm, loss_target, m_l0_attn_norm, m_l0_w_in, m_l0_gdn_conv, m_l0_gdn_A_log,
        m_l0_gdn_dt_bias, m_l0_gdn_norm, m_l0_mla_q_norm, m_l0_mla_w_uq, m_l0_mla_kv_norm, m_l0_mla_w_ukv, m_l0_w_out,
        m_l0_ffn_norm, m_l0_ffn_w_up, m_l0_ffn_conv_w, m_l0_ffn_conv_b, m_l0_ffn_w_down, m_l0_ple_proj,
        m_l0_ple_gate_norm, m_l0_ple_gate, m_l1_attn_norm, m_l1_w_in, m_l1_ret_norm, m_l1_w_out, m_l1_ffn_norm,
        m_l1_ffn_w_up, m_l1_ffn_conv_w, m_l1_ffn_conv_b, m_l1_ffn_w_down, m_l1_ple_proj, m_l1_ple_gate_norm,
        m_l1_ple_gate, m_final_norm, v_l0_attn_norm, v_l0_w_in, v_l0_gdn_conv, v_l0_gdn_A_log, v_l0_gdn_dt_bias,
        v_l0_gdn_norm, v_l0_mla_q_norm, v_l0_mla_w_uq, v_l0_mla_kv_norm, v_l0_mla_w_ukv, v_l0_w_out, v_l0_ffn_norm,
        v_l0_ffn_w_up, v_l0_ffn_conv_w, v_l0_ffn_conv_b, v_l0_ffn_w_down, v_l0_ple_proj, v_l0_ple_gate_norm,
        v_l0_ple_gate, v_l1_attn_norm, v_l1_w_in, v_l1_ret_norm, v_l1_w_out, v_l1_ffn_norm, v_l1_ffn_w_up,
        v_l1_ffn_conv_w, v_l1_ffn_conv_b, v_l1_ffn_w_down, v_l1_ple_proj, v_l1_ple_gate_norm, v_l1_ple_gate,
        v_final_norm):
    given = locals()
    a = {k: given[k] for k in INPUTS}
    x_i, y_i, c_i = _place()
    chip = 2 * x_i + y_i
    shard_shapes = {k: a[k].shape for k in WEIGHTS}

    slots = {k: cast_to_slot(a[k], chip, "cast_" + k) for k in BIG}
    W = {}
    placed = []
    for k in SMALL_SHARDED:
        r, c = shard_shapes[k]
        mine = jnp.where(c_i == 0, a[k], jnp.zeros_like(a[k]))
        placed.append(lax.dynamic_update_slice(jnp.zeros((r, N_CHIPS * c), F32), mine, (0, chip * c)))
    full_small = _unpack_small(all_reduce_small(_pack_small(placed), "gather_small_weights"),
                               [p_.shape for p_ in placed])
    for k in SMALL:
        W[k] = a[k]
    W.update(dict(zip(SMALL_SHARDED, full_small)))

    loss_part, grad_x, grads, G = local_step(a["x"][0], a["p"][:, 0], a["positions"][0], a["loss_target"][0], slots, W,
                                             Exchange(chip, c_i))
    loss = lax.psum(loss_part, ("x", "y", "c"))
    deltas, new_m, new_v = {}, {}, {}
    for k in BIG:
        deltas[k], new_m[k], new_v[k] = adamw(a[k], grads[k], a["m_" + k], a["v_" + k], "adamw_" + k)

    small_full = [G[k].reshape(-1) for k in SMALL]
    summed = _unpack_small(all_reduce_small(_pack_small(small_full), "reduce_small_grads"),
                           [G[k].shape for k in SMALL])
    for k, g in zip(SMALL, summed):
        if k in SMALL_SHARDED:
            r, c = shard_shapes[k]
            g = lax.dynamic_slice(g.reshape(r, N_CHIPS * c), (0, chip * c), (r, c))
        grads[k] = g.reshape(shard_shapes[k])
    packed = [_pack_small([d[k] for k in SMALL]) for d in (
        {k: a[k] for k in SMALL}, grads, {k: a["m_" + k] for k in SMALL}, {k: a["v_" + k] for k in SMALL})]
    outs = adamw(*packed, "adamw_small")
    shapes = [shard_shapes[k] for k in SMALL]
    for d, rows in zip((deltas, new_m, new_v), outs):
        d.update(dict(zip(SMALL, _unpack_small(rows, shapes))))

    return (loss, grad_x[None], *[grads[k] for k in WEIGHTS], *[deltas[k] for k in WEIGHTS],
            *[new_m[k] for k in WEIGHTS], *[new_v[k] for k in WEIGHTS])
```

```python
import functools
import math

import numpy as np
import jax
import jax.numpy as jnp
from jax import lax
from jax.experimental import pallas as pl
from jax.experimental.pallas import tpu as pltpu

F32, BF16 = jnp.float32, jnp.bfloat16
HI = lax.Precision.HIGHEST
MESH = pl.DeviceIdType.MESH

NORM_EPS = 1e-6
ROPE_THETA = 10000.0
D_MODEL = 2048
PLE_DIM = 256
GDN_HEADS, GDN_DK, GDN_DV, GDN_CONV = 8, 128, 128, 4
MLA_HEADS, MLA_Q_RANK, MLA_KV_RANK, MLA_NOPE, MLA_ROPE, MLA_V = 8, 512, 512, 128, 64, 128
RET_HEADS, RET_DK, RET_DV = 8, 256, 512
D_FF, FFN_CONV = 5632, 3
ADAM_LR, ADAM_B1, ADAM_B2, ADAM_EPS, ADAM_WD, ADAM_STEP = 0.001, 0.9, 0.999, 1e-08, 0.01, 10

LANES = 128
SUBLANES = 8
CHUNK = 128
N_CHIPS = 4
VMEM_LIMIT_MB = 56

ZIN_QKV, ZIN_Z, ZIN_CQ, ZIN_CKV, ZIN_KR, ZIN_AB, ZIN_W = 0, 3072, 4096, 4608, 5120, 5248, 5376


class Hosted:
    def __init__(self, inputs, out_shapes, aliases, n_sems, start, finish):
        self.inputs, self.out_shapes, self.aliases, self.n_sems = list(inputs), list(out_shapes), dict(aliases), n_sems
        self.start, self.finish = start, finish


def _pcall(body, *, name, out_shape, grid=(), in_specs=None, out_specs=None, scratch_shapes=(), dims=None,
           hosted=None):
    params = dict(vmem_limit_bytes=VMEM_LIMIT_MB << 20)
    if dims is not None:
        params["dimension_semantics"] = dims
    if hosted is None:
        return pl.pallas_call(body, name=name, out_shape=out_shape, grid=grid, in_specs=in_specs, out_specs=out_specs,
                              scratch_shapes=list(scratch_shapes), compiler_params=pltpu.CompilerParams(**params))
    single = not isinstance(out_shape, (list, tuple))
    out_shape = [out_shape] if single else list(out_shape)
    out_specs = [out_specs] if single else list(out_specs)
    n_in, n_out, n_scr = len(in_specs), len(out_shape), len(scratch_shapes)
    h_in, h_out = len(hosted.inputs), len(hosted.out_shapes)
    hbm = pl.BlockSpec(memory_space=pltpu.HBM)

    def hosting_body(*refs):
        ins, h_ins = refs[:n_in], refs[n_in:n_in + h_in]
        o0 = n_in + h_in
        outs, h_outs = refs[o0:o0 + n_out], refs[o0 + n_out:o0 + n_out + h_out]
        s0 = o0 + n_out + h_out
        scr, (send_sems, recv_sems) = refs[s0:s0 + n_scr], refs[s0 + n_scr:]
        ids = [pl.program_id(d) for d in range(len(grid))]
        first = functools.reduce(lambda u, v: u & v, [i == 0 for i in ids])
        last = functools.reduce(lambda u, v: u & v, [i == g - 1 for i, g in zip(ids, grid)])

        @pl.when(first)
        def _():
            hosted.start(h_ins, h_outs, send_sems, recv_sems)

        body(*ins, *outs, *scr)

        @pl.when(last)
        def _():
            hosted.finish(h_ins, h_outs, send_sems, recv_sems)

    params["dimension_semantics"] = ("arbitrary",) * len(grid)
    call = pl.pallas_call(
        hosting_body, name=name, out_shape=out_shape + hosted.out_shapes, grid=grid,
        in_specs=list(in_specs) + [hbm] * h_in, out_specs=out_specs + [hbm] * h_out,
        scratch_shapes=list(scratch_shapes) + [pltpu.SemaphoreType.DMA((hosted.n_sems,)),
                                               pltpu.SemaphoreType.DMA((hosted.n_sems,))],
        input_output_aliases={n_in + i: n_out + o for i, o in hosted.aliases.items()},
        compiler_params=pltpu.CompilerParams(**params))

    def run(*args):
        res = call(*args, *hosted.inputs)
        main = res[:n_out]
        return (main[0] if single else main), list(res[n_out:])

    return run


def _tile(n, target, mult=LANES):
    best = None
    for t in range(mult, min(n, target) + 1, mult):
        if n % t == 0:
            best = t
    return best or n


_DN = {"nn": (((1,), (0,)), ((), ())), "nt": (((1,), (1,)), ((), ())), "tn": (((0,), (0,)), ((), ()))}


def matmul(a, b, mode, out_dtype, *, name, add=None, b_shards=1, out_shards=1, tm=512, tn=1024, tk=2048,
           hosted=None):
    bs = b.shape[-2:]
    if mode == "nn":
        (M, K), (K2, N) = a.shape, (bs[0], bs[1] * b_shards)
    elif mode == "nt":
        (M, K), (N, K2) = a.shape, (bs[0], bs[1] * b_shards)
    else:
        (K, M), (K2, N) = a.shape, bs
    assert K == K2, (name, a.shape, b.shape)
    n_sh = N // max(b_shards if mode == "nn" else 1, out_shards)
    k_sh = K // (b_shards if mode == "nt" else 1)
    tm, tn, tk = _tile(M, tm), _tile(n_sh, tn), _tile(k_sh, tk)
    nk = K // tk
    nbn, nbk = n_sh // tn, k_sh // tk
    dn = _DN[mode]
    has_add = add is not None
    a_bytes, b_bytes = a.size * a.dtype.itemsize, b.size * b.dtype.itemsize
    i_outer = nk > 1 or a_bytes + (M // tm) * b_bytes <= b_bytes + (N // tn) * a_bytes

    def ij(g0, g1):
        return (g0, g1) if i_outer else (g1, g0)

    def body(*refs):
        a_ref, b_ref = refs[:2]
        add_ref = refs[2] if has_add else None
        o_ref = refs[3 if has_add else 2]
        part = lax.dot_general(a_ref[...].astype(BF16), b_ref[...].astype(BF16), dn, preferred_element_type=F32)

        def finish(r):
            if has_add:
                r = r + add_ref[...]
            o_ref[...] = r.astype(out_dtype)

        if nk == 1:
            finish(part)
            return
        acc = refs[-1]
        k = pl.program_id(2)

        @pl.when(k == 0)
        def _():
            acc[...] = part

        @pl.when(k > 0)
        def _():
            acc[...] += part

        @pl.when(k == nk - 1)
        def _():
            finish(acc[...])

    def spec(block, fn):
        return pl.BlockSpec(block, lambda g0, g1, k: fn(*ij(g0, g1), k))

    if mode == "tn":
        a_spec = spec((tk, tm), lambda i, j, k: (k, i))
    else:
        a_spec = spec((tm, tk), lambda i, j, k: (i, k))
    if mode == "nt":
        if b_shards > 1:
            b_spec = spec((None, tn, tk), lambda i, j, k: (k // nbk, j, k % nbk))
        else:
            b_spec = spec((tn, tk), lambda i, j, k: (j, k))
    elif b_shards > 1:
        b_spec = spec((None, tk, tn), lambda i, j, k: (j // nbn, k, j % nbn))
    else:
        b_spec = spec((tk, tn), lambda i, j, k: (k, j))
    in_specs = [a_spec, b_spec]
    args = [a, b]
    if has_add:
        in_specs.append(spec((tm, tn), lambda i, j, k: (i, j)))
        args.append(add)
    if out_shards > 1:
        out_spec = spec((None, tm, tn), lambda i, j, k: (j // nbn, i, j % nbn))
        out_shape = jax.ShapeDtypeStruct((out_shards, M, n_sh), out_dtype)
    else:
        out_spec = spec((tm, tn), lambda i, j, k: (i, j))
        out_shape = jax.ShapeDtypeStruct((M, N), out_dtype)
    gi, gj = M // tm, N // tn
    return _pcall(body, name=name, out_shape=out_shape, grid=(gi, gj, nk) if i_outer else (gj, gi, nk),
                  in_specs=in_specs, out_specs=out_spec,
                  scratch_shapes=[pltpu.VMEM((tm, tn), F32)] if nk > 1 else [],
                  dims=("parallel", "parallel", "arbitrary"), hosted=hosted)(*args)


def _row_spec(tr, w, c):
    return pl.BlockSpec((tr, w), lambda i: (i, c))


def _full_spec(arr):
    return pl.BlockSpec(arr.shape, lambda i: (0,) * arr.ndim)


def rowwise(fn, rows, params, nd_rows, outs, *, name, tr=256):
    S = rows[0][0].shape[0]
    tr = min(tr, S)
    n_in = len(rows) + len(params) + len(nd_rows)

    def body(*refs):
        res = fn(*[x[...] for x in refs[:n_in]])
        for o_ref, v in zip(refs[n_in:], res):
            o_ref[...] = v.astype(o_ref.dtype)

    return _pcall(body, name=name, grid=(S // tr,),
                  in_specs=([_row_spec(tr, w, c) for (_, w, c) in rows] + [_full_spec(q) for q in params]
                            + [_row_spec(tr, w, c) for (_, w, c) in nd_rows]),
                  out_specs=[_row_spec(tr, w, 0) for (w, _) in outs],
                  out_shape=[jax.ShapeDtypeStruct((S, w), dt) for (w, dt) in outs],
                  dims=("parallel",))(*[r[0] for r in rows], *params, *[r[0] for r in nd_rows])


def rowwise_bwd(fn, rows, params, nd_rows, cts, d_dtypes, *, name, adds=None, tr=256):
    S = rows[0][0].shape[0]
    tr = min(tr, S)
    n_r, n_p, n_n, n_c = len(rows), len(params), len(nd_rows), len(cts)
    adds = adds or [None] * n_r
    add_list = [a for a in adds if a is not None]
    n_a = len(add_list)
    d_dtypes = [dt if isinstance(dt, (list, tuple)) else (dt,) for dt in d_dtypes]
    n_d = sum(len(dt) for dt in d_dtypes)

    def body(*refs):
        it = iter(refs)
        r = [next(it)[...] for _ in range(n_r)]
        p = [next(it)[...] for _ in range(n_p)]
        nd = [next(it)[...] for _ in range(n_n)]
        c = [next(it)[...] for _ in range(n_c)]
        ad = [next(it)[...] for _ in range(n_a)]
        d_row_refs = [[next(it) for _ in dts] for dts in d_dtypes]
        d_par_refs = [next(it) for _ in range(n_p)]
        outs, vjp = jax.vjp(lambda *dp: fn(*dp, *nd), *r, *p)
        g = vjp(tuple(ci.astype(o.dtype) for ci, o in zip(c, outs)))
        ai = 0
        for k in range(n_r):
            gk = g[k].astype(F32)
            if adds[k] is not None:
                gk = gk + ad[ai].astype(F32)
                ai += 1
            for ref in d_row_refs[k]:
                ref[...] = gk.astype(ref.dtype)

        @pl.when(pl.program_id(0) == 0)
        def _():
            for ref in d_par_refs:
                ref[...] = jnp.zeros_like(ref)

        for k in range(n_p):
            d_par_refs[k][...] += g[n_r + k].astype(F32)

    in_specs = ([_row_spec(tr, w, c) for (_, w, c) in rows] + [_full_spec(q) for q in params]
                + [_row_spec(tr, w, c) for (_, w, c) in nd_rows] + [_row_spec(tr, w, c) for (_, w, c) in cts]
                + [_row_spec(tr, w, c) for (_, w, c) in add_list])
    out_specs = ([_row_spec(tr, w, 0) for (_, w, _), dts in zip(rows, d_dtypes) for _ in dts]
                 + [_full_spec(q) for q in params])
    out_shape = ([jax.ShapeDtypeStruct((S, w), dt) for (_, w, _), dts in zip(rows, d_dtypes) for dt in dts]
                 + [jax.ShapeDtypeStruct(q.shape, F32) for q in params])
    res = _pcall(body, name=name, grid=(S // tr,), in_specs=in_specs, out_specs=out_specs, out_shape=out_shape,
                 dims=("arbitrary",))(*[r[0] for r in rows], *params, *[r[0] for r in nd_rows],
                                      *[r[0] for r in cts], *[r[0] for r in add_list])
    d_rows, i = [], 0
    for dts in d_dtypes:
        d_rows.append(res[i] if len(dts) == 1 else tuple(res[i:i + len(dts)]))
        i += len(dts)
    return d_rows, res[n_d:]


def _rms(x, g):
    x = x.astype(F32)
    return x * lax.rsqrt(jnp.mean(x * x, axis=-1, keepdims=True) + NORM_EPS) * g


def _fn_rms(x, g):
    return (_rms(x, g),)


def _sigmoid(x):
    return 1.0 / (1.0 + jnp.exp(-x))


def _silu(x):
    return x * _sigmoid(x)


def _softplus(x):
    return jnp.maximum(x, 0.0) + jnp.log(1.0 + jnp.exp(-jnp.abs(x)))


def _fn_ple(h, pp, gl):
    return (h.astype(F32) + pp.astype(F32) * _sigmoid(gl.astype(F32)),)


def _fn_ple_terms(pp, gl):
    return (pp.astype(F32) * _sigmoid(gl.astype(F32)),)


def _rot_half_matrix():
    half = MLA_ROPE // 2
    r = lax.broadcasted_iota(jnp.int32, (LANES, LANES), 0)
    c = lax.broadcasted_iota(jnp.int32, (LANES, LANES), 1)
    plus = (c == r + half) & (r < half)
    minus = (r == c + half) & (c < half)
    return jnp.where(plus, 1.0, 0.0) - jnp.where(minus, 1.0, 0.0)


def _rope_pad(x, cosp, sinp):
    return x * cosp + jnp.dot(x, _rot_half_matrix(), precision=HI, preferred_element_type=F32) * sinp


def _fn_mla_pre(cq, ckv, kr, qn_w, kvn_w, cosp, sinp):
    return (_rms(cq, qn_w), _rms(ckv, kvn_w), _rope_pad(kr.astype(F32), cosp, sinp))


def _fn_rope_q(q, cosp, sinp):
    q = q.astype(F32)
    parts = []
    for h in range(MLA_HEADS):
        base = 2 * LANES * h
        parts.append(q[:, base:base + LANES])
        parts.append(_rope_pad(q[:, base + LANES:base + 2 * LANES], cosp, sinp))
    return (jnp.concatenate(parts, axis=1),)


def _fn_ret_gate(g, on, w):
    return (_silu(g.astype(F32)) * (on.astype(F32) * w),)


def _shift_down(cur, halo, s):
    if s == 0:
        return cur
    r = pltpu.roll(cur, s, 0)
    hs = pltpu.roll(halo, s, 0)
    row = lax.broadcasted_iota(jnp.int32, hs.shape, 0)
    first = jnp.where(row < s, hs, r[:SUBLANES])
    return jnp.concatenate([first, r[SUBLANES:]], axis=0)


def _shift_up(cur, halo, s):
    if s == 0:
        return cur
    n = cur.shape[0]
    r = pltpu.roll(cur, n - s, 0)
    hs = pltpu.roll(halo, SUBLANES - s, 0)
    row = lax.broadcasted_iota(jnp.int32, hs.shape, 0)
    last = jnp.where(row >= SUBLANES - s, hs, r[n - SUBLANES:])
    return jnp.concatenate([r[:n - SUBLANES], last], axis=0)


def _prev_halo_spec(tr, tw, col):
    return pl.BlockSpec((SUBLANES, tw), lambda c, i: (jnp.maximum(i * (tr // SUBLANES) - 1, 0), col(c)))


def _conv_taps(cur, halo, w_ref, width):
    taps = [_shift_down(cur, halo, width - 1 - j) for j in range(width)]
    y = taps[0] * w_ref[0:1, :]
    for j in range(1, width):
        y = y + taps[j] * w_ref[j:j + 1, :]
    return y, taps


def gdn_conv_fwd(zin, w, *, tr=512, tw=512):
    S = zin.shape[0]
    tr = min(tr, S)
    width, C = w.shape

    def body(cur_ref, halo_ref, w_ref, o_ref):
        i = pl.program_id(1)
        halo = halo_ref[...] * (i > 0).astype(F32)
        y, _ = _conv_taps(cur_ref[...], halo, w_ref, width)
        o_ref[...] = _silu(y)

    return _pcall(body, name="gdn_conv_fwd", grid=(C // tw, S // tr),
                  in_specs=[pl.BlockSpec((tr, tw), lambda c, i: (i, c)), _prev_halo_spec(tr, tw, lambda c: c),
                            pl.BlockSpec((width, tw), lambda c, i: (0, c))],
                  out_specs=pl.BlockSpec((tr, tw), lambda c, i: (i, c)),
                  out_shape=jax.ShapeDtypeStruct((S, C), F32), dims=("parallel", "arbitrary"))(zin, zin, w)


def gdn_conv_bwd(zin, w, dy, *, tr=512, tw=512):
    S = zin.shape[0]
    tr = min(tr, S)
    width, C = w.shape

    def body(cur_ref, halo_ref, w_ref, dy_ref, da_ref, dw_ref):
        i = pl.program_id(1)
        halo = halo_ref[...] * (i > 0).astype(F32)
        y, taps = _conv_taps(cur_ref[...], halo, w_ref, width)
        sg = _sigmoid(y)
        da = dy_ref[...] * (sg * (1.0 + y * (1.0 - sg)))
        da_ref[...] = da

        @pl.when(i == 0)
        def _():
            dw_ref[...] = jnp.zeros_like(dw_ref)

        for j in range(width):
            dw_ref[j:j + 1, :] += jnp.sum(da * taps[j], axis=0, keepdims=True)

    return _pcall(body, name="gdn_conv_bwd", grid=(C // tw, S // tr),
                  in_specs=[pl.BlockSpec((tr, tw), lambda c, i: (i, c)), _prev_halo_spec(tr, tw, lambda c: c),
                            pl.BlockSpec((width, tw), lambda c, i: (0, c)),
                            pl.BlockSpec((tr, tw), lambda c, i: (i, c))],
                  out_specs=[pl.BlockSpec((tr, tw), lambda c, i: (i, c)),
                             pl.BlockSpec((width, tw), lambda c, i: (0, c))],
                  out_shape=[jax.ShapeDtypeStruct((S, C), F32), jax.ShapeDtypeStruct((width, C), F32)],
                  dims=("parallel", "arbitrary"))(zin, zin, w, dy)


def conv_transpose(dy, w, out_dtype, *, name, tr=512, tw=512):
    if dy.ndim == 2:
        dy = dy[None]
    T, S, C = dy.shape
    tr = min(tr, S)
    width = w.shape[0]
    n_i, nc = S // tr, C // tw

    def body(cur_ref, halo_ref, w_ref, o_ref):
        i = pl.program_id(2)
        halo = halo_ref[...] * (i < n_i - 1).astype(F32)
        cur = cur_ref[...]
        acc = cur * w_ref[width - 1:width, :]
        for s in range(1, width):
            acc = acc + _shift_up(cur, halo, s) * w_ref[width - 1 - s:width - s, :]
        o_ref[...] = acc.astype(out_dtype)

    nxt = pl.BlockSpec((None, SUBLANES, tw),
                       lambda t, c, i: (t, jnp.minimum((i + 1) * (tr // SUBLANES), S // SUBLANES - 1), c))
    return _pcall(body, name=name, grid=(T, nc, n_i),
                  in_specs=[pl.BlockSpec((None, tr, tw), lambda t, c, i: (t, i, c)), nxt,
                            pl.BlockSpec((width, tw), lambda t, c, i: (0, t * nc + c))],
                  out_specs=pl.BlockSpec((tr, tw), lambda t, c, i: (i, t * nc + c)),
                  out_shape=jax.ShapeDtypeStruct((S, T * C), out_dtype),
                  dims=("parallel", "parallel", "arbitrary"))(dy, dy, w)


def ffn_conv_fwd(u, w, b, *, tr=512, tw=512):
    S, C2 = u.shape
    tr = min(tr, S)
    width = w.shape[0]
    half = C2 // 2
    nc = half // tw

    def body(g_ref, gh_ref, u_ref, uh_ref, wg_ref, wu_ref, bg_ref, bu_ref, o_ref):
        i = pl.program_id(1)
        live = (i > 0).astype(F32)
        yg, _ = _conv_taps(g_ref[...], gh_ref[...] * live, wg_ref, width)
        yu, _ = _conv_taps(u_ref[...], uh_ref[...] * live, wu_ref, width)
        o_ref[...] = (_silu(yg + bg_ref[...]) * (yu + bu_ref[...])).astype(o_ref.dtype)

    return _pcall(body, name="ffn_conv_fwd", grid=(nc, S // tr),
                  in_specs=[pl.BlockSpec((tr, tw), lambda c, i: (i, c)), _prev_halo_spec(tr, tw, lambda c: c),
                            pl.BlockSpec((tr, tw), lambda c, i: (i, c + nc)),
                            _prev_halo_spec(tr, tw, lambda c: c + nc),
                            pl.BlockSpec((width, tw), lambda c, i: (0, c)),
                            pl.BlockSpec((width, tw), lambda c, i: (0, c + nc)),
                            pl.BlockSpec((1, tw), lambda c, i: (0, c)), pl.BlockSpec((1, tw), lambda c, i: (0, c + nc))],
                  out_specs=pl.BlockSpec((tr, tw), lambda c, i: (i, c)),
                  out_shape=jax.ShapeDtypeStruct((S, half), BF16),
                  dims=("parallel", "arbitrary"))(u, u, u, u, w, w, b, b)


def ffn_conv_bwd(u, w, b, df, *, tr=512, tw=512):
    S, C2 = u.shape
    tr = min(tr, S)
    width = w.shape[0]
    half = C2 // 2
    nc = half // tw

    def body(g_ref, gh_ref, u_ref, uh_ref, wg_ref, wu_ref, bg_ref, bu_ref, df_ref, dc_ref, dw_ref, db_ref):
        i = pl.program_id(1)
        live = (i > 0).astype(F32)
        yg, gt = _conv_taps(g_ref[...], gh_ref[...] * live, wg_ref, width)
        yu, ut = _conv_taps(u_ref[...], uh_ref[...] * live, wu_ref, width)
        yg = yg + bg_ref[...]
        yu = yu + bu_ref[...]
        sg = _sigmoid(yg)
        dfv = df_ref[...].astype(F32)
        dcs = (dfv * yu * (sg * (1.0 + yg * (1.0 - sg))), dfv * (yg * sg))

        @pl.when(i == 0)
        def _():
            dw_ref[...] = jnp.zeros_like(dw_ref)
            db_ref[...] = jnp.zeros_like(db_ref)

        for t, (dc, taps) in enumerate(zip(dcs, (gt, ut))):
            dc_ref[t] = dc
            db_ref[t] += jnp.sum(dc, axis=0, keepdims=True)
            for j in range(width):
                dw_ref[t, j:j + 1, :] += jnp.sum(dc * taps[j], axis=0, keepdims=True)

    dc, dw, db = _pcall(
        body, name="ffn_conv_bwd", grid=(nc, S // tr),
        in_specs=[pl.BlockSpec((tr, tw), lambda c, i: (i, c)), _prev_halo_spec(tr, tw, lambda c: c),
                  pl.BlockSpec((tr, tw), lambda c, i: (i, c + nc)), _prev_halo_spec(tr, tw, lambda c: c + nc),
                  pl.BlockSpec((width, tw), lambda c, i: (0, c)), pl.BlockSpec((width, tw), lambda c, i: (0, c + nc)),
                  pl.BlockSpec((1, tw), lambda c, i: (0, c)), pl.BlockSpec((1, tw), lambda c, i: (0, c + nc)),
                  pl.BlockSpec((tr, tw), lambda c, i: (i, c))],
        out_specs=[pl.BlockSpec((2, tr, tw), lambda c, i: (0, i, c)), pl.BlockSpec((2, width, tw), lambda c, i: (0, 0, c)),
                   pl.BlockSpec((2, 1, tw), lambda c, i: (0, 0, c))],
        out_shape=[jax.ShapeDtypeStruct((2, S, half), F32), jax.ShapeDtypeStruct((2, width, half), F32),
                   jax.ShapeDtypeStruct((2, 1, half), F32)],
        dims=("parallel", "arbitrary"))(u, u, u, u, w, w, b, b, df)
    return dc, jnp.concatenate([dw[0], dw[1]], axis=1), jnp.concatenate([db[0], db[1]], axis=1)


_MODE_OF = {v: k for k, v in _DN.items()}


def _bf16_dot(a, b, mode):
    return lax.dot_general(a.astype(BF16), b.astype(BF16), _DN[mode], preferred_element_type=F32)


@functools.partial(jax.custom_vjp, nondiff_argnums=(2,))
def _bdot_mode(a, b, mode):
    return _bf16_dot(a, b, mode)


def _bdot_fwd(a, b, mode):
    return _bf16_dot(a, b, mode), (a, b)


def _bdot_bwd(mode, res, ct):
    a, b = res
    if mode == "nn":
        da, db = _bf16_dot(ct, b, "nt"), _bf16_dot(a, ct, "tn")
    elif mode == "nt":
        da, db = _bf16_dot(ct, b, "nn"), _bf16_dot(ct, a, "tn")
    else:
        da, db = _bf16_dot(b, ct, "nt"), _bf16_dot(a, ct, "nn")
    return da.astype(a.dtype), db.astype(b.dtype)


_bdot_mode.defvjp(_bdot_fwd, _bdot_bwd)


def _bdot(a, b, dn=_DN["nn"]):
    return _bdot_mode(a, b, _MODE_OF[dn])


def _hi_lo(x):
    hi = x.astype(BF16)
    return hi, (x - hi.astype(F32)).astype(BF16)


def _dot3_raw(a, b, mode):
    a1, a2 = _hi_lo(a)
    b1, b2 = _hi_lo(b)
    dot = lambda p, q: lax.dot_general(p, q, _DN[mode], preferred_element_type=F32)
    return dot(a1, b1) + (dot(a1, b2) + dot(a2, b1))


@functools.partial(jax.custom_vjp, nondiff_argnums=(2,))
def _dot3(a, b, mode="nn"):
    return _dot3_raw(a, b, mode)


def _dot3_fwd(a, b, mode):
    return _dot3_raw(a, b, mode), (a, b)


def _dot3_bwd(mode, res, ct):
    a, b = res
    if mode == "nn":
        return _dot3_raw(ct, b, "nt"), _dot3_raw(a, ct, "tn")
    if mode == "nt":
        return _dot3_raw(ct, b, "nn"), _dot3_raw(ct, a, "tn")
    return _dot3_raw(b, ct, "nt"), _dot3_raw(a, ct, "nn")


_dot3.defvjp(_dot3_fwd, _dot3_bwd)


@functools.partial(jax.custom_vjp, nondiff_argnums=(2,))
def _gdot(a, b, mode="nn"):
    return _bf16_dot(a, b, mode)


def _gdot_fwd(a, b, mode):
    return _bf16_dot(a, b, mode), (a.astype(BF16), b.astype(BF16))


def _ct_dot(p, q, mode, ct_first):
    ct, r = (p, q) if ct_first else (q, p)
    c1, c2 = _hi_lo(ct)
    dot = lambda c: lax.dot_general(*((c, r) if ct_first else (r, c)), _DN[mode], preferred_element_type=F32)
    return dot(c1) + dot(c2)


def _gdot_bwd(mode, res, ct):
    a, b = res
    if mode == "nn":
        return _ct_dot(ct, b, "nt", True), _ct_dot(a, ct, "tn", False)
    if mode == "nt":
        return _ct_dot(ct, b, "nn", True), _ct_dot(ct, a, "tn", True)
    return _ct_dot(b, ct, "nt", False), _ct_dot(a, ct, "nn", False)


_gdot.defvjp(_gdot_fwd, _gdot_bwd)


def _split_dot(ones, x):
    x1 = x.astype(BF16)
    r1 = x - x1.astype(F32)
    x2 = r1.astype(BF16)
    x3 = (r1 - x2.astype(F32)).astype(BF16)
    m = ones.astype(BF16)
    dot = lambda p: lax.dot_general(m, p, _DN["nn"], preferred_element_type=F32)
    return dot(x1) + dot(x2) + dot(x3)


@jax.custom_vjp
def _tri_cumsum(x, lower, upper):
    return _split_dot(lower, x)


def _tri_cumsum_fwd(x, lower, upper):
    return _split_dot(lower, x), (lower, upper)


def _tri_cumsum_bwd(res, ct):
    lower, upper = res
    return _split_dot(upper, ct), jnp.zeros_like(lower), jnp.zeros_like(upper)


_tri_cumsum.defvjp(_tri_cumsum_fwd, _tri_cumsum_bwd)


def _tri_masks(n):
    r = lax.broadcasted_iota(jnp.int32, (n, n), 0)
    c = lax.broadcasted_iota(jnp.int32, (n, n), 1)
    return r >= c, r > c


def _gdn_chunk(q, k, v, z, ab, a_row, dt_row, norm_w, state, sel_a, sel_b):
    C = q.shape[0]
    incl, strict = _tri_masks(C)
    lower = jnp.where(incl, 1.0, 0.0)
    qn = q * lax.rsqrt(jnp.sum(q * q, axis=-1, keepdims=True) + NORM_EPS) * (GDN_DK ** -0.5)
    kn = k * lax.rsqrt(jnp.sum(k * k, axis=-1, keepdims=True) + NORM_EPS)
    g = jnp.sum(-jnp.exp(a_row) * _softplus(ab + dt_row) * sel_a, axis=-1, keepdims=True)
    beta = jnp.sum(_sigmoid(ab) * sel_b, axis=-1, keepdims=True)
    gb = jnp.broadcast_to(g, (C, C))
    g_col = _tri_cumsum(gb, lower, jnp.where(strict, 0.0, 1.0))
    g_row = g_col.T
    g_last = jnp.sum(gb, axis=0, keepdims=True)
    gamma = jnp.where(incl, jnp.exp(jnp.where(incl, g_col - g_row, 0.0)), 0.0)
    e_col = jnp.exp(g_col)
    kb = kn * beta
    a_mat = jnp.where(strict, _gdot(kb, kn, "nt") * gamma, 0.0)
    x = jnp.concatenate([v * beta, kb * e_col], axis=1)
    pw = -a_mat
    steps = int(math.log2(C))
    for it in range(steps):
        x = x + _dot3(pw, x, "nn")
        if it < steps - 1:
            pw = _dot3(pw, pw, "nn")
    u, w = x[:, :GDN_DV], x[:, GDN_DV:]
    attn = _gdot(qn, kn, "nt") * gamma
    q_dec = qn * e_col
    k_dec = kn * jnp.exp(g_last - g_col)
    v_new = u - _gdot(w, state, "nn")
    o = _gdot(q_dec, state, "nn") + _gdot(attn, v_new, "nn")
    state_new = state * jnp.exp(jnp.broadcast_to(g_last, state.shape)) + _gdot(k_dec, v_new, "tn")
    y = _rms(o, norm_w) * _silu(z)
    return y, state_new


def _head_selectors(h):
    lane = lax.broadcasted_iota(jnp.int32, (1, LANES), 1)
    return jnp.where(lane == h, 1.0, 0.0), jnp.where(lane == h + GDN_HEADS, 1.0, 0.0)


GDN_HPS = 2
GDN_W = GDN_HPS * LANES


def _gdn_in_specs(rev, nc):
    def n_(n):
        return nc - 1 - n if rev else n
    G = GDN_HEADS // GDN_HPS
    blk = lambda off: pl.BlockSpec((CHUNK, GDN_W), lambda n, h: (n_(n), off + h))
    row = pl.BlockSpec((1, LANES), lambda n, h: (0, 0))
    return n_, [blk(0), blk(G), blk(2 * G), blk(ZIN_Z // GDN_W),
                pl.BlockSpec((CHUNK, LANES), lambda n, h: (n_(n), ZIN_AB // LANES)), row, row, row]


def _lanes(ref, j):
    return ref[:, j * LANES:(j + 1) * LANES]


def _hosting(call, hosted, *args):
    res = call(*args)
    return res if hosted is not None else (res, [])


def gdn_fwd(qkv, zin, a_row, dt_row, norm_w, hosted=None):
    S = qkv.shape[0]
    nc = S // CHUNK
    H = GDN_HEADS
    _, in_specs = _gdn_in_specs(False, nc)

    def body(q_ref, k_ref, v_ref, z_ref, ab_ref, a_ref, dt_ref, nw_ref, y_ref, st_ref, state):
        n, g = pl.program_id(0), pl.program_id(1)
        @pl.when((n == 0) & (g == 0))
        def _():
            state[...] = jnp.zeros_like(state)

        res = []
        for j in range(GDN_HPS):
            h = g * GDN_HPS + j
            st = state[h]
            sel_a, sel_b = _head_selectors(h)
            res.append((st,) + _gdn_chunk(_lanes(q_ref, j), _lanes(k_ref, j), _lanes(v_ref, j), _lanes(z_ref, j),
                                          ab_ref[...], a_ref[...], dt_ref[...], nw_ref[...], st, sel_a, sel_b))
        for j, (st, y, st_new) in enumerate(res):
            st_ref[j] = st
            y_ref[:, j * LANES:(j + 1) * LANES] = y.astype(y_ref.dtype)
            state[g * GDN_HPS + j] = st_new

    call = _pcall(body, name="gdn_fwd", grid=(nc, H // GDN_HPS), in_specs=in_specs,
                  out_specs=[pl.BlockSpec((CHUNK, GDN_W), lambda n, h: (n, h)),
                             pl.BlockSpec((GDN_HPS, None, GDN_DK, GDN_DV), lambda n, h: (h, n, 0, 0))],
                  out_shape=[jax.ShapeDtypeStruct((S, H * GDN_DV), BF16),
                             jax.ShapeDtypeStruct((H, nc, GDN_DK, GDN_DV), F32)],
                  scratch_shapes=[pltpu.VMEM((H, GDN_DK, GDN_DV), F32)],
                  dims=("arbitrary", "arbitrary"), hosted=hosted)
    return _hosting(call, hosted, qkv, qkv, qkv, zin, zin, a_row, dt_row, norm_w)


def gdn_bwd(qkv, zin, a_row, dt_row, norm_w, states, dy, dy_col0, hosted=None):
    S = qkv.shape[0]
    nc = S // CHUNK
    H = GDN_HEADS
    n_, in_specs = _gdn_in_specs(True, nc)
    assert dy_col0 % GDN_HPS == 0
    in_specs = in_specs + [pl.BlockSpec((GDN_HPS, None, GDN_DK, GDN_DV), lambda n, h: (h, n_(n), 0, 0)),
                           pl.BlockSpec((CHUNK, GDN_W), lambda n, h: (n_(n), dy_col0 // GDN_HPS + h))]

    def body(q_ref, k_ref, v_ref, z_ref, ab_ref, a_ref, dt_ref, nw_ref, st_ref, dy_ref,
             dq_ref, dk_ref, dv_ref, dz_ref, dab_ref, da_ref, ddt_ref, dnw_ref, dstate):
        n, g = pl.program_id(0), pl.program_id(1)

        @pl.when((n == 0) & (g == 0))
        def _():
            da_ref[...] = jnp.zeros_like(da_ref)
            ddt_ref[...] = jnp.zeros_like(ddt_ref)
            dnw_ref[...] = jnp.zeros_like(dnw_ref)
            dstate[...] = jnp.zeros_like(dstate)

        @pl.when(g == 0)
        def _():
            dab_ref[...] = jnp.zeros_like(dab_ref)

        res = []
        for j in range(GDN_HPS):
            h = g * GDN_HPS + j
            sel_a, sel_b = _head_selectors(h)
            _, vjp = jax.vjp(lambda *a, sa=sel_a, sb=sel_b: _gdn_chunk(*a, sa, sb), _lanes(q_ref, j), _lanes(k_ref, j),
                             _lanes(v_ref, j), _lanes(z_ref, j), ab_ref[...], a_ref[...], dt_ref[...], nw_ref[...],
                             st_ref[j])
            res.append(vjp((_lanes(dy_ref, j).astype(F32), dstate[h])))
        for j, (dq, dk, dv, dz, dab, da, ddt, dnw, dst) in enumerate(res):
            cols = slice(j * LANES, (j + 1) * LANES)
            dq_ref[:, cols] = dq
            dk_ref[:, cols] = dk
            dv_ref[:, cols] = dv
            dz_ref[:, cols] = dz.astype(dz_ref.dtype)
            dstate[g * GDN_HPS + j] = dst
        dab_ref[...] += sum(r[4] for r in res)
        da_ref[...] += sum(r[5] for r in res)
        ddt_ref[...] += sum(r[6] for r in res)
        dnw_ref[...] += sum(r[7] for r in res)

    blk = pl.BlockSpec((CHUNK, GDN_W), lambda n, h: (n_(n), h))
    row = pl.BlockSpec((1, LANES), lambda n, h: (0, 0))
    wide = jax.ShapeDtypeStruct((S, H * LANES), F32)
    call = _pcall(body, name="gdn_bwd", grid=(nc, H // GDN_HPS), in_specs=in_specs,
                  out_specs=[blk, blk, blk, blk, pl.BlockSpec((CHUNK, LANES), lambda n, h: (n_(n), 0)), row, row, row],
                  out_shape=[wide, wide, wide, jax.ShapeDtypeStruct((S, H * LANES), BF16),
                             jax.ShapeDtypeStruct((S, LANES), F32)] + [jax.ShapeDtypeStruct((1, LANES), F32)] * 3,
                  scratch_shapes=[pltpu.VMEM((H, GDN_DK, GDN_DV), F32)],
                  dims=("arbitrary", "arbitrary"), hosted=hosted)
    return _hosting(call, hosted, qkv, qkv, qkv, zin, zin, a_row, dt_row, norm_w, states, dy)


def _rope_full(x, cos, sin):
    x1, x2 = x[:, :RET_DK // 2], x[:, RET_DK // 2:]
    return jnp.concatenate([x1 * cos - x2 * sin, x2 * cos + x1 * sin], axis=1)


def _ret_chunk(q, k, v, cos, sin, lg, state):
    C = q.shape[0]
    incl, _ = _tri_masks(C)
    qr = _rope_full(q, cos, sin)
    kr = _rope_full(k, cos, sin) * (RET_DK ** -0.5)
    r = lax.broadcasted_iota(jnp.int32, (C, C), 0)
    c = lax.broadcasted_iota(jnp.int32, (C, C), 1)
    dist = jnp.where(incl, (r - c).astype(F32), 0.0)
    decay = jnp.where(incl, jnp.exp(dist * lg), 0.0)
    pos = lax.broadcasted_iota(jnp.int32, (C, 1), 0).astype(F32)
    lg1 = lg[:, :1]
    xi = jnp.exp((pos + 1.0) * lg1)
    zeta = jnp.exp((C - 1.0 - pos) * lg1)
    inner = _bdot(_bdot(qr, kr, _DN["nt"]) * decay, v)
    cross = _bdot(qr * xi, state)
    state_new = state * jnp.exp(C * lg1) + _bdot(kr * zeta, v, _DN["tn"])
    o = inner + cross
    mu = jnp.mean(o, axis=-1, keepdims=True)
    var = jnp.mean(jnp.square(o - mu), axis=-1, keepdims=True)
    return (o - mu) * lax.rsqrt(var + NORM_EPS), state_new


def _ret_log_gamma():
    lg = np.log1p(-np.power(2.0, -5.0 - np.arange(RET_HEADS, dtype=np.float64))).astype(np.float32)
    return jnp.asarray(np.broadcast_to(lg[:, None, None], (RET_HEADS, 1, LANES)).copy())


def _ret_in_specs(rev, nc):
    def n_(n):
        return nc - 1 - n if rev else n
    H = RET_HEADS
    return n_, [pl.BlockSpec((CHUNK, RET_DK), lambda n, h: (n_(n), h)),
                pl.BlockSpec((CHUNK, RET_DK), lambda n, h: (n_(n), H + h)),
                pl.BlockSpec((CHUNK, RET_DV), lambda n, h: (n_(n), 2 * H * RET_DK // RET_DV + h)),
                pl.BlockSpec((CHUNK, LANES), lambda n, h: (n_(n), 0)),
                pl.BlockSpec((CHUNK, LANES), lambda n, h: (n_(n), 0)),
                pl.BlockSpec((None, 1, LANES), lambda n, h: (h, 0, 0))]


def ret_fwd(zz, cos, sin, hosted=None):
    S = zz.shape[0]
    nc = S // CHUNK
    H = RET_HEADS
    _, in_specs = _ret_in_specs(False, nc)

    def body(q_ref, k_ref, v_ref, cos_ref, sin_ref, lg_ref, o_ref, st_ref, state):
        n, h = pl.program_id(0), pl.program_id(1)

        @pl.when(n == 0)
        def _():
            state[h] = jnp.zeros((RET_DK, RET_DV), F32)

        st = state[h]
        st_ref[...] = st
        o, st_new = _ret_chunk(q_ref[...], k_ref[...], v_ref[...], cos_ref[...], sin_ref[...], lg_ref[...], st)
        o_ref[...] = o
        state[h] = st_new

    call = _pcall(body, name="ret_fwd", grid=(nc, H), in_specs=in_specs,
                  out_specs=[pl.BlockSpec((CHUNK, RET_DV), lambda n, h: (n, h)),
                             pl.BlockSpec((None, None, RET_DK, RET_DV), lambda n, h: (h, n, 0, 0))],
                  out_shape=[jax.ShapeDtypeStruct((S, H * RET_DV), F32),
                             jax.ShapeDtypeStruct((H, nc, RET_DK, RET_DV), F32)],
                  scratch_shapes=[pltpu.VMEM((H, RET_DK, RET_DV), F32)],
                  dims=("arbitrary", "arbitrary"), hosted=hosted)
    return _hosting(call, hosted, zz, zz, zz, cos, sin, _ret_log_gamma())


def ret_bwd(zz, cos, sin, states, do):
    S = zz.shape[0]
    nc = S // CHUNK
    H = RET_HEADS
    n_, in_specs = _ret_in_specs(True, nc)
    in_specs = in_specs + [pl.BlockSpec((None, None, RET_DK, RET_DV), lambda n, h: (h, n_(n), 0, 0)),
                           pl.BlockSpec((CHUNK, RET_DV), lambda n, h: (n_(n), h))]

    def body(q_ref, k_ref, v_ref, cos_ref, sin_ref, lg_ref, st_ref, do_ref, dq_ref, dk_ref, dv_ref, dstate):
        n, h = pl.program_id(0), pl.program_id(1)

        @pl.when(n == 0)
        def _():
            dstate[h] = jnp.zeros((RET_DK, RET_DV), F32)

        cos, sin, lg = cos_ref[...], sin_ref[...], lg_ref[...]
        _, vjp = jax.vjp(lambda q, k, v, st: _ret_chunk(q, k, v, cos, sin, lg, st),
                         q_ref[...], k_ref[...], v_ref[...], st_ref[...])
        dq, dk, dv, dst = vjp((do_ref[...], dstate[h]))
        dq_ref[...] = dq.astype(dq_ref.dtype)
        dk_ref[...] = dk.astype(dk_ref.dtype)
        dv_ref[...] = dv.astype(dv_ref.dtype)
        dstate[h] = dst

    return _pcall(body, name="ret_bwd", grid=(nc, H), in_specs=in_specs,
                  out_specs=[pl.BlockSpec((CHUNK, RET_DK), lambda n, h: (n_(n), h)),
                             pl.BlockSpec((CHUNK, RET_DK), lambda n, h: (n_(n), h)),
                             pl.BlockSpec((CHUNK, RET_DV), lambda n, h: (n_(n), h))],
                  out_shape=[jax.ShapeDtypeStruct((S, H * RET_DK), BF16), jax.ShapeDtypeStruct((S, H * RET_DK), BF16),
                             jax.ShapeDtypeStruct((S, H * RET_DV), BF16)],
                  scratch_shapes=[pltpu.VMEM((H, RET_DK, RET_DV), F32)],
                  dims=("arbitrary", "arbitrary"))(zz, zz, zz, cos, sin, _ret_log_gamma(), states, do)


MLA_SCALE = (MLA_NOPE + MLA_ROPE) ** -0.5
NEG = -1e30


def _mla_scores(q, kn, kpe, diagonal):
    s = (lax.dot_general(q[:, :LANES], kn, _DN["nt"], preferred_element_type=F32)
         + lax.dot_general(q[:, LANES:], kpe, _DN["nt"], preferred_element_type=F32)) * MLA_SCALE
    if diagonal:
        row = lax.broadcasted_iota(jnp.int32, s.shape, 0)
        col = lax.broadcasted_iota(jnp.int32, s.shape, 1)
        s = jnp.where(col <= row, s, NEG)
    return s


def _on_and_below_diagonal(i, j, step):
    @pl.when(j < i)
    def _():
        step(False)

    @pl.when(j == i)
    def _():
        step(True)


FLASH_T = 1024


def flash_fwd(qr, kv, kpe, *, t=FLASH_T, hosted=None):
    S = qr.shape[0]
    t = min(t, S)
    nb = S // t
    H = MLA_HEADS

    def body(q_ref, kn_ref, v_ref, kpe_ref, o_ref, lse_ref, m_s, l_s, acc):
        i, j = pl.program_id(1), pl.program_id(2)

        @pl.when(j == 0)
        def _():
            m_s[...] = jnp.full_like(m_s, NEG)
            l_s[...] = jnp.zeros_like(l_s)
            acc[...] = jnp.zeros_like(acc)

        def step(diagonal):
            s = _mla_scores(q_ref[...], kn_ref[...], kpe_ref[...], diagonal)
            m_new = jnp.maximum(m_s[...], jnp.max(s, axis=-1, keepdims=True))
            p = jnp.exp(s - m_new)
            alpha = jnp.exp(m_s[...] - m_new)
            l_s[...] = alpha * l_s[...] + jnp.sum(p, axis=-1, keepdims=True)
            acc[...] = alpha * acc[...] + _bdot(p, v_ref[...])
            m_s[...] = m_new

        _on_and_below_diagonal(i, j, step)

        @pl.when(j == nb - 1)
        def _():
            o_ref[...] = (acc[...] / l_s[...]).astype(o_ref.dtype)
            lse_ref[...] = m_s[...] + jnp.log(l_s[...])

    kmap = lambda off: (lambda h, i, j: (jnp.minimum(j, i), off + h))
    call = _pcall(body, name="mla_flash_fwd", grid=(H, nb, nb),
                  in_specs=[pl.BlockSpec((t, 2 * LANES), lambda h, i, j: (i, h)),
                            pl.BlockSpec((t, LANES), kmap(0)), pl.BlockSpec((t, LANES), kmap(H)),
                            pl.BlockSpec((t, LANES), lambda h, i, j: (jnp.minimum(j, i), 0))],
                  out_specs=[pl.BlockSpec((t, LANES), lambda h, i, j: (i, h)),
                             pl.BlockSpec((None, t, 1), lambda h, i, j: (h, i, 0))],
                  out_shape=[jax.ShapeDtypeStruct((S, H * MLA_V), BF16), jax.ShapeDtypeStruct((H, S, 1), F32)],
                  scratch_shapes=[pltpu.VMEM((t, 1), F32), pltpu.VMEM((t, 1), F32), pltpu.VMEM((t, MLA_V), F32)],
                  dims=("parallel", "parallel", "arbitrary"), hosted=hosted)
    return _hosting(call, hosted, qr, kv, kv, kpe)


def _mla_p_ds(q, kn, v, kpe, do, o, lse, diagonal):
    p = jnp.exp(_mla_scores(q, kn, kpe, diagonal) - lse)
    dof = do.astype(F32)
    delta = jnp.sum(dof * o.astype(F32), axis=-1, keepdims=True)
    dp = lax.dot_general(do.astype(BF16), v, _DN["nt"], preferred_element_type=F32)
    ds = p * (dp - delta) * MLA_SCALE
    return p, ds


def flash_bwd_dq(qr, kv, kpe, o, lse, dy, dy_col0, *, t=FLASH_T, hosted=None):
    S = qr.shape[0]
    t = min(t, S)
    nb = S // t
    H = MLA_HEADS

    def body(q_ref, kn_ref, v_ref, kpe_ref, o_ref, lse_ref, do_ref, dq_ref, acc):
        i, j = pl.program_id(1), pl.program_id(2)

        @pl.when(j == 0)
        def _():
            acc[...] = jnp.zeros_like(acc)

        def step(diagonal):
            _, ds = _mla_p_ds(q_ref[...], kn_ref[...], v_ref[...], kpe_ref[...], do_ref[...], o_ref[...],
                              lse_ref[...], diagonal)
            acc[...] += jnp.concatenate([_bdot(ds, kn_ref[...]), _bdot(ds, kpe_ref[...])], axis=1)

        _on_and_below_diagonal(i, j, step)

        @pl.when(j == nb - 1)
        def _():
            dq_ref[...] = acc[...]

    kmap = lambda off: (lambda h, i, j: (jnp.minimum(j, i), off + h))
    call = _pcall(body, name="mla_flash_dq", grid=(H, nb, nb),
                  in_specs=[pl.BlockSpec((t, 2 * LANES), lambda h, i, j: (i, h)),
                            pl.BlockSpec((t, LANES), kmap(0)), pl.BlockSpec((t, LANES), kmap(H)),
                            pl.BlockSpec((t, LANES), lambda h, i, j: (jnp.minimum(j, i), 0)),
                            pl.BlockSpec((t, LANES), lambda h, i, j: (i, h)),
                            pl.BlockSpec((None, t, 1), lambda h, i, j: (h, i, 0)),
                            pl.BlockSpec((t, LANES), lambda h, i, j: (i, dy_col0 + h))],
                  out_specs=pl.BlockSpec((t, 2 * LANES), lambda h, i, j: (i, h)),
                  out_shape=jax.ShapeDtypeStruct((S, H * 2 * LANES), F32),
                  scratch_shapes=[pltpu.VMEM((t, 2 * LANES), F32)],
                  dims=("parallel", "parallel", "arbitrary"), hosted=hosted)
    return _hosting(call, hosted, qr, kv, kv, kpe, o, lse, dy)


def flash_bwd_dkv(qr, kv, kpe, o, lse, dy, dy_col0, *, t=FLASH_T, hosted=None):
    S = qr.shape[0]
    t = min(t, S)
    nb = S // t
    H = MLA_HEADS

    def body(q_ref, kn_ref, v_ref, kpe_ref, o_ref, lse_ref, do_ref, dkn_ref, dv_ref, dkpe_ref, akn, av):
        j, h, i = pl.program_id(0), pl.program_id(1), pl.program_id(2)

        @pl.when(i == 0)
        def _():
            akn[...] = jnp.zeros_like(akn)
            av[...] = jnp.zeros_like(av)

        @pl.when((i == 0) & (h == 0))
        def _():
            dkpe_ref[...] = jnp.zeros_like(dkpe_ref)

        def step(diagonal):
            q = q_ref[...]
            p, ds = _mla_p_ds(q, kn_ref[...], v_ref[...], kpe_ref[...], do_ref[...], o_ref[...], lse_ref[...],
                              diagonal)
            av[...] += _bdot(p, do_ref[...], _DN["tn"])
            akn[...] += _bdot(ds, q[:, :LANES], _DN["tn"])
            dkpe_ref[...] += _bdot(ds, q[:, LANES:], _DN["tn"])

        _on_and_below_diagonal(i, j, step)

        @pl.when(i == nb - 1)
        def _():
            dkn_ref[...] = akn[...].astype(dkn_ref.dtype)
            dv_ref[...] = av[...].astype(dv_ref.dtype)

    qmap = lambda off: (lambda j, h, i: (jnp.maximum(i, j), off + h))
    call = _pcall(
        body, name="mla_flash_dkv", grid=(nb, H, nb),
        in_specs=[pl.BlockSpec((t, 2 * LANES), qmap(0)),
                  pl.BlockSpec((t, LANES), lambda j, h, i: (j, h)), pl.BlockSpec((t, LANES), lambda j, h, i: (j, H + h)),
                  pl.BlockSpec((t, LANES), lambda j, h, i: (j, 0)),
                  pl.BlockSpec((t, LANES), qmap(0)),
                  pl.BlockSpec((None, t, 1), lambda j, h, i: (h, jnp.maximum(i, j), 0)),
                  pl.BlockSpec((t, LANES), qmap(dy_col0))],
        out_specs=[pl.BlockSpec((t, LANES), lambda j, h, i: (j, h)), pl.BlockSpec((t, LANES), lambda j, h, i: (j, h)),
                   pl.BlockSpec((t, LANES), lambda j, h, i: (j, 0))],
        out_shape=[jax.ShapeDtypeStruct((S, H * LANES), BF16), jax.ShapeDtypeStruct((S, H * LANES), BF16),
                   jax.ShapeDtypeStruct((S, LANES), F32)],
        scratch_shapes=[pltpu.VMEM((t, LANES), F32), pltpu.VMEM((t, LANES), F32)],
        dims=("arbitrary", "arbitrary", "arbitrary"), hosted=hosted)
    (dkn, dv, dkpe), extra = _hosting(call, hosted, qr, kv, kv, kpe, o, lse, dy)
    return (jnp.concatenate([dkn, dv], axis=1), dkpe), extra


def loss_head(h, target, g, *, tr=256):
    S, D = h.shape
    tr = min(tr, S)

    def body(h_ref, t_ref, g_ref, loss_ref, dh_ref, dg_ref):
        tgt = t_ref[...]

        def f(hh, gg):
            err = jnp.square(_rms(hh, gg) - tgt)
            per_row = jnp.sum(err, axis=-1, keepdims=True) * (0.5 / D)
            return jnp.sum(per_row, axis=0, keepdims=True)

        val, vjp = jax.vjp(f, h_ref[...], g_ref[...])
        dh, dg = vjp(jnp.ones((1, 1), F32))
        dh_ref[...] = dh

        @pl.when(pl.program_id(0) == 0)
        def _():
            loss_ref[...] = jnp.zeros_like(loss_ref)
            dg_ref[...] = jnp.zeros_like(dg_ref)

        loss_ref[...] += jnp.broadcast_to(val, loss_ref.shape)
        dg_ref[...] += dg

    return _pcall(body, name="loss_head", grid=(S // tr,),
                  in_specs=[_row_spec(tr, D, 0), _row_spec(tr, D, 0), _full_spec(g)],
                  out_specs=[pl.BlockSpec((1, LANES), lambda i: (0, 0)), _row_spec(tr, D, 0), _full_spec(g)],
                  out_shape=[jax.ShapeDtypeStruct((1, LANES), F32), jax.ShapeDtypeStruct((S, D), F32),
                             jax.ShapeDtypeStruct(g.shape, F32)],
                  dims=("arbitrary",))(h, target, g)


def _rope_tables(positions, dim):
    inv_freq = ROPE_THETA ** (-jnp.arange(0, dim, 2, dtype=F32) / dim)
    ang = positions.astype(F32)[:, None] * inv_freq
    return jnp.cos(ang), jnp.sin(ang)


def _pad_cols(w, n):
    return jnp.pad(w, ((0, 0), (0, n - w.shape[1])))


def _prep_w_in0(w):
    return jnp.concatenate([w[:, :4096], w[:, 4112:5136], _pad_cols(w[:, 5136:5200], LANES),
                            _pad_cols(w[:, 4096:4112], LANES)], axis=1)


def _unprep_w_in0(g):
    return jnp.concatenate([g[:, :4096], g[:, ZIN_AB:ZIN_AB + 16], g[:, ZIN_CQ:ZIN_KR], g[:, ZIN_KR:ZIN_KR + MLA_ROPE]],
                           axis=1)


def _prep_w_uq(w):
    w = w.reshape(MLA_Q_RANK, MLA_HEADS, MLA_NOPE + MLA_ROPE)
    w = jnp.pad(w, ((0, 0), (0, 0), (0, 2 * LANES - MLA_NOPE - MLA_ROPE)))
    return w.reshape(MLA_Q_RANK, MLA_HEADS * 2 * LANES)


def _unprep_w_uq(g):
    g = g.reshape(MLA_Q_RANK, MLA_HEADS, 2 * LANES)[:, :, :MLA_NOPE + MLA_ROPE]
    return g.reshape(MLA_Q_RANK, MLA_HEADS * (MLA_NOPE + MLA_ROPE))


def _prep_w_ukv(w):
    w = w.reshape(MLA_KV_RANK, MLA_HEADS, 2, LANES)
    return jnp.transpose(w, (0, 2, 1, 3)).reshape(MLA_KV_RANK, 2 * MLA_HEADS * LANES)


def _unprep_w_ukv(g):
    g = g.reshape(MLA_KV_RANK, 2, MLA_HEADS, LANES)
    return jnp.transpose(g, (0, 2, 1, 3)).reshape(MLA_KV_RANK, 2 * MLA_HEADS * LANES)


def _row(v, n=None):
    v = v.reshape(1, -1).astype(F32)
    return v if n is None else _pad_cols(v, n)


def _ffn_fwd(h, norm_g, w_up, conv_w, conv_b, w_down, tag, hosted=None):
    (hn,) = rowwise(_fn_rms, [(h, D_MODEL, 0)], [norm_g], [], [(D_MODEL, BF16)], name=f"{tag}_ffn_norm")
    u = matmul(hn, w_up, "nn", F32, b_shards=N_CHIPS, **TILES["wide_nn"], name=f"{tag}_ffn_up", hosted=hosted)
    u, got = u if hosted is not None else (u, [])
    f = ffn_conv_fwd(u, conv_w, conv_b)
    h_out = matmul(f, w_down, "nn", F32, add=h, tm=512, tn=1024, tk=8192, name=f"{tag}_ffn_down")
    return h_out, (hn, u, f), got


def _ffn_bwd(dh, dh16, h, norm_g, w_up, conv_w, conv_b, w_down, saved, tag):
    hn, u, f = saved
    df = matmul(dh16, w_down, "nt", BF16, tm=512, tn=2816, name=f"{tag}_ffn_down_dx")
    g_down = matmul(f, dh16, "tn", BF16, **TILES["dw"], name=f"{tag}_ffn_down_dw")
    dc, g_conv_w, g_conv_b = ffn_conv_bwd(u, conv_w, conv_b, df)
    du = conv_transpose(dc, conv_w, BF16, name=f"{tag}_ffn_conv_dx")
    g_up = matmul(hn, du, "tn", BF16, out_shards=N_CHIPS, tm=1024, tn=1408, tk=4096, name=f"{tag}_ffn_up_dw")
    dhn = matmul(du, w_up, "nt", F32, b_shards=N_CHIPS, tm=512, tn=2048, tk=2816, name=f"{tag}_ffn_up_dx")
    ((dh_in, dh_in16),), (g_norm,) = rowwise_bwd(_fn_rms, [(h, D_MODEL, 0)], [norm_g], [], [(dhn, D_MODEL, 0)],
                                                 [(F32, BF16)], adds=[(dh, D_MODEL, 0)], name=f"{tag}_ffn_norm_bwd")
    return dh_in, dh_in16, dict(ffn_norm=g_norm, ffn_w_up=g_up, ffn_conv_w=g_conv_w, ffn_conv_b=g_conv_b,
                                ffn_w_down=g_down)


TILES = {"wide_nn": dict(tm=512, tn=3072, tk=2048),
         "square": dict(tm=512, tn=2048, tk=2048),
         "dw": dict(tm=512, tn=2048, tk=4096)}


def _ple_fwd(h, p_i, w_proj, gate_g, w_gate, tag):
    (hg,) = rowwise(_fn_rms, [(h, D_MODEL, 0)], [gate_g], [], [(D_MODEL, BF16)], name=f"{tag}_ple_norm")
    gl = matmul(hg, w_gate, "nn", F32, **TILES["square"], name=f"{tag}_ple_gate")
    pp = matmul(p_i, w_proj, "nn", F32, b_shards=N_CHIPS, name=f"{tag}_ple_proj")
    (h_out,) = rowwise(_fn_ple, [(h, D_MODEL, 0), (pp, D_MODEL, 0), (gl, D_MODEL, 0)], [], [], [(D_MODEL, F32)],
                       name=f"{tag}_ple_add")
    return h_out, (hg, gl, pp)


def _ple_bwd(dh, h, p_i, w_proj, gate_g, w_gate, saved, tag):
    hg, gl, pp = saved
    (dpp, dgl), _ = rowwise_bwd(_fn_ple_terms, [(pp, D_MODEL, 0), (gl, D_MODEL, 0)], [], [], [(dh, D_MODEL, 0)],
                                [BF16, BF16], name=f"{tag}_ple_add_bwd")
    g_proj = matmul(p_i, dpp, "tn", BF16, out_shards=N_CHIPS, name=f"{tag}_ple_proj_dw")
    g_gate = matmul(hg, dgl, "tn", BF16, **TILES["dw"], name=f"{tag}_ple_gate_dw")
    dhg = matmul(dgl, w_gate, "nt", F32, **TILES["square"], name=f"{tag}_ple_gate_dx")
    ((dh_in, dh_in16),), (g_norm,) = rowwise_bwd(_fn_rms, [(h, D_MODEL, 0)], [gate_g], [], [(dhg, D_MODEL, 0)],
                                                 [(F32, BF16)], adds=[(dh, D_MODEL, 0)], name=f"{tag}_ple_norm_bwd")
    return dh_in, dh_in16, dict(ple_proj=g_proj, ple_gate_norm=g_norm, ple_gate=g_gate)


def local_step(x, p, positions, target, slots, W, ex=None):
    S = x.shape[0]
    G = {}
    p0, p1 = p[0].astype(BF16), p[1].astype(BF16)
    W = dict(W)

    def use(names, bufs):
        for k, b in zip(names, bufs):
            r, c = b.shape[1:]
            W[k] = b.reshape(N_CHIPS * r, c) if k in ROW_SHARDED else (b if k in KEPT_SHARDED else _cols_to_full(b))

    def by_chip(names):
        out = []
        for k in names:
            r, c = slots[k].shape[1:]
            out.append(G[k].reshape(N_CHIPS, r, c) if k in ROW_SHARDED
                       else (G[k] if k in KEPT_SHARDED else _full_to_cols(G[k])))
        return out

    first = [slots[k] for k in GATHER_FIRST]
    use(GATHER_FIRST, ex.gather(first) if ex else first)

    cm, sm = _rope_tables(positions, MLA_ROPE)
    zeros = jnp.zeros((S, LANES - MLA_ROPE), F32)
    cosp = jnp.concatenate([cm, cm, zeros], axis=1)
    sinp = jnp.concatenate([sm, sm, zeros], axis=1)
    cr, sr = _rope_tables(positions, RET_DK)

    w_in0 = _prep_w_in0(W["l0_w_in"])
    w_uq = _prep_w_uq(W["l0_mla_w_uq"])
    w_ukv = _prep_w_ukv(W["l0_mla_w_ukv"])
    a_row = _row(W["l0_gdn_A_log"], LANES)
    dt_row = _row(W["l0_gdn_dt_bias"], LANES)
    gdn_nw = _row(W["l0_gdn_norm"])
    n = {k: _row(W[k]) for k in ("l0_attn_norm", "l0_mla_q_norm", "l0_mla_kv_norm", "l0_ffn_norm",
                                 "l0_ple_gate_norm", "l1_attn_norm", "l1_ret_norm", "l1_ffn_norm",
                                 "l1_ple_gate_norm", "final_norm", "l0_ffn_conv_b", "l1_ffn_conv_b")}

    (hn0,) = rowwise(_fn_rms, [(x, D_MODEL, 0)], [n["l0_attn_norm"]], [], [(D_MODEL, BF16)], name="l0_attn_norm")
    zin = matmul(hn0, w_in0, "nn", F32, tm=512, tn=1792, name="l0_w_in")
    qkv = gdn_conv_fwd(zin, W["l0_gdn_conv"])
    layer0 = [slots[k] for k in GATHER_L0]
    (y_a, gdn_states), got = gdn_fwd(qkv, zin, a_row, dt_row, gdn_nw, hosted=hosted_gather(layer0) if ex else None)
    use(GATHER_L0, got if ex else layer0)
    mla_rows = [(zin, MLA_Q_RANK, ZIN_CQ // MLA_Q_RANK), (zin, MLA_KV_RANK, ZIN_CKV // MLA_KV_RANK),
                (zin, LANES, ZIN_KR // LANES)]
    mla_nd = [(cosp, LANES, 0), (sinp, LANES, 0)]
    cqn, ckvn, kpe = rowwise(_fn_mla_pre, mla_rows, [n["l0_mla_q_norm"], n["l0_mla_kv_norm"]], mla_nd,
                             [(MLA_Q_RANK, BF16), (MLA_KV_RANK, BF16), (LANES, BF16)], name="mla_pre")
    q_lin = matmul(cqn, w_uq, "nn", F32, name="mla_w_uq")
    kv = matmul(ckvn, w_ukv, "nn", BF16, name="mla_w_ukv")
    (qr,) = rowwise(_fn_rope_q, [(q_lin, 2048, 0)], [], mla_nd, [(2048, BF16)],
                    name="mla_rope_q")
    layer1 = [slots[k] for k in GATHER_L1]
    (y_b, lse), got = flash_fwd(qr, kv, kpe, hosted=hosted_gather(layer1) if ex else None)
    use(GATHER_L1, got if ex else layer1)
    y_ab = jnp.concatenate([y_a, y_b], axis=1)
    h1 = matmul(y_ab, W["l0_w_out"], "nn", F32, add=x, **TILES["square"], name="l0_w_out")
    ffn_late = [slots[k] for k in GATHER_FFN]
    h2, ffn0, got = _ffn_fwd(h1, n["l0_ffn_norm"], W["l0_ffn_w_up"], W["l0_ffn_conv_w"], n["l0_ffn_conv_b"],
                             W["l0_ffn_w_down"], "l0", hosted=hosted_gather(ffn_late) if ex else None)
    use(GATHER_FFN, got if ex else ffn_late)
    h3, ple0 = _ple_fwd(h2, p0, W["l0_ple_proj"], n["l0_ple_gate_norm"], W["l0_ple_gate"], "l0")

    (hn1,) = rowwise(_fn_rms, [(h3, D_MODEL, 0)], [n["l1_attn_norm"]], [], [(D_MODEL, BF16)], name="l1_attn_norm")
    zz = matmul(hn1, W["l1_w_in"], "nn", F32, b_shards=N_CHIPS, **TILES["wide_nn"], name="l1_w_in")
    late = [slots[k] for k in GATHER_RET]
    (o_ret, ret_states), got = ret_fwd(zz, cr, sr, hosted=hosted_gather(late) if ex else None)
    use(GATHER_RET, got if ex else late)
    gate_rows = [(zz, 4096, 2), (o_ret, 4096, 0)]
    (yg,) = rowwise(_fn_ret_gate, gate_rows, [n["l1_ret_norm"]], [], [(4096, BF16)], name="ret_gate")
    h4 = matmul(yg, W["l1_w_out"], "nn", F32, add=h3, tm=512, tn=2048, tk=4096, name="l1_w_out")
    h5, ffn1, _ = _ffn_fwd(h4, n["l1_ffn_norm"], W["l1_ffn_w_up"], W["l1_ffn_conv_w"], n["l1_ffn_conv_b"],
                           W["l1_ffn_w_down"], "l1")
    h6, ple1 = _ple_fwd(h5, p1, W["l1_ple_proj"], n["l1_ple_gate_norm"], W["l1_ple_gate"], "l1")

    loss_vec, dh, G["final_norm"] = loss_head(h6, target, n["final_norm"])

    dh, dh16, g = _ple_bwd(dh, h5, p1, W["l1_ple_proj"], n["l1_ple_gate_norm"], W["l1_ple_gate"], ple1, "l1")
    G.update({"l1_" + k: v for k, v in g.items()})
    dh, dh16, g = _ffn_bwd(dh, dh16, h4, n["l1_ffn_norm"], W["l1_ffn_w_up"], W["l1_ffn_conv_w"], n["l1_ffn_conv_b"],
                           W["l1_ffn_w_down"], ffn1, "l1")
    G.update({"l1_" + k: v for k, v in g.items()})

    dyg = matmul(dh16, W["l1_w_out"], "nt", F32, tm=512, tn=4096, name="l1_w_out_dx")
    G["l1_w_out"] = matmul(yg, dh16, "tn", BF16, **TILES["dw"], name="l1_w_out_dw")
    (dg, do_ret), (G["l1_ret_norm"],) = rowwise_bwd(_fn_ret_gate, gate_rows, [n["l1_ret_norm"]], [],
                                                   [(dyg, 4096, 0)], [BF16, F32], name="ret_gate_bwd")
    dq, dk, dv = ret_bwd(zz, cr, sr, ret_states, do_ret)
    dzz = jnp.concatenate([dq, dk, dv, dg], axis=1)
    G["l1_w_in"] = matmul(hn1, dzz, "tn", BF16, out_shards=N_CHIPS, tm=1024, tn=1536, tk=4096, name="l1_w_in_dw")
    dhn = matmul(dzz, W["l1_w_in"], "nt", F32, b_shards=N_CHIPS, tm=1024, tn=1024, tk=3072, name="l1_w_in_dx")
    (dh,), (G["l1_attn_norm"],) = rowwise_bwd(_fn_rms, [(h3, D_MODEL, 0)], [n["l1_attn_norm"]], [],
                                             [(dhn, D_MODEL, 0)], [F32], adds=[(dh, D_MODEL, 0)],
                                             name="l1_attn_norm_bwd")

    dh, dh16, g = _ple_bwd(dh, h2, p0, W["l0_ple_proj"], n["l0_ple_gate_norm"], W["l0_ple_gate"], ple0, "l0")
    G.update({"l0_" + k: v for k, v in g.items()})
    dh, dh16, g = _ffn_bwd(dh, dh16, h1, n["l0_ffn_norm"], W["l0_ffn_w_up"], W["l0_ffn_conv_w"], n["l0_ffn_conv_b"],
                           W["l0_ffn_w_down"], ffn0, "l0")
    G.update({"l0_" + k: v for k, v in g.items()})

    dy_ab = matmul(dh16, W["l0_w_out"], "nt", F32, **TILES["square"], name="l0_w_out_dx")
    G["l0_w_out"] = matmul(y_ab, dh16, "tn", BF16, **TILES["dw"], name="l0_w_out_dw")
    sums, landed = {}, {}
    if ex:
        sums.update(zip(REDUCE_L1, ex.pair_sums(REDUCE_L1, by_chip(REDUCE_L1))))
        late = REDUCE_DQ + REDUCE_L0
        sums.update(zip(late, ex.pair_sums(late, by_chip(late))))
    (dq, dk, dv, dz, dab, g_a, g_dt, G["l0_gdn_norm"]), got = gdn_bwd(
        qkv, zin, a_row, dt_row, gdn_nw, gdn_states, dy_ab, 0,
        hosted=hosted_scatter([sums[k] for k in REDUCE_L1]) if ex else None)
    landed.update(zip(REDUCE_L1, got))
    G["l0_gdn_A_log"], G["l0_gdn_dt_bias"] = g_a[:, :GDN_HEADS], g_dt[:, :GDN_HEADS]
    dpre, G["l0_gdn_conv"] = gdn_conv_bwd(zin, W["l0_gdn_conv"], jnp.concatenate([dq, dk, dv], axis=1))
    dqkv = conv_transpose(dpre, W["l0_gdn_conv"], BF16, name="gdn_conv_dx")
    dqr, got = flash_bwd_dq(qr, kv, kpe, y_b, lse, dy_ab, MLA_HEADS,
                            hosted=hosted_scatter([sums[k] for k in REDUCE_DQ]) if ex else None)
    landed.update(zip(REDUCE_DQ, got))
    (dkv, dkpe), got = flash_bwd_dkv(qr, kv, kpe, y_b, lse, dy_ab, MLA_HEADS,
                                     hosted=hosted_scatter([sums[k] for k in REDUCE_L0]) if ex else None)
    landed.update(zip(REDUCE_L0, got))
    (dq_lin,), _ = rowwise_bwd(_fn_rope_q, [(q_lin, 2048, 0)], [], mla_nd, [(dqr, 2048, 0)], [BF16],
                               name="mla_rope_q_bwd")
    G["l0_mla_w_uq"] = _unprep_w_uq(matmul(cqn, dq_lin, "tn", BF16, name="mla_w_uq_dw"))
    dcqn = matmul(dq_lin, w_uq, "nt", F32, name="mla_w_uq_dx")
    G["l0_mla_w_ukv"] = _unprep_w_ukv(matmul(ckvn, dkv, "tn", BF16, name="mla_w_ukv_dw"))
    dckvn = matmul(dkv, w_ukv, "nt", F32, name="mla_w_ukv_dx")
    (dcq, dckv, dkr), (G["l0_mla_q_norm"], G["l0_mla_kv_norm"]) = rowwise_bwd(
        _fn_mla_pre, mla_rows, [n["l0_mla_q_norm"], n["l0_mla_kv_norm"]], mla_nd,
        [(dcqn, MLA_Q_RANK, 0), (dckvn, MLA_KV_RANK, 0), (dkpe, LANES, 0)], [BF16, BF16, BF16], name="mla_pre_bwd")
    dzin = jnp.concatenate([dqkv, dz, dcq, dckv, dkr, dab.astype(BF16)], axis=1)
    G["l0_w_in"] = _unprep_w_in0(matmul(hn0, dzin, "tn", BF16, tm=512, tn=1792, tk=4096, name="l0_w_in_dw"))
    dhn = matmul(dzin, w_in0, "nt", F32, tm=512, tn=2048, tk=5376, name="l0_w_in_dx")
    (grad_x,), (G["l0_attn_norm"],) = rowwise_bwd(_fn_rms, [(x, D_MODEL, 0)], [n["l0_attn_norm"]], [],
                                                 [(dhn, D_MODEL, 0)], [F32], adds=[(dh, D_MODEL, 0)],
                                                 name="l0_attn_norm_bwd")
    small = {k: G[k] for k in SMALL}
    if not ex:
        return loss_vec[0, 0], grad_x, dict(zip(BIG, by_chip(BIG))), small
    sums.update(zip(REDUCE_LAST, ex.pair_sums(REDUCE_LAST, by_chip(REDUCE_LAST))))
    landed.update(zip(REDUCE_LAST, scatter_chips([sums[k] for k in REDUCE_LAST])))
    return loss_vec[0, 0], grad_x, ex.finish(sums, landed), small


HBM = pl.BlockSpec(memory_space=pltpu.HBM)
VMEM = pl.BlockSpec(memory_space=pltpu.VMEM)


def _place():
    return lax.axis_index("x"), lax.axis_index("y"), lax.axis_index("c")


def _other_chips(x, y):
    return [(1 - x, y), (x, 1 - y), (1 - x, 1 - y)]


def _comm_call(body, *, name, out_shape, in_specs, out_specs, scratch_shapes):
    return pl.pallas_call(body, name=name, out_shape=out_shape, in_specs=in_specs, out_specs=out_specs,
                          scratch_shapes=list(scratch_shapes),
                          compiler_params=pltpu.CompilerParams(vmem_limit_bytes=VMEM_LIMIT_MB << 20))


def _inplace_comm_call(body, bufs, *, name, n_sems):
    n = len(bufs)
    return pl.pallas_call(body, name=name, out_shape=[jax.ShapeDtypeStruct(b.shape, b.dtype) for b in bufs],
                          in_specs=[HBM] * n, out_specs=[HBM] * n, input_output_aliases={i: i for i in range(n)},
                          scratch_shapes=[pltpu.SemaphoreType.DMA((n_sems,)), pltpu.SemaphoreType.DMA((n_sems,))],
                          compiler_params=pltpu.CompilerParams(vmem_limit_bytes=VMEM_LIMIT_MB << 20))(*bufs)


def all_gather_chips(bufs):
    n_sems, start, finish = _gather_phase(len(bufs))
    n = len(bufs)

    def body(*refs):
        outs, send_sems, recv_sems = refs[n:2 * n], refs[2 * n], refs[2 * n + 1]
        start(None, outs, send_sems, recv_sems)
        finish(None, outs, send_sems, recv_sems)

    return _inplace_comm_call(body, bufs, name="all_gather_chips", n_sems=n_sems)


def _gather_phase(n):
    def plan(outs, send_sems, recv_sems):
        x, y, c = _place()

        def copy(w, k, chip, hc, to):
            half = outs[w].shape[1] // 2
            rows = outs[w].at[2 * chip[0] + chip[1], pl.ds(hc * half, half), :]
            return pltpu.make_async_remote_copy(src_ref=rows, dst_ref=rows, send_sem=send_sems.at[6 * w + k],
                                                recv_sem=recv_sems.at[6 * w + k], device_id=to, device_id_type=MESH)

        first = [[copy(w, k, (x, y), c, (*chip, c)) for k, chip in enumerate(_other_chips(x, y))] for w in range(n)]
        passed = [[copy(w, 3 + k, chip, c, (x, y, 1 - c)) for k, chip in enumerate(_other_chips(x, y))]
                  for w in range(n)]
        return copy, first, passed, (x, y, c)

    def start(_, outs, send_sems, recv_sems):
        _, first, _, _ = plan(outs, send_sems, recv_sems)
        for w in range(n):
            for cp in first[w]:
                cp.start()

    def finish(_, outs, send_sems, recv_sems):
        copy, first, passed, (x, y, c) = plan(outs, send_sems, recv_sems)
        chips = _other_chips(x, y)
        for w in range(n):
            for k, chip in enumerate(chips):
                copy(w, k, chip, c, (x, y, c)).wait_recv()
                passed[w][k].start()
        for w in range(n):
            for k, chip in enumerate(chips):
                copy(w, 3 + k, chip, 1 - c, (x, y, c)).wait_recv()
        for w in range(n):
            for cp in first[w] + passed[w]:
                cp.wait_send()

    return 6 * n, start, finish


def hosted_gather(bufs):
    n_sems, start, finish = _gather_phase(len(bufs))
    return Hosted(bufs, [jax.ShapeDtypeStruct(b.shape, b.dtype) for b in bufs], {i: i for i in range(len(bufs))},
                  n_sems, start, finish)


def pair_swap_halves(gs):
    n = len(gs)

    def body(*refs):
        g_refs, o_refs, send_sems, recv_sems = refs[:n], refs[n:2 * n], refs[2 * n], refs[2 * n + 1]
        x, y, c = _place()
        copies = []
        for w in range(n):
            half = g_refs[w].shape[1] // 2
            copies.append(pltpu.make_async_remote_copy(
                src_ref=g_refs[w].at[:, pl.ds((1 - c) * half, half), :], dst_ref=o_refs[w], send_sem=send_sems.at[w],
                recv_sem=recv_sems.at[w], device_id=(x, y, 1 - c), device_id_type=MESH))
        for cp in copies:
            cp.start()
        for cp in copies:
            cp.wait()

    return _comm_call(body, name="pair_swap_halves",
                      out_shape=[jax.ShapeDtypeStruct((N_CHIPS, g.shape[1] // 2, g.shape[2]), g.dtype) for g in gs],
                      in_specs=[HBM] * n, out_specs=[HBM] * n,
                      scratch_shapes=[pltpu.SemaphoreType.DMA((n,)), pltpu.SemaphoreType.DMA((n,))])(*gs)


def scatter_chips(ps):
    n = len(ps)
    n_sems, start, finish = _scatter_phase(n)

    def body(*refs):
        p_refs, o_refs, send_sems, recv_sems = refs[:n], refs[n:2 * n], refs[2 * n], refs[2 * n + 1]
        start(p_refs, o_refs, send_sems, recv_sems)
        finish(p_refs, o_refs, send_sems, recv_sems)

    return _comm_call(body, name="scatter_chips", out_shape=_scatter_shapes(ps), in_specs=[HBM] * n, out_specs=[HBM] * n,
                      scratch_shapes=[pltpu.SemaphoreType.DMA((n_sems,)), pltpu.SemaphoreType.DMA((n_sems,))])(*ps)


def _scatter_shapes(ps):
    return [jax.ShapeDtypeStruct((3,) + p.shape[1:], p.dtype) for p in ps]


def _scatter_phase(n):
    def copies(p_refs, o_refs, send_sems, recv_sems):
        x, y, c = _place()
        return [pltpu.make_async_remote_copy(src_ref=p_refs[w].at[2 * chip[0] + chip[1]], dst_ref=o_refs[w].at[k],
                                             send_sem=send_sems.at[3 * w + k], recv_sem=recv_sems.at[3 * w + k],
                                             device_id=(*chip, c), device_id_type=MESH)
                for w in range(n) for k, chip in enumerate(_other_chips(x, y))]

    def start(*refs):
        for cp in copies(*refs):
            cp.start()

    def finish(*refs):
        for cp in copies(*refs):
            cp.wait()

    return 3 * n, start, finish


def hosted_scatter(ps):
    n_sems, start, finish = _scatter_phase(len(ps))
    return Hosted(ps, _scatter_shapes(ps), {}, n_sems, start, finish)


def pair_join_halves(rs):
    n = len(rs)

    def body(*refs):
        outs, send_sems, recv_sems = refs[n:2 * n], refs[2 * n], refs[2 * n + 1]
        x, y, c = _place()
        copies = []
        for w in range(n):
            half = outs[w].shape[0] // 2
            rows = outs[w].at[pl.ds(c * half, half), :]
            copies.append(pltpu.make_async_remote_copy(src_ref=rows, dst_ref=rows, send_sem=send_sems.at[w],
                                                       recv_sem=recv_sems.at[w], device_id=(x, y, 1 - c),
                                                       device_id_type=MESH))
        for cp in copies:
            cp.start()
        for cp in copies:
            cp.wait()

    return _inplace_comm_call(body, rs, name="pair_join_halves", n_sems=n)


def all_reduce_small(v, name):
    n, L = v.shape
    n_dev = 8

    def body(v_ref, out_ref, buf, send_sems, recv_sems):
        x, y, c = _place()
        me = 4 * x + 2 * y + c
        buf[me] = v_ref[...]

        def copy(k, slot, peer):
            return pltpu.make_async_remote_copy(src_ref=v_ref, dst_ref=buf.at[slot], send_sem=send_sems.at[k],
                                                recv_sem=recv_sems.at[slot],
                                                device_id=(peer // 4, (peer // 2) % 2, peer % 2), device_id_type=MESH)

        sends = [copy(k - 1, me, (me + k) % n_dev) for k in range(1, n_dev)]
        for cp in sends:
            cp.start()
        for k in range(1, n_dev):
            src = (me + k) % n_dev
            copy(0, src, src).wait_recv()
        for cp in sends:
            cp.wait_send()
        acc = buf[0]
        for s in range(1, n_dev):
            acc = acc + buf[s]
        out_ref[...] = acc

    return _comm_call(body, name=name, out_shape=jax.ShapeDtypeStruct((n, L), v.dtype), in_specs=[VMEM], out_specs=VMEM,
                      scratch_shapes=[pltpu.VMEM((n_dev, n, L), v.dtype), pltpu.SemaphoreType.DMA((n_dev - 1,)),
                                      pltpu.SemaphoreType.DMA((n_dev,))])(v)


BF16_ROWS = 16


def _rows_tile(n, row_bytes, budget=1 << 20, mult=SUBLANES):
    best = mult if n % mult == 0 else n
    for t in range(mult, n + 1, mult):
        if n % t == 0 and t * row_bytes <= budget:
            best = t
    return best


def _scalars(*vals):
    return jnp.stack([jnp.asarray(v, jnp.int32) for v in vals])


def cast_to_slot(w, chip, name):
    r, c = w.shape
    tb = _rows_tile(r, c * 4, mult=BF16_ROWS)

    def body(s_ref, w_ref, o_ref):
        o_ref[...] = w_ref[...].astype(BF16)

    spec = pltpu.PrefetchScalarGridSpec(
        num_scalar_prefetch=1, grid=(r // tb,), in_specs=[pl.BlockSpec((tb, c), lambda i, s: (i, 0))],
        out_specs=pl.BlockSpec((None, tb, c), lambda i, s: (s[0], i, 0)))
    return pl.pallas_call(body, name=name, grid_spec=spec, out_shape=jax.ShapeDtypeStruct((N_CHIPS, r, c), BF16),
                          compiler_params=pltpu.CompilerParams(dimension_semantics=("parallel",)))(_scalars(chip), w)


def pair_add(g, got, c, name):
    _, r, w = g.shape
    half = r // 2
    tb = _rows_tile(half, w * 4, mult=BF16_ROWS)
    nb = half // tb

    def body(c_ref, g_ref, got_ref, o_ref):
        o_ref[...] = (g_ref[...].astype(F32) + got_ref[...].astype(F32)).astype(o_ref.dtype)

    spec = pltpu.PrefetchScalarGridSpec(
        num_scalar_prefetch=1, grid=(N_CHIPS, nb),
        in_specs=[pl.BlockSpec((None, tb, w), lambda s, i, c_ref: (s, c_ref[0] * nb + i, 0)),
                  pl.BlockSpec((None, tb, w), lambda s, i, c_ref: (s, i, 0))],
        out_specs=pl.BlockSpec((None, tb, w), lambda s, i, c_ref: (s, i, 0)))
    return pl.pallas_call(body, name=name, grid_spec=spec, out_shape=jax.ShapeDtypeStruct((N_CHIPS, half, w), BF16),
                          compiler_params=pltpu.CompilerParams(dimension_semantics=("parallel", "parallel")))(
        _scalars(c), g, got)


def chip_add(p, got, chip, c, name):
    _, h, w = p.shape
    tb = _rows_tile(h, w * 4, mult=BF16_ROWS)
    nb = h // tb

    def body(s_ref, p_ref, got_ref, o_ref):
        acc = p_ref[...].astype(F32)
        for k in range(3):
            acc = acc + got_ref[k].astype(F32)
        o_ref[...] = acc

    spec = pltpu.PrefetchScalarGridSpec(
        num_scalar_prefetch=1, grid=(nb,),
        in_specs=[pl.BlockSpec((None, tb, w), lambda i, s: (s[0], i, 0)),
                  pl.BlockSpec((3, tb, w), lambda i, s: (0, i, 0))],
        out_specs=pl.BlockSpec((tb, w), lambda i, s: (s[1] * nb + i, 0)))
    return pl.pallas_call(body, name=name, grid_spec=spec, out_shape=jax.ShapeDtypeStruct((2 * h, w), F32),
                          compiler_params=pltpu.CompilerParams(dimension_semantics=("parallel",)))(
        _scalars(chip, c), p, got)


def adamw(w, g, m, v, name):
    r, c = w.shape
    tr = _rows_tile(r, c * 4)

    def body(w_ref, g_ref, m_ref, v_ref, d_ref, m_out, v_out):
        gg = g_ref[...]
        m2 = ADAM_B1 * m_ref[...] + (1.0 - ADAM_B1) * gg
        v2 = ADAM_B2 * v_ref[...] + (1.0 - ADAM_B2) * jnp.square(gg)
        m_hat = m2 / (1.0 - ADAM_B1 ** ADAM_STEP)
        v_hat = v2 / (1.0 - ADAM_B2 ** ADAM_STEP)
        d_ref[...] = -ADAM_LR * (m_hat / (jnp.sqrt(v_hat) + ADAM_EPS) + ADAM_WD * w_ref[...])
        m_out[...] = m2
        v_out[...] = v2

    blk = pl.BlockSpec((tr, c), lambda i: (i, 0))
    return _pcall(body, name=name, grid=(r // tr,), in_specs=[blk] * 4, out_specs=[blk] * 3,
                  out_shape=[jax.ShapeDtypeStruct((r, c), F32)] * 3, dims=("parallel",))(w, g, m, v)


WEIGHTS = ["l0_attn_norm", "l0_w_in", "l0_gdn_conv", "l0_gdn_A_log", "l0_gdn_dt_bias", "l0_gdn_norm", "l0_mla_q_norm",
           "l0_mla_w_uq", "l0_mla_kv_norm", "l0_mla_w_ukv", "l0_w_out", "l0_ffn_norm", "l0_ffn_w_up", "l0_ffn_conv_w",
           "l0_ffn_conv_b", "l0_ffn_w_down", "l0_ple_proj", "l0_ple_gate_norm", "l0_ple_gate", "l1_attn_norm",
           "l1_w_in", "l1_ret_norm", "l1_w_out", "l1_ffn_norm", "l1_ffn_w_up", "l1_ffn_conv_w", "l1_ffn_conv_b",
           "l1_ffn_w_down", "l1_ple_proj", "l1_ple_gate_norm", "l1_ple_gate", "final_norm"]
COL_SHARDED = ["l0_w_in", "l0_mla_w_uq", "l0_mla_w_ukv", "l0_ffn_w_up", "l0_ple_proj", "l1_w_in", "l1_ffn_w_up",
               "l1_ple_proj"]
ROW_SHARDED = ["l0_w_out", "l0_ffn_w_down", "l0_ple_gate", "l1_w_out", "l1_ffn_w_down", "l1_ple_gate"]
BIG = [k for k in WEIGHTS if k in COL_SHARDED or k in ROW_SHARDED]
SMALL_SHARDED = ["l0_gdn_conv", "l0_ffn_conv_w", "l1_ffn_conv_w"]
SMALL = [k for k in WEIGHTS if k not in BIG]
KEPT_SHARDED = ["l0_ffn_w_up", "l0_ple_proj", "l1_w_in", "l1_ffn_w_up", "l1_ple_proj"]
GATHER_FIRST = ["l0_w_in", "l0_mla_w_uq", "l0_mla_w_ukv"]
GATHER_L0 = ["l0_w_out", "l0_ffn_w_up", "l0_ffn_w_down", "l0_ple_proj", "l0_ple_gate", "l1_w_out"]
GATHER_L1 = ["l1_w_in"]
GATHER_FFN = ["l1_ffn_w_down", "l1_ple_proj", "l1_ple_gate"]
GATHER_RET = ["l1_ffn_w_up"]
REDUCE_L1 = [k for k in BIG if k.startswith("l1_")]
REDUCE_DQ = ["l0_ffn_w_up"]
REDUCE_L0 = ["l0_ffn_w_down", "l0_ple_proj", "l0_ple_gate", "l0_w_out"]
REDUCE_LAST = ["l0_w_in", "l0_mla_w_uq", "l0_mla_w_ukv"]


class Exchange:
    def __init__(self, chip, core):
        self.chip, self.core = chip, core

    def gather(self, bufs):
        return all_gather_chips(bufs)

    def pair_sums(self, names, grads):
        return [pair_add(g, got, self.core, "rs_pair_add_" + k)
                for k, g, got in zip(names, grads, pair_swap_halves(grads))]

    def finish(self, sums, landed):
        halves = [chip_add(sums[k], landed[k], self.chip, self.core, "rs_chip_add_" + k) for k in BIG]
        return dict(zip(BIG, pair_join_halves(halves)))


def _cols_to_full(s):
    j, k, n = s.shape
    return jnp.transpose(s, (1, 0, 2)).reshape(k, j * n)


def _full_to_cols(g):
    k, n4 = g.shape
    return jnp.transpose(g.reshape(k, N_CHIPS, n4 // N_CHIPS), (1, 0, 2))


def _pack_small(vals):
    flat = jnp.concatenate([v.astype(F32).reshape(-1) for v in vals])
    align = SUBLANES * LANES
    flat = jnp.pad(flat, (0, -flat.shape[0] % align))
    return flat.reshape(-1, LANES)


def _unpack_small(rows, shapes):
    flat = rows.reshape(-1)
    out, off = [], 0
    for shp in shapes:
        n = int(np.prod(shp))
        out.append(flat[off:off + n].reshape(shp))
        off += n
    return out


INPUTS = (["x", "p", "positions"] + WEIGHTS + ["loss_target"] + ["m_" + k for k in WEIGHTS]
          + ["v_" + k for k in WEIGHTS])


def kernel(
        x, p, positions, l0_attn_norm, l0_w_in, l0_gdn_conv, l0_gdn_A_log, l0_gdn_dt_bias, l0_gdn_norm, l0_mla_q_norm,
        l0_mla_w_uq, l0_mla_kv_norm, l0_mla_w_ukv, l0_w_out, l0_ffn_norm, l0_ffn_w_up, l0_ffn_conv_w, l0_ffn_conv_b,
        l0_ffn_w_down, l0_ple_proj, l0_ple_gate_norm, l0_ple_gate, l1_attn_norm, l1_w_in, l1_ret_norm, l1_w_out,
        l1_ffn_norm, l1_ffn_w_up, l1_ffn_conv_w, l1_ffn_conv_b, l1_ffn_w_down, l1_ple_proj, l1_ple_gate_norm,
        l1_ple_gate, final_norm, loss_target, m_l0_attn_norm, m_l0_w_in, m_l0_gdn_conv, m_l0_gdn_A_log,
        m_l0_gdn_dt_bias, m_l0_gdn_norm, m_l0_mla_q_norm, m_l0_mla_w_uq, m_l0_mla_kv_norm, m_l0_mla_w_ukv, m_l0_w_out,
        m_l0_ffn_norm, m_l0_ffn_w_up, m_l0_ffn_conv_w, m_l0_ffn_conv_b, m_l0_ffn_w_down, m_l0_ple_proj,
        m_l0_ple_gate_norm, m_l0_ple_gate, m_l1_attn_norm, m_l1_w_in, m_l1_ret_norm, m_l1_w_out, m_l1_ffn_norm,
        m_l1_ffn_w_up, m_l1_ffn_conv_w, m_l1_ffn_conv_b, m_l1_ffn_w_down, m_l1_ple_proj, m_l1_ple_gate_norm,
        m_l1_ple_gate, m_final_norm, v_l0_attn_norm, v_l0_w_in, v_l0_gdn_conv, v_l0_gdn_A_log, v_l0_gdn_dt_bias,
        v_l0_gdn_norm, v_l0_mla_q_norm, v_l0_mla_w_uq, v_l0_mla_kv_norm, v_l0_mla_w_ukv, v_l0_w_out, v_l0_ffn_norm,
        v_l0_ffn_w_up, v_l0_ffn_conv_w, v_l0_ffn_conv_b, v_l0_ffn_w_down, v_l0_ple_proj, v_l0_ple_gate_norm,
        v_l0_ple_gate, v_l1_attn_norm, v_l1_w_in, v_l1_ret_norm, v_l1_w_out, v_l1_ffn_norm, v_l1_ffn_w_up,
        v_l1_ffn_conv_w, v_l1_ffn_conv_b, v_l1_ffn_w_down, v_l1_ple_proj, v_l1_ple_gate_norm, v_l1_ple_gate,
        v_final_norm):
    given = locals()
    a = {k: given[k] for k in INPUTS}
    x_i, y_i, c_i = _place()
    chip = 2 * x_i + y_i
    shard_shapes = {k: a[k].shape for k in WEIGHTS}

    slots = {k: cast_to_slot(a[k], chip, "cast_" + k) for k in BIG}
    W = {}
    placed = []
    for k in SMALL_SHARDED:
        r, c = shard_shapes[k]
        mine = jnp.where(c_i == 0, a[k], jnp.zeros_like(a[k]))
        placed.append(lax.dynamic_update_slice(jnp.zeros((r, N_CHIPS * c), F32), mine, (0, chip * c)))
    full_small = _unpack_small(all_reduce_small(_pack_small(placed), "gather_small_weights"),
                               [p_.shape for p_ in placed])
    for k in SMALL:
        W[k] = a[k]
    W.update(dict(zip(SMALL_SHARDED, full_small)))

    loss_part, grad_x, grads, G = local_step(a["x"][0], a["p"][:, 0], a["positions"][0], a["loss_target"][0], slots, W,
                                             Exchange(chip, c_i))
    loss = lax.psum(loss_part, ("x", "y", "c"))
    deltas, new_m, new_v = {}, {}, {}
    for k in BIG:
        deltas[k], new_m[k], new_v[k] = adamw(a[k], grads[k], a["m_" + k], a["v_" + k], "adamw_" + k)

    small_full = [G[k].reshape(-1) for k in SMALL]
    summed = _unpack_small(all_reduce_small(_pack_small(small_full), "reduce_small_grads"),
                           [G[k].shape for k in SMALL])
    for k, g in zip(SMALL, summed):
        if k in SMALL_SHARDED:
            r, c = shard_shapes[k]
            g = lax.dynamic_slice(g.reshape(r, N_CHIPS * c), (0, chip * c), (r, c))
        grads[k] = g.reshape(shard_shapes[k])
    packed = [_pack_small([d[k] for k in SMALL]) for d in (
        {k: a[k] for k in SMALL}, grads, {k: a["m_" + k] for k in SMALL}, {k: a["v_" + k] for k in SMALL})]
    outs = adamw(*packed, "adamw_small")
    shapes = [shard_shapes[k] for k in SMALL]
    for d, rows in zip((deltas, new_m, new_v), outs):
        d.update(dict(zip(SMALL, _unpack_small(rows, shapes))))

    return (loss, grad_x[None], *[grads[k] for k in WEIGHTS], *[deltas[k] for k in WEIGHTS],
            *[new_m[k] for k in WEIGHTS], *[new_v[k] for k in WEIGHTS])
```

```python
import functools
import math

import numpy as np
import jax
import jax.numpy as jnp
from jax import lax
from jax.experimental import pallas as pl
from jax.experimental.pallas import tpu as pltpu

F32, BF16 = jnp.float32, jnp.bfloat16
HI = lax.Precision.HIGHEST
MESH = pl.DeviceIdType.MESH

NORM_EPS = 1e-6
ROPE_THETA = 10000.0
D_MODEL = 2048
PLE_DIM = 256
GDN_HEADS, GDN_DK, GDN_DV, GDN_CONV = 8, 128, 128, 4
MLA_HEADS, MLA_Q_RANK, MLA_KV_RANK, MLA_NOPE, MLA_ROPE, MLA_V = 8, 512, 512, 128, 64, 128
RET_HEADS, RET_DK, RET_DV = 8, 256, 512
D_FF, FFN_CONV = 5632, 3
ADAM_LR, ADAM_B1, ADAM_B2, ADAM_EPS, ADAM_WD, ADAM_STEP = 0.001, 0.9, 0.999, 1e-08, 0.01, 10

LANES = 128
SUBLANES = 8
CHUNK = 128
N_CHIPS = 4
VMEM_LIMIT_MB = 56

ZIN_QKV, ZIN_Z, ZIN_CQ, ZIN_CKV, ZIN_KR, ZIN_AB, ZIN_W = 0, 3072, 4096, 4608, 5120, 5248, 5376


class Hosted:
    def __init__(self, inputs, out_shapes, aliases, n_sems, start, finish):
        self.inputs, self.out_shapes, self.aliases, self.n_sems = list(inputs), list(out_shapes), dict(aliases), n_sems
        self.start, self.finish = start, finish


def _pcall(body, *, name, out_shape, grid=(), in_specs=None, out_specs=None, scratch_shapes=(), dims=None,
           hosted=None):
    params = dict(vmem_limit_bytes=VMEM_LIMIT_MB << 20)
    if dims is not None:
        params["dimension_semantics"] = dims
    if hosted is None:
        return pl.pallas_call(body, name=name, out_shape=out_shape, grid=grid, in_specs=in_specs, out_specs=out_specs,
                              scratch_shapes=list(scratch_shapes), compiler_params=pltpu.CompilerParams(**params))
    single = not isinstance(out_shape, (list, tuple))
    out_shape = [out_shape] if single else list(out_shape)
    out_specs = [out_specs] if single else list(out_specs)
    n_in, n_out, n_scr = len(in_specs), len(out_shape), len(scratch_shapes)
    h_in, h_out = len(hosted.inputs), len(hosted.out_shapes)
    hbm = pl.BlockSpec(memory_space=pltpu.HBM)

    def hosting_body(*refs):
        ins, h_ins = refs[:n_in], refs[n_in:n_in + h_in]
        o0 = n_in + h_in
        outs, h_outs = refs[o0:o0 + n_out], refs[o0 + n_out:o0 + n_out + h_out]
        s0 = o0 + n_out + h_out
        scr, (send_sems, recv_sems) = refs[s0:s0 + n_scr], refs[s0 + n_scr:]
        ids = [pl.program_id(d) for d in range(len(grid))]
        first = functools.reduce(lambda u, v: u & v, [i == 0 for i in ids])
        last = functools.reduce(lambda u, v: u & v, [i == g - 1 for i, g in zip(ids, grid)])

        @pl.when(first)
        def _():
            hosted.start(h_ins, h_outs, send_sems, recv_sems)

        body(*ins, *outs, *scr)

        @pl.when(last)
        def _():
            hosted.finish(h_ins, h_outs, send_sems, recv_sems)

    params["dimension_semantics"] = ("arbitrary",) * len(grid)
    call = pl.pallas_call(
        hosting_body, name=name, out_shape=out_shape + hosted.out_shapes, grid=grid,
        in_specs=list(in_specs) + [hbm] * h_in, out_specs=out_specs + [hbm] * h_out,
        scratch_shapes=list(scratch_shapes) + [pltpu.SemaphoreType.DMA((hosted.n_sems,)),
                                               pltpu.SemaphoreType.DMA((hosted.n_sems,))],
        input_output_aliases={n_in + i: n_out + o for i, o in hosted.aliases.items()},
        compiler_params=pltpu.CompilerParams(**params))

    def run(*args):
        res = call(*args, *hosted.inputs)
        main = res[:n_out]
        return (main[0] if single else main), list(res[n_out:])

    return run


def _tile(n, target, mult=LANES):
    best = None
    for t in range(mult, min(n, target) + 1, mult):
        if n % t == 0:
            best = t
    return best or n


_DN = {"nn": (((1,), (0,)), ((), ())), "nt": (((1,), (1,)), ((), ())), "tn": (((0,), (0,)), ((), ()))}


def matmul(a, b, mode, out_dtype, *, name, add=None, b_shards=1, out_shards=1, tm=512, tn=1024, tk=2048,
           hosted=None):
    bs = b.shape[-2:]
    if mode == "nn":
        (M, K), (K2, N) = a.shape, (bs[0], bs[1] * b_shards)
    elif mode == "nt":
        (M, K), (N, K2) = a.shape, (bs[0], bs[1] * b_shards)
    else:
        (K, M), (K2, N) = a.shape, bs
    assert K == K2, (name, a.shape, b.shape)
    n_sh = N // max(b_shards if mode == "nn" else 1, out_shards)
    k_sh = K // (b_shards if mode == "nt" else 1)
    tm, tn, tk = _tile(M, tm), _tile(n_sh, tn), _tile(k_sh, tk)
    nk = K // tk
    nbn, nbk = n_sh // tn, k_sh // tk
    dn = _DN[mode]
    has_add = add is not None
    a_bytes, b_bytes = a.size * a.dtype.itemsize, b.size * b.dtype.itemsize
    i_outer = nk > 1 or a_bytes + (M // tm) * b_bytes <= b_bytes + (N // tn) * a_bytes

    def ij(g0, g1):
        return (g0, g1) if i_outer else (g1, g0)

    def body(*refs):
        a_ref, b_ref = refs[:2]
        add_ref = refs[2] if has_add else None
        o_ref = refs[3 if has_add else 2]
        part = lax.dot_general(a_ref[...].astype(BF16), b_ref[...].astype(BF16), dn, preferred_element_type=F32)

        def finish(r):
            if has_add:
                r = r + add_ref[...]
            o_ref[...] = r.astype(out_dtype)

        if nk == 1:
            finish(part)
            return
        acc = refs[-1]
        k = pl.program_id(2)

        @pl.when(k == 0)
        def _():
            acc[...] = part

        @pl.when(k > 0)
        def _():
            acc[...] += part

        @pl.when(k == nk - 1)
        def _():
            finish(acc[...])

    def spec(block, fn):
        return pl.BlockSpec(block, lambda g0, g1, k: fn(*ij(g0, g1), k))

    if mode == "tn":
        a_spec = spec((tk, tm), lambda i, j, k: (k, i))
    else:
        a_spec = spec((tm, tk), lambda i, j, k: (i, k))
    if mode == "nt":
        if b_shards > 1:
            b_spec = spec((None, tn, tk), lambda i, j, k: (k // nbk, j, k % nbk))
        else:
            b_spec = spec((tn, tk), lambda i, j, k: (j, k))
    elif b_shards > 1:
        b_spec = spec((None, tk, tn), lambda i, j, k: (j // nbn, k, j % nbn))
    else:
        b_spec = spec((tk, tn), lambda i, j, k: (k, j))
    in_specs = [a_spec, b_spec]
    args = [a, b]
    if has_add:
        in_specs.append(spec((tm, tn), lambda i, j, k: (i, j)))
        args.append(add)
    if out_shards > 1:
        out_spec = spec((None, tm, tn), lambda i, j, k: (j // nbn, i, j % nbn))
        out_shape = jax.ShapeDtypeStruct((out_shards, M, n_sh), out_dtype)
    else:
        out_spec = spec((tm, tn), lambda i, j, k: (i, j))
        out_shape = jax.ShapeDtypeStruct((M, N), out_dtype)
    gi, gj = M // tm, N // tn
    return _pcall(body, name=name, out_shape=out_shape, grid=(gi, gj, nk) if i_outer else (gj, gi, nk),
                  in_specs=in_specs, out_specs=out_spec,
                  scratch_shapes=[pltpu.VMEM((tm, tn), F32)] if nk > 1 else [],
                  dims=("parallel", "parallel", "arbitrary"), hosted=hosted)(*args)


def _row_spec(tr, w, c):
    return pl.BlockSpec((tr, w), lambda i: (i, c))


def _full_spec(arr):
    return pl.BlockSpec(arr.shape, lambda i: (0,) * arr.ndim)


def rowwise(fn, rows, params, nd_rows, outs, *, name, tr=256):
    S = rows[0][0].shape[0]
    tr = min(tr, S)
    n_in = len(rows) + len(params) + len(nd_rows)

    def body(*refs):
        res = fn(*[x[...] for x in refs[:n_in]])
        for o_ref, v in zip(refs[n_in:], res):
            o_ref[...] = v.astype(o_ref.dtype)

    return _pcall(body, name=name, grid=(S // tr,),
                  in_specs=([_row_spec(tr, w, c) for (_, w, c) in rows] + [_full_spec(q) for q in params]
                            + [_row_spec(tr, w, c) for (_, w, c) in nd_rows]),
                  out_specs=[_row_spec(tr, w, 0) for (w, _) in outs],
                  out_shape=[jax.ShapeDtypeStruct((S, w), dt) for (w, dt) in outs],
                  dims=("parallel",))(*[r[0] for r in rows], *params, *[r[0] for r in nd_rows])


def rowwise_bwd(fn, rows, params, nd_rows, cts, d_dtypes, *, name, adds=None, tr=256):
    S = rows[0][0].shape[0]
    tr = min(tr, S)
    n_r, n_p, n_n, n_c = len(rows), len(params), len(nd_rows), len(cts)
    adds = adds or [None] * n_r
    add_list = [a for a in adds if a is not None]
    n_a = len(add_list)
    d_dtypes = [dt if isinstance(dt, (list, tuple)) else (dt,) for dt in d_dtypes]
    n_d = sum(len(dt) for dt in d_dtypes)

    def body(*refs):
        it = iter(refs)
        r = [next(it)[...] for _ in range(n_r)]
        p = [next(it)[...] for _ in range(n_p)]
        nd = [next(it)[...] for _ in range(n_n)]
        c = [next(it)[...] for _ in range(n_c)]
        ad = [next(it)[...] for _ in range(n_a)]
        d_row_refs = [[next(it) for _ in dts] for dts in d_dtypes]
        d_par_refs = [next(it) for _ in range(n_p)]
        outs, vjp = jax.vjp(lambda *dp: fn(*dp, *nd), *r, *p)
        g = vjp(tuple(ci.astype(o.dtype) for ci, o in zip(c, outs)))
        ai = 0
        for k in range(n_r):
            gk = g[k].astype(F32)
            if adds[k] is not None:
                gk = gk + ad[ai].astype(F32)
                ai += 1
            for ref in d_row_refs[k]:
                ref[...] = gk.astype(ref.dtype)

        @pl.when(pl.program_id(0) == 0)
        def _():
            for ref in d_par_refs:
                ref[...] = jnp.zeros_like(ref)

        for k in range(n_p):
            d_par_refs[k][...] += g[n_r + k].astype(F32)

    in_specs = ([_row_spec(tr, w, c) for (_, w, c) in rows] + [_full_spec(q) for q in params]
                + [_row_spec(tr, w, c) for (_, w, c) in nd_rows] + [_row_spec(tr, w, c) for (_, w, c) in cts]
                + [_row_spec(tr, w, c) for (_, w, c) in add_list])
    out_specs = ([_row_spec(tr, w, 0) for (_, w, _), dts in zip(rows, d_dtypes) for _ in dts]
                 + [_full_spec(q) for q in params])
    out_shape = ([jax.ShapeDtypeStruct((S, w), dt) for (_, w, _), dts in zip(rows, d_dtypes) for dt in dts]
                 + [jax.ShapeDtypeStruct(q.shape, F32) for q in params])
    res = _pcall(body, name=name, grid=(S // tr,), in_specs=in_specs, out_specs=out_specs, out_shape=out_shape,
                 dims=("arbitrary",))(*[r[0] for r in rows], *params, *[r[0] for r in nd_rows],
                                      *[r[0] for r in cts], *[r[0] for r in add_list])
    d_rows, i = [], 0
    for dts in d_dtypes:
        d_rows.append(res[i] if len(dts) == 1 else tuple(res[i:i + len(dts)]))
        i += len(dts)
    return d_rows, res[n_d:]


def _rms(x, g):
    x = x.astype(F32)
    return x * lax.rsqrt(jnp.mean(x * x, axis=-1, keepdims=True) + NORM_EPS) * g


def _fn_rms(x, g):
    return (_rms(x, g),)


def _sigmoid(x):
    return 1.0 / (1.0 + jnp.exp(-x))


def _silu(x):
    return x * _sigmoid(x)


def _softplus(x):
    return jnp.maximum(x, 0.0) + jnp.log(1.0 + jnp.exp(-jnp.abs(x)))


def _fn_ple(h, pp, gl):
    return (h.astype(F32) + pp.astype(F32) * _sigmoid(gl.astype(F32)),)


def _fn_ple_terms(pp, gl):
    return (pp.astype(F32) * _sigmoid(gl.astype(F32)),)


def _rot_half_matrix():
    half = MLA_ROPE // 2
    r = lax.broadcasted_iota(jnp.int32, (LANES, LANES), 0)
    c = lax.broadcasted_iota(jnp.int32, (LANES, LANES), 1)
    plus = (c == r + half) & (r < half)
    minus = (r == c + half) & (c < half)
    return jnp.where(plus, 1.0, 0.0) - jnp.where(minus, 1.0, 0.0)


def _rope_pad(x, cosp, sinp):
    return x * cosp + jnp.dot(x, _rot_half_matrix(), precision=HI, preferred_element_type=F32) * sinp


def _fn_mla_pre(cq, ckv, kr, qn_w, kvn_w, cosp, sinp):
    return (_rms(cq, qn_w), _rms(ckv, kvn_w), _rope_pad(kr.astype(F32), cosp, sinp))


def _fn_rope_q(q, cosp, sinp):
    q = q.astype(F32)
    parts = []
    for h in range(MLA_HEADS):
        base = 2 * LANES * h
        parts.append(q[:, base:base + LANES])
        parts.append(_rope_pad(q[:, base + LANES:base + 2 * LANES], cosp, sinp))
    return (jnp.concatenate(parts, axis=1),)


def _fn_ret_gate(g, on, w):
    return (_silu(g.astype(F32)) * (on.astype(F32) * w),)


def _shift_down(cur, halo, s):
    if s == 0:
        return cur
    r = pltpu.roll(cur, s, 0)
    hs = pltpu.roll(halo, s, 0)
    row = lax.broadcasted_iota(jnp.int32, hs.shape, 0)
    first = jnp.where(row < s, hs, r[:SUBLANES])
    return jnp.concatenate([first, r[SUBLANES:]], axis=0)


def _shift_up(cur, halo, s):
    if s == 0:
        return cur
    n = cur.shape[0]
    r = pltpu.roll(cur, n - s, 0)
    hs = pltpu.roll(halo, SUBLANES - s, 0)
    row = lax.broadcasted_iota(jnp.int32, hs.shape, 0)
    last = jnp.where(row >= SUBLANES - s, hs, r[n - SUBLANES:])
    return jnp.concatenate([r[:n - SUBLANES], last], axis=0)


def _prev_halo_spec(tr, tw, col):
    return pl.BlockSpec((SUBLANES, tw), lambda c, i: (jnp.maximum(i * (tr // SUBLANES) - 1, 0), col(c)))


def _conv_taps(cur, halo, w_ref, width):
    taps = [_shift_down(cur, halo, width - 1 - j) for j in range(width)]
    y = taps[0] * w_ref[0:1, :]
    for j in range(1, width):
        y = y + taps[j] * w_ref[j:j + 1, :]
    return y, taps


def gdn_conv_fwd(zin, w, *, tr=512, tw=512):
    S = zin.shape[0]
    tr = min(tr, S)
    width, C = w.shape

    def body(cur_ref, halo_ref, w_ref, o_ref):
        i = pl.program_id(1)
        halo = halo_ref[...] * (i > 0).astype(F32)
        y, _ = _conv_taps(cur_ref[...], halo, w_ref, width)
        o_ref[...] = _silu(y)

    return _pcall(body, name="gdn_conv_fwd", grid=(C // tw, S // tr),
                  in_specs=[pl.BlockSpec((tr, tw), lambda c, i: (i, c)), _prev_halo_spec(tr, tw, lambda c: c),
                            pl.BlockSpec((width, tw), lambda c, i: (0, c))],
                  out_specs=pl.BlockSpec((tr, tw), lambda c, i: (i, c)),
                  out_shape=jax.ShapeDtypeStruct((S, C), F32), dims=("parallel", "arbitrary"))(zin, zin, w)


def gdn_conv_bwd(zin, w, dy, *, tr=512, tw=512):
    S = zin.shape[0]
    tr = min(tr, S)
    width, C = w.shape

    def body(cur_ref, halo_ref, w_ref, dy_ref, da_ref, dw_ref):
        i = pl.program_id(1)
        halo = halo_ref[...] * (i > 0).astype(F32)
        y, taps = _conv_taps(cur_ref[...], halo, w_ref, width)
        sg = _sigmoid(y)
        da = dy_ref[...] * (sg * (1.0 + y * (1.0 - sg)))
        da_ref[...] = da

        @pl.when(i == 0)
        def _():
            dw_ref[...] = jnp.zeros_like(dw_ref)

        for j in range(width):
            dw_ref[j:j + 1, :] += jnp.sum(da * taps[j], axis=0, keepdims=True)

    return _pcall(body, name="gdn_conv_bwd", grid=(C // tw, S // tr),
                  in_specs=[pl.BlockSpec((tr, tw), lambda c, i: (i, c)), _prev_halo_spec(tr, tw, lambda c: c),
                            pl.BlockSpec((width, tw), lambda c, i: (0, c)),
                            pl.BlockSpec((tr, tw), lambda c, i: (i, c))],
                  out_specs=[pl.BlockSpec((tr, tw), lambda c, i: (i, c)),
                             pl.BlockSpec((width, tw), lambda c, i: (0, c))],
                  out_shape=[jax.ShapeDtypeStruct((S, C), F32), jax.ShapeDtypeStruct((width, C), F32)],
                  dims=("parallel", "arbitrary"))(zin, zin, w, dy)


def conv_transpose(dy, w, out_dtype, *, name, tr=512, tw=512):
    if dy.ndim == 2:
        dy = dy[None]
    T, S, C = dy.shape
    tr = min(tr, S)
    width = w.shape[0]
    n_i, nc = S // tr, C // tw

    def body(cur_ref, halo_ref, w_ref, o_ref):
        i = pl.program_id(2)
        halo = halo_ref[...] * (i < n_i - 1).astype(F32)
        cur = cur_ref[...]
        acc = cur * w_ref[width - 1:width, :]
        for s in range(1, width):
            acc = acc + _shift_up(cur, halo, s) * w_ref[width - 1 - s:width - s, :]
        o_ref[...] = acc.astype(out_dtype)

    nxt = pl.BlockSpec((None, SUBLANES, tw),
                       lambda t, c, i: (t, jnp.minimum((i + 1) * (tr // SUBLANES), S // SUBLANES - 1), c))
    return _pcall(body, name=name, grid=(T, nc, n_i),
                  in_specs=[pl.BlockSpec((None, tr, tw), lambda t, c, i: (t, i, c)), nxt,
                            pl.BlockSpec((width, tw), lambda t, c, i: (0, t * nc + c))],
                  out_specs=pl.BlockSpec((tr, tw), lambda t, c, i: (i, t * nc + c)),
                  out_shape=jax.ShapeDtypeStruct((S, T * C), out_dtype),
                  dims=("parallel", "parallel", "arbitrary"))(dy, dy, w)


def ffn_conv_fwd(u, w, b, *, tr=512, tw=512):
    S, C2 = u.shape
    tr = min(tr, S)
    width = w.shape[0]
    half = C2 // 2
    nc = half // tw

    def body(g_ref, gh_ref, u_ref, uh_ref, wg_ref, wu_ref, bg_ref, bu_ref, o_ref):
        i = pl.program_id(1)
        live = (i > 0).astype(F32)
        yg, _ = _conv_taps(g_ref[...], gh_ref[...] * live, wg_ref, width)
        yu, _ = _conv_taps(u_ref[...], uh_ref[...] * live, wu_ref, width)
        o_ref[...] = (_silu(yg + bg_ref[...]) * (yu + bu_ref[...])).astype(o_ref.dtype)

    return _pcall(body, name="ffn_conv_fwd", grid=(nc, S // tr),
                  in_specs=[pl.BlockSpec((tr, tw), lambda c, i: (i, c)), _prev_halo_spec(tr, tw, lambda c: c),
                            pl.BlockSpec((tr, tw), lambda c, i: (i, c + nc)),
                            _prev_halo_spec(tr, tw, lambda c: c + nc),
                            pl.BlockSpec((width, tw), lambda c, i: (0, c)),
                            pl.BlockSpec((width, tw), lambda c, i: (0, c + nc)),
                            pl.BlockSpec((1, tw), lambda c, i: (0, c)), pl.BlockSpec((1, tw), lambda c, i: (0, c + nc))],
                  out_specs=pl.BlockSpec((tr, tw), lambda c, i: (i, c)),
                  out_shape=jax.ShapeDtypeStruct((S, half), BF16),
                  dims=("parallel", "arbitrary"))(u, u, u, u, w, w, b, b)


def ffn_conv_bwd(u, w, b, df, *, tr=512, tw=512):
    S, C2 = u.shape
    tr = min(tr, S)
    width = w.shape[0]
    half = C2 // 2
    nc = half // tw

    def body(g_ref, gh_ref, u_ref, uh_ref, wg_ref, wu_ref, bg_ref, bu_ref, df_ref, dc_ref, dw_ref, db_ref):
        i = pl.program_id(1)
        live = (i > 0).astype(F32)
        yg, gt = _conv_taps(g_ref[...], gh_ref[...] * live, wg_ref, width)
        yu, ut = _conv_taps(u_ref[...], uh_ref[...] * live, wu_ref, width)
        yg = yg + bg_ref[...]
        yu = yu + bu_ref[...]
        sg = _sigmoid(yg)
        dfv = df_ref[...].astype(F32)
        dcs = (dfv * yu * (sg * (1.0 + yg * (1.0 - sg))), dfv * (yg * sg))

        @pl.when(i == 0)
        def _():
            dw_ref[...] = jnp.zeros_like(dw_ref)
            db_ref[...] = jnp.zeros_like(db_ref)

        for t, (dc, taps) in enumerate(zip(dcs, (gt, ut))):
            dc_ref[t] = dc
            db_ref[t] += jnp.sum(dc, axis=0, keepdims=True)
            for j in range(width):
                dw_ref[t, j:j + 1, :] += jnp.sum(dc * taps[j], axis=0, keepdims=True)

    dc, dw, db = _pcall(
        body, name="ffn_conv_bwd", grid=(nc, S // tr),
        in_specs=[pl.BlockSpec((tr, tw), lambda c, i: (i, c)), _prev_halo_spec(tr, tw, lambda c: c),
                  pl.BlockSpec((tr, tw), lambda c, i: (i, c + nc)), _prev_halo_spec(tr, tw, lambda c: c + nc),
                  pl.BlockSpec((width, tw), lambda c, i: (0, c)), pl.BlockSpec((width, tw), lambda c, i: (0, c + nc)),
                  pl.BlockSpec((1, tw), lambda c, i: (0, c)), pl.BlockSpec((1, tw), lambda c, i: (0, c + nc)),
                  pl.BlockSpec((tr, tw), lambda c, i: (i, c))],
        out_specs=[pl.BlockSpec((2, tr, tw), lambda c, i: (0, i, c)), pl.BlockSpec((2, width, tw), lambda c, i: (0, 0, c)),
                   pl.BlockSpec((2, 1, tw), lambda c, i: (0, 0, c))],
        out_shape=[jax.ShapeDtypeStruct((2, S, half), F32), jax.ShapeDtypeStruct((2, width, half), F32),
                   jax.ShapeDtypeStruct((2, 1, half), F32)],
        dims=("parallel", "arbitrary"))(u, u, u, u, w, w, b, b, df)
    return dc, jnp.concatenate([dw[0], dw[1]], axis=1), jnp.concatenate([db[0], db[1]], axis=1)


_MODE_OF = {v: k for k, v in _DN.items()}


def _bf16_dot(a, b, mode):
    return lax.dot_general(a.astype(BF16), b.astype(BF16), _DN[mode], preferred_element_type=F32)


@functools.partial(jax.custom_vjp, nondiff_argnums=(2,))
def _bdot_mode(a, b, mode):
    return _bf16_dot(a, b, mode)


def _bdot_fwd(a, b, mode):
    return _bf16_dot(a, b, mode), (a, b)


def _bdot_bwd(mode, res, ct):
    a, b = res
    if mode == "nn":
        da, db = _bf16_dot(ct, b, "nt"), _bf16_dot(a, ct, "tn")
    elif mode == "nt":
        da, db = _bf16_dot(ct, b, "nn"), _bf16_dot(ct, a, "tn")
    else:
        da, db = _bf16_dot(b, ct, "nt"), _bf16_dot(a, ct, "nn")
    return da.astype(a.dtype), db.astype(b.dtype)


_bdot_mode.defvjp(_bdot_fwd, _bdot_bwd)


def _bdot(a, b, dn=_DN["nn"]):
    return _bdot_mode(a, b, _MODE_OF[dn])


def _hi_lo(x):
    hi = x.astype(BF16)
    return hi, (x - hi.astype(F32)).astype(BF16)


def _dot3_raw(a, b, mode):
    a1, a2 = _hi_lo(a)
    b1, b2 = _hi_lo(b)
    dot = lambda p, q: lax.dot_general(p, q, _DN[mode], preferred_element_type=F32)
    return dot(a1, b1) + (dot(a1, b2) + dot(a2, b1))


@functools.partial(jax.custom_vjp, nondiff_argnums=(2,))
def _dot3(a, b, mode="nn"):
    return _dot3_raw(a, b, mode)


def _dot3_fwd(a, b, mode):
    return _dot3_raw(a, b, mode), (a, b)


def _dot3_bwd(mode, res, ct):
    a, b = res
    if mode == "nn":
        return _dot3_raw(ct, b, "nt"), _dot3_raw(a, ct, "tn")
    if mode == "nt":
        return _dot3_raw(ct, b, "nn"), _dot3_raw(ct, a, "tn")
    return _dot3_raw(b, ct, "nt"), _dot3_raw(a, ct, "nn")


_dot3.defvjp(_dot3_fwd, _dot3_bwd)


@functools.partial(jax.custom_vjp, nondiff_argnums=(2,))
def _gdot(a, b, mode="nn"):
    return _bf16_dot(a, b, mode)


def _gdot_fwd(a, b, mode):
    return _bf16_dot(a, b, mode), (a.astype(BF16), b.astype(BF16))


def _ct_dot(p, q, mode, ct_first):
    ct, r = (p, q) if ct_first else (q, p)
    c1, c2 = _hi_lo(ct)
    dot = lambda c: lax.dot_general(*((c, r) if ct_first else (r, c)), _DN[mode], preferred_element_type=F32)
    return dot(c1) + dot(c2)


def _gdot_bwd(mode, res, ct):
    a, b = res
    if mode == "nn":
        return _ct_dot(ct, b, "nt", True), _ct_dot(a, ct, "tn", False)
    if mode == "nt":
        return _ct_dot(ct, b, "nn", True), _ct_dot(ct, a, "tn", True)
    return _ct_dot(b, ct, "nt", False), _ct_dot(a, ct, "nn", False)


_gdot.defvjp(_gdot_fwd, _gdot_bwd)


def _split_dot(ones, x):
    x1 = x.astype(BF16)
    r1 = x - x1.astype(F32)
    x2 = r1.astype(BF16)
    x3 = (r1 - x2.astype(F32)).astype(BF16)
    m = ones.astype(BF16)
    dot = lambda p: lax.dot_general(m, p, _DN["nn"], preferred_element_type=F32)
    return dot(x1) + dot(x2) + dot(x3)


@jax.custom_vjp
def _tri_cumsum(x, lower, upper):
    return _split_dot(lower, x)


def _tri_cumsum_fwd(x, lower, upper):
    return _split_dot(lower, x), (lower, upper)


def _tri_cumsum_bwd(res, ct):
    lower, upper = res
    return _split_dot(upper, ct), jnp.zeros_like(lower), jnp.zeros_like(upper)


_tri_cumsum.defvjp(_tri_cumsum_fwd, _tri_cumsum_bwd)


def _tri_masks(n):
    r = lax.broadcasted_iota(jnp.int32, (n, n), 0)
    c = lax.broadcasted_iota(jnp.int32, (n, n), 1)
    return r >= c, r > c


def _gdn_chunk(q, k, v, z, ab, a_row, dt_row, norm_w, state, sel_a, sel_b):
    C = q.shape[0]
    incl, strict = _tri_masks(C)
    lower = jnp.where(incl, 1.0, 0.0)
    qn = q * lax.rsqrt(jnp.sum(q * q, axis=-1, keepdims=True) + NORM_EPS) * (GDN_DK ** -0.5)
    kn = k * lax.rsqrt(jnp.sum(k * k, axis=-1, keepdims=True) + NORM_EPS)
    g = jnp.sum(-jnp.exp(a_row) * _softplus(ab + dt_row) * sel_a, axis=-1, keepdims=True)
    beta = jnp.sum(_sigmoid(ab) * sel_b, axis=-1, keepdims=True)
    gb = jnp.broadcast_to(g, (C, C))
    g_col = _tri_cumsum(gb, lower, jnp.where(strict, 0.0, 1.0))
    g_row = g_col.T
    g_last = jnp.sum(gb, axis=0, keepdims=True)
    gamma = jnp.where(incl, jnp.exp(jnp.where(incl, g_col - g_row, 0.0)), 0.0)
    e_col = jnp.exp(g_col)
    kb = kn * beta
    a_mat = jnp.where(strict, _gdot(kb, kn, "nt") * gamma, 0.0)
    x = jnp.concatenate([v * beta, kb * e_col], axis=1)
    pw = -a_mat
    steps = int(math.log2(C))
    for it in range(steps):
        x = x + _dot3(pw, x, "nn")
        if it < steps - 1:
            pw = _dot3(pw, pw, "nn")
    u, w = x[:, :GDN_DV], x[:, GDN_DV:]
    attn = _gdot(qn, kn, "nt") * gamma
    q_dec = qn * e_col
    k_dec = kn * jnp.exp(g_last - g_col)
    v_new = u - _gdot(w, state, "nn")
    o = _gdot(q_dec, state, "nn") + _gdot(attn, v_new, "nn")
    state_new = state * jnp.exp(jnp.broadcast_to(g_last, state.shape)) + _gdot(k_dec, v_new, "tn")
    y = _rms(o, norm_w) * _silu(z)
    return y, state_new


def _head_selectors(h):
    lane = lax.broadcasted_iota(jnp.int32, (1, LANES), 1)
    return jnp.where(lane == h, 1.0, 0.0), jnp.where(lane == h + GDN_HEADS, 1.0, 0.0)


GDN_HPS = 4
GDN_W = GDN_HPS * LANES


def _gdn_in_specs(rev, nc):
    def n_(n):
        return nc - 1 - n if rev else n
    G = GDN_HEADS // GDN_HPS
    blk = lambda off: pl.BlockSpec((CHUNK, GDN_W), lambda n, h: (n_(n), off + h))
    row = pl.BlockSpec((1, LANES), lambda n, h: (0, 0))
    return n_, [blk(0), blk(G), blk(2 * G), blk(ZIN_Z // GDN_W),
                pl.BlockSpec((CHUNK, LANES), lambda n, h: (n_(n), ZIN_AB // LANES)), row, row, row]


def _lanes(ref, j):
    return ref[:, j * LANES:(j + 1) * LANES]


def _hosting(call, hosted, *args):
    res = call(*args)
    return res if hosted is not None else (res, [])


def gdn_fwd(qkv, zin, a_row, dt_row, norm_w, hosted=None):
    S = qkv.shape[0]
    nc = S // CHUNK
    H = GDN_HEADS
    _, in_specs = _gdn_in_specs(False, nc)

    def body(q_ref, k_ref, v_ref, z_ref, ab_ref, a_ref, dt_ref, nw_ref, y_ref, st_ref, state):
        n, g = pl.program_id(0), pl.program_id(1)
        @pl.when((n == 0) & (g == 0))
        def _():
            state[...] = jnp.zeros_like(state)

        res = []
        for j in range(GDN_HPS):
            h = g * GDN_HPS + j
            st = state[h]
            sel_a, sel_b = _head_selectors(h)
            res.append((st,) + _gdn_chunk(_lanes(q_ref, j), _lanes(k_ref, j), _lanes(v_ref, j), _lanes(z_ref, j),
                                          ab_ref[...], a_ref[...], dt_ref[...], nw_ref[...], st, sel_a, sel_b))
        for j, (st, y, st_new) in enumerate(res):
            st_ref[j] = st
            y_ref[:, j * LANES:(j + 1) * LANES] = y.astype(y_ref.dtype)
            state[g * GDN_HPS + j] = st_new

    call = _pcall(body, name="gdn_fwd", grid=(nc, H // GDN_HPS), in_specs=in_specs,
                  out_specs=[pl.BlockSpec((CHUNK, GDN_W), lambda n, h: (n, h)),
                             pl.BlockSpec((GDN_HPS, None, GDN_DK, GDN_DV), lambda n, h: (h, n, 0, 0))],
                  out_shape=[jax.ShapeDtypeStruct((S, H * GDN_DV), BF16),
                             jax.ShapeDtypeStruct((H, nc, GDN_DK, GDN_DV), F32)],
                  scratch_shapes=[pltpu.VMEM((H, GDN_DK, GDN_DV), F32)],
                  dims=("arbitrary", "arbitrary"), hosted=hosted)
    return _hosting(call, hosted, qkv, qkv, qkv, zin, zin, a_row, dt_row, norm_w)


def gdn_bwd(qkv, zin, a_row, dt_row, norm_w, states, dy, dy_col0, hosted=None):
    S = qkv.shape[0]
    nc = S // CHUNK
    H = GDN_HEADS
    n_, in_specs = _gdn_in_specs(True, nc)
    assert dy_col0 % GDN_HPS == 0
    in_specs = in_specs + [pl.BlockSpec((GDN_HPS, None, GDN_DK, GDN_DV), lambda n, h: (h, n_(n), 0, 0)),
                           pl.BlockSpec((CHUNK, GDN_W), lambda n, h: (n_(n), dy_col0 // GDN_HPS + h))]

    def body(q_ref, k_ref, v_ref, z_ref, ab_ref, a_ref, dt_ref, nw_ref, st_ref, dy_ref,
             dq_ref, dk_ref, dv_ref, dz_ref, dab_ref, da_ref, ddt_ref, dnw_ref, dstate):
        n, g = pl.program_id(0), pl.program_id(1)

        @pl.when((n == 0) & (g == 0))
        def _():
            da_ref[...] = jnp.zeros_like(da_ref)
            ddt_ref[...] = jnp.zeros_like(ddt_ref)
            dnw_ref[...] = jnp.zeros_like(dnw_ref)
            dstate[...] = jnp.zeros_like(dstate)

        @pl.when(g == 0)
        def _():
            dab_ref[...] = jnp.zeros_like(dab_ref)

        res = []
        for j in range(GDN_HPS):
            h = g * GDN_HPS + j
            sel_a, sel_b = _head_selectors(h)
            _, vjp = jax.vjp(lambda *a, sa=sel_a, sb=sel_b: _gdn_chunk(*a, sa, sb), _lanes(q_ref, j), _lanes(k_ref, j),
                             _lanes(v_ref, j), _lanes(z_ref, j), ab_ref[...], a_ref[...], dt_ref[...], nw_ref[...],
                             st_ref[j])
            res.append(vjp((_lanes(dy_ref, j).astype(F32), dstate[h])))
        for j, (dq, dk, dv, dz, dab, da, ddt, dnw, dst) in enumerate(res):
            cols = slice(j * LANES, (j + 1) * LANES)
            dq_ref[:, cols] = dq
            dk_ref[:, cols] = dk
            dv_ref[:, cols] = dv
            dz_ref[:, cols] = dz.astype(dz_ref.dtype)
            dstate[g * GDN_HPS + j] = dst
        dab_ref[...] += sum(r[4] for r in res)
        da_ref[...] += sum(r[5] for r in res)
        ddt_ref[...] += sum(r[6] for r in res)
        dnw_ref[...] += sum(r[7] for r in res)

    blk = pl.BlockSpec((CHUNK, GDN_W), lambda n, h: (n_(n), h))
    row = pl.BlockSpec((1, LANES), lambda n, h: (0, 0))
    wide = jax.ShapeDtypeStruct((S, H * LANES), F32)
    call = _pcall(body, name="gdn_bwd", grid=(nc, H // GDN_HPS), in_specs=in_specs,
                  out_specs=[blk, blk, blk, blk, pl.BlockSpec((CHUNK, LANES), lambda n, h: (n_(n), 0)), row, row, row],
                  out_shape=[wide, wide, wide, jax.ShapeDtypeStruct((S, H * LANES), BF16),
                             jax.ShapeDtypeStruct((S, LANES), F32)] + [jax.ShapeDtypeStruct((1, LANES), F32)] * 3,
                  scratch_shapes=[pltpu.VMEM((H, GDN_DK, GDN_DV), F32)],
                  dims=("arbitrary", "arbitrary"), hosted=hosted)
    return _hosting(call, hosted, qkv, qkv, qkv, zin, zin, a_row, dt_row, norm_w, states, dy)


def _rope_full(x, cos, sin):
    x1, x2 = x[:, :RET_DK // 2], x[:, RET_DK // 2:]
    return jnp.concatenate([x1 * cos - x2 * sin, x2 * cos + x1 * sin], axis=1)


RET_CHUNK = 512


def _ret_chunk(q, k, v, cos, sin, lg, state):
    C = q.shape[0]
    incl, _ = _tri_masks(C)
    qr = _rope_full(q, cos, sin)
    kr = _rope_full(k, cos, sin) * (RET_DK ** -0.5)
    r = lax.broadcasted_iota(jnp.int32, (C, C), 0)
    c = lax.broadcasted_iota(jnp.int32, (C, C), 1)
    dist = jnp.where(incl, (r - c).astype(F32), 0.0)
    lg1 = lg[:, :1]
    decay = jnp.where(incl, jnp.exp(dist * lg1), 0.0)
    pos = lax.broadcasted_iota(jnp.int32, (C, 1), 0).astype(F32)
    xi = jnp.exp((pos + 1.0) * lg1)
    zeta = jnp.exp((C - 1.0 - pos) * lg1)
    inner = _bdot(_bdot(qr, kr, _DN["nt"]) * decay, v)
    cross = _bdot(qr * xi, state)
    state_new = state * jnp.exp(C * lg1) + _bdot(kr * zeta, v, _DN["tn"])
    o = inner + cross
    mu = jnp.mean(o, axis=-1, keepdims=True)
    var = jnp.mean(jnp.square(o - mu), axis=-1, keepdims=True)
    return (o - mu) * lax.rsqrt(var + NORM_EPS), state_new


def _ret_log_gamma():
    lg = np.log1p(-np.power(2.0, -5.0 - np.arange(RET_HEADS, dtype=np.float64))).astype(np.float32)
    return jnp.asarray(np.broadcast_to(lg[:, None, None], (RET_HEADS, 1, LANES)).copy())


def _ret_in_specs(rev, nc):
    def n_(n):
        return nc - 1 - n if rev else n
    H = RET_HEADS
    return n_, [pl.BlockSpec((RET_CHUNK, RET_DK),lambda n, h: (n_(n), h)),
                pl.BlockSpec((RET_CHUNK, RET_DK),lambda n, h: (n_(n), H + h)),
                pl.BlockSpec((RET_CHUNK, RET_DV),lambda n, h: (n_(n), 2 * H * RET_DK // RET_DV + h)),
                pl.BlockSpec((RET_CHUNK, LANES), lambda n, h: (n_(n), 0)),
                pl.BlockSpec((RET_CHUNK, LANES), lambda n, h: (n_(n), 0)),
                pl.BlockSpec((None, 1, LANES), lambda n, h: (h, 0, 0))]


def ret_fwd(zz, cos, sin, hosted=None):
    S = zz.shape[0]
    nc = S // RET_CHUNK
    H = RET_HEADS
    _, in_specs = _ret_in_specs(False, nc)

    def body(q_ref, k_ref, v_ref, cos_ref, sin_ref, lg_ref, o_ref, st_ref, state):
        n, h = pl.program_id(0), pl.program_id(1)

        @pl.when(n == 0)
        def _():
            state[h] = jnp.zeros((RET_DK, RET_DV), F32)

        st = state[h]
        st_ref[...] = st
        o, st_new = _ret_chunk(q_ref[...], k_ref[...], v_ref[...], cos_ref[...], sin_ref[...], lg_ref[...], st)
        o_ref[...] = o
        state[h] = st_new

    call = _pcall(body, name="ret_fwd", grid=(nc, H), in_specs=in_specs,
                  out_specs=[pl.BlockSpec((RET_CHUNK, RET_DV),lambda n, h: (n, h)),
                             pl.BlockSpec((None, None, RET_DK, RET_DV), lambda n, h: (h, n, 0, 0))],
                  out_shape=[jax.ShapeDtypeStruct((S, H * RET_DV), F32),
                             jax.ShapeDtypeStruct((H, nc, RET_DK, RET_DV), F32)],
                  scratch_shapes=[pltpu.VMEM((H, RET_DK, RET_DV), F32)],
                  dims=("arbitrary", "arbitrary"), hosted=hosted)
    return _hosting(call, hosted, zz, zz, zz, cos, sin, _ret_log_gamma())


def ret_bwd(zz, cos, sin, states, do):
    S = zz.shape[0]
    nc = S // RET_CHUNK
    H = RET_HEADS
    n_, in_specs = _ret_in_specs(True, nc)
    in_specs = in_specs + [pl.BlockSpec((None, None, RET_DK, RET_DV), lambda n, h: (h, n_(n), 0, 0)),
                           pl.BlockSpec((RET_CHUNK, RET_DV),lambda n, h: (n_(n), h))]

    def body(q_ref, k_ref, v_ref, cos_ref, sin_ref, lg_ref, st_ref, do_ref, dq_ref, dk_ref, dv_ref, dstate):
        n, h = pl.program_id(0), pl.program_id(1)

        @pl.when(n == 0)
        def _():
            dstate[h] = jnp.zeros((RET_DK, RET_DV), F32)

        cos, sin, lg = cos_ref[...], sin_ref[...], lg_ref[...]
        _, vjp = jax.vjp(lambda q, k, v, st: _ret_chunk(q, k, v, cos, sin, lg, st),
                         q_ref[...], k_ref[...], v_ref[...], st_ref[...])
        dq, dk, dv, dst = vjp((do_ref[...], dstate[h]))
        dq_ref[...] = dq.astype(dq_ref.dtype)
        dk_ref[...] = dk.astype(dk_ref.dtype)
        dv_ref[...] = dv.astype(dv_ref.dtype)
        dstate[h] = dst

    return _pcall(body, name="ret_bwd", grid=(nc, H), in_specs=in_specs,
                  out_specs=[pl.BlockSpec((RET_CHUNK, RET_DK),lambda n, h: (n_(n), h)),
                             pl.BlockSpec((RET_CHUNK, RET_DK),lambda n, h: (n_(n), h)),
                             pl.BlockSpec((RET_CHUNK, RET_DV),lambda n, h: (n_(n), h))],
                  out_shape=[jax.ShapeDtypeStruct((S, H * RET_DK), BF16), jax.ShapeDtypeStruct((S, H * RET_DK), BF16),
                             jax.ShapeDtypeStruct((S, H * RET_DV), BF16)],
                  scratch_shapes=[pltpu.VMEM((H, RET_DK, RET_DV), F32)],
                  dims=("arbitrary", "arbitrary"))(zz, zz, zz, cos, sin, _ret_log_gamma(), states, do)


MLA_SCALE = (MLA_NOPE + MLA_ROPE) ** -0.5
NEG = -1e30


def _mla_scores(q, kn, kpe, diagonal):
    s = (lax.dot_general(q[:, :LANES], kn, _DN["nt"], preferred_element_type=F32)
         + lax.dot_general(q[:, LANES:], kpe, _DN["nt"], preferred_element_type=F32)) * MLA_SCALE
    if diagonal:
        row = lax.broadcasted_iota(jnp.int32, s.shape, 0)
        col = lax.broadcasted_iota(jnp.int32, s.shape, 1)
        s = jnp.where(col <= row, s, NEG)
    return s


def _on_and_below_diagonal(i, j, step):
    @pl.when(j < i)
    def _():
        step(False)

    @pl.when(j == i)
    def _():
        step(True)


FLASH_T = 1024


def flash_fwd(qr, kv, kpe, *, t=FLASH_T, hosted=None):
    S = qr.shape[0]
    t = min(t, S)
    nb = S // t
    H = MLA_HEADS

    def body(q_ref, kn_ref, v_ref, kpe_ref, o_ref, lse_ref, m_s, l_s, acc):
        i, j = pl.program_id(1), pl.program_id(2)

        @pl.when(j == 0)
        def _():
            m_s[...] = jnp.full_like(m_s, NEG)
            l_s[...] = jnp.zeros_like(l_s)
            acc[...] = jnp.zeros_like(acc)

        def step(diagonal):
            s = _mla_scores(q_ref[...], kn_ref[...], kpe_ref[...], diagonal)
            m_new = jnp.maximum(m_s[...], jnp.max(s, axis=-1, keepdims=True))
            p = jnp.exp(s - m_new)
            alpha = jnp.exp(m_s[...] - m_new)
            l_s[...] = alpha * l_s[...] + jnp.sum(p, axis=-1, keepdims=True)
            acc[...] = alpha * acc[...] + _bdot(p, v_ref[...])
            m_s[...] = m_new

        _on_and_below_diagonal(i, j, step)

        @pl.when(j == nb - 1)
        def _():
            o_ref[...] = (acc[...] / l_s[...]).astype(o_ref.dtype)
            lse_ref[...] = m_s[...] + jnp.log(l_s[...])

    kmap = lambda off: (lambda h, i, j: (jnp.minimum(j, i), off + h))
    call = _pcall(body, name="mla_flash_fwd", grid=(H, nb, nb),
                  in_specs=[pl.BlockSpec((t, 2 * LANES), lambda h, i, j: (i, h)),
                            pl.BlockSpec((t, LANES), kmap(0)), pl.BlockSpec((t, LANES), kmap(H)),
                            pl.BlockSpec((t, LANES), lambda h, i, j: (jnp.minimum(j, i), 0))],
                  out_specs=[pl.BlockSpec((t, LANES), lambda h, i, j: (i, h)),
                             pl.BlockSpec((None, t, 1), lambda h, i, j: (h, i, 0))],
                  out_shape=[jax.ShapeDtypeStruct((S, H * MLA_V), BF16), jax.ShapeDtypeStruct((H, S, 1), F32)],
                  scratch_shapes=[pltpu.VMEM((t, 1), F32), pltpu.VMEM((t, 1), F32), pltpu.VMEM((t, MLA_V), F32)],
                  dims=("parallel", "parallel", "arbitrary"), hosted=hosted)
    return _hosting(call, hosted, qr, kv, kv, kpe)


def _mla_p_ds(q, kn, v, kpe, do, o, lse, diagonal):
    p = jnp.exp(_mla_scores(q, kn, kpe, diagonal) - lse)
    dof = do.astype(F32)
    delta = jnp.sum(dof * o.astype(F32), axis=-1, keepdims=True)
    dp = lax.dot_general(do.astype(BF16), v, _DN["nt"], preferred_element_type=F32)
    ds = p * (dp - delta) * MLA_SCALE
    return p, ds


def flash_bwd_dq(qr, kv, kpe, o, lse, dy, dy_col0, *, t=FLASH_T, hosted=None):
    S = qr.shape[0]
    t = min(t, S)
    nb = S // t
    H = MLA_HEADS

    def body(q_ref, kn_ref, v_ref, kpe_ref, o_ref, lse_ref, do_ref, dq_ref, acc):
        i, j = pl.program_id(1), pl.program_id(2)

        @pl.when(j == 0)
        def _():
            acc[...] = jnp.zeros_like(acc)

        def step(diagonal):
            _, ds = _mla_p_ds(q_ref[...], kn_ref[...], v_ref[...], kpe_ref[...], do_ref[...], o_ref[...],
                              lse_ref[...], diagonal)
            acc[...] += jnp.concatenate([_bdot(ds, kn_ref[...]), _bdot(ds, kpe_ref[...])], axis=1)

        _on_and_below_diagonal(i, j, step)

        @pl.when(j == nb - 1)
        def _():
            dq_ref[...] = acc[...]

    kmap = lambda off: (lambda h, i, j: (jnp.minimum(j, i), off + h))
    call = _pcall(body, name="mla_flash_dq", grid=(H, nb, nb),
                  in_specs=[pl.BlockSpec((t, 2 * LANES), lambda h, i, j: (i, h)),
                            pl.BlockSpec((t, LANES), kmap(0)), pl.BlockSpec((t, LANES), kmap(H)),
                            pl.BlockSpec((t, LANES), lambda h, i, j: (jnp.minimum(j, i), 0)),
                            pl.BlockSpec((t, LANES), lambda h, i, j: (i, h)),
                            pl.BlockSpec((None, t, 1), lambda h, i, j: (h, i, 0)),
                            pl.BlockSpec((t, LANES), lambda h, i, j: (i, dy_col0 + h))],
                  out_specs=pl.BlockSpec((t, 2 * LANES), lambda h, i, j: (i, h)),
                  out_shape=jax.ShapeDtypeStruct((S, H * 2 * LANES), F32),
                  scratch_shapes=[pltpu.VMEM((t, 2 * LANES), F32)],
                  dims=("parallel", "parallel", "arbitrary"), hosted=hosted)
    return _hosting(call, hosted, qr, kv, kv, kpe, o, lse, dy)


def flash_bwd_dkv(qr, kv, kpe, o, lse, dy, dy_col0, *, t=FLASH_T, hosted=None):
    S = qr.shape[0]
    t = min(t, S)
    nb = S // t
    H = MLA_HEADS

    def body(q_ref, kn_ref, v_ref, kpe_ref, o_ref, lse_ref, do_ref, dkn_ref, dv_ref, dkpe_ref, akn, av):
        j, h, i = pl.program_id(0), pl.program_id(1), pl.program_id(2)

        @pl.when(i == 0)
        def _():
            akn[...] = jnp.zeros_like(akn)
            av[...] = jnp.zeros_like(av)

        @pl.when((i == 0) & (h == 0))
        def _():
            dkpe_ref[...] = jnp.zeros_like(dkpe_ref)

        def step(diagonal):
            q = q_ref[...]
            p, ds = _mla_p_ds(q, kn_ref[...], v_ref[...], kpe_ref[...], do_ref[...], o_ref[...], lse_ref[...],
                              diagonal)
            av[...] += _bdot(p, do_ref[...], _DN["tn"])
            akn[...] += _bdot(ds, q[:, :LANES], _DN["tn"])
            dkpe_ref[...] += _bdot(ds, q[:, LANES:], _DN["tn"])

        _on_and_below_diagonal(i, j, step)

        @pl.when(i == nb - 1)
        def _():
            dkn_ref[...] = akn[...].astype(dkn_ref.dtype)
            dv_ref[...] = av[...].astype(dv_ref.dtype)

    qmap = lambda off: (lambda j, h, i: (jnp.maximum(i, j), off + h))
    call = _pcall(
        body, name="mla_flash_dkv", grid=(nb, H, nb),
        in_specs=[pl.BlockSpec((t, 2 * LANES), qmap(0)),
                  pl.BlockSpec((t, LANES), lambda j, h, i: (j, h)), pl.BlockSpec((t, LANES), lambda j, h, i: (j, H + h)),
                  pl.BlockSpec((t, LANES), lambda j, h, i: (j, 0)),
                  pl.BlockSpec((t, LANES), qmap(0)),
                  pl.BlockSpec((None, t, 1), lambda j, h, i: (h, jnp.maximum(i, j), 0)),
                  pl.BlockSpec((t, LANES), qmap(dy_col0))],
        out_specs=[pl.BlockSpec((t, LANES), lambda j, h, i: (j, h)), pl.BlockSpec((t, LANES), lambda j, h, i: (j, h)),
                   pl.BlockSpec((t, LANES), lambda j, h, i: (j, 0))],
        out_shape=[jax.ShapeDtypeStruct((S, H * LANES), BF16), jax.ShapeDtypeStruct((S, H * LANES), BF16),
                   jax.ShapeDtypeStruct((S, LANES), F32)],
        scratch_shapes=[pltpu.VMEM((t, LANES), F32), pltpu.VMEM((t, LANES), F32)],
        dims=("arbitrary", "arbitrary", "arbitrary"), hosted=hosted)
    (dkn, dv, dkpe), extra = _hosting(call, hosted, qr, kv, kv, kpe, o, lse, dy)
    return (jnp.concatenate([dkn, dv], axis=1), dkpe), extra


def loss_head(h, target, g, *, tr=256):
    S, D = h.shape
    tr = min(tr, S)

    def body(h_ref, t_ref, g_ref, loss_ref, dh_ref, dg_ref):
        tgt = t_ref[...]

        def f(hh, gg):
            err = jnp.square(_rms(hh, gg) - tgt)
            per_row = jnp.sum(err, axis=-1, keepdims=True) * (0.5 / D)
            return jnp.sum(per_row, axis=0, keepdims=True)

        val, vjp = jax.vjp(f, h_ref[...], g_ref[...])
        dh, dg = vjp(jnp.ones((1, 1), F32))
        dh_ref[...] = dh

        @pl.when(pl.program_id(0) == 0)
        def _():
            loss_ref[...] = jnp.zeros_like(loss_ref)
            dg_ref[...] = jnp.zeros_like(dg_ref)

        loss_ref[...] += jnp.broadcast_to(val, loss_ref.shape)
        dg_ref[...] += dg

    return _pcall(body, name="loss_head", grid=(S // tr,),
                  in_specs=[_row_spec(tr, D, 0), _row_spec(tr, D, 0), _full_spec(g)],
                  out_specs=[pl.BlockSpec((1, LANES), lambda i: (0, 0)), _row_spec(tr, D, 0), _full_spec(g)],
                  out_shape=[jax.ShapeDtypeStruct((1, LANES), F32), jax.ShapeDtypeStruct((S, D), F32),
                             jax.ShapeDtypeStruct(g.shape, F32)],
                  dims=("arbitrary",))(h, target, g)


def _rope_tables(positions, dim):
    inv_freq = ROPE_THETA ** (-jnp.arange(0, dim, 2, dtype=F32) / dim)
    ang = positions.astype(F32)[:, None] * inv_freq
    return jnp.cos(ang), jnp.sin(ang)


def _pad_cols(w, n):
    return jnp.pad(w, ((0, 0), (0, n - w.shape[1])))


def _prep_w_in0(w):
    return jnp.concatenate([w[:, :4096], w[:, 4112:5136], _pad_cols(w[:, 5136:5200], LANES),
                            _pad_cols(w[:, 4096:4112], LANES)], axis=1)


def _unprep_w_in0(g):
    return jnp.concatenate([g[:, :4096], g[:, ZIN_AB:ZIN_AB + 16], g[:, ZIN_CQ:ZIN_KR], g[:, ZIN_KR:ZIN_KR + MLA_ROPE]],
                           axis=1)


def _prep_w_uq(w):
    w = w.reshape(MLA_Q_RANK, MLA_HEADS, MLA_NOPE + MLA_ROPE)
    w = jnp.pad(w, ((0, 0), (0, 0), (0, 2 * LANES - MLA_NOPE - MLA_ROPE)))
    return w.reshape(MLA_Q_RANK, MLA_HEADS * 2 * LANES)


def _unprep_w_uq(g):
    g = g.reshape(MLA_Q_RANK, MLA_HEADS, 2 * LANES)[:, :, :MLA_NOPE + MLA_ROPE]
    return g.reshape(MLA_Q_RANK, MLA_HEADS * (MLA_NOPE + MLA_ROPE))


def _prep_w_ukv(w):
    w = w.reshape(MLA_KV_RANK, MLA_HEADS, 2, LANES)
    return jnp.transpose(w, (0, 2, 1, 3)).reshape(MLA_KV_RANK, 2 * MLA_HEADS * LANES)


def _unprep_w_ukv(g):
    g = g.reshape(MLA_KV_RANK, 2, MLA_HEADS, LANES)
    return jnp.transpose(g, (0, 2, 1, 3)).reshape(MLA_KV_RANK, 2 * MLA_HEADS * LANES)


def _row(v, n=None):
    v = v.reshape(1, -1).astype(F32)
    return v if n is None else _pad_cols(v, n)


def _ffn_fwd(h, norm_g, w_up, conv_w, conv_b, w_down, tag, hosted=None):
    (hn,) = rowwise(_fn_rms, [(h, D_MODEL, 0)], [norm_g], [], [(D_MODEL, BF16)], name=f"{tag}_ffn_norm")
    u = matmul(hn, w_up, "nn", F32, b_shards=N_CHIPS, **TILES["wide_nn"], name=f"{tag}_ffn_up", hosted=hosted)
    u, got = u if hosted is not None else (u, [])
    f = ffn_conv_fwd(u, conv_w, conv_b)
    h_out = matmul(f, w_down, "nn", F32, add=h, tm=512, tn=1024, tk=8192, name=f"{tag}_ffn_down")
    return h_out, (hn, u, f), got


def _ffn_bwd(dh, dh16, h, norm_g, w_up, conv_w, conv_b, w_down, saved, tag):
    hn, u, f = saved
    df = matmul(dh16, w_down, "nt", BF16, tm=512, tn=2816, name=f"{tag}_ffn_down_dx")
    g_down = matmul(f, dh16, "tn", BF16, **TILES["dw"], name=f"{tag}_ffn_down_dw")
    dc, g_conv_w, g_conv_b = ffn_conv_bwd(u, conv_w, conv_b, df)
    du = conv_transpose(dc, conv_w, BF16, name=f"{tag}_ffn_conv_dx")
    g_up = matmul(hn, du, "tn", BF16, out_shards=N_CHIPS, tm=1024, tn=1408, tk=4096, name=f"{tag}_ffn_up_dw")
    dhn = matmul(du, w_up, "nt", F32, b_shards=N_CHIPS, tm=512, tn=2048, tk=2816, name=f"{tag}_ffn_up_dx")
    ((dh_in, dh_in16),), (g_norm,) = rowwise_bwd(_fn_rms, [(h, D_MODEL, 0)], [norm_g], [], [(dhn, D_MODEL, 0)],
                                                 [(F32, BF16)], adds=[(dh, D_MODEL, 0)], name=f"{tag}_ffn_norm_bwd")
    return dh_in, dh_in16, dict(ffn_norm=g_norm, ffn_w_up=g_up, ffn_conv_w=g_conv_w, ffn_conv_b=g_conv_b,
                                ffn_w_down=g_down)


TILES = {"wide_nn": dict(tm=512, tn=3072, tk=2048),
         "square": dict(tm=512, tn=2048, tk=2048),
         "dw": dict(tm=512, tn=2048, tk=4096)}


def _ple_fwd(h, p_i, w_proj, gate_g, w_gate, tag):
    (hg,) = rowwise(_fn_rms, [(h, D_MODEL, 0)], [gate_g], [], [(D_MODEL, BF16)], name=f"{tag}_ple_norm")
    gl = matmul(hg, w_gate, "nn", F32, **TILES["square"], name=f"{tag}_ple_gate")
    pp = matmul(p_i, w_proj, "nn", F32, b_shards=N_CHIPS, name=f"{tag}_ple_proj")
    (h_out,) = rowwise(_fn_ple, [(h, D_MODEL, 0), (pp, D_MODEL, 0), (gl, D_MODEL, 0)], [], [], [(D_MODEL, F32)],
                       name=f"{tag}_ple_add")
    return h_out, (hg, gl, pp)


def _ple_bwd(dh, h, p_i, w_proj, gate_g, w_gate, saved, tag):
    hg, gl, pp = saved
    (dpp, dgl), _ = rowwise_bwd(_fn_ple_terms, [(pp, D_MODEL, 0), (gl, D_MODEL, 0)], [], [], [(dh, D_MODEL, 0)],
                                [BF16, BF16], name=f"{tag}_ple_add_bwd")
    g_proj = matmul(p_i, dpp, "tn", BF16, out_shards=N_CHIPS, name=f"{tag}_ple_proj_dw")
    g_gate = matmul(hg, dgl, "tn", BF16, **TILES["dw"], name=f"{tag}_ple_gate_dw")
    dhg = matmul(dgl, w_gate, "nt", F32, **TILES["square"], name=f"{tag}_ple_gate_dx")
    ((dh_in, dh_in16),), (g_norm,) = rowwise_bwd(_fn_rms, [(h, D_MODEL, 0)], [gate_g], [], [(dhg, D_MODEL, 0)],
                                                 [(F32, BF16)], adds=[(dh, D_MODEL, 0)], name=f"{tag}_ple_norm_bwd")
    return dh_in, dh_in16, dict(ple_proj=g_proj, ple_gate_norm=g_norm, ple_gate=g_gate)


def local_step(x, p, positions, target, slots, W, ex=None):
    S = x.shape[0]
    G = {}
    p0, p1 = p[0].astype(BF16), p[1].astype(BF16)
    W = dict(W)

    def use(names, bufs):
        for k, b in zip(names, bufs):
            r, c = b.shape[1:]
            W[k] = b.reshape(N_CHIPS * r, c) if k in ROW_SHARDED else (b if k in KEPT_SHARDED else _cols_to_full(b))

    def by_chip(names):
        out = []
        for k in names:
            r, c = slots[k].shape[1:]
            out.append(G[k].reshape(N_CHIPS, r, c) if k in ROW_SHARDED
                       else (G[k] if k in KEPT_SHARDED else _full_to_cols(G[k])))
        return out

    first = [slots[k] for k in GATHER_FIRST]
    use(GATHER_FIRST, ex.gather(first) if ex else first)

    cm, sm = _rope_tables(positions, MLA_ROPE)
    zeros = jnp.zeros((S, LANES - MLA_ROPE), F32)
    cosp = jnp.concatenate([cm, cm, zeros], axis=1)
    sinp = jnp.concatenate([sm, sm, zeros], axis=1)
    cr, sr = _rope_tables(positions, RET_DK)

    w_in0 = _prep_w_in0(W["l0_w_in"])
    w_uq = _prep_w_uq(W["l0_mla_w_uq"])
    w_ukv = _prep_w_ukv(W["l0_mla_w_ukv"])
    a_row = _row(W["l0_gdn_A_log"], LANES)
    dt_row = _row(W["l0_gdn_dt_bias"], LANES)
    gdn_nw = _row(W["l0_gdn_norm"])
    n = {k: _row(W[k]) for k in ("l0_attn_norm", "l0_mla_q_norm", "l0_mla_kv_norm", "l0_ffn_norm",
                                 "l0_ple_gate_norm", "l1_attn_norm", "l1_ret_norm", "l1_ffn_norm",
                                 "l1_ple_gate_norm", "final_norm", "l0_ffn_conv_b", "l1_ffn_conv_b")}

    (hn0,) = rowwise(_fn_rms, [(x, D_MODEL, 0)], [n["l0_attn_norm"]], [], [(D_MODEL, BF16)], name="l0_attn_norm")
    zin = matmul(hn0, w_in0, "nn", F32, tm=512, tn=1792, name="l0_w_in")
    qkv = gdn_conv_fwd(zin, W["l0_gdn_conv"])
    layer0 = [slots[k] for k in GATHER_L0]
    (y_a, gdn_states), got = gdn_fwd(qkv, zin, a_row, dt_row, gdn_nw, hosted=hosted_gather(layer0) if ex else None)
    use(GATHER_L0, got if ex else layer0)
    mla_rows = [(zin, MLA_Q_RANK, ZIN_CQ // MLA_Q_RANK), (zin, MLA_KV_RANK, ZIN_CKV // MLA_KV_RANK),
                (zin, LANES, ZIN_KR // LANES)]
    mla_nd = [(cosp, LANES, 0), (sinp, LANES, 0)]
    cqn, ckvn, kpe = rowwise(_fn_mla_pre, mla_rows, [n["l0_mla_q_norm"], n["l0_mla_kv_norm"]], mla_nd,
                             [(MLA_Q_RANK, BF16), (MLA_KV_RANK, BF16), (LANES, BF16)], name="mla_pre")
    q_lin = matmul(cqn, w_uq, "nn", F32, name="mla_w_uq")
    kv = matmul(ckvn, w_ukv, "nn", BF16, name="mla_w_ukv")
    (qr,) = rowwise(_fn_rope_q, [(q_lin, 2048, 0)], [], mla_nd, [(2048, BF16)],
                    name="mla_rope_q")
    layer1 = [slots[k] for k in GATHER_L1]
    (y_b, lse), got = flash_fwd(qr, kv, kpe, hosted=hosted_gather(layer1) if ex else None)
    use(GATHER_L1, got if ex else layer1)
    y_ab = jnp.concatenate([y_a, y_b], axis=1)
    h1 = matmul(y_ab, W["l0_w_out"], "nn", F32, add=x, **TILES["square"], name="l0_w_out")
    ffn_late = [slots[k] for k in GATHER_FFN]
    h2, ffn0, got = _ffn_fwd(h1, n["l0_ffn_norm"], W["l0_ffn_w_up"], W["l0_ffn_conv_w"], n["l0_ffn_conv_b"],
                             W["l0_ffn_w_down"], "l0", hosted=hosted_gather(ffn_late) if ex else None)
    use(GATHER_FFN, got if ex else ffn_late)
    h3, ple0 = _ple_fwd(h2, p0, W["l0_ple_proj"], n["l0_ple_gate_norm"], W["l0_ple_gate"], "l0")

    (hn1,) = rowwise(_fn_rms, [(h3, D_MODEL, 0)], [n["l1_attn_norm"]], [], [(D_MODEL, BF16)], name="l1_attn_norm")
    late = [slots[k] for k in GATHER_L1_IN]
    zz = matmul(hn1, W["l1_w_in"], "nn", F32, b_shards=N_CHIPS, **TILES["wide_nn"], name="l1_w_in",
                hosted=hosted_gather(late) if ex else None)
    zz, got = zz if ex else (zz, late)
    use(GATHER_L1_IN, got)
    (o_ret, ret_states), _ = ret_fwd(zz, cr, sr)
    gate_rows = [(zz, 4096, 2), (o_ret, 4096, 0)]
    (yg,) = rowwise(_fn_ret_gate, gate_rows, [n["l1_ret_norm"]], [], [(4096, BF16)], name="ret_gate")
    h4 = matmul(yg, W["l1_w_out"], "nn", F32, add=h3, tm=512, tn=2048, tk=4096, name="l1_w_out")
    h5, ffn1, _ = _ffn_fwd(h4, n["l1_ffn_norm"], W["l1_ffn_w_up"], W["l1_ffn_conv_w"], n["l1_ffn_conv_b"],
                           W["l1_ffn_w_down"], "l1")
    h6, ple1 = _ple_fwd(h5, p1, W["l1_ple_proj"], n["l1_ple_gate_norm"], W["l1_ple_gate"], "l1")

    loss_vec, dh, G["final_norm"] = loss_head(h6, target, n["final_norm"])

    dh, dh16, g = _ple_bwd(dh, h5, p1, W["l1_ple_proj"], n["l1_ple_gate_norm"], W["l1_ple_gate"], ple1, "l1")
    G.update({"l1_" + k: v for k, v in g.items()})
    dh, dh16, g = _ffn_bwd(dh, dh16, h4, n["l1_ffn_norm"], W["l1_ffn_w_up"], W["l1_ffn_conv_w"], n["l1_ffn_conv_b"],
                           W["l1_ffn_w_down"], ffn1, "l1")
    G.update({"l1_" + k: v for k, v in g.items()})

    dyg = matmul(dh16, W["l1_w_out"], "nt", F32, tm=512, tn=4096, name="l1_w_out_dx")
    G["l1_w_out"] = matmul(yg, dh16, "tn", BF16, **TILES["dw"], name="l1_w_out_dw")
    (dg, do_ret), (G["l1_ret_norm"],) = rowwise_bwd(_fn_ret_gate, gate_rows, [n["l1_ret_norm"]], [],
                                                   [(dyg, 4096, 0)], [BF16, F32], name="ret_gate_bwd")
    dq, dk, dv = ret_bwd(zz, cr, sr, ret_states, do_ret)
    dzz = jnp.concatenate([dq, dk, dv, dg], axis=1)
    G["l1_w_in"] = matmul(hn1, dzz, "tn", BF16, out_shards=N_CHIPS, tm=1024, tn=1536, tk=4096, name="l1_w_in_dw")
    dhn = matmul(dzz, W["l1_w_in"], "nt", F32, b_shards=N_CHIPS, tm=1024, tn=1024, tk=3072, name="l1_w_in_dx")
    (dh,), (G["l1_attn_norm"],) = rowwise_bwd(_fn_rms, [(h3, D_MODEL, 0)], [n["l1_attn_norm"]], [],
                                             [(dhn, D_MODEL, 0)], [F32], adds=[(dh, D_MODEL, 0)],
                                             name="l1_attn_norm_bwd")

    dh, dh16, g = _ple_bwd(dh, h2, p0, W["l0_ple_proj"], n["l0_ple_gate_norm"], W["l0_ple_gate"], ple0, "l0")
    G.update({"l0_" + k: v for k, v in g.items()})
    dh, dh16, g = _ffn_bwd(dh, dh16, h1, n["l0_ffn_norm"], W["l0_ffn_w_up"], W["l0_ffn_conv_w"], n["l0_ffn_conv_b"],
                           W["l0_ffn_w_down"], ffn0, "l0")
    G.update({"l0_" + k: v for k, v in g.items()})

    dy_ab = matmul(dh16, W["l0_w_out"], "nt", F32, **TILES["square"], name="l0_w_out_dx")
    G["l0_w_out"] = matmul(y_ab, dh16, "tn", BF16, **TILES["dw"], name="l0_w_out_dw")
    sums, landed = {}, {}
    if ex:
        sums.update(zip(REDUCE_L1, ex.pair_sums(REDUCE_L1, by_chip(REDUCE_L1))))
        late = REDUCE_DQ + REDUCE_L0
        sums.update(zip(late, ex.pair_sums(late, by_chip(late))))
    (dq, dk, dv, dz, dab, g_a, g_dt, G["l0_gdn_norm"]), got = gdn_bwd(
        qkv, zin, a_row, dt_row, gdn_nw, gdn_states, dy_ab, 0,
        hosted=hosted_scatter([sums[k] for k in REDUCE_L1]) if ex else None)
    landed.update(zip(REDUCE_L1, got))
    G["l0_gdn_A_log"], G["l0_gdn_dt_bias"] = g_a[:, :GDN_HEADS], g_dt[:, :GDN_HEADS]
    dpre, G["l0_gdn_conv"] = gdn_conv_bwd(zin, W["l0_gdn_conv"], jnp.concatenate([dq, dk, dv], axis=1))
    dqkv = conv_transpose(dpre, W["l0_gdn_conv"], BF16, name="gdn_conv_dx")
    dqr, got = flash_bwd_dq(qr, kv, kpe, y_b, lse, dy_ab, MLA_HEADS,
                            hosted=hosted_scatter([sums[k] for k in REDUCE_DQ]) if ex else None)
    landed.update(zip(REDUCE_DQ, got))
    (dkv, dkpe), got = flash_bwd_dkv(qr, kv, kpe, y_b, lse, dy_ab, MLA_HEADS,
                                     hosted=hosted_scatter([sums[k] for k in REDUCE_L0]) if ex else None)
    landed.update(zip(REDUCE_L0, got))
    (dq_lin,), _ = rowwise_bwd(_fn_rope_q, [(q_lin, 2048, 0)], [], mla_nd, [(dqr, 2048, 0)], [BF16],
                               name="mla_rope_q_bwd")
    G["l0_mla_w_uq"] = _unprep_w_uq(matmul(cqn, dq_lin, "tn", BF16, name="mla_w_uq_dw"))
    dcqn = matmul(dq_lin, w_uq, "nt", F32, name="mla_w_uq_dx")
    G["l0_mla_w_ukv"] = _unprep_w_ukv(matmul(ckvn, dkv, "tn", BF16, name="mla_w_ukv_dw"))
    dckvn = matmul(dkv, w_ukv, "nt", F32, name="mla_w_ukv_dx")
    (dcq, dckv, dkr), (G["l0_mla_q_norm"], G["l0_mla_kv_norm"]) = rowwise_bwd(
        _fn_mla_pre, mla_rows, [n["l0_mla_q_norm"], n["l0_mla_kv_norm"]], mla_nd,
        [(dcqn, MLA_Q_RANK, 0), (dckvn, MLA_KV_RANK, 0), (dkpe, LANES, 0)], [BF16, BF16, BF16], name="mla_pre_bwd")
    dzin = jnp.concatenate([dqkv, dz, dcq, dckv, dkr, dab.astype(BF16)], axis=1)
    G["l0_w_in"] = _unprep_w_in0(matmul(hn0, dzin, "tn", BF16, tm=512, tn=1792, tk=4096, name="l0_w_in_dw"))
    dhn = matmul(dzin, w_in0, "nt", F32, tm=512, tn=2048, tk=5376, name="l0_w_in_dx")
    (grad_x,), (G["l0_attn_norm"],) = rowwise_bwd(_fn_rms, [(x, D_MODEL, 0)], [n["l0_attn_norm"]], [],
                                                 [(dhn, D_MODEL, 0)], [F32], adds=[(dh, D_MODEL, 0)],
                                                 name="l0_attn_norm_bwd")
    small = {k: G[k] for k in SMALL}
    if not ex:
        return loss_vec[0, 0], grad_x, dict(zip(BIG, by_chip(BIG))), small
    sums.update(zip(REDUCE_LAST, ex.pair_sums(REDUCE_LAST, by_chip(REDUCE_LAST))))
    landed.update(zip(REDUCE_LAST, scatter_chips([sums[k] for k in REDUCE_LAST])))
    return loss_vec[0, 0], grad_x, ex.finish(sums, landed), small


HBM = pl.BlockSpec(memory_space=pltpu.HBM)
VMEM = pl.BlockSpec(memory_space=pltpu.VMEM)


def _place():
    return lax.axis_index("x"), lax.axis_index("y"), lax.axis_index("c")


def _other_chips(x, y):
    return [(1 - x, y), (x, 1 - y), (1 - x, 1 - y)]


def _comm_call(body, *, name, out_shape, in_specs, out_specs, scratch_shapes):
    return pl.pallas_call(body, name=name, out_shape=out_shape, in_specs=in_specs, out_specs=out_specs,
                          scratch_shapes=list(scratch_shapes),
                          compiler_params=pltpu.CompilerParams(vmem_limit_bytes=VMEM_LIMIT_MB << 20))


def _inplace_comm_call(body, bufs, *, name, n_sems):
    n = len(bufs)
    return pl.pallas_call(body, name=name, out_shape=[jax.ShapeDtypeStruct(b.shape, b.dtype) for b in bufs],
                          in_specs=[HBM] * n, out_specs=[HBM] * n, input_output_aliases={i: i for i in range(n)},
                          scratch_shapes=[pltpu.SemaphoreType.DMA((n_sems,)), pltpu.SemaphoreType.DMA((n_sems,))],
                          compiler_params=pltpu.CompilerParams(vmem_limit_bytes=VMEM_LIMIT_MB << 20))(*bufs)


def all_gather_chips(bufs):
    n_sems, start, finish = _gather_phase(len(bufs))
    n = len(bufs)

    def body(*refs):
        outs, send_sems, recv_sems = refs[n:2 * n], refs[2 * n], refs[2 * n + 1]
        start(None, outs, send_sems, recv_sems)
        finish(None, outs, send_sems, recv_sems)

    return _inplace_comm_call(body, bufs, name="all_gather_chips", n_sems=n_sems)


def _gather_phase(n):
    def plan(outs, send_sems, recv_sems):
        x, y, c = _place()

        def copy(w, k, chip, hc, to):
            half = outs[w].shape[1] // 2
            rows = outs[w].at[2 * chip[0] + chip[1], pl.ds(hc * half, half), :]
            return pltpu.make_async_remote_copy(src_ref=rows, dst_ref=rows, send_sem=send_sems.at[6 * w + k],
                                                recv_sem=recv_sems.at[6 * w + k], device_id=to, device_id_type=MESH)

        first = [[copy(w, k, (x, y), c, (*chip, c)) for k, chip in enumerate(_other_chips(x, y))] for w in range(n)]
        passed = [[copy(w, 3 + k, chip, c, (x, y, 1 - c)) for k, chip in enumerate(_other_chips(x, y))]
                  for w in range(n)]
        return copy, first, passed, (x, y, c)

    def start(_, outs, send_sems, recv_sems):
        _, first, _, _ = plan(outs, send_sems, recv_sems)
        for w in range(n):
            for cp in first[w]:
                cp.start()

    def finish(_, outs, send_sems, recv_sems):
        copy, first, passed, (x, y, c) = plan(outs, send_sems, recv_sems)
        chips = _other_chips(x, y)
        for w in range(n):
            for k, chip in enumerate(chips):
                copy(w, k, chip, c, (x, y, c)).wait_recv()
                passed[w][k].start()
        for w in range(n):
            for k, chip in enumerate(chips):
                copy(w, 3 + k, chip, 1 - c, (x, y, c)).wait_recv()
        for w in range(n):
            for cp in first[w] + passed[w]:
                cp.wait_send()

    return 6 * n, start, finish


def hosted_gather(bufs):
    n_sems, start, finish = _gather_phase(len(bufs))
    return Hosted(bufs, [jax.ShapeDtypeStruct(b.shape, b.dtype) for b in bufs], {i: i for i in range(len(bufs))},
                  n_sems, start, finish)


def pair_swap_halves(gs):
    n = len(gs)

    def body(*refs):
        g_refs, o_refs, send_sems, recv_sems = refs[:n], refs[n:2 * n], refs[2 * n], refs[2 * n + 1]
        x, y, c = _place()
        copies = []
        for w in range(n):
            half = g_refs[w].shape[1] // 2
            copies.append(pltpu.make_async_remote_copy(
                src_ref=g_refs[w].at[:, pl.ds((1 - c) * half, half), :], dst_ref=o_refs[w], send_sem=send_sems.at[w],
                recv_sem=recv_sems.at[w], device_id=(x, y, 1 - c), device_id_type=MESH))
        for cp in copies:
            cp.start()
        for cp in copies:
            cp.wait()

    return _comm_call(body, name="pair_swap_halves",
                      out_shape=[jax.ShapeDtypeStruct((N_CHIPS, g.shape[1] // 2, g.shape[2]), g.dtype) for g in gs],
                      in_specs=[HBM] * n, out_specs=[HBM] * n,
                      scratch_shapes=[pltpu.SemaphoreType.DMA((n,)), pltpu.SemaphoreType.DMA((n,))])(*gs)


def scatter_chips(ps):
    n = len(ps)
    n_sems, start, finish = _scatter_phase(n)

    def body(*refs):
        p_refs, o_refs, send_sems, recv_sems = refs[:n], refs[n:2 * n], refs[2 * n], refs[2 * n + 1]
        start(p_refs, o_refs, send_sems, recv_sems)
        finish(p_refs, o_refs, send_sems, recv_sems)

    return _comm_call(body, name="scatter_chips", out_shape=_scatter_shapes(ps), in_specs=[HBM] * n, out_specs=[HBM] * n,
                      scratch_shapes=[pltpu.SemaphoreType.DMA((n_sems,)), pltpu.SemaphoreType.DMA((n_sems,))])(*ps)


def _scatter_shapes(ps):
    return [jax.ShapeDtypeStruct((3,) + p.shape[1:], p.dtype) for p in ps]


def _scatter_phase(n):
    def copies(p_refs, o_refs, send_sems, recv_sems):
        x, y, c = _place()
        return [pltpu.make_async_remote_copy(src_ref=p_refs[w].at[2 * chip[0] + chip[1]], dst_ref=o_refs[w].at[k],
                                             send_sem=send_sems.at[3 * w + k], recv_sem=recv_sems.at[3 * w + k],
                                             device_id=(*chip, c), device_id_type=MESH)
                for w in range(n) for k, chip in enumerate(_other_chips(x, y))]

    def start(*refs):
        for cp in copies(*refs):
            cp.start()

    def finish(*refs):
        for cp in copies(*refs):
            cp.wait()

    return 3 * n, start, finish


def hosted_scatter(ps):
    n_sems, start, finish = _scatter_phase(len(ps))
    return Hosted(ps, _scatter_shapes(ps), {}, n_sems, start, finish)


def pair_join_halves(rs):
    n = len(rs)

    def body(*refs):
        outs, send_sems, recv_sems = refs[n:2 * n], refs[2 * n], refs[2 * n + 1]
        x, y, c = _place()
        copies = []
        for w in range(n):
            half = outs[w].shape[0] // 2
            rows = outs[w].at[pl.ds(c * half, half), :]
            copies.append(pltpu.make_async_remote_copy(src_ref=rows, dst_ref=rows, send_sem=send_sems.at[w],
                                                       recv_sem=recv_sems.at[w], device_id=(x, y, 1 - c),
                                                       device_id_type=MESH))
        for cp in copies:
            cp.start()
        for cp in copies:
            cp.wait()

    return _inplace_comm_call(body, rs, name="pair_join_halves", n_sems=n)


def all_reduce_small(v, name):
    n, L = v.shape
    n_dev = 8

    def body(v_ref, out_ref, buf, send_sems, recv_sems):
        x, y, c = _place()
        me = 4 * x + 2 * y + c
        buf[me] = v_ref[...]

        def copy(k, slot, peer):
            return pltpu.make_async_remote_copy(src_ref=v_ref, dst_ref=buf.at[slot], send_sem=send_sems.at[k],
                                                recv_sem=recv_sems.at[slot],
                                                device_id=(peer // 4, (peer // 2) % 2, peer % 2), device_id_type=MESH)

        sends = [copy(k - 1, me, (me + k) % n_dev) for k in range(1, n_dev)]
        for cp in sends:
            cp.start()
        for k in range(1, n_dev):
            src = (me + k) % n_dev
            copy(0, src, src).wait_recv()
        for cp in sends:
            cp.wait_send()
        acc = buf[0]
        for s in range(1, n_dev):
            acc = acc + buf[s]
        out_ref[...] = acc

    return _comm_call(body, name=name, out_shape=jax.ShapeDtypeStruct((n, L), v.dtype), in_specs=[VMEM], out_specs=VMEM,
                      scratch_shapes=[pltpu.VMEM((n_dev, n, L), v.dtype), pltpu.SemaphoreType.DMA((n_dev - 1,)),
                                      pltpu.SemaphoreType.DMA((n_dev,))])(v)


BF16_ROWS = 16


def _rows_tile(n, row_bytes, budget=1 << 20, mult=SUBLANES):
    best = mult if n % mult == 0 else n
    for t in range(mult, n + 1, mult):
        if n % t == 0 and t * row_bytes <= budget:
            best = t
    return best


def _scalars(*vals):
    return jnp.stack([jnp.asarray(v, jnp.int32) for v in vals])


def cast_to_slot(w, chip, name):
    r, c = w.shape
    tb = _rows_tile(r, c * 4, mult=BF16_ROWS)

    def body(s_ref, w_ref, o_ref):
        o_ref[...] = w_ref[...].astype(BF16)

    spec = pltpu.PrefetchScalarGridSpec(
        num_scalar_prefetch=1, grid=(r // tb,), in_specs=[pl.BlockSpec((tb, c), lambda i, s: (i, 0))],
        out_specs=pl.BlockSpec((None, tb, c), lambda i, s: (s[0], i, 0)))
    return pl.pallas_call(body, name=name, grid_spec=spec, out_shape=jax.ShapeDtypeStruct((N_CHIPS, r, c), BF16),
                          compiler_params=pltpu.CompilerParams(dimension_semantics=("parallel",)))(_scalars(chip), w)


def pair_add(g, got, c, name):
    _, r, w = g.shape
    half = r // 2
    tb = _rows_tile(half, w * 4, mult=BF16_ROWS)
    nb = half // tb

    def body(c_ref, g_ref, got_ref, o_ref):
        o_ref[...] = (g_ref[...].astype(F32) + got_ref[...].astype(F32)).astype(o_ref.dtype)

    spec = pltpu.PrefetchScalarGridSpec(
        num_scalar_prefetch=1, grid=(N_CHIPS, nb),
        in_specs=[pl.BlockSpec((None, tb, w), lambda s, i, c_ref: (s, c_ref[0] * nb + i, 0)),
                  pl.BlockSpec((None, tb, w), lambda s, i, c_ref: (s, i, 0))],
        out_specs=pl.BlockSpec((None, tb, w), lambda s, i, c_ref: (s, i, 0)))
    return pl.pallas_call(body, name=name, grid_spec=spec, out_shape=jax.ShapeDtypeStruct((N_CHIPS, half, w), BF16),
                          compiler_params=pltpu.CompilerParams(dimension_semantics=("parallel", "parallel")))(
        _scalars(c), g, got)


def chip_add(p, got, chip, c, name):
    _, h, w = p.shape
    tb = _rows_tile(h, w * 4, mult=BF16_ROWS)
    nb = h // tb

    def body(s_ref, p_ref, got_ref, o_ref):
        acc = p_ref[...].astype(F32)
        for k in range(3):
            acc = acc + got_ref[k].astype(F32)
        o_ref[...] = acc

    spec = pltpu.PrefetchScalarGridSpec(
        num_scalar_prefetch=1, grid=(nb,),
        in_specs=[pl.BlockSpec((None, tb, w), lambda i, s: (s[0], i, 0)),
                  pl.BlockSpec((3, tb, w), lambda i, s: (0, i, 0))],
        out_specs=pl.BlockSpec((tb, w), lambda i, s: (s[1] * nb + i, 0)))
    return pl.pallas_call(body, name=name, grid_spec=spec, out_shape=jax.ShapeDtypeStruct((2 * h, w), F32),
                          compiler_params=pltpu.CompilerParams(dimension_semantics=("parallel",)))(
        _scalars(chip, c), p, got)


def adamw(w, g, m, v, name):
    r, c = w.shape
    tr = _rows_tile(r, c * 4)

    def body(w_ref, g_ref, m_ref, v_ref, d_ref, m_out, v_out):
        gg = g_ref[...]
        m2 = ADAM_B1 * m_ref[...] + (1.0 - ADAM_B1) * gg
        v2 = ADAM_B2 * v_ref[...] + (1.0 - ADAM_B2) * jnp.square(gg)
        m_hat = m2 / (1.0 - ADAM_B1 ** ADAM_STEP)
        v_hat = v2 / (1.0 - ADAM_B2 ** ADAM_STEP)
        d_ref[...] = -ADAM_LR * (m_hat / (jnp.sqrt(v_hat) + ADAM_EPS) + ADAM_WD * w_ref[...])
        m_out[...] = m2
        v_out[...] = v2

    blk = pl.BlockSpec((tr, c), lambda i: (i, 0))
    return _pcall(body, name=name, grid=(r // tr,), in_specs=[blk] * 4, out_specs=[blk] * 3,
                  out_shape=[jax.ShapeDtypeStruct((r, c), F32)] * 3, dims=("parallel",))(w, g, m, v)


WEIGHTS = ["l0_attn_norm", "l0_w_in", "l0_gdn_conv", "l0_gdn_A_log", "l0_gdn_dt_bias", "l0_gdn_norm", "l0_mla_q_norm",
           "l0_mla_w_uq", "l0_mla_kv_norm", "l0_mla_w_ukv", "l0_w_out", "l0_ffn_norm", "l0_ffn_w_up", "l0_ffn_conv_w",
           "l0_ffn_conv_b", "l0_ffn_w_down", "l0_ple_proj", "l0_ple_gate_norm", "l0_ple_gate", "l1_attn_norm",
           "l1_w_in", "l1_ret_norm", "l1_w_out", "l1_ffn_norm", "l1_ffn_w_up", "l1_ffn_conv_w", "l1_ffn_conv_b",
           "l1_ffn_w_down", "l1_ple_proj", "l1_ple_gate_norm", "l1_ple_gate", "final_norm"]
COL_SHARDED = ["l0_w_in", "l0_mla_w_uq", "l0_mla_w_ukv", "l0_ffn_w_up", "l0_ple_proj", "l1_w_in", "l1_ffn_w_up",
               "l1_ple_proj"]
ROW_SHARDED = ["l0_w_out", "l0_ffn_w_down", "l0_ple_gate", "l1_w_out", "l1_ffn_w_down", "l1_ple_gate"]
BIG = [k for k in WEIGHTS if k in COL_SHARDED or k in ROW_SHARDED]
SMALL_SHARDED = ["l0_gdn_conv", "l0_ffn_conv_w", "l1_ffn_conv_w"]
SMALL = [k for k in WEIGHTS if k not in BIG]
KEPT_SHARDED = ["l0_ffn_w_up", "l0_ple_proj", "l1_w_in", "l1_ffn_w_up", "l1_ple_proj"]
GATHER_FIRST = ["l0_w_in", "l0_mla_w_uq", "l0_mla_w_ukv"]
GATHER_L0 = ["l0_w_out", "l0_ffn_w_up", "l0_ffn_w_down", "l0_ple_proj", "l0_ple_gate", "l1_w_out"]
GATHER_L1 = ["l1_w_in"]
GATHER_FFN = ["l1_ffn_w_down", "l1_ple_proj", "l1_ple_gate"]
GATHER_L1_IN = ["l1_ffn_w_up"]
REDUCE_L1 = [k for k in BIG if k.startswith("l1_")]
REDUCE_DQ = ["l0_ffn_w_up"]
REDUCE_L0 = ["l0_ffn_w_down", "l0_ple_proj", "l0_ple_gate", "l0_w_out"]
REDUCE_LAST = ["l0_w_in", "l0_mla_w_uq", "l0_mla_w_ukv"]


class Exchange:
    def __init__(self, chip, core):
        self.chip, self.core = chip, core

    def gather(self, bufs):
        return all_gather_chips(bufs)

    def pair_sums(self, names, grads):
        return [pair_add(g, got, self.core, "rs_pair_add_" + k)
                for k, g, got in zip(names, grads, pair_swap_halves(grads))]

    def finish(self, sums, landed):
        halves = [chip_add(sums[k], landed[k], self.chip, self.core, "rs_chip_add_" + k) for k in BIG]
        return dict(zip(BIG, pair_join_halves(halves)))


def _cols_to_full(s):
    j, k, n = s.shape
    return jnp.transpose(s, (1, 0, 2)).reshape(k, j * n)


def _full_to_cols(g):
    k, n4 = g.shape
    return jnp.transpose(g.reshape(k, N_CHIPS, n4 // N_CHIPS), (1, 0, 2))


def _pack_small(vals):
    flat = jnp.concatenate([v.astype(F32).reshape(-1) for v in vals])
    align = SUBLANES * LANES
    flat = jnp.pad(flat, (0, -flat.shape[0] % align))
    return flat.reshape(-1, LANES)


def _unpack_small(rows, shapes):
    flat = rows.reshape(-1)
    out, off = [], 0
    for shp in shapes:
        n = int(np.prod(shp))
        out.append(flat[off:off + n].reshape(shp))
        off += n
    return out


INPUTS = (["x", "p", "positions"] + WEIGHTS + ["loss_target"] + ["m_" + k for k in WEIGHTS]
          + ["v_" + k for k in WEIGHTS])


def kernel(
        x, p, positions, l0_attn_norm, l0_w_in, l0_gdn_conv, l0_gdn_A_log, l0_gdn_dt_bias, l0_gdn_norm, l0_mla_q_norm,
        l0_mla_w_uq, l0_mla_kv_norm, l0_mla_w_ukv, l0_w_out, l0_ffn_norm, l0_ffn_w_up, l0_ffn_conv_w, l0_ffn_conv_b,
        l0_ffn_w_down, l0_ple_proj, l0_ple_gate_norm, l0_ple_gate, l1_attn_norm, l1_w_in, l1_ret_norm, l1_w_out,
        l1_ffn_norm, l1_ffn_w_up, l1_ffn_conv_w, l1_ffn_conv_b, l1_ffn_w_down, l1_ple_proj, l1_ple_gate_norm,
        l1_ple_gate, final_norm, loss_target, m_l0_attn_norm, m_l0_w_in, m_l0_gdn_conv, m_l0_gdn_A_log,
        m_l0_gdn_dt_bias, m_l0_gdn_norm, m_l0_mla_q_norm, m_l0_mla_w_uq, m_l0_mla_kv_norm, m_l0_mla_w_ukv, m_l0_w_out,
        m_l0_ffn_norm, m_l0_ffn_w_up, m_l0_ffn_conv_w, m_l0_ffn_conv_b, m_l0_ffn_w_down, m_l0_ple_proj,
        m_l0_ple_gate_norm, m_l0_ple_gate, m_l1_attn_norm, m_l1_w_in, m_l1_ret_norm, m_l1_w_out, m_l1_ffn_norm,
        m_l1_ffn_w_up, m_l1_ffn_conv_w, m_l1_ffn_conv_b, m_l1_ffn_w_down, m_l1_ple_proj, m_l1_ple_gate_norm,
        m_l1_ple_gate, m_final_norm, v_l0_attn_norm, v_l0_w_in, v_l0_gdn_conv, v_l0_gdn_A_log, v_l0_gdn_dt_bias,
        v_l0_gdn_norm, v_l0_mla_q_norm, v_l0_mla_w_uq, v_l0_mla_kv_norm, v_l0_mla_w_ukv, v_l0_w_out, v_l0_ffn_norm,
        v_l0_ffn_w_up, v_l0_ffn_conv_w, v_l0_ffn_conv_b, v_l0_ffn_w_down, v_l0_ple_proj, v_l0_ple_gate_norm,
        v_l0_ple_gate, v_l1_attn_norm, v_l1_w_in, v_l1_ret_norm, v_l1_w_out, v_l1_ffn_norm, v_l1_ffn_w_up,
        v_l1_ffn_conv_w, v_l1_ffn_conv_b, v_l1_ffn_w_down, v_l1_ple_proj, v_l1_ple_gate_norm, v_l1_ple_gate,
        v_final_norm):
    given = locals()
    a = {k: given[k] for k in INPUTS}
    x_i, y_i, c_i = _place()
    chip = 2 * x_i + y_i
    shard_shapes = {k: a[k].shape for k in WEIGHTS}

    slots = {k: cast_to_slot(a[k], chip, "cast_" + k) for k in BIG}
    W = {}
    placed = []
    for k in SMALL_SHARDED:
        r, c = shard_shapes[k]
        mine = jnp.where(c_i == 0, a[k], jnp.zeros_like(a[k]))
        placed.append(lax.dynamic_update_slice(jnp.zeros((r, N_CHIPS * c), F32), mine, (0, chip * c)))
    full_small = _unpack_small(all_reduce_small(_pack_small(placed), "gather_small_weights"),
                               [p_.shape for p_ in placed])
    for k in SMALL:
        W[k] = a[k]
    W.update(dict(zip(SMALL_SHARDED, full_small)))

    loss_part, grad_x, grads, G = local_step(a["x"][0], a["p"][:, 0], a["positions"][0], a["loss_target"][0], slots, W,
                                             Exchange(chip, c_i))
    loss = lax.psum(loss_part, ("x", "y", "c"))
    deltas, new_m, new_v = {}, {}, {}
    for k in BIG:
        deltas[k], new_m[k], new_v[k] = adamw(a[k], grads[k], a["m_" + k], a["v_" + k], "adamw_" + k)

    small_full = [G[k].reshape(-1) for k in SMALL]
    summed = _unpack_small(all_reduce_small(_pack_small(small_full), "reduce_small_grads"),
                           [G[k].shape for k in SMALL])
    for k, g in zip(SMALL, summed):
        if k in SMALL_SHARDED:
            r, c = shard_shapes[k]
            g = lax.dynamic_slice(g.reshape(r, N_CHIPS * c), (0, chip * c), (r, c))
        grads[k] = g.reshape(shard_shapes[k])
    packed = [_pack_small([d[k] for k in SMALL]) for d in (
        {k: a[k] for k in SMALL}, grads, {k: a["m_" + k] for k in SMALL}, {k: a["v_" + k] for k in SMALL})]
    outs = adamw(*packed, "adamw_small")
    shapes = [shard_shapes[k] for k in SMALL]
    for d, rows in zip((deltas, new_m, new_v), outs):
        d.update(dict(zip(SMALL, _unpack_small(rows, shapes))))

    return (loss, grad_x[None], *[grads[k] for k in WEIGHTS], *[deltas[k] for k in WEIGHTS],
            *[new_m[k] for k in WEIGHTS], *[new_v[k] for k in WEIGHTS])
```

```python
import functools
import math

import numpy as np
import jax
import jax.numpy as jnp
from jax import lax
from jax.experimental import pallas as pl
from jax.experimental.pallas import tpu as pltpu

F32, BF16 = jnp.float32, jnp.bfloat16
HI = lax.Precision.HIGHEST
MESH = pl.DeviceIdType.MESH

NORM_EPS = 1e-6
ROPE_THETA = 10000.0
D_MODEL = 2048
PLE_DIM = 256
GDN_HEADS, GDN_DK, GDN_DV, GDN_CONV = 8, 128, 128, 4
MLA_HEADS, MLA_Q_RANK, MLA_KV_RANK, MLA_NOPE, MLA_ROPE, MLA_V = 8, 512, 512, 128, 64, 128
RET_HEADS, RET_DK, RET_DV = 8, 256, 512
D_FF, FFN_CONV = 5632, 3
ADAM_LR, ADAM_B1, ADAM_B2, ADAM_EPS, ADAM_WD, ADAM_STEP = 0.001, 0.9, 0.999, 1e-08, 0.01, 10

LANES = 128
SUBLANES = 8
CHUNK = 128
N_CHIPS = 4
VMEM_LIMIT_MB = 56

ZIN_QKV, ZIN_Z, ZIN_CQ, ZIN_CKV, ZIN_KR, ZIN_AB, ZIN_W = 0, 3072, 4096, 4608, 5120, 5248, 5376


class Hosted:
    def __init__(self, inputs, out_shapes, aliases, n_sems, start, finish):
        self.inputs, self.out_shapes, self.aliases, self.n_sems = list(inputs), list(out_shapes), dict(aliases), n_sems
        self.start, self.finish = start, finish


def _pcall(body, *, name, out_shape, grid=(), in_specs=None, out_specs=None, scratch_shapes=(), dims=None,
           hosted=None):
    params = dict(vmem_limit_bytes=VMEM_LIMIT_MB << 20)
    if dims is not None:
        params["dimension_semantics"] = dims
    if hosted is None:
        return pl.pallas_call(body, name=name, out_shape=out_shape, grid=grid, in_specs=in_specs, out_specs=out_specs,
                              scratch_shapes=list(scratch_shapes), compiler_params=pltpu.CompilerParams(**params))
    single = not isinstance(out_shape, (list, tuple))
    out_shape = [out_shape] if single else list(out_shape)
    out_specs = [out_specs] if single else list(out_specs)
    n_in, n_out, n_scr = len(in_specs), len(out_shape), len(scratch_shapes)
    h_in, h_out = len(hosted.inputs), len(hosted.out_shapes)
    hbm = pl.BlockSpec(memory_space=pltpu.HBM)

    def hosting_body(*refs):
        ins, h_ins = refs[:n_in], refs[n_in:n_in + h_in]
        o0 = n_in + h_in
        outs, h_outs = refs[o0:o0 + n_out], refs[o0 + n_out:o0 + n_out + h_out]
        s0 = o0 + n_out + h_out
        scr, (send_sems, recv_sems) = refs[s0:s0 + n_scr], refs[s0 + n_scr:]
        ids = [pl.program_id(d) for d in range(len(grid))]
        first = functools.reduce(lambda u, v: u & v, [i == 0 for i in ids])
        last = functools.reduce(lambda u, v: u & v, [i == g - 1 for i, g in zip(ids, grid)])

        @pl.when(first)
        def _():
            hosted.start(h_ins, h_outs, send_sems, recv_sems)

        body(*ins, *outs, *scr)

        @pl.when(last)
        def _():
            hosted.finish(h_ins, h_outs, send_sems, recv_sems)

    params["dimension_semantics"] = ("arbitrary",) * len(grid)
    call = pl.pallas_call(
        hosting_body, name=name, out_shape=out_shape + hosted.out_shapes, grid=grid,
        in_specs=list(in_specs) + [hbm] * h_in, out_specs=out_specs + [hbm] * h_out,
        scratch_shapes=list(scratch_shapes) + [pltpu.SemaphoreType.DMA((hosted.n_sems,)),
                                               pltpu.SemaphoreType.DMA((hosted.n_sems,))],
        input_output_aliases={n_in + i: n_out + o for i, o in hosted.aliases.items()},
        compiler_params=pltpu.CompilerParams(**params))

    def run(*args):
        res = call(*args, *hosted.inputs)
        main = res[:n_out]
        return (main[0] if single else main), list(res[n_out:])

    return run


def _tile(n, target, mult=LANES):
    best = None
    for t in range(mult, min(n, target) + 1, mult):
        if n % t == 0:
            best = t
    return best or n


_DN = {"nn": (((1,), (0,)), ((), ())), "nt": (((1,), (1,)), ((), ())), "tn": (((0,), (0,)), ((), ()))}


def matmul(a, b, mode, out_dtype, *, name, add=None, b_shards=1, out_shards=1, tm=512, tn=1024, tk=2048,
           hosted=None, norm_bwd=None):
    bs = b.shape[-2:]
    if mode == "nn":
        (M, K), (K2, N) = a.shape, (bs[0], bs[1] * b_shards)
    elif mode == "nt":
        (M, K), (N, K2) = a.shape, (bs[0], bs[1] * b_shards)
    else:
        (K, M), (K2, N) = a.shape, bs
    assert K == K2, (name, a.shape, b.shape)
    n_sh = N // max(b_shards if mode == "nn" else 1, out_shards)
    k_sh = K // (b_shards if mode == "nt" else 1)
    tm, tn, tk = _tile(M, tm), _tile(n_sh, tn), _tile(k_sh, tk)
    nk = K // tk
    nbn, nbk = n_sh // tn, k_sh // tk
    dn = _DN[mode]
    has_add = add is not None
    a_bytes, b_bytes = a.size * a.dtype.itemsize, b.size * b.dtype.itemsize
    i_outer = nk > 1 or a_bytes + (M // tm) * b_bytes <= b_bytes + (N // tn) * a_bytes

    def ij(g0, g1):
        return (g0, g1) if i_outer else (g1, g0)

    fused_norm = norm_bwd is not None
    if fused_norm:
        assert tn == N and out_shards == 1 and not has_add and hosted is None, name
        i_outer = True

    def body(*refs):
        a_ref, b_ref = refs[:2]
        add_ref = refs[2] if has_add else None
        o_ref = refs[3 if has_add else 2]
        part = lax.dot_general(a_ref[...].astype(BF16), b_ref[...].astype(BF16), dn, preferred_element_type=F32)

        def finish(r):
            if fused_norm:
                h_ref, g_ref, dh_ref, o32_ref, o16_ref, dg_ref = refs[2:8]
                _, vjp = jax.vjp(_rms, h_ref[...], g_ref[...])
                dx, dg = vjp(r)
                out = dx + dh_ref[...]
                o32_ref[...] = out
                o16_ref[...] = out.astype(BF16)
                dg_ref[...] += dg
                return
            if has_add:
                r = r + add_ref[...]
            o_ref[...] = r.astype(out_dtype)

        if fused_norm:
            @pl.when((pl.program_id(0) == 0) & (pl.program_id(2) == 0))
            def _():
                refs[7][...] = jnp.zeros_like(refs[7])

        if nk == 1:
            finish(part)
            return
        acc = refs[-1]
        k = pl.program_id(2)

        @pl.when(k == 0)
        def _():
            acc[...] = part

        @pl.when(k > 0)
        def _():
            acc[...] += part

        @pl.when(k == nk - 1)
        def _():
            finish(acc[...])

    def spec(block, fn):
        return pl.BlockSpec(block, lambda g0, g1, k: fn(*ij(g0, g1), k))

    if mode == "tn":
        a_spec = spec((tk, tm), lambda i, j, k: (k, i))
    else:
        a_spec = spec((tm, tk), lambda i, j, k: (i, k))
    if mode == "nt":
        if b_shards > 1:
            b_spec = spec((None, tn, tk), lambda i, j, k: (k // nbk, j, k % nbk))
        else:
            b_spec = spec((tn, tk), lambda i, j, k: (j, k))
    elif b_shards > 1:
        b_spec = spec((None, tk, tn), lambda i, j, k: (j // nbn, k, j % nbn))
    else:
        b_spec = spec((tk, tn), lambda i, j, k: (k, j))
    in_specs = [a_spec, b_spec]
    args = [a, b]
    if has_add:
        in_specs.append(spec((tm, tn), lambda i, j, k: (i, j)))
        args.append(add)
    if out_shards > 1:
        out_spec = spec((None, tm, tn), lambda i, j, k: (j // nbn, i, j % nbn))
        out_shape = jax.ShapeDtypeStruct((out_shards, M, n_sh), out_dtype)
    else:
        out_spec = spec((tm, tn), lambda i, j, k: (i, j))
        out_shape = jax.ShapeDtypeStruct((M, N), out_dtype)
    gi, gj = M // tm, N // tn
    if fused_norm:
        h, gain, dh = norm_bwd
        row_blk = spec((tm, N), lambda i, j, k: (i, 0))
        gain_blk = spec((1, N), lambda i, j, k: (0, 0))
        return _pcall(body, name=name, grid=(gi, 1, nk), in_specs=in_specs + [row_blk, gain_blk, row_blk],
                      out_specs=[row_blk, row_blk, gain_blk],
                      out_shape=[jax.ShapeDtypeStruct((M, N), F32), jax.ShapeDtypeStruct((M, N), BF16),
                                 jax.ShapeDtypeStruct((1, N), F32)],
                      scratch_shapes=[pltpu.VMEM((tm, tn), F32)] if nk > 1 else [],
                      dims=("arbitrary", "arbitrary", "arbitrary"))(a, b, h, gain, dh)
    return _pcall(body, name=name, out_shape=out_shape, grid=(gi, gj, nk) if i_outer else (gj, gi, nk),
                  in_specs=in_specs, out_specs=out_spec,
                  scratch_shapes=[pltpu.VMEM((tm, tn), F32)] if nk > 1 else [],
                  dims=("parallel", "parallel", "arbitrary"), hosted=hosted)(*args)


def _row_spec(tr, w, c):
    return pl.BlockSpec((tr, w), lambda i: (i, c))


def _full_spec(arr):
    return pl.BlockSpec(arr.shape, lambda i: (0,) * arr.ndim)


def rowwise(fn, rows, params, nd_rows, outs, *, name, tr=256):
    S = rows[0][0].shape[0]
    tr = min(tr, S)
    n_in = len(rows) + len(params) + len(nd_rows)

    def body(*refs):
        res = fn(*[x[...] for x in refs[:n_in]])
        for o_ref, v in zip(refs[n_in:], res):
            o_ref[...] = v.astype(o_ref.dtype)

    return _pcall(body, name=name, grid=(S // tr,),
                  in_specs=([_row_spec(tr, w, c) for (_, w, c) in rows] + [_full_spec(q) for q in params]
                            + [_row_spec(tr, w, c) for (_, w, c) in nd_rows]),
                  out_specs=[_row_spec(tr, w, 0) for (w, _) in outs],
                  out_shape=[jax.ShapeDtypeStruct((S, w), dt) for (w, dt) in outs],
                  dims=("parallel",))(*[r[0] for r in rows], *params, *[r[0] for r in nd_rows])


def rowwise_bwd(fn, rows, params, nd_rows, cts, d_dtypes, *, name, adds=None, tr=256):
    S = rows[0][0].shape[0]
    tr = min(tr, S)
    n_r, n_p, n_n, n_c = len(rows), len(params), len(nd_rows), len(cts)
    adds = adds or [None] * n_r
    add_list = [a for a in adds if a is not None]
    n_a = len(add_list)
    d_dtypes = [dt if isinstance(dt, (list, tuple)) else (dt,) for dt in d_dtypes]
    n_d = sum(len(dt) for dt in d_dtypes)

    def body(*refs):
        it = iter(refs)
        r = [next(it)[...] for _ in range(n_r)]
        p = [next(it)[...] for _ in range(n_p)]
        nd = [next(it)[...] for _ in range(n_n)]
        c = [next(it)[...] for _ in range(n_c)]
        ad = [next(it)[...] for _ in range(n_a)]
        d_row_refs = [[next(it) for _ in dts] for dts in d_dtypes]
        d_par_refs = [next(it) for _ in range(n_p)]
        outs, vjp = jax.vjp(lambda *dp: fn(*dp, *nd), *r, *p)
        g = vjp(tuple(ci.astype(o.dtype) for ci, o in zip(c, outs)))
        ai = 0
        for k in range(n_r):
            gk = g[k].astype(F32)
            if adds[k] is not None:
                gk = gk + ad[ai].astype(F32)
                ai += 1
            for ref in d_row_refs[k]:
                ref[...] = gk.astype(ref.dtype)

        @pl.when(pl.program_id(0) == 0)
        def _():
            for ref in d_par_refs:
                ref[...] = jnp.zeros_like(ref)

        for k in range(n_p):
            d_par_refs[k][...] += g[n_r + k].astype(F32)

    in_specs = ([_row_spec(tr, w, c) for (_, w, c) in rows] + [_full_spec(q) for q in params]
                + [_row_spec(tr, w, c) for (_, w, c) in nd_rows] + [_row_spec(tr, w, c) for (_, w, c) in cts]
                + [_row_spec(tr, w, c) for (_, w, c) in add_list])
    out_specs = ([_row_spec(tr, w, 0) for (_, w, _), dts in zip(rows, d_dtypes) for _ in dts]
                 + [_full_spec(q) for q in params])
    out_shape = ([jax.ShapeDtypeStruct((S, w), dt) for (_, w, _), dts in zip(rows, d_dtypes) for dt in dts]
                 + [jax.ShapeDtypeStruct(q.shape, F32) for q in params])
    res = _pcall(body, name=name, grid=(S // tr,), in_specs=in_specs, out_specs=out_specs, out_shape=out_shape,
                 dims=("arbitrary",))(*[r[0] for r in rows], *params, *[r[0] for r in nd_rows],
                                      *[r[0] for r in cts], *[r[0] for r in add_list])
    d_rows, i = [], 0
    for dts in d_dtypes:
        d_rows.append(res[i] if len(dts) == 1 else tuple(res[i:i + len(dts)]))
        i += len(dts)
    return d_rows, res[n_d:]


def _rms(x, g):
    x = x.astype(F32)
    return x * lax.rsqrt(jnp.mean(x * x, axis=-1, keepdims=True) + NORM_EPS) * g


def _fn_rms(x, g):
    return (_rms(x, g),)


def _sigmoid(x):
    return 1.0 / (1.0 + jnp.exp(-x))


def _silu(x):
    return x * _sigmoid(x)


def _softplus(x):
    return jnp.maximum(x, 0.0) + jnp.log(1.0 + jnp.exp(-jnp.abs(x)))


def _fn_ple(h, pp, gl):
    return (h.astype(F32) + pp.astype(F32) * _sigmoid(gl.astype(F32)),)


def _fn_ple_terms(pp, gl):
    return (pp.astype(F32) * _sigmoid(gl.astype(F32)),)


def _rot_half_matrix():
    half = MLA_ROPE // 2
    r = lax.broadcasted_iota(jnp.int32, (LANES, LANES), 0)
    c = lax.broadcasted_iota(jnp.int32, (LANES, LANES), 1)
    plus = (c == r + half) & (r < half)
    minus = (r == c + half) & (c < half)
    return jnp.where(plus, 1.0, 0.0) - jnp.where(minus, 1.0, 0.0)


def _rope_pad(x, cosp, sinp):
    return x * cosp + jnp.dot(x, _rot_half_matrix(), precision=HI, preferred_element_type=F32) * sinp


def _fn_mla_pre(cq, ckv, kr, qn_w, kvn_w, cosp, sinp):
    return (_rms(cq, qn_w), _rms(ckv, kvn_w), _rope_pad(kr.astype(F32), cosp, sinp))


def _fn_rope_q(q, cosp, sinp):
    q = q.astype(F32)
    parts = []
    for h in range(MLA_HEADS):
        base = 2 * LANES * h
        parts.append(q[:, base:base + LANES])
        parts.append(_rope_pad(q[:, base + LANES:base + 2 * LANES], cosp, sinp))
    return (jnp.concatenate(parts, axis=1),)


def _fn_ret_gate(g, on, w):
    return (_silu(g.astype(F32)) * (on.astype(F32) * w),)


def _shift_down(cur, halo, s):
    if s == 0:
        return cur
    r = pltpu.roll(cur, s, 0)
    hs = pltpu.roll(halo, s, 0)
    row = lax.broadcasted_iota(jnp.int32, hs.shape, 0)
    first = jnp.where(row < s, hs, r[:SUBLANES])
    return jnp.concatenate([first, r[SUBLANES:]], axis=0)


def _shift_up(cur, halo, s):
    if s == 0:
        return cur
    n = cur.shape[0]
    r = pltpu.roll(cur, n - s, 0)
    hs = pltpu.roll(halo, SUBLANES - s, 0)
    row = lax.broadcasted_iota(jnp.int32, hs.shape, 0)
    last = jnp.where(row >= SUBLANES - s, hs, r[n - SUBLANES:])
    return jnp.concatenate([r[:n - SUBLANES], last], axis=0)


def _prev_halo_spec(tr, tw, col):
    return pl.BlockSpec((SUBLANES, tw), lambda c, i: (jnp.maximum(i * (tr // SUBLANES) - 1, 0), col(c)))


def _conv_taps(cur, halo, w_ref, width):
    taps = [_shift_down(cur, halo, width - 1 - j) for j in range(width)]
    y = taps[0] * w_ref[0:1, :]
    for j in range(1, width):
        y = y + taps[j] * w_ref[j:j + 1, :]
    return y, taps


def gdn_conv_fwd(zin, w, *, tr=512, tw=512):
    S = zin.shape[0]
    tr = min(tr, S)
    width, C = w.shape

    def body(cur_ref, halo_ref, w_ref, o_ref):
        i = pl.program_id(1)
        halo = halo_ref[...] * (i > 0).astype(F32)
        y, _ = _conv_taps(cur_ref[...], halo, w_ref, width)
        o_ref[...] = _silu(y)

    return _pcall(body, name="gdn_conv_fwd", grid=(C // tw, S // tr),
                  in_specs=[pl.BlockSpec((tr, tw), lambda c, i: (i, c)), _prev_halo_spec(tr, tw, lambda c: c),
                            pl.BlockSpec((width, tw), lambda c, i: (0, c))],
                  out_specs=pl.BlockSpec((tr, tw), lambda c, i: (i, c)),
                  out_shape=jax.ShapeDtypeStruct((S, C), F32), dims=("parallel", "arbitrary"))(zin, zin, w)


def gdn_conv_bwd(zin, w, dy, *, tr=512, tw=512):
    S = zin.shape[0]
    tr = min(tr, S)
    width, C = w.shape

    def body(cur_ref, halo_ref, w_ref, dy_ref, da_ref, dw_ref):
        i = pl.program_id(1)
        halo = halo_ref[...] * (i > 0).astype(F32)
        y, taps = _conv_taps(cur_ref[...], halo, w_ref, width)
        sg = _sigmoid(y)
        da = dy_ref[...] * (sg * (1.0 + y * (1.0 - sg)))
        da_ref[...] = da

        @pl.when(i == 0)
        def _():
            dw_ref[...] = jnp.zeros_like(dw_ref)

        for j in range(width):
            dw_ref[j:j + 1, :] += jnp.sum(da * taps[j], axis=0, keepdims=True)

    return _pcall(body, name="gdn_conv_bwd", grid=(C // tw, S // tr),
                  in_specs=[pl.BlockSpec((tr, tw), lambda c, i: (i, c)), _prev_halo_spec(tr, tw, lambda c: c),
                            pl.BlockSpec((width, tw), lambda c, i: (0, c)),
                            pl.BlockSpec((tr, tw), lambda c, i: (i, c))],
                  out_specs=[pl.BlockSpec((tr, tw), lambda c, i: (i, c)),
                             pl.BlockSpec((width, tw), lambda c, i: (0, c))],
                  out_shape=[jax.ShapeDtypeStruct((S, C), F32), jax.ShapeDtypeStruct((width, C), F32)],
                  dims=("parallel", "arbitrary"))(zin, zin, w, dy)


def conv_transpose(dy, w, out_dtype, *, name, tr=512, tw=512):
    if dy.ndim == 2:
        dy = dy[None]
    T, S, C = dy.shape
    tr = min(tr, S)
    width = w.shape[0]
    n_i, nc = S // tr, C // tw

    def body(cur_ref, halo_ref, w_ref, o_ref):
        i = pl.program_id(2)
        halo = halo_ref[...] * (i < n_i - 1).astype(F32)
        cur = cur_ref[...]
        acc = cur * w_ref[width - 1:width, :]
        for s in range(1, width):
            acc = acc + _shift_up(cur, halo, s) * w_ref[width - 1 - s:width - s, :]
        o_ref[...] = acc.astype(out_dtype)

    nxt = pl.BlockSpec((None, SUBLANES, tw),
                       lambda t, c, i: (t, jnp.minimum((i + 1) * (tr // SUBLANES), S // SUBLANES - 1), c))
    return _pcall(body, name=name, grid=(T, nc, n_i),
                  in_specs=[pl.BlockSpec((None, tr, tw), lambda t, c, i: (t, i, c)), nxt,
                            pl.BlockSpec((width, tw), lambda t, c, i: (0, t * nc + c))],
                  out_specs=pl.BlockSpec((tr, tw), lambda t, c, i: (i, t * nc + c)),
                  out_shape=jax.ShapeDtypeStruct((S, T * C), out_dtype),
                  dims=("parallel", "parallel", "arbitrary"))(dy, dy, w)


def ffn_conv_fwd(u, w, b, *, tr=512, tw=512):
    S, C2 = u.shape
    tr = min(tr, S)
    width = w.shape[0]
    half = C2 // 2
    nc = half // tw

    def body(g_ref, gh_ref, u_ref, uh_ref, wg_ref, wu_ref, bg_ref, bu_ref, o_ref):
        i = pl.program_id(1)
        live = (i > 0).astype(F32)
        yg, _ = _conv_taps(g_ref[...], gh_ref[...] * live, wg_ref, width)
        yu, _ = _conv_taps(u_ref[...], uh_ref[...] * live, wu_ref, width)
        o_ref[...] = (_silu(yg + bg_ref[...]) * (yu + bu_ref[...])).astype(o_ref.dtype)

    return _pcall(body, name="ffn_conv_fwd", grid=(nc, S // tr),
                  in_specs=[pl.BlockSpec((tr, tw), lambda c, i: (i, c)), _prev_halo_spec(tr, tw, lambda c: c),
                            pl.BlockSpec((tr, tw), lambda c, i: (i, c + nc)),
                            _prev_halo_spec(tr, tw, lambda c: c + nc),
                            pl.BlockSpec((width, tw), lambda c, i: (0, c)),
                            pl.BlockSpec((width, tw), lambda c, i: (0, c + nc)),
                            pl.BlockSpec((1, tw), lambda c, i: (0, c)), pl.BlockSpec((1, tw), lambda c, i: (0, c + nc))],
                  out_specs=pl.BlockSpec((tr, tw), lambda c, i: (i, c)),
                  out_shape=jax.ShapeDtypeStruct((S, half), BF16),
                  dims=("parallel", "arbitrary"))(u, u, u, u, w, w, b, b)


def ffn_conv_bwd(u, w, b, df, *, tr=512, tw=512):
    S, C2 = u.shape
    tr = min(tr, S)
    width = w.shape[0]
    half = C2 // 2
    nc = half // tw

    def body(g_ref, gh_ref, u_ref, uh_ref, wg_ref, wu_ref, bg_ref, bu_ref, df_ref, dc_ref, dw_ref, db_ref):
        i = pl.program_id(1)
        live = (i > 0).astype(F32)
        yg, gt = _conv_taps(g_ref[...], gh_ref[...] * live, wg_ref, width)
        yu, ut = _conv_taps(u_ref[...], uh_ref[...] * live, wu_ref, width)
        yg = yg + bg_ref[...]
        yu = yu + bu_ref[...]
        sg = _sigmoid(yg)
        dfv = df_ref[...].astype(F32)
        dcs = (dfv * yu * (sg * (1.0 + yg * (1.0 - sg))), dfv * (yg * sg))

        @pl.when(i == 0)
        def _():
            dw_ref[...] = jnp.zeros_like(dw_ref)
            db_ref[...] = jnp.zeros_like(db_ref)

        for t, (dc, taps) in enumerate(zip(dcs, (gt, ut))):
            dc_ref[t] = dc
            db_ref[t] += jnp.sum(dc, axis=0, keepdims=True)
            for j in range(width):
                dw_ref[t, j:j + 1, :] += jnp.sum(dc * taps[j], axis=0, keepdims=True)

    dc, dw, db = _pcall(
        body, name="ffn_conv_bwd", grid=(nc, S // tr),
        in_specs=[pl.BlockSpec((tr, tw), lambda c, i: (i, c)), _prev_halo_spec(tr, tw, lambda c: c),
                  pl.BlockSpec((tr, tw), lambda c, i: (i, c + nc)), _prev_halo_spec(tr, tw, lambda c: c + nc),
                  pl.BlockSpec((width, tw), lambda c, i: (0, c)), pl.BlockSpec((width, tw), lambda c, i: (0, c + nc)),
                  pl.BlockSpec((1, tw), lambda c, i: (0, c)), pl.BlockSpec((1, tw), lambda c, i: (0, c + nc)),
                  pl.BlockSpec((tr, tw), lambda c, i: (i, c))],
        out_specs=[pl.BlockSpec((2, tr, tw), lambda c, i: (0, i, c)), pl.BlockSpec((2, width, tw), lambda c, i: (0, 0, c)),
                   pl.BlockSpec((2, 1, tw), lambda c, i: (0, 0, c))],
        out_shape=[jax.ShapeDtypeStruct((2, S, half), F32), jax.ShapeDtypeStruct((2, width, half), F32),
                   jax.ShapeDtypeStruct((2, 1, half), F32)],
        dims=("parallel", "arbitrary"))(u, u, u, u, w, w, b, b, df)
    return dc, jnp.concatenate([dw[0], dw[1]], axis=1), jnp.concatenate([db[0], db[1]], axis=1)


_MODE_OF = {v: k for k, v in _DN.items()}


def _bf16_dot(a, b, mode):
    return lax.dot_general(a.astype(BF16), b.astype(BF16), _DN[mode], preferred_element_type=F32)


@functools.partial(jax.custom_vjp, nondiff_argnums=(2,))
def _bdot_mode(a, b, mode):
    return _bf16_dot(a, b, mode)


def _bdot_fwd(a, b, mode):
    return _bf16_dot(a, b, mode), (a, b)


def _bdot_bwd(mode, res, ct):
    a, b = res
    if mode == "nn":
        da, db = _bf16_dot(ct, b, "nt"), _bf16_dot(a, ct, "tn")
    elif mode == "nt":
        da, db = _bf16_dot(ct, b, "nn"), _bf16_dot(ct, a, "tn")
    else:
        da, db = _bf16_dot(b, ct, "nt"), _bf16_dot(a, ct, "nn")
    return da.astype(a.dtype), db.astype(b.dtype)


_bdot_mode.defvjp(_bdot_fwd, _bdot_bwd)


def _bdot(a, b, dn=_DN["nn"]):
    return _bdot_mode(a, b, _MODE_OF[dn])


def _hi_lo(x):
    hi = x.astype(BF16)
    return hi, (x - hi.astype(F32)).astype(BF16)


def _dot3_raw(a, b, mode):
    a1, a2 = _hi_lo(a)
    b1, b2 = _hi_lo(b)
    dot = lambda p, q: lax.dot_general(p, q, _DN[mode], preferred_element_type=F32)
    return dot(a1, b1) + (dot(a1, b2) + dot(a2, b1))


@functools.partial(jax.custom_vjp, nondiff_argnums=(2,))
def _dot3(a, b, mode="nn"):
    return _dot3_raw(a, b, mode)


def _dot3_fwd(a, b, mode):
    return _dot3_raw(a, b, mode), (a, b)


def _dot3_bwd(mode, res, ct):
    a, b = res
    if mode == "nn":
        return _dot3_raw(ct, b, "nt"), _dot3_raw(a, ct, "tn")
    if mode == "nt":
        return _dot3_raw(ct, b, "nn"), _dot3_raw(ct, a, "tn")
    return _dot3_raw(b, ct, "nt"), _dot3_raw(a, ct, "nn")


_dot3.defvjp(_dot3_fwd, _dot3_bwd)


@functools.partial(jax.custom_vjp, nondiff_argnums=(2,))
def _gdot(a, b, mode="nn"):
    return _bf16_dot(a, b, mode)


def _gdot_fwd(a, b, mode):
    return _bf16_dot(a, b, mode), (a.astype(BF16), b.astype(BF16))


def _ct_dot(p, q, mode, ct_first):
    ct, r = (p, q) if ct_first else (q, p)
    c1, c2 = _hi_lo(ct)
    dot = lambda c: lax.dot_general(*((c, r) if ct_first else (r, c)), _DN[mode], preferred_element_type=F32)
    return dot(c1) + dot(c2)


def _gdot_bwd(mode, res, ct):
    a, b = res
    if mode == "nn":
        return _ct_dot(ct, b, "nt", True), _ct_dot(a, ct, "tn", False)
    if mode == "nt":
        return _ct_dot(ct, b, "nn", True), _ct_dot(ct, a, "tn", True)
    return _ct_dot(b, ct, "nt", False), _ct_dot(a, ct, "nn", False)


_gdot.defvjp(_gdot_fwd, _gdot_bwd)


def _split_dot(ones, x):
    x1 = x.astype(BF16)
    r1 = x - x1.astype(F32)
    x2 = r1.astype(BF16)
    x3 = (r1 - x2.astype(F32)).astype(BF16)
    m = ones.astype(BF16)
    dot = lambda p: lax.dot_general(m, p, _DN["nn"], preferred_element_type=F32)
    return dot(x1) + dot(x2) + dot(x3)


@jax.custom_vjp
def _tri_cumsum(x, lower, upper):
    return _split_dot(lower, x)


def _tri_cumsum_fwd(x, lower, upper):
    return _split_dot(lower, x), (lower, upper)


def _tri_cumsum_bwd(res, ct):
    lower, upper = res
    return _split_dot(upper, ct), jnp.zeros_like(lower), jnp.zeros_like(upper)


_tri_cumsum.defvjp(_tri_cumsum_fwd, _tri_cumsum_bwd)


def _tri_masks(n):
    r = lax.broadcasted_iota(jnp.int32, (n, n), 0)
    c = lax.broadcasted_iota(jnp.int32, (n, n), 1)
    return r >= c, r > c


def _gdn_chunk(q, k, v, z, ab, a_row, dt_row, norm_w, state, sel_a, sel_b):
    C = q.shape[0]
    incl, strict = _tri_masks(C)
    lower = jnp.where(incl, 1.0, 0.0)
    qn = q * lax.rsqrt(jnp.sum(q * q, axis=-1, keepdims=True) + NORM_EPS) * (GDN_DK ** -0.5)
    kn = k * lax.rsqrt(jnp.sum(k * k, axis=-1, keepdims=True) + NORM_EPS)
    g = jnp.sum(-jnp.exp(a_row) * _softplus(ab + dt_row) * sel_a, axis=-1, keepdims=True)
    beta = jnp.sum(_sigmoid(ab) * sel_b, axis=-1, keepdims=True)
    gb = jnp.broadcast_to(g, (C, C))
    g_col = _tri_cumsum(gb, lower, jnp.where(strict, 0.0, 1.0))
    g_row = g_col.T
    g_last = jnp.sum(gb, axis=0, keepdims=True)
    gamma = jnp.where(incl, jnp.exp(jnp.where(incl, g_col - g_row, 0.0)), 0.0)
    e_col = jnp.exp(g_col)
    kb = kn * beta
    a_mat = jnp.where(strict, _gdot(kb, kn, "nt") * gamma, 0.0)
    x = jnp.concatenate([v * beta, kb * e_col], axis=1)
    pw = -a_mat
    steps = int(math.log2(C))
    for it in range(steps):
        x = x + _dot3(pw, x, "nn")
        if it < steps - 1:
            pw = _dot3(pw, pw, "nn")
    u, w = x[:, :GDN_DV], x[:, GDN_DV:]
    attn = _gdot(qn, kn, "nt") * gamma
    q_dec = qn * e_col
    k_dec = kn * jnp.exp(g_last - g_col)
    v_new = u - _gdot(w, state, "nn")
    o = _gdot(q_dec, state, "nn") + _gdot(attn, v_new, "nn")
    state_new = state * jnp.exp(jnp.broadcast_to(g_last, state.shape)) + _gdot(k_dec, v_new, "tn")
    y = _rms(o, norm_w) * _silu(z)
    return y, state_new


def _head_selectors(h):
    lane = lax.broadcasted_iota(jnp.int32, (1, LANES), 1)
    return jnp.where(lane == h, 1.0, 0.0), jnp.where(lane == h + GDN_HEADS, 1.0, 0.0)


GDN_HPS = 4
GDN_W = GDN_HPS * LANES


def _gdn_in_specs(rev, nc):
    def n_(n):
        return nc - 1 - n if rev else n
    G = GDN_HEADS // GDN_HPS
    blk = lambda off: pl.BlockSpec((CHUNK, GDN_W), lambda n, h: (n_(n), off + h))
    row = pl.BlockSpec((1, LANES), lambda n, h: (0, 0))
    return n_, [blk(0), blk(G), blk(2 * G), blk(ZIN_Z // GDN_W),
                pl.BlockSpec((CHUNK, LANES), lambda n, h: (n_(n), ZIN_AB // LANES)), row, row, row]


def _lanes(ref, j):
    return ref[:, j * LANES:(j + 1) * LANES]


def _hosting(call, hosted, *args):
    res = call(*args)
    return res if hosted is not None else (res, [])


def gdn_fwd(qkv, zin, a_row, dt_row, norm_w, hosted=None):
    S = qkv.shape[0]
    nc = S // CHUNK
    H = GDN_HEADS
    _, in_specs = _gdn_in_specs(False, nc)

    def body(q_ref, k_ref, v_ref, z_ref, ab_ref, a_ref, dt_ref, nw_ref, y_ref, st_ref, state):
        n, g = pl.program_id(0), pl.program_id(1)
        @pl.when((n == 0) & (g == 0))
        def _():
            state[...] = jnp.zeros_like(state)

        res = []
        for j in range(GDN_HPS):
            h = g * GDN_HPS + j
            st = state[h]
            sel_a, sel_b = _head_selectors(h)
            res.append((st,) + _gdn_chunk(_lanes(q_ref, j), _lanes(k_ref, j), _lanes(v_ref, j), _lanes(z_ref, j),
                                          ab_ref[...], a_ref[...], dt_ref[...], nw_ref[...], st, sel_a, sel_b))
        for j, (st, y, st_new) in enumerate(res):
            st_ref[j] = st
            y_ref[:, j * LANES:(j + 1) * LANES] = y.astype(y_ref.dtype)
            state[g * GDN_HPS + j] = st_new

    call = _pcall(body, name="gdn_fwd", grid=(nc, H // GDN_HPS), in_specs=in_specs,
                  out_specs=[pl.BlockSpec((CHUNK, GDN_W), lambda n, h: (n, h)),
                             pl.BlockSpec((GDN_HPS, None, GDN_DK, GDN_DV), lambda n, h: (h, n, 0, 0))],
                  out_shape=[jax.ShapeDtypeStruct((S, H * GDN_DV), BF16),
                             jax.ShapeDtypeStruct((H, nc, GDN_DK, GDN_DV), F32)],
                  scratch_shapes=[pltpu.VMEM((H, GDN_DK, GDN_DV), F32)],
                  dims=("arbitrary", "arbitrary"), hosted=hosted)
    return _hosting(call, hosted, qkv, qkv, qkv, zin, zin, a_row, dt_row, norm_w)


def gdn_bwd(qkv, zin, a_row, dt_row, norm_w, states, dy, dy_col0, hosted=None):
    S = qkv.shape[0]
    nc = S // CHUNK
    H = GDN_HEADS
    n_, in_specs = _gdn_in_specs(True, nc)
    assert dy_col0 % GDN_HPS == 0
    in_specs = in_specs + [pl.BlockSpec((GDN_HPS, None, GDN_DK, GDN_DV), lambda n, h: (h, n_(n), 0, 0)),
                           pl.BlockSpec((CHUNK, GDN_W), lambda n, h: (n_(n), dy_col0 // GDN_HPS + h))]

    def body(q_ref, k_ref, v_ref, z_ref, ab_ref, a_ref, dt_ref, nw_ref, st_ref, dy_ref,
             dq_ref, dk_ref, dv_ref, dz_ref, dab_ref, da_ref, ddt_ref, dnw_ref, dstate):
        n, g = pl.program_id(0), pl.program_id(1)

        @pl.when((n == 0) & (g == 0))
        def _():
            da_ref[...] = jnp.zeros_like(da_ref)
            ddt_ref[...] = jnp.zeros_like(ddt_ref)
            dnw_ref[...] = jnp.zeros_like(dnw_ref)
            dstate[...] = jnp.zeros_like(dstate)

        @pl.when(g == 0)
        def _():
            dab_ref[...] = jnp.zeros_like(dab_ref)

        res = []
        for j in range(GDN_HPS):
            h = g * GDN_HPS + j
            sel_a, sel_b = _head_selectors(h)
            _, vjp = jax.vjp(lambda *a, sa=sel_a, sb=sel_b: _gdn_chunk(*a, sa, sb), _lanes(q_ref, j), _lanes(k_ref, j),
                             _lanes(v_ref, j), _lanes(z_ref, j), ab_ref[...], a_ref[...], dt_ref[...], nw_ref[...],
                             st_ref[j])
            res.append(vjp((_lanes(dy_ref, j).astype(F32), dstate[h])))
        for j, (dq, dk, dv, dz, dab, da, ddt, dnw, dst) in enumerate(res):
            cols = slice(j * LANES, (j + 1) * LANES)
            dq_ref[:, cols] = dq
            dk_ref[:, cols] = dk
            dv_ref[:, cols] = dv
            dz_ref[:, cols] = dz.astype(dz_ref.dtype)
            dstate[g * GDN_HPS + j] = dst
        dab_ref[...] += sum(r[4] for r in res)
        da_ref[...] += sum(r[5] for r in res)
        ddt_ref[...] += sum(r[6] for r in res)
        dnw_ref[...] += sum(r[7] for r in res)

    blk = pl.BlockSpec((CHUNK, GDN_W), lambda n, h: (n_(n), h))
    row = pl.BlockSpec((1, LANES), lambda n, h: (0, 0))
    wide = jax.ShapeDtypeStruct((S, H * LANES), F32)
    call = _pcall(body, name="gdn_bwd", grid=(nc, H // GDN_HPS), in_specs=in_specs,
                  out_specs=[blk, blk, blk, blk, pl.BlockSpec((CHUNK, LANES), lambda n, h: (n_(n), 0)), row, row, row],
                  out_shape=[wide, wide, wide, jax.ShapeDtypeStruct((S, H * LANES), BF16),
                             jax.ShapeDtypeStruct((S, LANES), F32)] + [jax.ShapeDtypeStruct((1, LANES), F32)] * 3,
                  scratch_shapes=[pltpu.VMEM((H, GDN_DK, GDN_DV), F32)],
                  dims=("arbitrary", "arbitrary"), hosted=hosted)
    return _hosting(call, hosted, qkv, qkv, qkv, zin, zin, a_row, dt_row, norm_w, states, dy)


def _rope_full(x, cos, sin):
    x1, x2 = x[:, :RET_DK // 2], x[:, RET_DK // 2:]
    return jnp.concatenate([x1 * cos - x2 * sin, x2 * cos + x1 * sin], axis=1)


RET_CHUNK = 512


def _ret_chunk(q, k, v, cos, sin, lg, state):
    C = q.shape[0]
    incl, _ = _tri_masks(C)
    qr = _rope_full(q, cos, sin)
    kr = _rope_full(k, cos, sin) * (RET_DK ** -0.5)
    r = lax.broadcasted_iota(jnp.int32, (C, C), 0)
    c = lax.broadcasted_iota(jnp.int32, (C, C), 1)
    dist = jnp.where(incl, (r - c).astype(F32), 0.0)
    lg1 = lg[:, :1]
    decay = jnp.where(incl, jnp.exp(dist * lg1), 0.0)
    pos = lax.broadcasted_iota(jnp.int32, (C, 1), 0).astype(F32)
    xi = jnp.exp((pos + 1.0) * lg1)
    zeta = jnp.exp((C - 1.0 - pos) * lg1)
    inner = _bdot(_bdot(qr, kr, _DN["nt"]) * decay, v)
    cross = _bdot(qr * xi, state)
    state_new = state * jnp.exp(C * lg1) + _bdot(kr * zeta, v, _DN["tn"])
    o = inner + cross
    mu = jnp.mean(o, axis=-1, keepdims=True)
    var = jnp.mean(jnp.square(o - mu), axis=-1, keepdims=True)
    return (o - mu) * lax.rsqrt(var + NORM_EPS), state_new


def _ret_log_gamma():
    lg = np.log1p(-np.power(2.0, -5.0 - np.arange(RET_HEADS, dtype=np.float64))).astype(np.float32)
    return jnp.asarray(np.broadcast_to(lg[:, None, None], (RET_HEADS, 1, LANES)).copy())


def _ret_in_specs(rev, nc):
    def n_(n):
        return nc - 1 - n if rev else n
    H = RET_HEADS
    return n_, [pl.BlockSpec((RET_CHUNK, RET_DK),lambda n, h: (n_(n), h)),
                pl.BlockSpec((RET_CHUNK, RET_DK),lambda n, h: (n_(n), H + h)),
                pl.BlockSpec((RET_CHUNK, RET_DV),lambda n, h: (n_(n), 2 * H * RET_DK // RET_DV + h)),
                pl.BlockSpec((RET_CHUNK, LANES), lambda n, h: (n_(n), 0)),
                pl.BlockSpec((RET_CHUNK, LANES), lambda n, h: (n_(n), 0)),
                pl.BlockSpec((None, 1, LANES), lambda n, h: (h, 0, 0))]


def ret_fwd(zz, cos, sin, hosted=None):
    S = zz.shape[0]
    nc = S // RET_CHUNK
    H = RET_HEADS
    _, in_specs = _ret_in_specs(False, nc)

    def body(q_ref, k_ref, v_ref, cos_ref, sin_ref, lg_ref, o_ref, st_ref, state):
        n, h = pl.program_id(0), pl.program_id(1)

        @pl.when(n == 0)
        def _():
            state[h] = jnp.zeros((RET_DK, RET_DV), F32)

        st = state[h]
        st_ref[...] = st
        o, st_new = _ret_chunk(q_ref[...], k_ref[...], v_ref[...], cos_ref[...], sin_ref[...], lg_ref[...], st)
        o_ref[...] = o
        state[h] = st_new

    call = _pcall(body, name="ret_fwd", grid=(nc, H), in_specs=in_specs,
                  out_specs=[pl.BlockSpec((RET_CHUNK, RET_DV),lambda n, h: (n, h)),
                             pl.BlockSpec((None, None, RET_DK, RET_DV), lambda n, h: (h, n, 0, 0))],
                  out_shape=[jax.ShapeDtypeStruct((S, H * RET_DV), F32),
                             jax.ShapeDtypeStruct((H, nc, RET_DK, RET_DV), F32)],
                  scratch_shapes=[pltpu.VMEM((H, RET_DK, RET_DV), F32)],
                  dims=("arbitrary", "arbitrary"), hosted=hosted)
    return _hosting(call, hosted, zz, zz, zz, cos, sin, _ret_log_gamma())


def ret_bwd(zz, cos, sin, states, do):
    S = zz.shape[0]
    nc = S // RET_CHUNK
    H = RET_HEADS
    n_, in_specs = _ret_in_specs(True, nc)
    in_specs = in_specs + [pl.BlockSpec((None, None, RET_DK, RET_DV), lambda n, h: (h, n_(n), 0, 0)),
                           pl.BlockSpec((RET_CHUNK, RET_DV),lambda n, h: (n_(n), h))]

    def body(q_ref, k_ref, v_ref, cos_ref, sin_ref, lg_ref, st_ref, do_ref, dq_ref, dk_ref, dv_ref, dstate):
        n, h = pl.program_id(0), pl.program_id(1)

        @pl.when(n == 0)
        def _():
            dstate[h] = jnp.zeros((RET_DK, RET_DV), F32)

        cos, sin, lg = cos_ref[...], sin_ref[...], lg_ref[...]
        _, vjp = jax.vjp(lambda q, k, v, st: _ret_chunk(q, k, v, cos, sin, lg, st),
                         q_ref[...], k_ref[...], v_ref[...], st_ref[...])
        dq, dk, dv, dst = vjp((do_ref[...], dstate[h]))
        dq_ref[...] = dq.astype(dq_ref.dtype)
        dk_ref[...] = dk.astype(dk_ref.dtype)
        dv_ref[...] = dv.astype(dv_ref.dtype)
        dstate[h] = dst

    return _pcall(body, name="ret_bwd", grid=(nc, H), in_specs=in_specs,
                  out_specs=[pl.BlockSpec((RET_CHUNK, RET_DK),lambda n, h: (n_(n), h)),
                             pl.BlockSpec((RET_CHUNK, RET_DK),lambda n, h: (n_(n), h)),
                             pl.BlockSpec((RET_CHUNK, RET_DV),lambda n, h: (n_(n), h))],
                  out_shape=[jax.ShapeDtypeStruct((S, H * RET_DK), BF16), jax.ShapeDtypeStruct((S, H * RET_DK), BF16),
                             jax.ShapeDtypeStruct((S, H * RET_DV), BF16)],
                  scratch_shapes=[pltpu.VMEM((H, RET_DK, RET_DV), F32)],
                  dims=("arbitrary", "arbitrary"))(zz, zz, zz, cos, sin, _ret_log_gamma(), states, do)


MLA_SCALE = (MLA_NOPE + MLA_ROPE) ** -0.5
NEG = -1e30


def _mla_scores(q, kn, kpe, diagonal):
    s = (lax.dot_general(q[:, :LANES], kn, _DN["nt"], preferred_element_type=F32)
         + lax.dot_general(q[:, LANES:], kpe, _DN["nt"], preferred_element_type=F32)) * MLA_SCALE
    if diagonal:
        row = lax.broadcasted_iota(jnp.int32, s.shape, 0)
        col = lax.broadcasted_iota(jnp.int32, s.shape, 1)
        s = jnp.where(col <= row, s, NEG)
    return s


def _on_and_below_diagonal(i, j, step):
    @pl.when(j < i)
    def _():
        step(False)

    @pl.when(j == i)
    def _():
        step(True)


FLASH_T = 1024


def flash_fwd(qr, kv, kpe, *, t=FLASH_T, hosted=None):
    S = qr.shape[0]
    t = min(t, S)
    nb = S // t
    H = MLA_HEADS

    def body(q_ref, kn_ref, v_ref, kpe_ref, o_ref, lse_ref, m_s, l_s, acc):
        i, j = pl.program_id(1), pl.program_id(2)

        @pl.when(j == 0)
        def _():
            m_s[...] = jnp.full_like(m_s, NEG)
            l_s[...] = jnp.zeros_like(l_s)
            acc[...] = jnp.zeros_like(acc)

        def step(diagonal):
            s = _mla_scores(q_ref[...], kn_ref[...], kpe_ref[...], diagonal)
            m_new = jnp.maximum(m_s[...], jnp.max(s, axis=-1, keepdims=True))
            p = jnp.exp(s - m_new)
            alpha = jnp.exp(m_s[...] - m_new)
            l_s[...] = alpha * l_s[...] + jnp.sum(p, axis=-1, keepdims=True)
            acc[...] = alpha * acc[...] + _bdot(p, v_ref[...])
            m_s[...] = m_new

        _on_and_below_diagonal(i, j, step)

        @pl.when(j == nb - 1)
        def _():
            o_ref[...] = (acc[...] / l_s[...]).astype(o_ref.dtype)
            lse_ref[...] = m_s[...] + jnp.log(l_s[...])

    kmap = lambda off: (lambda h, i, j: (jnp.minimum(j, i), off + h))
    call = _pcall(body, name="mla_flash_fwd", grid=(H, nb, nb),
                  in_specs=[pl.BlockSpec((t, 2 * LANES), lambda h, i, j: (i, h)),
                            pl.BlockSpec((t, LANES), kmap(0)), pl.BlockSpec((t, LANES), kmap(H)),
                            pl.BlockSpec((t, LANES), lambda h, i, j: (jnp.minimum(j, i), 0))],
                  out_specs=[pl.BlockSpec((t, LANES), lambda h, i, j: (i, h)),
                             pl.BlockSpec((None, t, 1), lambda h, i, j: (h, i, 0))],
                  out_shape=[jax.ShapeDtypeStruct((S, H * MLA_V), BF16), jax.ShapeDtypeStruct((H, S, 1), F32)],
                  scratch_shapes=[pltpu.VMEM((t, 1), F32), pltpu.VMEM((t, 1), F32), pltpu.VMEM((t, MLA_V), F32)],
                  dims=("parallel", "parallel", "arbitrary"), hosted=hosted)
    return _hosting(call, hosted, qr, kv, kv, kpe)


def _mla_p_ds(q, kn, v, kpe, do, o, lse, diagonal):
    p = jnp.exp(_mla_scores(q, kn, kpe, diagonal) - lse)
    dof = do.astype(F32)
    delta = jnp.sum(dof * o.astype(F32), axis=-1, keepdims=True)
    dp = lax.dot_general(do.astype(BF16), v, _DN["nt"], preferred_element_type=F32)
    ds = p * (dp - delta) * MLA_SCALE
    return p, ds


def flash_bwd_dq(qr, kv, kpe, o, lse, dy, dy_col0, *, t=FLASH_T, hosted=None):
    S = qr.shape[0]
    t = min(t, S)
    nb = S // t
    H = MLA_HEADS

    def body(q_ref, kn_ref, v_ref, kpe_ref, o_ref, lse_ref, do_ref, dq_ref, acc):
        i, j = pl.program_id(1), pl.program_id(2)

        @pl.when(j == 0)
        def _():
            acc[...] = jnp.zeros_like(acc)

        def step(diagonal):
            _, ds = _mla_p_ds(q_ref[...], kn_ref[...], v_ref[...], kpe_ref[...], do_ref[...], o_ref[...],
                              lse_ref[...], diagonal)
            acc[...] += jnp.concatenate([_bdot(ds, kn_ref[...]), _bdot(ds, kpe_ref[...])], axis=1)

        _on_and_below_diagonal(i, j, step)

        @pl.when(j == nb - 1)
        def _():
            dq_ref[...] = acc[...]

    kmap = lambda off: (lambda h, i, j: (jnp.minimum(j, i), off + h))
    call = _pcall(body, name="mla_flash_dq", grid=(H, nb, nb),
                  in_specs=[pl.BlockSpec((t, 2 * LANES), lambda h, i, j: (i, h)),
                            pl.BlockSpec((t, LANES), kmap(0)), pl.BlockSpec((t, LANES), kmap(H)),
                            pl.BlockSpec((t, LANES), lambda h, i, j: (jnp.minimum(j, i), 0)),
                            pl.BlockSpec((t, LANES), lambda h, i, j: (i, h)),
                            pl.BlockSpec((None, t, 1), lambda h, i, j: (h, i, 0)),
                            pl.BlockSpec((t, LANES), lambda h, i, j: (i, dy_col0 + h))],
                  out_specs=pl.BlockSpec((t, 2 * LANES), lambda h, i, j: (i, h)),
                  out_shape=jax.ShapeDtypeStruct((S, H * 2 * LANES), F32),
                  scratch_shapes=[pltpu.VMEM((t, 2 * LANES), F32)],
                  dims=("parallel", "parallel", "arbitrary"), hosted=hosted)
    return _hosting(call, hosted, qr, kv, kv, kpe, o, lse, dy)


def flash_bwd_dkv(qr, kv, kpe, o, lse, dy, dy_col0, *, t=FLASH_T, hosted=None):
    S = qr.shape[0]
    t = min(t, S)
    nb = S // t
    H = MLA_HEADS

    def body(q_ref, kn_ref, v_ref, kpe_ref, o_ref, lse_ref, do_ref, dkn_ref, dv_ref, dkpe_ref, akn, av):
        j, h, i = pl.program_id(0), pl.program_id(1), pl.program_id(2)

        @pl.when(i == 0)
        def _():
            akn[...] = jnp.zeros_like(akn)
            av[...] = jnp.zeros_like(av)

        @pl.when((i == 0) & (h == 0))
        def _():
            dkpe_ref[...] = jnp.zeros_like(dkpe_ref)

        def step(diagonal):
            q = q_ref[...]
            p, ds = _mla_p_ds(q, kn_ref[...], v_ref[...], kpe_ref[...], do_ref[...], o_ref[...], lse_ref[...],
                              diagonal)
            av[...] += _bdot(p, do_ref[...], _DN["tn"])
            akn[...] += _bdot(ds, q[:, :LANES], _DN["tn"])
            dkpe_ref[...] += _bdot(ds, q[:, LANES:], _DN["tn"])

        _on_and_below_diagonal(i, j, step)

        @pl.when(i == nb - 1)
        def _():
            dkn_ref[...] = akn[...].astype(dkn_ref.dtype)
            dv_ref[...] = av[...].astype(dv_ref.dtype)

    qmap = lambda off: (lambda j, h, i: (jnp.maximum(i, j), off + h))
    call = _pcall(
        body, name="mla_flash_dkv", grid=(nb, H, nb),
        in_specs=[pl.BlockSpec((t, 2 * LANES), qmap(0)),
                  pl.BlockSpec((t, LANES), lambda j, h, i: (j, h)), pl.BlockSpec((t, LANES), lambda j, h, i: (j, H + h)),
                  pl.BlockSpec((t, LANES), lambda j, h, i: (j, 0)),
                  pl.BlockSpec((t, LANES), qmap(0)),
                  pl.BlockSpec((None, t, 1), lambda j, h, i: (h, jnp.maximum(i, j), 0)),
                  pl.BlockSpec((t, LANES), qmap(dy_col0))],
        out_specs=[pl.BlockSpec((t, LANES), lambda j, h, i: (j, h)), pl.BlockSpec((t, LANES), lambda j, h, i: (j, h)),
                   pl.BlockSpec((t, LANES), lambda j, h, i: (j, 0))],
        out_shape=[jax.ShapeDtypeStruct((S, H * LANES), BF16), jax.ShapeDtypeStruct((S, H * LANES), BF16),
                   jax.ShapeDtypeStruct((S, LANES), F32)],
        scratch_shapes=[pltpu.VMEM((t, LANES), F32), pltpu.VMEM((t, LANES), F32)],
        dims=("arbitrary", "arbitrary", "arbitrary"), hosted=hosted)
    (dkn, dv, dkpe), extra = _hosting(call, hosted, qr, kv, kv, kpe, o, lse, dy)
    return (jnp.concatenate([dkn, dv], axis=1), dkpe), extra


def loss_head(h, target, g, *, tr=256):
    S, D = h.shape
    tr = min(tr, S)

    def body(h_ref, t_ref, g_ref, loss_ref, dh_ref, dg_ref):
        tgt = t_ref[...]

        def f(hh, gg):
            err = jnp.square(_rms(hh, gg) - tgt)
            per_row = jnp.sum(err, axis=-1, keepdims=True) * (0.5 / D)
            return jnp.sum(per_row, axis=0, keepdims=True)

        val, vjp = jax.vjp(f, h_ref[...], g_ref[...])
        dh, dg = vjp(jnp.ones((1, 1), F32))
        dh_ref[...] = dh

        @pl.when(pl.program_id(0) == 0)
        def _():
            loss_ref[...] = jnp.zeros_like(loss_ref)
            dg_ref[...] = jnp.zeros_like(dg_ref)

        loss_ref[...] += jnp.broadcast_to(val, loss_ref.shape)
        dg_ref[...] += dg

    return _pcall(body, name="loss_head", grid=(S // tr,),
                  in_specs=[_row_spec(tr, D, 0), _row_spec(tr, D, 0), _full_spec(g)],
                  out_specs=[pl.BlockSpec((1, LANES), lambda i: (0, 0)), _row_spec(tr, D, 0), _full_spec(g)],
                  out_shape=[jax.ShapeDtypeStruct((1, LANES), F32), jax.ShapeDtypeStruct((S, D), F32),
                             jax.ShapeDtypeStruct(g.shape, F32)],
                  dims=("arbitrary",))(h, target, g)


def _rope_tables(positions, dim):
    inv_freq = ROPE_THETA ** (-jnp.arange(0, dim, 2, dtype=F32) / dim)
    ang = positions.astype(F32)[:, None] * inv_freq
    return jnp.cos(ang), jnp.sin(ang)


def _pad_cols(w, n):
    return jnp.pad(w, ((0, 0), (0, n - w.shape[1])))


def _prep_w_in0(w):
    return jnp.concatenate([w[:, :4096], w[:, 4112:5136], _pad_cols(w[:, 5136:5200], LANES),
                            _pad_cols(w[:, 4096:4112], LANES)], axis=1)


def _unprep_w_in0(g):
    return jnp.concatenate([g[:, :4096], g[:, ZIN_AB:ZIN_AB + 16], g[:, ZIN_CQ:ZIN_KR], g[:, ZIN_KR:ZIN_KR + MLA_ROPE]],
                           axis=1)


def _prep_w_uq(w):
    w = w.reshape(MLA_Q_RANK, MLA_HEADS, MLA_NOPE + MLA_ROPE)
    w = jnp.pad(w, ((0, 0), (0, 0), (0, 2 * LANES - MLA_NOPE - MLA_ROPE)))
    return w.reshape(MLA_Q_RANK, MLA_HEADS * 2 * LANES)


def _unprep_w_uq(g):
    g = g.reshape(MLA_Q_RANK, MLA_HEADS, 2 * LANES)[:, :, :MLA_NOPE + MLA_ROPE]
    return g.reshape(MLA_Q_RANK, MLA_HEADS * (MLA_NOPE + MLA_ROPE))


def _prep_w_ukv(w):
    w = w.reshape(MLA_KV_RANK, MLA_HEADS, 2, LANES)
    return jnp.transpose(w, (0, 2, 1, 3)).reshape(MLA_KV_RANK, 2 * MLA_HEADS * LANES)


def _unprep_w_ukv(g):
    g = g.reshape(MLA_KV_RANK, 2, MLA_HEADS, LANES)
    return jnp.transpose(g, (0, 2, 1, 3)).reshape(MLA_KV_RANK, 2 * MLA_HEADS * LANES)


def _row(v, n=None):
    v = v.reshape(1, -1).astype(F32)
    return v if n is None else _pad_cols(v, n)


def _ffn_fwd(h, norm_g, w_up, conv_w, conv_b, w_down, tag, hosted=None):
    (hn,) = rowwise(_fn_rms, [(h, D_MODEL, 0)], [norm_g], [], [(D_MODEL, BF16)], name=f"{tag}_ffn_norm")
    u = matmul(hn, w_up, "nn", F32, b_shards=N_CHIPS, **TILES["wide_nn"], name=f"{tag}_ffn_up", hosted=hosted)
    u, got = u if hosted is not None else (u, [])
    f = ffn_conv_fwd(u, conv_w, conv_b)
    h_out = matmul(f, w_down, "nn", F32, add=h, tm=512, tn=1024, tk=8192, name=f"{tag}_ffn_down")
    return h_out, (hn, u, f), got


def _ffn_bwd(dh, dh16, h, norm_g, w_up, conv_w, conv_b, w_down, saved, tag):
    hn, u, f = saved
    df = matmul(dh16, w_down, "nt", BF16, tm=512, tn=2816, name=f"{tag}_ffn_down_dx")
    g_down = matmul(f, dh16, "tn", BF16, **TILES["dw"], name=f"{tag}_ffn_down_dw")
    dc, g_conv_w, g_conv_b = ffn_conv_bwd(u, conv_w, conv_b, df)
    du = conv_transpose(dc, conv_w, BF16, name=f"{tag}_ffn_conv_dx")
    g_up = matmul(hn, du, "tn", BF16, out_shards=N_CHIPS, tm=1024, tn=1408, tk=4096, name=f"{tag}_ffn_up_dw")
    dh_in, dh_in16, g_norm = matmul(du, w_up, "nt", F32, b_shards=N_CHIPS, tm=256, tn=2048, tk=2816,
                                    name=f"{tag}_ffn_up_dx", norm_bwd=(h, norm_g, dh))
    return dh_in, dh_in16, dict(ffn_norm=g_norm, ffn_w_up=g_up, ffn_conv_w=g_conv_w, ffn_conv_b=g_conv_b,
                                ffn_w_down=g_down)


TILES = {"wide_nn": dict(tm=512, tn=3072, tk=2048),
         "square": dict(tm=512, tn=2048, tk=2048),
         "dw": dict(tm=512, tn=2048, tk=4096)}


def _ple_fwd(h, p_i, w_proj, gate_g, w_gate, tag):
    (hg,) = rowwise(_fn_rms, [(h, D_MODEL, 0)], [gate_g], [], [(D_MODEL, BF16)], name=f"{tag}_ple_norm")
    gl = matmul(hg, w_gate, "nn", F32, **TILES["square"], name=f"{tag}_ple_gate")
    pp = matmul(p_i, w_proj, "nn", F32, b_shards=N_CHIPS, name=f"{tag}_ple_proj")
    (h_out,) = rowwise(_fn_ple, [(h, D_MODEL, 0), (pp, D_MODEL, 0), (gl, D_MODEL, 0)], [], [], [(D_MODEL, F32)],
                       name=f"{tag}_ple_add")
    return h_out, (hg, gl, pp)


def _ple_bwd(dh, h, p_i, w_proj, gate_g, w_gate, saved, tag):
    hg, gl, pp = saved
    (dpp, dgl), _ = rowwise_bwd(_fn_ple_terms, [(pp, D_MODEL, 0), (gl, D_MODEL, 0)], [], [], [(dh, D_MODEL, 0)],
                                [BF16, BF16], name=f"{tag}_ple_add_bwd")
    g_proj = matmul(p_i, dpp, "tn", BF16, out_shards=N_CHIPS, name=f"{tag}_ple_proj_dw")
    g_gate = matmul(hg, dgl, "tn", BF16, **TILES["dw"], name=f"{tag}_ple_gate_dw")
    dh_in, dh_in16, g_norm = matmul(dgl, w_gate, "nt", F32, tm=256, tn=2048, tk=2048, name=f"{tag}_ple_gate_dx",
                                    norm_bwd=(h, gate_g, dh))
    return dh_in, dh_in16, dict(ple_proj=g_proj, ple_gate_norm=g_norm, ple_gate=g_gate)


def local_step(x, p, positions, target, slots, W, ex=None):
    S = x.shape[0]
    G = {}
    p0, p1 = p[0].astype(BF16), p[1].astype(BF16)
    W = dict(W)

    def use(names, bufs):
        for k, b in zip(names, bufs):
            r, c = b.shape[1:]
            W[k] = b.reshape(N_CHIPS * r, c) if k in ROW_SHARDED else (b if k in KEPT_SHARDED else _cols_to_full(b))

    def by_chip(names):
        out = []
        for k in names:
            r, c = slots[k].shape[1:]
            out.append(G[k].reshape(N_CHIPS, r, c) if k in ROW_SHARDED
                       else (G[k] if k in KEPT_SHARDED else _full_to_cols(G[k])))
        return out

    first = [slots[k] for k in GATHER_FIRST]
    use(GATHER_FIRST, ex.gather(first) if ex else first)

    cm, sm = _rope_tables(positions, MLA_ROPE)
    zeros = jnp.zeros((S, LANES - MLA_ROPE), F32)
    cosp = jnp.concatenate([cm, cm, zeros], axis=1)
    sinp = jnp.concatenate([sm, sm, zeros], axis=1)
    cr, sr = _rope_tables(positions, RET_DK)

    w_in0 = _prep_w_in0(W["l0_w_in"])
    a_row = _row(W["l0_gdn_A_log"], LANES)
    dt_row = _row(W["l0_gdn_dt_bias"], LANES)
    gdn_nw = _row(W["l0_gdn_norm"])
    n = {k: _row(W[k]) for k in ("l0_attn_norm", "l0_mla_q_norm", "l0_mla_kv_norm", "l0_ffn_norm",
                                 "l0_ple_gate_norm", "l1_attn_norm", "l1_ret_norm", "l1_ffn_norm",
                                 "l1_ple_gate_norm", "final_norm", "l0_ffn_conv_b", "l1_ffn_conv_b")}

    (hn0,) = rowwise(_fn_rms, [(x, D_MODEL, 0)], [n["l0_attn_norm"]], [], [(D_MODEL, BF16)], name="l0_attn_norm")
    zin = matmul(hn0, w_in0, "nn", F32, tm=512, tn=1792, name="l0_w_in")
    qkv = gdn_conv_fwd(zin, W["l0_gdn_conv"])
    layer0 = [slots[k] for k in GATHER_L0]
    (y_a, gdn_states), got = gdn_fwd(qkv, zin, a_row, dt_row, gdn_nw, hosted=hosted_gather(layer0) if ex else None)
    use(GATHER_L0, got if ex else layer0)
    w_uq = _prep_w_uq(W["l0_mla_w_uq"])
    w_ukv = _prep_w_ukv(W["l0_mla_w_ukv"])
    mla_rows = [(zin, MLA_Q_RANK, ZIN_CQ // MLA_Q_RANK), (zin, MLA_KV_RANK, ZIN_CKV // MLA_KV_RANK),
                (zin, LANES, ZIN_KR // LANES)]
    mla_nd = [(cosp, LANES, 0), (sinp, LANES, 0)]
    cqn, ckvn, kpe = rowwise(_fn_mla_pre, mla_rows, [n["l0_mla_q_norm"], n["l0_mla_kv_norm"]], mla_nd,
                             [(MLA_Q_RANK, BF16), (MLA_KV_RANK, BF16), (LANES, BF16)], name="mla_pre")
    q_lin = matmul(cqn, w_uq, "nn", F32, name="mla_w_uq")
    kv = matmul(ckvn, w_ukv, "nn", BF16, name="mla_w_ukv")
    (qr,) = rowwise(_fn_rope_q, [(q_lin, 2048, 0)], [], mla_nd, [(2048, BF16)],
                    name="mla_rope_q")
    layer1 = [slots[k] for k in GATHER_L1]
    (y_b, lse), got = flash_fwd(qr, kv, kpe, hosted=hosted_gather(layer1) if ex else None)
    use(GATHER_L1, got if ex else layer1)
    y_ab = jnp.concatenate([y_a, y_b], axis=1)
    h1 = matmul(y_ab, W["l0_w_out"], "nn", F32, add=x, **TILES["square"], name="l0_w_out")
    ffn_late = [slots[k] for k in GATHER_FFN]
    h2, ffn0, got = _ffn_fwd(h1, n["l0_ffn_norm"], W["l0_ffn_w_up"], W["l0_ffn_conv_w"], n["l0_ffn_conv_b"],
                             W["l0_ffn_w_down"], "l0", hosted=hosted_gather(ffn_late) if ex else None)
    use(GATHER_FFN, got if ex else ffn_late)
    h3, ple0 = _ple_fwd(h2, p0, W["l0_ple_proj"], n["l0_ple_gate_norm"], W["l0_ple_gate"], "l0")

    (hn1,) = rowwise(_fn_rms, [(h3, D_MODEL, 0)], [n["l1_attn_norm"]], [], [(D_MODEL, BF16)], name="l1_attn_norm")
    late = [slots[k] for k in GATHER_L1_IN]
    zz = matmul(hn1, W["l1_w_in"], "nn", F32, b_shards=N_CHIPS, **TILES["wide_nn"], name="l1_w_in",
                hosted=hosted_gather(late) if ex else None)
    zz, got = zz if ex else (zz, late)
    use(GATHER_L1_IN, got)
    (o_ret, ret_states), _ = ret_fwd(zz, cr, sr)
    gate_rows = [(zz, 4096, 2), (o_ret, 4096, 0)]
    (yg,) = rowwise(_fn_ret_gate, gate_rows, [n["l1_ret_norm"]], [], [(4096, BF16)], name="ret_gate")
    h4 = matmul(yg, W["l1_w_out"], "nn", F32, add=h3, tm=512, tn=2048, tk=4096, name="l1_w_out")
    h5, ffn1, _ = _ffn_fwd(h4, n["l1_ffn_norm"], W["l1_ffn_w_up"], W["l1_ffn_conv_w"], n["l1_ffn_conv_b"],
                           W["l1_ffn_w_down"], "l1")
    h6, ple1 = _ple_fwd(h5, p1, W["l1_ple_proj"], n["l1_ple_gate_norm"], W["l1_ple_gate"], "l1")

    loss_vec, dh, G["final_norm"] = loss_head(h6, target, n["final_norm"])

    dh, dh16, g = _ple_bwd(dh, h5, p1, W["l1_ple_proj"], n["l1_ple_gate_norm"], W["l1_ple_gate"], ple1, "l1")
    G.update({"l1_" + k: v for k, v in g.items()})
    dh, dh16, g = _ffn_bwd(dh, dh16, h4, n["l1_ffn_norm"], W["l1_ffn_w_up"], W["l1_ffn_conv_w"], n["l1_ffn_conv_b"],
                           W["l1_ffn_w_down"], ffn1, "l1")
    G.update({"l1_" + k: v for k, v in g.items()})

    dyg = matmul(dh16, W["l1_w_out"], "nt", F32, tm=512, tn=4096, name="l1_w_out_dx")
    G["l1_w_out"] = matmul(yg, dh16, "tn", BF16, **TILES["dw"], name="l1_w_out_dw")
    (dg, do_ret), (G["l1_ret_norm"],) = rowwise_bwd(_fn_ret_gate, gate_rows, [n["l1_ret_norm"]], [],
                                                   [(dyg, 4096, 0)], [BF16, F32], name="ret_gate_bwd")
    dq, dk, dv = ret_bwd(zz, cr, sr, ret_states, do_ret)
    dzz = jnp.concatenate([dq, dk, dv, dg], axis=1)
    G["l1_w_in"] = matmul(hn1, dzz, "tn", BF16, out_shards=N_CHIPS, tm=1024, tn=1536, tk=4096, name="l1_w_in_dw")
    dh, _, G["l1_attn_norm"] = matmul(dzz, W["l1_w_in"], "nt", F32, b_shards=N_CHIPS, tm=256, tn=2048, tk=3072,
                                      name="l1_w_in_dx", norm_bwd=(h3, n["l1_attn_norm"], dh))

    dh, dh16, g = _ple_bwd(dh, h2, p0, W["l0_ple_proj"], n["l0_ple_gate_norm"], W["l0_ple_gate"], ple0, "l0")
    G.update({"l0_" + k: v for k, v in g.items()})
    dh, dh16, g = _ffn_bwd(dh, dh16, h1, n["l0_ffn_norm"], W["l0_ffn_w_up"], W["l0_ffn_conv_w"], n["l0_ffn_conv_b"],
                           W["l0_ffn_w_down"], ffn0, "l0")
    G.update({"l0_" + k: v for k, v in g.items()})

    dy_ab = matmul(dh16, W["l0_w_out"], "nt", F32, **TILES["square"], name="l0_w_out_dx")
    G["l0_w_out"] = matmul(y_ab, dh16, "tn", BF16, **TILES["dw"], name="l0_w_out_dw")
    sums, landed = {}, {}
    if ex:
        sums.update(zip(REDUCE_L1, ex.pair_sums(REDUCE_L1, by_chip(REDUCE_L1))))
        late = REDUCE_DQ + REDUCE_L0
        sums.update(zip(late, ex.pair_sums(late, by_chip(late))))
    (dq, dk, dv, dz, dab, g_a, g_dt, G["l0_gdn_norm"]), got = gdn_bwd(
        qkv, zin, a_row, dt_row, gdn_nw, gdn_states, dy_ab, 0,
        hosted=hosted_scatter([sums[k] for k in REDUCE_L1]) if ex else None)
    landed.update(zip(REDUCE_L1, got))
    G["l0_gdn_A_log"], G["l0_gdn_dt_bias"] = g_a[:, :GDN_HEADS], g_dt[:, :GDN_HEADS]
    dpre, G["l0_gdn_conv"] = gdn_conv_bwd(zin, W["l0_gdn_conv"], jnp.concatenate([dq, dk, dv], axis=1))
    dqkv = conv_transpose(dpre, W["l0_gdn_conv"], BF16, name="gdn_conv_dx")
    dqr, got = flash_bwd_dq(qr, kv, kpe, y_b, lse, dy_ab, MLA_HEADS,
                            hosted=hosted_scatter([sums[k] for k in REDUCE_DQ]) if ex else None)
    landed.update(zip(REDUCE_DQ, got))
    (dkv, dkpe), got = flash_bwd_dkv(qr, kv, kpe, y_b, lse, dy_ab, MLA_HEADS,
                                     hosted=hosted_scatter([sums[k] for k in REDUCE_L0]) if ex else None)
    landed.update(zip(REDUCE_L0, got))
    (dq_lin,), _ = rowwise_bwd(_fn_rope_q, [(q_lin, 2048, 0)], [], mla_nd, [(dqr, 2048, 0)], [BF16],
                               name="mla_rope_q_bwd")
    G["l0_mla_w_uq"] = _unprep_w_uq(matmul(cqn, dq_lin, "tn", BF16, name="mla_w_uq_dw"))
    dcqn = matmul(dq_lin, w_uq, "nt", F32, name="mla_w_uq_dx")
    G["l0_mla_w_ukv"] = _unprep_w_ukv(matmul(ckvn, dkv, "tn", BF16, name="mla_w_ukv_dw"))
    dckvn = matmul(dkv, w_ukv, "nt", F32, name="mla_w_ukv_dx")
    (dcq, dckv, dkr), (G["l0_mla_q_norm"], G["l0_mla_kv_norm"]) = rowwise_bwd(
        _fn_mla_pre, mla_rows, [n["l0_mla_q_norm"], n["l0_mla_kv_norm"]], mla_nd,
        [(dcqn, MLA_Q_RANK, 0), (dckvn, MLA_KV_RANK, 0), (dkpe, LANES, 0)], [BF16, BF16, BF16], name="mla_pre_bwd")
    dzin = jnp.concatenate([dqkv, dz, dcq, dckv, dkr, dab.astype(BF16)], axis=1)
    G["l0_w_in"] = _unprep_w_in0(matmul(hn0, dzin, "tn", BF16, tm=512, tn=1792, tk=4096, name="l0_w_in_dw"))
    dhn = matmul(dzin, w_in0, "nt", F32, tm=512, tn=2048, tk=5376, name="l0_w_in_dx")
    (grad_x,), (G["l0_attn_norm"],) = rowwise_bwd(_fn_rms, [(x, D_MODEL, 0)], [n["l0_attn_norm"]], [],
                                                 [(dhn, D_MODEL, 0)], [F32], adds=[(dh, D_MODEL, 0)],
                                                 name="l0_attn_norm_bwd")
    small = {k: G[k] for k in SMALL}
    if not ex:
        return loss_vec[0, 0], grad_x, dict(zip(BIG, by_chip(BIG))), small
    sums.update(zip(REDUCE_LAST, ex.pair_sums(REDUCE_LAST, by_chip(REDUCE_LAST))))
    landed.update(zip(REDUCE_LAST, scatter_chips([sums[k] for k in REDUCE_LAST])))
    return loss_vec[0, 0], grad_x, ex.finish(sums, landed), small


HBM = pl.BlockSpec(memory_space=pltpu.HBM)
VMEM = pl.BlockSpec(memory_space=pltpu.VMEM)


def _place():
    return lax.axis_index("x"), lax.axis_index("y"), lax.axis_index("c")


def _other_chips(x, y):
    return [(1 - x, y), (x, 1 - y), (1 - x, 1 - y)]


def _comm_call(body, *, name, out_shape, in_specs, out_specs, scratch_shapes):
    return pl.pallas_call(body, name=name, out_shape=out_shape, in_specs=in_specs, out_specs=out_specs,
                          scratch_shapes=list(scratch_shapes),
                          compiler_params=pltpu.CompilerParams(vmem_limit_bytes=VMEM_LIMIT_MB << 20))


def _inplace_comm_call(body, bufs, *, name, n_sems):
    n = len(bufs)
    return pl.pallas_call(body, name=name, out_shape=[jax.ShapeDtypeStruct(b.shape, b.dtype) for b in bufs],
                          in_specs=[HBM] * n, out_specs=[HBM] * n, input_output_aliases={i: i for i in range(n)},
                          scratch_shapes=[pltpu.SemaphoreType.DMA((n_sems,)), pltpu.SemaphoreType.DMA((n_sems,))],
                          compiler_params=pltpu.CompilerParams(vmem_limit_bytes=VMEM_LIMIT_MB << 20))(*bufs)


def all_gather_chips(bufs):
    n_sems, start, finish = _gather_phase(len(bufs))
    n = len(bufs)

    def body(*refs):
        outs, send_sems, recv_sems = refs[n:2 * n], refs[2 * n], refs[2 * n + 1]
        start(None, outs, send_sems, recv_sems)
        finish(None, outs, send_sems, recv_sems)

    return _inplace_comm_call(body, bufs, name="all_gather_chips", n_sems=n_sems)


def _gather_phase(n):
    def plan(outs, send_sems, recv_sems):
        x, y, c = _place()

        def copy(w, k, chip, hc, to):
            half = outs[w].shape[1] // 2
            rows = outs[w].at[2 * chip[0] + chip[1], pl.ds(hc * half, half), :]
            return pltpu.make_async_remote_copy(src_ref=rows, dst_ref=rows, send_sem=send_sems.at[6 * w + k],
                                                recv_sem=recv_sems.at[6 * w + k], device_id=to, device_id_type=MESH)

        first = [[copy(w, k, (x, y), c, (*chip, c)) for k, chip in enumerate(_other_chips(x, y))] for w in range(n)]
        passed = [[copy(w, 3 + k, chip, c, (x, y, 1 - c)) for k, chip in enumerate(_other_chips(x, y))]
                  for w in range(n)]
        return copy, first, passed, (x, y, c)

    def start(_, outs, send_sems, recv_sems):
        _, first, _, _ = plan(outs, send_sems, recv_sems)
        for w in range(n):
            for cp in first[w]:
                cp.start()

    def finish(_, outs, send_sems, recv_sems):
        copy, first, passed, (x, y, c) = plan(outs, send_sems, recv_sems)
        chips = _other_chips(x, y)
        for w in range(n):
            for k, chip in enumerate(chips):
                copy(w, k, chip, c, (x, y, c)).wait_recv()
                passed[w][k].start()
        for w in range(n):
            for k, chip in enumerate(chips):
                copy(w, 3 + k, chip, 1 - c, (x, y, c)).wait_recv()
        for w in range(n):
            for cp in first[w] + passed[w]:
                cp.wait_send()

    return 6 * n, start, finish


def hosted_gather(bufs):
    n_sems, start, finish = _gather_phase(len(bufs))
    return Hosted(bufs, [jax.ShapeDtypeStruct(b.shape, b.dtype) for b in bufs], {i: i for i in range(len(bufs))},
                  n_sems, start, finish)


def pair_swap_halves(gs):
    n = len(gs)

    def body(*refs):
        g_refs, o_refs, send_sems, recv_sems = refs[:n], refs[n:2 * n], refs[2 * n], refs[2 * n + 1]
        x, y, c = _place()
        copies = []
        for w in range(n):
            half = g_refs[w].shape[1] // 2
            copies.append(pltpu.make_async_remote_copy(
                src_ref=g_refs[w].at[:, pl.ds((1 - c) * half, half), :], dst_ref=o_refs[w], send_sem=send_sems.at[w],
                recv_sem=recv_sems.at[w], device_id=(x, y, 1 - c), device_id_type=MESH))
        for cp in copies:
            cp.start()
        for cp in copies:
            cp.wait()

    return _comm_call(body, name="pair_swap_halves",
                      out_shape=[jax.ShapeDtypeStruct((N_CHIPS, g.shape[1] // 2, g.shape[2]), g.dtype) for g in gs],
                      in_specs=[HBM] * n, out_specs=[HBM] * n,
                      scratch_shapes=[pltpu.SemaphoreType.DMA((n,)), pltpu.SemaphoreType.DMA((n,))])(*gs)


def scatter_chips(ps):
    n = len(ps)
    n_sems, start, finish = _scatter_phase(n)

    def body(*refs):
        p_refs, o_refs, send_sems, recv_sems = refs[:n], refs[n:2 * n], refs[2 * n], refs[2 * n + 1]
        start(p_refs, o_refs, send_sems, recv_sems)
        finish(p_refs, o_refs, send_sems, recv_sems)

    return _comm_call(body, name="scatter_chips", out_shape=_scatter_shapes(ps), in_specs=[HBM] * n, out_specs=[HBM] * n,
                      scratch_shapes=[pltpu.SemaphoreType.DMA((n_sems,)), pltpu.SemaphoreType.DMA((n_sems,))])(*ps)


def _scatter_shapes(ps):
    return [jax.ShapeDtypeStruct((3,) + p.shape[1:], p.dtype) for p in ps]


def _scatter_phase(n):
    def copies(p_refs, o_refs, send_sems, recv_sems):
        x, y, c = _place()
        return [pltpu.make_async_remote_copy(src_ref=p_refs[w].at[2 * chip[0] + chip[1]], dst_ref=o_refs[w].at[k],
                                             send_sem=send_sems.at[3 * w + k], recv_sem=recv_sems.at[3 * w + k],
                                             device_id=(*chip, c), device_id_type=MESH)
                for w in range(n) for k, chip in enumerate(_other_chips(x, y))]

    def start(*refs):
        for cp in copies(*refs):
            cp.start()

    def finish(*refs):
        for cp in copies(*refs):
            cp.wait()

    return 3 * n, start, finish


def hosted_scatter(ps):
    n_sems, start, finish = _scatter_phase(len(ps))
    return Hosted(ps, _scatter_shapes(ps), {}, n_sems, start, finish)


def pair_join_halves(rs):
    n = len(rs)

    def body(*refs):
        outs, send_sems, recv_sems = refs[n:2 * n], refs[2 * n], refs[2 * n + 1]
        x, y, c = _place()
        copies = []
        for w in range(n):
            half = outs[w].shape[0] // 2
            rows = outs[w].at[pl.ds(c * half, half), :]
            copies.append(pltpu.make_async_remote_copy(src_ref=rows, dst_ref=rows, send_sem=send_sems.at[w],
                                                       recv_sem=recv_sems.at[w], device_id=(x, y, 1 - c),
                                                       device_id_type=MESH))
        for cp in copies:
            cp.start()
        for cp in copies:
            cp.wait()

    return _inplace_comm_call(body, rs, name="pair_join_halves", n_sems=n)


def all_reduce_small(v, name):
    n, L = v.shape
    n_dev = 8

    def body(v_ref, out_ref, buf, send_sems, recv_sems):
        x, y, c = _place()
        me = 4 * x + 2 * y + c
        buf[me] = v_ref[...]

        def copy(k, slot, peer):
            return pltpu.make_async_remote_copy(src_ref=v_ref, dst_ref=buf.at[slot], send_sem=send_sems.at[k],
                                                recv_sem=recv_sems.at[slot],
                                                device_id=(peer // 4, (peer // 2) % 2, peer % 2), device_id_type=MESH)

        sends = [copy(k - 1, me, (me + k) % n_dev) for k in range(1, n_dev)]
        for cp in sends:
            cp.start()
        for k in range(1, n_dev):
            src = (me + k) % n_dev
            copy(0, src, src).wait_recv()
        for cp in sends:
            cp.wait_send()
        acc = buf[0]
        for s in range(1, n_dev):
            acc = acc + buf[s]
        out_ref[...] = acc

    return _comm_call(body, name=name, out_shape=jax.ShapeDtypeStruct((n, L), v.dtype), in_specs=[VMEM], out_specs=VMEM,
                      scratch_shapes=[pltpu.VMEM((n_dev, n, L), v.dtype), pltpu.SemaphoreType.DMA((n_dev - 1,)),
                                      pltpu.SemaphoreType.DMA((n_dev,))])(v)


BF16_ROWS = 16
STREAM_BLOCK_BYTES = 4 << 20


def _rows_tile(n, row_bytes, budget=1 << 20, mult=SUBLANES):
    best = mult if n % mult == 0 else n
    for t in range(mult, n + 1, mult):
        if n % t == 0 and t * row_bytes <= budget:
            best = t
    return best


def _scalars(*vals):
    return jnp.stack([jnp.asarray(v, jnp.int32) for v in vals])


def cast_to_slot(w, chip, name):
    r, c = w.shape
    tb = _rows_tile(r, c * 4, budget=STREAM_BLOCK_BYTES, mult=BF16_ROWS)

    def body(s_ref, w_ref, o_ref):
        o_ref[...] = w_ref[...].astype(BF16)

    spec = pltpu.PrefetchScalarGridSpec(
        num_scalar_prefetch=1, grid=(r // tb,), in_specs=[pl.BlockSpec((tb, c), lambda i, s: (i, 0))],
        out_specs=pl.BlockSpec((None, tb, c), lambda i, s: (s[0], i, 0)))
    return pl.pallas_call(body, name=name, grid_spec=spec, out_shape=jax.ShapeDtypeStruct((N_CHIPS, r, c), BF16),
                          compiler_params=pltpu.CompilerParams(dimension_semantics=("parallel",)))(_scalars(chip), w)


def pair_add(g, got, c, name):
    _, r, w = g.shape
    half = r // 2
    tb = _rows_tile(half, w * 4, budget=STREAM_BLOCK_BYTES, mult=BF16_ROWS)
    nb = half // tb

    def body(c_ref, g_ref, got_ref, o_ref):
        o_ref[...] = (g_ref[...].astype(F32) + got_ref[...].astype(F32)).astype(o_ref.dtype)

    spec = pltpu.PrefetchScalarGridSpec(
        num_scalar_prefetch=1, grid=(N_CHIPS, nb),
        in_specs=[pl.BlockSpec((None, tb, w), lambda s, i, c_ref: (s, c_ref[0] * nb + i, 0)),
                  pl.BlockSpec((None, tb, w), lambda s, i, c_ref: (s, i, 0))],
        out_specs=pl.BlockSpec((None, tb, w), lambda s, i, c_ref: (s, i, 0)))
    return pl.pallas_call(body, name=name, grid_spec=spec, out_shape=jax.ShapeDtypeStruct((N_CHIPS, half, w), BF16),
                          compiler_params=pltpu.CompilerParams(dimension_semantics=("parallel", "parallel")))(
        _scalars(c), g, got)


def chip_add(p, got, chip, c, name):
    _, h, w = p.shape
    tb = _rows_tile(h, w * 4, budget=STREAM_BLOCK_BYTES, mult=BF16_ROWS)
    nb = h // tb

    def body(s_ref, p_ref, got_ref, o_ref):
        acc = p_ref[...].astype(F32)
        for k in range(3):
            acc = acc + got_ref[k].astype(F32)
        o_ref[...] = acc

    spec = pltpu.PrefetchScalarGridSpec(
        num_scalar_prefetch=1, grid=(nb,),
        in_specs=[pl.BlockSpec((None, tb, w), lambda i, s: (s[0], i, 0)),
                  pl.BlockSpec((3, tb, w), lambda i, s: (0, i, 0))],
        out_specs=pl.BlockSpec((tb, w), lambda i, s: (s[1] * nb + i, 0)))
    return pl.pallas_call(body, name=name, grid_spec=spec, out_shape=jax.ShapeDtypeStruct((2 * h, w), F32),
                          compiler_params=pltpu.CompilerParams(dimension_semantics=("parallel",)))(
        _scalars(chip, c), p, got)


def adamw(w, g, m, v, name):
    r, c = w.shape
    tr = _rows_tile(r, c * 4, budget=STREAM_BLOCK_BYTES // 2)

    def body(w_ref, g_ref, m_ref, v_ref, d_ref, m_out, v_out):
        gg = g_ref[...]
        m2 = ADAM_B1 * m_ref[...] + (1.0 - ADAM_B1) * gg
        v2 = ADAM_B2 * v_ref[...] + (1.0 - ADAM_B2) * jnp.square(gg)
        m_hat = m2 / (1.0 - ADAM_B1 ** ADAM_STEP)
        v_hat = v2 / (1.0 - ADAM_B2 ** ADAM_STEP)
        d_ref[...] = -ADAM_LR * (m_hat / (jnp.sqrt(v_hat) + ADAM_EPS) + ADAM_WD * w_ref[...])
        m_out[...] = m2
        v_out[...] = v2

    blk = pl.BlockSpec((tr, c), lambda i: (i, 0))
    return _pcall(body, name=name, grid=(r // tr,), in_specs=[blk] * 4, out_specs=[blk] * 3,
                  out_shape=[jax.ShapeDtypeStruct((r, c), F32)] * 3, dims=("parallel",))(w, g, m, v)


WEIGHTS = ["l0_attn_norm", "l0_w_in", "l0_gdn_conv", "l0_gdn_A_log", "l0_gdn_dt_bias", "l0_gdn_norm", "l0_mla_q_norm",
           "l0_mla_w_uq", "l0_mla_kv_norm", "l0_mla_w_ukv", "l0_w_out", "l0_ffn_norm", "l0_ffn_w_up", "l0_ffn_conv_w",
           "l0_ffn_conv_b", "l0_ffn_w_down", "l0_ple_proj", "l0_ple_gate_norm", "l0_ple_gate", "l1_attn_norm",
           "l1_w_in", "l1_ret_norm", "l1_w_out", "l1_ffn_norm", "l1_ffn_w_up", "l1_ffn_conv_w", "l1_ffn_conv_b",
           "l1_ffn_w_down", "l1_ple_proj", "l1_ple_gate_norm", "l1_ple_gate", "final_norm"]
COL_SHARDED = ["l0_w_in", "l0_mla_w_uq", "l0_mla_w_ukv", "l0_ffn_w_up", "l0_ple_proj", "l1_w_in", "l1_ffn_w_up",
               "l1_ple_proj"]
ROW_SHARDED = ["l0_w_out", "l0_ffn_w_down", "l0_ple_gate", "l1_w_out", "l1_ffn_w_down", "l1_ple_gate"]
BIG = [k for k in WEIGHTS if k in COL_SHARDED or k in ROW_SHARDED]
SMALL_SHARDED = ["l0_gdn_conv", "l0_ffn_conv_w", "l1_ffn_conv_w"]
SMALL = [k for k in WEIGHTS if k not in BIG]
KEPT_SHARDED = ["l0_ffn_w_up", "l0_ple_proj", "l1_w_in", "l1_ffn_w_up", "l1_ple_proj"]
GATHER_FIRST = ["l0_w_in"]
GATHER_L0 = ["l0_mla_w_uq", "l0_mla_w_ukv", "l0_w_out", "l0_ffn_w_up", "l0_ffn_w_down", "l0_ple_proj", "l0_ple_gate",
             "l1_w_out"]
GATHER_L1 = ["l1_w_in"]
GATHER_FFN = ["l1_ffn_w_down", "l1_ple_proj", "l1_ple_gate"]
GATHER_L1_IN = ["l1_ffn_w_up"]
REDUCE_L1 = [k for k in BIG if k.startswith("l1_")]
REDUCE_DQ = ["l0_ffn_w_up"]
REDUCE_L0 = ["l0_ffn_w_down", "l0_ple_proj", "l0_ple_gate", "l0_w_out"]
REDUCE_LAST = ["l0_w_in", "l0_mla_w_uq", "l0_mla_w_ukv"]


class Exchange:
    def __init__(self, chip, core):
        self.chip, self.core = chip, core

    def gather(self, bufs):
        return all_gather_chips(bufs)

    def pair_sums(self, names, grads):
        return [pair_add(g, got, self.core, "rs_pair_add_" + k)
                for k, g, got in zip(names, grads, pair_swap_halves(grads))]

    def finish(self, sums, landed):
        halves = [chip_add(sums[k], landed[k], self.chip, self.core, "rs_chip_add_" + k) for k in BIG]
        return dict(zip(BIG, pair_join_halves(halves)))


def _cols_to_full(s):
    j, k, n = s.shape
    return jnp.transpose(s, (1, 0, 2)).reshape(k, j * n)


def _full_to_cols(g):
    k, n4 = g.shape
    return jnp.transpose(g.reshape(k, N_CHIPS, n4 // N_CHIPS), (1, 0, 2))


def _pack_small(vals):
    flat = jnp.concatenate([v.astype(F32).reshape(-1) for v in vals])
    align = SUBLANES * LANES
    flat = jnp.pad(flat, (0, -flat.shape[0] % align))
    return flat.reshape(-1, LANES)


def _unpack_small(rows, shapes):
    flat = rows.reshape(-1)
    out, off = [], 0
    for shp in shapes:
        n = int(np.prod(shp))
        out.append(flat[off:off + n].reshape(shp))
        off += n
    return out


INPUTS = (["x", "p", "positions"] + WEIGHTS + ["loss_target"] + ["m_" + k for k in WEIGHTS]
          + ["v_" + k for k in WEIGHTS])


def kernel(
        x, p, positions, l0_attn_norm, l0_w_in, l0_gdn_conv, l0_gdn_A_log, l0_gdn_dt_bias, l0_gdn_norm, l0_mla_q_norm,
        l0_mla_w_uq, l0_mla_kv_norm, l0_mla_w_ukv, l0_w_out, l0_ffn_norm, l0_ffn_w_up, l0_ffn_conv_w, l0_ffn_conv_b,
        l0_ffn_w_down, l0_ple_proj, l0_ple_gate_norm, l0_ple_gate, l1_attn_norm, l1_w_in, l1_ret_norm, l1_w_out,
        l1_ffn_norm, l1_ffn_w_up, l1_ffn_conv_w, l1_ffn_conv_b, l1_ffn_w_down, l1_ple_proj, l1_ple_gate_norm,
        l1_ple_gate, final_norm, loss_target, m_l0_attn_norm, m_l0_w_in, m_l0_gdn_conv, m_l0_gdn_A_log,
        m_l0_gdn_dt_bias, m_l0_gdn_norm, m_l0_mla_q_norm, m_l0_mla_w_uq, m_l0_mla_kv_norm, m_l0_mla_w_ukv, m_l0_w_out,
        m_l0_ffn_norm, m_l0_ffn_w_up, m_l0_ffn_conv_w, m_l0_ffn_conv_b, m_l0_ffn_w_down, m_l0_ple_proj,
        m_l0_ple_gate_norm, m_l0_ple_gate, m_l1_attn_norm, m_l1_w_in, m_l1_ret_norm, m_l1_w_out, m_l1_ffn_norm,
        m_l1_ffn_w_up, m_l1_ffn_conv_w, m_l1_ffn_conv_b, m_l1_ffn_w_down, m_l1_ple_proj, m_l1_ple_gate_norm,
        m_l1_ple_gate, m_final_norm, v_l0_attn_norm, v_l0_w_in, v_l0_gdn_conv, v_l0_gdn_A_log, v_l0_gdn_dt_bias,
        v_l0_gdn_norm, v_l0_mla_q_norm, v_l0_mla_w_uq, v_l0_mla_kv_norm, v_l0_mla_w_ukv, v_l0_w_out, v_l0_ffn_norm,
        v_l0_ffn_w_up, v_l0_ffn_conv_w, v_l0_ffn_conv_b, v_l0_ffn_w_down, v_l0_ple_proj, v_l0_ple_gate_norm,
        v_l0_ple_gate, v_l1_attn_norm, v_l1_w_in, v_l1_ret_norm, v_l1_w_out, v_l1_ffn_norm, v_l1_ffn_w_up,
        v_l1_ffn_conv_w, v_l1_ffn_conv_b, v_l1_ffn_w_down, v_l1_ple_proj, v_l1_ple_gate_norm, v_l1_ple_gate,
        v_final_norm):
    given = locals()
    a = {k: given[k] for k in INPUTS}
    x_i, y_i, c_i = _place()
    chip = 2 * x_i + y_i
    shard_shapes = {k: a[k].shape for k in WEIGHTS}

    slots = {k: cast_to_slot(a[k], chip, "cast_" + k) for k in BIG}
    W = {}
    placed = []
    for k in SMALL_SHARDED:
        r, c = shard_shapes[k]
        mine = jnp.where(c_i == 0, a[k], jnp.zeros_like(a[k]))
        placed.append(lax.dynamic_update_slice(jnp.zeros((r, N_CHIPS * c), F32), mine, (0, chip * c)))
    full_small = _unpack_small(all_reduce_small(_pack_small(placed), "gather_small_weights"),
                               [p_.shape for p_ in placed])
    for k in SMALL:
        W[k] = a[k]
    W.update(dict(zip(SMALL_SHARDED, full_small)))

    loss_part, grad_x, grads, G = local_step(a["x"][0], a["p"][:, 0], a["positions"][0], a["loss_target"][0], slots, W,
                                             Exchange(chip, c_i))
    loss = lax.psum(loss_part, ("x", "y", "c"))
    deltas, new_m, new_v = {}, {}, {}
    for k in BIG:
        deltas[k], new_m[k], new_v[k] = adamw(a[k], grads[k], a["m_" + k], a["v_" + k], "adamw_" + k)

    small_full = [G[k].reshape(-1) for k in SMALL]
    summed = _unpack_small(all_reduce_small(_pack_small(small_full), "reduce_small_grads"),
                           [G[k].shape for k in SMALL])
    for k, g in zip(SMALL, summed):
        if k in SMALL_SHARDED:
            r, c = shard_shapes[k]
            g = lax.dynamic_slice(g.reshape(r, N_CHIPS * c), (0, chip * c), (r, c))
        grads[k] = g.reshape(shard_shapes[k])
    packed = [_pack_small([d[k] for k in SMALL]) for d in (
        {k: a[k] for k in SMALL}, grads, {k: a["m_" + k] for k in SMALL}, {k: a["v_" + k] for k in SMALL})]
    outs = adamw(*packed, "adamw_small")
    shapes = [shard_shapes[k] for k in SMALL]
    for d, rows in zip((deltas, new_m, new_v), outs):
        d.update(dict(zip(SMALL, _unpack_small(rows, shapes))))

    return (loss, grad_x[None], *[grads[k] for k in WEIGHTS], *[deltas[k] for k in WEIGHTS],
            *[new_m[k] for k in WEIGHTS], *[new_v[k] for k in WEIGHTS])
```

```python
import functools
import math

import numpy as np
import jax
import jax.numpy as jnp
from jax import lax
from jax.experimental import pallas as pl
from jax.experimental.pallas import tpu as pltpu

F32, BF16 = jnp.float32, jnp.bfloat16
HI = lax.Precision.HIGHEST
MESH = pl.DeviceIdType.MESH

NORM_EPS = 1e-6
ROPE_THETA = 10000.0
D_MODEL = 2048
PLE_DIM = 256
GDN_HEADS, GDN_DK, GDN_DV, GDN_CONV = 8, 128, 128, 4
MLA_HEADS, MLA_Q_RANK, MLA_KV_RANK, MLA_NOPE, MLA_ROPE, MLA_V = 8, 512, 512, 128, 64, 128
RET_HEADS, RET_DK, RET_DV = 8, 256, 512
D_FF, FFN_CONV = 5632, 3
ADAM_LR, ADAM_B1, ADAM_B2, ADAM_EPS, ADAM_WD, ADAM_STEP = 0.001, 0.9, 0.999, 1e-08, 0.01, 10

LANES = 128
SUBLANES = 8
CHUNK = 128
N_CHIPS = 4
VMEM_LIMIT_MB = 56

ZIN_QKV, ZIN_Z, ZIN_CQ, ZIN_CKV, ZIN_KR, ZIN_AB, ZIN_W = 0, 3072, 4096, 4608, 5120, 5248, 5376


class Hosted:
    def __init__(self, inputs, out_shapes, aliases, n_sems, start, finish):
        self.inputs, self.out_shapes, self.aliases, self.n_sems = list(inputs), list(out_shapes), dict(aliases), n_sems
        self.start, self.finish = start, finish


def _pcall(body, *, name, out_shape, grid=(), in_specs=None, out_specs=None, scratch_shapes=(), dims=None,
           hosted=None):
    params = dict(vmem_limit_bytes=VMEM_LIMIT_MB << 20)
    if dims is not None:
        params["dimension_semantics"] = dims
    if hosted is None:
        return pl.pallas_call(body, name=name, out_shape=out_shape, grid=grid, in_specs=in_specs, out_specs=out_specs,
                              scratch_shapes=list(scratch_shapes), compiler_params=pltpu.CompilerParams(**params))
    single = not isinstance(out_shape, (list, tuple))
    out_shape = [out_shape] if single else list(out_shape)
    out_specs = [out_specs] if single else list(out_specs)
    n_in, n_out, n_scr = len(in_specs), len(out_shape), len(scratch_shapes)
    h_in, h_out = len(hosted.inputs), len(hosted.out_shapes)
    hbm = pl.BlockSpec(memory_space=pltpu.HBM)

    def hosting_body(*refs):
        ins, h_ins = refs[:n_in], refs[n_in:n_in + h_in]
        o0 = n_in + h_in
        outs, h_outs = refs[o0:o0 + n_out], refs[o0 + n_out:o0 + n_out + h_out]
        s0 = o0 + n_out + h_out
        scr, (send_sems, recv_sems) = refs[s0:s0 + n_scr], refs[s0 + n_scr:]
        ids = [pl.program_id(d) for d in range(len(grid))]
        first = functools.reduce(lambda u, v: u & v, [i == 0 for i in ids])
        last = functools.reduce(lambda u, v: u & v, [i == g - 1 for i, g in zip(ids, grid)])

        @pl.when(first)
        def _():
            hosted.start(h_ins, h_outs, send_sems, recv_sems)

        body(*ins, *outs, *scr)

        @pl.when(last)
        def _():
            hosted.finish(h_ins, h_outs, send_sems, recv_sems)

    params["dimension_semantics"] = ("arbitrary",) * len(grid)
    call = pl.pallas_call(
        hosting_body, name=name, out_shape=out_shape + hosted.out_shapes, grid=grid,
        in_specs=list(in_specs) + [hbm] * h_in, out_specs=out_specs + [hbm] * h_out,
        scratch_shapes=list(scratch_shapes) + [pltpu.SemaphoreType.DMA((hosted.n_sems,)),
                                               pltpu.SemaphoreType.DMA((hosted.n_sems,))],
        input_output_aliases={n_in + i: n_out + o for i, o in hosted.aliases.items()},
        compiler_params=pltpu.CompilerParams(**params))

    def run(*args):
        res = call(*args, *hosted.inputs)
        main = res[:n_out]
        return (main[0] if single else main), list(res[n_out:])

    return run


def _tile(n, target, mult=LANES):
    best = None
    for t in range(mult, min(n, target) + 1, mult):
        if n % t == 0:
            best = t
    return best or n


_DN = {"nn": (((1,), (0,)), ((), ())), "nt": (((1,), (1,)), ((), ())), "tn": (((0,), (0,)), ((), ()))}


def matmul(a, b, mode, out_dtype, *, name, add=None, b_shards=1, out_shards=1, tm=512, tn=1024, tk=2048,
           hosted=None, norm_bwd=None):
    bs = b.shape[-2:]
    if mode == "nn":
        (M, K), (K2, N) = a.shape, (bs[0], bs[1] * b_shards)
    elif mode == "nt":
        (M, K), (N, K2) = a.shape, (bs[0], bs[1] * b_shards)
    else:
        (K, M), (K2, N) = a.shape, bs
    assert K == K2, (name, a.shape, b.shape)
    n_sh = N // max(b_shards if mode == "nn" else 1, out_shards)
    k_sh = K // (b_shards if mode == "nt" else 1)
    tm, tn, tk = _tile(M, tm), _tile(n_sh, tn), _tile(k_sh, tk)
    nk = K // tk
    nbn, nbk = n_sh // tn, k_sh // tk
    dn = _DN[mode]
    has_add = add is not None
    a_bytes, b_bytes = a.size * a.dtype.itemsize, b.size * b.dtype.itemsize
    i_outer = nk > 1 or a_bytes + (M // tm) * b_bytes <= b_bytes + (N // tn) * a_bytes

    def ij(g0, g1):
        return (g0, g1) if i_outer else (g1, g0)

    fused_norm = norm_bwd is not None
    if fused_norm:
        assert tn == N and out_shards == 1 and not has_add and hosted is None, name
        i_outer = True

    def body(*refs):
        a_ref, b_ref = refs[:2]
        add_ref = refs[2] if has_add else None
        o_ref = refs[3 if has_add else 2]
        part = lax.dot_general(a_ref[...].astype(BF16), b_ref[...].astype(BF16), dn, preferred_element_type=F32)

        def finish(r):
            if fused_norm:
                h_ref, g_ref, dh_ref, o32_ref, o16_ref, dg_ref = refs[2:8]
                _, vjp = jax.vjp(_rms, h_ref[...], g_ref[...])
                dx, dg = vjp(r)
                out = dx + dh_ref[...]
                o32_ref[...] = out
                o16_ref[...] = out.astype(BF16)
                dg_ref[...] += dg
                return
            if has_add:
                r = r + add_ref[...]
            o_ref[...] = r.astype(out_dtype)

        if fused_norm:
            @pl.when((pl.program_id(0) == 0) & (pl.program_id(2) == 0))
            def _():
                refs[7][...] = jnp.zeros_like(refs[7])

        if nk == 1:
            finish(part)
            return
        acc = refs[-1]
        k = pl.program_id(2)

        @pl.when(k == 0)
        def _():
            acc[...] = part

        @pl.when(k > 0)
        def _():
            acc[...] += part

        @pl.when(k == nk - 1)
        def _():
            finish(acc[...])

    def spec(block, fn):
        return pl.BlockSpec(block, lambda g0, g1, k: fn(*ij(g0, g1), k))

    if mode == "tn":
        a_spec = spec((tk, tm), lambda i, j, k: (k, i))
    else:
        a_spec = spec((tm, tk), lambda i, j, k: (i, k))
    if mode == "nt":
        if b_shards > 1:
            b_spec = spec((None, tn, tk), lambda i, j, k: (k // nbk, j, k % nbk))
        else:
            b_spec = spec((tn, tk), lambda i, j, k: (j, k))
    elif b_shards > 1:
        b_spec = spec((None, tk, tn), lambda i, j, k: (j // nbn, k, j % nbn))
    else:
        b_spec = spec((tk, tn), lambda i, j, k: (k, j))
    in_specs = [a_spec, b_spec]
    args = [a, b]
    if has_add:
        in_specs.append(spec((tm, tn), lambda i, j, k: (i, j)))
        args.append(add)
    if out_shards > 1:
        out_spec = spec((None, tm, tn), lambda i, j, k: (j // nbn, i, j % nbn))
        out_shape = jax.ShapeDtypeStruct((out_shards, M, n_sh), out_dtype)
    else:
        out_spec = spec((tm, tn), lambda i, j, k: (i, j))
        out_shape = jax.ShapeDtypeStruct((M, N), out_dtype)
    gi, gj = M // tm, N // tn
    if fused_norm:
        h, gain, dh = norm_bwd
        row_blk = spec((tm, N), lambda i, j, k: (i, 0))
        gain_blk = spec((1, N), lambda i, j, k: (0, 0))
        return _pcall(body, name=name, grid=(gi, 1, nk), in_specs=in_specs + [row_blk, gain_blk, row_blk],
                      out_specs=[row_blk, row_blk, gain_blk],
                      out_shape=[jax.ShapeDtypeStruct((M, N), F32), jax.ShapeDtypeStruct((M, N), BF16),
                                 jax.ShapeDtypeStruct((1, N), F32)],
                      scratch_shapes=[pltpu.VMEM((tm, tn), F32)] if nk > 1 else [],
                      dims=("arbitrary", "arbitrary", "arbitrary"))(a, b, h, gain, dh)
    return _pcall(body, name=name, out_shape=out_shape, grid=(gi, gj, nk) if i_outer else (gj, gi, nk),
                  in_specs=in_specs, out_specs=out_spec,
                  scratch_shapes=[pltpu.VMEM((tm, tn), F32)] if nk > 1 else [],
                  dims=("parallel", "parallel", "arbitrary"), hosted=hosted)(*args)


def _row_spec(tr, w, c):
    return pl.BlockSpec((tr, w), lambda i: (i, c))


def _full_spec(arr):
    return pl.BlockSpec(arr.shape, lambda i: (0,) * arr.ndim)


def rowwise(fn, rows, params, nd_rows, outs, *, name, tr=256):
    S = rows[0][0].shape[0]
    tr = min(tr, S)
    n_in = len(rows) + len(params) + len(nd_rows)

    def body(*refs):
        res = fn(*[x[...] for x in refs[:n_in]])
        for o_ref, v in zip(refs[n_in:], res):
            o_ref[...] = v.astype(o_ref.dtype)

    return _pcall(body, name=name, grid=(S // tr,),
                  in_specs=([_row_spec(tr, w, c) for (_, w, c) in rows] + [_full_spec(q) for q in params]
                            + [_row_spec(tr, w, c) for (_, w, c) in nd_rows]),
                  out_specs=[_row_spec(tr, w, 0) for (w, _) in outs],
                  out_shape=[jax.ShapeDtypeStruct((S, w), dt) for (w, dt) in outs],
                  dims=("parallel",))(*[r[0] for r in rows], *params, *[r[0] for r in nd_rows])


def rowwise_bwd(fn, rows, params, nd_rows, cts, d_dtypes, *, name, adds=None, tr=256):
    S = rows[0][0].shape[0]
    tr = min(tr, S)
    n_r, n_p, n_n, n_c = len(rows), len(params), len(nd_rows), len(cts)
    adds = adds or [None] * n_r
    add_list = [a for a in adds if a is not None]
    n_a = len(add_list)
    d_dtypes = [dt if isinstance(dt, (list, tuple)) else (dt,) for dt in d_dtypes]
    n_d = sum(len(dt) for dt in d_dtypes)

    def body(*refs):
        it = iter(refs)
        r = [next(it)[...] for _ in range(n_r)]
        p = [next(it)[...] for _ in range(n_p)]
        nd = [next(it)[...] for _ in range(n_n)]
        c = [next(it)[...] for _ in range(n_c)]
        ad = [next(it)[...] for _ in range(n_a)]
        d_row_refs = [[next(it) for _ in dts] for dts in d_dtypes]
        d_par_refs = [next(it) for _ in range(n_p)]
        outs, vjp = jax.vjp(lambda *dp: fn(*dp, *nd), *r, *p)
        g = vjp(tuple(ci.astype(o.dtype) for ci, o in zip(c, outs)))
        ai = 0
        for k in range(n_r):
            gk = g[k].astype(F32)
            if adds[k] is not None:
                gk = gk + ad[ai].astype(F32)
                ai += 1
            for ref in d_row_refs[k]:
                ref[...] = gk.astype(ref.dtype)

        @pl.when(pl.program_id(0) == 0)
        def _():
            for ref in d_par_refs:
                ref[...] = jnp.zeros_like(ref)

        for k in range(n_p):
            d_par_refs[k][...] += g[n_r + k].astype(F32)

    in_specs = ([_row_spec(tr, w, c) for (_, w, c) in rows] + [_full_spec(q) for q in params]
                + [_row_spec(tr, w, c) for (_, w, c) in nd_rows] + [_row_spec(tr, w, c) for (_, w, c) in cts]
                + [_row_spec(tr, w, c) for (_, w, c) in add_list])
    out_specs = ([_row_spec(tr, w, 0) for (_, w, _), dts in zip(rows, d_dtypes) for _ in dts]
                 + [_full_spec(q) for q in params])
    out_shape = ([jax.ShapeDtypeStruct((S, w), dt) for (_, w, _), dts in zip(rows, d_dtypes) for dt in dts]
                 + [jax.ShapeDtypeStruct(q.shape, F32) for q in params])
    res = _pcall(body, name=name, grid=(S // tr,), in_specs=in_specs, out_specs=out_specs, out_shape=out_shape,
                 dims=("arbitrary",))(*[r[0] for r in rows], *params, *[r[0] for r in nd_rows],
                                      *[r[0] for r in cts], *[r[0] for r in add_list])
    d_rows, i = [], 0
    for dts in d_dtypes:
        d_rows.append(res[i] if len(dts) == 1 else tuple(res[i:i + len(dts)]))
        i += len(dts)
    return d_rows, res[n_d:]


def _rms(x, g):
    x = x.astype(F32)
    return x * lax.rsqrt(jnp.mean(x * x, axis=-1, keepdims=True) + NORM_EPS) * g


def _fn_rms(x, g):
    return (_rms(x, g),)


def _sigmoid(x):
    return 1.0 / (1.0 + jnp.exp(-x))


def _silu(x):
    return x * _sigmoid(x)


def _softplus(x):
    return jnp.maximum(x, 0.0) + jnp.log(1.0 + jnp.exp(-jnp.abs(x)))


def _fn_ple(h, pp, gl):
    return (h.astype(F32) + pp.astype(F32) * _sigmoid(gl.astype(F32)),)


def _fn_ple_terms(pp, gl):
    return (pp.astype(F32) * _sigmoid(gl.astype(F32)),)


def _rot_half_matrix():
    half = MLA_ROPE // 2
    r = lax.broadcasted_iota(jnp.int32, (LANES, LANES), 0)
    c = lax.broadcasted_iota(jnp.int32, (LANES, LANES), 1)
    plus = (c == r + half) & (r < half)
    minus = (r == c + half) & (c < half)
    return jnp.where(plus, 1.0, 0.0) - jnp.where(minus, 1.0, 0.0)


def _rope_pad(x, cosp, sinp):
    return x * cosp + jnp.dot(x, _rot_half_matrix(), precision=HI, preferred_element_type=F32) * sinp


def _fn_mla_pre(cq, ckv, kr, qn_w, kvn_w, cosp, sinp):
    return (_rms(cq, qn_w), _rms(ckv, kvn_w), _rope_pad(kr.astype(F32), cosp, sinp))


def _fn_rope_q(q, cosp, sinp):
    q = q.astype(F32)
    parts = []
    for h in range(MLA_HEADS):
        base = 2 * LANES * h
        parts.append(q[:, base:base + LANES])
        parts.append(_rope_pad(q[:, base + LANES:base + 2 * LANES], cosp, sinp))
    return (jnp.concatenate(parts, axis=1),)


def _fn_ret_gate(g, on, w):
    return (_silu(g.astype(F32)) * (on.astype(F32) * w),)


def _shift_down(cur, halo, s):
    if s == 0:
        return cur
    r = pltpu.roll(cur, s, 0)
    hs = pltpu.roll(halo, s, 0)
    row = lax.broadcasted_iota(jnp.int32, hs.shape, 0)
    first = jnp.where(row < s, hs, r[:SUBLANES])
    return jnp.concatenate([first, r[SUBLANES:]], axis=0)


def _shift_up(cur, halo, s):
    if s == 0:
        return cur
    n = cur.shape[0]
    r = pltpu.roll(cur, n - s, 0)
    hs = pltpu.roll(halo, SUBLANES - s, 0)
    row = lax.broadcasted_iota(jnp.int32, hs.shape, 0)
    last = jnp.where(row >= SUBLANES - s, hs, r[n - SUBLANES:])
    return jnp.concatenate([r[:n - SUBLANES], last], axis=0)


def _prev_halo_spec(tr, tw, col):
    return pl.BlockSpec((SUBLANES, tw), lambda c, i: (jnp.maximum(i * (tr // SUBLANES) - 1, 0), col(c)))


def _conv_taps(cur, halo, w_ref, width):
    taps = [_shift_down(cur, halo, width - 1 - j) for j in range(width)]
    y = taps[0] * w_ref[0:1, :]
    for j in range(1, width):
        y = y + taps[j] * w_ref[j:j + 1, :]
    return y, taps


def gdn_conv_fwd(zin, w, *, tr=512, tw=512):
    S = zin.shape[0]
    tr = min(tr, S)
    width, C = w.shape

    def body(cur_ref, halo_ref, w_ref, o_ref):
        i = pl.program_id(1)
        halo = halo_ref[...] * (i > 0).astype(F32)
        y, _ = _conv_taps(cur_ref[...], halo, w_ref, width)
        o_ref[...] = _silu(y)

    return _pcall(body, name="gdn_conv_fwd", grid=(C // tw, S // tr),
                  in_specs=[pl.BlockSpec((tr, tw), lambda c, i: (i, c)), _prev_halo_spec(tr, tw, lambda c: c),
                            pl.BlockSpec((width, tw), lambda c, i: (0, c))],
                  out_specs=pl.BlockSpec((tr, tw), lambda c, i: (i, c)),
                  out_shape=jax.ShapeDtypeStruct((S, C), F32), dims=("parallel", "arbitrary"))(zin, zin, w)


def gdn_conv_bwd(zin, w, dy, *, tr=512, tw=512):
    S = zin.shape[0]
    tr = min(tr, S)
    width, C = w.shape

    def body(cur_ref, halo_ref, w_ref, dy_ref, da_ref, dw_ref):
        i = pl.program_id(1)
        halo = halo_ref[...] * (i > 0).astype(F32)
        y, taps = _conv_taps(cur_ref[...], halo, w_ref, width)
        sg = _sigmoid(y)
        da = dy_ref[...] * (sg * (1.0 + y * (1.0 - sg)))
        da_ref[...] = da

        @pl.when(i == 0)
        def _():
            dw_ref[...] = jnp.zeros_like(dw_ref)

        for j in range(width):
            dw_ref[j:j + 1, :] += jnp.sum(da * taps[j], axis=0, keepdims=True)

    return _pcall(body, name="gdn_conv_bwd", grid=(C // tw, S // tr),
                  in_specs=[pl.BlockSpec((tr, tw), lambda c, i: (i, c)), _prev_halo_spec(tr, tw, lambda c: c),
                            pl.BlockSpec((width, tw), lambda c, i: (0, c)),
                            pl.BlockSpec((tr, tw), lambda c, i: (i, c))],
                  out_specs=[pl.BlockSpec((tr, tw), lambda c, i: (i, c)),
                             pl.BlockSpec((width, tw), lambda c, i: (0, c))],
                  out_shape=[jax.ShapeDtypeStruct((S, C), F32), jax.ShapeDtypeStruct((width, C), F32)],
                  dims=("parallel", "arbitrary"))(zin, zin, w, dy)


def conv_transpose(dy, w, out_dtype, *, name, tr=512, tw=512):
    if dy.ndim == 2:
        dy = dy[None]
    T, S, C = dy.shape
    tr = min(tr, S)
    width = w.shape[0]
    n_i, nc = S // tr, C // tw

    def body(cur_ref, halo_ref, w_ref, o_ref):
        i = pl.program_id(2)
        halo = halo_ref[...] * (i < n_i - 1).astype(F32)
        cur = cur_ref[...]
        acc = cur * w_ref[width - 1:width, :]
        for s in range(1, width):
            acc = acc + _shift_up(cur, halo, s) * w_ref[width - 1 - s:width - s, :]
        o_ref[...] = acc.astype(out_dtype)

    nxt = pl.BlockSpec((None, SUBLANES, tw),
                       lambda t, c, i: (t, jnp.minimum((i + 1) * (tr // SUBLANES), S // SUBLANES - 1), c))
    return _pcall(body, name=name, grid=(T, nc, n_i),
                  in_specs=[pl.BlockSpec((None, tr, tw), lambda t, c, i: (t, i, c)), nxt,
                            pl.BlockSpec((width, tw), lambda t, c, i: (0, t * nc + c))],
                  out_specs=pl.BlockSpec((tr, tw), lambda t, c, i: (i, t * nc + c)),
                  out_shape=jax.ShapeDtypeStruct((S, T * C), out_dtype),
                  dims=("parallel", "parallel", "arbitrary"))(dy, dy, w)


def ffn_conv_fwd(u, w, b, *, tr=512, tw=512):
    S, C2 = u.shape
    tr = min(tr, S)
    width = w.shape[0]
    half = C2 // 2
    nc = half // tw

    def body(g_ref, gh_ref, u_ref, uh_ref, wg_ref, wu_ref, bg_ref, bu_ref, o_ref):
        i = pl.program_id(1)
        live = (i > 0).astype(F32)
        yg, _ = _conv_taps(g_ref[...], gh_ref[...] * live, wg_ref, width)
        yu, _ = _conv_taps(u_ref[...], uh_ref[...] * live, wu_ref, width)
        o_ref[...] = (_silu(yg + bg_ref[...]) * (yu + bu_ref[...])).astype(o_ref.dtype)

    return _pcall(body, name="ffn_conv_fwd", grid=(nc, S // tr),
                  in_specs=[pl.BlockSpec((tr, tw), lambda c, i: (i, c)), _prev_halo_spec(tr, tw, lambda c: c),
                            pl.BlockSpec((tr, tw), lambda c, i: (i, c + nc)),
                            _prev_halo_spec(tr, tw, lambda c: c + nc),
                            pl.BlockSpec((width, tw), lambda c, i: (0, c)),
                            pl.BlockSpec((width, tw), lambda c, i: (0, c + nc)),
                            pl.BlockSpec((1, tw), lambda c, i: (0, c)), pl.BlockSpec((1, tw), lambda c, i: (0, c + nc))],
                  out_specs=pl.BlockSpec((tr, tw), lambda c, i: (i, c)),
                  out_shape=jax.ShapeDtypeStruct((S, half), BF16),
                  dims=("parallel", "arbitrary"))(u, u, u, u, w, w, b, b)


def ffn_conv_bwd(u, w, b, df, *, tr=512, tw=512):
    S, C2 = u.shape
    tr = min(tr, S)
    width = w.shape[0]
    half = C2 // 2
    nc = half // tw

    def body(g_ref, gh_ref, u_ref, uh_ref, wg_ref, wu_ref, bg_ref, bu_ref, df_ref, dc_ref, dw_ref, db_ref):
        i = pl.program_id(1)
        live = (i > 0).astype(F32)
        yg, gt = _conv_taps(g_ref[...], gh_ref[...] * live, wg_ref, width)
        yu, ut = _conv_taps(u_ref[...], uh_ref[...] * live, wu_ref, width)
        yg = yg + bg_ref[...]
        yu = yu + bu_ref[...]
        sg = _sigmoid(yg)
        dfv = df_ref[...].astype(F32)
        dcs = (dfv * yu * (sg * (1.0 + yg * (1.0 - sg))), dfv * (yg * sg))

        @pl.when(i == 0)
        def _():
            dw_ref[...] = jnp.zeros_like(dw_ref)
            db_ref[...] = jnp.zeros_like(db_ref)

        for t, (dc, taps) in enumerate(zip(dcs, (gt, ut))):
            dc_ref[t] = dc
            db_ref[t] += jnp.sum(dc, axis=0, keepdims=True)
            for j in range(width):
                dw_ref[t, j:j + 1, :] += jnp.sum(dc * taps[j], axis=0, keepdims=True)

    dc, dw, db = _pcall(
        body, name="ffn_conv_bwd", grid=(nc, S // tr),
        in_specs=[pl.BlockSpec((tr, tw), lambda c, i: (i, c)), _prev_halo_spec(tr, tw, lambda c: c),
                  pl.BlockSpec((tr, tw), lambda c, i: (i, c + nc)), _prev_halo_spec(tr, tw, lambda c: c + nc),
                  pl.BlockSpec((width, tw), lambda c, i: (0, c)), pl.BlockSpec((width, tw), lambda c, i: (0, c + nc)),
                  pl.BlockSpec((1, tw), lambda c, i: (0, c)), pl.BlockSpec((1, tw), lambda c, i: (0, c + nc)),
                  pl.BlockSpec((tr, tw), lambda c, i: (i, c))],
        out_specs=[pl.BlockSpec((2, tr, tw), lambda c, i: (0, i, c)), pl.BlockSpec((2, width, tw), lambda c, i: (0, 0, c)),
                   pl.BlockSpec((2, 1, tw), lambda c, i: (0, 0, c))],
        out_shape=[jax.ShapeDtypeStruct((2, S, half), F32), jax.ShapeDtypeStruct((2, width, half), F32),
                   jax.ShapeDtypeStruct((2, 1, half), F32)],
        dims=("parallel", "arbitrary"))(u, u, u, u, w, w, b, b, df)
    return dc, jnp.concatenate([dw[0], dw[1]], axis=1), jnp.concatenate([db[0], db[1]], axis=1)


_MODE_OF = {v: k for k, v in _DN.items()}


def _bf16_dot(a, b, mode):
    return lax.dot_general(a.astype(BF16), b.astype(BF16), _DN[mode], preferred_element_type=F32)


@functools.partial(jax.custom_vjp, nondiff_argnums=(2,))
def _bdot_mode(a, b, mode):
    return _bf16_dot(a, b, mode)


def _bdot_fwd(a, b, mode):
    return _bf16_dot(a, b, mode), (a, b)


def _bdot_bwd(mode, res, ct):
    a, b = res
    if mode == "nn":
        da, db = _bf16_dot(ct, b, "nt"), _bf16_dot(a, ct, "tn")
    elif mode == "nt":
        da, db = _bf16_dot(ct, b, "nn"), _bf16_dot(ct, a, "tn")
    else:
        da, db = _bf16_dot(b, ct, "nt"), _bf16_dot(a, ct, "nn")
    return da.astype(a.dtype), db.astype(b.dtype)


_bdot_mode.defvjp(_bdot_fwd, _bdot_bwd)


def _bdot(a, b, dn=_DN["nn"]):
    return _bdot_mode(a, b, _MODE_OF[dn])


def _hi_lo(x):
    hi = x.astype(BF16)
    return hi, (x - hi.astype(F32)).astype(BF16)


def _dot3_raw(a, b, mode):
    a1, a2 = _hi_lo(a)
    b1, b2 = _hi_lo(b)
    dot = lambda p, q: lax.dot_general(p, q, _DN[mode], preferred_element_type=F32)
    return dot(a1, b1) + (dot(a1, b2) + dot(a2, b1))


@functools.partial(jax.custom_vjp, nondiff_argnums=(2,))
def _dot3(a, b, mode="nn"):
    return _dot3_raw(a, b, mode)


def _dot3_fwd(a, b, mode):
    return _dot3_raw(a, b, mode), (a, b)


def _dot3_bwd(mode, res, ct):
    a, b = res
    if mode == "nn":
        return _dot3_raw(ct, b, "nt"), _dot3_raw(a, ct, "tn")
    if mode == "nt":
        return _dot3_raw(ct, b, "nn"), _dot3_raw(ct, a, "tn")
    return _dot3_raw(b, ct, "nt"), _dot3_raw(a, ct, "nn")


_dot3.defvjp(_dot3_fwd, _dot3_bwd)


@functools.partial(jax.custom_vjp, nondiff_argnums=(2,))
def _gdot(a, b, mode="nn"):
    return _bf16_dot(a, b, mode)


def _gdot_fwd(a, b, mode):
    return _bf16_dot(a, b, mode), (a.astype(BF16), b.astype(BF16))


def _ct_dot(p, q, mode, ct_first):
    ct, r = (p, q) if ct_first else (q, p)
    c1, c2 = _hi_lo(ct)
    dot = lambda c: lax.dot_general(*((c, r) if ct_first else (r, c)), _DN[mode], preferred_element_type=F32)
    return dot(c1) + dot(c2)


def _gdot_bwd(mode, res, ct):
    a, b = res
    if mode == "nn":
        return _ct_dot(ct, b, "nt", True), _ct_dot(a, ct, "tn", False)
    if mode == "nt":
        return _ct_dot(ct, b, "nn", True), _ct_dot(ct, a, "tn", True)
    return _ct_dot(b, ct, "nt", False), _ct_dot(a, ct, "nn", False)


_gdot.defvjp(_gdot_fwd, _gdot_bwd)


def _split_dot(ones, x):
    x1 = x.astype(BF16)
    r1 = x - x1.astype(F32)
    x2 = r1.astype(BF16)
    x3 = (r1 - x2.astype(F32)).astype(BF16)
    m = ones.astype(BF16)
    dot = lambda p: lax.dot_general(m, p, _DN["nn"], preferred_element_type=F32)
    return dot(x1) + dot(x2) + dot(x3)


@jax.custom_vjp
def _tri_cumsum(x, lower, upper):
    return _split_dot(lower, x)


def _tri_cumsum_fwd(x, lower, upper):
    return _split_dot(lower, x), (lower, upper)


def _tri_cumsum_bwd(res, ct):
    lower, upper = res
    return _split_dot(upper, ct), jnp.zeros_like(lower), jnp.zeros_like(upper)


_tri_cumsum.defvjp(_tri_cumsum_fwd, _tri_cumsum_bwd)


def _tri_masks(n):
    r = lax.broadcasted_iota(jnp.int32, (n, n), 0)
    c = lax.broadcasted_iota(jnp.int32, (n, n), 1)
    return r >= c, r > c


def _gdn_chunk(q, k, v, z, ab, a_row, dt_row, norm_w, state, sel_a, sel_b):
    C = q.shape[0]
    incl, strict = _tri_masks(C)
    lower = jnp.where(incl, 1.0, 0.0)
    qn = q * lax.rsqrt(jnp.sum(q * q, axis=-1, keepdims=True) + NORM_EPS) * (GDN_DK ** -0.5)
    kn = k * lax.rsqrt(jnp.sum(k * k, axis=-1, keepdims=True) + NORM_EPS)
    g = jnp.sum(-jnp.exp(a_row) * _softplus(ab + dt_row) * sel_a, axis=-1, keepdims=True)
    beta = jnp.sum(_sigmoid(ab) * sel_b, axis=-1, keepdims=True)
    gb = jnp.broadcast_to(g, (C, C))
    g_col = _tri_cumsum(gb, lower, jnp.where(strict, 0.0, 1.0))
    g_row = g_col.T
    g_last = jnp.sum(gb, axis=0, keepdims=True)
    gamma = jnp.where(incl, jnp.exp(jnp.where(incl, g_col - g_row, 0.0)), 0.0)
    e_col = jnp.exp(g_col)
    kb = kn * beta
    a_mat = jnp.where(strict, _gdot(kb, kn, "nt") * gamma, 0.0)
    x = jnp.concatenate([v * beta, kb * e_col], axis=1)
    pw = -a_mat
    steps = int(math.log2(C))
    for it in range(steps):
        x = x + _dot3(pw, x, "nn")
        if it < steps - 1:
            pw = _dot3(pw, pw, "nn")
    u, w = x[:, :GDN_DV], x[:, GDN_DV:]
    attn = _gdot(qn, kn, "nt") * gamma
    q_dec = qn * e_col
    k_dec = kn * jnp.exp(g_last - g_col)
    v_new = u - _gdot(w, state, "nn")
    o = _gdot(q_dec, state, "nn") + _gdot(attn, v_new, "nn")
    state_new = state * jnp.exp(jnp.broadcast_to(g_last, state.shape)) + _gdot(k_dec, v_new, "tn")
    y = _rms(o, norm_w) * _silu(z)
    return y, state_new


def _head_selectors(h):
    lane = lax.broadcasted_iota(jnp.int32, (1, LANES), 1)
    return jnp.where(lane == h, 1.0, 0.0), jnp.where(lane == h + GDN_HEADS, 1.0, 0.0)


GDN_HPS = 4
GDN_W = GDN_HPS * LANES


def _gdn_in_specs(rev, nc):
    def n_(n):
        return nc - 1 - n if rev else n
    G = GDN_HEADS // GDN_HPS
    blk = lambda off: pl.BlockSpec((CHUNK, GDN_W), lambda n, h: (n_(n), off + h))
    row = pl.BlockSpec((1, LANES), lambda n, h: (0, 0))
    return n_, [blk(0), blk(G), blk(2 * G), blk(ZIN_Z // GDN_W),
                pl.BlockSpec((CHUNK, LANES), lambda n, h: (n_(n), ZIN_AB // LANES)), row, row, row]


def _lanes(ref, j):
    return ref[:, j * LANES:(j + 1) * LANES]


def _hosting(call, hosted, *args):
    res = call(*args)
    return res if hosted is not None else (res, [])


def gdn_fwd(qkv, zin, a_row, dt_row, norm_w, hosted=None):
    S = qkv.shape[0]
    nc = S // CHUNK
    H = GDN_HEADS
    _, in_specs = _gdn_in_specs(False, nc)

    def body(q_ref, k_ref, v_ref, z_ref, ab_ref, a_ref, dt_ref, nw_ref, y_ref, st_ref, state):
        n, g = pl.program_id(0), pl.program_id(1)
        @pl.when((n == 0) & (g == 0))
        def _():
            state[...] = jnp.zeros_like(state)

        res = []
        for j in range(GDN_HPS):
            h = g * GDN_HPS + j
            st = state[h]
            sel_a, sel_b = _head_selectors(h)
            res.append((st,) + _gdn_chunk(_lanes(q_ref, j), _lanes(k_ref, j), _lanes(v_ref, j), _lanes(z_ref, j),
                                          ab_ref[...], a_ref[...], dt_ref[...], nw_ref[...], st, sel_a, sel_b))
        for j, (st, y, st_new) in enumerate(res):
            st_ref[j] = st
            y_ref[:, j * LANES:(j + 1) * LANES] = y.astype(y_ref.dtype)
            state[g * GDN_HPS + j] = st_new

    call = _pcall(body, name="gdn_fwd", grid=(nc, H // GDN_HPS), in_specs=in_specs,
                  out_specs=[pl.BlockSpec((CHUNK, GDN_W), lambda n, h: (n, h)),
                             pl.BlockSpec((GDN_HPS, None, GDN_DK, GDN_DV), lambda n, h: (h, n, 0, 0))],
                  out_shape=[jax.ShapeDtypeStruct((S, H * GDN_DV), BF16),
                             jax.ShapeDtypeStruct((H, nc, GDN_DK, GDN_DV), F32)],
                  scratch_shapes=[pltpu.VMEM((H, GDN_DK, GDN_DV), F32)],
                  dims=("arbitrary", "arbitrary"), hosted=hosted)
    return _hosting(call, hosted, qkv, qkv, qkv, zin, zin, a_row, dt_row, norm_w)


def gdn_bwd(qkv, zin, a_row, dt_row, norm_w, states, dy, dy_col0, hosted=None):
    S = qkv.shape[0]
    nc = S // CHUNK
    H = GDN_HEADS
    n_, in_specs = _gdn_in_specs(True, nc)
    assert dy_col0 % GDN_HPS == 0
    in_specs = in_specs + [pl.BlockSpec((GDN_HPS, None, GDN_DK, GDN_DV), lambda n, h: (h, n_(n), 0, 0)),
                           pl.BlockSpec((CHUNK, GDN_W), lambda n, h: (n_(n), dy_col0 // GDN_HPS + h))]

    def body(q_ref, k_ref, v_ref, z_ref, ab_ref, a_ref, dt_ref, nw_ref, st_ref, dy_ref,
             dq_ref, dk_ref, dv_ref, dz_ref, dab_ref, da_ref, ddt_ref, dnw_ref, dstate):
        n, g = pl.program_id(0), pl.program_id(1)

        @pl.when((n == 0) & (g == 0))
        def _():
            da_ref[...] = jnp.zeros_like(da_ref)
            ddt_ref[...] = jnp.zeros_like(ddt_ref)
            dnw_ref[...] = jnp.zeros_like(dnw_ref)
            dstate[...] = jnp.zeros_like(dstate)

        @pl.when(g == 0)
        def _():
            dab_ref[...] = jnp.zeros_like(dab_ref)

        res = []
        for j in range(GDN_HPS):
            h = g * GDN_HPS + j
            sel_a, sel_b = _head_selectors(h)
            _, vjp = jax.vjp(lambda *a, sa=sel_a, sb=sel_b: _gdn_chunk(*a, sa, sb), _lanes(q_ref, j), _lanes(k_ref, j),
                             _lanes(v_ref, j), _lanes(z_ref, j), ab_ref[...], a_ref[...], dt_ref[...], nw_ref[...],
                             st_ref[j])
            res.append(vjp((_lanes(dy_ref, j).astype(F32), dstate[h])))
        for j, (dq, dk, dv, dz, dab, da, ddt, dnw, dst) in enumerate(res):
            cols = slice(j * LANES, (j + 1) * LANES)
            dq_ref[:, cols] = dq
            dk_ref[:, cols] = dk
            dv_ref[:, cols] = dv
            dz_ref[:, cols] = dz.astype(dz_ref.dtype)
            dstate[g * GDN_HPS + j] = dst
        dab_ref[...] += sum(r[4] for r in res)
        da_ref[...] += sum(r[5] for r in res)
        ddt_ref[...] += sum(r[6] for r in res)
        dnw_ref[...] += sum(r[7] for r in res)

    blk = pl.BlockSpec((CHUNK, GDN_W), lambda n, h: (n_(n), h))
    row = pl.BlockSpec((1, LANES), lambda n, h: (0, 0))
    wide = jax.ShapeDtypeStruct((S, H * LANES), F32)
    call = _pcall(body, name="gdn_bwd", grid=(nc, H // GDN_HPS), in_specs=in_specs,
                  out_specs=[blk, blk, blk, blk, pl.BlockSpec((CHUNK, LANES), lambda n, h: (n_(n), 0)), row, row, row],
                  out_shape=[wide, wide, wide, jax.ShapeDtypeStruct((S, H * LANES), BF16),
                             jax.ShapeDtypeStruct((S, LANES), F32)] + [jax.ShapeDtypeStruct((1, LANES), F32)] * 3,
                  scratch_shapes=[pltpu.VMEM((H, GDN_DK, GDN_DV), F32)],
                  dims=("arbitrary", "arbitrary"), hosted=hosted)
    return _hosting(call, hosted, qkv, qkv, qkv, zin, zin, a_row, dt_row, norm_w, states, dy)


def _rope_full(x, cos, sin):
    x1, x2 = x[:, :RET_DK // 2], x[:, RET_DK // 2:]
    return jnp.concatenate([x1 * cos - x2 * sin, x2 * cos + x1 * sin], axis=1)


RET_CHUNK = 512


def _ret_chunk(q, k, v, cos, sin, lg, state):
    C = q.shape[0]
    incl, _ = _tri_masks(C)
    qr = _rope_full(q, cos, sin)
    kr = _rope_full(k, cos, sin) * (RET_DK ** -0.5)
    r = lax.broadcasted_iota(jnp.int32, (C, C), 0)
    c = lax.broadcasted_iota(jnp.int32, (C, C), 1)
    dist = jnp.where(incl, (r - c).astype(F32), 0.0)
    lg1 = lg[:, :1]
    decay = jnp.where(incl, jnp.exp(dist * lg1), 0.0)
    pos = lax.broadcasted_iota(jnp.int32, (C, 1), 0).astype(F32)
    xi = jnp.exp((pos + 1.0) * lg1)
    zeta = jnp.exp((C - 1.0 - pos) * lg1)
    inner = _bdot(_bdot(qr, kr, _DN["nt"]) * decay, v)
    cross = _bdot(qr * xi, state)
    state_new = state * jnp.exp(C * lg1) + _bdot(kr * zeta, v, _DN["tn"])
    o = inner + cross
    mu = jnp.mean(o, axis=-1, keepdims=True)
    var = jnp.mean(jnp.square(o - mu), axis=-1, keepdims=True)
    return (o - mu) * lax.rsqrt(var + NORM_EPS), state_new


def _ret_log_gamma():
    lg = np.log1p(-np.power(2.0, -5.0 - np.arange(RET_HEADS, dtype=np.float64))).astype(np.float32)
    return jnp.asarray(np.broadcast_to(lg[:, None, None], (RET_HEADS, 1, LANES)).copy())


def _ret_in_specs(rev, nc):
    def n_(n):
        return nc - 1 - n if rev else n
    H = RET_HEADS
    return n_, [pl.BlockSpec((RET_CHUNK, RET_DK),lambda n, h: (n_(n), h)),
                pl.BlockSpec((RET_CHUNK, RET_DK),lambda n, h: (n_(n), H + h)),
                pl.BlockSpec((RET_CHUNK, RET_DV),lambda n, h: (n_(n), 2 * H * RET_DK // RET_DV + h)),
                pl.BlockSpec((RET_CHUNK, LANES), lambda n, h: (n_(n), 0)),
                pl.BlockSpec((RET_CHUNK, LANES), lambda n, h: (n_(n), 0)),
                pl.BlockSpec((None, 1, LANES), lambda n, h: (h, 0, 0))]


def ret_fwd(zz, cos, sin, hosted=None):
    S = zz.shape[0]
    nc = S // RET_CHUNK
    H = RET_HEADS
    _, in_specs = _ret_in_specs(False, nc)

    def body(q_ref, k_ref, v_ref, cos_ref, sin_ref, lg_ref, o_ref, st_ref, state):
        n, h = pl.program_id(0), pl.program_id(1)

        @pl.when(n == 0)
        def _():
            state[h] = jnp.zeros((RET_DK, RET_DV), F32)

        st = state[h]
        st_ref[...] = st
        o, st_new = _ret_chunk(q_ref[...], k_ref[...], v_ref[...], cos_ref[...], sin_ref[...], lg_ref[...], st)
        o_ref[...] = o
        state[h] = st_new

    call = _pcall(body, name="ret_fwd", grid=(nc, H), in_specs=in_specs,
                  out_specs=[pl.BlockSpec((RET_CHUNK, RET_DV),lambda n, h: (n, h)),
                             pl.BlockSpec((None, None, RET_DK, RET_DV), lambda n, h: (h, n, 0, 0))],
                  out_shape=[jax.ShapeDtypeStruct((S, H * RET_DV), F32),
                             jax.ShapeDtypeStruct((H, nc, RET_DK, RET_DV), F32)],
                  scratch_shapes=[pltpu.VMEM((H, RET_DK, RET_DV), F32)],
                  dims=("arbitrary", "arbitrary"), hosted=hosted)
    return _hosting(call, hosted, zz, zz, zz, cos, sin, _ret_log_gamma())


def ret_bwd(zz, cos, sin, states, do):
    S = zz.shape[0]
    nc = S // RET_CHUNK
    H = RET_HEADS
    n_, in_specs = _ret_in_specs(True, nc)
    in_specs = in_specs + [pl.BlockSpec((None, None, RET_DK, RET_DV), lambda n, h: (h, n_(n), 0, 0)),
                           pl.BlockSpec((RET_CHUNK, RET_DV),lambda n, h: (n_(n), h))]

    def body(q_ref, k_ref, v_ref, cos_ref, sin_ref, lg_ref, st_ref, do_ref, dq_ref, dk_ref, dv_ref, dstate):
        n, h = pl.program_id(0), pl.program_id(1)

        @pl.when(n == 0)
        def _():
            dstate[h] = jnp.zeros((RET_DK, RET_DV), F32)

        cos, sin, lg = cos_ref[...], sin_ref[...], lg_ref[...]
        _, vjp = jax.vjp(lambda q, k, v, st: _ret_chunk(q, k, v, cos, sin, lg, st),
                         q_ref[...], k_ref[...], v_ref[...], st_ref[...])
        dq, dk, dv, dst = vjp((do_ref[...], dstate[h]))
        dq_ref[...] = dq.astype(dq_ref.dtype)
        dk_ref[...] = dk.astype(dk_ref.dtype)
        dv_ref[...] = dv.astype(dv_ref.dtype)
        dstate[h] = dst

    return _pcall(body, name="ret_bwd", grid=(nc, H), in_specs=in_specs,
                  out_specs=[pl.BlockSpec((RET_CHUNK, RET_DK),lambda n, h: (n_(n), h)),
                             pl.BlockSpec((RET_CHUNK, RET_DK),lambda n, h: (n_(n), h)),
                             pl.BlockSpec((RET_CHUNK, RET_DV),lambda n, h: (n_(n), h))],
                  out_shape=[jax.ShapeDtypeStruct((S, H * RET_DK), BF16), jax.ShapeDtypeStruct((S, H * RET_DK), BF16),
                             jax.ShapeDtypeStruct((S, H * RET_DV), BF16)],
                  scratch_shapes=[pltpu.VMEM((H, RET_DK, RET_DV), F32)],
                  dims=("arbitrary", "arbitrary"))(zz, zz, zz, cos, sin, _ret_log_gamma(), states, do)


MLA_SCALE = (MLA_NOPE + MLA_ROPE) ** -0.5
NEG = -1e30


def _mla_scores(q, kn, kpe, diagonal):
    s = (lax.dot_general(q[:, :LANES], kn, _DN["nt"], preferred_element_type=F32)
         + lax.dot_general(q[:, LANES:], kpe, _DN["nt"], preferred_element_type=F32)) * MLA_SCALE
    if diagonal:
        row = lax.broadcasted_iota(jnp.int32, s.shape, 0)
        col = lax.broadcasted_iota(jnp.int32, s.shape, 1)
        s = jnp.where(col <= row, s, NEG)
    return s


def _on_and_below_diagonal(i, j, step):
    @pl.when(j < i)
    def _():
        step(False)

    @pl.when(j == i)
    def _():
        step(True)


FLASH_T = 1024


def flash_fwd(qr, kv, kpe, *, t=FLASH_T, hosted=None):
    S = qr.shape[0]
    t = min(t, S)
    nb = S // t
    H = MLA_HEADS

    def body(q_ref, kn_ref, v_ref, kpe_ref, o_ref, lse_ref, m_s, l_s, acc):
        i, j = pl.program_id(1), pl.program_id(2)

        @pl.when(j == 0)
        def _():
            m_s[...] = jnp.full_like(m_s, NEG)
            l_s[...] = jnp.zeros_like(l_s)
            acc[...] = jnp.zeros_like(acc)

        def step(diagonal):
            s = _mla_scores(q_ref[...], kn_ref[...], kpe_ref[...], diagonal)
            m_new = jnp.maximum(m_s[...], jnp.max(s, axis=-1, keepdims=True))
            p = jnp.exp(s - m_new)
            alpha = jnp.exp(m_s[...] - m_new)
            l_s[...] = alpha * l_s[...] + jnp.sum(p, axis=-1, keepdims=True)
            acc[...] = alpha * acc[...] + _bdot(p, v_ref[...])
            m_s[...] = m_new

        _on_and_below_diagonal(i, j, step)

        @pl.when(j == nb - 1)
        def _():
            o_ref[...] = (acc[...] / l_s[...]).astype(o_ref.dtype)
            lse_ref[...] = m_s[...] + jnp.log(l_s[...])

    kmap = lambda off: (lambda h, i, j: (jnp.minimum(j, i), off + h))
    call = _pcall(body, name="mla_flash_fwd", grid=(H, nb, nb),
                  in_specs=[pl.BlockSpec((t, 2 * LANES), lambda h, i, j: (i, h)),
                            pl.BlockSpec((t, LANES), kmap(0)), pl.BlockSpec((t, LANES), kmap(H)),
                            pl.BlockSpec((t, LANES), lambda h, i, j: (jnp.minimum(j, i), 0))],
                  out_specs=[pl.BlockSpec((t, LANES), lambda h, i, j: (i, h)),
                             pl.BlockSpec((None, t, 1), lambda h, i, j: (h, i, 0))],
                  out_shape=[jax.ShapeDtypeStruct((S, H * MLA_V), BF16), jax.ShapeDtypeStruct((H, S, 1), F32)],
                  scratch_shapes=[pltpu.VMEM((t, 1), F32), pltpu.VMEM((t, 1), F32), pltpu.VMEM((t, MLA_V), F32)],
                  dims=("parallel", "parallel", "arbitrary"), hosted=hosted)
    return _hosting(call, hosted, qr, kv, kv, kpe)


def _mla_p_ds(q, kn, v, kpe, do, o, lse, diagonal):
    p = jnp.exp(_mla_scores(q, kn, kpe, diagonal) - lse)
    dof = do.astype(F32)
    delta = jnp.sum(dof * o.astype(F32), axis=-1, keepdims=True)
    dp = lax.dot_general(do.astype(BF16), v, _DN["nt"], preferred_element_type=F32)
    ds = p * (dp - delta) * MLA_SCALE
    return p, ds


def flash_bwd_dq(qr, kv, kpe, o, lse, dy, dy_col0, *, t=FLASH_T, hosted=None):
    S = qr.shape[0]
    t = min(t, S)
    nb = S // t
    H = MLA_HEADS

    def body(q_ref, kn_ref, v_ref, kpe_ref, o_ref, lse_ref, do_ref, dq_ref, acc):
        i, j = pl.program_id(1), pl.program_id(2)

        @pl.when(j == 0)
        def _():
            acc[...] = jnp.zeros_like(acc)

        def step(diagonal):
            _, ds = _mla_p_ds(q_ref[...], kn_ref[...], v_ref[...], kpe_ref[...], do_ref[...], o_ref[...],
                              lse_ref[...], diagonal)
            acc[...] += jnp.concatenate([_bdot(ds, kn_ref[...]), _bdot(ds, kpe_ref[...])], axis=1)

        _on_and_below_diagonal(i, j, step)

        @pl.when(j == nb - 1)
        def _():
            dq_ref[...] = acc[...]

    kmap = lambda off: (lambda h, i, j: (jnp.minimum(j, i), off + h))
    call = _pcall(body, name="mla_flash_dq", grid=(H, nb, nb),
                  in_specs=[pl.BlockSpec((t, 2 * LANES), lambda h, i, j: (i, h)),
                            pl.BlockSpec((t, LANES), kmap(0)), pl.BlockSpec((t, LANES), kmap(H)),
                            pl.BlockSpec((t, LANES), lambda h, i, j: (jnp.minimum(j, i), 0)),
                            pl.BlockSpec((t, LANES), lambda h, i, j: (i, h)),
                            pl.BlockSpec((None, t, 1), lambda h, i, j: (h, i, 0)),
                            pl.BlockSpec((t, LANES), lambda h, i, j: (i, dy_col0 + h))],
                  out_specs=pl.BlockSpec((t, 2 * LANES), lambda h, i, j: (i, h)),
                  out_shape=jax.ShapeDtypeStruct((S, H * 2 * LANES), F32),
                  scratch_shapes=[pltpu.VMEM((t, 2 * LANES), F32)],
                  dims=("parallel", "parallel", "arbitrary"), hosted=hosted)
    return _hosting(call, hosted, qr, kv, kv, kpe, o, lse, dy)


def flash_bwd_dkv(qr, kv, kpe, o, lse, dy, dy_col0, *, t=FLASH_T, hosted=None):
    S = qr.shape[0]
    t = min(t, S)
    nb = S // t
    H = MLA_HEADS

    def body(q_ref, kn_ref, v_ref, kpe_ref, o_ref, lse_ref, do_ref, dkn_ref, dv_ref, dkpe_ref, akn, av):
        j, h, i = pl.program_id(0), pl.program_id(1), pl.program_id(2)

        @pl.when(i == 0)
        def _():
            akn[...] = jnp.zeros_like(akn)
            av[...] = jnp.zeros_like(av)

        @pl.when((i == 0) & (h == 0))
        def _():
            dkpe_ref[...] = jnp.zeros_like(dkpe_ref)

        def step(diagonal):
            q = q_ref[...]
            p, ds = _mla_p_ds(q, kn_ref[...], v_ref[...], kpe_ref[...], do_ref[...], o_ref[...], lse_ref[...],
                              diagonal)
            av[...] += _bdot(p, do_ref[...], _DN["tn"])
            akn[...] += _bdot(ds, q[:, :LANES], _DN["tn"])
            dkpe_ref[...] += _bdot(ds, q[:, LANES:], _DN["tn"])

        _on_and_below_diagonal(i, j, step)

        @pl.when(i == nb - 1)
        def _():
            dkn_ref[...] = akn[...].astype(dkn_ref.dtype)
            dv_ref[...] = av[...].astype(dv_ref.dtype)

    qmap = lambda off: (lambda j, h, i: (jnp.maximum(i, j), off + h))
    call = _pcall(
        body, name="mla_flash_dkv", grid=(nb, H, nb),
        in_specs=[pl.BlockSpec((t, 2 * LANES), qmap(0)),
                  pl.BlockSpec((t, LANES), lambda j, h, i: (j, h)), pl.BlockSpec((t, LANES), lambda j, h, i: (j, H + h)),
                  pl.BlockSpec((t, LANES), lambda j, h, i: (j, 0)),
                  pl.BlockSpec((t, LANES), qmap(0)),
                  pl.BlockSpec((None, t, 1), lambda j, h, i: (h, jnp.maximum(i, j), 0)),
                  pl.BlockSpec((t, LANES), qmap(dy_col0))],
        out_specs=[pl.BlockSpec((t, LANES), lambda j, h, i: (j, h)), pl.BlockSpec((t, LANES), lambda j, h, i: (j, h)),
                   pl.BlockSpec((t, LANES), lambda j, h, i: (j, 0))],
        out_shape=[jax.ShapeDtypeStruct((S, H * LANES), BF16), jax.ShapeDtypeStruct((S, H * LANES), BF16),
                   jax.ShapeDtypeStruct((S, LANES), F32)],
        scratch_shapes=[pltpu.VMEM((t, LANES), F32), pltpu.VMEM((t, LANES), F32)],
        dims=("arbitrary", "arbitrary", "arbitrary"), hosted=hosted)
    (dkn, dv, dkpe), extra = _hosting(call, hosted, qr, kv, kv, kpe, o, lse, dy)
    return (jnp.concatenate([dkn, dv], axis=1), dkpe), extra


def loss_head(h, target, g, *, tr=256):
    S, D = h.shape
    tr = min(tr, S)

    def body(h_ref, t_ref, g_ref, loss_ref, dh_ref, dg_ref):
        tgt = t_ref[...]

        def f(hh, gg):
            err = jnp.square(_rms(hh, gg) - tgt)
            per_row = jnp.sum(err, axis=-1, keepdims=True) * (0.5 / D)
            return jnp.sum(per_row, axis=0, keepdims=True)

        val, vjp = jax.vjp(f, h_ref[...], g_ref[...])
        dh, dg = vjp(jnp.ones((1, 1), F32))
        dh_ref[...] = dh

        @pl.when(pl.program_id(0) == 0)
        def _():
            loss_ref[...] = jnp.zeros_like(loss_ref)
            dg_ref[...] = jnp.zeros_like(dg_ref)

        loss_ref[...] += jnp.broadcast_to(val, loss_ref.shape)
        dg_ref[...] += dg

    return _pcall(body, name="loss_head", grid=(S // tr,),
                  in_specs=[_row_spec(tr, D, 0), _row_spec(tr, D, 0), _full_spec(g)],
                  out_specs=[pl.BlockSpec((1, LANES), lambda i: (0, 0)), _row_spec(tr, D, 0), _full_spec(g)],
                  out_shape=[jax.ShapeDtypeStruct((1, LANES), F32), jax.ShapeDtypeStruct((S, D), F32),
                             jax.ShapeDtypeStruct(g.shape, F32)],
                  dims=("arbitrary",))(h, target, g)


def _rope_tables(positions, dim):
    inv_freq = ROPE_THETA ** (-jnp.arange(0, dim, 2, dtype=F32) / dim)
    ang = positions.astype(F32)[:, None] * inv_freq
    return jnp.cos(ang), jnp.sin(ang)


def _pad_cols(w, n):
    return jnp.pad(w, ((0, 0), (0, n - w.shape[1])))


def _prep_w_in0(w):
    return jnp.concatenate([w[:, :4096], w[:, 4112:5136], _pad_cols(w[:, 5136:5200], LANES),
                            _pad_cols(w[:, 4096:4112], LANES)], axis=1)


def _unprep_w_in0(g):
    return jnp.concatenate([g[:, :4096], g[:, ZIN_AB:ZIN_AB + 16], g[:, ZIN_CQ:ZIN_KR], g[:, ZIN_KR:ZIN_KR + MLA_ROPE]],
                           axis=1)


def _prep_w_uq(w):
    w = w.reshape(MLA_Q_RANK, MLA_HEADS, MLA_NOPE + MLA_ROPE)
    w = jnp.pad(w, ((0, 0), (0, 0), (0, 2 * LANES - MLA_NOPE - MLA_ROPE)))
    return w.reshape(MLA_Q_RANK, MLA_HEADS * 2 * LANES)


def _unprep_w_uq(g):
    g = g.reshape(MLA_Q_RANK, MLA_HEADS, 2 * LANES)[:, :, :MLA_NOPE + MLA_ROPE]
    return g.reshape(MLA_Q_RANK, MLA_HEADS * (MLA_NOPE + MLA_ROPE))


def _prep_w_ukv(w):
    w = w.reshape(MLA_KV_RANK, MLA_HEADS, 2, LANES)
    return jnp.transpose(w, (0, 2, 1, 3)).reshape(MLA_KV_RANK, 2 * MLA_HEADS * LANES)


def _unprep_w_ukv(g):
    g = g.reshape(MLA_KV_RANK, 2, MLA_HEADS, LANES)
    return jnp.transpose(g, (0, 2, 1, 3)).reshape(MLA_KV_RANK, 2 * MLA_HEADS * LANES)


def _row(v, n=None):
    v = v.reshape(1, -1).astype(F32)
    return v if n is None else _pad_cols(v, n)


def _ffn_fwd(h, norm_g, w_up, conv_w, conv_b, w_down, tag, hosted=None):
    (hn,) = rowwise(_fn_rms, [(h, D_MODEL, 0)], [norm_g], [], [(D_MODEL, BF16)], name=f"{tag}_ffn_norm")
    u = matmul(hn, w_up, "nn", F32, b_shards=N_CHIPS, **TILES["wide_nn"], name=f"{tag}_ffn_up", hosted=hosted)
    u, got = u if hosted is not None else (u, [])
    f = ffn_conv_fwd(u, conv_w, conv_b)
    h_out = matmul(f, w_down, "nn", F32, add=h, tm=512, tn=1024, tk=8192, name=f"{tag}_ffn_down")
    return h_out, (hn, u, f), got


def _ffn_bwd(dh, dh16, h, norm_g, w_up, conv_w, conv_b, w_down, saved, tag, make_hosted=None):
    hn, u, f = saved
    df = matmul(dh16, w_down, "nt", BF16, tm=512, tn=2816, name=f"{tag}_ffn_down_dx")
    g_down = matmul(f, dh16, "tn", BF16, **TILES["dw"], name=f"{tag}_ffn_down_dw")
    dc, g_conv_w, g_conv_b = ffn_conv_bwd(u, conv_w, conv_b, df)
    du = conv_transpose(dc, conv_w, BF16, name=f"{tag}_ffn_conv_dx")
    g_up = matmul(hn, du, "tn", BF16, out_shards=N_CHIPS, tm=1024, tn=1408, tk=4096, name=f"{tag}_ffn_up_dw")
    hosted = make_hosted(dict(ffn_w_up=g_up, ffn_w_down=g_down)) if make_hosted else None
    dhn = matmul(du, w_up, "nt", F32, b_shards=N_CHIPS, tm=512, tn=2048, tk=2816, name=f"{tag}_ffn_up_dx",
                 hosted=hosted)
    dhn, got = dhn if hosted is not None else (dhn, [])
    ((dh_in, dh_in16),), (g_norm,) = rowwise_bwd(_fn_rms, [(h, D_MODEL, 0)], [norm_g], [], [(dhn, D_MODEL, 0)],
                                                 [(F32, BF16)], adds=[(dh, D_MODEL, 0)], name=f"{tag}_ffn_norm_bwd")
    return dh_in, dh_in16, dict(ffn_norm=g_norm, ffn_w_up=g_up, ffn_conv_w=g_conv_w, ffn_conv_b=g_conv_b,
                                ffn_w_down=g_down), got


TILES = {"wide_nn": dict(tm=512, tn=3072, tk=2048),
         "square": dict(tm=512, tn=2048, tk=2048),
         "dw": dict(tm=512, tn=2048, tk=4096)}


def _ple_fwd(h, p_i, w_proj, gate_g, w_gate, tag):
    (hg,) = rowwise(_fn_rms, [(h, D_MODEL, 0)], [gate_g], [], [(D_MODEL, BF16)], name=f"{tag}_ple_norm")
    gl = matmul(hg, w_gate, "nn", F32, **TILES["square"], name=f"{tag}_ple_gate")
    pp = matmul(p_i, w_proj, "nn", F32, b_shards=N_CHIPS, name=f"{tag}_ple_proj")
    (h_out,) = rowwise(_fn_ple, [(h, D_MODEL, 0), (pp, D_MODEL, 0), (gl, D_MODEL, 0)], [], [], [(D_MODEL, F32)],
                       name=f"{tag}_ple_add")
    return h_out, (hg, gl, pp)


def _ple_bwd(dh, h, p_i, w_proj, gate_g, w_gate, saved, tag):
    hg, gl, pp = saved
    (dpp, dgl), _ = rowwise_bwd(_fn_ple_terms, [(pp, D_MODEL, 0), (gl, D_MODEL, 0)], [], [], [(dh, D_MODEL, 0)],
                                [BF16, BF16], name=f"{tag}_ple_add_bwd")
    g_proj = matmul(p_i, dpp, "tn", BF16, out_shards=N_CHIPS, name=f"{tag}_ple_proj_dw")
    g_gate = matmul(hg, dgl, "tn", BF16, **TILES["dw"], name=f"{tag}_ple_gate_dw")
    dh_in, dh_in16, g_norm = matmul(dgl, w_gate, "nt", F32, tm=256, tn=2048, tk=2048, name=f"{tag}_ple_gate_dx",
                                    norm_bwd=(h, gate_g, dh))
    return dh_in, dh_in16, dict(ple_proj=g_proj, ple_gate_norm=g_norm, ple_gate=g_gate)


def local_step(x, p, positions, target, slots, W, ex=None):
    S = x.shape[0]
    G = {}
    p0, p1 = p[0].astype(BF16), p[1].astype(BF16)
    W = dict(W)

    def use(names, bufs):
        for k, b in zip(names, bufs):
            r, c = b.shape[1:]
            W[k] = b.reshape(N_CHIPS * r, c) if k in ROW_SHARDED else (b if k in KEPT_SHARDED else _cols_to_full(b))

    def by_chip(names):
        out = []
        for k in names:
            r, c = slots[k].shape[1:]
            out.append(G[k].reshape(N_CHIPS, r, c) if k in ROW_SHARDED
                       else (G[k] if k in KEPT_SHARDED else _full_to_cols(G[k])))
        return out

    first = [slots[k] for k in GATHER_FIRST]
    use(GATHER_FIRST, ex.gather(first) if ex else first)

    cm, sm = _rope_tables(positions, MLA_ROPE)
    zeros = jnp.zeros((S, LANES - MLA_ROPE), F32)
    cosp = jnp.concatenate([cm, cm, zeros], axis=1)
    sinp = jnp.concatenate([sm, sm, zeros], axis=1)
    cr, sr = _rope_tables(positions, RET_DK)

    w_in0 = _prep_w_in0(W["l0_w_in"])
    a_row = _row(W["l0_gdn_A_log"], LANES)
    dt_row = _row(W["l0_gdn_dt_bias"], LANES)
    gdn_nw = _row(W["l0_gdn_norm"])
    n = {k: _row(W[k]) for k in ("l0_attn_norm", "l0_mla_q_norm", "l0_mla_kv_norm", "l0_ffn_norm",
                                 "l0_ple_gate_norm", "l1_attn_norm", "l1_ret_norm", "l1_ffn_norm",
                                 "l1_ple_gate_norm", "final_norm", "l0_ffn_conv_b", "l1_ffn_conv_b")}

    (hn0,) = rowwise(_fn_rms, [(x, D_MODEL, 0)], [n["l0_attn_norm"]], [], [(D_MODEL, BF16)], name="l0_attn_norm")
    zin = matmul(hn0, w_in0, "nn", F32, tm=512, tn=1792, name="l0_w_in")
    qkv = gdn_conv_fwd(zin, W["l0_gdn_conv"])
    layer0 = [slots[k] for k in GATHER_L0]
    (y_a, gdn_states), got = gdn_fwd(qkv, zin, a_row, dt_row, gdn_nw, hosted=hosted_gather(layer0) if ex else None)
    use(GATHER_L0, got if ex else layer0)
    w_uq = _prep_w_uq(W["l0_mla_w_uq"])
    w_ukv = _prep_w_ukv(W["l0_mla_w_ukv"])
    mla_rows = [(zin, MLA_Q_RANK, ZIN_CQ // MLA_Q_RANK), (zin, MLA_KV_RANK, ZIN_CKV // MLA_KV_RANK),
                (zin, LANES, ZIN_KR // LANES)]
    mla_nd = [(cosp, LANES, 0), (sinp, LANES, 0)]
    cqn, ckvn, kpe = rowwise(_fn_mla_pre, mla_rows, [n["l0_mla_q_norm"], n["l0_mla_kv_norm"]], mla_nd,
                             [(MLA_Q_RANK, BF16), (MLA_KV_RANK, BF16), (LANES, BF16)], name="mla_pre")
    q_lin = matmul(cqn, w_uq, "nn", F32, name="mla_w_uq")
    kv = matmul(ckvn, w_ukv, "nn", BF16, name="mla_w_ukv")
    (qr,) = rowwise(_fn_rope_q, [(q_lin, 2048, 0)], [], mla_nd, [(2048, BF16)],
                    name="mla_rope_q")
    layer1 = [slots[k] for k in GATHER_L1]
    (y_b, lse), got = flash_fwd(qr, kv, kpe, hosted=hosted_gather(layer1) if ex else None)
    use(GATHER_L1, got if ex else layer1)
    y_ab = jnp.concatenate([y_a, y_b], axis=1)
    h1 = matmul(y_ab, W["l0_w_out"], "nn", F32, add=x, **TILES["square"], name="l0_w_out")
    ffn_late = [slots[k] for k in GATHER_FFN]
    h2, ffn0, got = _ffn_fwd(h1, n["l0_ffn_norm"], W["l0_ffn_w_up"], W["l0_ffn_conv_w"], n["l0_ffn_conv_b"],
                             W["l0_ffn_w_down"], "l0", hosted=hosted_gather(ffn_late) if ex else None)
    use(GATHER_FFN, got if ex else ffn_late)
    h3, ple0 = _ple_fwd(h2, p0, W["l0_ple_proj"], n["l0_ple_gate_norm"], W["l0_ple_gate"], "l0")

    (hn1,) = rowwise(_fn_rms, [(h3, D_MODEL, 0)], [n["l1_attn_norm"]], [], [(D_MODEL, BF16)], name="l1_attn_norm")
    late = [slots[k] for k in GATHER_L1_IN]
    zz = matmul(hn1, W["l1_w_in"], "nn", F32, b_shards=N_CHIPS, **TILES["wide_nn"], name="l1_w_in",
                hosted=hosted_gather(late) if ex else None)
    zz, got = zz if ex else (zz, late)
    use(GATHER_L1_IN, got)
    (o_ret, ret_states), _ = ret_fwd(zz, cr, sr)
    gate_rows = [(zz, 4096, 2), (o_ret, 4096, 0)]
    (yg,) = rowwise(_fn_ret_gate, gate_rows, [n["l1_ret_norm"]], [], [(4096, BF16)], name="ret_gate")
    h4 = matmul(yg, W["l1_w_out"], "nn", F32, add=h3, tm=512, tn=2048, tk=4096, name="l1_w_out")
    h5, ffn1, _ = _ffn_fwd(h4, n["l1_ffn_norm"], W["l1_ffn_w_up"], W["l1_ffn_conv_w"], n["l1_ffn_conv_b"],
                           W["l1_ffn_w_down"], "l1")
    h6, ple1 = _ple_fwd(h5, p1, W["l1_ple_proj"], n["l1_ple_gate_norm"], W["l1_ple_gate"], "l1")

    loss_vec, dh, G["final_norm"] = loss_head(h6, target, n["final_norm"])

    dh, dh16, g = _ple_bwd(dh, h5, p1, W["l1_ple_proj"], n["l1_ple_gate_norm"], W["l1_ple_gate"], ple1, "l1")
    G.update({"l1_" + k: v for k, v in g.items()})
    dh, dh16, g, _ = _ffn_bwd(dh, dh16, h4, n["l1_ffn_norm"], W["l1_ffn_w_up"], W["l1_ffn_conv_w"],
                              n["l1_ffn_conv_b"], W["l1_ffn_w_down"], ffn1, "l1")
    G.update({"l1_" + k: v for k, v in g.items()})

    dyg = matmul(dh16, W["l1_w_out"], "nt", F32, tm=512, tn=4096, name="l1_w_out_dx")
    G["l1_w_out"] = matmul(yg, dh16, "tn", BF16, **TILES["dw"], name="l1_w_out_dw")
    (dg, do_ret), (G["l1_ret_norm"],) = rowwise_bwd(_fn_ret_gate, gate_rows, [n["l1_ret_norm"]], [],
                                                   [(dyg, 4096, 0)], [BF16, F32], name="ret_gate_bwd")
    dq, dk, dv = ret_bwd(zz, cr, sr, ret_states, do_ret)
    dzz = jnp.concatenate([dq, dk, dv, dg], axis=1)
    G["l1_w_in"] = matmul(hn1, dzz, "tn", BF16, out_shards=N_CHIPS, tm=1024, tn=1536, tk=4096, name="l1_w_in_dw")
    sums, landed = {}, {}
    grads_l1 = by_chip(REDUCE_L1)
    dhn = matmul(dzz, W["l1_w_in"], "nt", F32, b_shards=N_CHIPS, tm=1024, tn=1024, tk=3072, name="l1_w_in_dx",
                 hosted=hosted_swap(grads_l1) if ex else None)
    if ex:
        dhn, swapped = dhn
        sums.update(zip(REDUCE_L1, ex.pair_sums(REDUCE_L1, grads_l1, swapped)))
    (dh,), (G["l1_attn_norm"],) = rowwise_bwd(_fn_rms, [(h3, D_MODEL, 0)], [n["l1_attn_norm"]], [],
                                             [(dhn, D_MODEL, 0)], [F32], adds=[(dh, D_MODEL, 0)],
                                             name="l1_attn_norm_bwd")

    dh, dh16, g = _ple_bwd(dh, h2, p0, W["l0_ple_proj"], n["l0_ple_gate_norm"], W["l0_ple_gate"], ple0, "l0")
    G.update({"l0_" + k: v for k, v in g.items()})
    mid = REDUCE_DQ + REDUCE_L0

    def swap_mid(g_ffn):
        G.update({"l0_" + k: v for k, v in g_ffn.items()})
        return hosted_swap(by_chip(mid))

    dh, dh16, g, swapped = _ffn_bwd(dh, dh16, h1, n["l0_ffn_norm"], W["l0_ffn_w_up"], W["l0_ffn_conv_w"],
                                    n["l0_ffn_conv_b"], W["l0_ffn_w_down"], ffn0, "l0",
                                    make_hosted=swap_mid if ex else None)
    G.update({"l0_" + k: v for k, v in g.items()})
    if ex:
        sums.update(zip(mid, ex.pair_sums(mid, by_chip(mid), swapped)))

    dy_ab = matmul(dh16, W["l0_w_out"], "nt", F32, **TILES["square"], name="l0_w_out_dx")
    G["l0_w_out"] = matmul(y_ab, dh16, "tn", BF16, **TILES["dw"], name="l0_w_out_dw")
    (dq, dk, dv, dz, dab, g_a, g_dt, G["l0_gdn_norm"]), got = gdn_bwd(
        qkv, zin, a_row, dt_row, gdn_nw, gdn_states, dy_ab, 0,
        hosted=hosted_scatter([sums[k] for k in REDUCE_L1]) if ex else None)
    landed.update(zip(REDUCE_L1, got))
    G["l0_gdn_A_log"], G["l0_gdn_dt_bias"] = g_a[:, :GDN_HEADS], g_dt[:, :GDN_HEADS]
    dpre, G["l0_gdn_conv"] = gdn_conv_bwd(zin, W["l0_gdn_conv"], jnp.concatenate([dq, dk, dv], axis=1))
    dqkv = conv_transpose(dpre, W["l0_gdn_conv"], BF16, name="gdn_conv_dx")
    dqr, got = flash_bwd_dq(qr, kv, kpe, y_b, lse, dy_ab, MLA_HEADS,
                            hosted=hosted_scatter([sums[k] for k in REDUCE_DQ]) if ex else None)
    landed.update(zip(REDUCE_DQ, got))
    (dkv, dkpe), got = flash_bwd_dkv(qr, kv, kpe, y_b, lse, dy_ab, MLA_HEADS,
                                     hosted=hosted_scatter([sums[k] for k in REDUCE_L0]) if ex else None)
    landed.update(zip(REDUCE_L0, got))
    (dq_lin,), _ = rowwise_bwd(_fn_rope_q, [(q_lin, 2048, 0)], [], mla_nd, [(dqr, 2048, 0)], [BF16],
                               name="mla_rope_q_bwd")
    G["l0_mla_w_uq"] = _unprep_w_uq(matmul(cqn, dq_lin, "tn", BF16, name="mla_w_uq_dw"))
    dcqn = matmul(dq_lin, w_uq, "nt", F32, name="mla_w_uq_dx")
    G["l0_mla_w_ukv"] = _unprep_w_ukv(matmul(ckvn, dkv, "tn", BF16, name="mla_w_ukv_dw"))
    dckvn = matmul(dkv, w_ukv, "nt", F32, name="mla_w_ukv_dx")
    (dcq, dckv, dkr), (G["l0_mla_q_norm"], G["l0_mla_kv_norm"]) = rowwise_bwd(
        _fn_mla_pre, mla_rows, [n["l0_mla_q_norm"], n["l0_mla_kv_norm"]], mla_nd,
        [(dcqn, MLA_Q_RANK, 0), (dckvn, MLA_KV_RANK, 0), (dkpe, LANES, 0)], [BF16, BF16, BF16], name="mla_pre_bwd")
    dzin = jnp.concatenate([dqkv, dz, dcq, dckv, dkr, dab.astype(BF16)], axis=1)
    G["l0_w_in"] = _unprep_w_in0(matmul(hn0, dzin, "tn", BF16, tm=512, tn=1792, tk=4096, name="l0_w_in_dw"))
    dhn = matmul(dzin, w_in0, "nt", F32, tm=512, tn=2048, tk=5376, name="l0_w_in_dx")
    (grad_x,), (G["l0_attn_norm"],) = rowwise_bwd(_fn_rms, [(x, D_MODEL, 0)], [n["l0_attn_norm"]], [],
                                                 [(dhn, D_MODEL, 0)], [F32], adds=[(dh, D_MODEL, 0)],
                                                 name="l0_attn_norm_bwd")
    small = {k: G[k] for k in SMALL}
    if not ex:
        return loss_vec[0, 0], grad_x, dict(zip(BIG, by_chip(BIG))), small
    sums.update(zip(REDUCE_LAST, ex.pair_sums(REDUCE_LAST, by_chip(REDUCE_LAST))))
    landed.update(zip(REDUCE_LAST, scatter_chips([sums[k] for k in REDUCE_LAST])))
    return loss_vec[0, 0], grad_x, ex.finish(sums, landed), small


HBM = pl.BlockSpec(memory_space=pltpu.HBM)
VMEM = pl.BlockSpec(memory_space=pltpu.VMEM)


def _place():
    return lax.axis_index("x"), lax.axis_index("y"), lax.axis_index("c")


def _other_chips(x, y):
    return [(1 - x, y), (x, 1 - y), (1 - x, 1 - y)]


def _comm_call(body, *, name, out_shape, in_specs, out_specs, scratch_shapes):
    return pl.pallas_call(body, name=name, out_shape=out_shape, in_specs=in_specs, out_specs=out_specs,
                          scratch_shapes=list(scratch_shapes),
                          compiler_params=pltpu.CompilerParams(vmem_limit_bytes=VMEM_LIMIT_MB << 20))


def _inplace_comm_call(body, bufs, *, name, n_sems):
    n = len(bufs)
    return pl.pallas_call(body, name=name, out_shape=[jax.ShapeDtypeStruct(b.shape, b.dtype) for b in bufs],
                          in_specs=[HBM] * n, out_specs=[HBM] * n, input_output_aliases={i: i for i in range(n)},
                          scratch_shapes=[pltpu.SemaphoreType.DMA((n_sems,)), pltpu.SemaphoreType.DMA((n_sems,))],
                          compiler_params=pltpu.CompilerParams(vmem_limit_bytes=VMEM_LIMIT_MB << 20))(*bufs)


def all_gather_chips(bufs):
    n_sems, start, finish = _gather_phase(len(bufs))
    n = len(bufs)

    def body(*refs):
        outs, send_sems, recv_sems = refs[n:2 * n], refs[2 * n], refs[2 * n + 1]
        start(None, outs, send_sems, recv_sems)
        finish(None, outs, send_sems, recv_sems)

    return _inplace_comm_call(body, bufs, name="all_gather_chips", n_sems=n_sems)


def _gather_phase(n):
    def plan(outs, send_sems, recv_sems):
        x, y, c = _place()

        def copy(w, k, chip, hc, to):
            half = outs[w].shape[1] // 2
            rows = outs[w].at[2 * chip[0] + chip[1], pl.ds(hc * half, half), :]
            return pltpu.make_async_remote_copy(src_ref=rows, dst_ref=rows, send_sem=send_sems.at[6 * w + k],
                                                recv_sem=recv_sems.at[6 * w + k], device_id=to, device_id_type=MESH)

        first = [[copy(w, k, (x, y), c, (*chip, c)) for k, chip in enumerate(_other_chips(x, y))] for w in range(n)]
        passed = [[copy(w, 3 + k, chip, c, (x, y, 1 - c)) for k, chip in enumerate(_other_chips(x, y))]
                  for w in range(n)]
        return copy, first, passed, (x, y, c)

    def start(_, outs, send_sems, recv_sems):
        _, first, _, _ = plan(outs, send_sems, recv_sems)
        for w in range(n):
            for cp in first[w]:
                cp.start()

    def finish(_, outs, send_sems, recv_sems):
        copy, first, passed, (x, y, c) = plan(outs, send_sems, recv_sems)
        chips = _other_chips(x, y)
        for w in range(n):
            for k, chip in enumerate(chips):
                copy(w, k, chip, c, (x, y, c)).wait_recv()
                passed[w][k].start()
        for w in range(n):
            for k, chip in enumerate(chips):
                copy(w, 3 + k, chip, 1 - c, (x, y, c)).wait_recv()
        for w in range(n):
            for cp in first[w] + passed[w]:
                cp.wait_send()

    return 6 * n, start, finish


def hosted_gather(bufs):
    n_sems, start, finish = _gather_phase(len(bufs))
    return Hosted(bufs, [jax.ShapeDtypeStruct(b.shape, b.dtype) for b in bufs], {i: i for i in range(len(bufs))},
                  n_sems, start, finish)


def pair_swap_halves(gs):
    n = len(gs)
    n_sems, start, finish = _swap_phase(n)

    def body(*refs):
        g_refs, o_refs, send_sems, recv_sems = refs[:n], refs[n:2 * n], refs[2 * n], refs[2 * n + 1]
        start(g_refs, o_refs, send_sems, recv_sems)
        finish(g_refs, o_refs, send_sems, recv_sems)

    return _comm_call(body, name="pair_swap_halves", out_shape=_swap_shapes(gs), in_specs=[HBM] * n, out_specs=[HBM] * n,
                      scratch_shapes=[pltpu.SemaphoreType.DMA((n_sems,)), pltpu.SemaphoreType.DMA((n_sems,))])(*gs)


def _swap_shapes(gs):
    return [jax.ShapeDtypeStruct((N_CHIPS, g.shape[1] // 2, g.shape[2]), g.dtype) for g in gs]


def _swap_phase(n):
    def copies(g_refs, o_refs, send_sems, recv_sems):
        x, y, c = _place()
        out = []
        for w in range(n):
            half = g_refs[w].shape[1] // 2
            out.append(pltpu.make_async_remote_copy(
                src_ref=g_refs[w].at[:, pl.ds((1 - c) * half, half), :], dst_ref=o_refs[w], send_sem=send_sems.at[w],
                recv_sem=recv_sems.at[w], device_id=(x, y, 1 - c), device_id_type=MESH))
        return out

    def start(*refs):
        for cp in copies(*refs):
            cp.start()

    def finish(*refs):
        for cp in copies(*refs):
            cp.wait()

    return n, start, finish


def hosted_swap(gs):
    n_sems, start, finish = _swap_phase(len(gs))
    return Hosted(gs, _swap_shapes(gs), {}, n_sems, start, finish)


def scatter_chips(ps):
    n = len(ps)
    n_sems, start, finish = _scatter_phase(n)

    def body(*refs):
        p_refs, o_refs, send_sems, recv_sems = refs[:n], refs[n:2 * n], refs[2 * n], refs[2 * n + 1]
        start(p_refs, o_refs, send_sems, recv_sems)
        finish(p_refs, o_refs, send_sems, recv_sems)

    return _comm_call(body, name="scatter_chips", out_shape=_scatter_shapes(ps), in_specs=[HBM] * n, out_specs=[HBM] * n,
                      scratch_shapes=[pltpu.SemaphoreType.DMA((n_sems,)), pltpu.SemaphoreType.DMA((n_sems,))])(*ps)


def _scatter_shapes(ps):
    return [jax.ShapeDtypeStruct((3,) + p.shape[1:], p.dtype) for p in ps]


def _scatter_phase(n):
    def copies(p_refs, o_refs, send_sems, recv_sems):
        x, y, c = _place()
        return [pltpu.make_async_remote_copy(src_ref=p_refs[w].at[2 * chip[0] + chip[1]], dst_ref=o_refs[w].at[k],
                                             send_sem=send_sems.at[3 * w + k], recv_sem=recv_sems.at[3 * w + k],
                                             device_id=(*chip, c), device_id_type=MESH)
                for w in range(n) for k, chip in enumerate(_other_chips(x, y))]

    def start(*refs):
        for cp in copies(*refs):
            cp.start()

    def finish(*refs):
        for cp in copies(*refs):
            cp.wait()

    return 3 * n, start, finish


def hosted_scatter(ps):
    n_sems, start, finish = _scatter_phase(len(ps))
    return Hosted(ps, _scatter_shapes(ps), {}, n_sems, start, finish)


def pair_join_halves(rs):
    n = len(rs)

    def body(*refs):
        outs, send_sems, recv_sems = refs[n:2 * n], refs[2 * n], refs[2 * n + 1]
        x, y, c = _place()
        copies = []
        for w in range(n):
            half = outs[w].shape[0] // 2
            rows = outs[w].at[pl.ds(c * half, half), :]
            copies.append(pltpu.make_async_remote_copy(src_ref=rows, dst_ref=rows, send_sem=send_sems.at[w],
                                                       recv_sem=recv_sems.at[w], device_id=(x, y, 1 - c),
                                                       device_id_type=MESH))
        for cp in copies:
            cp.start()
        for cp in copies:
            cp.wait()

    return _inplace_comm_call(body, rs, name="pair_join_halves", n_sems=n)


def all_reduce_small(v, name):
    n, L = v.shape
    n_dev = 8

    def body(v_ref, out_ref, buf, send_sems, recv_sems):
        x, y, c = _place()
        me = 4 * x + 2 * y + c
        buf[me] = v_ref[...]

        def copy(k, slot, peer):
            return pltpu.make_async_remote_copy(src_ref=v_ref, dst_ref=buf.at[slot], send_sem=send_sems.at[k],
                                                recv_sem=recv_sems.at[slot],
                                                device_id=(peer // 4, (peer // 2) % 2, peer % 2), device_id_type=MESH)

        sends = [copy(k - 1, me, (me + k) % n_dev) for k in range(1, n_dev)]
        for cp in sends:
            cp.start()
        for k in range(1, n_dev):
            src = (me + k) % n_dev
            copy(0, src, src).wait_recv()
        for cp in sends:
            cp.wait_send()
        acc = buf[0]
        for s in range(1, n_dev):
            acc = acc + buf[s]
        out_ref[...] = acc

    return _comm_call(body, name=name, out_shape=jax.ShapeDtypeStruct((n, L), v.dtype), in_specs=[VMEM], out_specs=VMEM,
                      scratch_shapes=[pltpu.VMEM((n_dev, n, L), v.dtype), pltpu.SemaphoreType.DMA((n_dev - 1,)),
                                      pltpu.SemaphoreType.DMA((n_dev,))])(v)


BF16_ROWS = 16
STREAM_BLOCK_BYTES = 4 << 20


def _rows_tile(n, row_bytes, budget=1 << 20, mult=SUBLANES):
    best = mult if n % mult == 0 else n
    for t in range(mult, n + 1, mult):
        if n % t == 0 and t * row_bytes <= budget:
            best = t
    return best


def _scalars(*vals):
    return jnp.stack([jnp.asarray(v, jnp.int32) for v in vals])


def cast_to_slot(w, chip, name):
    r, c = w.shape
    tb = _rows_tile(r, c * 4, budget=STREAM_BLOCK_BYTES, mult=BF16_ROWS)

    def body(s_ref, w_ref, o_ref):
        o_ref[...] = w_ref[...].astype(BF16)

    spec = pltpu.PrefetchScalarGridSpec(
        num_scalar_prefetch=1, grid=(r // tb,), in_specs=[pl.BlockSpec((tb, c), lambda i, s: (i, 0))],
        out_specs=pl.BlockSpec((None, tb, c), lambda i, s: (s[0], i, 0)))
    return pl.pallas_call(body, name=name, grid_spec=spec, out_shape=jax.ShapeDtypeStruct((N_CHIPS, r, c), BF16),
                          compiler_params=pltpu.CompilerParams(dimension_semantics=("parallel",)))(_scalars(chip), w)


def pair_add(g, got, c, name):
    _, r, w = g.shape
    half = r // 2
    tb = _rows_tile(half, w * 4, budget=STREAM_BLOCK_BYTES, mult=BF16_ROWS)
    nb = half // tb

    def body(c_ref, g_ref, got_ref, o_ref):
        o_ref[...] = (g_ref[...].astype(F32) + got_ref[...].astype(F32)).astype(o_ref.dtype)

    spec = pltpu.PrefetchScalarGridSpec(
        num_scalar_prefetch=1, grid=(N_CHIPS, nb),
        in_specs=[pl.BlockSpec((None, tb, w), lambda s, i, c_ref: (s, c_ref[0] * nb + i, 0)),
                  pl.BlockSpec((None, tb, w), lambda s, i, c_ref: (s, i, 0))],
        out_specs=pl.BlockSpec((None, tb, w), lambda s, i, c_ref: (s, i, 0)))
    return pl.pallas_call(body, name=name, grid_spec=spec, out_shape=jax.ShapeDtypeStruct((N_CHIPS, half, w), BF16),
                          compiler_params=pltpu.CompilerParams(dimension_semantics=("parallel", "parallel")))(
        _scalars(c), g, got)


def chip_add(p, got, chip, c, name):
    _, h, w = p.shape
    tb = _rows_tile(h, w * 4, budget=STREAM_BLOCK_BYTES, mult=BF16_ROWS)
    nb = h // tb

    def body(s_ref, p_ref, got_ref, o_ref):
        acc = p_ref[...].astype(F32)
        for k in range(3):
            acc = acc + got_ref[k].astype(F32)
        o_ref[...] = acc

    spec = pltpu.PrefetchScalarGridSpec(
        num_scalar_prefetch=1, grid=(nb,),
        in_specs=[pl.BlockSpec((None, tb, w), lambda i, s: (s[0], i, 0)),
                  pl.BlockSpec((3, tb, w), lambda i, s: (0, i, 0))],
        out_specs=pl.BlockSpec((tb, w), lambda i, s: (s[1] * nb + i, 0)))
    return pl.pallas_call(body, name=name, grid_spec=spec, out_shape=jax.ShapeDtypeStruct((2 * h, w), F32),
                          compiler_params=pltpu.CompilerParams(dimension_semantics=("parallel",)))(
        _scalars(chip, c), p, got)


def adamw(w, g, m, v, name):
    r, c = w.shape
    tr = _rows_tile(r, c * 4, budget=STREAM_BLOCK_BYTES // 2)

    def body(w_ref, g_ref, m_ref, v_ref, d_ref, m_out, v_out):
        gg = g_ref[...]
        m2 = ADAM_B1 * m_ref[...] + (1.0 - ADAM_B1) * gg
        v2 = ADAM_B2 * v_ref[...] + (1.0 - ADAM_B2) * jnp.square(gg)
        m_hat = m2 / (1.0 - ADAM_B1 ** ADAM_STEP)
        v_hat = v2 / (1.0 - ADAM_B2 ** ADAM_STEP)
        d_ref[...] = -ADAM_LR * (m_hat / (jnp.sqrt(v_hat) + ADAM_EPS) + ADAM_WD * w_ref[...])
        m_out[...] = m2
        v_out[...] = v2

    blk = pl.BlockSpec((tr, c), lambda i: (i, 0))
    return _pcall(body, name=name, grid=(r // tr,), in_specs=[blk] * 4, out_specs=[blk] * 3,
                  out_shape=[jax.ShapeDtypeStruct((r, c), F32)] * 3, dims=("parallel",))(w, g, m, v)


WEIGHTS = ["l0_attn_norm", "l0_w_in", "l0_gdn_conv", "l0_gdn_A_log", "l0_gdn_dt_bias", "l0_gdn_norm", "l0_mla_q_norm",
           "l0_mla_w_uq", "l0_mla_kv_norm", "l0_mla_w_ukv", "l0_w_out", "l0_ffn_norm", "l0_ffn_w_up", "l0_ffn_conv_w",
           "l0_ffn_conv_b", "l0_ffn_w_down", "l0_ple_proj", "l0_ple_gate_norm", "l0_ple_gate", "l1_attn_norm",
           "l1_w_in", "l1_ret_norm", "l1_w_out", "l1_ffn_norm", "l1_ffn_w_up", "l1_ffn_conv_w", "l1_ffn_conv_b",
           "l1_ffn_w_down", "l1_ple_proj", "l1_ple_gate_norm", "l1_ple_gate", "final_norm"]
COL_SHARDED = ["l0_w_in", "l0_mla_w_uq", "l0_mla_w_ukv", "l0_ffn_w_up", "l0_ple_proj", "l1_w_in", "l1_ffn_w_up",
               "l1_ple_proj"]
ROW_SHARDED = ["l0_w_out", "l0_ffn_w_down", "l0_ple_gate", "l1_w_out", "l1_ffn_w_down", "l1_ple_gate"]
BIG = [k for k in WEIGHTS if k in COL_SHARDED or k in ROW_SHARDED]
SMALL_SHARDED = ["l0_gdn_conv", "l0_ffn_conv_w", "l1_ffn_conv_w"]
SMALL = [k for k in WEIGHTS if k not in BIG]
KEPT_SHARDED = ["l0_ffn_w_up", "l0_ple_proj", "l1_w_in", "l1_ffn_w_up", "l1_ple_proj"]
GATHER_FIRST = ["l0_w_in"]
GATHER_L0 = ["l0_mla_w_uq", "l0_mla_w_ukv", "l0_w_out", "l0_ffn_w_up", "l0_ffn_w_down", "l0_ple_proj", "l0_ple_gate",
             "l1_w_out"]
GATHER_L1 = ["l1_w_in"]
GATHER_FFN = ["l1_ffn_w_down", "l1_ple_proj", "l1_ple_gate"]
GATHER_L1_IN = ["l1_ffn_w_up"]
REDUCE_L1 = [k for k in BIG if k.startswith("l1_")]
REDUCE_DQ = ["l0_ffn_w_up"]
REDUCE_L0 = ["l0_ffn_w_down", "l0_ple_proj", "l0_ple_gate"]
REDUCE_LAST = ["l0_w_in", "l0_mla_w_uq", "l0_mla_w_ukv", "l0_w_out"]


class Exchange:
    def __init__(self, chip, core):
        self.chip, self.core = chip, core

    def gather(self, bufs):
        return all_gather_chips(bufs)

    def pair_sums(self, names, grads, swapped=None):
        swapped = pair_swap_halves(grads) if swapped is None else swapped
        return [pair_add(g, got, self.core, "rs_pair_add_" + k) for k, g, got in zip(names, grads, swapped)]

    def finish(self, sums, landed):
        halves = [chip_add(sums[k], landed[k], self.chip, self.core, "rs_chip_add_" + k) for k in BIG]
        return dict(zip(BIG, pair_join_halves(halves)))


def _cols_to_full(s):
    j, k, n = s.shape
    return jnp.transpose(s, (1, 0, 2)).reshape(k, j * n)


def _full_to_cols(g):
    k, n4 = g.shape
    return jnp.transpose(g.reshape(k, N_CHIPS, n4 // N_CHIPS), (1, 0, 2))


def _pack_small(vals):
    flat = jnp.concatenate([v.astype(F32).reshape(-1) for v in vals])
    align = SUBLANES * LANES
    flat = jnp.pad(flat, (0, -flat.shape[0] % align))
    return flat.reshape(-1, LANES)


def _unpack_small(rows, shapes):
    flat = rows.reshape(-1)
    out, off = [], 0
    for shp in shapes:
        n = int(np.prod(shp))
        out.append(flat[off:off + n].reshape(shp))
        off += n
    return out


INPUTS = (["x", "p", "positions"] + WEIGHTS + ["loss_target"] + ["m_" + k for k in WEIGHTS]
          + ["v_" + k for k in WEIGHTS])


def kernel(
        x, p, positions, l0_attn_norm, l0_w_in, l0_gdn_conv, l0_gdn_A_log, l0_gdn_dt_bias, l0_gdn_norm, l0_mla_q_norm,
        l0_mla_w_uq, l0_mla_kv_norm, l0_mla_w_ukv, l0_w_out, l0_ffn_norm, l0_ffn_w_up, l0_ffn_conv_w, l0_ffn_conv_b,
        l0_ffn_w_down, l0_ple_proj, l0_ple_gate_norm, l0_ple_gate, l1_attn_norm, l1_w_in, l1_ret_norm, l1_w_out,
        l1_ffn_norm, l1_ffn_w_up, l1_ffn_conv_w, l1_ffn_conv_b, l1_ffn_w_down, l1_ple_proj, l1_ple_gate_norm,
        l1_ple_gate, final_norm, loss_target, m_l0_attn_norm, m_l0_w_in, m_l0_gdn_conv, m_l0_gdn_A_log,
        m_l0_gdn_dt_bias, m_l0_gdn_norm, m_l0_mla_q_norm, m_l0_mla_w_uq, m_l0_mla_kv_norm, m_l0_mla_w_ukv, m_l0_w_out,
        m_l0_ffn_norm, m_l0_ffn_w_up, m_l0_ffn_conv_w, m_l0_ffn_conv_b, m_l0_ffn_w_down, m_l0_ple_proj,
        m_l0_ple_gate_norm, m_l0_ple_gate, m_l1_attn_norm, m_l1_w_in, m_l1_ret_norm, m_l1_w_out, m_l1_ffn_norm,
        m_l1_ffn_w_up, m_l1_ffn_conv_w, m_l1_ffn_conv_b, m_l1_ffn_w_down, m_l1_ple_proj, m_l1_ple_gate_norm,
        m_l1_ple_gate, m_final_norm, v_l0_attn_norm, v_l0_w_in, v_l0_gdn_conv, v_l0_gdn_A_log, v_l0_gdn_dt_bias,
        v_l0_gdn_norm, v_l0_mla_q_norm, v_l0_mla_w_uq, v_l0_mla_kv_norm, v_l0_mla_w_ukv, v_l0_w_out, v_l0_ffn_norm,
        v_l0_ffn_w_up, v_l0_ffn_conv_w, v_l0_ffn_conv_b, v_l0_ffn_w_down, v_l0_ple_proj, v_l0_ple_gate_norm,
        v_l0_ple_gate, v_l1_attn_norm, v_l1_w_in, v_l1_ret_norm, v_l1_w_out, v_l1_ffn_norm, v_l1_ffn_w_up,
        v_l1_ffn_conv_w, v_l1_ffn_conv_b, v_l1_ffn_w_down, v_l1_ple_proj, v_l1_ple_gate_norm, v_l1_ple_gate,
        v_final_norm):
    given = locals()
    a = {k: given[k] for k in INPUTS}
    x_i, y_i, c_i = _place()
    chip = 2 * x_i + y_i
    shard_shapes = {k: a[k].shape for k in WEIGHTS}

    slots = {k: cast_to_slot(a[k], chip, "cast_" + k) for k in BIG}
    W = {}
    placed = []
    for k in SMALL_SHARDED:
        r, c = shard_shapes[k]
        mine = jnp.where(c_i == 0, a[k], jnp.zeros_like(a[k]))
        placed.append(lax.dynamic_update_slice(jnp.zeros((r, N_CHIPS * c), F32), mine, (0, chip * c)))
    full_small = _unpack_small(all_reduce_small(_pack_small(placed), "gather_small_weights"),
                               [p_.shape for p_ in placed])
    for k in SMALL:
        W[k] = a[k]
    W.update(dict(zip(SMALL_SHARDED, full_small)))

    loss_part, grad_x, grads, G = local_step(a["x"][0], a["p"][:, 0], a["positions"][0], a["loss_target"][0], slots, W,
                                             Exchange(chip, c_i))
    loss = lax.psum(loss_part, ("x", "y", "c"))
    deltas, new_m, new_v = {}, {}, {}
    for k in BIG:
        deltas[k], new_m[k], new_v[k] = adamw(a[k], grads[k], a["m_" + k], a["v_" + k], "adamw_" + k)

    small_full = [G[k].reshape(-1) for k in SMALL]
    summed = _unpack_small(all_reduce_small(_pack_small(small_full), "reduce_small_grads"),
                           [G[k].shape for k in SMALL])
    for k, g in zip(SMALL, summed):
        if k in SMALL_SHARDED:
            r, c = shard_shapes[k]
            g = lax.dynamic_slice(g.reshape(r, N_CHIPS * c), (0, chip * c), (r, c))
        grads[k] = g.reshape(shard_shapes[k])
    packed = [_pack_small([d[k] for k in SMALL]) for d in (
        {k: a[k] for k in SMALL}, grads, {k: a["m_" + k] for k in SMALL}, {k: a["v_" + k] for k in SMALL})]
    outs = adamw(*packed, "adamw_small")
    shapes = [shard_shapes[k] for k in SMALL]
    for d, rows in zip((deltas, new_m, new_v), outs):
        d.update(dict(zip(SMALL, _unpack_small(rows, shapes))))

    return (loss, grad_x[None], *[grads[k] for k in WEIGHTS], *[deltas[k] for k in WEIGHTS],
            *[new_m[k] for k in WEIGHTS], *[new_v[k] for k in WEIGHTS])
```

```python
import functools
import math

import numpy as np
import jax
import jax.numpy as jnp
from jax import lax
from jax.experimental import pallas as pl
from jax.experimental.pallas import tpu as pltpu

F32, BF16 = jnp.float32, jnp.bfloat16
HI = lax.Precision.HIGHEST
MESH = pl.DeviceIdType.MESH

NORM_EPS = 1e-6
ROPE_THETA = 10000.0
D_MODEL = 2048
PLE_DIM = 256
GDN_HEADS, GDN_DK, GDN_DV, GDN_CONV = 8, 128, 128, 4
MLA_HEADS, MLA_Q_RANK, MLA_KV_RANK, MLA_NOPE, MLA_ROPE, MLA_V = 8, 512, 512, 128, 64, 128
RET_HEADS, RET_DK, RET_DV = 8, 256, 512
D_FF, FFN_CONV = 5632, 3
ADAM_LR, ADAM_B1, ADAM_B2, ADAM_EPS, ADAM_WD, ADAM_STEP = 0.001, 0.9, 0.999, 1e-08, 0.01, 10

LANES = 128
SUBLANES = 8
CHUNK = 128
N_CHIPS = 4
VMEM_LIMIT_MB = 56

ZIN_QKV, ZIN_Z, ZIN_CQ, ZIN_CKV, ZIN_KR, ZIN_AB, ZIN_W = 0, 3072, 4096, 4608, 5120, 5248, 5376


class Hosted:
    def __init__(self, inputs, out_shapes, aliases, n_sems, start, finish):
        self.inputs, self.out_shapes, self.aliases, self.n_sems = list(inputs), list(out_shapes), dict(aliases), n_sems
        self.start, self.finish = start, finish


def _pcall(body, *, name, out_shape, grid=(), in_specs=None, out_specs=None, scratch_shapes=(), dims=None,
           hosted=None):
    params = dict(vmem_limit_bytes=VMEM_LIMIT_MB << 20)
    if dims is not None:
        params["dimension_semantics"] = dims
    if hosted is None:
        return pl.pallas_call(body, name=name, out_shape=out_shape, grid=grid, in_specs=in_specs, out_specs=out_specs,
                              scratch_shapes=list(scratch_shapes), compiler_params=pltpu.CompilerParams(**params))
    single = not isinstance(out_shape, (list, tuple))
    out_shape = [out_shape] if single else list(out_shape)
    out_specs = [out_specs] if single else list(out_specs)
    n_in, n_out, n_scr = len(in_specs), len(out_shape), len(scratch_shapes)
    h_in, h_out = len(hosted.inputs), len(hosted.out_shapes)
    hbm = pl.BlockSpec(memory_space=pltpu.HBM)

    def hosting_body(*refs):
        ins, h_ins = refs[:n_in], refs[n_in:n_in + h_in]
        o0 = n_in + h_in
        outs, h_outs = refs[o0:o0 + n_out], refs[o0 + n_out:o0 + n_out + h_out]
        s0 = o0 + n_out + h_out
        scr, (send_sems, recv_sems) = refs[s0:s0 + n_scr], refs[s0 + n_scr:]
        ids = [pl.program_id(d) for d in range(len(grid))]
        first = functools.reduce(lambda u, v: u & v, [i == 0 for i in ids])
        last = functools.reduce(lambda u, v: u & v, [i == g - 1 for i, g in zip(ids, grid)])

        @pl.when(first)
        def _():
            hosted.start(h_ins, h_outs, send_sems, recv_sems)

        body(*ins, *outs, *scr)

        @pl.when(last)
        def _():
            hosted.finish(h_ins, h_outs, send_sems, recv_sems)

    params["dimension_semantics"] = ("arbitrary",) * len(grid)
    call = pl.pallas_call(
        hosting_body, name=name, out_shape=out_shape + hosted.out_shapes, grid=grid,
        in_specs=list(in_specs) + [hbm] * h_in, out_specs=out_specs + [hbm] * h_out,
        scratch_shapes=list(scratch_shapes) + [pltpu.SemaphoreType.DMA((hosted.n_sems,)),
                                               pltpu.SemaphoreType.DMA((hosted.n_sems,))],
        input_output_aliases={n_in + i: n_out + o for i, o in hosted.aliases.items()},
        compiler_params=pltpu.CompilerParams(**params))

    def run(*args):
        res = call(*args, *hosted.inputs)
        main = res[:n_out]
        return (main[0] if single else main), list(res[n_out:])

    return run


def _tile(n, target, mult=LANES):
    best = None
    for t in range(mult, min(n, target) + 1, mult):
        if n % t == 0:
            best = t
    return best or n


_DN = {"nn": (((1,), (0,)), ((), ())), "nt": (((1,), (1,)), ((), ())), "tn": (((0,), (0,)), ((), ()))}


def matmul(a, b, mode, out_dtype, *, name, add=None, b_shards=1, out_shards=1, tm=512, tn=1024, tk=2048,
           hosted=None, norm_bwd=None):
    bs = b.shape[-2:]
    if mode == "nn":
        (M, K), (K2, N) = a.shape, (bs[0], bs[1] * b_shards)
    elif mode == "nt":
        (M, K), (N, K2) = a.shape, (bs[0], bs[1] * b_shards)
    else:
        (K, M), (K2, N) = a.shape, bs
    assert K == K2, (name, a.shape, b.shape)
    n_sh = N // max(b_shards if mode == "nn" else 1, out_shards)
    k_sh = K // (b_shards if mode == "nt" else 1)
    tm, tn, tk = _tile(M, tm), _tile(n_sh, tn), _tile(k_sh, tk)
    nk = K // tk
    nbn, nbk = n_sh // tn, k_sh // tk
    dn = _DN[mode]
    has_add = add is not None
    a_bytes, b_bytes = a.size * a.dtype.itemsize, b.size * b.dtype.itemsize
    i_outer = nk > 1 or a_bytes + (M // tm) * b_bytes <= b_bytes + (N // tn) * a_bytes

    def ij(g0, g1):
        return (g0, g1) if i_outer else (g1, g0)

    fused_norm = norm_bwd is not None
    if fused_norm:
        assert tn == N and out_shards == 1 and not has_add and hosted is None, name
        i_outer = True

    def body(*refs):
        a_ref, b_ref = refs[:2]
        add_ref = refs[2] if has_add else None
        o_ref = refs[3 if has_add else 2]
        part = lax.dot_general(a_ref[...].astype(BF16), b_ref[...].astype(BF16), dn, preferred_element_type=F32)

        def finish(r):
            if fused_norm:
                h_ref, g_ref, dh_ref, o32_ref, o16_ref, dg_ref = refs[2:8]
                _, vjp = jax.vjp(_rms, h_ref[...], g_ref[...])
                dx, dg = vjp(r)
                out = dx + dh_ref[...]
                o32_ref[...] = out
                o16_ref[...] = out.astype(BF16)
                dg_ref[...] += dg
                return
            if has_add:
                r = r + add_ref[...]
            o_ref[...] = r.astype(out_dtype)

        if fused_norm:
            @pl.when((pl.program_id(0) == 0) & (pl.program_id(2) == 0))
            def _():
                refs[7][...] = jnp.zeros_like(refs[7])

        if nk == 1:
            finish(part)
            return
        acc = refs[-1]
        k = pl.program_id(2)

        @pl.when(k == 0)
        def _():
            acc[...] = part

        @pl.when(k > 0)
        def _():
            acc[...] += part

        @pl.when(k == nk - 1)
        def _():
            finish(acc[...])

    def spec(block, fn):
        return pl.BlockSpec(block, lambda g0, g1, k: fn(*ij(g0, g1), k))

    if mode == "tn":
        a_spec = spec((tk, tm), lambda i, j, k: (k, i))
    else:
        a_spec = spec((tm, tk), lambda i, j, k: (i, k))
    if mode == "nt":
        if b_shards > 1:
            b_spec = spec((None, tn, tk), lambda i, j, k: (k // nbk, j, k % nbk))
        else:
            b_spec = spec((tn, tk), lambda i, j, k: (j, k))
    elif b_shards > 1:
        b_spec = spec((None, tk, tn), lambda i, j, k: (j // nbn, k, j % nbn))
    else:
        b_spec = spec((tk, tn), lambda i, j, k: (k, j))
    in_specs = [a_spec, b_spec]
    args = [a, b]
    if has_add:
        in_specs.append(spec((tm, tn), lambda i, j, k: (i, j)))
        args.append(add)
    if out_shards > 1:
        out_spec = spec((None, tm, tn), lambda i, j, k: (j // nbn, i, j % nbn))
        out_shape = jax.ShapeDtypeStruct((out_shards, M, n_sh), out_dtype)
    else:
        out_spec = spec((tm, tn), lambda i, j, k: (i, j))
        out_shape = jax.ShapeDtypeStruct((M, N), out_dtype)
    gi, gj = M // tm, N // tn
    if fused_norm:
        h, gain, dh = norm_bwd
        row_blk = spec((tm, N), lambda i, j, k: (i, 0))
        gain_blk = spec((1, N), lambda i, j, k: (0, 0))
        return _pcall(body, name=name, grid=(gi, 1, nk), in_specs=in_specs + [row_blk, gain_blk, row_blk],
                      out_specs=[row_blk, row_blk, gain_blk],
                      out_shape=[jax.ShapeDtypeStruct((M, N), F32), jax.ShapeDtypeStruct((M, N), BF16),
                                 jax.ShapeDtypeStruct((1, N), F32)],
                      scratch_shapes=[pltpu.VMEM((tm, tn), F32)] if nk > 1 else [],
                      dims=("arbitrary", "arbitrary", "arbitrary"))(a, b, h, gain, dh)
    return _pcall(body, name=name, out_shape=out_shape, grid=(gi, gj, nk) if i_outer else (gj, gi, nk),
                  in_specs=in_specs, out_specs=out_spec,
                  scratch_shapes=[pltpu.VMEM((tm, tn), F32)] if nk > 1 else [],
                  dims=("parallel", "parallel", "arbitrary"), hosted=hosted)(*args)


def _row_spec(tr, w, c):
    return pl.BlockSpec((tr, w), lambda i: (i, c))


def _full_spec(arr):
    return pl.BlockSpec(arr.shape, lambda i: (0,) * arr.ndim)


def rowwise(fn, rows, params, nd_rows, outs, *, name, tr=256):
    S = rows[0][0].shape[0]
    tr = min(tr, S)
    n_in = len(rows) + len(params) + len(nd_rows)

    def body(*refs):
        res = fn(*[x[...] for x in refs[:n_in]])
        for o_ref, v in zip(refs[n_in:], res):
            o_ref[...] = v.astype(o_ref.dtype)

    return _pcall(body, name=name, grid=(S // tr,),
                  in_specs=([_row_spec(tr, w, c) for (_, w, c) in rows] + [_full_spec(q) for q in params]
                            + [_row_spec(tr, w, c) for (_, w, c) in nd_rows]),
                  out_specs=[_row_spec(tr, w, 0) for (w, _) in outs],
                  out_shape=[jax.ShapeDtypeStruct((S, w), dt) for (w, dt) in outs],
                  dims=("parallel",))(*[r[0] for r in rows], *params, *[r[0] for r in nd_rows])


def rowwise_bwd(fn, rows, params, nd_rows, cts, d_dtypes, *, name, adds=None, tr=256):
    S = rows[0][0].shape[0]
    tr = min(tr, S)
    n_r, n_p, n_n, n_c = len(rows), len(params), len(nd_rows), len(cts)
    adds = adds or [None] * n_r
    add_list = [a for a in adds if a is not None]
    n_a = len(add_list)
    d_dtypes = [dt if isinstance(dt, (list, tuple)) else (dt,) for dt in d_dtypes]
    n_d = sum(len(dt) for dt in d_dtypes)

    def body(*refs):
        it = iter(refs)
        r = [next(it)[...] for _ in range(n_r)]
        p = [next(it)[...] for _ in range(n_p)]
        nd = [next(it)[...] for _ in range(n_n)]
        c = [next(it)[...] for _ in range(n_c)]
        ad = [next(it)[...] for _ in range(n_a)]
        d_row_refs = [[next(it) for _ in dts] for dts in d_dtypes]
        d_par_refs = [next(it) for _ in range(n_p)]
        outs, vjp = jax.vjp(lambda *dp: fn(*dp, *nd), *r, *p)
        g = vjp(tuple(ci.astype(o.dtype) for ci, o in zip(c, outs)))
        ai = 0
        for k in range(n_r):
            gk = g[k].astype(F32)
            if adds[k] is not None:
                gk = gk + ad[ai].astype(F32)
                ai += 1
            for ref in d_row_refs[k]:
                ref[...] = gk.astype(ref.dtype)

        @pl.when(pl.program_id(0) == 0)
        def _():
            for ref in d_par_refs:
                ref[...] = jnp.zeros_like(ref)

        for k in range(n_p):
            d_par_refs[k][...] += g[n_r + k].astype(F32)

    in_specs = ([_row_spec(tr, w, c) for (_, w, c) in rows] + [_full_spec(q) for q in params]
                + [_row_spec(tr, w, c) for (_, w, c) in nd_rows] + [_row_spec(tr, w, c) for (_, w, c) in cts]
                + [_row_spec(tr, w, c) for (_, w, c) in add_list])
    out_specs = ([_row_spec(tr, w, 0) for (_, w, _), dts in zip(rows, d_dtypes) for _ in dts]
                 + [_full_spec(q) for q in params])
    out_shape = ([jax.ShapeDtypeStruct((S, w), dt) for (_, w, _), dts in zip(rows, d_dtypes) for dt in dts]
                 + [jax.ShapeDtypeStruct(q.shape, F32) for q in params])
    res = _pcall(body, name=name, grid=(S // tr,), in_specs=in_specs, out_specs=out_specs, out_shape=out_shape,
                 dims=("arbitrary",))(*[r[0] for r in rows], *params, *[r[0] for r in nd_rows],
                                      *[r[0] for r in cts], *[r[0] for r in add_list])
    d_rows, i = [], 0
    for dts in d_dtypes:
        d_rows.append(res[i] if len(dts) == 1 else tuple(res[i:i + len(dts)]))
        i += len(dts)
    return d_rows, res[n_d:]


def _rms(x, g):
    x = x.astype(F32)
    return x * lax.rsqrt(jnp.mean(x * x, axis=-1, keepdims=True) + NORM_EPS) * g


def _fn_rms(x, g):
    return (_rms(x, g),)


def _sigmoid(x):
    return 1.0 / (1.0 + jnp.exp(-x))


def _silu(x):
    return x * _sigmoid(x)


def _softplus(x):
    return jnp.maximum(x, 0.0) + jnp.log(1.0 + jnp.exp(-jnp.abs(x)))


def _fn_ple(h, pp, gl):
    return (h.astype(F32) + pp.astype(F32) * _sigmoid(gl.astype(F32)),)


def _fn_ple_terms(pp, gl):
    return (pp.astype(F32) * _sigmoid(gl.astype(F32)),)


def _rot_half_matrix():
    half = MLA_ROPE // 2
    r = lax.broadcasted_iota(jnp.int32, (LANES, LANES), 0)
    c = lax.broadcasted_iota(jnp.int32, (LANES, LANES), 1)
    plus = (c == r + half) & (r < half)
    minus = (r == c + half) & (c < half)
    return jnp.where(plus, 1.0, 0.0) - jnp.where(minus, 1.0, 0.0)


def _rope_pad(x, cosp, sinp):
    return x * cosp + jnp.dot(x, _rot_half_matrix(), precision=HI, preferred_element_type=F32) * sinp


def _fn_mla_pre(cq, ckv, kr, qn_w, kvn_w, cosp, sinp):
    return (_rms(cq, qn_w), _rms(ckv, kvn_w), _rope_pad(kr.astype(F32), cosp, sinp))


def _fn_rope_q(q, cosp, sinp):
    q = q.astype(F32)
    parts = []
    for h in range(MLA_HEADS):
        base = 2 * LANES * h
        parts.append(q[:, base:base + LANES])
        parts.append(_rope_pad(q[:, base + LANES:base + 2 * LANES], cosp, sinp))
    return (jnp.concatenate(parts, axis=1),)


def _fn_ret_gate(g, on, w):
    return (_silu(g.astype(F32)) * (on.astype(F32) * w),)


def _shift_down(cur, halo, s):
    if s == 0:
        return cur
    r = pltpu.roll(cur, s, 0)
    hs = pltpu.roll(halo, s, 0)
    row = lax.broadcasted_iota(jnp.int32, hs.shape, 0)
    first = jnp.where(row < s, hs, r[:SUBLANES])
    return jnp.concatenate([first, r[SUBLANES:]], axis=0)


def _shift_up(cur, halo, s):
    if s == 0:
        return cur
    n = cur.shape[0]
    r = pltpu.roll(cur, n - s, 0)
    hs = pltpu.roll(halo, SUBLANES - s, 0)
    row = lax.broadcasted_iota(jnp.int32, hs.shape, 0)
    last = jnp.where(row >= SUBLANES - s, hs, r[n - SUBLANES:])
    return jnp.concatenate([r[:n - SUBLANES], last], axis=0)


def _prev_halo_spec(tr, tw, col):
    return pl.BlockSpec((SUBLANES, tw), lambda c, i: (jnp.maximum(i * (tr // SUBLANES) - 1, 0), col(c)))


def _conv_taps(cur, halo, w_ref, width):
    taps = [_shift_down(cur, halo, width - 1 - j) for j in range(width)]
    y = taps[0] * w_ref[0:1, :]
    for j in range(1, width):
        y = y + taps[j] * w_ref[j:j + 1, :]
    return y, taps


def gdn_conv_fwd(zin, w, *, tr=512, tw=512):
    S = zin.shape[0]
    tr = min(tr, S)
    width, C = w.shape

    def body(cur_ref, halo_ref, w_ref, o_ref):
        i = pl.program_id(1)
        halo = halo_ref[...] * (i > 0).astype(F32)
        y, _ = _conv_taps(cur_ref[...], halo, w_ref, width)
        o_ref[...] = _silu(y)

    return _pcall(body, name="gdn_conv_fwd", grid=(C // tw, S // tr),
                  in_specs=[pl.BlockSpec((tr, tw), lambda c, i: (i, c)), _prev_halo_spec(tr, tw, lambda c: c),
                            pl.BlockSpec((width, tw), lambda c, i: (0, c))],
                  out_specs=pl.BlockSpec((tr, tw), lambda c, i: (i, c)),
                  out_shape=jax.ShapeDtypeStruct((S, C), F32), dims=("parallel", "arbitrary"))(zin, zin, w)


def gdn_conv_bwd(zin, w, dy, *, tr=512, tw=512):
    S = zin.shape[0]
    tr = min(tr, S)
    width, C = w.shape

    def body(cur_ref, halo_ref, w_ref, dy_ref, da_ref, dw_ref):
        i = pl.program_id(1)
        halo = halo_ref[...] * (i > 0).astype(F32)
        y, taps = _conv_taps(cur_ref[...], halo, w_ref, width)
        sg = _sigmoid(y)
        da = dy_ref[...] * (sg * (1.0 + y * (1.0 - sg)))
        da_ref[...] = da

        @pl.when(i == 0)
        def _():
            dw_ref[...] = jnp.zeros_like(dw_ref)

        for j in range(width):
            dw_ref[j:j + 1, :] += jnp.sum(da * taps[j], axis=0, keepdims=True)

    return _pcall(body, name="gdn_conv_bwd", grid=(C // tw, S // tr),
                  in_specs=[pl.BlockSpec((tr, tw), lambda c, i: (i, c)), _prev_halo_spec(tr, tw, lambda c: c),
                            pl.BlockSpec((width, tw), lambda c, i: (0, c)),
                            pl.BlockSpec((tr, tw), lambda c, i: (i, c))],
                  out_specs=[pl.BlockSpec((tr, tw), lambda c, i: (i, c)),
                             pl.BlockSpec((width, tw), lambda c, i: (0, c))],
                  out_shape=[jax.ShapeDtypeStruct((S, C), F32), jax.ShapeDtypeStruct((width, C), F32)],
                  dims=("parallel", "arbitrary"))(zin, zin, w, dy)


def conv_transpose(dy, w, out_dtype, *, name, tr=512, tw=512):
    if dy.ndim == 2:
        dy = dy[None]
    T, S, C = dy.shape
    tr = min(tr, S)
    width = w.shape[0]
    n_i, nc = S // tr, C // tw

    def body(cur_ref, halo_ref, w_ref, o_ref):
        i = pl.program_id(2)
        halo = halo_ref[...] * (i < n_i - 1).astype(F32)
        cur = cur_ref[...]
        acc = cur * w_ref[width - 1:width, :]
        for s in range(1, width):
            acc = acc + _shift_up(cur, halo, s) * w_ref[width - 1 - s:width - s, :]
        o_ref[...] = acc.astype(out_dtype)

    nxt = pl.BlockSpec((None, SUBLANES, tw),
                       lambda t, c, i: (t, jnp.minimum((i + 1) * (tr // SUBLANES), S // SUBLANES - 1), c))
    return _pcall(body, name=name, grid=(T, nc, n_i),
                  in_specs=[pl.BlockSpec((None, tr, tw), lambda t, c, i: (t, i, c)), nxt,
                            pl.BlockSpec((width, tw), lambda t, c, i: (0, t * nc + c))],
                  out_specs=pl.BlockSpec((tr, tw), lambda t, c, i: (i, t * nc + c)),
                  out_shape=jax.ShapeDtypeStruct((S, T * C), out_dtype),
                  dims=("parallel", "parallel", "arbitrary"))(dy, dy, w)


def ffn_conv_fwd(u, w, b, *, tr=1024, tw=256):
    S, C2 = u.shape
    tr = min(tr, S)
    width = w.shape[0]
    half = C2 // 2
    nc = half // tw

    def body(g_ref, gh_ref, u_ref, uh_ref, wg_ref, wu_ref, bg_ref, bu_ref, o_ref):
        i = pl.program_id(1)
        live = (i > 0).astype(F32)
        yg, _ = _conv_taps(g_ref[...], gh_ref[...] * live, wg_ref, width)
        yu, _ = _conv_taps(u_ref[...], uh_ref[...] * live, wu_ref, width)
        o_ref[...] = (_silu(yg + bg_ref[...]) * (yu + bu_ref[...])).astype(o_ref.dtype)

    return _pcall(body, name="ffn_conv_fwd", grid=(nc, S // tr),
                  in_specs=[pl.BlockSpec((tr, tw), lambda c, i: (i, c)), _prev_halo_spec(tr, tw, lambda c: c),
                            pl.BlockSpec((tr, tw), lambda c, i: (i, c + nc)),
                            _prev_halo_spec(tr, tw, lambda c: c + nc),
                            pl.BlockSpec((width, tw), lambda c, i: (0, c)),
                            pl.BlockSpec((width, tw), lambda c, i: (0, c + nc)),
                            pl.BlockSpec((1, tw), lambda c, i: (0, c)), pl.BlockSpec((1, tw), lambda c, i: (0, c + nc))],
                  out_specs=pl.BlockSpec((tr, tw), lambda c, i: (i, c)),
                  out_shape=jax.ShapeDtypeStruct((S, half), BF16),
                  dims=("parallel", "arbitrary"))(u, u, u, u, w, w, b, b)


def ffn_conv_bwd(u, w, b, df, *, tr=512, tw=512):
    S, C2 = u.shape
    tr = min(tr, S)
    width = w.shape[0]
    half = C2 // 2
    nc = half // tw

    def body(g_ref, gh_ref, u_ref, uh_ref, wg_ref, wu_ref, bg_ref, bu_ref, df_ref, dc_ref, dw_ref, db_ref):
        i = pl.program_id(1)
        live = (i > 0).astype(F32)
        yg, gt = _conv_taps(g_ref[...], gh_ref[...] * live, wg_ref, width)
        yu, ut = _conv_taps(u_ref[...], uh_ref[...] * live, wu_ref, width)
        yg = yg + bg_ref[...]
        yu = yu + bu_ref[...]
        sg = _sigmoid(yg)
        dfv = df_ref[...].astype(F32)
        dcs = (dfv * yu * (sg * (1.0 + yg * (1.0 - sg))), dfv * (yg * sg))

        @pl.when(i == 0)
        def _():
            dw_ref[...] = jnp.zeros_like(dw_ref)
            db_ref[...] = jnp.zeros_like(db_ref)

        for t, (dc, taps) in enumerate(zip(dcs, (gt, ut))):
            dc_ref[t] = dc
            db_ref[t] += jnp.sum(dc, axis=0, keepdims=True)
            for j in range(width):
                dw_ref[t, j:j + 1, :] += jnp.sum(dc * taps[j], axis=0, keepdims=True)

    dc, dw, db = _pcall(
        body, name="ffn_conv_bwd", grid=(nc, S // tr),
        in_specs=[pl.BlockSpec((tr, tw), lambda c, i: (i, c)), _prev_halo_spec(tr, tw, lambda c: c),
                  pl.BlockSpec((tr, tw), lambda c, i: (i, c + nc)), _prev_halo_spec(tr, tw, lambda c: c + nc),
                  pl.BlockSpec((width, tw), lambda c, i: (0, c)), pl.BlockSpec((width, tw), lambda c, i: (0, c + nc)),
                  pl.BlockSpec((1, tw), lambda c, i: (0, c)), pl.BlockSpec((1, tw), lambda c, i: (0, c + nc)),
                  pl.BlockSpec((tr, tw), lambda c, i: (i, c))],
        out_specs=[pl.BlockSpec((2, tr, tw), lambda c, i: (0, i, c)), pl.BlockSpec((2, width, tw), lambda c, i: (0, 0, c)),
                   pl.BlockSpec((2, 1, tw), lambda c, i: (0, 0, c))],
        out_shape=[jax.ShapeDtypeStruct((2, S, half), F32), jax.ShapeDtypeStruct((2, width, half), F32),
                   jax.ShapeDtypeStruct((2, 1, half), F32)],
        dims=("parallel", "arbitrary"))(u, u, u, u, w, w, b, b, df)
    return dc, jnp.concatenate([dw[0], dw[1]], axis=1), jnp.concatenate([db[0], db[1]], axis=1)


_MODE_OF = {v: k for k, v in _DN.items()}


def _bf16_dot(a, b, mode):
    return lax.dot_general(a.astype(BF16), b.astype(BF16), _DN[mode], preferred_element_type=F32)


@functools.partial(jax.custom_vjp, nondiff_argnums=(2,))
def _bdot_mode(a, b, mode):
    return _bf16_dot(a, b, mode)


def _bdot_fwd(a, b, mode):
    return _bf16_dot(a, b, mode), (a, b)


def _bdot_bwd(mode, res, ct):
    a, b = res
    if mode == "nn":
        da, db = _bf16_dot(ct, b, "nt"), _bf16_dot(a, ct, "tn")
    elif mode == "nt":
        da, db = _bf16_dot(ct, b, "nn"), _bf16_dot(ct, a, "tn")
    else:
        da, db = _bf16_dot(b, ct, "nt"), _bf16_dot(a, ct, "nn")
    return da.astype(a.dtype), db.astype(b.dtype)


_bdot_mode.defvjp(_bdot_fwd, _bdot_bwd)


def _bdot(a, b, dn=_DN["nn"]):
    return _bdot_mode(a, b, _MODE_OF[dn])


def _hi_lo(x):
    hi = x.astype(BF16)
    return hi, (x - hi.astype(F32)).astype(BF16)


def _dot3_raw(a, b, mode):
    a1, a2 = _hi_lo(a)
    b1, b2 = _hi_lo(b)
    dot = lambda p, q: lax.dot_general(p, q, _DN[mode], preferred_element_type=F32)
    return dot(a1, b1) + (dot(a1, b2) + dot(a2, b1))


@functools.partial(jax.custom_vjp, nondiff_argnums=(2,))
def _dot3(a, b, mode="nn"):
    return _dot3_raw(a, b, mode)


def _dot3_fwd(a, b, mode):
    return _dot3_raw(a, b, mode), (a, b)


def _dot3_bwd(mode, res, ct):
    a, b = res
    if mode == "nn":
        return _dot3_raw(ct, b, "nt"), _dot3_raw(a, ct, "tn")
    if mode == "nt":
        return _dot3_raw(ct, b, "nn"), _dot3_raw(ct, a, "tn")
    return _dot3_raw(b, ct, "nt"), _dot3_raw(a, ct, "nn")


_dot3.defvjp(_dot3_fwd, _dot3_bwd)


@functools.partial(jax.custom_vjp, nondiff_argnums=(2,))
def _gdot(a, b, mode="nn"):
    return _bf16_dot(a, b, mode)


def _gdot_fwd(a, b, mode):
    return _bf16_dot(a, b, mode), (a.astype(BF16), b.astype(BF16))


def _ct_dot(p, q, mode, ct_first):
    ct, r = (p, q) if ct_first else (q, p)
    c1, c2 = _hi_lo(ct)
    dot = lambda c: lax.dot_general(*((c, r) if ct_first else (r, c)), _DN[mode], preferred_element_type=F32)
    return dot(c1) + dot(c2)


def _gdot_bwd(mode, res, ct):
    a, b = res
    if mode == "nn":
        return _ct_dot(ct, b, "nt", True), _ct_dot(a, ct, "tn", False)
    if mode == "nt":
        return _ct_dot(ct, b, "nn", True), _ct_dot(ct, a, "tn", True)
    return _ct_dot(b, ct, "nt", False), _ct_dot(a, ct, "nn", False)


_gdot.defvjp(_gdot_fwd, _gdot_bwd)


def _split_dot(ones, x):
    x1 = x.astype(BF16)
    r1 = x - x1.astype(F32)
    x2 = r1.astype(BF16)
    x3 = (r1 - x2.astype(F32)).astype(BF16)
    m = ones.astype(BF16)
    dot = lambda p: lax.dot_general(m, p, _DN["nn"], preferred_element_type=F32)
    return dot(x1) + dot(x2) + dot(x3)


@jax.custom_vjp
def _tri_cumsum(x, lower, upper):
    return _split_dot(lower, x)


def _tri_cumsum_fwd(x, lower, upper):
    return _split_dot(lower, x), (lower, upper)


def _tri_cumsum_bwd(res, ct):
    lower, upper = res
    return _split_dot(upper, ct), jnp.zeros_like(lower), jnp.zeros_like(upper)


_tri_cumsum.defvjp(_tri_cumsum_fwd, _tri_cumsum_bwd)


def _tri_masks(n):
    r = lax.broadcasted_iota(jnp.int32, (n, n), 0)
    c = lax.broadcasted_iota(jnp.int32, (n, n), 1)
    return r >= c, r > c


def _gdn_chunk(q, k, v, z, ab, a_row, dt_row, norm_w, state, sel_a, sel_b):
    C = q.shape[0]
    incl, strict = _tri_masks(C)
    lower = jnp.where(incl, 1.0, 0.0)
    qn = q * lax.rsqrt(jnp.sum(q * q, axis=-1, keepdims=True) + NORM_EPS) * (GDN_DK ** -0.5)
    kn = k * lax.rsqrt(jnp.sum(k * k, axis=-1, keepdims=True) + NORM_EPS)
    g = jnp.sum(-jnp.exp(a_row) * _softplus(ab + dt_row) * sel_a, axis=-1, keepdims=True)
    beta = jnp.sum(_sigmoid(ab) * sel_b, axis=-1, keepdims=True)
    gb = jnp.broadcast_to(g, (C, C))
    g_col = _tri_cumsum(gb, lower, jnp.where(strict, 0.0, 1.0))
    g_row = g_col.T
    g_last = jnp.sum(gb, axis=0, keepdims=True)
    gamma = jnp.where(incl, jnp.exp(jnp.where(incl, g_col - g_row, 0.0)), 0.0)
    e_col = jnp.exp(g_col)
    kb = kn * beta
    a_mat = jnp.where(strict, _gdot(kb, kn, "nt") * gamma, 0.0)
    x = jnp.concatenate([v * beta, kb * e_col], axis=1)
    pw = -a_mat
    steps = int(math.log2(C))
    for it in range(steps):
        x = x + _dot3(pw, x, "nn")
        if it < steps - 1:
            pw = _dot3(pw, pw, "nn")
    u, w = x[:, :GDN_DV], x[:, GDN_DV:]
    attn = _gdot(qn, kn, "nt") * gamma
    q_dec = qn * e_col
    k_dec = kn * jnp.exp(g_last - g_col)
    v_new = u - _gdot(w, state, "nn")
    o = _gdot(q_dec, state, "nn") + _gdot(attn, v_new, "nn")
    state_new = state * jnp.exp(jnp.broadcast_to(g_last, state.shape)) + _gdot(k_dec, v_new, "tn")
    y = _rms(o, norm_w) * _silu(z)
    return y, state_new


def _head_selectors(h):
    lane = lax.broadcasted_iota(jnp.int32, (1, LANES), 1)
    return jnp.where(lane == h, 1.0, 0.0), jnp.where(lane == h + GDN_HEADS, 1.0, 0.0)


GDN_HPS = 4
GDN_W = GDN_HPS * LANES


def _gdn_in_specs(rev, nc):
    def n_(n):
        return nc - 1 - n if rev else n
    G = GDN_HEADS // GDN_HPS
    blk = lambda off: pl.BlockSpec((CHUNK, GDN_W), lambda n, h: (n_(n), off + h))
    row = pl.BlockSpec((1, LANES), lambda n, h: (0, 0))
    return n_, [blk(0), blk(G), blk(2 * G), blk(ZIN_Z // GDN_W),
                pl.BlockSpec((CHUNK, LANES), lambda n, h: (n_(n), ZIN_AB // LANES)), row, row, row]


def _lanes(ref, j):
    return ref[:, j * LANES:(j + 1) * LANES]


def _hosting(call, hosted, *args):
    res = call(*args)
    return res if hosted is not None else (res, [])


def gdn_fwd(qkv, zin, a_row, dt_row, norm_w, hosted=None):
    S = qkv.shape[0]
    nc = S // CHUNK
    H = GDN_HEADS
    _, in_specs = _gdn_in_specs(False, nc)

    def body(q_ref, k_ref, v_ref, z_ref, ab_ref, a_ref, dt_ref, nw_ref, y_ref, st_ref, state):
        n, g = pl.program_id(0), pl.program_id(1)
        @pl.when((n == 0) & (g == 0))
        def _():
            state[...] = jnp.zeros_like(state)

        res = []
        for j in range(GDN_HPS):
            h = g * GDN_HPS + j
            st = state[h]
            sel_a, sel_b = _head_selectors(h)
            res.append((st,) + _gdn_chunk(_lanes(q_ref, j), _lanes(k_ref, j), _lanes(v_ref, j), _lanes(z_ref, j),
                                          ab_ref[...], a_ref[...], dt_ref[...], nw_ref[...], st, sel_a, sel_b))
        for j, (st, y, st_new) in enumerate(res):
            st_ref[j] = st
            y_ref[:, j * LANES:(j + 1) * LANES] = y.astype(y_ref.dtype)
            state[g * GDN_HPS + j] = st_new

    call = _pcall(body, name="gdn_fwd", grid=(nc, H // GDN_HPS), in_specs=in_specs,
                  out_specs=[pl.BlockSpec((CHUNK, GDN_W), lambda n, h: (n, h)),
                             pl.BlockSpec((GDN_HPS, None, GDN_DK, GDN_DV), lambda n, h: (h, n, 0, 0))],
                  out_shape=[jax.ShapeDtypeStruct((S, H * GDN_DV), BF16),
                             jax.ShapeDtypeStruct((H, nc, GDN_DK, GDN_DV), F32)],
                  scratch_shapes=[pltpu.VMEM((H, GDN_DK, GDN_DV), F32)],
                  dims=("arbitrary", "arbitrary"), hosted=hosted)
    return _hosting(call, hosted, qkv, qkv, qkv, zin, zin, a_row, dt_row, norm_w)


def gdn_bwd(qkv, zin, a_row, dt_row, norm_w, states, dy, dy_col0, hosted=None):
    S = qkv.shape[0]
    nc = S // CHUNK
    H = GDN_HEADS
    n_, in_specs = _gdn_in_specs(True, nc)
    assert dy_col0 % GDN_HPS == 0
    in_specs = in_specs + [pl.BlockSpec((GDN_HPS, None, GDN_DK, GDN_DV), lambda n, h: (h, n_(n), 0, 0)),
                           pl.BlockSpec((CHUNK, GDN_W), lambda n, h: (n_(n), dy_col0 // GDN_HPS + h))]

    def body(q_ref, k_ref, v_ref, z_ref, ab_ref, a_ref, dt_ref, nw_ref, st_ref, dy_ref,
             dq_ref, dk_ref, dv_ref, dz_ref, dab_ref, da_ref, ddt_ref, dnw_ref, dstate):
        n, g = pl.program_id(0), pl.program_id(1)

        @pl.when((n == 0) & (g == 0))
        def _():
            da_ref[...] = jnp.zeros_like(da_ref)
            ddt_ref[...] = jnp.zeros_like(ddt_ref)
            dnw_ref[...] = jnp.zeros_like(dnw_ref)
            dstate[...] = jnp.zeros_like(dstate)

        @pl.when(g == 0)
        def _():
            dab_ref[...] = jnp.zeros_like(dab_ref)

        res = []
        for j in range(GDN_HPS):
            h = g * GDN_HPS + j
            sel_a, sel_b = _head_selectors(h)
            _, vjp = jax.vjp(lambda *a, sa=sel_a, sb=sel_b: _gdn_chunk(*a, sa, sb), _lanes(q_ref, j), _lanes(k_ref, j),
                             _lanes(v_ref, j), _lanes(z_ref, j), ab_ref[...], a_ref[...], dt_ref[...], nw_ref[...],
                             st_ref[j])
            res.append(vjp((_lanes(dy_ref, j).astype(F32), dstate[h])))
        for j, (dq, dk, dv, dz, dab, da, ddt, dnw, dst) in enumerate(res):
            cols = slice(j * LANES, (j + 1) * LANES)
            dq_ref[:, cols] = dq
            dk_ref[:, cols] = dk
            dv_ref[:, cols] = dv
            dz_ref[:, cols] = dz.astype(dz_ref.dtype)
            dstate[g * GDN_HPS + j] = dst
        dab_ref[...] += sum(r[4] for r in res)
        da_ref[...] += sum(r[5] for r in res)
        ddt_ref[...] += sum(r[6] for r in res)
        dnw_ref[...] += sum(r[7] for r in res)

    blk = pl.BlockSpec((CHUNK, GDN_W), lambda n, h: (n_(n), h))
    row = pl.BlockSpec((1, LANES), lambda n, h: (0, 0))
    wide = jax.ShapeDtypeStruct((S, H * LANES), F32)
    call = _pcall(body, name="gdn_bwd", grid=(nc, H // GDN_HPS), in_specs=in_specs,
                  out_specs=[blk, blk, blk, blk, pl.BlockSpec((CHUNK, LANES), lambda n, h: (n_(n), 0)), row, row, row],
                  out_shape=[wide, wide, wide, jax.ShapeDtypeStruct((S, H * LANES), BF16),
                             jax.ShapeDtypeStruct((S, LANES), F32)] + [jax.ShapeDtypeStruct((1, LANES), F32)] * 3,
                  scratch_shapes=[pltpu.VMEM((H, GDN_DK, GDN_DV), F32)],
                  dims=("arbitrary", "arbitrary"), hosted=hosted)
    return _hosting(call, hosted, qkv, qkv, qkv, zin, zin, a_row, dt_row, norm_w, states, dy)


def _rope_full(x, cos, sin):
    x1, x2 = x[:, :RET_DK // 2], x[:, RET_DK // 2:]
    return jnp.concatenate([x1 * cos - x2 * sin, x2 * cos + x1 * sin], axis=1)


RET_CHUNK = 512


def _ret_chunk(q, k, v, cos, sin, lg, state):
    C = q.shape[0]
    incl, _ = _tri_masks(C)
    qr = _rope_full(q, cos, sin)
    kr = _rope_full(k, cos, sin) * (RET_DK ** -0.5)
    r = lax.broadcasted_iota(jnp.int32, (C, C), 0)
    c = lax.broadcasted_iota(jnp.int32, (C, C), 1)
    dist = jnp.where(incl, (r - c).astype(F32), 0.0)
    lg1 = lg[:, :1]
    decay = jnp.where(incl, jnp.exp(dist * lg1), 0.0)
    pos = lax.broadcasted_iota(jnp.int32, (C, 1), 0).astype(F32)
    xi = jnp.exp((pos + 1.0) * lg1)
    zeta = jnp.exp((C - 1.0 - pos) * lg1)
    inner = _bdot(_bdot(qr, kr, _DN["nt"]) * decay, v)
    cross = _bdot(qr * xi, state)
    state_new = state * jnp.exp(C * lg1) + _bdot(kr * zeta, v, _DN["tn"])
    o = inner + cross
    mu = jnp.mean(o, axis=-1, keepdims=True)
    var = jnp.mean(jnp.square(o - mu), axis=-1, keepdims=True)
    return (o - mu) * lax.rsqrt(var + NORM_EPS), state_new


def _ret_log_gamma():
    lg = np.log1p(-np.power(2.0, -5.0 - np.arange(RET_HEADS, dtype=np.float64))).astype(np.float32)
    return jnp.asarray(np.broadcast_to(lg[:, None, None], (RET_HEADS, 1, LANES)).copy())


def _ret_in_specs(rev, nc):
    def n_(n):
        return nc - 1 - n if rev else n
    H = RET_HEADS
    return n_, [pl.BlockSpec((RET_CHUNK, RET_DK),lambda n, h: (n_(n), h)),
                pl.BlockSpec((RET_CHUNK, RET_DK),lambda n, h: (n_(n), H + h)),
                pl.BlockSpec((RET_CHUNK, RET_DV),lambda n, h: (n_(n), 2 * H * RET_DK // RET_DV + h)),
                pl.BlockSpec((RET_CHUNK, LANES), lambda n, h: (n_(n), 0)),
                pl.BlockSpec((RET_CHUNK, LANES), lambda n, h: (n_(n), 0)),
                pl.BlockSpec((None, 1, LANES), lambda n, h: (h, 0, 0))]


def ret_fwd(zz, cos, sin, hosted=None):
    S = zz.shape[0]
    nc = S // RET_CHUNK
    H = RET_HEADS
    _, in_specs = _ret_in_specs(False, nc)

    def body(q_ref, k_ref, v_ref, cos_ref, sin_ref, lg_ref, o_ref, st_ref, state):
        n, h = pl.program_id(0), pl.program_id(1)

        @pl.when(n == 0)
        def _():
            state[h] = jnp.zeros((RET_DK, RET_DV), F32)

        st = state[h]
        st_ref[...] = st
        o, st_new = _ret_chunk(q_ref[...], k_ref[...], v_ref[...], cos_ref[...], sin_ref[...], lg_ref[...], st)
        o_ref[...] = o
        state[h] = st_new

    call = _pcall(body, name="ret_fwd", grid=(nc, H), in_specs=in_specs,
                  out_specs=[pl.BlockSpec((RET_CHUNK, RET_DV),lambda n, h: (n, h)),
                             pl.BlockSpec((None, None, RET_DK, RET_DV), lambda n, h: (h, n, 0, 0))],
                  out_shape=[jax.ShapeDtypeStruct((S, H * RET_DV), F32),
                             jax.ShapeDtypeStruct((H, nc, RET_DK, RET_DV), F32)],
                  scratch_shapes=[pltpu.VMEM((H, RET_DK, RET_DV), F32)],
                  dims=("arbitrary", "arbitrary"), hosted=hosted)
    return _hosting(call, hosted, zz, zz, zz, cos, sin, _ret_log_gamma())


def ret_bwd(zz, cos, sin, states, do):
    S = zz.shape[0]
    nc = S // RET_CHUNK
    H = RET_HEADS
    n_, in_specs = _ret_in_specs(True, nc)
    in_specs = in_specs + [pl.BlockSpec((None, None, RET_DK, RET_DV), lambda n, h: (h, n_(n), 0, 0)),
                           pl.BlockSpec((RET_CHUNK, RET_DV),lambda n, h: (n_(n), h))]

    def body(q_ref, k_ref, v_ref, cos_ref, sin_ref, lg_ref, st_ref, do_ref, dq_ref, dk_ref, dv_ref, dstate):
        n, h = pl.program_id(0), pl.program_id(1)

        @pl.when(n == 0)
        def _():
            dstate[h] = jnp.zeros((RET_DK, RET_DV), F32)

        cos, sin, lg = cos_ref[...], sin_ref[...], lg_ref[...]
        _, vjp = jax.vjp(lambda q, k, v, st: _ret_chunk(q, k, v, cos, sin, lg, st),
                         q_ref[...], k_ref[...], v_ref[...], st_ref[...])
        dq, dk, dv, dst = vjp((do_ref[...], dstate[h]))
        dq_ref[...] = dq.astype(dq_ref.dtype)
        dk_ref[...] = dk.astype(dk_ref.dtype)
        dv_ref[...] = dv.astype(dv_ref.dtype)
        dstate[h] = dst

    return _pcall(body, name="ret_bwd", grid=(nc, H), in_specs=in_specs,
                  out_specs=[pl.BlockSpec((RET_CHUNK, RET_DK),lambda n, h: (n_(n), h)),
                             pl.BlockSpec((RET_CHUNK, RET_DK),lambda n, h: (n_(n), h)),
                             pl.BlockSpec((RET_CHUNK, RET_DV),lambda n, h: (n_(n), h))],
                  out_shape=[jax.ShapeDtypeStruct((S, H * RET_DK), BF16), jax.ShapeDtypeStruct((S, H * RET_DK), BF16),
                             jax.ShapeDtypeStruct((S, H * RET_DV), BF16)],
                  scratch_shapes=[pltpu.VMEM((H, RET_DK, RET_DV), F32)],
                  dims=("arbitrary", "arbitrary"))(zz, zz, zz, cos, sin, _ret_log_gamma(), states, do)


MLA_SCALE = (MLA_NOPE + MLA_ROPE) ** -0.5
NEG = -1e30


def _mla_scores(q, kn, kpe, diagonal):
    s = (lax.dot_general(q[:, :LANES], kn, _DN["nt"], preferred_element_type=F32)
         + lax.dot_general(q[:, LANES:], kpe, _DN["nt"], preferred_element_type=F32)) * MLA_SCALE
    if diagonal:
        row = lax.broadcasted_iota(jnp.int32, s.shape, 0)
        col = lax.broadcasted_iota(jnp.int32, s.shape, 1)
        s = jnp.where(col <= row, s, NEG)
    return s


def _on_and_below_diagonal(i, j, step):
    @pl.when(j < i)
    def _():
        step(False)

    @pl.when(j == i)
    def _():
        step(True)


FLASH_T = 1024


def flash_fwd(qr, kv, kpe, *, t=FLASH_T, hosted=None):
    S = qr.shape[0]
    t = min(t, S)
    nb = S // t
    H = MLA_HEADS

    def body(q_ref, kn_ref, v_ref, kpe_ref, o_ref, lse_ref, m_s, l_s, acc):
        i, j = pl.program_id(1), pl.program_id(2)

        @pl.when(j == 0)
        def _():
            m_s[...] = jnp.full_like(m_s, NEG)
            l_s[...] = jnp.zeros_like(l_s)
            acc[...] = jnp.zeros_like(acc)

        def step(diagonal):
            s = _mla_scores(q_ref[...], kn_ref[...], kpe_ref[...], diagonal)
            m_new = jnp.maximum(m_s[...], jnp.max(s, axis=-1, keepdims=True))
            p = jnp.exp(s - m_new)
            alpha = jnp.exp(m_s[...] - m_new)
            l_s[...] = alpha * l_s[...] + jnp.sum(p, axis=-1, keepdims=True)
            acc[...] = alpha * acc[...] + _bdot(p, v_ref[...])
            m_s[...] = m_new

        _on_and_below_diagonal(i, j, step)

        @pl.when(j == nb - 1)
        def _():
            o_ref[...] = (acc[...] / l_s[...]).astype(o_ref.dtype)
            lse_ref[...] = m_s[...] + jnp.log(l_s[...])

    kmap = lambda off: (lambda h, i, j: (jnp.minimum(j, i), off + h))
    call = _pcall(body, name="mla_flash_fwd", grid=(H, nb, nb),
                  in_specs=[pl.BlockSpec((t, 2 * LANES), lambda h, i, j: (i, h)),
                            pl.BlockSpec((t, LANES), kmap(0)), pl.BlockSpec((t, LANES), kmap(H)),
                            pl.BlockSpec((t, LANES), lambda h, i, j: (jnp.minimum(j, i), 0))],
                  out_specs=[pl.BlockSpec((t, LANES), lambda h, i, j: (i, h)),
                             pl.BlockSpec((None, t, 1), lambda h, i, j: (h, i, 0))],
                  out_shape=[jax.ShapeDtypeStruct((S, H * MLA_V), BF16), jax.ShapeDtypeStruct((H, S, 1), F32)],
                  scratch_shapes=[pltpu.VMEM((t, 1), F32), pltpu.VMEM((t, 1), F32), pltpu.VMEM((t, MLA_V), F32)],
                  dims=("parallel", "parallel", "arbitrary"), hosted=hosted)
    return _hosting(call, hosted, qr, kv, kv, kpe)


def _mla_p_ds(q, kn, v, kpe, do, o, lse, diagonal):
    p = jnp.exp(_mla_scores(q, kn, kpe, diagonal) - lse)
    dof = do.astype(F32)
    delta = jnp.sum(dof * o.astype(F32), axis=-1, keepdims=True)
    dp = lax.dot_general(do.astype(BF16), v, _DN["nt"], preferred_element_type=F32)
    ds = p * (dp - delta) * MLA_SCALE
    return p, ds


def flash_bwd_dq(qr, kv, kpe, o, lse, dy, dy_col0, *, t=FLASH_T, hosted=None):
    S = qr.shape[0]
    t = min(t, S)
    nb = S // t
    H = MLA_HEADS

    def body(q_ref, kn_ref, v_ref, kpe_ref, o_ref, lse_ref, do_ref, dq_ref, acc):
        i, j = pl.program_id(1), pl.program_id(2)

        @pl.when(j == 0)
        def _():
            acc[...] = jnp.zeros_like(acc)

        def step(diagonal):
            _, ds = _mla_p_ds(q_ref[...], kn_ref[...], v_ref[...], kpe_ref[...], do_ref[...], o_ref[...],
                              lse_ref[...], diagonal)
            acc[...] += jnp.concatenate([_bdot(ds, kn_ref[...]), _bdot(ds, kpe_ref[...])], axis=1)

        _on_and_below_diagonal(i, j, step)

        @pl.when(j == nb - 1)
        def _():
            dq_ref[...] = acc[...]

    kmap = lambda off: (lambda h, i, j: (jnp.minimum(j, i), off + h))
    call = _pcall(body, name="mla_flash_dq", grid=(H, nb, nb),
                  in_specs=[pl.BlockSpec((t, 2 * LANES), lambda h, i, j: (i, h)),
                            pl.BlockSpec((t, LANES), kmap(0)), pl.BlockSpec((t, LANES), kmap(H)),
                            pl.BlockSpec((t, LANES), lambda h, i, j: (jnp.minimum(j, i), 0)),
                            pl.BlockSpec((t, LANES), lambda h, i, j: (i, h)),
                            pl.BlockSpec((None, t, 1), lambda h, i, j: (h, i, 0)),
                            pl.BlockSpec((t, LANES), lambda h, i, j: (i, dy_col0 + h))],
                  out_specs=pl.BlockSpec((t, 2 * LANES), lambda h, i, j: (i, h)),
                  out_shape=jax.ShapeDtypeStruct((S, H * 2 * LANES), F32),
                  scratch_shapes=[pltpu.VMEM((t, 2 * LANES), F32)],
                  dims=("parallel", "parallel", "arbitrary"), hosted=hosted)
    return _hosting(call, hosted, qr, kv, kv, kpe, o, lse, dy)


def flash_bwd_dkv(qr, kv, kpe, o, lse, dy, dy_col0, *, t=FLASH_T, hosted=None):
    S = qr.shape[0]
    t = min(t, S)
    nb = S // t
    H = MLA_HEADS

    def body(q_ref, kn_ref, v_ref, kpe_ref, o_ref, lse_ref, do_ref, dkn_ref, dv_ref, dkpe_ref, akn, av):
        j, h, i = pl.program_id(0), pl.program_id(1), pl.program_id(2)

        @pl.when(i == 0)
        def _():
            akn[...] = jnp.zeros_like(akn)
            av[...] = jnp.zeros_like(av)

        @pl.when((i == 0) & (h == 0))
        def _():
            dkpe_ref[...] = jnp.zeros_like(dkpe_ref)

        def step(diagonal):
            q = q_ref[...]
            p, ds = _mla_p_ds(q, kn_ref[...], v_ref[...], kpe_ref[...], do_ref[...], o_ref[...], lse_ref[...],
                              diagonal)
            av[...] += _bdot(p, do_ref[...], _DN["tn"])
            akn[...] += _bdot(ds, q[:, :LANES], _DN["tn"])
            dkpe_ref[...] += _bdot(ds, q[:, LANES:], _DN["tn"])

        _on_and_below_diagonal(i, j, step)

        @pl.when(i == nb - 1)
        def _():
            dkn_ref[...] = akn[...].astype(dkn_ref.dtype)
            dv_ref[...] = av[...].astype(dv_ref.dtype)

    qmap = lambda off: (lambda j, h, i: (jnp.maximum(i, j), off + h))
    call = _pcall(
        body, name="mla_flash_dkv", grid=(nb, H, nb),
        in_specs=[pl.BlockSpec((t, 2 * LANES), qmap(0)),
                  pl.BlockSpec((t, LANES), lambda j, h, i: (j, h)), pl.BlockSpec((t, LANES), lambda j, h, i: (j, H + h)),
                  pl.BlockSpec((t, LANES), lambda j, h, i: (j, 0)),
                  pl.BlockSpec((t, LANES), qmap(0)),
                  pl.BlockSpec((None, t, 1), lambda j, h, i: (h, jnp.maximum(i, j), 0)),
                  pl.BlockSpec((t, LANES), qmap(dy_col0))],
        out_specs=[pl.BlockSpec((t, LANES), lambda j, h, i: (j, h)), pl.BlockSpec((t, LANES), lambda j, h, i: (j, h)),
                   pl.BlockSpec((t, LANES), lambda j, h, i: (j, 0))],
        out_shape=[jax.ShapeDtypeStruct((S, H * LANES), BF16), jax.ShapeDtypeStruct((S, H * LANES), BF16),
                   jax.ShapeDtypeStruct((S, LANES), F32)],
        scratch_shapes=[pltpu.VMEM((t, LANES), F32), pltpu.VMEM((t, LANES), F32)],
        dims=("arbitrary", "arbitrary", "arbitrary"), hosted=hosted)
    (dkn, dv, dkpe), extra = _hosting(call, hosted, qr, kv, kv, kpe, o, lse, dy)
    return (jnp.concatenate([dkn, dv], axis=1), dkpe), extra


def loss_head(h, target, g, *, tr=256):
    S, D = h.shape
    tr = min(tr, S)

    def body(h_ref, t_ref, g_ref, loss_ref, dh_ref, dg_ref):
        tgt = t_ref[...]

        def f(hh, gg):
            err = jnp.square(_rms(hh, gg) - tgt)
            per_row = jnp.sum(err, axis=-1, keepdims=True) * (0.5 / D)
            return jnp.sum(per_row, axis=0, keepdims=True)

        val, vjp = jax.vjp(f, h_ref[...], g_ref[...])
        dh, dg = vjp(jnp.ones((1, 1), F32))
        dh_ref[...] = dh

        @pl.when(pl.program_id(0) == 0)
        def _():
            loss_ref[...] = jnp.zeros_like(loss_ref)
            dg_ref[...] = jnp.zeros_like(dg_ref)

        loss_ref[...] += jnp.broadcast_to(val, loss_ref.shape)
        dg_ref[...] += dg

    return _pcall(body, name="loss_head", grid=(S // tr,),
                  in_specs=[_row_spec(tr, D, 0), _row_spec(tr, D, 0), _full_spec(g)],
                  out_specs=[pl.BlockSpec((1, LANES), lambda i: (0, 0)), _row_spec(tr, D, 0), _full_spec(g)],
                  out_shape=[jax.ShapeDtypeStruct((1, LANES), F32), jax.ShapeDtypeStruct((S, D), F32),
                             jax.ShapeDtypeStruct(g.shape, F32)],
                  dims=("arbitrary",))(h, target, g)


def _rope_tables(positions, dim):
    inv_freq = ROPE_THETA ** (-jnp.arange(0, dim, 2, dtype=F32) / dim)
    ang = positions.astype(F32)[:, None] * inv_freq
    return jnp.cos(ang), jnp.sin(ang)


def _pad_cols(w, n):
    return jnp.pad(w, ((0, 0), (0, n - w.shape[1])))


def _prep_w_in0(w):
    return jnp.concatenate([w[:, :4096], w[:, 4112:5136], _pad_cols(w[:, 5136:5200], LANES),
                            _pad_cols(w[:, 4096:4112], LANES)], axis=1)


def _unprep_w_in0(g):
    return jnp.concatenate([g[:, :4096], g[:, ZIN_AB:ZIN_AB + 16], g[:, ZIN_CQ:ZIN_KR], g[:, ZIN_KR:ZIN_KR + MLA_ROPE]],
                           axis=1)


def _prep_w_uq(w):
    w = w.reshape(MLA_Q_RANK, MLA_HEADS, MLA_NOPE + MLA_ROPE)
    w = jnp.pad(w, ((0, 0), (0, 0), (0, 2 * LANES - MLA_NOPE - MLA_ROPE)))
    return w.reshape(MLA_Q_RANK, MLA_HEADS * 2 * LANES)


def _unprep_w_uq(g):
    g = g.reshape(MLA_Q_RANK, MLA_HEADS, 2 * LANES)[:, :, :MLA_NOPE + MLA_ROPE]
    return g.reshape(MLA_Q_RANK, MLA_HEADS * (MLA_NOPE + MLA_ROPE))


def _prep_w_ukv(w):
    w = w.reshape(MLA_KV_RANK, MLA_HEADS, 2, LANES)
    return jnp.transpose(w, (0, 2, 1, 3)).reshape(MLA_KV_RANK, 2 * MLA_HEADS * LANES)


def _unprep_w_ukv(g):
    g = g.reshape(MLA_KV_RANK, 2, MLA_HEADS, LANES)
    return jnp.transpose(g, (0, 2, 1, 3)).reshape(MLA_KV_RANK, 2 * MLA_HEADS * LANES)


def _row(v, n=None):
    v = v.reshape(1, -1).astype(F32)
    return v if n is None else _pad_cols(v, n)


def _ffn_fwd(h, norm_g, w_up, conv_w, conv_b, w_down, tag, hosted=None):
    (hn,) = rowwise(_fn_rms, [(h, D_MODEL, 0)], [norm_g], [], [(D_MODEL, BF16)], name=f"{tag}_ffn_norm")
    u = matmul(hn, w_up, "nn", F32, b_shards=N_CHIPS, **TILES["wide_nn"], name=f"{tag}_ffn_up", hosted=hosted)
    u, got = u if hosted is not None else (u, [])
    f = ffn_conv_fwd(u, conv_w, conv_b)
    h_out = matmul(f, w_down, "nn", F32, add=h, tm=512, tn=1024, tk=8192, name=f"{tag}_ffn_down")
    return h_out, (hn, u, f), got


def _ffn_bwd(dh, dh16, h, norm_g, w_up, conv_w, conv_b, w_down, saved, tag, make_hosted=None):
    hn, u, f = saved
    df = matmul(dh16, w_down, "nt", BF16, tm=512, tn=2816, name=f"{tag}_ffn_down_dx")
    g_down = matmul(f, dh16, "tn", BF16, **TILES["dw"], name=f"{tag}_ffn_down_dw")
    dc, g_conv_w, g_conv_b = ffn_conv_bwd(u, conv_w, conv_b, df)
    du = conv_transpose(dc, conv_w, BF16, name=f"{tag}_ffn_conv_dx")
    g_up = matmul(hn, du, "tn", BF16, out_shards=N_CHIPS, tm=1024, tn=1408, tk=4096, name=f"{tag}_ffn_up_dw")
    hosted = make_hosted(dict(ffn_w_up=g_up, ffn_w_down=g_down)) if make_hosted else None
    dhn = matmul(du, w_up, "nt", F32, b_shards=N_CHIPS, tm=512, tn=2048, tk=2816, name=f"{tag}_ffn_up_dx",
                 hosted=hosted)
    dhn, got = dhn if hosted is not None else (dhn, [])
    ((dh_in, dh_in16),), (g_norm,) = rowwise_bwd(_fn_rms, [(h, D_MODEL, 0)], [norm_g], [], [(dhn, D_MODEL, 0)],
                                                 [(F32, BF16)], adds=[(dh, D_MODEL, 0)], name=f"{tag}_ffn_norm_bwd")
    return dh_in, dh_in16, dict(ffn_norm=g_norm, ffn_w_up=g_up, ffn_conv_w=g_conv_w, ffn_conv_b=g_conv_b,
                                ffn_w_down=g_down), got


TILES = {"wide_nn": dict(tm=512, tn=3072, tk=2048),
         "square": dict(tm=512, tn=2048, tk=2048),
         "dw": dict(tm=512, tn=2048, tk=4096)}


def _ple_fwd(h, p_i, w_proj, gate_g, w_gate, tag):
    (hg,) = rowwise(_fn_rms, [(h, D_MODEL, 0)], [gate_g], [], [(D_MODEL, BF16)], name=f"{tag}_ple_norm")
    gl = matmul(hg, w_gate, "nn", F32, **TILES["square"], name=f"{tag}_ple_gate")
    pp = matmul(p_i, w_proj, "nn", F32, b_shards=N_CHIPS, name=f"{tag}_ple_proj")
    (h_out,) = rowwise(_fn_ple, [(h, D_MODEL, 0), (pp, D_MODEL, 0), (gl, D_MODEL, 0)], [], [], [(D_MODEL, F32)],
                       name=f"{tag}_ple_add")
    return h_out, (hg, gl, pp)


def _ple_bwd(dh, h, p_i, w_proj, gate_g, w_gate, saved, tag):
    hg, gl, pp = saved
    (dpp, dgl), _ = rowwise_bwd(_fn_ple_terms, [(pp, D_MODEL, 0), (gl, D_MODEL, 0)], [], [], [(dh, D_MODEL, 0)],
                                [BF16, BF16], name=f"{tag}_ple_add_bwd")
    g_proj = matmul(p_i, dpp, "tn", BF16, out_shards=N_CHIPS, name=f"{tag}_ple_proj_dw")
    g_gate = matmul(hg, dgl, "tn", BF16, **TILES["dw"], name=f"{tag}_ple_gate_dw")
    dh_in, dh_in16, g_norm = matmul(dgl, w_gate, "nt", F32, tm=256, tn=2048, tk=2048, name=f"{tag}_ple_gate_dx",
                                    norm_bwd=(h, gate_g, dh))
    return dh_in, dh_in16, dict(ple_proj=g_proj, ple_gate_norm=g_norm, ple_gate=g_gate)


def local_step(x, p, positions, target, slots, W, ex=None):
    S = x.shape[0]
    G = {}
    p0, p1 = p[0].astype(BF16), p[1].astype(BF16)
    W = dict(W)

    def use(names, bufs):
        for k, b in zip(names, bufs):
            r, c = b.shape[1:]
            W[k] = b.reshape(N_CHIPS * r, c) if k in ROW_SHARDED else (b if k in KEPT_SHARDED else _cols_to_full(b))

    def by_chip(names):
        out = []
        for k in names:
            r, c = slots[k].shape[1:]
            out.append(G[k].reshape(N_CHIPS, r, c) if k in ROW_SHARDED
                       else (G[k] if k in KEPT_SHARDED else _full_to_cols(G[k])))
        return out

    first = [slots[k] for k in GATHER_FIRST]
    use(GATHER_FIRST, ex.gather(first) if ex else first)

    cm, sm = _rope_tables(positions, MLA_ROPE)
    zeros = jnp.zeros((S, LANES - MLA_ROPE), F32)
    cosp = jnp.concatenate([cm, cm, zeros], axis=1)
    sinp = jnp.concatenate([sm, sm, zeros], axis=1)
    cr, sr = _rope_tables(positions, RET_DK)

    w_in0 = _prep_w_in0(W["l0_w_in"])
    a_row = _row(W["l0_gdn_A_log"], LANES)
    dt_row = _row(W["l0_gdn_dt_bias"], LANES)
    gdn_nw = _row(W["l0_gdn_norm"])
    n = {k: _row(W[k]) for k in ("l0_attn_norm", "l0_mla_q_norm", "l0_mla_kv_norm", "l0_ffn_norm",
                                 "l0_ple_gate_norm", "l1_attn_norm", "l1_ret_norm", "l1_ffn_norm",
                                 "l1_ple_gate_norm", "final_norm", "l0_ffn_conv_b", "l1_ffn_conv_b")}

    (hn0,) = rowwise(_fn_rms, [(x, D_MODEL, 0)], [n["l0_attn_norm"]], [], [(D_MODEL, BF16)], name="l0_attn_norm")
    zin = matmul(hn0, w_in0, "nn", F32, tm=512, tn=1792, name="l0_w_in")
    qkv = gdn_conv_fwd(zin, W["l0_gdn_conv"])
    layer0 = [slots[k] for k in GATHER_L0]
    (y_a, gdn_states), got = gdn_fwd(qkv, zin, a_row, dt_row, gdn_nw, hosted=hosted_gather(layer0) if ex else None)
    use(GATHER_L0, got if ex else layer0)
    w_uq = _prep_w_uq(W["l0_mla_w_uq"])
    w_ukv = _prep_w_ukv(W["l0_mla_w_ukv"])
    mla_rows = [(zin, MLA_Q_RANK, ZIN_CQ // MLA_Q_RANK), (zin, MLA_KV_RANK, ZIN_CKV // MLA_KV_RANK),
                (zin, LANES, ZIN_KR // LANES)]
    mla_nd = [(cosp, LANES, 0), (sinp, LANES, 0)]
    cqn, ckvn, kpe = rowwise(_fn_mla_pre, mla_rows, [n["l0_mla_q_norm"], n["l0_mla_kv_norm"]], mla_nd,
                             [(MLA_Q_RANK, BF16), (MLA_KV_RANK, BF16), (LANES, BF16)], name="mla_pre")
    q_lin = matmul(cqn, w_uq, "nn", F32, name="mla_w_uq")
    kv = matmul(ckvn, w_ukv, "nn", BF16, name="mla_w_ukv")
    (qr,) = rowwise(_fn_rope_q, [(q_lin, 2048, 0)], [], mla_nd, [(2048, BF16)],
                    name="mla_rope_q")
    layer1 = [slots[k] for k in GATHER_L1]
    (y_b, lse), got = flash_fwd(qr, kv, kpe, hosted=hosted_gather(layer1) if ex else None)
    use(GATHER_L1, got if ex else layer1)
    y_ab = jnp.concatenate([y_a, y_b], axis=1)
    h1 = matmul(y_ab, W["l0_w_out"], "nn", F32, add=x, **TILES["square"], name="l0_w_out")
    ffn_late = [slots[k] for k in GATHER_FFN]
    h2, ffn0, got = _ffn_fwd(h1, n["l0_ffn_norm"], W["l0_ffn_w_up"], W["l0_ffn_conv_w"], n["l0_ffn_conv_b"],
                             W["l0_ffn_w_down"], "l0", hosted=hosted_gather(ffn_late) if ex else None)
    use(GATHER_FFN, got if ex else ffn_late)
    h3, ple0 = _ple_fwd(h2, p0, W["l0_ple_proj"], n["l0_ple_gate_norm"], W["l0_ple_gate"], "l0")

    (hn1,) = rowwise(_fn_rms, [(h3, D_MODEL, 0)], [n["l1_attn_norm"]], [], [(D_MODEL, BF16)], name="l1_attn_norm")
    late = [slots[k] for k in GATHER_L1_IN]
    zz = matmul(hn1, W["l1_w_in"], "nn", F32, b_shards=N_CHIPS, **TILES["wide_nn"], name="l1_w_in",
                hosted=hosted_gather(late) if ex else None)
    zz, got = zz if ex else (zz, late)
    use(GATHER_L1_IN, got)
    (o_ret, ret_states), _ = ret_fwd(zz, cr, sr)
    gate_rows = [(zz, 4096, 2), (o_ret, 4096, 0)]
    (yg,) = rowwise(_fn_ret_gate, gate_rows, [n["l1_ret_norm"]], [], [(4096, BF16)], name="ret_gate")
    h4 = matmul(yg, W["l1_w_out"], "nn", F32, add=h3, tm=512, tn=2048, tk=4096, name="l1_w_out")
    h5, ffn1, _ = _ffn_fwd(h4, n["l1_ffn_norm"], W["l1_ffn_w_up"], W["l1_ffn_conv_w"], n["l1_ffn_conv_b"],
                           W["l1_ffn_w_down"], "l1")
    h6, ple1 = _ple_fwd(h5, p1, W["l1_ple_proj"], n["l1_ple_gate_norm"], W["l1_ple_gate"], "l1")

    loss_vec, dh, G["final_norm"] = loss_head(h6, target, n["final_norm"])

    dh, dh16, g = _ple_bwd(dh, h5, p1, W["l1_ple_proj"], n["l1_ple_gate_norm"], W["l1_ple_gate"], ple1, "l1")
    G.update({"l1_" + k: v for k, v in g.items()})
    dh, dh16, g, _ = _ffn_bwd(dh, dh16, h4, n["l1_ffn_norm"], W["l1_ffn_w_up"], W["l1_ffn_conv_w"],
                              n["l1_ffn_conv_b"], W["l1_ffn_w_down"], ffn1, "l1")
    G.update({"l1_" + k: v for k, v in g.items()})

    dyg = matmul(dh16, W["l1_w_out"], "nt", F32, tm=512, tn=4096, name="l1_w_out_dx")
    G["l1_w_out"] = matmul(yg, dh16, "tn", BF16, **TILES["dw"], name="l1_w_out_dw")
    (dg, do_ret), (G["l1_ret_norm"],) = rowwise_bwd(_fn_ret_gate, gate_rows, [n["l1_ret_norm"]], [],
                                                   [(dyg, 4096, 0)], [BF16, F32], name="ret_gate_bwd")
    dq, dk, dv = ret_bwd(zz, cr, sr, ret_states, do_ret)
    dzz = jnp.concatenate([dq, dk, dv, dg], axis=1)
    G["l1_w_in"] = matmul(hn1, dzz, "tn", BF16, out_shards=N_CHIPS, tm=1024, tn=1536, tk=4096, name="l1_w_in_dw")
    sums, landed = {}, {}
    grads_l1 = by_chip(REDUCE_L1)
    dhn = matmul(dzz, W["l1_w_in"], "nt", F32, b_shards=N_CHIPS, tm=1024, tn=1024, tk=3072, name="l1_w_in_dx",
                 hosted=hosted_swap(grads_l1) if ex else None)
    if ex:
        dhn, swapped = dhn
        sums.update(zip(REDUCE_L1, ex.pair_sums(REDUCE_L1, grads_l1, swapped)))
    (dh,), (G["l1_attn_norm"],) = rowwise_bwd(_fn_rms, [(h3, D_MODEL, 0)], [n["l1_attn_norm"]], [],
                                             [(dhn, D_MODEL, 0)], [F32], adds=[(dh, D_MODEL, 0)],
                                             name="l1_attn_norm_bwd")

    dh, dh16, g = _ple_bwd(dh, h2, p0, W["l0_ple_proj"], n["l0_ple_gate_norm"], W["l0_ple_gate"], ple0, "l0")
    G.update({"l0_" + k: v for k, v in g.items()})
    mid = REDUCE_DQ + REDUCE_L0

    def swap_mid(g_ffn):
        G.update({"l0_" + k: v for k, v in g_ffn.items()})
        return hosted_swap(by_chip(mid))

    dh, dh16, g, swapped = _ffn_bwd(dh, dh16, h1, n["l0_ffn_norm"], W["l0_ffn_w_up"], W["l0_ffn_conv_w"],
                                    n["l0_ffn_conv_b"], W["l0_ffn_w_down"], ffn0, "l0",
                                    make_hosted=swap_mid if ex else None)
    G.update({"l0_" + k: v for k, v in g.items()})
    if ex:
        sums.update(zip(mid, ex.pair_sums(mid, by_chip(mid), swapped)))

    dy_ab = matmul(dh16, W["l0_w_out"], "nt", F32, **TILES["square"], name="l0_w_out_dx")
    G["l0_w_out"] = matmul(y_ab, dh16, "tn", BF16, **TILES["dw"], name="l0_w_out_dw")
    (dq, dk, dv, dz, dab, g_a, g_dt, G["l0_gdn_norm"]), got = gdn_bwd(
        qkv, zin, a_row, dt_row, gdn_nw, gdn_states, dy_ab, 0,
        hosted=hosted_scatter([sums[k] for k in REDUCE_L1]) if ex else None)
    landed.update(zip(REDUCE_L1, got))
    G["l0_gdn_A_log"], G["l0_gdn_dt_bias"] = g_a[:, :GDN_HEADS], g_dt[:, :GDN_HEADS]
    dpre, G["l0_gdn_conv"] = gdn_conv_bwd(zin, W["l0_gdn_conv"], jnp.concatenate([dq, dk, dv], axis=1))
    dqkv = conv_transpose(dpre, W["l0_gdn_conv"], BF16, name="gdn_conv_dx")
    dqr, got = flash_bwd_dq(qr, kv, kpe, y_b, lse, dy_ab, MLA_HEADS,
                            hosted=hosted_scatter([sums[k] for k in REDUCE_DQ]) if ex else None)
    landed.update(zip(REDUCE_DQ, got))
    (dkv, dkpe), got = flash_bwd_dkv(qr, kv, kpe, y_b, lse, dy_ab, MLA_HEADS,
                                     hosted=hosted_scatter([sums[k] for k in REDUCE_L0]) if ex else None)
    landed.update(zip(REDUCE_L0, got))
    (dq_lin,), _ = rowwise_bwd(_fn_rope_q, [(q_lin, 2048, 0)], [], mla_nd, [(dqr, 2048, 0)], [BF16],
                               name="mla_rope_q_bwd")
    G["l0_mla_w_uq"] = _unprep_w_uq(matmul(cqn, dq_lin, "tn", BF16, name="mla_w_uq_dw"))
    dcqn = matmul(dq_lin, w_uq, "nt", F32, name="mla_w_uq_dx")
    G["l0_mla_w_ukv"] = _unprep_w_ukv(matmul(ckvn, dkv, "tn", BF16, name="mla_w_ukv_dw"))
    dckvn = matmul(dkv, w_ukv, "nt", F32, name="mla_w_ukv_dx")
    (dcq, dckv, dkr), (G["l0_mla_q_norm"], G["l0_mla_kv_norm"]) = rowwise_bwd(
        _fn_mla_pre, mla_rows, [n["l0_mla_q_norm"], n["l0_mla_kv_norm"]], mla_nd,
        [(dcqn, MLA_Q_RANK, 0), (dckvn, MLA_KV_RANK, 0), (dkpe, LANES, 0)], [BF16, BF16, BF16], name="mla_pre_bwd")
    dzin = jnp.concatenate([dqkv, dz, dcq, dckv, dkr, dab.astype(BF16)], axis=1)
    early = REDUCE_L1 + REDUCE_DQ + REDUCE_L0
    reduced = {}
    g_in = matmul(hn0, dzin, "tn", BF16, tm=512, tn=1792, tk=4096, name="l0_w_in_dw",
                  hosted=hosted_join(ex.halves(early, sums, landed)) if ex else None)
    if ex:
        g_in, joined = g_in
        reduced.update(zip(early, joined))
    G["l0_w_in"] = _unprep_w_in0(g_in)
    if ex:
        sums.update(zip(REDUCE_LAST, ex.pair_sums(REDUCE_LAST, by_chip(REDUCE_LAST))))
    dhn = matmul(dzin, w_in0, "nt", F32, tm=512, tn=2048, tk=5376, name="l0_w_in_dx",
                 hosted=hosted_scatter([sums[k] for k in REDUCE_LAST]) if ex else None)
    if ex:
        dhn, got = dhn
        landed.update(zip(REDUCE_LAST, got))
    (grad_x,), (G["l0_attn_norm"],) = rowwise_bwd(_fn_rms, [(x, D_MODEL, 0)], [n["l0_attn_norm"]], [],
                                                 [(dhn, D_MODEL, 0)], [F32], adds=[(dh, D_MODEL, 0)],
                                                 name="l0_attn_norm_bwd")
    small = {k: G[k] for k in SMALL}
    if not ex:
        return loss_vec[0, 0], grad_x, dict(zip(BIG, by_chip(BIG))), small
    reduced.update(zip(REDUCE_LAST, pair_join_halves(ex.halves(REDUCE_LAST, sums, landed))))
    return loss_vec[0, 0], grad_x, reduced, small


HBM = pl.BlockSpec(memory_space=pltpu.HBM)
VMEM = pl.BlockSpec(memory_space=pltpu.VMEM)


def _place():
    return lax.axis_index("x"), lax.axis_index("y"), lax.axis_index("c")


def _other_chips(x, y):
    return [(1 - x, y), (x, 1 - y), (1 - x, 1 - y)]


def _comm_call(body, *, name, out_shape, in_specs, out_specs, scratch_shapes):
    return pl.pallas_call(body, name=name, out_shape=out_shape, in_specs=in_specs, out_specs=out_specs,
                          scratch_shapes=list(scratch_shapes),
                          compiler_params=pltpu.CompilerParams(vmem_limit_bytes=VMEM_LIMIT_MB << 20))


def _inplace_comm_call(body, bufs, *, name, n_sems):
    n = len(bufs)
    return pl.pallas_call(body, name=name, out_shape=[jax.ShapeDtypeStruct(b.shape, b.dtype) for b in bufs],
                          in_specs=[HBM] * n, out_specs=[HBM] * n, input_output_aliases={i: i for i in range(n)},
                          scratch_shapes=[pltpu.SemaphoreType.DMA((n_sems,)), pltpu.SemaphoreType.DMA((n_sems,))],
                          compiler_params=pltpu.CompilerParams(vmem_limit_bytes=VMEM_LIMIT_MB << 20))(*bufs)


def all_gather_chips(bufs):
    n_sems, start, finish = _gather_phase(len(bufs))
    n = len(bufs)

    def body(*refs):
        outs, send_sems, recv_sems = refs[n:2 * n], refs[2 * n], refs[2 * n + 1]
        start(None, outs, send_sems, recv_sems)
        finish(None, outs, send_sems, recv_sems)

    return _inplace_comm_call(body, bufs, name="all_gather_chips", n_sems=n_sems)


def _gather_phase(n):
    def plan(outs, send_sems, recv_sems):
        x, y, c = _place()

        def copy(w, k, chip, hc, to):
            half = outs[w].shape[1] // 2
            rows = outs[w].at[2 * chip[0] + chip[1], pl.ds(hc * half, half), :]
            return pltpu.make_async_remote_copy(src_ref=rows, dst_ref=rows, send_sem=send_sems.at[6 * w + k],
                                                recv_sem=recv_sems.at[6 * w + k], device_id=to, device_id_type=MESH)

        first = [[copy(w, k, (x, y), c, (*chip, c)) for k, chip in enumerate(_other_chips(x, y))] for w in range(n)]
        passed = [[copy(w, 3 + k, chip, c, (x, y, 1 - c)) for k, chip in enumerate(_other_chips(x, y))]
                  for w in range(n)]
        return copy, first, passed, (x, y, c)

    def start(_, outs, send_sems, recv_sems):
        _, first, _, _ = plan(outs, send_sems, recv_sems)
        for w in range(n):
            for cp in first[w]:
                cp.start()

    def finish(_, outs, send_sems, recv_sems):
        copy, first, passed, (x, y, c) = plan(outs, send_sems, recv_sems)
        chips = _other_chips(x, y)
        for w in range(n):
            for k, chip in enumerate(chips):
                copy(w, k, chip, c, (x, y, c)).wait_recv()
                passed[w][k].start()
        for w in range(n):
            for k, chip in enumerate(chips):
                copy(w, 3 + k, chip, 1 - c, (x, y, c)).wait_recv()
        for w in range(n):
            for cp in first[w] + passed[w]:
                cp.wait_send()

    return 6 * n, start, finish


def hosted_gather(bufs):
    n_sems, start, finish = _gather_phase(len(bufs))
    return Hosted(bufs, [jax.ShapeDtypeStruct(b.shape, b.dtype) for b in bufs], {i: i for i in range(len(bufs))},
                  n_sems, start, finish)


def pair_swap_halves(gs):
    n = len(gs)
    n_sems, start, finish = _swap_phase(n)

    def body(*refs):
        g_refs, o_refs, send_sems, recv_sems = refs[:n], refs[n:2 * n], refs[2 * n], refs[2 * n + 1]
        start(g_refs, o_refs, send_sems, recv_sems)
        finish(g_refs, o_refs, send_sems, recv_sems)

    return _comm_call(body, name="pair_swap_halves", out_shape=_swap_shapes(gs), in_specs=[HBM] * n, out_specs=[HBM] * n,
                      scratch_shapes=[pltpu.SemaphoreType.DMA((n_sems,)), pltpu.SemaphoreType.DMA((n_sems,))])(*gs)


def _swap_shapes(gs):
    return [jax.ShapeDtypeStruct((N_CHIPS, g.shape[1] // 2, g.shape[2]), g.dtype) for g in gs]


def _swap_phase(n):
    def copies(g_refs, o_refs, send_sems, recv_sems):
        x, y, c = _place()
        out = []
        for w in range(n):
            half = g_refs[w].shape[1] // 2
            out.append(pltpu.make_async_remote_copy(
                src_ref=g_refs[w].at[:, pl.ds((1 - c) * half, half), :], dst_ref=o_refs[w], send_sem=send_sems.at[w],
                recv_sem=recv_sems.at[w], device_id=(x, y, 1 - c), device_id_type=MESH))
        return out

    def start(*refs):
        for cp in copies(*refs):
            cp.start()

    def finish(*refs):
        for cp in copies(*refs):
            cp.wait()

    return n, start, finish


def hosted_swap(gs):
    n_sems, start, finish = _swap_phase(len(gs))
    return Hosted(gs, _swap_shapes(gs), {}, n_sems, start, finish)


def scatter_chips(ps):
    n = len(ps)
    n_sems, start, finish = _scatter_phase(n)

    def body(*refs):
        p_refs, o_refs, send_sems, recv_sems = refs[:n], refs[n:2 * n], refs[2 * n], refs[2 * n + 1]
        start(p_refs, o_refs, send_sems, recv_sems)
        finish(p_refs, o_refs, send_sems, recv_sems)

    return _comm_call(body, name="scatter_chips", out_shape=_scatter_shapes(ps), in_specs=[HBM] * n, out_specs=[HBM] * n,
                      scratch_shapes=[pltpu.SemaphoreType.DMA((n_sems,)), pltpu.SemaphoreType.DMA((n_sems,))])(*ps)


def _scatter_shapes(ps):
    return [jax.ShapeDtypeStruct((3,) + p.shape[1:], p.dtype) for p in ps]


def _scatter_phase(n):
    def copies(p_refs, o_refs, send_sems, recv_sems):
        x, y, c = _place()
        return [pltpu.make_async_remote_copy(src_ref=p_refs[w].at[2 * chip[0] + chip[1]], dst_ref=o_refs[w].at[k],
                                             send_sem=send_sems.at[3 * w + k], recv_sem=recv_sems.at[3 * w + k],
                                             device_id=(*chip, c), device_id_type=MESH)
                for w in range(n) for k, chip in enumerate(_other_chips(x, y))]

    def start(*refs):
        for cp in copies(*refs):
            cp.start()

    def finish(*refs):
        for cp in copies(*refs):
            cp.wait()

    return 3 * n, start, finish


def hosted_scatter(ps):
    n_sems, start, finish = _scatter_phase(len(ps))
    return Hosted(ps, _scatter_shapes(ps), {}, n_sems, start, finish)


def pair_join_halves(rs):
    n = len(rs)
    n_sems, start, finish = _join_phase(n)

    def body(*refs):
        outs, send_sems, recv_sems = refs[n:2 * n], refs[2 * n], refs[2 * n + 1]
        start(None, outs, send_sems, recv_sems)
        finish(None, outs, send_sems, recv_sems)

    return _inplace_comm_call(body, rs, name="pair_join_halves", n_sems=n_sems)


def _join_phase(n):
    def copies(_, outs, send_sems, recv_sems):
        x, y, c = _place()
        out = []
        for w in range(n):
            half = outs[w].shape[0] // 2
            rows = outs[w].at[pl.ds(c * half, half), :]
            out.append(pltpu.make_async_remote_copy(src_ref=rows, dst_ref=rows, send_sem=send_sems.at[w],
                                                    recv_sem=recv_sems.at[w], device_id=(x, y, 1 - c),
                                                    device_id_type=MESH))
        return out

    def start(*refs):
        for cp in copies(*refs):
            cp.start()

    def finish(*refs):
        for cp in copies(*refs):
            cp.wait()

    return n, start, finish


def hosted_join(rs):
    n_sems, start, finish = _join_phase(len(rs))
    return Hosted(rs, [jax.ShapeDtypeStruct(r.shape, r.dtype) for r in rs], {i: i for i in range(len(rs))},
                  n_sems, start, finish)


def all_reduce_small(v, name):
    n, L = v.shape
    n_dev = 8

    def body(v_ref, out_ref, buf, send_sems, recv_sems):
        x, y, c = _place()
        me = 4 * x + 2 * y + c
        buf[me] = v_ref[...]

        def copy(k, slot, peer):
            return pltpu.make_async_remote_copy(src_ref=v_ref, dst_ref=buf.at[slot], send_sem=send_sems.at[k],
                                                recv_sem=recv_sems.at[slot],
                                                device_id=(peer // 4, (peer // 2) % 2, peer % 2), device_id_type=MESH)

        sends = [copy(k - 1, me, (me + k) % n_dev) for k in range(1, n_dev)]
        for cp in sends:
            cp.start()
        for k in range(1, n_dev):
            src = (me + k) % n_dev
            copy(0, src, src).wait_recv()
        for cp in sends:
            cp.wait_send()
        acc = buf[0]
        for s in range(1, n_dev):
            acc = acc + buf[s]
        out_ref[...] = acc

    return _comm_call(body, name=name, out_shape=jax.ShapeDtypeStruct((n, L), v.dtype), in_specs=[VMEM], out_specs=VMEM,
                      scratch_shapes=[pltpu.VMEM((n_dev, n, L), v.dtype), pltpu.SemaphoreType.DMA((n_dev - 1,)),
                                      pltpu.SemaphoreType.DMA((n_dev,))])(v)


BF16_ROWS = 16
STREAM_BLOCK_BYTES = 4 << 20


def _rows_tile(n, row_bytes, budget=1 << 20, mult=SUBLANES):
    best = mult if n % mult == 0 else n
    for t in range(mult, n + 1, mult):
        if n % t == 0 and t * row_bytes <= budget:
            best = t
    return best


def _scalars(*vals):
    return jnp.stack([jnp.asarray(v, jnp.int32) for v in vals])


def cast_to_slot(w, chip, name):
    r, c = w.shape
    tb = _rows_tile(r, c * 4, budget=STREAM_BLOCK_BYTES, mult=BF16_ROWS)

    def body(s_ref, w_ref, o_ref):
        o_ref[...] = w_ref[...].astype(BF16)

    spec = pltpu.PrefetchScalarGridSpec(
        num_scalar_prefetch=1, grid=(r // tb,), in_specs=[pl.BlockSpec((tb, c), lambda i, s: (i, 0))],
        out_specs=pl.BlockSpec((None, tb, c), lambda i, s: (s[0], i, 0)))
    return pl.pallas_call(body, name=name, grid_spec=spec, out_shape=jax.ShapeDtypeStruct((N_CHIPS, r, c), BF16),
                          compiler_params=pltpu.CompilerParams(dimension_semantics=("parallel",)))(_scalars(chip), w)


def pair_add(g, got, c, name):
    _, r, w = g.shape
    half = r // 2
    tb = _rows_tile(half, w * 4, budget=STREAM_BLOCK_BYTES, mult=BF16_ROWS)
    nb = half // tb

    def body(c_ref, g_ref, got_ref, o_ref):
        o_ref[...] = (g_ref[...].astype(F32) + got_ref[...].astype(F32)).astype(o_ref.dtype)

    spec = pltpu.PrefetchScalarGridSpec(
        num_scalar_prefetch=1, grid=(N_CHIPS, nb),
        in_specs=[pl.BlockSpec((None, tb, w), lambda s, i, c_ref: (s, c_ref[0] * nb + i, 0)),
                  pl.BlockSpec((None, tb, w), lambda s, i, c_ref: (s, i, 0))],
        out_specs=pl.BlockSpec((None, tb, w), lambda s, i, c_ref: (s, i, 0)))
    return pl.pallas_call(body, name=name, grid_spec=spec, out_shape=jax.ShapeDtypeStruct((N_CHIPS, half, w), BF16),
                          compiler_params=pltpu.CompilerParams(dimension_semantics=("parallel", "parallel")))(
        _scalars(c), g, got)


def chip_add(p, got, chip, c, name):
    _, h, w = p.shape
    tb = _rows_tile(h, w * 4, budget=STREAM_BLOCK_BYTES, mult=BF16_ROWS)
    nb = h // tb

    def body(s_ref, p_ref, got_ref, o_ref):
        acc = p_ref[...].astype(F32)
        for k in range(3):
            acc = acc + got_ref[k].astype(F32)
        o_ref[...] = acc

    spec = pltpu.PrefetchScalarGridSpec(
        num_scalar_prefetch=1, grid=(nb,),
        in_specs=[pl.BlockSpec((None, tb, w), lambda i, s: (s[0], i, 0)),
                  pl.BlockSpec((3, tb, w), lambda i, s: (0, i, 0))],
        out_specs=pl.BlockSpec((tb, w), lambda i, s: (s[1] * nb + i, 0)))
    return pl.pallas_call(body, name=name, grid_spec=spec, out_shape=jax.ShapeDtypeStruct((2 * h, w), F32),
                          compiler_params=pltpu.CompilerParams(dimension_semantics=("parallel",)))(
        _scalars(chip, c), p, got)


def adamw(w, g, m, v, name):
    r, c = w.shape
    tr = _rows_tile(r, c * 4, budget=STREAM_BLOCK_BYTES // 2)

    def body(w_ref, g_ref, m_ref, v_ref, d_ref, m_out, v_out):
        gg = g_ref[...]
        m2 = ADAM_B1 * m_ref[...] + (1.0 - ADAM_B1) * gg
        v2 = ADAM_B2 * v_ref[...] + (1.0 - ADAM_B2) * jnp.square(gg)
        m_hat = m2 / (1.0 - ADAM_B1 ** ADAM_STEP)
        v_hat = v2 / (1.0 - ADAM_B2 ** ADAM_STEP)
        d_ref[...] = -ADAM_LR * (m_hat / (jnp.sqrt(v_hat) + ADAM_EPS) + ADAM_WD * w_ref[...])
        m_out[...] = m2
        v_out[...] = v2

    blk = pl.BlockSpec((tr, c), lambda i: (i, 0))
    return _pcall(body, name=name, grid=(r // tr,), in_specs=[blk] * 4, out_specs=[blk] * 3,
                  out_shape=[jax.ShapeDtypeStruct((r, c), F32)] * 3, dims=("parallel",))(w, g, m, v)


WEIGHTS = ["l0_attn_norm", "l0_w_in", "l0_gdn_conv", "l0_gdn_A_log", "l0_gdn_dt_bias", "l0_gdn_norm", "l0_mla_q_norm",
           "l0_mla_w_uq", "l0_mla_kv_norm", "l0_mla_w_ukv", "l0_w_out", "l0_ffn_norm", "l0_ffn_w_up", "l0_ffn_conv_w",
           "l0_ffn_conv_b", "l0_ffn_w_down", "l0_ple_proj", "l0_ple_gate_norm", "l0_ple_gate", "l1_attn_norm",
           "l1_w_in", "l1_ret_norm", "l1_w_out", "l1_ffn_norm", "l1_ffn_w_up", "l1_ffn_conv_w", "l1_ffn_conv_b",
           "l1_ffn_w_down", "l1_ple_proj", "l1_ple_gate_norm", "l1_ple_gate", "final_norm"]
COL_SHARDED = ["l0_w_in", "l0_mla_w_uq", "l0_mla_w_ukv", "l0_ffn_w_up", "l0_ple_proj", "l1_w_in", "l1_ffn_w_up",
               "l1_ple_proj"]
ROW_SHARDED = ["l0_w_out", "l0_ffn_w_down", "l0_ple_gate", "l1_w_out", "l1_ffn_w_down", "l1_ple_gate"]
BIG = [k for k in WEIGHTS if k in COL_SHARDED or k in ROW_SHARDED]
SMALL_SHARDED = ["l0_gdn_conv", "l0_ffn_conv_w", "l1_ffn_conv_w"]
SMALL = [k for k in WEIGHTS if k not in BIG]
KEPT_SHARDED = ["l0_ffn_w_up", "l0_ple_proj", "l1_w_in", "l1_ffn_w_up", "l1_ple_proj"]
GATHER_FIRST = ["l0_w_in"]
GATHER_L0 = ["l0_mla_w_uq", "l0_mla_w_ukv", "l0_w_out", "l0_ffn_w_up", "l0_ffn_w_down", "l0_ple_proj", "l0_ple_gate",
             "l1_w_out"]
GATHER_L1 = ["l1_w_in"]
GATHER_FFN = ["l1_ffn_w_down", "l1_ple_proj", "l1_ple_gate"]
GATHER_L1_IN = ["l1_ffn_w_up"]
REDUCE_L1 = [k for k in BIG if k.startswith("l1_")]
REDUCE_DQ = ["l0_ffn_w_up"]
REDUCE_L0 = ["l0_ffn_w_down", "l0_ple_proj", "l0_ple_gate"]
REDUCE_LAST = ["l0_w_in", "l0_mla_w_uq", "l0_mla_w_ukv", "l0_w_out"]


class Exchange:
    def __init__(self, chip, core):
        self.chip, self.core = chip, core

    def gather(self, bufs):
        return all_gather_chips(bufs)

    def pair_sums(self, names, grads, swapped=None):
        swapped = pair_swap_halves(grads) if swapped is None else swapped
        return [pair_add(g, got, self.core, "rs_pair_add_" + k) for k, g, got in zip(names, grads, swapped)]

    def halves(self, names, sums, landed):
        return [chip_add(sums[k], landed[k], self.chip, self.core, "rs_chip_add_" + k) for k in names]


def _cols_to_full(s):
    j, k, n = s.shape
    return jnp.transpose(s, (1, 0, 2)).reshape(k, j * n)


def _full_to_cols(g):
    k, n4 = g.shape
    return jnp.transpose(g.reshape(k, N_CHIPS, n4 // N_CHIPS), (1, 0, 2))


def _pack_small(vals):
    flat = jnp.concatenate([v.astype(F32).reshape(-1) for v in vals])
    align = SUBLANES * LANES
    flat = jnp.pad(flat, (0, -flat.shape[0] % align))
    return flat.reshape(-1, LANES)


def _unpack_small(rows, shapes):
    flat = rows.reshape(-1)
    out, off = [], 0
    for shp in shapes:
        n = int(np.prod(shp))
        out.append(flat[off:off + n].reshape(shp))
        off += n
    return out


INPUTS = (["x", "p", "positions"] + WEIGHTS + ["loss_target"] + ["m_" + k for k in WEIGHTS]
          + ["v_" + k for k in WEIGHTS])


def kernel(
        x, p, positions, l0_attn_norm, l0_w_in, l0_gdn_conv, l0_gdn_A_log, l0_gdn_dt_bias, l0_gdn_norm, l0_mla_q_norm,
        l0_mla_w_uq, l0_mla_kv_norm, l0_mla_w_ukv, l0_w_out, l0_ffn_norm, l0_ffn_w_up, l0_ffn_conv_w, l0_ffn_conv_b,
        l0_ffn_w_down, l0_ple_proj, l0_ple_gate_norm, l0_ple_gate, l1_attn_norm, l1_w_in, l1_ret_norm, l1_w_out,
        l1_ffn_norm, l1_ffn_w_up, l1_ffn_conv_w, l1_ffn_conv_b, l1_ffn_w_down, l1_ple_proj, l1_ple_gate_norm,
        l1_ple_gate, final_norm, loss_target, m_l0_attn_norm, m_l0_w_in, m_l0_gdn_conv, m_l0_gdn_A_log,
        m_l0_gdn_dt_bias, m_l0_gdn_norm, m_l0_mla_q_norm, m_l0_mla_w_uq, m_l0_mla_kv_norm, m_l0_mla_w_ukv, m_l0_w_out,
        m_l0_ffn_norm, m_l0_ffn_w_up, m_l0_ffn_conv_w, m_l0_ffn_conv_b, m_l0_ffn_w_down, m_l0_ple_proj,
        m_l0_ple_gate_norm, m_l0_ple_gate, m_l1_attn_norm, m_l1_w_in, m_l1_ret_norm, m_l1_w_out, m_l1_ffn_norm,
        m_l1_ffn_w_up, m_l1_ffn_conv_w, m_l1_ffn_conv_b, m_l1_ffn_w_down, m_l1_ple_proj, m_l1_ple_gate_norm,
        m_l1_ple_gate, m_final_norm, v_l0_attn_norm, v_l0_w_in, v_l0_gdn_conv, v_l0_gdn_A_log, v_l0_gdn_dt_bias,
        v_l0_gdn_norm, v_l0_mla_q_norm, v_l0_mla_w_uq, v_l0_mla_kv_norm, v_l0_mla_w_ukv, v_l0_w_out, v_l0_ffn_norm,
        v_l0_ffn_w_up, v_l0_ffn_conv_w, v_l0_ffn_conv_b, v_l0_ffn_w_down, v_l0_ple_proj, v_l0_ple_gate_norm,
        v_l0_ple_gate, v_l1_attn_norm, v_l1_w_in, v_l1_ret_norm, v_l1_w_out, v_l1_ffn_norm, v_l1_ffn_w_up,
        v_l1_ffn_conv_w, v_l1_ffn_conv_b, v_l1_ffn_w_down, v_l1_ple_proj, v_l1_ple_gate_norm, v_l1_ple_gate,
        v_final_norm):
    given = locals()
    a = {k: given[k] for k in INPUTS}
    x_i, y_i, c_i = _place()
    chip = 2 * x_i + y_i
    shard_shapes = {k: a[k].shape for k in WEIGHTS}

    slots = {k: cast_to_slot(a[k], chip, "cast_" + k) for k in BIG}
    W = {}
    placed = []
    for k in SMALL_SHARDED:
        r, c = shard_shapes[k]
        mine = jnp.where(c_i == 0, a[k], jnp.zeros_like(a[k]))
        placed.append(lax.dynamic_update_slice(jnp.zeros((r, N_CHIPS * c), F32), mine, (0, chip * c)))
    full_small = _unpack_small(all_reduce_small(_pack_small(placed), "gather_small_weights"),
                               [p_.shape for p_ in placed])
    for k in SMALL:
        W[k] = a[k]
    W.update(dict(zip(SMALL_SHARDED, full_small)))

    loss_part, grad_x, grads, G = local_step(a["x"][0], a["p"][:, 0], a["positions"][0], a["loss_target"][0], slots, W,
                                             Exchange(chip, c_i))
    loss = lax.psum(loss_part, ("x", "y", "c"))
    deltas, new_m, new_v = {}, {}, {}
    for k in BIG:
        deltas[k], new_m[k], new_v[k] = adamw(a[k], grads[k], a["m_" + k], a["v_" + k], "adamw_" + k)

    small_full = [G[k].reshape(-1) for k in SMALL]
    summed = _unpack_small(all_reduce_small(_pack_small(small_full), "reduce_small_grads"),
                           [G[k].shape for k in SMALL])
    for k, g in zip(SMALL, summed):
        if k in SMALL_SHARDED:
            r, c = shard_shapes[k]
            g = lax.dynamic_slice(g.reshape(r, N_CHIPS * c), (0, chip * c), (r, c))
        grads[k] = g.reshape(shard_shapes[k])
    packed = [_pack_small([d[k] for k in SMALL]) for d in (
        {k: a[k] for k in SMALL}, grads, {k: a["m_" + k] for k in SMALL}, {k: a["v_" + k] for k in SMALL})]
    outs = adamw(*packed, "adamw_small")
    shapes = [shard_shapes[k] for k in SMALL]
    for d, rows in zip((deltas, new_m, new_v), outs):
        d.update(dict(zip(SMALL, _unpack_small(rows, shapes))))

    return (loss, grad_x[None], *[grads[k] for k in WEIGHTS], *[deltas[k] for k in WEIGHTS],
            *[new_m[k] for k in WEIGHTS], *[new_v[k] for k in WEIGHTS])
```

```python
import functools
import math

import numpy as np
import jax
import jax.numpy as jnp
from jax import lax
from jax.experimental import pallas as pl
from jax.experimental.pallas import tpu as pltpu

F32, BF16 = jnp.float32, jnp.bfloat16
HI = lax.Precision.HIGHEST
MESH = pl.DeviceIdType.MESH

NORM_EPS = 1e-6
ROPE_THETA = 10000.0
D_MODEL = 2048
PLE_DIM = 256
GDN_HEADS, GDN_DK, GDN_DV, GDN_CONV = 8, 128, 128, 4
MLA_HEADS, MLA_Q_RANK, MLA_KV_RANK, MLA_NOPE, MLA_ROPE, MLA_V = 8, 512, 512, 128, 64, 128
RET_HEADS, RET_DK, RET_DV = 8, 256, 512
D_FF, FFN_CONV = 5632, 3
ADAM_LR, ADAM_B1, ADAM_B2, ADAM_EPS, ADAM_WD, ADAM_STEP = 0.001, 0.9, 0.999, 1e-08, 0.01, 10

LANES = 128
SUBLANES = 8
CHUNK = 128
N_CHIPS = 4
VMEM_LIMIT_MB = 56

ZIN_QKV, ZIN_Z, ZIN_CQ, ZIN_CKV, ZIN_KR, ZIN_AB, ZIN_W = 0, 3072, 4096, 4608, 5120, 5248, 5376


class Hosted:
    def __init__(self, inputs, out_shapes, aliases, n_sems, start, finish):
        self.inputs, self.out_shapes, self.aliases, self.n_sems = list(inputs), list(out_shapes), dict(aliases), n_sems
        self.start, self.finish = start, finish


def _pcall(body, *, name, out_shape, grid=(), in_specs=None, out_specs=None, scratch_shapes=(), dims=None,
           hosted=None):
    params = dict(vmem_limit_bytes=VMEM_LIMIT_MB << 20)
    if dims is not None:
        params["dimension_semantics"] = dims
    if hosted is None:
        return pl.pallas_call(body, name=name, out_shape=out_shape, grid=grid, in_specs=in_specs, out_specs=out_specs,
                              scratch_shapes=list(scratch_shapes), compiler_params=pltpu.CompilerParams(**params))
    single = not isinstance(out_shape, (list, tuple))
    out_shape = [out_shape] if single else list(out_shape)
    out_specs = [out_specs] if single else list(out_specs)
    n_in, n_out, n_scr = len(in_specs), len(out_shape), len(scratch_shapes)
    h_in, h_out = len(hosted.inputs), len(hosted.out_shapes)
    hbm = pl.BlockSpec(memory_space=pltpu.HBM)

    def hosting_body(*refs):
        ins, h_ins = refs[:n_in], refs[n_in:n_in + h_in]
        o0 = n_in + h_in
        outs, h_outs = refs[o0:o0 + n_out], refs[o0 + n_out:o0 + n_out + h_out]
        s0 = o0 + n_out + h_out
        scr, (send_sems, recv_sems) = refs[s0:s0 + n_scr], refs[s0 + n_scr:]
        ids = [pl.program_id(d) for d in range(len(grid))]
        first = functools.reduce(lambda u, v: u & v, [i == 0 for i in ids])
        last = functools.reduce(lambda u, v: u & v, [i == g - 1 for i, g in zip(ids, grid)])

        @pl.when(first)
        def _():
            hosted.start(h_ins, h_outs, send_sems, recv_sems)

        body(*ins, *outs, *scr)

        @pl.when(last)
        def _():
            hosted.finish(h_ins, h_outs, send_sems, recv_sems)

    params["dimension_semantics"] = ("arbitrary",) * len(grid)
    call = pl.pallas_call(
        hosting_body, name=name, out_shape=out_shape + hosted.out_shapes, grid=grid,
        in_specs=list(in_specs) + [hbm] * h_in, out_specs=out_specs + [hbm] * h_out,
        scratch_shapes=list(scratch_shapes) + [pltpu.SemaphoreType.DMA((hosted.n_sems,)),
                                               pltpu.SemaphoreType.DMA((hosted.n_sems,))],
        input_output_aliases={n_in + i: n_out + o for i, o in hosted.aliases.items()},
        compiler_params=pltpu.CompilerParams(**params))

    def run(*args):
        res = call(*args, *hosted.inputs)
        main = res[:n_out]
        return (main[0] if single else main), list(res[n_out:])

    return run


def _tile(n, target, mult=LANES):
    best = None
    for t in range(mult, min(n, target) + 1, mult):
        if n % t == 0:
            best = t
    return best or n


_DN = {"nn": (((1,), (0,)), ((), ())), "nt": (((1,), (1,)), ((), ())), "tn": (((0,), (0,)), ((), ()))}


def matmul(a, b, mode, out_dtype, *, name, add=None, b_shards=1, out_shards=1, tm=512, tn=1024, tk=2048,
           hosted=None, norm_bwd=None):
    bs = b.shape[-2:]
    if mode == "nn":
        (M, K), (K2, N) = a.shape, (bs[0], bs[1] * b_shards)
    elif mode == "nt":
        (M, K), (N, K2) = a.shape, (bs[0], bs[1] * b_shards)
    else:
        (K, M), (K2, N) = a.shape, bs
    assert K == K2, (name, a.shape, b.shape)
    n_sh = N // max(b_shards if mode == "nn" else 1, out_shards)
    k_sh = K // (b_shards if mode == "nt" else 1)
    tm, tn, tk = _tile(M, tm), _tile(n_sh, tn), _tile(k_sh, tk)
    nk = K // tk
    nbn, nbk = n_sh // tn, k_sh // tk
    dn = _DN[mode]
    has_add = add is not None
    a_bytes, b_bytes = a.size * a.dtype.itemsize, b.size * b.dtype.itemsize
    i_outer = nk > 1 or a_bytes + (M // tm) * b_bytes <= b_bytes + (N // tn) * a_bytes

    def ij(g0, g1):
        return (g0, g1) if i_outer else (g1, g0)

    fused_norm = norm_bwd is not None
    if fused_norm:
        assert tn == N and out_shards == 1 and not has_add and hosted is None, name
        i_outer = True

    def body(*refs):
        a_ref, b_ref = refs[:2]
        add_ref = refs[2] if has_add else None
        o_ref = refs[3 if has_add else 2]
        part = lax.dot_general(a_ref[...].astype(BF16), b_ref[...].astype(BF16), dn, preferred_element_type=F32)

        def finish(r):
            if fused_norm:
                h_ref, g_ref, dh_ref, o32_ref, o16_ref, dg_ref = refs[2:8]
                _, vjp = jax.vjp(_rms, h_ref[...], g_ref[...])
                dx, dg = vjp(r)
                out = dx + dh_ref[...]
                o32_ref[...] = out
                o16_ref[...] = out.astype(BF16)
                dg_ref[...] += dg
                return
            if has_add:
                r = r + add_ref[...]
            o_ref[...] = r.astype(out_dtype)

        if fused_norm:
            @pl.when((pl.program_id(0) == 0) & (pl.program_id(2) == 0))
            def _():
                refs[7][...] = jnp.zeros_like(refs[7])

        if nk == 1:
            finish(part)
            return
        acc = refs[-1]
        k = pl.program_id(2)

        @pl.when(k == 0)
        def _():
            acc[...] = part

        @pl.when(k > 0)
        def _():
            acc[...] += part

        @pl.when(k == nk - 1)
        def _():
            finish(acc[...])

    def spec(block, fn):
        return pl.BlockSpec(block, lambda g0, g1, k: fn(*ij(g0, g1), k))

    if mode == "tn":
        a_spec = spec((tk, tm), lambda i, j, k: (k, i))
    else:
        a_spec = spec((tm, tk), lambda i, j, k: (i, k))
    if mode == "nt":
        if b_shards > 1:
            b_spec = spec((None, tn, tk), lambda i, j, k: (k // nbk, j, k % nbk))
        else:
            b_spec = spec((tn, tk), lambda i, j, k: (j, k))
    elif b_shards > 1:
        b_spec = spec((None, tk, tn), lambda i, j, k: (j // nbn, k, j % nbn))
    else:
        b_spec = spec((tk, tn), lambda i, j, k: (k, j))
    in_specs = [a_spec, b_spec]
    args = [a, b]
    if has_add:
        in_specs.append(spec((tm, tn), lambda i, j, k: (i, j)))
        args.append(add)
    if out_shards > 1:
        out_spec = spec((None, tm, tn), lambda i, j, k: (j // nbn, i, j % nbn))
        out_shape = jax.ShapeDtypeStruct((out_shards, M, n_sh), out_dtype)
    else:
        out_spec = spec((tm, tn), lambda i, j, k: (i, j))
        out_shape = jax.ShapeDtypeStruct((M, N), out_dtype)
    gi, gj = M // tm, N // tn
    if fused_norm:
        h, gain, dh = norm_bwd
        row_blk = spec((tm, N), lambda i, j, k: (i, 0))
        gain_blk = spec((1, N), lambda i, j, k: (0, 0))
        return _pcall(body, name=name, grid=(gi, 1, nk), in_specs=in_specs + [row_blk, gain_blk, row_blk],
                      out_specs=[row_blk, row_blk, gain_blk],
                      out_shape=[jax.ShapeDtypeStruct((M, N), F32), jax.ShapeDtypeStruct((M, N), BF16),
                                 jax.ShapeDtypeStruct((1, N), F32)],
                      scratch_shapes=[pltpu.VMEM((tm, tn), F32)] if nk > 1 else [],
                      dims=("arbitrary", "arbitrary", "arbitrary"))(a, b, h, gain, dh)
    return _pcall(body, name=name, out_shape=out_shape, grid=(gi, gj, nk) if i_outer else (gj, gi, nk),
                  in_specs=in_specs, out_specs=out_spec,
                  scratch_shapes=[pltpu.VMEM((tm, tn), F32)] if nk > 1 else [],
                  dims=("parallel", "parallel", "arbitrary"), hosted=hosted)(*args)


def _row_spec(tr, w, c):
    return pl.BlockSpec((tr, w), lambda i: (i, c))


def _full_spec(arr):
    return pl.BlockSpec(arr.shape, lambda i: (0,) * arr.ndim)


def rowwise(fn, rows, params, nd_rows, outs, *, name, tr=256):
    S = rows[0][0].shape[0]
    tr = min(tr, S)
    n_in = len(rows) + len(params) + len(nd_rows)

    def body(*refs):
        res = fn(*[x[...] for x in refs[:n_in]])
        for o_ref, v in zip(refs[n_in:], res):
            o_ref[...] = v.astype(o_ref.dtype)

    return _pcall(body, name=name, grid=(S // tr,),
                  in_specs=([_row_spec(tr, w, c) for (_, w, c) in rows] + [_full_spec(q) for q in params]
                            + [_row_spec(tr, w, c) for (_, w, c) in nd_rows]),
                  out_specs=[_row_spec(tr, w, 0) for (w, _) in outs],
                  out_shape=[jax.ShapeDtypeStruct((S, w), dt) for (w, dt) in outs],
                  dims=("parallel",))(*[r[0] for r in rows], *params, *[r[0] for r in nd_rows])


def rowwise_bwd(fn, rows, params, nd_rows, cts, d_dtypes, *, name, adds=None, tr=256):
    S = rows[0][0].shape[0]
    tr = min(tr, S)
    n_r, n_p, n_n, n_c = len(rows), len(params), len(nd_rows), len(cts)
    adds = adds or [None] * n_r
    add_list = [a for a in adds if a is not None]
    n_a = len(add_list)
    d_dtypes = [dt if isinstance(dt, (list, tuple)) else (dt,) for dt in d_dtypes]
    n_d = sum(len(dt) for dt in d_dtypes)

    def body(*refs):
        it = iter(refs)
        r = [next(it)[...] for _ in range(n_r)]
        p = [next(it)[...] for _ in range(n_p)]
        nd = [next(it)[...] for _ in range(n_n)]
        c = [next(it)[...] for _ in range(n_c)]
        ad = [next(it)[...] for _ in range(n_a)]
        d_row_refs = [[next(it) for _ in dts] for dts in d_dtypes]
        d_par_refs = [next(it) for _ in range(n_p)]
        outs, vjp = jax.vjp(lambda *dp: fn(*dp, *nd), *r, *p)
        g = vjp(tuple(ci.astype(o.dtype) for ci, o in zip(c, outs)))
        ai = 0
        for k in range(n_r):
            gk = g[k].astype(F32)
            if adds[k] is not None:
                gk = gk + ad[ai].astype(F32)
                ai += 1
            for ref in d_row_refs[k]:
                ref[...] = gk.astype(ref.dtype)

        @pl.when(pl.program_id(0) == 0)
        def _():
            for ref in d_par_refs:
                ref[...] = jnp.zeros_like(ref)

        for k in range(n_p):
            d_par_refs[k][...] += g[n_r + k].astype(F32)

    in_specs = ([_row_spec(tr, w, c) for (_, w, c) in rows] + [_full_spec(q) for q in params]
                + [_row_spec(tr, w, c) for (_, w, c) in nd_rows] + [_row_spec(tr, w, c) for (_, w, c) in cts]
                + [_row_spec(tr, w, c) for (_, w, c) in add_list])
    out_specs = ([_row_spec(tr, w, 0) for (_, w, _), dts in zip(rows, d_dtypes) for _ in dts]
                 + [_full_spec(q) for q in params])
    out_shape = ([jax.ShapeDtypeStruct((S, w), dt) for (_, w, _), dts in zip(rows, d_dtypes) for dt in dts]
                 + [jax.ShapeDtypeStruct(q.shape, F32) for q in params])
    res = _pcall(body, name=name, grid=(S // tr,), in_specs=in_specs, out_specs=out_specs, out_shape=out_shape,
                 dims=("arbitrary",))(*[r[0] for r in rows], *params, *[r[0] for r in nd_rows],
                                      *[r[0] for r in cts], *[r[0] for r in add_list])
    d_rows, i = [], 0
    for dts in d_dtypes:
        d_rows.append(res[i] if len(dts) == 1 else tuple(res[i:i + len(dts)]))
        i += len(dts)
    return d_rows, res[n_d:]


def _rms(x, g):
    x = x.astype(F32)
    return x * lax.rsqrt(jnp.mean(x * x, axis=-1, keepdims=True) + NORM_EPS) * g


def _fn_rms(x, g):
    return (_rms(x, g),)


def _sigmoid(x):
    return 1.0 / (1.0 + jnp.exp(-x))


def _silu(x):
    return x * _sigmoid(x)


def _softplus(x):
    return jnp.maximum(x, 0.0) + jnp.log(1.0 + jnp.exp(-jnp.abs(x)))


def _fn_ple(h, pp, gl):
    return (h.astype(F32) + pp.astype(F32) * _sigmoid(gl.astype(F32)),)


def _fn_ple_terms(pp, gl):
    return (pp.astype(F32) * _sigmoid(gl.astype(F32)),)


def _rot_half_matrix():
    half = MLA_ROPE // 2
    r = lax.broadcasted_iota(jnp.int32, (LANES, LANES), 0)
    c = lax.broadcasted_iota(jnp.int32, (LANES, LANES), 1)
    plus = (c == r + half) & (r < half)
    minus = (r == c + half) & (c < half)
    return jnp.where(plus, 1.0, 0.0) - jnp.where(minus, 1.0, 0.0)


def _rope_pad(x, cosp, sinp):
    return x * cosp + jnp.dot(x, _rot_half_matrix(), precision=HI, preferred_element_type=F32) * sinp


def _fn_mla_pre(cq, ckv, kr, qn_w, kvn_w, cosp, sinp):
    return (_rms(cq, qn_w), _rms(ckv, kvn_w), _rope_pad(kr.astype(F32), cosp, sinp))


def _fn_rope_q(q, cosp, sinp):
    q = q.astype(F32)
    parts = []
    for h in range(MLA_HEADS):
        base = 2 * LANES * h
        parts.append(q[:, base:base + LANES])
        parts.append(_rope_pad(q[:, base + LANES:base + 2 * LANES], cosp, sinp))
    return (jnp.concatenate(parts, axis=1),)


def _fn_ret_gate(g, on, w):
    return (_silu(g.astype(F32)) * (on.astype(F32) * w),)


def _shift_down(cur, halo, s):
    if s == 0:
        return cur
    r = pltpu.roll(cur, s, 0)
    hs = pltpu.roll(halo, s, 0)
    row = lax.broadcasted_iota(jnp.int32, hs.shape, 0)
    first = jnp.where(row < s, hs, r[:SUBLANES])
    return jnp.concatenate([first, r[SUBLANES:]], axis=0)


def _shift_up(cur, halo, s):
    if s == 0:
        return cur
    n = cur.shape[0]
    r = pltpu.roll(cur, n - s, 0)
    hs = pltpu.roll(halo, SUBLANES - s, 0)
    row = lax.broadcasted_iota(jnp.int32, hs.shape, 0)
    last = jnp.where(row >= SUBLANES - s, hs, r[n - SUBLANES:])
    return jnp.concatenate([r[:n - SUBLANES], last], axis=0)


def _prev_halo_spec(tr, tw, col):
    return pl.BlockSpec((SUBLANES, tw), lambda c, i: (jnp.maximum(i * (tr // SUBLANES) - 1, 0), col(c)))


def _conv_taps(cur, halo, w_ref, width):
    taps = [_shift_down(cur, halo, width - 1 - j) for j in range(width)]
    y = taps[0] * w_ref[0:1, :]
    for j in range(1, width):
        y = y + taps[j] * w_ref[j:j + 1, :]
    return y, taps


def gdn_conv_fwd(zin, w, *, tr=512, tw=512):
    S = zin.shape[0]
    tr = min(tr, S)
    width, C = w.shape

    def body(cur_ref, halo_ref, w_ref, o_ref):
        i = pl.program_id(1)
        halo = halo_ref[...] * (i > 0).astype(F32)
        y, _ = _conv_taps(cur_ref[...], halo, w_ref, width)
        o_ref[...] = _silu(y)

    return _pcall(body, name="gdn_conv_fwd", grid=(C // tw, S // tr),
                  in_specs=[pl.BlockSpec((tr, tw), lambda c, i: (i, c)), _prev_halo_spec(tr, tw, lambda c: c),
                            pl.BlockSpec((width, tw), lambda c, i: (0, c))],
                  out_specs=pl.BlockSpec((tr, tw), lambda c, i: (i, c)),
                  out_shape=jax.ShapeDtypeStruct((S, C), F32), dims=("parallel", "arbitrary"))(zin, zin, w)


def gdn_conv_bwd(zin, w, dy, *, tr=512, tw=512):
    S = zin.shape[0]
    tr = min(tr, S)
    width, C = w.shape

    def body(cur_ref, halo_ref, w_ref, dy_ref, da_ref, dw_ref):
        i = pl.program_id(1)
        halo = halo_ref[...] * (i > 0).astype(F32)
        y, taps = _conv_taps(cur_ref[...], halo, w_ref, width)
        sg = _sigmoid(y)
        da = dy_ref[...] * (sg * (1.0 + y * (1.0 - sg)))
        da_ref[...] = da

        @pl.when(i == 0)
        def _():
            dw_ref[...] = jnp.zeros_like(dw_ref)

        for j in range(width):
            dw_ref[j:j + 1, :] += jnp.sum(da * taps[j], axis=0, keepdims=True)

    return _pcall(body, name="gdn_conv_bwd", grid=(C // tw, S // tr),
                  in_specs=[pl.BlockSpec((tr, tw), lambda c, i: (i, c)), _prev_halo_spec(tr, tw, lambda c: c),
                            pl.BlockSpec((width, tw), lambda c, i: (0, c)),
                            pl.BlockSpec((tr, tw), lambda c, i: (i, c))],
                  out_specs=[pl.BlockSpec((tr, tw), lambda c, i: (i, c)),
                             pl.BlockSpec((width, tw), lambda c, i: (0, c))],
                  out_shape=[jax.ShapeDtypeStruct((S, C), F32), jax.ShapeDtypeStruct((width, C), F32)],
                  dims=("parallel", "arbitrary"))(zin, zin, w, dy)


def conv_transpose(dy, w, out_dtype, *, name, tr=512, tw=512):
    if dy.ndim == 2:
        dy = dy[None]
    T, S, C = dy.shape
    tr = min(tr, S)
    width = w.shape[0]
    n_i, nc = S // tr, C // tw

    def body(cur_ref, halo_ref, w_ref, o_ref):
        i = pl.program_id(2)
        halo = halo_ref[...] * (i < n_i - 1).astype(F32)
        cur = cur_ref[...]
        acc = cur * w_ref[width - 1:width, :]
        for s in range(1, width):
            acc = acc + _shift_up(cur, halo, s) * w_ref[width - 1 - s:width - s, :]
        o_ref[...] = acc.astype(out_dtype)

    nxt = pl.BlockSpec((None, SUBLANES, tw),
                       lambda t, c, i: (t, jnp.minimum((i + 1) * (tr // SUBLANES), S // SUBLANES - 1), c))
    return _pcall(body, name=name, grid=(T, nc, n_i),
                  in_specs=[pl.BlockSpec((None, tr, tw), lambda t, c, i: (t, i, c)), nxt,
                            pl.BlockSpec((width, tw), lambda t, c, i: (0, t * nc + c))],
                  out_specs=pl.BlockSpec((tr, tw), lambda t, c, i: (i, t * nc + c)),
                  out_shape=jax.ShapeDtypeStruct((S, T * C), out_dtype),
                  dims=("parallel", "parallel", "arbitrary"))(dy, dy, w)


def ffn_conv_fwd(u, w, b, *, tr=1024, tw=256):
    S, C2 = u.shape
    tr = min(tr, S)
    width = w.shape[0]
    half = C2 // 2
    nc = half // tw

    def body(g_ref, gh_ref, u_ref, uh_ref, wg_ref, wu_ref, bg_ref, bu_ref, o_ref):
        i = pl.program_id(1)
        live = (i > 0).astype(F32)
        yg, _ = _conv_taps(g_ref[...], gh_ref[...] * live, wg_ref, width)
        yu, _ = _conv_taps(u_ref[...], uh_ref[...] * live, wu_ref, width)
        o_ref[...] = (_silu(yg + bg_ref[...]) * (yu + bu_ref[...])).astype(o_ref.dtype)

    return _pcall(body, name="ffn_conv_fwd", grid=(nc, S // tr),
                  in_specs=[pl.BlockSpec((tr, tw), lambda c, i: (i, c)), _prev_halo_spec(tr, tw, lambda c: c),
                            pl.BlockSpec((tr, tw), lambda c, i: (i, c + nc)),
                            _prev_halo_spec(tr, tw, lambda c: c + nc),
                            pl.BlockSpec((width, tw), lambda c, i: (0, c)),
                            pl.BlockSpec((width, tw), lambda c, i: (0, c + nc)),
                            pl.BlockSpec((1, tw), lambda c, i: (0, c)), pl.BlockSpec((1, tw), lambda c, i: (0, c + nc))],
                  out_specs=pl.BlockSpec((tr, tw), lambda c, i: (i, c)),
                  out_shape=jax.ShapeDtypeStruct((S, half), BF16),
                  dims=("parallel", "arbitrary"))(u, u, u, u, w, w, b, b)


def ffn_conv_bwd(u, w, b, df, *, tr=512, tw=512):
    S, C2 = u.shape
    tr = min(tr, S)
    width = w.shape[0]
    half = C2 // 2
    nc = half // tw

    def body(g_ref, gh_ref, u_ref, uh_ref, wg_ref, wu_ref, bg_ref, bu_ref, df_ref, dc_ref, dw_ref, db_ref):
        i = pl.program_id(1)
        live = (i > 0).astype(F32)
        yg, gt = _conv_taps(g_ref[...], gh_ref[...] * live, wg_ref, width)
        yu, ut = _conv_taps(u_ref[...], uh_ref[...] * live, wu_ref, width)
        yg = yg + bg_ref[...]
        yu = yu + bu_ref[...]
        sg = _sigmoid(yg)
        dfv = df_ref[...].astype(F32)
        dcs = (dfv * yu * (sg * (1.0 + yg * (1.0 - sg))), dfv * (yg * sg))

        @pl.when(i == 0)
        def _():
            dw_ref[...] = jnp.zeros_like(dw_ref)
            db_ref[...] = jnp.zeros_like(db_ref)

        for t, (dc, taps) in enumerate(zip(dcs, (gt, ut))):
            dc_ref[t] = dc
            db_ref[t] += jnp.sum(dc, axis=0, keepdims=True)
            for j in range(width):
                dw_ref[t, j:j + 1, :] += jnp.sum(dc * taps[j], axis=0, keepdims=True)

    dc, dw, db = _pcall(
        body, name="ffn_conv_bwd", grid=(nc, S // tr),
        in_specs=[pl.BlockSpec((tr, tw), lambda c, i: (i, c)), _prev_halo_spec(tr, tw, lambda c: c),
                  pl.BlockSpec((tr, tw), lambda c, i: (i, c + nc)), _prev_halo_spec(tr, tw, lambda c: c + nc),
                  pl.BlockSpec((width, tw), lambda c, i: (0, c)), pl.BlockSpec((width, tw), lambda c, i: (0, c + nc)),
                  pl.BlockSpec((1, tw), lambda c, i: (0, c)), pl.BlockSpec((1, tw), lambda c, i: (0, c + nc)),
                  pl.BlockSpec((tr, tw), lambda c, i: (i, c))],
        out_specs=[pl.BlockSpec((2, tr, tw), lambda c, i: (0, i, c)), pl.BlockSpec((2, width, tw), lambda c, i: (0, 0, c)),
                   pl.BlockSpec((2, 1, tw), lambda c, i: (0, 0, c))],
        out_shape=[jax.ShapeDtypeStruct((2, S, half), F32), jax.ShapeDtypeStruct((2, width, half), F32),
                   jax.ShapeDtypeStruct((2, 1, half), F32)],
        dims=("parallel", "arbitrary"))(u, u, u, u, w, w, b, b, df)
    return dc, jnp.concatenate([dw[0], dw[1]], axis=1), jnp.concatenate([db[0], db[1]], axis=1)


_MODE_OF = {v: k for k, v in _DN.items()}


def _bf16_dot(a, b, mode):
    return lax.dot_general(a.astype(BF16), b.astype(BF16), _DN[mode], preferred_element_type=F32)


@functools.partial(jax.custom_vjp, nondiff_argnums=(2,))
def _bdot_mode(a, b, mode):
    return _bf16_dot(a, b, mode)


def _bdot_fwd(a, b, mode):
    return _bf16_dot(a, b, mode), (a, b)


def _bdot_bwd(mode, res, ct):
    a, b = res
    if mode == "nn":
        da, db = _bf16_dot(ct, b, "nt"), _bf16_dot(a, ct, "tn")
    elif mode == "nt":
        da, db = _bf16_dot(ct, b, "nn"), _bf16_dot(ct, a, "tn")
    else:
        da, db = _bf16_dot(b, ct, "nt"), _bf16_dot(a, ct, "nn")
    return da.astype(a.dtype), db.astype(b.dtype)


_bdot_mode.defvjp(_bdot_fwd, _bdot_bwd)


def _bdot(a, b, dn=_DN["nn"]):
    return _bdot_mode(a, b, _MODE_OF[dn])


def _hi_lo(x):
    hi = x.astype(BF16)
    return hi, (x - hi.astype(F32)).astype(BF16)


def _dot3_raw(a, b, mode):
    a1, a2 = _hi_lo(a)
    b1, b2 = _hi_lo(b)
    dot = lambda p, q: lax.dot_general(p, q, _DN[mode], preferred_element_type=F32)
    return dot(a1, b1) + (dot(a1, b2) + dot(a2, b1))


@functools.partial(jax.custom_vjp, nondiff_argnums=(2,))
def _dot3(a, b, mode="nn"):
    return _dot3_raw(a, b, mode)


def _dot3_fwd(a, b, mode):
    return _dot3_raw(a, b, mode), (a, b)


def _dot3_bwd(mode, res, ct):
    a, b = res
    if mode == "nn":
        return _dot3_raw(ct, b, "nt"), _dot3_raw(a, ct, "tn")
    if mode == "nt":
        return _dot3_raw(ct, b, "nn"), _dot3_raw(ct, a, "tn")
    return _dot3_raw(b, ct, "nt"), _dot3_raw(a, ct, "nn")


_dot3.defvjp(_dot3_fwd, _dot3_bwd)


@functools.partial(jax.custom_vjp, nondiff_argnums=(2,))
def _gdot(a, b, mode="nn"):
    return _bf16_dot(a, b, mode)


def _gdot_fwd(a, b, mode):
    return _bf16_dot(a, b, mode), (a.astype(BF16), b.astype(BF16))


def _ct_dot(p, q, mode, ct_first):
    ct, r = (p, q) if ct_first else (q, p)
    c1, c2 = _hi_lo(ct)
    dot = lambda c: lax.dot_general(*((c, r) if ct_first else (r, c)), _DN[mode], preferred_element_type=F32)
    return dot(c1) + dot(c2)


def _gdot_bwd(mode, res, ct):
    a, b = res
    if mode == "nn":
        return _ct_dot(ct, b, "nt", True), _ct_dot(a, ct, "tn", False)
    if mode == "nt":
        return _ct_dot(ct, b, "nn", True), _ct_dot(ct, a, "tn", True)
    return _ct_dot(b, ct, "nt", False), _ct_dot(a, ct, "nn", False)


_gdot.defvjp(_gdot_fwd, _gdot_bwd)


def _split_dot(ones, x):
    x1 = x.astype(BF16)
    r1 = x - x1.astype(F32)
    x2 = r1.astype(BF16)
    x3 = (r1 - x2.astype(F32)).astype(BF16)
    m = ones.astype(BF16)
    dot = lambda p: lax.dot_general(m, p, _DN["nn"], preferred_element_type=F32)
    return dot(x1) + dot(x2) + dot(x3)


@jax.custom_vjp
def _tri_cumsum(x, lower, upper):
    return _split_dot(lower, x)


def _tri_cumsum_fwd(x, lower, upper):
    return _split_dot(lower, x), (lower, upper)


def _tri_cumsum_bwd(res, ct):
    lower, upper = res
    return _split_dot(upper, ct), jnp.zeros_like(lower), jnp.zeros_like(upper)


_tri_cumsum.defvjp(_tri_cumsum_fwd, _tri_cumsum_bwd)


def _tri_masks(n):
    r = lax.broadcasted_iota(jnp.int32, (n, n), 0)
    c = lax.broadcasted_iota(jnp.int32, (n, n), 1)
    return r >= c, r > c


def _gdn_chunk(q, k, v, z, ab, a_row, dt_row, norm_w, state, sel_a, sel_b):
    C = q.shape[0]
    incl, strict = _tri_masks(C)
    lower = jnp.where(incl, 1.0, 0.0)
    qn = q * lax.rsqrt(jnp.sum(q * q, axis=-1, keepdims=True) + NORM_EPS) * (GDN_DK ** -0.5)
    kn = k * lax.rsqrt(jnp.sum(k * k, axis=-1, keepdims=True) + NORM_EPS)
    g = jnp.sum(-jnp.exp(a_row) * _softplus(ab + dt_row) * sel_a, axis=-1, keepdims=True)
    beta = jnp.sum(_sigmoid(ab) * sel_b, axis=-1, keepdims=True)
    gb = jnp.broadcast_to(g, (C, C))
    g_col = _tri_cumsum(gb, lower, jnp.where(strict, 0.0, 1.0))
    g_row = g_col.T
    g_last = jnp.sum(gb, axis=0, keepdims=True)
    gamma = jnp.where(incl, jnp.exp(jnp.where(incl, g_col - g_row, 0.0)), 0.0)
    e_col = jnp.exp(g_col)
    kb = kn * beta
    a_mat = jnp.where(strict, _gdot(kb, kn, "nt") * gamma, 0.0)
    x = jnp.concatenate([v * beta, kb * e_col], axis=1)
    pw = -a_mat
    steps = int(math.log2(C))
    for it in range(steps):
        x = x + _dot3(pw, x, "nn")
        if it < steps - 1:
            pw = _dot3(pw, pw, "nn")
    u, w = x[:, :GDN_DV], x[:, GDN_DV:]
    attn = _gdot(qn, kn, "nt") * gamma
    q_dec = qn * e_col
    k_dec = kn * jnp.exp(g_last - g_col)
    v_new = u - _gdot(w, state, "nn")
    o = _gdot(q_dec, state, "nn") + _gdot(attn, v_new, "nn")
    state_new = state * jnp.exp(jnp.broadcast_to(g_last, state.shape)) + _gdot(k_dec, v_new, "tn")
    y = _rms(o, norm_w) * _silu(z)
    return y, state_new


def _head_selectors(h):
    lane = lax.broadcasted_iota(jnp.int32, (1, LANES), 1)
    return jnp.where(lane == h, 1.0, 0.0), jnp.where(lane == h + GDN_HEADS, 1.0, 0.0)


GDN_HPS = 4
GDN_W = GDN_HPS * LANES


def _gdn_in_specs(rev, nc):
    def n_(n):
        return nc - 1 - n if rev else n
    G = GDN_HEADS // GDN_HPS
    blk = lambda off: pl.BlockSpec((CHUNK, GDN_W), lambda n, h: (n_(n), off + h))
    row = pl.BlockSpec((1, LANES), lambda n, h: (0, 0))
    return n_, [blk(0), blk(G), blk(2 * G), blk(ZIN_Z // GDN_W),
                pl.BlockSpec((CHUNK, LANES), lambda n, h: (n_(n), ZIN_AB // LANES)), row, row, row]


def _lanes(ref, j):
    return ref[:, j * LANES:(j + 1) * LANES]


def _hosting(call, hosted, *args):
    res = call(*args)
    return res if hosted is not None else (res, [])


def gdn_fwd(qkv, zin, a_row, dt_row, norm_w, hosted=None):
    S = qkv.shape[0]
    nc = S // CHUNK
    H = GDN_HEADS
    _, in_specs = _gdn_in_specs(False, nc)

    def body(q_ref, k_ref, v_ref, z_ref, ab_ref, a_ref, dt_ref, nw_ref, y_ref, st_ref, state):
        n, g = pl.program_id(0), pl.program_id(1)
        @pl.when((n == 0) & (g == 0))
        def _():
            state[...] = jnp.zeros_like(state)

        res = []
        for j in range(GDN_HPS):
            h = g * GDN_HPS + j
            st = state[h]
            sel_a, sel_b = _head_selectors(h)
            res.append((st,) + _gdn_chunk(_lanes(q_ref, j), _lanes(k_ref, j), _lanes(v_ref, j), _lanes(z_ref, j),
                                          ab_ref[...], a_ref[...], dt_ref[...], nw_ref[...], st, sel_a, sel_b))
        for j, (st, y, st_new) in enumerate(res):
            st_ref[j] = st
            y_ref[:, j * LANES:(j + 1) * LANES] = y.astype(y_ref.dtype)
            state[g * GDN_HPS + j] = st_new

    call = _pcall(body, name="gdn_fwd", grid=(nc, H // GDN_HPS), in_specs=in_specs,
                  out_specs=[pl.BlockSpec((CHUNK, GDN_W), lambda n, h: (n, h)),
                             pl.BlockSpec((GDN_HPS, None, GDN_DK, GDN_DV), lambda n, h: (h, n, 0, 0))],
                  out_shape=[jax.ShapeDtypeStruct((S, H * GDN_DV), BF16),
                             jax.ShapeDtypeStruct((H, nc, GDN_DK, GDN_DV), F32)],
                  scratch_shapes=[pltpu.VMEM((H, GDN_DK, GDN_DV), F32)],
                  dims=("arbitrary", "arbitrary"), hosted=hosted)
    return _hosting(call, hosted, qkv, qkv, qkv, zin, zin, a_row, dt_row, norm_w)


def gdn_bwd(qkv, zin, a_row, dt_row, norm_w, states, dy, dy_col0, hosted=None):
    S = qkv.shape[0]
    nc = S // CHUNK
    H = GDN_HEADS
    n_, in_specs = _gdn_in_specs(True, nc)
    assert dy_col0 % GDN_HPS == 0
    in_specs = in_specs + [pl.BlockSpec((GDN_HPS, None, GDN_DK, GDN_DV), lambda n, h: (h, n_(n), 0, 0)),
                           pl.BlockSpec((CHUNK, GDN_W), lambda n, h: (n_(n), dy_col0 // GDN_HPS + h))]

    def body(q_ref, k_ref, v_ref, z_ref, ab_ref, a_ref, dt_ref, nw_ref, st_ref, dy_ref,
             dq_ref, dk_ref, dv_ref, dz_ref, dab_ref, da_ref, ddt_ref, dnw_ref, dstate):
        n, g = pl.program_id(0), pl.program_id(1)

        @pl.when((n == 0) & (g == 0))
        def _():
            da_ref[...] = jnp.zeros_like(da_ref)
            ddt_ref[...] = jnp.zeros_like(ddt_ref)
            dnw_ref[...] = jnp.zeros_like(dnw_ref)
            dstate[...] = jnp.zeros_like(dstate)

        @pl.when(g == 0)
        def _():
            dab_ref[...] = jnp.zeros_like(dab_ref)

        res = []
        for j in range(GDN_HPS):
            h = g * GDN_HPS + j
            sel_a, sel_b = _head_selectors(h)
            _, vjp = jax.vjp(lambda *a, sa=sel_a, sb=sel_b: _gdn_chunk(*a, sa, sb), _lanes(q_ref, j), _lanes(k_ref, j),
                             _lanes(v_ref, j), _lanes(z_ref, j), ab_ref[...], a_ref[...], dt_ref[...], nw_ref[...],
                             st_ref[j])
            res.append(vjp((_lanes(dy_ref, j).astype(F32), dstate[h])))
        for j, (dq, dk, dv, dz, dab, da, ddt, dnw, dst) in enumerate(res):
            cols = slice(j * LANES, (j + 1) * LANES)
            dq_ref[:, cols] = dq
            dk_ref[:, cols] = dk
            dv_ref[:, cols] = dv
            dz_ref[:, cols] = dz.astype(dz_ref.dtype)
            dstate[g * GDN_HPS + j] = dst
        dab_ref[...] += sum(r[4] for r in res)
        da_ref[...] += sum(r[5] for r in res)
        ddt_ref[...] += sum(r[6] for r in res)
        dnw_ref[...] += sum(r[7] for r in res)

    blk = pl.BlockSpec((CHUNK, GDN_W), lambda n, h: (n_(n), h))
    row = pl.BlockSpec((1, LANES), lambda n, h: (0, 0))
    wide = jax.ShapeDtypeStruct((S, H * LANES), F32)
    call = _pcall(body, name="gdn_bwd", grid=(nc, H // GDN_HPS), in_specs=in_specs,
                  out_specs=[blk, blk, blk, blk, pl.BlockSpec((CHUNK, LANES), lambda n, h: (n_(n), 0)), row, row, row],
                  out_shape=[wide, wide, wide, jax.ShapeDtypeStruct((S, H * LANES), BF16),
                             jax.ShapeDtypeStruct((S, LANES), F32)] + [jax.ShapeDtypeStruct((1, LANES), F32)] * 3,
                  scratch_shapes=[pltpu.VMEM((H, GDN_DK, GDN_DV), F32)],
                  dims=("arbitrary", "arbitrary"), hosted=hosted)
    return _hosting(call, hosted, qkv, qkv, qkv, zin, zin, a_row, dt_row, norm_w, states, dy)


def _rope_full(x, cos, sin):
    x1, x2 = x[:, :RET_DK // 2], x[:, RET_DK // 2:]
    return jnp.concatenate([x1 * cos - x2 * sin, x2 * cos + x1 * sin], axis=1)


RET_CHUNK = 512


def _ret_chunk(q, k, v, cos, sin, lg, state):
    C = q.shape[0]
    incl, _ = _tri_masks(C)
    qr = _rope_full(q, cos, sin)
    kr = _rope_full(k, cos, sin) * (RET_DK ** -0.5)
    r = lax.broadcasted_iota(jnp.int32, (C, C), 0)
    c = lax.broadcasted_iota(jnp.int32, (C, C), 1)
    dist = jnp.where(incl, (r - c).astype(F32), 0.0)
    lg1 = lg[:, :1]
    decay = jnp.where(incl, jnp.exp(dist * lg1), 0.0)
    pos = lax.broadcasted_iota(jnp.int32, (C, 1), 0).astype(F32)
    xi = jnp.exp((pos + 1.0) * lg1)
    zeta = jnp.exp((C - 1.0 - pos) * lg1)
    inner = _bdot(_bdot(qr, kr, _DN["nt"]) * decay, v)
    cross = _bdot(qr * xi, state)
    state_new = state * jnp.exp(C * lg1) + _bdot(kr * zeta, v, _DN["tn"])
    o = inner + cross
    mu = jnp.mean(o, axis=-1, keepdims=True)
    var = jnp.mean(jnp.square(o - mu), axis=-1, keepdims=True)
    return (o - mu) * lax.rsqrt(var + NORM_EPS), state_new


def _ret_log_gamma():
    lg = np.log1p(-np.power(2.0, -5.0 - np.arange(RET_HEADS, dtype=np.float64))).astype(np.float32)
    return jnp.asarray(np.broadcast_to(lg[:, None, None], (RET_HEADS, 1, LANES)).copy())


def _ret_in_specs(rev, nc):
    def n_(n):
        return nc - 1 - n if rev else n
    H = RET_HEADS
    return n_, [pl.BlockSpec((RET_CHUNK, RET_DK),lambda n, h: (n_(n), h)),
                pl.BlockSpec((RET_CHUNK, RET_DK),lambda n, h: (n_(n), H + h)),
                pl.BlockSpec((RET_CHUNK, RET_DV),lambda n, h: (n_(n), 2 * H * RET_DK // RET_DV + h)),
                pl.BlockSpec((RET_CHUNK, LANES), lambda n, h: (n_(n), 0)),
                pl.BlockSpec((RET_CHUNK, LANES), lambda n, h: (n_(n), 0)),
                pl.BlockSpec((None, 1, LANES), lambda n, h: (h, 0, 0))]


def ret_fwd(zz, cos, sin, hosted=None):
    S = zz.shape[0]
    nc = S // RET_CHUNK
    H = RET_HEADS
    _, in_specs = _ret_in_specs(False, nc)

    def body(q_ref, k_ref, v_ref, cos_ref, sin_ref, lg_ref, o_ref, st_ref, state):
        n, h = pl.program_id(0), pl.program_id(1)

        @pl.when(n == 0)
        def _():
            state[h] = jnp.zeros((RET_DK, RET_DV), F32)

        st = state[h]
        st_ref[...] = st
        o, st_new = _ret_chunk(q_ref[...], k_ref[...], v_ref[...], cos_ref[...], sin_ref[...], lg_ref[...], st)
        o_ref[...] = o
        state[h] = st_new

    call = _pcall(body, name="ret_fwd", grid=(nc, H), in_specs=in_specs,
                  out_specs=[pl.BlockSpec((RET_CHUNK, RET_DV),lambda n, h: (n, h)),
                             pl.BlockSpec((None, None, RET_DK, RET_DV), lambda n, h: (h, n, 0, 0))],
                  out_shape=[jax.ShapeDtypeStruct((S, H * RET_DV), F32),
                             jax.ShapeDtypeStruct((H, nc, RET_DK, RET_DV), F32)],
                  scratch_shapes=[pltpu.VMEM((H, RET_DK, RET_DV), F32)],
                  dims=("arbitrary", "arbitrary"), hosted=hosted)
    return _hosting(call, hosted, zz, zz, zz, cos, sin, _ret_log_gamma())


def ret_bwd(zz, cos, sin, states, do):
    S = zz.shape[0]
    nc = S // RET_CHUNK
    H = RET_HEADS
    n_, in_specs = _ret_in_specs(True, nc)
    in_specs = in_specs + [pl.BlockSpec((None, None, RET_DK, RET_DV), lambda n, h: (h, n_(n), 0, 0)),
                           pl.BlockSpec((RET_CHUNK, RET_DV),lambda n, h: (n_(n), h))]

    def body(q_ref, k_ref, v_ref, cos_ref, sin_ref, lg_ref, st_ref, do_ref, dq_ref, dk_ref, dv_ref, dstate):
        n, h = pl.program_id(0), pl.program_id(1)

        @pl.when(n == 0)
        def _():
            dstate[h] = jnp.zeros((RET_DK, RET_DV), F32)

        cos, sin, lg = cos_ref[...], sin_ref[...], lg_ref[...]
        _, vjp = jax.vjp(lambda q, k, v, st: _ret_chunk(q, k, v, cos, sin, lg, st),
                         q_ref[...], k_ref[...], v_ref[...], st_ref[...])
        dq, dk, dv, dst = vjp((do_ref[...], dstate[h]))
        dq_ref[...] = dq.astype(dq_ref.dtype)
        dk_ref[...] = dk.astype(dk_ref.dtype)
        dv_ref[...] = dv.astype(dv_ref.dtype)
        dstate[h] = dst

    return _pcall(body, name="ret_bwd", grid=(nc, H), in_specs=in_specs,
                  out_specs=[pl.BlockSpec((RET_CHUNK, RET_DK),lambda n, h: (n_(n), h)),
                             pl.BlockSpec((RET_CHUNK, RET_DK),lambda n, h: (n_(n), h)),
                             pl.BlockSpec((RET_CHUNK, RET_DV),lambda n, h: (n_(n), h))],
                  out_shape=[jax.ShapeDtypeStruct((S, H * RET_DK), BF16), jax.ShapeDtypeStruct((S, H * RET_DK), BF16),
                             jax.ShapeDtypeStruct((S, H * RET_DV), BF16)],
                  scratch_shapes=[pltpu.VMEM((H, RET_DK, RET_DV), F32)],
                  dims=("arbitrary", "arbitrary"))(zz, zz, zz, cos, sin, _ret_log_gamma(), states, do)


MLA_SCALE = (MLA_NOPE + MLA_ROPE) ** -0.5
NEG = -1e30


def _mla_scores(q, kn, kpe, diagonal):
    s = (lax.dot_general(q[:, :LANES], kn, _DN["nt"], preferred_element_type=F32)
         + lax.dot_general(q[:, LANES:], kpe, _DN["nt"], preferred_element_type=F32)) * MLA_SCALE
    if diagonal:
        row = lax.broadcasted_iota(jnp.int32, s.shape, 0)
        col = lax.broadcasted_iota(jnp.int32, s.shape, 1)
        s = jnp.where(col <= row, s, NEG)
    return s


def _on_and_below_diagonal(i, j, step):
    @pl.when(j < i)
    def _():
        step(False)

    @pl.when(j == i)
    def _():
        step(True)


FLASH_T = 1024


def flash_fwd(qr, kv, kpe, *, t=FLASH_T, hosted=None):
    S = qr.shape[0]
    t = min(t, S)
    nb = S // t
    H = MLA_HEADS

    def body(q_ref, kn_ref, v_ref, kpe_ref, o_ref, lse_ref, m_s, l_s, acc):
        i, j = pl.program_id(1), pl.program_id(2)

        @pl.when(j == 0)
        def _():
            m_s[...] = jnp.full_like(m_s, NEG)
            l_s[...] = jnp.zeros_like(l_s)
            acc[...] = jnp.zeros_like(acc)

        def step(diagonal):
            s = _mla_scores(q_ref[...], kn_ref[...], kpe_ref[...], diagonal)
            m_new = jnp.maximum(m_s[...], jnp.max(s, axis=-1, keepdims=True))
            p = jnp.exp(s - m_new)
            alpha = jnp.exp(m_s[...] - m_new)
            l_s[...] = alpha * l_s[...] + jnp.sum(p, axis=-1, keepdims=True)
            acc[...] = alpha * acc[...] + _bdot(p, v_ref[...])
            m_s[...] = m_new

        _on_and_below_diagonal(i, j, step)

        @pl.when(j == nb - 1)
        def _():
            o_ref[...] = (acc[...] / l_s[...]).astype(o_ref.dtype)
            lse_ref[...] = m_s[...] + jnp.log(l_s[...])

    kmap = lambda off: (lambda h, i, j: (jnp.minimum(j, i), off + h))
    call = _pcall(body, name="mla_flash_fwd", grid=(H, nb, nb),
                  in_specs=[pl.BlockSpec((t, 2 * LANES), lambda h, i, j: (i, h)),
                            pl.BlockSpec((t, LANES), kmap(0)), pl.BlockSpec((t, LANES), kmap(H)),
                            pl.BlockSpec((t, LANES), lambda h, i, j: (jnp.minimum(j, i), 0))],
                  out_specs=[pl.BlockSpec((t, LANES), lambda h, i, j: (i, h)),
                             pl.BlockSpec((None, t, 1), lambda h, i, j: (h, i, 0))],
                  out_shape=[jax.ShapeDtypeStruct((S, H * MLA_V), BF16), jax.ShapeDtypeStruct((H, S, 1), F32)],
                  scratch_shapes=[pltpu.VMEM((t, 1), F32), pltpu.VMEM((t, 1), F32), pltpu.VMEM((t, MLA_V), F32)],
                  dims=("parallel", "parallel", "arbitrary"), hosted=hosted)
    return _hosting(call, hosted, qr, kv, kv, kpe)


def _mla_p_ds(q, kn, v, kpe, do, o, lse, diagonal):
    p = jnp.exp(_mla_scores(q, kn, kpe, diagonal) - lse)
    dof = do.astype(F32)
    delta = jnp.sum(dof * o.astype(F32), axis=-1, keepdims=True)
    dp = lax.dot_general(do.astype(BF16), v, _DN["nt"], preferred_element_type=F32)
    ds = p * (dp - delta) * MLA_SCALE
    return p, ds


def flash_bwd(qr, kv, kpe, o, lse, dy, dy_col0, *, t=FLASH_T, hosted=None):
    S = qr.shape[0]
    t = min(t, S)
    nb = S // t
    H = MLA_HEADS

    def body(q_ref, kn_ref, v_ref, kpe_ref, o_ref, lse_ref, do_ref, dq_ref, dkn_ref, dv_ref, dkpe_ref, acc, akn, av):
        h, i, j = pl.program_id(0), pl.program_id(1), pl.program_id(2)

        @pl.when((h == 0) & (i == 0) & (j == 0))
        def _():
            dkpe_ref[...] = jnp.zeros_like(dkpe_ref)

        @pl.when((i == 0) & (j == 0))
        def _():
            akn[...] = jnp.zeros_like(akn)
            av[...] = jnp.zeros_like(av)

        @pl.when(j == 0)
        def _():
            acc[...] = jnp.zeros_like(acc)

        def step(diagonal):
            q = q_ref[...]
            p, ds = _mla_p_ds(q, kn_ref[...], v_ref[...], kpe_ref[...], do_ref[...], o_ref[...], lse_ref[...],
                              diagonal)
            acc[...] += jnp.concatenate([_bdot(ds, kn_ref[...]), _bdot(ds, kpe_ref[...])], axis=1)
            av[j] += _bdot(p, do_ref[...], _DN["tn"])
            akn[j] += _bdot(ds, q[:, :LANES], _DN["tn"])
            rows = pl.ds(pl.multiple_of(j * t, t), t)
            dkpe_ref[rows, :] += _bdot(ds, q[:, LANES:], _DN["tn"])

        _on_and_below_diagonal(i, j, step)

        @pl.when(j == nb - 1)
        def _():
            dq_ref[...] = acc[...]

        @pl.when((i == nb - 1) & (j == nb - 1))
        def _():
            dkn_ref[...] = akn[...].reshape(S, LANES).astype(dkn_ref.dtype)
            dv_ref[...] = av[...].reshape(S, LANES).astype(dv_ref.dtype)

    kmap = lambda off: (lambda h, i, j: (jnp.minimum(j, i), off + h))
    head_col = pl.BlockSpec((S, LANES), lambda h, i, j: (0, h))
    call = _pcall(body, name="mla_flash_bwd", grid=(H, nb, nb),
                  in_specs=[pl.BlockSpec((t, 2 * LANES), lambda h, i, j: (i, h)),
                            pl.BlockSpec((t, LANES), kmap(0)), pl.BlockSpec((t, LANES), kmap(H)),
                            pl.BlockSpec((t, LANES), lambda h, i, j: (jnp.minimum(j, i), 0)),
                            pl.BlockSpec((t, LANES), lambda h, i, j: (i, h)),
                            pl.BlockSpec((None, t, 1), lambda h, i, j: (h, i, 0)),
                            pl.BlockSpec((t, LANES), lambda h, i, j: (i, dy_col0 + h))],
                  out_specs=[pl.BlockSpec((t, 2 * LANES), lambda h, i, j: (i, h)), head_col, head_col,
                             pl.BlockSpec((S, LANES), lambda h, i, j: (0, 0))],
                  out_shape=[jax.ShapeDtypeStruct((S, H * 2 * LANES), F32), jax.ShapeDtypeStruct((S, H * LANES), BF16),
                             jax.ShapeDtypeStruct((S, H * LANES), BF16), jax.ShapeDtypeStruct((S, LANES), F32)],
                  scratch_shapes=[pltpu.VMEM((t, 2 * LANES), F32), pltpu.VMEM((nb, t, LANES), F32),
                                  pltpu.VMEM((nb, t, LANES), F32)],
                  dims=("arbitrary", "arbitrary", "arbitrary"), hosted=hosted)
    (dq, dkn, dv, dkpe), extra = _hosting(call, hosted, qr, kv, kv, kpe, o, lse, dy)
    return (dq, jnp.concatenate([dkn, dv], axis=1), dkpe), extra


def loss_head(h, target, g, *, tr=256):
    S, D = h.shape
    tr = min(tr, S)

    def body(h_ref, t_ref, g_ref, loss_ref, dh_ref, dg_ref):
        tgt = t_ref[...]

        def f(hh, gg):
            err = jnp.square(_rms(hh, gg) - tgt)
            per_row = jnp.sum(err, axis=-1, keepdims=True) * (0.5 / D)
            return jnp.sum(per_row, axis=0, keepdims=True)

        val, vjp = jax.vjp(f, h_ref[...], g_ref[...])
        dh, dg = vjp(jnp.ones((1, 1), F32))
        dh_ref[...] = dh

        @pl.when(pl.program_id(0) == 0)
        def _():
            loss_ref[...] = jnp.zeros_like(loss_ref)
            dg_ref[...] = jnp.zeros_like(dg_ref)

        loss_ref[...] += jnp.broadcast_to(val, loss_ref.shape)
        dg_ref[...] += dg

    return _pcall(body, name="loss_head", grid=(S // tr,),
                  in_specs=[_row_spec(tr, D, 0), _row_spec(tr, D, 0), _full_spec(g)],
                  out_specs=[pl.BlockSpec((1, LANES), lambda i: (0, 0)), _row_spec(tr, D, 0), _full_spec(g)],
                  out_shape=[jax.ShapeDtypeStruct((1, LANES), F32), jax.ShapeDtypeStruct((S, D), F32),
                             jax.ShapeDtypeStruct(g.shape, F32)],
                  dims=("arbitrary",))(h, target, g)


def _rope_tables(positions, dim):
    inv_freq = ROPE_THETA ** (-jnp.arange(0, dim, 2, dtype=F32) / dim)
    ang = positions.astype(F32)[:, None] * inv_freq
    return jnp.cos(ang), jnp.sin(ang)


def _pad_cols(w, n):
    return jnp.pad(w, ((0, 0), (0, n - w.shape[1])))


def _prep_w_in0(w):
    return jnp.concatenate([w[:, :4096], w[:, 4112:5136], _pad_cols(w[:, 5136:5200], LANES),
                            _pad_cols(w[:, 4096:4112], LANES)], axis=1)


def _unprep_w_in0(g):
    return jnp.concatenate([g[:, :4096], g[:, ZIN_AB:ZIN_AB + 16], g[:, ZIN_CQ:ZIN_KR], g[:, ZIN_KR:ZIN_KR + MLA_ROPE]],
                           axis=1)


def _prep_w_uq(w):
    w = w.reshape(MLA_Q_RANK, MLA_HEADS, MLA_NOPE + MLA_ROPE)
    w = jnp.pad(w, ((0, 0), (0, 0), (0, 2 * LANES - MLA_NOPE - MLA_ROPE)))
    return w.reshape(MLA_Q_RANK, MLA_HEADS * 2 * LANES)


def _unprep_w_uq(g):
    g = g.reshape(MLA_Q_RANK, MLA_HEADS, 2 * LANES)[:, :, :MLA_NOPE + MLA_ROPE]
    return g.reshape(MLA_Q_RANK, MLA_HEADS * (MLA_NOPE + MLA_ROPE))


def _prep_w_ukv(w):
    w = w.reshape(MLA_KV_RANK, MLA_HEADS, 2, LANES)
    return jnp.transpose(w, (0, 2, 1, 3)).reshape(MLA_KV_RANK, 2 * MLA_HEADS * LANES)


def _unprep_w_ukv(g):
    g = g.reshape(MLA_KV_RANK, 2, MLA_HEADS, LANES)
    return jnp.transpose(g, (0, 2, 1, 3)).reshape(MLA_KV_RANK, 2 * MLA_HEADS * LANES)


def _row(v, n=None):
    v = v.reshape(1, -1).astype(F32)
    return v if n is None else _pad_cols(v, n)


def _ffn_fwd(h, norm_g, w_up, conv_w, conv_b, w_down, tag, hosted=None):
    (hn,) = rowwise(_fn_rms, [(h, D_MODEL, 0)], [norm_g], [], [(D_MODEL, BF16)], name=f"{tag}_ffn_norm")
    u = matmul(hn, w_up, "nn", F32, b_shards=N_CHIPS, **TILES["wide_nn"], name=f"{tag}_ffn_up", hosted=hosted)
    u, got = u if hosted is not None else (u, [])
    f = ffn_conv_fwd(u, conv_w, conv_b)
    h_out = matmul(f, w_down, "nn", F32, add=h, tm=512, tn=1024, tk=8192, name=f"{tag}_ffn_down")
    return h_out, (hn, u, f), got


def _ffn_bwd(dh, dh16, h, norm_g, w_up, conv_w, conv_b, w_down, saved, tag, make_hosted=None):
    hn, u, f = saved
    df = matmul(dh16, w_down, "nt", BF16, tm=512, tn=2816, name=f"{tag}_ffn_down_dx")
    g_down = matmul(f, dh16, "tn", BF16, **TILES["dw"], name=f"{tag}_ffn_down_dw")
    dc, g_conv_w, g_conv_b = ffn_conv_bwd(u, conv_w, conv_b, df)
    du = conv_transpose(dc, conv_w, BF16, name=f"{tag}_ffn_conv_dx")
    g_up = matmul(hn, du, "tn", BF16, out_shards=N_CHIPS, tm=1024, tn=1408, tk=4096, name=f"{tag}_ffn_up_dw")
    hosted = make_hosted(dict(ffn_w_up=g_up, ffn_w_down=g_down)) if make_hosted else None
    dhn = matmul(du, w_up, "nt", F32, b_shards=N_CHIPS, tm=512, tn=2048, tk=2816, name=f"{tag}_ffn_up_dx",
                 hosted=hosted)
    dhn, got = dhn if hosted is not None else (dhn, [])
    ((dh_in, dh_in16),), (g_norm,) = rowwise_bwd(_fn_rms, [(h, D_MODEL, 0)], [norm_g], [], [(dhn, D_MODEL, 0)],
                                                 [(F32, BF16)], adds=[(dh, D_MODEL, 0)], name=f"{tag}_ffn_norm_bwd")
    return dh_in, dh_in16, dict(ffn_norm=g_norm, ffn_w_up=g_up, ffn_conv_w=g_conv_w, ffn_conv_b=g_conv_b,
                                ffn_w_down=g_down), got


TILES = {"wide_nn": dict(tm=512, tn=3072, tk=2048),
         "square": dict(tm=512, tn=2048, tk=2048),
         "dw": dict(tm=512, tn=2048, tk=4096)}


def _ple_fwd(h, p_i, w_proj, gate_g, w_gate, tag):
    (hg,) = rowwise(_fn_rms, [(h, D_MODEL, 0)], [gate_g], [], [(D_MODEL, BF16)], name=f"{tag}_ple_norm")
    gl = matmul(hg, w_gate, "nn", F32, **TILES["square"], name=f"{tag}_ple_gate")
    pp = matmul(p_i, w_proj, "nn", F32, b_shards=N_CHIPS, name=f"{tag}_ple_proj")
    (h_out,) = rowwise(_fn_ple, [(h, D_MODEL, 0), (pp, D_MODEL, 0), (gl, D_MODEL, 0)], [], [], [(D_MODEL, F32)],
                       name=f"{tag}_ple_add")
    return h_out, (hg, gl, pp)


def _ple_bwd(dh, h, p_i, w_proj, gate_g, w_gate, saved, tag):
    hg, gl, pp = saved
    (dpp, dgl), _ = rowwise_bwd(_fn_ple_terms, [(pp, D_MODEL, 0), (gl, D_MODEL, 0)], [], [], [(dh, D_MODEL, 0)],
                                [BF16, BF16], name=f"{tag}_ple_add_bwd")
    g_proj = matmul(p_i, dpp, "tn", BF16, out_shards=N_CHIPS, name=f"{tag}_ple_proj_dw")
    g_gate = matmul(hg, dgl, "tn", BF16, **TILES["dw"], name=f"{tag}_ple_gate_dw")
    dh_in, dh_in16, g_norm = matmul(dgl, w_gate, "nt", F32, tm=256, tn=2048, tk=2048, name=f"{tag}_ple_gate_dx",
                                    norm_bwd=(h, gate_g, dh))
    return dh_in, dh_in16, dict(ple_proj=g_proj, ple_gate_norm=g_norm, ple_gate=g_gate)


def local_step(x, p, positions, target, slots, W, ex=None):
    S = x.shape[0]
    G = {}
    p0, p1 = p[0].astype(BF16), p[1].astype(BF16)
    W = dict(W)

    def use(names, bufs):
        for k, b in zip(names, bufs):
            r, c = b.shape[1:]
            W[k] = b.reshape(N_CHIPS * r, c) if k in ROW_SHARDED else (b if k in KEPT_SHARDED else _cols_to_full(b))

    def by_chip(names):
        out = []
        for k in names:
            r, c = slots[k].shape[1:]
            out.append(G[k].reshape(N_CHIPS, r, c) if k in ROW_SHARDED
                       else (G[k] if k in KEPT_SHARDED else _full_to_cols(G[k])))
        return out

    first = [slots[k] for k in GATHER_FIRST]
    use(GATHER_FIRST, ex.gather(first) if ex else first)

    cm, sm = _rope_tables(positions, MLA_ROPE)
    zeros = jnp.zeros((S, LANES - MLA_ROPE), F32)
    cosp = jnp.concatenate([cm, cm, zeros], axis=1)
    sinp = jnp.concatenate([sm, sm, zeros], axis=1)
    cr, sr = _rope_tables(positions, RET_DK)

    w_in0 = _prep_w_in0(W["l0_w_in"])
    a_row = _row(W["l0_gdn_A_log"], LANES)
    dt_row = _row(W["l0_gdn_dt_bias"], LANES)
    gdn_nw = _row(W["l0_gdn_norm"])
    n = {k: _row(W[k]) for k in ("l0_attn_norm", "l0_mla_q_norm", "l0_mla_kv_norm", "l0_ffn_norm",
                                 "l0_ple_gate_norm", "l1_attn_norm", "l1_ret_norm", "l1_ffn_norm",
                                 "l1_ple_gate_norm", "final_norm", "l0_ffn_conv_b", "l1_ffn_conv_b")}

    (hn0,) = rowwise(_fn_rms, [(x, D_MODEL, 0)], [n["l0_attn_norm"]], [], [(D_MODEL, BF16)], name="l0_attn_norm")
    zin = matmul(hn0, w_in0, "nn", F32, tm=512, tn=1792, name="l0_w_in")
    qkv = gdn_conv_fwd(zin, W["l0_gdn_conv"])
    layer0 = [slots[k] for k in GATHER_L0]
    (y_a, gdn_states), got = gdn_fwd(qkv, zin, a_row, dt_row, gdn_nw, hosted=hosted_gather(layer0) if ex else None)
    use(GATHER_L0, got if ex else layer0)
    w_uq = _prep_w_uq(W["l0_mla_w_uq"])
    w_ukv = _prep_w_ukv(W["l0_mla_w_ukv"])
    mla_rows = [(zin, MLA_Q_RANK, ZIN_CQ // MLA_Q_RANK), (zin, MLA_KV_RANK, ZIN_CKV // MLA_KV_RANK),
                (zin, LANES, ZIN_KR // LANES)]
    mla_nd = [(cosp, LANES, 0), (sinp, LANES, 0)]
    cqn, ckvn, kpe = rowwise(_fn_mla_pre, mla_rows, [n["l0_mla_q_norm"], n["l0_mla_kv_norm"]], mla_nd,
                             [(MLA_Q_RANK, BF16), (MLA_KV_RANK, BF16), (LANES, BF16)], name="mla_pre")
    q_lin = matmul(cqn, w_uq, "nn", F32, name="mla_w_uq")
    kv = matmul(ckvn, w_ukv, "nn", BF16, name="mla_w_ukv")
    (qr,) = rowwise(_fn_rope_q, [(q_lin, 2048, 0)], [], mla_nd, [(2048, BF16)],
                    name="mla_rope_q")
    layer1 = [slots[k] for k in GATHER_L1]
    (y_b, lse), got = flash_fwd(qr, kv, kpe, hosted=hosted_gather(layer1) if ex else None)
    use(GATHER_L1, got if ex else layer1)
    y_ab = jnp.concatenate([y_a, y_b], axis=1)
    h1 = matmul(y_ab, W["l0_w_out"], "nn", F32, add=x, **TILES["square"], name="l0_w_out")
    ffn_late = [slots[k] for k in GATHER_FFN]
    h2, ffn0, got = _ffn_fwd(h1, n["l0_ffn_norm"], W["l0_ffn_w_up"], W["l0_ffn_conv_w"], n["l0_ffn_conv_b"],
                             W["l0_ffn_w_down"], "l0", hosted=hosted_gather(ffn_late) if ex else None)
    use(GATHER_FFN, got if ex else ffn_late)
    h3, ple0 = _ple_fwd(h2, p0, W["l0_ple_proj"], n["l0_ple_gate_norm"], W["l0_ple_gate"], "l0")

    (hn1,) = rowwise(_fn_rms, [(h3, D_MODEL, 0)], [n["l1_attn_norm"]], [], [(D_MODEL, BF16)], name="l1_attn_norm")
    late = [slots[k] for k in GATHER_L1_IN]
    zz = matmul(hn1, W["l1_w_in"], "nn", F32, b_shards=N_CHIPS, **TILES["wide_nn"], name="l1_w_in",
                hosted=hosted_gather(late) if ex else None)
    zz, got = zz if ex else (zz, late)
    use(GATHER_L1_IN, got)
    (o_ret, ret_states), _ = ret_fwd(zz, cr, sr)
    gate_rows = [(zz, 4096, 2), (o_ret, 4096, 0)]
    (yg,) = rowwise(_fn_ret_gate, gate_rows, [n["l1_ret_norm"]], [], [(4096, BF16)], name="ret_gate")
    h4 = matmul(yg, W["l1_w_out"], "nn", F32, add=h3, tm=512, tn=2048, tk=4096, name="l1_w_out")
    h5, ffn1, _ = _ffn_fwd(h4, n["l1_ffn_norm"], W["l1_ffn_w_up"], W["l1_ffn_conv_w"], n["l1_ffn_conv_b"],
                           W["l1_ffn_w_down"], "l1")
    h6, ple1 = _ple_fwd(h5, p1, W["l1_ple_proj"], n["l1_ple_gate_norm"], W["l1_ple_gate"], "l1")

    loss_vec, dh, G["final_norm"] = loss_head(h6, target, n["final_norm"])

    dh, dh16, g = _ple_bwd(dh, h5, p1, W["l1_ple_proj"], n["l1_ple_gate_norm"], W["l1_ple_gate"], ple1, "l1")
    G.update({"l1_" + k: v for k, v in g.items()})
    dh, dh16, g, _ = _ffn_bwd(dh, dh16, h4, n["l1_ffn_norm"], W["l1_ffn_w_up"], W["l1_ffn_conv_w"],
                              n["l1_ffn_conv_b"], W["l1_ffn_w_down"], ffn1, "l1")
    G.update({"l1_" + k: v for k, v in g.items()})

    dyg = matmul(dh16, W["l1_w_out"], "nt", F32, tm=512, tn=4096, name="l1_w_out_dx")
    G["l1_w_out"] = matmul(yg, dh16, "tn", BF16, **TILES["dw"], name="l1_w_out_dw")
    (dg, do_ret), (G["l1_ret_norm"],) = rowwise_bwd(_fn_ret_gate, gate_rows, [n["l1_ret_norm"]], [],
                                                   [(dyg, 4096, 0)], [BF16, F32], name="ret_gate_bwd")
    dq, dk, dv = ret_bwd(zz, cr, sr, ret_states, do_ret)
    dzz = jnp.concatenate([dq, dk, dv, dg], axis=1)
    G["l1_w_in"] = matmul(hn1, dzz, "tn", BF16, out_shards=N_CHIPS, tm=1024, tn=1536, tk=4096, name="l1_w_in_dw")
    sums, landed = {}, {}
    grads_l1 = by_chip(REDUCE_L1)
    dhn = matmul(dzz, W["l1_w_in"], "nt", F32, b_shards=N_CHIPS, tm=1024, tn=1024, tk=3072, name="l1_w_in_dx",
                 hosted=hosted_swap(grads_l1) if ex else None)
    if ex:
        dhn, swapped = dhn
        sums.update(zip(REDUCE_L1, ex.pair_sums(REDUCE_L1, grads_l1, swapped)))
    (dh,), (G["l1_attn_norm"],) = rowwise_bwd(_fn_rms, [(h3, D_MODEL, 0)], [n["l1_attn_norm"]], [],
                                             [(dhn, D_MODEL, 0)], [F32], adds=[(dh, D_MODEL, 0)],
                                             name="l1_attn_norm_bwd")

    dh, dh16, g = _ple_bwd(dh, h2, p0, W["l0_ple_proj"], n["l0_ple_gate_norm"], W["l0_ple_gate"], ple0, "l0")
    G.update({"l0_" + k: v for k, v in g.items()})
    mid = REDUCE_DQ + REDUCE_L0

    def swap_mid(g_ffn):
        G.update({"l0_" + k: v for k, v in g_ffn.items()})
        return hosted_swap(by_chip(mid))

    dh, dh16, g, swapped = _ffn_bwd(dh, dh16, h1, n["l0_ffn_norm"], W["l0_ffn_w_up"], W["l0_ffn_conv_w"],
                                    n["l0_ffn_conv_b"], W["l0_ffn_w_down"], ffn0, "l0",
                                    make_hosted=swap_mid if ex else None)
    G.update({"l0_" + k: v for k, v in g.items()})
    if ex:
        sums.update(zip(mid, ex.pair_sums(mid, by_chip(mid), swapped)))

    dy_ab = matmul(dh16, W["l0_w_out"], "nt", F32, **TILES["square"], name="l0_w_out_dx")
    G["l0_w_out"] = matmul(y_ab, dh16, "tn", BF16, **TILES["dw"], name="l0_w_out_dw")
    (dq, dk, dv, dz, dab, g_a, g_dt, G["l0_gdn_norm"]), got = gdn_bwd(
        qkv, zin, a_row, dt_row, gdn_nw, gdn_states, dy_ab, 0,
        hosted=hosted_scatter([sums[k] for k in REDUCE_L1]) if ex else None)
    landed.update(zip(REDUCE_L1, got))
    G["l0_gdn_A_log"], G["l0_gdn_dt_bias"] = g_a[:, :GDN_HEADS], g_dt[:, :GDN_HEADS]
    dpre, G["l0_gdn_conv"] = gdn_conv_bwd(zin, W["l0_gdn_conv"], jnp.concatenate([dq, dk, dv], axis=1))
    dqkv = conv_transpose(dpre, W["l0_gdn_conv"], BF16, name="gdn_conv_dx")
    (dqr, dkv, dkpe), got = flash_bwd(qr, kv, kpe, y_b, lse, dy_ab, MLA_HEADS,
                                      hosted=hosted_scatter([sums[k] for k in mid]) if ex else None)
    landed.update(zip(mid, got))
    (dq_lin,), _ = rowwise_bwd(_fn_rope_q, [(q_lin, 2048, 0)], [], mla_nd, [(dqr, 2048, 0)], [BF16],
                               name="mla_rope_q_bwd")
    G["l0_mla_w_uq"] = _unprep_w_uq(matmul(cqn, dq_lin, "tn", BF16, name="mla_w_uq_dw"))
    dcqn = matmul(dq_lin, w_uq, "nt", F32, name="mla_w_uq_dx")
    G["l0_mla_w_ukv"] = _unprep_w_ukv(matmul(ckvn, dkv, "tn", BF16, name="mla_w_ukv_dw"))
    dckvn = matmul(dkv, w_ukv, "nt", F32, name="mla_w_ukv_dx")
    (dcq, dckv, dkr), (G["l0_mla_q_norm"], G["l0_mla_kv_norm"]) = rowwise_bwd(
        _fn_mla_pre, mla_rows, [n["l0_mla_q_norm"], n["l0_mla_kv_norm"]], mla_nd,
        [(dcqn, MLA_Q_RANK, 0), (dckvn, MLA_KV_RANK, 0), (dkpe, LANES, 0)], [BF16, BF16, BF16], name="mla_pre_bwd")
    dzin = jnp.concatenate([dqkv, dz, dcq, dckv, dkr, dab.astype(BF16)], axis=1)
    early = REDUCE_L1 + REDUCE_DQ + REDUCE_L0
    reduced = {}
    g_in = matmul(hn0, dzin, "tn", BF16, tm=512, tn=1792, tk=4096, name="l0_w_in_dw",
                  hosted=hosted_join(ex.halves(early, sums, landed)) if ex else None)
    if ex:
        g_in, joined = g_in
        reduced.update(zip(early, joined))
    G["l0_w_in"] = _unprep_w_in0(g_in)
    if ex:
        sums.update(zip(REDUCE_LAST, ex.pair_sums(REDUCE_LAST, by_chip(REDUCE_LAST))))
    dhn = matmul(dzin, w_in0, "nt", F32, tm=512, tn=2048, tk=5376, name="l0_w_in_dx",
                 hosted=hosted_scatter([sums[k] for k in REDUCE_LAST]) if ex else None)
    if ex:
        dhn, got = dhn
        landed.update(zip(REDUCE_LAST, got))
    (grad_x,), (G["l0_attn_norm"],) = rowwise_bwd(_fn_rms, [(x, D_MODEL, 0)], [n["l0_attn_norm"]], [],
                                                 [(dhn, D_MODEL, 0)], [F32], adds=[(dh, D_MODEL, 0)],
                                                 name="l0_attn_norm_bwd")
    small = {k: G[k] for k in SMALL}
    if not ex:
        return loss_vec[0, 0], grad_x, dict(zip(BIG, by_chip(BIG))), small
    reduced.update(zip(REDUCE_LAST, pair_join_halves(ex.halves(REDUCE_LAST, sums, landed))))
    return loss_vec[0, 0], grad_x, reduced, small


HBM = pl.BlockSpec(memory_space=pltpu.HBM)
VMEM = pl.BlockSpec(memory_space=pltpu.VMEM)


def _place():
    return lax.axis_index("x"), lax.axis_index("y"), lax.axis_index("c")


def _other_chips(x, y):
    return [(1 - x, y), (x, 1 - y), (1 - x, 1 - y)]


def _comm_call(body, *, name, out_shape, in_specs, out_specs, scratch_shapes):
    return pl.pallas_call(body, name=name, out_shape=out_shape, in_specs=in_specs, out_specs=out_specs,
                          scratch_shapes=list(scratch_shapes),
                          compiler_params=pltpu.CompilerParams(vmem_limit_bytes=VMEM_LIMIT_MB << 20))


def _inplace_comm_call(body, bufs, *, name, n_sems):
    n = len(bufs)
    return pl.pallas_call(body, name=name, out_shape=[jax.ShapeDtypeStruct(b.shape, b.dtype) for b in bufs],
                          in_specs=[HBM] * n, out_specs=[HBM] * n, input_output_aliases={i: i for i in range(n)},
                          scratch_shapes=[pltpu.SemaphoreType.DMA((n_sems,)), pltpu.SemaphoreType.DMA((n_sems,))],
                          compiler_params=pltpu.CompilerParams(vmem_limit_bytes=VMEM_LIMIT_MB << 20))(*bufs)


def all_gather_chips(bufs):
    n_sems, start, finish = _gather_phase(len(bufs))
    n = len(bufs)

    def body(*refs):
        outs, send_sems, recv_sems = refs[n:2 * n], refs[2 * n], refs[2 * n + 1]
        start(None, outs, send_sems, recv_sems)
        finish(None, outs, send_sems, recv_sems)

    return _inplace_comm_call(body, bufs, name="all_gather_chips", n_sems=n_sems)


def _gather_phase(n):
    def plan(outs, send_sems, recv_sems):
        x, y, c = _place()

        def copy(w, k, chip, hc, to):
            half = outs[w].shape[1] // 2
            rows = outs[w].at[2 * chip[0] + chip[1], pl.ds(hc * half, half), :]
            return pltpu.make_async_remote_copy(src_ref=rows, dst_ref=rows, send_sem=send_sems.at[6 * w + k],
                                                recv_sem=recv_sems.at[6 * w + k], device_id=to, device_id_type=MESH)

        first = [[copy(w, k, (x, y), c, (*chip, c)) for k, chip in enumerate(_other_chips(x, y))] for w in range(n)]
        passed = [[copy(w, 3 + k, chip, c, (x, y, 1 - c)) for k, chip in enumerate(_other_chips(x, y))]
                  for w in range(n)]
        return copy, first, passed, (x, y, c)

    def start(_, outs, send_sems, recv_sems):
        _, first, _, _ = plan(outs, send_sems, recv_sems)
        for w in range(n):
            for cp in first[w]:
                cp.start()

    def finish(_, outs, send_sems, recv_sems):
        copy, first, passed, (x, y, c) = plan(outs, send_sems, recv_sems)
        chips = _other_chips(x, y)
        for w in range(n):
            for k, chip in enumerate(chips):
                copy(w, k, chip, c, (x, y, c)).wait_recv()
                passed[w][k].start()
        for w in range(n):
            for k, chip in enumerate(chips):
                copy(w, 3 + k, chip, 1 - c, (x, y, c)).wait_recv()
        for w in range(n):
            for cp in first[w] + passed[w]:
                cp.wait_send()

    return 6 * n, start, finish


def hosted_gather(bufs):
    n_sems, start, finish = _gather_phase(len(bufs))
    return Hosted(bufs, [jax.ShapeDtypeStruct(b.shape, b.dtype) for b in bufs], {i: i for i in range(len(bufs))},
                  n_sems, start, finish)


def pair_swap_halves(gs):
    n = len(gs)
    n_sems, start, finish = _swap_phase(n)

    def body(*refs):
        g_refs, o_refs, send_sems, recv_sems = refs[:n], refs[n:2 * n], refs[2 * n], refs[2 * n + 1]
        start(g_refs, o_refs, send_sems, recv_sems)
        finish(g_refs, o_refs, send_sems, recv_sems)

    return _comm_call(body, name="pair_swap_halves", out_shape=_swap_shapes(gs), in_specs=[HBM] * n, out_specs=[HBM] * n,
                      scratch_shapes=[pltpu.SemaphoreType.DMA((n_sems,)), pltpu.SemaphoreType.DMA((n_sems,))])(*gs)


def _swap_shapes(gs):
    return [jax.ShapeDtypeStruct((N_CHIPS, g.shape[1] // 2, g.shape[2]), g.dtype) for g in gs]


def _swap_phase(n):
    def copies(g_refs, o_refs, send_sems, recv_sems):
        x, y, c = _place()
        out = []
        for w in range(n):
            half = g_refs[w].shape[1] // 2
            out.append(pltpu.make_async_remote_copy(
                src_ref=g_refs[w].at[:, pl.ds((1 - c) * half, half), :], dst_ref=o_refs[w], send_sem=send_sems.at[w],
                recv_sem=recv_sems.at[w], device_id=(x, y, 1 - c), device_id_type=MESH))
        return out

    def start(*refs):
        for cp in copies(*refs):
            cp.start()

    def finish(*refs):
        for cp in copies(*refs):
            cp.wait()

    return n, start, finish


def hosted_swap(gs):
    n_sems, start, finish = _swap_phase(len(gs))
    return Hosted(gs, _swap_shapes(gs), {}, n_sems, start, finish)


def scatter_chips(ps):
    n = len(ps)
    n_sems, start, finish = _scatter_phase(n)

    def body(*refs):
        p_refs, o_refs, send_sems, recv_sems = refs[:n], refs[n:2 * n], refs[2 * n], refs[2 * n + 1]
        start(p_refs, o_refs, send_sems, recv_sems)
        finish(p_refs, o_refs, send_sems, recv_sems)

    return _comm_call(body, name="scatter_chips", out_shape=_scatter_shapes(ps), in_specs=[HBM] * n, out_specs=[HBM] * n,
                      scratch_shapes=[pltpu.SemaphoreType.DMA((n_sems,)), pltpu.SemaphoreType.DMA((n_sems,))])(*ps)


def _scatter_shapes(ps):
    return [jax.ShapeDtypeStruct((3,) + p.shape[1:], p.dtype) for p in ps]


def _scatter_phase(n):
    def copies(p_refs, o_refs, send_sems, recv_sems):
        x, y, c = _place()
        return [pltpu.make_async_remote_copy(src_ref=p_refs[w].at[2 * chip[0] + chip[1]], dst_ref=o_refs[w].at[k],
                                             send_sem=send_sems.at[3 * w + k], recv_sem=recv_sems.at[3 * w + k],
                                             device_id=(*chip, c), device_id_type=MESH)
                for w in range(n) for k, chip in enumerate(_other_chips(x, y))]

    def start(*refs):
        for cp in copies(*refs):
            cp.start()

    def finish(*refs):
        for cp in copies(*refs):
            cp.wait()

    return 3 * n, start, finish


def hosted_scatter(ps):
    n_sems, start, finish = _scatter_phase(len(ps))
    return Hosted(ps, _scatter_shapes(ps), {}, n_sems, start, finish)


def pair_join_halves(rs):
    n = len(rs)
    n_sems, start, finish = _join_phase(n)

    def body(*refs):
        outs, send_sems, recv_sems = refs[n:2 * n], refs[2 * n], refs[2 * n + 1]
        start(None, outs, send_sems, recv_sems)
        finish(None, outs, send_sems, recv_sems)

    return _inplace_comm_call(body, rs, name="pair_join_halves", n_sems=n_sems)


def _join_phase(n):
    def copies(_, outs, send_sems, recv_sems):
        x, y, c = _place()
        out = []
        for w in range(n):
            half = outs[w].shape[0] // 2
            rows = outs[w].at[pl.ds(c * half, half), :]
            out.append(pltpu.make_async_remote_copy(src_ref=rows, dst_ref=rows, send_sem=send_sems.at[w],
                                                    recv_sem=recv_sems.at[w], device_id=(x, y, 1 - c),
                                                    device_id_type=MESH))
        return out

    def start(*refs):
        for cp in copies(*refs):
            cp.start()

    def finish(*refs):
        for cp in copies(*refs):
            cp.wait()

    return n, start, finish


def hosted_join(rs):
    n_sems, start, finish = _join_phase(len(rs))
    return Hosted(rs, [jax.ShapeDtypeStruct(r.shape, r.dtype) for r in rs], {i: i for i in range(len(rs))},
                  n_sems, start, finish)


def all_reduce_small(v, name):
    n, L = v.shape
    n_dev = 8

    def body(v_ref, out_ref, buf, send_sems, recv_sems):
        x, y, c = _place()
        me = 4 * x + 2 * y + c
        buf[me] = v_ref[...]

        def copy(k, slot, peer):
            return pltpu.make_async_remote_copy(src_ref=v_ref, dst_ref=buf.at[slot], send_sem=send_sems.at[k],
                                                recv_sem=recv_sems.at[slot],
                                                device_id=(peer // 4, (peer // 2) % 2, peer % 2), device_id_type=MESH)

        sends = [copy(k - 1, me, (me + k) % n_dev) for k in range(1, n_dev)]
        for cp in sends:
            cp.start()
        for k in range(1, n_dev):
            src = (me + k) % n_dev
            copy(0, src, src).wait_recv()
        for cp in sends:
            cp.wait_send()
        acc = buf[0]
        for s in range(1, n_dev):
            acc = acc + buf[s]
        out_ref[...] = acc

    return _comm_call(body, name=name, out_shape=jax.ShapeDtypeStruct((n, L), v.dtype), in_specs=[VMEM], out_specs=VMEM,
                      scratch_shapes=[pltpu.VMEM((n_dev, n, L), v.dtype), pltpu.SemaphoreType.DMA((n_dev - 1,)),
                                      pltpu.SemaphoreType.DMA((n_dev,))])(v)


BF16_ROWS = 16
STREAM_BLOCK_BYTES = 4 << 20


def _rows_tile(n, row_bytes, budget=1 << 20, mult=SUBLANES):
    best = mult if n % mult == 0 else n
    for t in range(mult, n + 1, mult):
        if n % t == 0 and t * row_bytes <= budget:
            best = t
    return best


def _scalars(*vals):
    return jnp.stack([jnp.asarray(v, jnp.int32) for v in vals])


def cast_to_slot(w, chip, name):
    r, c = w.shape
    tb = _rows_tile(r, c * 4, budget=STREAM_BLOCK_BYTES, mult=BF16_ROWS)

    def body(s_ref, w_ref, o_ref):
        o_ref[...] = w_ref[...].astype(BF16)

    spec = pltpu.PrefetchScalarGridSpec(
        num_scalar_prefetch=1, grid=(r // tb,), in_specs=[pl.BlockSpec((tb, c), lambda i, s: (i, 0))],
        out_specs=pl.BlockSpec((None, tb, c), lambda i, s: (s[0], i, 0)))
    return pl.pallas_call(body, name=name, grid_spec=spec, out_shape=jax.ShapeDtypeStruct((N_CHIPS, r, c), BF16),
                          compiler_params=pltpu.CompilerParams(dimension_semantics=("parallel",)))(_scalars(chip), w)


def pair_add(g, got, c, name):
    _, r, w = g.shape
    half = r // 2
    tb = _rows_tile(half, w * 4, budget=STREAM_BLOCK_BYTES, mult=BF16_ROWS)
    nb = half // tb

    def body(c_ref, g_ref, got_ref, o_ref):
        o_ref[...] = (g_ref[...].astype(F32) + got_ref[...].astype(F32)).astype(o_ref.dtype)

    spec = pltpu.PrefetchScalarGridSpec(
        num_scalar_prefetch=1, grid=(N_CHIPS, nb),
        in_specs=[pl.BlockSpec((None, tb, w), lambda s, i, c_ref: (s, c_ref[0] * nb + i, 0)),
                  pl.BlockSpec((None, tb, w), lambda s, i, c_ref: (s, i, 0))],
        out_specs=pl.BlockSpec((None, tb, w), lambda s, i, c_ref: (s, i, 0)))
    return pl.pallas_call(body, name=name, grid_spec=spec, out_shape=jax.ShapeDtypeStruct((N_CHIPS, half, w), BF16),
                          compiler_params=pltpu.CompilerParams(dimension_semantics=("parallel", "parallel")))(
        _scalars(c), g, got)


def chip_add(p, got, chip, c, name):
    _, h, w = p.shape
    tb = _rows_tile(h, w * 4, budget=STREAM_BLOCK_BYTES, mult=BF16_ROWS)
    nb = h // tb

    def body(s_ref, p_ref, got_ref, o_ref):
        acc = p_ref[...].astype(F32)
        for k in range(3):
            acc = acc + got_ref[k].astype(F32)
        o_ref[...] = acc

    spec = pltpu.PrefetchScalarGridSpec(
        num_scalar_prefetch=1, grid=(nb,),
        in_specs=[pl.BlockSpec((None, tb, w), lambda i, s: (s[0], i, 0)),
                  pl.BlockSpec((3, tb, w), lambda i, s: (0, i, 0))],
        out_specs=pl.BlockSpec((tb, w), lambda i, s: (s[1] * nb + i, 0)))
    return pl.pallas_call(body, name=name, grid_spec=spec, out_shape=jax.ShapeDtypeStruct((2 * h, w), F32),
                          compiler_params=pltpu.CompilerParams(dimension_semantics=("parallel",)))(
        _scalars(chip, c), p, got)


def adamw(w, g, m, v, name):
    r, c = w.shape
    tr = _rows_tile(r, c * 4, budget=STREAM_BLOCK_BYTES // 2)

    def body(w_ref, g_ref, m_ref, v_ref, d_ref, m_out, v_out):
        gg = g_ref[...]
        m2 = ADAM_B1 * m_ref[...] + (1.0 - ADAM_B1) * gg
        v2 = ADAM_B2 * v_ref[...] + (1.0 - ADAM_B2) * jnp.square(gg)
        m_hat = m2 / (1.0 - ADAM_B1 ** ADAM_STEP)
        v_hat = v2 / (1.0 - ADAM_B2 ** ADAM_STEP)
        d_ref[...] = -ADAM_LR * (m_hat / (jnp.sqrt(v_hat) + ADAM_EPS) + ADAM_WD * w_ref[...])
        m_out[...] = m2
        v_out[...] = v2

    blk = pl.BlockSpec((tr, c), lambda i: (i, 0))
    return _pcall(body, name=name, grid=(r // tr,), in_specs=[blk] * 4, out_specs=[blk] * 3,
                  out_shape=[jax.ShapeDtypeStruct((r, c), F32)] * 3, dims=("parallel",))(w, g, m, v)


WEIGHTS = ["l0_attn_norm", "l0_w_in", "l0_gdn_conv", "l0_gdn_A_log", "l0_gdn_dt_bias", "l0_gdn_norm", "l0_mla_q_norm",
           "l0_mla_w_uq", "l0_mla_kv_norm", "l0_mla_w_ukv", "l0_w_out", "l0_ffn_norm", "l0_ffn_w_up", "l0_ffn_conv_w",
           "l0_ffn_conv_b", "l0_ffn_w_down", "l0_ple_proj", "l0_ple_gate_norm", "l0_ple_gate", "l1_attn_norm",
           "l1_w_in", "l1_ret_norm", "l1_w_out", "l1_ffn_norm", "l1_ffn_w_up", "l1_ffn_conv_w", "l1_ffn_conv_b",
           "l1_ffn_w_down", "l1_ple_proj", "l1_ple_gate_norm", "l1_ple_gate", "final_norm"]
COL_SHARDED = ["l0_w_in", "l0_mla_w_uq", "l0_mla_w_ukv", "l0_ffn_w_up", "l0_ple_proj", "l1_w_in", "l1_ffn_w_up",
               "l1_ple_proj"]
ROW_SHARDED = ["l0_w_out", "l0_ffn_w_down", "l0_ple_gate", "l1_w_out", "l1_ffn_w_down", "l1_ple_gate"]
BIG = [k for k in WEIGHTS if k in COL_SHARDED or k in ROW_SHARDED]
SMALL_SHARDED = ["l0_gdn_conv", "l0_ffn_conv_w", "l1_ffn_conv_w"]
SMALL = [k for k in WEIGHTS if k not in BIG]
KEPT_SHARDED = ["l0_ffn_w_up", "l0_ple_proj", "l1_w_in", "l1_ffn_w_up", "l1_ple_proj"]
GATHER_FIRST = ["l0_w_in"]
GATHER_L0 = ["l0_mla_w_uq", "l0_mla_w_ukv", "l0_w_out", "l0_ffn_w_up", "l0_ffn_w_down", "l0_ple_proj", "l0_ple_gate",
             "l1_w_out"]
GATHER_L1 = ["l1_w_in"]
GATHER_FFN = ["l1_ffn_w_down", "l1_ple_proj", "l1_ple_gate"]
GATHER_L1_IN = ["l1_ffn_w_up"]
REDUCE_L1 = [k for k in BIG if k.startswith("l1_")]
REDUCE_DQ = ["l0_ffn_w_up"]
REDUCE_L0 = ["l0_ffn_w_down", "l0_ple_proj", "l0_ple_gate"]
REDUCE_LAST = ["l0_w_in", "l0_mla_w_uq", "l0_mla_w_ukv", "l0_w_out"]


class Exchange:
    def __init__(self, chip, core):
        self.chip, self.core = chip, core

    def gather(self, bufs):
        return all_gather_chips(bufs)

    def pair_sums(self, names, grads, swapped=None):
        swapped = pair_swap_halves(grads) if swapped is None else swapped
        return [pair_add(g, got, self.core, "rs_pair_add_" + k) for k, g, got in zip(names, grads, swapped)]

    def halves(self, names, sums, landed):
        return [chip_add(sums[k], landed[k], self.chip, self.core, "rs_chip_add_" + k) for k in names]


def _cols_to_full(s):
    j, k, n = s.shape
    return jnp.transpose(s, (1, 0, 2)).reshape(k, j * n)


def _full_to_cols(g):
    k, n4 = g.shape
    return jnp.transpose(g.reshape(k, N_CHIPS, n4 // N_CHIPS), (1, 0, 2))


def _pack_small(vals):
    flat = jnp.concatenate([v.astype(F32).reshape(-1) for v in vals])
    align = SUBLANES * LANES
    flat = jnp.pad(flat, (0, -flat.shape[0] % align))
    return flat.reshape(-1, LANES)


def _unpack_small(rows, shapes):
    flat = rows.reshape(-1)
    out, off = [], 0
    for shp in shapes:
        n = int(np.prod(shp))
        out.append(flat[off:off + n].reshape(shp))
        off += n
    return out


INPUTS = (["x", "p", "positions"] + WEIGHTS + ["loss_target"] + ["m_" + k for k in WEIGHTS]
          + ["v_" + k for k in WEIGHTS])


def kernel(
        x, p, positions, l0_attn_norm, l0_w_in, l0_gdn_conv, l0_gdn_A_log, l0_gdn_dt_bias, l0_gdn_norm, l0_mla_q_norm,
        l0_mla_w_uq, l0_mla_kv_norm, l0_mla_w_ukv, l0_w_out, l0_ffn_norm, l0_ffn_w_up, l0_ffn_conv_w, l0_ffn_conv_b,
        l0_ffn_w_down, l0_ple_proj, l0_ple_gate_norm, l0_ple_gate, l1_attn_norm, l1_w_in, l1_ret_norm, l1_w_out,
        l1_ffn_norm, l1_ffn_w_up, l1_ffn_conv_w, l1_ffn_conv_b, l1_ffn_w_down, l1_ple_proj, l1_ple_gate_norm,
        l1_ple_gate, final_norm, loss_target, m_l0_attn_norm, m_l0_w_in, m_l0_gdn_conv, m_l0_gdn_A_log,
        m_l0_gdn_dt_bias, m_l0_gdn_norm, m_l0_mla_q_norm, m_l0_mla_w_uq, m_l0_mla_kv_norm, m_l0_mla_w_ukv, m_l0_w_out,
        m_l0_ffn_norm, m_l0_ffn_w_up, m_l0_ffn_conv_w, m_l0_ffn_conv_b, m_l0_ffn_w_down, m_l0_ple_proj,
        m_l0_ple_gate_norm, m_l0_ple_gate, m_l1_attn_norm, m_l1_w_in, m_l1_ret_norm, m_l1_w_out, m_l1_ffn_norm,
        m_l1_ffn_w_up, m_l1_ffn_conv_w, m_l1_ffn_conv_b, m_l1_ffn_w_down, m_l1_ple_proj, m_l1_ple_gate_norm,
        m_l1_ple_gate, m_final_norm, v_l0_attn_norm, v_l0_w_in, v_l0_gdn_conv, v_l0_gdn_A_log, v_l0_gdn_dt_bias,
        v_l0_gdn_norm, v_l0_mla_q_norm, v_l0_mla_w_uq, v_l0_mla_kv_norm, v_l0_mla_w_ukv, v_l0_w_out, v_l0_ffn_norm,
        v_l0_ffn_w_up, v_l0_ffn_conv_w, v_l0_ffn_conv_b, v_l0_ffn_w_down, v_l0_ple_proj, v_l0_ple_gate_norm,
        v_l0_ple_gate, v_l1_attn_norm, v_l1_w_in, v_l1_ret_norm, v_l1_w_out, v_l1_ffn_norm, v_l1_ffn_w_up,
        v_l1_ffn_conv_w, v_l1_ffn_conv_b, v_l1_ffn_w_down, v_l1_ple_proj, v_l1_ple_gate_norm, v_l1_ple_gate,
        v_final_norm):
    given = locals()
    a = {k: given[k] for k in INPUTS}
    x_i, y_i, c_i = _place()
    chip = 2 * x_i + y_i
    shard_shapes = {k: a[k].shape for k in WEIGHTS}

    slots = {k: cast_to_slot(a[k], chip, "cast_" + k) for k in BIG}
    W = {}
    placed = []
    for k in SMALL_SHARDED:
        r, c = shard_shapes[k]
        mine = jnp.where(c_i == 0, a[k], jnp.zeros_like(a[k]))
        placed.append(lax.dynamic_update_slice(jnp.zeros((r, N_CHIPS * c), F32), mine, (0, chip * c)))
    full_small = _unpack_small(all_reduce_small(_pack_small(placed), "gather_small_weights"),
                               [p_.shape for p_ in placed])
    for k in SMALL:
        W[k] = a[k]
    W.update(dict(zip(SMALL_SHARDED, full_small)))

    loss_part, grad_x, grads, G = local_step(a["x"][0], a["p"][:, 0], a["positions"][0], a["loss_target"][0], slots, W,
                                             Exchange(chip, c_i))
    loss = lax.psum(loss_part, ("x", "y", "c"))
    deltas, new_m, new_v = {}, {}, {}
    for k in BIG:
        deltas[k], new_m[k], new_v[k] = adamw(a[k], grads[k], a["m_" + k], a["v_" + k], "adamw_" + k)

    small_full = [G[k].reshape(-1) for k in SMALL]
    summed = _unpack_small(all_reduce_small(_pack_small(small_full), "reduce_small_grads"),
                           [G[k].shape for k in SMALL])
    for k, g in zip(SMALL, summed):
        if k in SMALL_SHARDED:
            r, c = shard_shapes[k]
            g = lax.dynamic_slice(g.reshape(r, N_CHIPS * c), (0, chip * c), (r, c))
        grads[k] = g.reshape(shard_shapes[k])
    packed = [_pack_small([d[k] for k in SMALL]) for d in (
        {k: a[k] for k in SMALL}, grads, {k: a["m_" + k] for k in SMALL}, {k: a["v_" + k] for k in SMALL})]
    outs = adamw(*packed, "adamw_small")
    shapes = [shard_shapes[k] for k in SMALL]
    for d, rows in zip((deltas, new_m, new_v), outs):
        d.update(dict(zip(SMALL, _unpack_small(rows, shapes))))

    return (loss, grad_x[None], *[grads[k] for k in WEIGHTS], *[deltas[k] for k in WEIGHTS],
            *[new_m[k] for k in WEIGHTS], *[new_v[k] for k in WEIGHTS])
```

```python
import functools
import math

import numpy as np
import jax
import jax.numpy as jnp
from jax import lax
from jax.experimental import pallas as pl
from jax.experimental.pallas import tpu as pltpu

F32, BF16 = jnp.float32, jnp.bfloat16
HI = lax.Precision.HIGHEST
MESH = pl.DeviceIdType.MESH

NORM_EPS = 1e-6
ROPE_THETA = 10000.0
D_MODEL = 2048
PLE_DIM = 256
GDN_HEADS, GDN_DK, GDN_DV, GDN_CONV = 8, 128, 128, 4
MLA_HEADS, MLA_Q_RANK, MLA_KV_RANK, MLA_NOPE, MLA_ROPE, MLA_V = 8, 512, 512, 128, 64, 128
RET_HEADS, RET_DK, RET_DV = 8, 256, 512
D_FF, FFN_CONV = 5632, 3
ADAM_LR, ADAM_B1, ADAM_B2, ADAM_EPS, ADAM_WD, ADAM_STEP = 0.001, 0.9, 0.999, 1e-08, 0.01, 10

LANES = 128
SUBLANES = 8
CHUNK = 128
N_CHIPS = 4
VMEM_LIMIT_MB = 56

ZIN_QKV, ZIN_Z, ZIN_CQ, ZIN_CKV, ZIN_KR, ZIN_AB, ZIN_W = 0, 3072, 4096, 4608, 5120, 5248, 5376


class Hosted:
    def __init__(self, inputs, out_shapes, aliases, n_sems, start, finish):
        self.inputs, self.out_shapes, self.aliases, self.n_sems = list(inputs), list(out_shapes), dict(aliases), n_sems
        self.start, self.finish = start, finish


def _pcall(body, *, name, out_shape, grid=(), in_specs=None, out_specs=None, scratch_shapes=(), dims=None,
           hosted=None):
    params = dict(vmem_limit_bytes=VMEM_LIMIT_MB << 20)
    if dims is not None:
        params["dimension_semantics"] = dims
    if hosted is None:
        return pl.pallas_call(body, name=name, out_shape=out_shape, grid=grid, in_specs=in_specs, out_specs=out_specs,
                              scratch_shapes=list(scratch_shapes), compiler_params=pltpu.CompilerParams(**params))
    single = not isinstance(out_shape, (list, tuple))
    out_shape = [out_shape] if single else list(out_shape)
    out_specs = [out_specs] if single else list(out_specs)
    n_in, n_out, n_scr = len(in_specs), len(out_shape), len(scratch_shapes)
    h_in, h_out = len(hosted.inputs), len(hosted.out_shapes)
    hbm = pl.BlockSpec(memory_space=pltpu.HBM)

    def hosting_body(*refs):
        ins, h_ins = refs[:n_in], refs[n_in:n_in + h_in]
        o0 = n_in + h_in
        outs, h_outs = refs[o0:o0 + n_out], refs[o0 + n_out:o0 + n_out + h_out]
        s0 = o0 + n_out + h_out
        scr, (send_sems, recv_sems) = refs[s0:s0 + n_scr], refs[s0 + n_scr:]
        ids = [pl.program_id(d) for d in range(len(grid))]
        first = functools.reduce(lambda u, v: u & v, [i == 0 for i in ids])
        last = functools.reduce(lambda u, v: u & v, [i == g - 1 for i, g in zip(ids, grid)])

        @pl.when(first)
        def _():
            hosted.start(h_ins, h_outs, send_sems, recv_sems)

        body(*ins, *outs, *scr)

        @pl.when(last)
        def _():
            hosted.finish(h_ins, h_outs, send_sems, recv_sems)

    params["dimension_semantics"] = ("arbitrary",) * len(grid)
    call = pl.pallas_call(
        hosting_body, name=name, out_shape=out_shape + hosted.out_shapes, grid=grid,
        in_specs=list(in_specs) + [hbm] * h_in, out_specs=out_specs + [hbm] * h_out,
        scratch_shapes=list(scratch_shapes) + [pltpu.SemaphoreType.DMA((hosted.n_sems,)),
                                               pltpu.SemaphoreType.DMA((hosted.n_sems,))],
        input_output_aliases={n_in + i: n_out + o for i, o in hosted.aliases.items()},
        compiler_params=pltpu.CompilerParams(**params))

    def run(*args):
        res = call(*args, *hosted.inputs)
        main = res[:n_out]
        return (main[0] if single else main), list(res[n_out:])

    return run


def _tile(n, target, mult=LANES):
    best = None
    for t in range(mult, min(n, target) + 1, mult):
        if n % t == 0:
            best = t
    return best or n


_DN = {"nn": (((1,), (0,)), ((), ())), "nt": (((1,), (1,)), ((), ())), "tn": (((0,), (0,)), ((), ()))}


def matmul(a, b, mode, out_dtype, *, name, add=None, b_shards=1, out_shards=1, tm=512, tn=1024, tk=2048,
           hosted=None, norm_bwd=None, b_shard0=0):
    bs = b.shape[-2:]
    if mode == "nn":
        (M, K), (K2, N) = a.shape, (bs[0], bs[1] * b_shards)
    elif mode == "nt":
        (M, K), (N, K2) = a.shape, (bs[0], bs[1] * b_shards)
    else:
        (K, M), (K2, N) = a.shape, bs
    assert K == K2, (name, a.shape, b.shape)
    n_sh = N // max(b_shards if mode == "nn" else 1, out_shards)
    k_sh = K // (b_shards if mode == "nt" else 1)
    tm, tn, tk = _tile(M, tm), _tile(n_sh, tn), _tile(k_sh, tk)
    nk = K // tk
    nbn, nbk = n_sh // tn, k_sh // tk
    dn = _DN[mode]
    has_add = add is not None
    a_bytes, b_bytes = a.size * a.dtype.itemsize, b.size * b.dtype.itemsize
    i_outer = nk > 1 or a_bytes + (M // tm) * b_bytes <= b_bytes + (N // tn) * a_bytes

    def ij(g0, g1):
        return (g0, g1) if i_outer else (g1, g0)

    fused_norm = norm_bwd is not None
    if fused_norm:
        assert tn == N and out_shards == 1 and not has_add and hosted is None, name
        i_outer = True

    def body(*refs):
        a_ref, b_ref = refs[:2]
        add_ref = refs[2] if has_add else None
        o_ref = refs[3 if has_add else 2]
        part = lax.dot_general(a_ref[...].astype(BF16), b_ref[...].astype(BF16), dn, preferred_element_type=F32)

        def finish(r):
            if fused_norm:
                h_ref, g_ref, dh_ref, o32_ref, o16_ref, dg_ref = refs[2:8]
                _, vjp = jax.vjp(_rms, h_ref[...], g_ref[...])
                dx, dg = vjp(r)
                out = dx + dh_ref[...]
                o32_ref[...] = out
                o16_ref[...] = out.astype(BF16)
                dg_ref[...] += dg
                return
            if has_add:
                r = r + add_ref[...]
            o_ref[...] = r.astype(out_dtype)

        if fused_norm:
            @pl.when((pl.program_id(0) == 0) & (pl.program_id(2) == 0))
            def _():
                refs[7][...] = jnp.zeros_like(refs[7])

        if nk == 1:
            finish(part)
            return
        acc = refs[-1]
        k = pl.program_id(2)

        @pl.when(k == 0)
        def _():
            acc[...] = part

        @pl.when(k > 0)
        def _():
            acc[...] += part

        @pl.when(k == nk - 1)
        def _():
            finish(acc[...])

    def spec(block, fn):
        return pl.BlockSpec(block, lambda g0, g1, k: fn(*ij(g0, g1), k))

    if mode == "tn":
        a_spec = spec((tk, tm), lambda i, j, k: (k, i))
    else:
        a_spec = spec((tm, tk), lambda i, j, k: (i, k))
    if mode == "nt":
        if b_shards > 1:
            b_spec = spec((None, tn, tk), lambda i, j, k: (b_shard0 + k // nbk, j, k % nbk))
        else:
            b_spec = spec((tn, tk), lambda i, j, k: (j, k))
    elif b_shards > 1:
        b_spec = spec((None, tk, tn), lambda i, j, k: (j // nbn, k, j % nbn))
    else:
        b_spec = spec((tk, tn), lambda i, j, k: (k, j))
    in_specs = [a_spec, b_spec]
    args = [a, b]
    if has_add:
        in_specs.append(spec((tm, tn), lambda i, j, k: (i, j)))
        args.append(add)
    if out_shards > 1:
        out_spec = spec((None, tm, tn), lambda i, j, k: (j // nbn, i, j % nbn))
        out_shape = jax.ShapeDtypeStruct((out_shards, M, n_sh), out_dtype)
    else:
        out_spec = spec((tm, tn), lambda i, j, k: (i, j))
        out_shape = jax.ShapeDtypeStruct((M, N), out_dtype)
    gi, gj = M // tm, N // tn
    if fused_norm:
        h, gain, dh = norm_bwd
        row_blk = spec((tm, N), lambda i, j, k: (i, 0))
        gain_blk = spec((1, N), lambda i, j, k: (0, 0))
        return _pcall(body, name=name, grid=(gi, 1, nk), in_specs=in_specs + [row_blk, gain_blk, row_blk],
                      out_specs=[row_blk, row_blk, gain_blk],
                      out_shape=[jax.ShapeDtypeStruct((M, N), F32), jax.ShapeDtypeStruct((M, N), BF16),
                                 jax.ShapeDtypeStruct((1, N), F32)],
                      scratch_shapes=[pltpu.VMEM((tm, tn), F32)] if nk > 1 else [],
                      dims=("arbitrary", "arbitrary", "arbitrary"))(a, b, h, gain, dh)
    return _pcall(body, name=name, out_shape=out_shape, grid=(gi, gj, nk) if i_outer else (gj, gi, nk),
                  in_specs=in_specs, out_specs=out_spec,
                  scratch_shapes=[pltpu.VMEM((tm, tn), F32)] if nk > 1 else [],
                  dims=("parallel", "parallel", "arbitrary"), hosted=hosted)(*args)


def _row_spec(tr, w, c):
    return pl.BlockSpec((tr, w), lambda i: (i, c))


def _full_spec(arr):
    return pl.BlockSpec(arr.shape, lambda i: (0,) * arr.ndim)


def rowwise(fn, rows, params, nd_rows, outs, *, name, tr=256):
    S = rows[0][0].shape[0]
    tr = min(tr, S)
    n_in = len(rows) + len(params) + len(nd_rows)

    def body(*refs):
        res = fn(*[x[...] for x in refs[:n_in]])
        for o_ref, v in zip(refs[n_in:], res):
            o_ref[...] = v.astype(o_ref.dtype)

    return _pcall(body, name=name, grid=(S // tr,),
                  in_specs=([_row_spec(tr, w, c) for (_, w, c) in rows] + [_full_spec(q) for q in params]
                            + [_row_spec(tr, w, c) for (_, w, c) in nd_rows]),
                  out_specs=[_row_spec(tr, w, 0) for (w, _) in outs],
                  out_shape=[jax.ShapeDtypeStruct((S, w), dt) for (w, dt) in outs],
                  dims=("parallel",))(*[r[0] for r in rows], *params, *[r[0] for r in nd_rows])


def rowwise_bwd(fn, rows, params, nd_rows, cts, d_dtypes, *, name, adds=None, tr=256):
    S = rows[0][0].shape[0]
    tr = min(tr, S)
    n_r, n_p, n_n, n_c = len(rows), len(params), len(nd_rows), len(cts)
    adds = adds or [None] * n_r
    add_list = [a for a in adds if a is not None]
    n_a = len(add_list)
    d_dtypes = [dt if isinstance(dt, (list, tuple)) else (dt,) for dt in d_dtypes]
    n_d = sum(len(dt) for dt in d_dtypes)

    def body(*refs):
        it = iter(refs)
        r = [next(it)[...] for _ in range(n_r)]
        p = [next(it)[...] for _ in range(n_p)]
        nd = [next(it)[...] for _ in range(n_n)]
        c = [next(it)[...] for _ in range(n_c)]
        ad = [next(it)[...] for _ in range(n_a)]
        d_row_refs = [[next(it) for _ in dts] for dts in d_dtypes]
        d_par_refs = [next(it) for _ in range(n_p)]
        outs, vjp = jax.vjp(lambda *dp: fn(*dp, *nd), *r, *p)
        g = vjp(tuple(ci.astype(o.dtype) for ci, o in zip(c, outs)))
        ai = 0
        for k in range(n_r):
            gk = g[k].astype(F32)
            if adds[k] is not None:
                gk = gk + ad[ai].astype(F32)
                ai += 1
            for ref in d_row_refs[k]:
                ref[...] = gk.astype(ref.dtype)

        @pl.when(pl.program_id(0) == 0)
        def _():
            for ref in d_par_refs:
                ref[...] = jnp.zeros_like(ref)

        for k in range(n_p):
            d_par_refs[k][...] += g[n_r + k].astype(F32)

    in_specs = ([_row_spec(tr, w, c) for (_, w, c) in rows] + [_full_spec(q) for q in params]
                + [_row_spec(tr, w, c) for (_, w, c) in nd_rows] + [_row_spec(tr, w, c) for (_, w, c) in cts]
                + [_row_spec(tr, w, c) for (_, w, c) in add_list])
    out_specs = ([_row_spec(tr, w, 0) for (_, w, _), dts in zip(rows, d_dtypes) for _ in dts]
                 + [_full_spec(q) for q in params])
    out_shape = ([jax.ShapeDtypeStruct((S, w), dt) for (_, w, _), dts in zip(rows, d_dtypes) for dt in dts]
                 + [jax.ShapeDtypeStruct(q.shape, F32) for q in params])
    res = _pcall(body, name=name, grid=(S // tr,), in_specs=in_specs, out_specs=out_specs, out_shape=out_shape,
                 dims=("arbitrary",))(*[r[0] for r in rows], *params, *[r[0] for r in nd_rows],
                                      *[r[0] for r in cts], *[r[0] for r in add_list])
    d_rows, i = [], 0
    for dts in d_dtypes:
        d_rows.append(res[i] if len(dts) == 1 else tuple(res[i:i + len(dts)]))
        i += len(dts)
    return d_rows, res[n_d:]


def _rms(x, g):
    x = x.astype(F32)
    return x * lax.rsqrt(jnp.mean(x * x, axis=-1, keepdims=True) + NORM_EPS) * g


def _fn_rms(x, g):
    return (_rms(x, g),)


def _sigmoid(x):
    return 1.0 / (1.0 + jnp.exp(-x))


def _silu(x):
    return x * _sigmoid(x)


def _softplus(x):
    return jnp.maximum(x, 0.0) + jnp.log(1.0 + jnp.exp(-jnp.abs(x)))


def _fn_ple(h, pp, gl):
    return (h.astype(F32) + pp.astype(F32) * _sigmoid(gl.astype(F32)),)


def _fn_ple_terms(pp, gl):
    return (pp.astype(F32) * _sigmoid(gl.astype(F32)),)


def _rot_half_matrix():
    half = MLA_ROPE // 2
    r = lax.broadcasted_iota(jnp.int32, (LANES, LANES), 0)
    c = lax.broadcasted_iota(jnp.int32, (LANES, LANES), 1)
    plus = (c == r + half) & (r < half)
    minus = (r == c + half) & (c < half)
    return jnp.where(plus, 1.0, 0.0) - jnp.where(minus, 1.0, 0.0)


def _rope_pad(x, cosp, sinp):
    return x * cosp + jnp.dot(x, _rot_half_matrix(), precision=HI, preferred_element_type=F32) * sinp


def _fn_mla_pre(cq, ckv, kr, qn_w, kvn_w, cosp, sinp):
    return (_rms(cq, qn_w), _rms(ckv, kvn_w), _rope_pad(kr.astype(F32), cosp, sinp))


def _fn_rope_q(q, cosp, sinp):
    q = q.astype(F32)
    parts = []
    for h in range(MLA_HEADS):
        base = 2 * LANES * h
        parts.append(q[:, base:base + LANES])
        parts.append(_rope_pad(q[:, base + LANES:base + 2 * LANES], cosp, sinp))
    return (jnp.concatenate(parts, axis=1),)


def _fn_ret_gate(g, on, w):
    return (_silu(g.astype(F32)) * (on.astype(F32) * w),)


def _shift_down(cur, halo, s):
    if s == 0:
        return cur
    r = pltpu.roll(cur, s, 0)
    hs = pltpu.roll(halo, s, 0)
    row = lax.broadcasted_iota(jnp.int32, hs.shape, 0)
    first = jnp.where(row < s, hs, r[:SUBLANES])
    return jnp.concatenate([first, r[SUBLANES:]], axis=0)


def _shift_up(cur, halo, s):
    if s == 0:
        return cur
    n = cur.shape[0]
    r = pltpu.roll(cur, n - s, 0)
    hs = pltpu.roll(halo, SUBLANES - s, 0)
    row = lax.broadcasted_iota(jnp.int32, hs.shape, 0)
    last = jnp.where(row >= SUBLANES - s, hs, r[n - SUBLANES:])
    return jnp.concatenate([r[:n - SUBLANES], last], axis=0)


def _prev_halo_spec(tr, tw, col):
    return pl.BlockSpec((SUBLANES, tw), lambda c, i: (jnp.maximum(i * (tr // SUBLANES) - 1, 0), col(c)))


def _conv_taps(cur, halo, w_ref, width):
    taps = [_shift_down(cur, halo, width - 1 - j) for j in range(width)]
    y = taps[0] * w_ref[0:1, :]
    for j in range(1, width):
        y = y + taps[j] * w_ref[j:j + 1, :]
    return y, taps


def gdn_conv_fwd(zin, w, *, tr=512, tw=512):
    S = zin.shape[0]
    tr = min(tr, S)
    width, C = w.shape

    def body(cur_ref, halo_ref, w_ref, o_ref):
        i = pl.program_id(1)
        halo = halo_ref[...] * (i > 0).astype(F32)
        y, _ = _conv_taps(cur_ref[...], halo, w_ref, width)
        o_ref[...] = _silu(y)

    return _pcall(body, name="gdn_conv_fwd", grid=(C // tw, S // tr),
                  in_specs=[pl.BlockSpec((tr, tw), lambda c, i: (i, c)), _prev_halo_spec(tr, tw, lambda c: c),
                            pl.BlockSpec((width, tw), lambda c, i: (0, c))],
                  out_specs=pl.BlockSpec((tr, tw), lambda c, i: (i, c)),
                  out_shape=jax.ShapeDtypeStruct((S, C), F32), dims=("parallel", "arbitrary"))(zin, zin, w)


def gdn_conv_bwd(zin, w, dy, *, tr=512, tw=512):
    S = zin.shape[0]
    tr = min(tr, S)
    width, C = w.shape

    def body(cur_ref, halo_ref, w_ref, dy_ref, da_ref, dw_ref):
        i = pl.program_id(1)
        halo = halo_ref[...] * (i > 0).astype(F32)
        y, taps = _conv_taps(cur_ref[...], halo, w_ref, width)
        sg = _sigmoid(y)
        da = dy_ref[...] * (sg * (1.0 + y * (1.0 - sg)))
        da_ref[...] = da

        @pl.when(i == 0)
        def _():
            dw_ref[...] = jnp.zeros_like(dw_ref)

        for j in range(width):
            dw_ref[j:j + 1, :] += jnp.sum(da * taps[j], axis=0, keepdims=True)

    return _pcall(body, name="gdn_conv_bwd", grid=(C // tw, S // tr),
                  in_specs=[pl.BlockSpec((tr, tw), lambda c, i: (i, c)), _prev_halo_spec(tr, tw, lambda c: c),
                            pl.BlockSpec((width, tw), lambda c, i: (0, c)),
                            pl.BlockSpec((tr, tw), lambda c, i: (i, c))],
                  out_specs=[pl.BlockSpec((tr, tw), lambda c, i: (i, c)),
                             pl.BlockSpec((width, tw), lambda c, i: (0, c))],
                  out_shape=[jax.ShapeDtypeStruct((S, C), F32), jax.ShapeDtypeStruct((width, C), F32)],
                  dims=("parallel", "arbitrary"))(zin, zin, w, dy)


def conv_transpose(dy, w, out_dtype, *, name, tr=512, tw=512):
    if dy.ndim == 2:
        dy = dy[None]
    T, S, C = dy.shape
    tr = min(tr, S)
    width = w.shape[0]
    n_i, nc = S // tr, C // tw

    def body(cur_ref, halo_ref, w_ref, o_ref):
        i = pl.program_id(2)
        halo = halo_ref[...] * (i < n_i - 1).astype(F32)
        cur = cur_ref[...]
        acc = cur * w_ref[width - 1:width, :]
        for s in range(1, width):
            acc = acc + _shift_up(cur, halo, s) * w_ref[width - 1 - s:width - s, :]
        o_ref[...] = acc.astype(out_dtype)

    nxt = pl.BlockSpec((None, SUBLANES, tw),
                       lambda t, c, i: (t, jnp.minimum((i + 1) * (tr // SUBLANES), S // SUBLANES - 1), c))
    return _pcall(body, name=name, grid=(T, nc, n_i),
                  in_specs=[pl.BlockSpec((None, tr, tw), lambda t, c, i: (t, i, c)), nxt,
                            pl.BlockSpec((width, tw), lambda t, c, i: (0, t * nc + c))],
                  out_specs=pl.BlockSpec((tr, tw), lambda t, c, i: (i, t * nc + c)),
                  out_shape=jax.ShapeDtypeStruct((S, T * C), out_dtype),
                  dims=("parallel", "parallel", "arbitrary"))(dy, dy, w)


def ffn_conv_fwd(u, w, b, *, tr=1024, tw=256):
    S, C2 = u.shape
    tr = min(tr, S)
    width = w.shape[0]
    half = C2 // 2
    nc = half // tw

    def body(g_ref, gh_ref, u_ref, uh_ref, wg_ref, wu_ref, bg_ref, bu_ref, o_ref):
        i = pl.program_id(1)
        live = (i > 0).astype(F32)
        yg, _ = _conv_taps(g_ref[...], gh_ref[...] * live, wg_ref, width)
        yu, _ = _conv_taps(u_ref[...], uh_ref[...] * live, wu_ref, width)
        o_ref[...] = (_silu(yg + bg_ref[...]) * (yu + bu_ref[...])).astype(o_ref.dtype)

    return _pcall(body, name="ffn_conv_fwd", grid=(nc, S // tr),
                  in_specs=[pl.BlockSpec((tr, tw), lambda c, i: (i, c)), _prev_halo_spec(tr, tw, lambda c: c),
                            pl.BlockSpec((tr, tw), lambda c, i: (i, c + nc)),
                            _prev_halo_spec(tr, tw, lambda c: c + nc),
                            pl.BlockSpec((width, tw), lambda c, i: (0, c)),
                            pl.BlockSpec((width, tw), lambda c, i: (0, c + nc)),
                            pl.BlockSpec((1, tw), lambda c, i: (0, c)), pl.BlockSpec((1, tw), lambda c, i: (0, c + nc))],
                  out_specs=pl.BlockSpec((tr, tw), lambda c, i: (i, c)),
                  out_shape=jax.ShapeDtypeStruct((S, half), BF16),
                  dims=("parallel", "arbitrary"))(u, u, u, u, w, w, b, b)


def ffn_conv_bwd(u, w, b, df, *, tr=512, tw=512):
    S, C2 = u.shape
    tr = min(tr, S)
    width = w.shape[0]
    half = C2 // 2
    nc = half // tw
    n_i = S // tr

    def body(g_ref, gh_ref, u_ref, uh_ref, wg_ref, wu_ref, bg_ref, bu_ref, df_ref, dug_ref, duu_ref, dw_ref, db_ref,
             carry):
        i = pl.program_id(1)
        live = (i < n_i - 1).astype(F32)
        yg, gt = _conv_taps(g_ref[...], gh_ref[...] * live, wg_ref, width)
        yu, ut = _conv_taps(u_ref[...], uh_ref[...] * live, wu_ref, width)
        yg = yg + bg_ref[...]
        yu = yu + bu_ref[...]
        sg = _sigmoid(yg)
        dfv = df_ref[...].astype(F32)
        dcs = (dfv * yu * (sg * (1.0 + yg * (1.0 - sg))), dfv * (yg * sg))

        @pl.when(i == 0)
        def _():
            dw_ref[...] = jnp.zeros_like(dw_ref)
            db_ref[...] = jnp.zeros_like(db_ref)
            carry[...] = jnp.zeros_like(carry)

        for t, (dc, taps, w_ref, du_ref) in enumerate(zip(dcs, (gt, ut), (wg_ref, wu_ref), (dug_ref, duu_ref))):
            halo = carry[t]
            du = dc * w_ref[width - 1:width, :]
            for s in range(1, width):
                du = du + _shift_up(dc, halo, s) * w_ref[width - 1 - s:width - s, :]
            du_ref[...] = du.astype(du_ref.dtype)
            carry[t] = dc[:SUBLANES]
            db_ref[t] += jnp.sum(dc, axis=0, keepdims=True)
            for j in range(width):
                dw_ref[t, j:j + 1, :] += jnp.sum(dc * taps[j], axis=0, keepdims=True)

    def prev(col):
        return pl.BlockSpec((SUBLANES, tw),
                            lambda c, i: (jnp.maximum((n_i - 1 - i) * (tr // SUBLANES) - 1, 0), col(c)))

    rows = lambda col: pl.BlockSpec((tr, tw), lambda c, i: (n_i - 1 - i, col(c)))
    du_g, du_u, dw, db = _pcall(
        body, name="ffn_conv_bwd", grid=(nc, n_i),
        in_specs=[rows(lambda c: c), prev(lambda c: c), rows(lambda c: c + nc), prev(lambda c: c + nc),
                  pl.BlockSpec((width, tw), lambda c, i: (0, c)), pl.BlockSpec((width, tw), lambda c, i: (0, c + nc)),
                  pl.BlockSpec((1, tw), lambda c, i: (0, c)), pl.BlockSpec((1, tw), lambda c, i: (0, c + nc)),
                  rows(lambda c: c)],
        out_specs=[rows(lambda c: c), rows(lambda c: c), pl.BlockSpec((2, width, tw), lambda c, i: (0, 0, c)),
                   pl.BlockSpec((2, 1, tw), lambda c, i: (0, 0, c))],
        out_shape=[jax.ShapeDtypeStruct((S, half), BF16), jax.ShapeDtypeStruct((S, half), BF16),
                   jax.ShapeDtypeStruct((2, width, half), F32), jax.ShapeDtypeStruct((2, 1, half), F32)],
        scratch_shapes=[pltpu.VMEM((2, SUBLANES, tw), F32)],
        dims=("arbitrary", "arbitrary"))(u, u, u, u, w, w, b, b, df)
    return du_g, du_u, jnp.concatenate([dw[0], dw[1]], axis=1), jnp.concatenate([db[0], db[1]], axis=1)


_MODE_OF = {v: k for k, v in _DN.items()}


def _bf16_dot(a, b, mode):
    return lax.dot_general(a.astype(BF16), b.astype(BF16), _DN[mode], preferred_element_type=F32)


@functools.partial(jax.custom_vjp, nondiff_argnums=(2,))
def _bdot_mode(a, b, mode):
    return _bf16_dot(a, b, mode)


def _bdot_fwd(a, b, mode):
    return _bf16_dot(a, b, mode), (a, b)


def _bdot_bwd(mode, res, ct):
    a, b = res
    if mode == "nn":
        da, db = _bf16_dot(ct, b, "nt"), _bf16_dot(a, ct, "tn")
    elif mode == "nt":
        da, db = _bf16_dot(ct, b, "nn"), _bf16_dot(ct, a, "tn")
    else:
        da, db = _bf16_dot(b, ct, "nt"), _bf16_dot(a, ct, "nn")
    return da.astype(a.dtype), db.astype(b.dtype)


_bdot_mode.defvjp(_bdot_fwd, _bdot_bwd)


def _bdot(a, b, dn=_DN["nn"]):
    return _bdot_mode(a, b, _MODE_OF[dn])


def _hi_lo(x):
    hi = x.astype(BF16)
    return hi, (x - hi.astype(F32)).astype(BF16)


def _dot3_raw(a, b, mode):
    a1, a2 = _hi_lo(a)
    b1, b2 = _hi_lo(b)
    dot = lambda p, q: lax.dot_general(p, q, _DN[mode], preferred_element_type=F32)
    return dot(a1, b1) + (dot(a1, b2) + dot(a2, b1))


@functools.partial(jax.custom_vjp, nondiff_argnums=(2,))
def _dot3(a, b, mode="nn"):
    return _dot3_raw(a, b, mode)


def _dot3_fwd(a, b, mode):
    return _dot3_raw(a, b, mode), (a, b)


def _dot3_bwd(mode, res, ct):
    a, b = res
    if mode == "nn":
        return _dot3_raw(ct, b, "nt"), _dot3_raw(a, ct, "tn")
    if mode == "nt":
        return _dot3_raw(ct, b, "nn"), _dot3_raw(ct, a, "tn")
    return _dot3_raw(b, ct, "nt"), _dot3_raw(a, ct, "nn")


_dot3.defvjp(_dot3_fwd, _dot3_bwd)


@functools.partial(jax.custom_vjp, nondiff_argnums=(2,))
def _gdot(a, b, mode="nn"):
    return _bf16_dot(a, b, mode)


def _gdot_fwd(a, b, mode):
    return _bf16_dot(a, b, mode), (a.astype(BF16), b.astype(BF16))


def _ct_dot(p, q, mode, ct_first):
    ct, r = (p, q) if ct_first else (q, p)
    c1, c2 = _hi_lo(ct)
    dot = lambda c: lax.dot_general(*((c, r) if ct_first else (r, c)), _DN[mode], preferred_element_type=F32)
    return dot(c1) + dot(c2)


def _gdot_bwd(mode, res, ct):
    a, b = res
    if mode == "nn":
        return _ct_dot(ct, b, "nt", True), _ct_dot(a, ct, "tn", False)
    if mode == "nt":
        return _ct_dot(ct, b, "nn", True), _ct_dot(ct, a, "tn", True)
    return _ct_dot(b, ct, "nt", False), _ct_dot(a, ct, "nn", False)


_gdot.defvjp(_gdot_fwd, _gdot_bwd)


def _split_dot(ones, x):
    x1 = x.astype(BF16)
    r1 = x - x1.astype(F32)
    x2 = r1.astype(BF16)
    x3 = (r1 - x2.astype(F32)).astype(BF16)
    m = ones.astype(BF16)
    dot = lambda p: lax.dot_general(m, p, _DN["nn"], preferred_element_type=F32)
    return dot(x1) + dot(x2) + dot(x3)


@jax.custom_vjp
def _tri_cumsum(x, lower, upper):
    return _split_dot(lower, x)


def _tri_cumsum_fwd(x, lower, upper):
    return _split_dot(lower, x), (lower, upper)


def _tri_cumsum_bwd(res, ct):
    lower, upper = res
    return _split_dot(upper, ct), jnp.zeros_like(lower), jnp.zeros_like(upper)


_tri_cumsum.defvjp(_tri_cumsum_fwd, _tri_cumsum_bwd)


def _tri_masks(n):
    r = lax.broadcasted_iota(jnp.int32, (n, n), 0)
    c = lax.broadcasted_iota(jnp.int32, (n, n), 1)
    return r >= c, r > c


def _gdn_chunk(q, k, v, z, ab, a_row, dt_row, norm_w, state, sel_a, sel_b):
    C = q.shape[0]
    incl, strict = _tri_masks(C)
    lower = jnp.where(incl, 1.0, 0.0)
    qn = q * lax.rsqrt(jnp.sum(q * q, axis=-1, keepdims=True) + NORM_EPS) * (GDN_DK ** -0.5)
    kn = k * lax.rsqrt(jnp.sum(k * k, axis=-1, keepdims=True) + NORM_EPS)
    g = jnp.sum(-jnp.exp(a_row) * _softplus(ab + dt_row) * sel_a, axis=-1, keepdims=True)
    beta = jnp.sum(_sigmoid(ab) * sel_b, axis=-1, keepdims=True)
    gb = jnp.broadcast_to(g, (C, C))
    g_col = _tri_cumsum(gb, lower, jnp.where(strict, 0.0, 1.0))
    g_row = g_col.T
    g_last = jnp.sum(gb, axis=0, keepdims=True)
    gamma = jnp.where(incl, jnp.exp(jnp.where(incl, g_col - g_row, 0.0)), 0.0)
    e_col = jnp.exp(g_col)
    kb = kn * beta
    a_mat = jnp.where(strict, _gdot(kb, kn, "nt") * gamma, 0.0)
    x = jnp.concatenate([v * beta, kb * e_col], axis=1)
    pw = -a_mat
    steps = int(math.log2(C))
    for it in range(steps):
        x = x + _dot3(pw, x, "nn")
        if it < steps - 1:
            pw = _dot3(pw, pw, "nn")
    u, w = x[:, :GDN_DV], x[:, GDN_DV:]
    attn = _gdot(qn, kn, "nt") * gamma
    q_dec = qn * e_col
    k_dec = kn * jnp.exp(g_last - g_col)
    v_new = u - _gdot(w, state, "nn")
    o = _gdot(q_dec, state, "nn") + _gdot(attn, v_new, "nn")
    state_new = state * jnp.exp(jnp.broadcast_to(g_last, state.shape)) + _gdot(k_dec, v_new, "tn")
    y = _rms(o, norm_w) * _silu(z)
    return y, state_new


def _head_selectors(h):
    lane = lax.broadcasted_iota(jnp.int32, (1, LANES), 1)
    return jnp.where(lane == h, 1.0, 0.0), jnp.where(lane == h + GDN_HEADS, 1.0, 0.0)


GDN_HPS = 4
GDN_W = GDN_HPS * LANES


def _gdn_in_specs(rev, nc):
    def n_(n):
        return nc - 1 - n if rev else n
    G = GDN_HEADS // GDN_HPS
    blk = lambda off: pl.BlockSpec((CHUNK, GDN_W), lambda n, h: (n_(n), off + h))
    row = pl.BlockSpec((1, LANES), lambda n, h: (0, 0))
    return n_, [blk(0), blk(G), blk(2 * G), blk(ZIN_Z // GDN_W),
                pl.BlockSpec((CHUNK, LANES), lambda n, h: (n_(n), ZIN_AB // LANES)), row, row, row]


def _lanes(ref, j):
    return ref[:, j * LANES:(j + 1) * LANES]


def _hosting(call, hosted, *args):
    res = call(*args)
    return res if hosted is not None else (res, [])


def gdn_fwd(qkv, zin, a_row, dt_row, norm_w, hosted=None):
    S = qkv.shape[0]
    nc = S // CHUNK
    H = GDN_HEADS
    _, in_specs = _gdn_in_specs(False, nc)

    def body(q_ref, k_ref, v_ref, z_ref, ab_ref, a_ref, dt_ref, nw_ref, y_ref, st_ref, state):
        n, g = pl.program_id(0), pl.program_id(1)
        @pl.when((n == 0) & (g == 0))
        def _():
            state[...] = jnp.zeros_like(state)

        res = []
        for j in range(GDN_HPS):
            h = g * GDN_HPS + j
            st = state[h]
            sel_a, sel_b = _head_selectors(h)
            res.append((st,) + _gdn_chunk(_lanes(q_ref, j), _lanes(k_ref, j), _lanes(v_ref, j), _lanes(z_ref, j),
                                          ab_ref[...], a_ref[...], dt_ref[...], nw_ref[...], st, sel_a, sel_b))
        for j, (st, y, st_new) in enumerate(res):
            st_ref[j] = st
            y_ref[:, j * LANES:(j + 1) * LANES] = y.astype(y_ref.dtype)
            state[g * GDN_HPS + j] = st_new

    call = _pcall(body, name="gdn_fwd", grid=(nc, H // GDN_HPS), in_specs=in_specs,
                  out_specs=[pl.BlockSpec((CHUNK, GDN_W), lambda n, h: (n, h)),
                             pl.BlockSpec((GDN_HPS, None, GDN_DK, GDN_DV), lambda n, h: (h, n, 0, 0))],
                  out_shape=[jax.ShapeDtypeStruct((S, H * GDN_DV), BF16),
                             jax.ShapeDtypeStruct((H, nc, GDN_DK, GDN_DV), F32)],
                  scratch_shapes=[pltpu.VMEM((H, GDN_DK, GDN_DV), F32)],
                  dims=("arbitrary", "arbitrary"), hosted=hosted)
    return _hosting(call, hosted, qkv, qkv, qkv, zin, zin, a_row, dt_row, norm_w)


def gdn_bwd(qkv, zin, a_row, dt_row, norm_w, states, dy, dy_col0, hosted=None):
    S = qkv.shape[0]
    nc = S // CHUNK
    H = GDN_HEADS
    n_, in_specs = _gdn_in_specs(True, nc)
    assert dy_col0 % GDN_HPS == 0
    in_specs = in_specs + [pl.BlockSpec((GDN_HPS, None, GDN_DK, GDN_DV), lambda n, h: (h, n_(n), 0, 0)),
                           pl.BlockSpec((CHUNK, GDN_W), lambda n, h: (n_(n), dy_col0 // GDN_HPS + h))]

    def body(q_ref, k_ref, v_ref, z_ref, ab_ref, a_ref, dt_ref, nw_ref, st_ref, dy_ref,
             dq_ref, dk_ref, dv_ref, dz_ref, dab_ref, da_ref, ddt_ref, dnw_ref, dstate):
        n, g = pl.program_id(0), pl.program_id(1)

        @pl.when((n == 0) & (g == 0))
        def _():
            da_ref[...] = jnp.zeros_like(da_ref)
            ddt_ref[...] = jnp.zeros_like(ddt_ref)
            dnw_ref[...] = jnp.zeros_like(dnw_ref)
            dstate[...] = jnp.zeros_like(dstate)

        @pl.when(g == 0)
        def _():
            dab_ref[...] = jnp.zeros_like(dab_ref)

        res = []
        for j in range(GDN_HPS):
            h = g * GDN_HPS + j
            sel_a, sel_b = _head_selectors(h)
            _, vjp = jax.vjp(lambda *a, sa=sel_a, sb=sel_b: _gdn_chunk(*a, sa, sb), _lanes(q_ref, j), _lanes(k_ref, j),
                             _lanes(v_ref, j), _lanes(z_ref, j), ab_ref[...], a_ref[...], dt_ref[...], nw_ref[...],
                             st_ref[j])
            res.append(vjp((_lanes(dy_ref, j).astype(F32), dstate[h])))
        for j, (dq, dk, dv, dz, dab, da, ddt, dnw, dst) in enumerate(res):
            cols = slice(j * LANES, (j + 1) * LANES)
            dq_ref[:, cols] = dq
            dk_ref[:, cols] = dk
            dv_ref[:, cols] = dv
            dz_ref[:, cols] = dz.astype(dz_ref.dtype)
            dstate[g * GDN_HPS + j] = dst
        dab_ref[...] += sum(r[4] for r in res)
        da_ref[...] += sum(r[5] for r in res)
        ddt_ref[...] += sum(r[6] for r in res)
        dnw_ref[...] += sum(r[7] for r in res)

    blk = pl.BlockSpec((CHUNK, GDN_W), lambda n, h: (n_(n), h))
    row = pl.BlockSpec((1, LANES), lambda n, h: (0, 0))
    wide = jax.ShapeDtypeStruct((S, H * LANES), F32)
    call = _pcall(body, name="gdn_bwd", grid=(nc, H // GDN_HPS), in_specs=in_specs,
                  out_specs=[blk, blk, blk, blk, pl.BlockSpec((CHUNK, LANES), lambda n, h: (n_(n), 0)), row, row, row],
                  out_shape=[wide, wide, wide, jax.ShapeDtypeStruct((S, H * LANES), BF16),
                             jax.ShapeDtypeStruct((S, LANES), F32)] + [jax.ShapeDtypeStruct((1, LANES), F32)] * 3,
                  scratch_shapes=[pltpu.VMEM((H, GDN_DK, GDN_DV), F32)],
                  dims=("arbitrary", "arbitrary"), hosted=hosted)
    return _hosting(call, hosted, qkv, qkv, qkv, zin, zin, a_row, dt_row, norm_w, states, dy)


def _rope_full(x, cos, sin):
    x1, x2 = x[:, :RET_DK // 2], x[:, RET_DK // 2:]
    return jnp.concatenate([x1 * cos - x2 * sin, x2 * cos + x1 * sin], axis=1)


RET_CHUNK = 512


def _ret_chunk(q, k, v, cos, sin, lg, state):
    C = q.shape[0]
    incl, _ = _tri_masks(C)
    qr = _rope_full(q, cos, sin)
    kr = _rope_full(k, cos, sin) * (RET_DK ** -0.5)
    r = lax.broadcasted_iota(jnp.int32, (C, C), 0)
    c = lax.broadcasted_iota(jnp.int32, (C, C), 1)
    dist = jnp.where(incl, (r - c).astype(F32), 0.0)
    lg1 = lg[:, :1]
    decay = jnp.where(incl, jnp.exp(dist * lg1), 0.0)
    pos = lax.broadcasted_iota(jnp.int32, (C, 1), 0).astype(F32)
    xi = jnp.exp((pos + 1.0) * lg1)
    zeta = jnp.exp((C - 1.0 - pos) * lg1)
    inner = _bdot(_bdot(qr, kr, _DN["nt"]) * decay, v)
    cross = _bdot(qr * xi, state)
    state_new = state * jnp.exp(C * lg1) + _bdot(kr * zeta, v, _DN["tn"])
    o = inner + cross
    mu = jnp.mean(o, axis=-1, keepdims=True)
    var = jnp.mean(jnp.square(o - mu), axis=-1, keepdims=True)
    return (o - mu) * lax.rsqrt(var + NORM_EPS), state_new


def _ret_log_gamma():
    lg = np.log1p(-np.power(2.0, -5.0 - np.arange(RET_HEADS, dtype=np.float64))).astype(np.float32)
    return jnp.asarray(np.broadcast_to(lg[:, None, None], (RET_HEADS, 1, LANES)).copy())


def _ret_in_specs(rev, nc):
    def n_(n):
        return nc - 1 - n if rev else n
    H = RET_HEADS
    return n_, [pl.BlockSpec((RET_CHUNK, RET_DK),lambda n, h: (n_(n), h)),
                pl.BlockSpec((RET_CHUNK, RET_DK),lambda n, h: (n_(n), H + h)),
                pl.BlockSpec((RET_CHUNK, RET_DV),lambda n, h: (n_(n), 2 * H * RET_DK // RET_DV + h)),
                pl.BlockSpec((RET_CHUNK, LANES), lambda n, h: (n_(n), 0)),
                pl.BlockSpec((RET_CHUNK, LANES), lambda n, h: (n_(n), 0)),
                pl.BlockSpec((None, 1, LANES), lambda n, h: (h, 0, 0))]


def ret_fwd(zz, cos, sin, hosted=None):
    S = zz.shape[0]
    nc = S // RET_CHUNK
    H = RET_HEADS
    _, in_specs = _ret_in_specs(False, nc)

    def body(q_ref, k_ref, v_ref, cos_ref, sin_ref, lg_ref, o_ref, st_ref, state):
        n, h = pl.program_id(0), pl.program_id(1)

        @pl.when(n == 0)
        def _():
            state[h] = jnp.zeros((RET_DK, RET_DV), F32)

        st = state[h]
        st_ref[...] = st
        o, st_new = _ret_chunk(q_ref[...], k_ref[...], v_ref[...], cos_ref[...], sin_ref[...], lg_ref[...], st)
        o_ref[...] = o
        state[h] = st_new

    call = _pcall(body, name="ret_fwd", grid=(nc, H), in_specs=in_specs,
                  out_specs=[pl.BlockSpec((RET_CHUNK, RET_DV),lambda n, h: (n, h)),
                             pl.BlockSpec((None, None, RET_DK, RET_DV), lambda n, h: (h, n, 0, 0))],
                  out_shape=[jax.ShapeDtypeStruct((S, H * RET_DV), F32),
                             jax.ShapeDtypeStruct((H, nc, RET_DK, RET_DV), F32)],
                  scratch_shapes=[pltpu.VMEM((H, RET_DK, RET_DV), F32)],
                  dims=("arbitrary", "arbitrary"), hosted=hosted)
    return _hosting(call, hosted, zz, zz, zz, cos, sin, _ret_log_gamma())


def ret_bwd(zz, cos, sin, states, do):
    S = zz.shape[0]
    nc = S // RET_CHUNK
    H = RET_HEADS
    n_, in_specs = _ret_in_specs(True, nc)
    in_specs = in_specs + [pl.BlockSpec((None, None, RET_DK, RET_DV), lambda n, h: (h, n_(n), 0, 0)),
                           pl.BlockSpec((RET_CHUNK, RET_DV),lambda n, h: (n_(n), h))]

    def body(q_ref, k_ref, v_ref, cos_ref, sin_ref, lg_ref, st_ref, do_ref, dq_ref, dk_ref, dv_ref, dstate):
        n, h = pl.program_id(0), pl.program_id(1)

        @pl.when(n == 0)
        def _():
            dstate[h] = jnp.zeros((RET_DK, RET_DV), F32)

        cos, sin, lg = cos_ref[...], sin_ref[...], lg_ref[...]
        _, vjp = jax.vjp(lambda q, k, v, st: _ret_chunk(q, k, v, cos, sin, lg, st),
                         q_ref[...], k_ref[...], v_ref[...], st_ref[...])
        dq, dk, dv, dst = vjp((do_ref[...], dstate[h]))
        dq_ref[...] = dq.astype(dq_ref.dtype)
        dk_ref[...] = dk.astype(dk_ref.dtype)
        dv_ref[...] = dv.astype(dv_ref.dtype)
        dstate[h] = dst

    return _pcall(body, name="ret_bwd", grid=(nc, H), in_specs=in_specs,
                  out_specs=[pl.BlockSpec((RET_CHUNK, RET_DK),lambda n, h: (n_(n), h)),
                             pl.BlockSpec((RET_CHUNK, RET_DK),lambda n, h: (n_(n), h)),
                             pl.BlockSpec((RET_CHUNK, RET_DV),lambda n, h: (n_(n), h))],
                  out_shape=[jax.ShapeDtypeStruct((S, H * RET_DK), BF16), jax.ShapeDtypeStruct((S, H * RET_DK), BF16),
                             jax.ShapeDtypeStruct((S, H * RET_DV), BF16)],
                  scratch_shapes=[pltpu.VMEM((H, RET_DK, RET_DV), F32)],
                  dims=("arbitrary", "arbitrary"))(zz, zz, zz, cos, sin, _ret_log_gamma(), states, do)


MLA_SCALE = (MLA_NOPE + MLA_ROPE) ** -0.5
NEG = -1e30


def _mla_scores(q, kn, kpe, diagonal):
    s = (lax.dot_general(q[:, :LANES], kn, _DN["nt"], preferred_element_type=F32)
         + lax.dot_general(q[:, LANES:], kpe, _DN["nt"], preferred_element_type=F32)) * MLA_SCALE
    if diagonal:
        row = lax.broadcasted_iota(jnp.int32, s.shape, 0)
        col = lax.broadcasted_iota(jnp.int32, s.shape, 1)
        s = jnp.where(col <= row, s, NEG)
    return s


def _on_and_below_diagonal(i, j, step):
    @pl.when(j < i)
    def _():
        step(False)

    @pl.when(j == i)
    def _():
        step(True)


FLASH_T = 1024


def flash_fwd(qr, kv, kpe, *, t=FLASH_T, hosted=None):
    S = qr.shape[0]
    t = min(t, S)
    nb = S // t
    H = MLA_HEADS

    def body(q_ref, kn_ref, v_ref, kpe_ref, o_ref, lse_ref, m_s, l_s, acc):
        i, j = pl.program_id(1), pl.program_id(2)

        @pl.when(j == 0)
        def _():
            m_s[...] = jnp.full_like(m_s, NEG)
            l_s[...] = jnp.zeros_like(l_s)
            acc[...] = jnp.zeros_like(acc)

        def step(diagonal):
            s = _mla_scores(q_ref[...], kn_ref[...], kpe_ref[...], diagonal)
            m_new = jnp.maximum(m_s[...], jnp.max(s, axis=-1, keepdims=True))
            p = jnp.exp(s - m_new)
            alpha = jnp.exp(m_s[...] - m_new)
            l_s[...] = alpha * l_s[...] + jnp.sum(p, axis=-1, keepdims=True)
            acc[...] = alpha * acc[...] + _bdot(p, v_ref[...])
            m_s[...] = m_new

        _on_and_below_diagonal(i, j, step)

        @pl.when(j == nb - 1)
        def _():
            o_ref[...] = (acc[...] / l_s[...]).astype(o_ref.dtype)
            lse_ref[...] = m_s[...] + jnp.log(l_s[...])

    kmap = lambda off: (lambda h, i, j: (jnp.minimum(j, i), off + h))
    call = _pcall(body, name="mla_flash_fwd", grid=(H, nb, nb),
                  in_specs=[pl.BlockSpec((t, 2 * LANES), lambda h, i, j: (i, h)),
                            pl.BlockSpec((t, LANES), kmap(0)), pl.BlockSpec((t, LANES), kmap(H)),
                            pl.BlockSpec((t, LANES), lambda h, i, j: (jnp.minimum(j, i), 0))],
                  out_specs=[pl.BlockSpec((t, LANES), lambda h, i, j: (i, h)),
                             pl.BlockSpec((None, t, 1), lambda h, i, j: (h, i, 0))],
                  out_shape=[jax.ShapeDtypeStruct((S, H * MLA_V), BF16), jax.ShapeDtypeStruct((H, S, 1), F32)],
                  scratch_shapes=[pltpu.VMEM((t, 1), F32), pltpu.VMEM((t, 1), F32), pltpu.VMEM((t, MLA_V), F32)],
                  dims=("parallel", "parallel", "arbitrary"), hosted=hosted)
    return _hosting(call, hosted, qr, kv, kv, kpe)


def _mla_p_ds(q, kn, v, kpe, do, o, lse, diagonal):
    p = jnp.exp(_mla_scores(q, kn, kpe, diagonal) - lse)
    dof = do.astype(F32)
    delta = jnp.sum(dof * o.astype(F32), axis=-1, keepdims=True)
    dp = lax.dot_general(do.astype(BF16), v, _DN["nt"], preferred_element_type=F32)
    ds = p * (dp - delta) * MLA_SCALE
    return p, ds


def flash_bwd(qr, kv, kpe, o, lse, dy, dy_col0, *, t=FLASH_T, hosted=None):
    S = qr.shape[0]
    t = min(t, S)
    nb = S // t
    H = MLA_HEADS

    def body(q_ref, kn_ref, v_ref, kpe_ref, o_ref, lse_ref, do_ref, dq_ref, dkn_ref, dv_ref, dkpe_ref, acc, akn, av):
        h, i, j = pl.program_id(0), pl.program_id(1), pl.program_id(2)

        @pl.when((h == 0) & (i == 0) & (j == 0))
        def _():
            dkpe_ref[...] = jnp.zeros_like(dkpe_ref)

        @pl.when((i == 0) & (j == 0))
        def _():
            akn[...] = jnp.zeros_like(akn)
            av[...] = jnp.zeros_like(av)

        @pl.when(j == 0)
        def _():
            acc[...] = jnp.zeros_like(acc)

        def step(diagonal):
            q = q_ref[...]
            p, ds = _mla_p_ds(q, kn_ref[...], v_ref[...], kpe_ref[...], do_ref[...], o_ref[...], lse_ref[...],
                              diagonal)
            acc[...] += jnp.concatenate([_bdot(ds, kn_ref[...]), _bdot(ds, kpe_ref[...])], axis=1)
            av[j] += _bdot(p, do_ref[...], _DN["tn"])
            akn[j] += _bdot(ds, q[:, :LANES], _DN["tn"])
            rows = pl.ds(pl.multiple_of(j * t, t), t)
            dkpe_ref[rows, :] += _bdot(ds, q[:, LANES:], _DN["tn"])

        _on_and_below_diagonal(i, j, step)

        @pl.when(j == nb - 1)
        def _():
            dq_ref[...] = acc[...]

        @pl.when((i == nb - 1) & (j == nb - 1))
        def _():
            dkn_ref[...] = akn[...].reshape(S, LANES).astype(dkn_ref.dtype)
            dv_ref[...] = av[...].reshape(S, LANES).astype(dv_ref.dtype)

    kmap = lambda off: (lambda h, i, j: (jnp.minimum(j, i), off + h))
    head_col = pl.BlockSpec((S, LANES), lambda h, i, j: (0, h))
    call = _pcall(body, name="mla_flash_bwd", grid=(H, nb, nb),
                  in_specs=[pl.BlockSpec((t, 2 * LANES), lambda h, i, j: (i, h)),
                            pl.BlockSpec((t, LANES), kmap(0)), pl.BlockSpec((t, LANES), kmap(H)),
                            pl.BlockSpec((t, LANES), lambda h, i, j: (jnp.minimum(j, i), 0)),
                            pl.BlockSpec((t, LANES), lambda h, i, j: (i, h)),
                            pl.BlockSpec((None, t, 1), lambda h, i, j: (h, i, 0)),
                            pl.BlockSpec((t, LANES), lambda h, i, j: (i, dy_col0 + h))],
                  out_specs=[pl.BlockSpec((t, 2 * LANES), lambda h, i, j: (i, h)), head_col, head_col,
                             pl.BlockSpec((S, LANES), lambda h, i, j: (0, 0))],
                  out_shape=[jax.ShapeDtypeStruct((S, H * 2 * LANES), F32), jax.ShapeDtypeStruct((S, H * LANES), BF16),
                             jax.ShapeDtypeStruct((S, H * LANES), BF16), jax.ShapeDtypeStruct((S, LANES), F32)],
                  scratch_shapes=[pltpu.VMEM((t, 2 * LANES), F32), pltpu.VMEM((nb, t, LANES), F32),
                                  pltpu.VMEM((nb, t, LANES), F32)],
                  dims=("arbitrary", "arbitrary", "arbitrary"), hosted=hosted)
    (dq, dkn, dv, dkpe), extra = _hosting(call, hosted, qr, kv, kv, kpe, o, lse, dy)
    return (dq, jnp.concatenate([dkn, dv], axis=1), dkpe), extra


def loss_head(h, target, g, *, tr=256):
    S, D = h.shape
    tr = min(tr, S)

    def body(h_ref, t_ref, g_ref, loss_ref, dh_ref, dg_ref):
        tgt = t_ref[...]

        def f(hh, gg):
            err = jnp.square(_rms(hh, gg) - tgt)
            per_row = jnp.sum(err, axis=-1, keepdims=True) * (0.5 / D)
            return jnp.sum(per_row, axis=0, keepdims=True)

        val, vjp = jax.vjp(f, h_ref[...], g_ref[...])
        dh, dg = vjp(jnp.ones((1, 1), F32))
        dh_ref[...] = dh

        @pl.when(pl.program_id(0) == 0)
        def _():
            loss_ref[...] = jnp.zeros_like(loss_ref)
            dg_ref[...] = jnp.zeros_like(dg_ref)

        loss_ref[...] += jnp.broadcast_to(val, loss_ref.shape)
        dg_ref[...] += dg

    return _pcall(body, name="loss_head", grid=(S // tr,),
                  in_specs=[_row_spec(tr, D, 0), _row_spec(tr, D, 0), _full_spec(g)],
                  out_specs=[pl.BlockSpec((1, LANES), lambda i: (0, 0)), _row_spec(tr, D, 0), _full_spec(g)],
                  out_shape=[jax.ShapeDtypeStruct((1, LANES), F32), jax.ShapeDtypeStruct((S, D), F32),
                             jax.ShapeDtypeStruct(g.shape, F32)],
                  dims=("arbitrary",))(h, target, g)


def _rope_tables(positions, dim):
    inv_freq = ROPE_THETA ** (-jnp.arange(0, dim, 2, dtype=F32) / dim)
    ang = positions.astype(F32)[:, None] * inv_freq
    return jnp.cos(ang), jnp.sin(ang)


def _pad_cols(w, n):
    return jnp.pad(w, ((0, 0), (0, n - w.shape[1])))


def _prep_w_in0(w):
    return jnp.concatenate([w[:, :4096], w[:, 4112:5136], _pad_cols(w[:, 5136:5200], LANES),
                            _pad_cols(w[:, 4096:4112], LANES)], axis=1)


def _unprep_w_in0(g):
    return jnp.concatenate([g[:, :4096], g[:, ZIN_AB:ZIN_AB + 16], g[:, ZIN_CQ:ZIN_KR], g[:, ZIN_KR:ZIN_KR + MLA_ROPE]],
                           axis=1)


def _prep_w_uq(w):
    w = w.reshape(MLA_Q_RANK, MLA_HEADS, MLA_NOPE + MLA_ROPE)
    w = jnp.pad(w, ((0, 0), (0, 0), (0, 2 * LANES - MLA_NOPE - MLA_ROPE)))
    return w.reshape(MLA_Q_RANK, MLA_HEADS * 2 * LANES)


def _unprep_w_uq(g):
    g = g.reshape(MLA_Q_RANK, MLA_HEADS, 2 * LANES)[:, :, :MLA_NOPE + MLA_ROPE]
    return g.reshape(MLA_Q_RANK, MLA_HEADS * (MLA_NOPE + MLA_ROPE))


def _prep_w_ukv(w):
    w = w.reshape(MLA_KV_RANK, MLA_HEADS, 2, LANES)
    return jnp.transpose(w, (0, 2, 1, 3)).reshape(MLA_KV_RANK, 2 * MLA_HEADS * LANES)


def _unprep_w_ukv(g):
    g = g.reshape(MLA_KV_RANK, 2, MLA_HEADS, LANES)
    return jnp.transpose(g, (0, 2, 1, 3)).reshape(MLA_KV_RANK, 2 * MLA_HEADS * LANES)


def _row(v, n=None):
    v = v.reshape(1, -1).astype(F32)
    return v if n is None else _pad_cols(v, n)


def _ffn_fwd(h, norm_g, w_up, conv_w, conv_b, w_down, tag, hosted=None):
    (hn,) = rowwise(_fn_rms, [(h, D_MODEL, 0)], [norm_g], [], [(D_MODEL, BF16)], name=f"{tag}_ffn_norm")
    u = matmul(hn, w_up, "nn", F32, b_shards=N_CHIPS, **TILES["wide_nn"], name=f"{tag}_ffn_up", hosted=hosted)
    u, got = u if hosted is not None else (u, [])
    f = ffn_conv_fwd(u, conv_w, conv_b)
    h_out = matmul(f, w_down, "nn", F32, add=h, tm=512, tn=1024, tk=8192, name=f"{tag}_ffn_down")
    return h_out, (hn, u, f), got


def _ffn_bwd(dh, dh16, h, norm_g, w_up, conv_w, conv_b, w_down, saved, tag, make_hosted=None):
    hn, u, f = saved
    df = matmul(dh16, w_down, "nt", BF16, tm=512, tn=2816, name=f"{tag}_ffn_down_dx")
    g_down = matmul(f, dh16, "tn", BF16, **TILES["dw"], name=f"{tag}_ffn_down_dw")
    du_g, du_u, g_conv_w, g_conv_b = ffn_conv_bwd(u, conv_w, conv_b, df)
    pair = N_CHIPS // 2
    g_up = jnp.concatenate([matmul(hn, du_, "tn", BF16, out_shards=pair, tm=1024, tn=1408, tk=4096,
                                   name=f"{tag}_ffn_up_dw_{part}") for part, du_ in (("gate", du_g), ("up", du_u))])
    hosted = make_hosted(dict(ffn_w_up=g_up, ffn_w_down=g_down)) if make_hosted else None
    dhn_gate = matmul(du_g, w_up, "nt", F32, b_shards=pair, tm=512, tn=2048, tk=2816, name=f"{tag}_ffn_up_dx_gate")
    dhn = matmul(du_u, w_up, "nt", F32, b_shards=pair, b_shard0=pair, add=dhn_gate, tm=512, tn=2048, tk=2816,
                 name=f"{tag}_ffn_up_dx_up", hosted=hosted)
    dhn, got = dhn if hosted is not None else (dhn, [])
    ((dh_in, dh_in16),), (g_norm,) = rowwise_bwd(_fn_rms, [(h, D_MODEL, 0)], [norm_g], [], [(dhn, D_MODEL, 0)],
                                                 [(F32, BF16)], adds=[(dh, D_MODEL, 0)], name=f"{tag}_ffn_norm_bwd")
    return dh_in, dh_in16, dict(ffn_norm=g_norm, ffn_w_up=g_up, ffn_conv_w=g_conv_w, ffn_conv_b=g_conv_b,
                                ffn_w_down=g_down), got


TILES = {"wide_nn": dict(tm=512, tn=3072, tk=2048),
         "square": dict(tm=512, tn=2048, tk=2048),
         "dw": dict(tm=512, tn=2048, tk=4096)}


def _ple_fwd(h, p_i, w_proj, gate_g, w_gate, tag):
    (hg,) = rowwise(_fn_rms, [(h, D_MODEL, 0)], [gate_g], [], [(D_MODEL, BF16)], name=f"{tag}_ple_norm")
    gl = matmul(hg, w_gate, "nn", F32, **TILES["square"], name=f"{tag}_ple_gate")
    pp = matmul(p_i, w_proj, "nn", F32, b_shards=N_CHIPS, name=f"{tag}_ple_proj")
    (h_out,) = rowwise(_fn_ple, [(h, D_MODEL, 0), (pp, D_MODEL, 0), (gl, D_MODEL, 0)], [], [], [(D_MODEL, F32)],
                       name=f"{tag}_ple_add")
    return h_out, (hg, gl, pp)


def _ple_bwd(dh, h, p_i, w_proj, gate_g, w_gate, saved, tag):
    hg, gl, pp = saved
    (dpp, dgl), _ = rowwise_bwd(_fn_ple_terms, [(pp, D_MODEL, 0), (gl, D_MODEL, 0)], [], [], [(dh, D_MODEL, 0)],
                                [BF16, BF16], name=f"{tag}_ple_add_bwd")
    g_proj = matmul(p_i, dpp, "tn", BF16, out_shards=N_CHIPS, name=f"{tag}_ple_proj_dw")
    g_gate = matmul(hg, dgl, "tn", BF16, **TILES["dw"], name=f"{tag}_ple_gate_dw")
    dh_in, dh_in16, g_norm = matmul(dgl, w_gate, "nt", F32, tm=256, tn=2048, tk=2048, name=f"{tag}_ple_gate_dx",
                                    norm_bwd=(h, gate_g, dh))
    return dh_in, dh_in16, dict(ple_proj=g_proj, ple_gate_norm=g_norm, ple_gate=g_gate)


def local_step(x, p, positions, target, slots, W, ex=None):
    S = x.shape[0]
    G = {}
    p0, p1 = p[0].astype(BF16), p[1].astype(BF16)
    W = dict(W)

    def use(names, bufs):
        for k, b in zip(names, bufs):
            r, c = b.shape[1:]
            W[k] = b.reshape(N_CHIPS * r, c) if k in ROW_SHARDED else (b if k in KEPT_SHARDED else _cols_to_full(b))

    def by_chip(names):
        out = []
        for k in names:
            r, c = slots[k].shape[1:]
            out.append(G[k].reshape(N_CHIPS, r, c) if k in ROW_SHARDED
                       else (G[k] if k in KEPT_SHARDED else _full_to_cols(G[k])))
        return out

    first = [slots[k] for k in GATHER_FIRST]
    use(GATHER_FIRST, ex.gather(first) if ex else first)

    cm, sm = _rope_tables(positions, MLA_ROPE)
    zeros = jnp.zeros((S, LANES - MLA_ROPE), F32)
    cosp = jnp.concatenate([cm, cm, zeros], axis=1)
    sinp = jnp.concatenate([sm, sm, zeros], axis=1)
    cr, sr = _rope_tables(positions, RET_DK)

    w_in0 = _prep_w_in0(W["l0_w_in"])
    a_row = _row(W["l0_gdn_A_log"], LANES)
    dt_row = _row(W["l0_gdn_dt_bias"], LANES)
    gdn_nw = _row(W["l0_gdn_norm"])
    n = {k: _row(W[k]) for k in ("l0_attn_norm", "l0_mla_q_norm", "l0_mla_kv_norm", "l0_ffn_norm",
                                 "l0_ple_gate_norm", "l1_attn_norm", "l1_ret_norm", "l1_ffn_norm",
                                 "l1_ple_gate_norm", "final_norm", "l0_ffn_conv_b", "l1_ffn_conv_b")}

    (hn0,) = rowwise(_fn_rms, [(x, D_MODEL, 0)], [n["l0_attn_norm"]], [], [(D_MODEL, BF16)], name="l0_attn_norm")
    zin = matmul(hn0, w_in0, "nn", F32, tm=512, tn=1792, name="l0_w_in")
    qkv = gdn_conv_fwd(zin, W["l0_gdn_conv"])
    layer0 = [slots[k] for k in GATHER_L0]
    (y_a, gdn_states), got = gdn_fwd(qkv, zin, a_row, dt_row, gdn_nw, hosted=hosted_gather(layer0) if ex else None)
    use(GATHER_L0, got if ex else layer0)
    w_uq = _prep_w_uq(W["l0_mla_w_uq"])
    w_ukv = _prep_w_ukv(W["l0_mla_w_ukv"])
    mla_rows = [(zin, MLA_Q_RANK, ZIN_CQ // MLA_Q_RANK), (zin, MLA_KV_RANK, ZIN_CKV // MLA_KV_RANK),
                (zin, LANES, ZIN_KR // LANES)]
    mla_nd = [(cosp, LANES, 0), (sinp, LANES, 0)]
    cqn, ckvn, kpe = rowwise(_fn_mla_pre, mla_rows, [n["l0_mla_q_norm"], n["l0_mla_kv_norm"]], mla_nd,
                             [(MLA_Q_RANK, BF16), (MLA_KV_RANK, BF16), (LANES, BF16)], name="mla_pre")
    q_lin = matmul(cqn, w_uq, "nn", F32, name="mla_w_uq")
    kv = matmul(ckvn, w_ukv, "nn", BF16, name="mla_w_ukv")
    (qr,) = rowwise(_fn_rope_q, [(q_lin, 2048, 0)], [], mla_nd, [(2048, BF16)],
                    name="mla_rope_q")
    layer1 = [slots[k] for k in GATHER_L1]
    (y_b, lse), got = flash_fwd(qr, kv, kpe, hosted=hosted_gather(layer1) if ex else None)
    use(GATHER_L1, got if ex else layer1)
    y_ab = jnp.concatenate([y_a, y_b], axis=1)
    h1 = matmul(y_ab, W["l0_w_out"], "nn", F32, add=x, **TILES["square"], name="l0_w_out")
    ffn_late = [slots[k] for k in GATHER_FFN]
    h2, ffn0, got = _ffn_fwd(h1, n["l0_ffn_norm"], W["l0_ffn_w_up"], W["l0_ffn_conv_w"], n["l0_ffn_conv_b"],
                             W["l0_ffn_w_down"], "l0", hosted=hosted_gather(ffn_late) if ex else None)
    use(GATHER_FFN, got if ex else ffn_late)
    h3, ple0 = _ple_fwd(h2, p0, W["l0_ple_proj"], n["l0_ple_gate_norm"], W["l0_ple_gate"], "l0")

    (hn1,) = rowwise(_fn_rms, [(h3, D_MODEL, 0)], [n["l1_attn_norm"]], [], [(D_MODEL, BF16)], name="l1_attn_norm")
    late = [slots[k] for k in GATHER_L1_IN]
    zz = matmul(hn1, W["l1_w_in"], "nn", F32, b_shards=N_CHIPS, **TILES["wide_nn"], name="l1_w_in",
                hosted=hosted_gather(late) if ex else None)
    zz, got = zz if ex else (zz, late)
    use(GATHER_L1_IN, got)
    (o_ret, ret_states), _ = ret_fwd(zz, cr, sr)
    gate_rows = [(zz, 4096, 2), (o_ret, 4096, 0)]
    (yg,) = rowwise(_fn_ret_gate, gate_rows, [n["l1_ret_norm"]], [], [(4096, BF16)], name="ret_gate")
    h4 = matmul(yg, W["l1_w_out"], "nn", F32, add=h3, tm=512, tn=2048, tk=4096, name="l1_w_out")
    h5, ffn1, _ = _ffn_fwd(h4, n["l1_ffn_norm"], W["l1_ffn_w_up"], W["l1_ffn_conv_w"], n["l1_ffn_conv_b"],
                           W["l1_ffn_w_down"], "l1")
    h6, ple1 = _ple_fwd(h5, p1, W["l1_ple_proj"], n["l1_ple_gate_norm"], W["l1_ple_gate"], "l1")

    loss_vec, dh, G["final_norm"] = loss_head(h6, target, n["final_norm"])

    dh, dh16, g = _ple_bwd(dh, h5, p1, W["l1_ple_proj"], n["l1_ple_gate_norm"], W["l1_ple_gate"], ple1, "l1")
    G.update({"l1_" + k: v for k, v in g.items()})
    dh, dh16, g, _ = _ffn_bwd(dh, dh16, h4, n["l1_ffn_norm"], W["l1_ffn_w_up"], W["l1_ffn_conv_w"],
                              n["l1_ffn_conv_b"], W["l1_ffn_w_down"], ffn1, "l1")
    G.update({"l1_" + k: v for k, v in g.items()})

    dyg = matmul(dh16, W["l1_w_out"], "nt", F32, tm=512, tn=4096, name="l1_w_out_dx")
    G["l1_w_out"] = matmul(yg, dh16, "tn", BF16, **TILES["dw"], name="l1_w_out_dw")
    (dg, do_ret), (G["l1_ret_norm"],) = rowwise_bwd(_fn_ret_gate, gate_rows, [n["l1_ret_norm"]], [],
                                                   [(dyg, 4096, 0)], [BF16, F32], name="ret_gate_bwd")
    dq, dk, dv = ret_bwd(zz, cr, sr, ret_states, do_ret)
    dzz = jnp.concatenate([dq, dk, dv, dg], axis=1)
    G["l1_w_in"] = matmul(hn1, dzz, "tn", BF16, out_shards=N_CHIPS, tm=1024, tn=1536, tk=4096, name="l1_w_in_dw")
    sums, landed = {}, {}
    grads_l1 = by_chip(REDUCE_L1)
    dhn = matmul(dzz, W["l1_w_in"], "nt", F32, b_shards=N_CHIPS, tm=1024, tn=1024, tk=3072, name="l1_w_in_dx",
                 hosted=hosted_swap(grads_l1) if ex else None)
    if ex:
        dhn, swapped = dhn
        sums.update(zip(REDUCE_L1, ex.pair_sums(REDUCE_L1, grads_l1, swapped)))
    (dh,), (G["l1_attn_norm"],) = rowwise_bwd(_fn_rms, [(h3, D_MODEL, 0)], [n["l1_attn_norm"]], [],
                                             [(dhn, D_MODEL, 0)], [F32], adds=[(dh, D_MODEL, 0)],
                                             name="l1_attn_norm_bwd")

    dh, dh16, g = _ple_bwd(dh, h2, p0, W["l0_ple_proj"], n["l0_ple_gate_norm"], W["l0_ple_gate"], ple0, "l0")
    G.update({"l0_" + k: v for k, v in g.items()})
    mid = REDUCE_DQ + REDUCE_L0

    def swap_mid(g_ffn):
        G.update({"l0_" + k: v for k, v in g_ffn.items()})
        return hosted_swap(by_chip(mid))

    dh, dh16, g, swapped = _ffn_bwd(dh, dh16, h1, n["l0_ffn_norm"], W["l0_ffn_w_up"], W["l0_ffn_conv_w"],
                                    n["l0_ffn_conv_b"], W["l0_ffn_w_down"], ffn0, "l0",
                                    make_hosted=swap_mid if ex else None)
    G.update({"l0_" + k: v for k, v in g.items()})
    if ex:
        sums.update(zip(mid, ex.pair_sums(mid, by_chip(mid), swapped)))

    dy_ab = matmul(dh16, W["l0_w_out"], "nt", F32, **TILES["square"], name="l0_w_out_dx")
    G["l0_w_out"] = matmul(y_ab, dh16, "tn", BF16, **TILES["dw"], name="l0_w_out_dw")
    (dq, dk, dv, dz, dab, g_a, g_dt, G["l0_gdn_norm"]), got = gdn_bwd(
        qkv, zin, a_row, dt_row, gdn_nw, gdn_states, dy_ab, 0,
        hosted=hosted_scatter([sums[k] for k in REDUCE_L1]) if ex else None)
    landed.update(zip(REDUCE_L1, got))
    G["l0_gdn_A_log"], G["l0_gdn_dt_bias"] = g_a[:, :GDN_HEADS], g_dt[:, :GDN_HEADS]
    dpre, G["l0_gdn_conv"] = gdn_conv_bwd(zin, W["l0_gdn_conv"], jnp.concatenate([dq, dk, dv], axis=1))
    dqkv = conv_transpose(dpre, W["l0_gdn_conv"], BF16, name="gdn_conv_dx")
    (dqr, dkv, dkpe), got = flash_bwd(qr, kv, kpe, y_b, lse, dy_ab, MLA_HEADS,
                                      hosted=hosted_scatter([sums[k] for k in mid]) if ex else None)
    landed.update(zip(mid, got))
    (dq_lin,), _ = rowwise_bwd(_fn_rope_q, [(q_lin, 2048, 0)], [], mla_nd, [(dqr, 2048, 0)], [BF16],
                               name="mla_rope_q_bwd")
    G["l0_mla_w_uq"] = _unprep_w_uq(matmul(cqn, dq_lin, "tn", BF16, name="mla_w_uq_dw"))
    dcqn = matmul(dq_lin, w_uq, "nt", F32, name="mla_w_uq_dx")
    G["l0_mla_w_ukv"] = _unprep_w_ukv(matmul(ckvn, dkv, "tn", BF16, name="mla_w_ukv_dw"))
    dckvn = matmul(dkv, w_ukv, "nt", F32, name="mla_w_ukv_dx")
    (dcq, dckv, dkr), (G["l0_mla_q_norm"], G["l0_mla_kv_norm"]) = rowwise_bwd(
        _fn_mla_pre, mla_rows, [n["l0_mla_q_norm"], n["l0_mla_kv_norm"]], mla_nd,
        [(dcqn, MLA_Q_RANK, 0), (dckvn, MLA_KV_RANK, 0), (dkpe, LANES, 0)], [BF16, BF16, BF16], name="mla_pre_bwd")
    dzin = jnp.concatenate([dqkv, dz, dcq, dckv, dkr, dab.astype(BF16)], axis=1)
    early = REDUCE_L1 + REDUCE_DQ + REDUCE_L0
    reduced = {}
    g_in = matmul(hn0, dzin, "tn", BF16, tm=512, tn=1792, tk=4096, name="l0_w_in_dw",
                  hosted=hosted_join(ex.halves(early, sums, landed)) if ex else None)
    if ex:
        g_in, joined = g_in
        reduced.update(zip(early, joined))
    G["l0_w_in"] = _unprep_w_in0(g_in)
    if ex:
        sums.update(zip(REDUCE_LAST, ex.pair_sums(REDUCE_LAST, by_chip(REDUCE_LAST))))
    dhn = matmul(dzin, w_in0, "nt", F32, tm=512, tn=2048, tk=5376, name="l0_w_in_dx",
                 hosted=hosted_scatter([sums[k] for k in REDUCE_LAST]) if ex else None)
    if ex:
        dhn, got = dhn
        landed.update(zip(REDUCE_LAST, got))
    (grad_x,), (G["l0_attn_norm"],) = rowwise_bwd(_fn_rms, [(x, D_MODEL, 0)], [n["l0_attn_norm"]], [],
                                                 [(dhn, D_MODEL, 0)], [F32], adds=[(dh, D_MODEL, 0)],
                                                 name="l0_attn_norm_bwd")
    small = {k: G[k] for k in SMALL}
    if not ex:
        return loss_vec[0, 0], grad_x, dict(zip(BIG, by_chip(BIG))), small
    reduced.update(zip(REDUCE_LAST, pair_join_halves(ex.halves(REDUCE_LAST, sums, landed))))
    return loss_vec[0, 0], grad_x, reduced, small


HBM = pl.BlockSpec(memory_space=pltpu.HBM)
VMEM = pl.BlockSpec(memory_space=pltpu.VMEM)


def _place():
    return lax.axis_index("x"), lax.axis_index("y"), lax.axis_index("c")


def _other_chips(x, y):
    return [(1 - x, y), (x, 1 - y), (1 - x, 1 - y)]


def _comm_call(body, *, name, out_shape, in_specs, out_specs, scratch_shapes):
    return pl.pallas_call(body, name=name, out_shape=out_shape, in_specs=in_specs, out_specs=out_specs,
                          scratch_shapes=list(scratch_shapes),
                          compiler_params=pltpu.CompilerParams(vmem_limit_bytes=VMEM_LIMIT_MB << 20))


def _inplace_comm_call(body, bufs, *, name, n_sems):
    n = len(bufs)
    return pl.pallas_call(body, name=name, out_shape=[jax.ShapeDtypeStruct(b.shape, b.dtype) for b in bufs],
                          in_specs=[HBM] * n, out_specs=[HBM] * n, input_output_aliases={i: i for i in range(n)},
                          scratch_shapes=[pltpu.SemaphoreType.DMA((n_sems,)), pltpu.SemaphoreType.DMA((n_sems,))],
                          compiler_params=pltpu.CompilerParams(vmem_limit_bytes=VMEM_LIMIT_MB << 20))(*bufs)


def all_gather_chips(bufs):
    n_sems, start, finish = _gather_phase(len(bufs))
    n = len(bufs)

    def body(*refs):
        outs, send_sems, recv_sems = refs[n:2 * n], refs[2 * n], refs[2 * n + 1]
        start(None, outs, send_sems, recv_sems)
        finish(None, outs, send_sems, recv_sems)

    return _inplace_comm_call(body, bufs, name="all_gather_chips", n_sems=n_sems)


def _gather_phase(n):
    def plan(outs, send_sems, recv_sems):
        x, y, c = _place()

        def copy(w, k, chip, hc, to):
            half = outs[w].shape[1] // 2
            rows = outs[w].at[2 * chip[0] + chip[1], pl.ds(hc * half, half), :]
            return pltpu.make_async_remote_copy(src_ref=rows, dst_ref=rows, send_sem=send_sems.at[6 * w + k],
                                                recv_sem=recv_sems.at[6 * w + k], device_id=to, device_id_type=MESH)

        first = [[copy(w, k, (x, y), c, (*chip, c)) for k, chip in enumerate(_other_chips(x, y))] for w in range(n)]
        passed = [[copy(w, 3 + k, chip, c, (x, y, 1 - c)) for k, chip in enumerate(_other_chips(x, y))]
                  for w in range(n)]
        return copy, first, passed, (x, y, c)

    def start(_, outs, send_sems, recv_sems):
        _, first, _, _ = plan(outs, send_sems, recv_sems)
        for w in range(n):
            for cp in first[w]:
                cp.start()

    def finish(_, outs, send_sems, recv_sems):
        copy, first, passed, (x, y, c) = plan(outs, send_sems, recv_sems)
        chips = _other_chips(x, y)
        for w in range(n):
            for k, chip in enumerate(chips):
                copy(w, k, chip, c, (x, y, c)).wait_recv()
                passed[w][k].start()
        for w in range(n):
            for k, chip in enumerate(chips):
                copy(w, 3 + k, chip, 1 - c, (x, y, c)).wait_recv()
        for w in range(n):
            for cp in first[w] + passed[w]:
                cp.wait_send()

    return 6 * n, start, finish


def hosted_gather(bufs):
    n_sems, start, finish = _gather_phase(len(bufs))
    return Hosted(bufs, [jax.ShapeDtypeStruct(b.shape, b.dtype) for b in bufs], {i: i for i in range(len(bufs))},
                  n_sems, start, finish)


def pair_swap_halves(gs):
    n = len(gs)
    n_sems, start, finish = _swap_phase(n)

    def body(*refs):
        g_refs, o_refs, send_sems, recv_sems = refs[:n], refs[n:2 * n], refs[2 * n], refs[2 * n + 1]
        start(g_refs, o_refs, send_sems, recv_sems)
        finish(g_refs, o_refs, send_sems, recv_sems)

    return _comm_call(body, name="pair_swap_halves", out_shape=_swap_shapes(gs), in_specs=[HBM] * n, out_specs=[HBM] * n,
                      scratch_shapes=[pltpu.SemaphoreType.DMA((n_sems,)), pltpu.SemaphoreType.DMA((n_sems,))])(*gs)


def _swap_shapes(gs):
    return [jax.ShapeDtypeStruct((N_CHIPS, g.shape[1] // 2, g.shape[2]), g.dtype) for g in gs]


def _swap_phase(n):
    def copies(g_refs, o_refs, send_sems, recv_sems):
        x, y, c = _place()
        out = []
        for w in range(n):
            half = g_refs[w].shape[1] // 2
            out.append(pltpu.make_async_remote_copy(
                src_ref=g_refs[w].at[:, pl.ds((1 - c) * half, half), :], dst_ref=o_refs[w], send_sem=send_sems.at[w],
                recv_sem=recv_sems.at[w], device_id=(x, y, 1 - c), device_id_type=MESH))
        return out

    def start(*refs):
        for cp in copies(*refs):
            cp.start()

    def finish(*refs):
        for cp in copies(*refs):
            cp.wait()

    return n, start, finish


def hosted_swap(gs):
    n_sems, start, finish = _swap_phase(len(gs))
    return Hosted(gs, _swap_shapes(gs), {}, n_sems, start, finish)


def scatter_chips(ps):
    n = len(ps)
    n_sems, start, finish = _scatter_phase(n)

    def body(*refs):
        p_refs, o_refs, send_sems, recv_sems = refs[:n], refs[n:2 * n], refs[2 * n], refs[2 * n + 1]
        start(p_refs, o_refs, send_sems, recv_sems)
        finish(p_refs, o_refs, send_sems, recv_sems)

    return _comm_call(body, name="scatter_chips", out_shape=_scatter_shapes(ps), in_specs=[HBM] * n, out_specs=[HBM] * n,
                      scratch_shapes=[pltpu.SemaphoreType.DMA((n_sems,)), pltpu.SemaphoreType.DMA((n_sems,))])(*ps)


def _scatter_shapes(ps):
    return [jax.ShapeDtypeStruct((3,) + p.shape[1:], p.dtype) for p in ps]


def _scatter_phase(n):
    def copies(p_refs, o_refs, send_sems, recv_sems):
        x, y, c = _place()
        return [pltpu.make_async_remote_copy(src_ref=p_refs[w].at[2 * chip[0] + chip[1]], dst_ref=o_refs[w].at[k],
                                             send_sem=send_sems.at[3 * w + k], recv_sem=recv_sems.at[3 * w + k],
                                             device_id=(*chip, c), device_id_type=MESH)
                for w in range(n) for k, chip in enumerate(_other_chips(x, y))]

    def start(*refs):
        for cp in copies(*refs):
            cp.start()

    def finish(*refs):
        for cp in copies(*refs):
            cp.wait()

    return 3 * n, start, finish


def hosted_scatter(ps):
    n_sems, start, finish = _scatter_phase(len(ps))
    return Hosted(ps, _scatter_shapes(ps), {}, n_sems, start, finish)


def pair_join_halves(rs):
    n = len(rs)
    n_sems, start, finish = _join_phase(n)

    def body(*refs):
        outs, send_sems, recv_sems = refs[n:2 * n], refs[2 * n], refs[2 * n + 1]
        start(None, outs, send_sems, recv_sems)
        finish(None, outs, send_sems, recv_sems)

    return _inplace_comm_call(body, rs, name="pair_join_halves", n_sems=n_sems)


def _join_phase(n):
    def copies(_, outs, send_sems, recv_sems):
        x, y, c = _place()
        out = []
        for w in range(n):
            half = outs[w].shape[0] // 2
            rows = outs[w].at[pl.ds(c * half, half), :]
            out.append(pltpu.make_async_remote_copy(src_ref=rows, dst_ref=rows, send_sem=send_sems.at[w],
                                                    recv_sem=recv_sems.at[w], device_id=(x, y, 1 - c),
                                                    device_id_type=MESH))
        return out

    def start(*refs):
        for cp in copies(*refs):
            cp.start()

    def finish(*refs):
        for cp in copies(*refs):
            cp.wait()

    return n, start, finish


def hosted_join(rs):
    n_sems, start, finish = _join_phase(len(rs))
    return Hosted(rs, [jax.ShapeDtypeStruct(r.shape, r.dtype) for r in rs], {i: i for i in range(len(rs))},
                  n_sems, start, finish)


def all_reduce_small(v, name):
    n, L = v.shape
    n_dev = 8

    def body(v_ref, out_ref, buf, send_sems, recv_sems):
        x, y, c = _place()
        me = 4 * x + 2 * y + c
        buf[me] = v_ref[...]

        def copy(k, slot, peer):
            return pltpu.make_async_remote_copy(src_ref=v_ref, dst_ref=buf.at[slot], send_sem=send_sems.at[k],
                                                recv_sem=recv_sems.at[slot],
                                                device_id=(peer // 4, (peer // 2) % 2, peer % 2), device_id_type=MESH)

        sends = [copy(k - 1, me, (me + k) % n_dev) for k in range(1, n_dev)]
        for cp in sends:
            cp.start()
        for k in range(1, n_dev):
            src = (me + k) % n_dev
            copy(0, src, src).wait_recv()
        for cp in sends:
            cp.wait_send()
        acc = buf[0]
        for s in range(1, n_dev):
            acc = acc + buf[s]
        out_ref[...] = acc

    return _comm_call(body, name=name, out_shape=jax.ShapeDtypeStruct((n, L), v.dtype), in_specs=[VMEM], out_specs=VMEM,
                      scratch_shapes=[pltpu.VMEM((n_dev, n, L), v.dtype), pltpu.SemaphoreType.DMA((n_dev - 1,)),
                                      pltpu.SemaphoreType.DMA((n_dev,))])(v)


BF16_ROWS = 16
STREAM_BLOCK_BYTES = 4 << 20


def _rows_tile(n, row_bytes, budget=1 << 20, mult=SUBLANES):
    best = mult if n % mult == 0 else n
    for t in range(mult, n + 1, mult):
        if n % t == 0 and t * row_bytes <= budget:
            best = t
    return best


def _scalars(*vals):
    return jnp.stack([jnp.asarray(v, jnp.int32) for v in vals])


def cast_to_slot(w, chip, name):
    r, c = w.shape
    tb = _rows_tile(r, c * 4, budget=STREAM_BLOCK_BYTES, mult=BF16_ROWS)

    def body(s_ref, w_ref, o_ref):
        o_ref[...] = w_ref[...].astype(BF16)

    spec = pltpu.PrefetchScalarGridSpec(
        num_scalar_prefetch=1, grid=(r // tb,), in_specs=[pl.BlockSpec((tb, c), lambda i, s: (i, 0))],
        out_specs=pl.BlockSpec((None, tb, c), lambda i, s: (s[0], i, 0)))
    return pl.pallas_call(body, name=name, grid_spec=spec, out_shape=jax.ShapeDtypeStruct((N_CHIPS, r, c), BF16),
                          compiler_params=pltpu.CompilerParams(dimension_semantics=("parallel",)))(_scalars(chip), w)


def pair_add(g, got, c, name):
    _, r, w = g.shape
    half = r // 2
    tb = _rows_tile(half, w * 4, budget=STREAM_BLOCK_BYTES, mult=BF16_ROWS)
    nb = half // tb

    def body(c_ref, g_ref, got_ref, o_ref):
        o_ref[...] = (g_ref[...].astype(F32) + got_ref[...].astype(F32)).astype(o_ref.dtype)

    spec = pltpu.PrefetchScalarGridSpec(
        num_scalar_prefetch=1, grid=(N_CHIPS, nb),
        in_specs=[pl.BlockSpec((None, tb, w), lambda s, i, c_ref: (s, c_ref[0] * nb + i, 0)),
                  pl.BlockSpec((None, tb, w), lambda s, i, c_ref: (s, i, 0))],
        out_specs=pl.BlockSpec((None, tb, w), lambda s, i, c_ref: (s, i, 0)))
    return pl.pallas_call(body, name=name, grid_spec=spec, out_shape=jax.ShapeDtypeStruct((N_CHIPS, half, w), BF16),
                          compiler_params=pltpu.CompilerParams(dimension_semantics=("parallel", "parallel")))(
        _scalars(c), g, got)


def chip_add(p, got, chip, c, name):
    _, h, w = p.shape
    tb = _rows_tile(h, w * 4, budget=STREAM_BLOCK_BYTES, mult=BF16_ROWS)
    nb = h // tb

    def body(s_ref, p_ref, got_ref, o_ref):
        acc = p_ref[...].astype(F32)
        for k in range(3):
            acc = acc + got_ref[k].astype(F32)
        o_ref[...] = acc

    spec = pltpu.PrefetchScalarGridSpec(
        num_scalar_prefetch=1, grid=(nb,),
        in_specs=[pl.BlockSpec((None, tb, w), lambda i, s: (s[0], i, 0)),
                  pl.BlockSpec((3, tb, w), lambda i, s: (0, i, 0))],
        out_specs=pl.BlockSpec((tb, w), lambda i, s: (s[1] * nb + i, 0)))
    return pl.pallas_call(body, name=name, grid_spec=spec, out_shape=jax.ShapeDtypeStruct((2 * h, w), F32),
                          compiler_params=pltpu.CompilerParams(dimension_semantics=("parallel",)))(
        _scalars(chip, c), p, got)


def adamw(w, g, m, v, name):
    r, c = w.shape
    tr = _rows_tile(r, c * 4, budget=STREAM_BLOCK_BYTES // 2)

    def body(w_ref, g_ref, m_ref, v_ref, d_ref, m_out, v_out):
        gg = g_ref[...]
        m2 = ADAM_B1 * m_ref[...] + (1.0 - ADAM_B1) * gg
        v2 = ADAM_B2 * v_ref[...] + (1.0 - ADAM_B2) * jnp.square(gg)
        m_hat = m2 / (1.0 - ADAM_B1 ** ADAM_STEP)
        v_hat = v2 / (1.0 - ADAM_B2 ** ADAM_STEP)
        d_ref[...] = -ADAM_LR * (m_hat / (jnp.sqrt(v_hat) + ADAM_EPS) + ADAM_WD * w_ref[...])
        m_out[...] = m2
        v_out[...] = v2

    blk = pl.BlockSpec((tr, c), lambda i: (i, 0))
    return _pcall(body, name=name, grid=(r // tr,), in_specs=[blk] * 4, out_specs=[blk] * 3,
                  out_shape=[jax.ShapeDtypeStruct((r, c), F32)] * 3, dims=("parallel",))(w, g, m, v)


WEIGHTS = ["l0_attn_norm", "l0_w_in", "l0_gdn_conv", "l0_gdn_A_log", "l0_gdn_dt_bias", "l0_gdn_norm", "l0_mla_q_norm",
           "l0_mla_w_uq", "l0_mla_kv_norm", "l0_mla_w_ukv", "l0_w_out", "l0_ffn_norm", "l0_ffn_w_up", "l0_ffn_conv_w",
           "l0_ffn_conv_b", "l0_ffn_w_down", "l0_ple_proj", "l0_ple_gate_norm", "l0_ple_gate", "l1_attn_norm",
           "l1_w_in", "l1_ret_norm", "l1_w_out", "l1_ffn_norm", "l1_ffn_w_up", "l1_ffn_conv_w", "l1_ffn_conv_b",
           "l1_ffn_w_down", "l1_ple_proj", "l1_ple_gate_norm", "l1_ple_gate", "final_norm"]
COL_SHARDED = ["l0_w_in", "l0_mla_w_uq", "l0_mla_w_ukv", "l0_ffn_w_up", "l0_ple_proj", "l1_w_in", "l1_ffn_w_up",
               "l1_ple_proj"]
ROW_SHARDED = ["l0_w_out", "l0_ffn_w_down", "l0_ple_gate", "l1_w_out", "l1_ffn_w_down", "l1_ple_gate"]
BIG = [k for k in WEIGHTS if k in COL_SHARDED or k in ROW_SHARDED]
SMALL_SHARDED = ["l0_gdn_conv", "l0_ffn_conv_w", "l1_ffn_conv_w"]
SMALL = [k for k in WEIGHTS if k not in BIG]
KEPT_SHARDED = ["l0_ffn_w_up", "l0_ple_proj", "l1_w_in", "l1_ffn_w_up", "l1_ple_proj"]
GATHER_FIRST = ["l0_w_in"]
GATHER_L0 = ["l0_mla_w_uq", "l0_mla_w_ukv", "l0_w_out", "l0_ffn_w_up", "l0_ffn_w_down", "l0_ple_proj", "l0_ple_gate",
             "l1_w_out"]
GATHER_L1 = ["l1_w_in"]
GATHER_FFN = ["l1_ffn_w_down", "l1_ple_proj", "l1_ple_gate"]
GATHER_L1_IN = ["l1_ffn_w_up"]
REDUCE_L1 = [k for k in BIG if k.startswith("l1_")]
REDUCE_DQ = ["l0_ffn_w_up"]
REDUCE_L0 = ["l0_ffn_w_down", "l0_ple_proj", "l0_ple_gate"]
REDUCE_LAST = ["l0_w_in", "l0_mla_w_uq", "l0_mla_w_ukv", "l0_w_out"]


class Exchange:
    def __init__(self, chip, core):
        self.chip, self.core = chip, core

    def gather(self, bufs):
        return all_gather_chips(bufs)

    def pair_sums(self, names, grads, swapped=None):
        swapped = pair_swap_halves(grads) if swapped is None else swapped
        return [pair_add(g, got, self.core, "rs_pair_add_" + k) for k, g, got in zip(names, grads, swapped)]

    def halves(self, names, sums, landed):
        return [chip_add(sums[k], landed[k], self.chip, self.core, "rs_chip_add_" + k) for k in names]


def _cols_to_full(s):
    j, k, n = s.shape
    return jnp.transpose(s, (1, 0, 2)).reshape(k, j * n)


def _full_to_cols(g):
    k, n4 = g.shape
    return jnp.transpose(g.reshape(k, N_CHIPS, n4 // N_CHIPS), (1, 0, 2))


def _pack_small(vals):
    flat = jnp.concatenate([v.astype(F32).reshape(-1) for v in vals])
    align = SUBLANES * LANES
    flat = jnp.pad(flat, (0, -flat.shape[0] % align))
    return flat.reshape(-1, LANES)


def _unpack_small(rows, shapes):
    flat = rows.reshape(-1)
    out, off = [], 0
    for shp in shapes:
        n = int(np.prod(shp))
        out.append(flat[off:off + n].reshape(shp))
        off += n
    return out


INPUTS = (["x", "p", "positions"] + WEIGHTS + ["loss_target"] + ["m_" + k for k in WEIGHTS]
          + ["v_" + k for k in WEIGHTS])


def kernel(
        x, p, positions, l0_attn_norm, l0_w_in, l0_gdn_conv, l0_gdn_A_log, l0_gdn_dt_bias, l0_gdn_norm, l0_mla_q_norm,
        l0_mla_w_uq, l0_mla_kv_norm, l0_mla_w_ukv, l0_w_out, l0_ffn_norm, l0_ffn_w_up, l0_ffn_conv_w, l0_ffn_conv_b,
        l0_ffn_w_down, l0_ple_proj, l0_ple_gate_norm, l0_ple_gate, l1_attn_norm, l1_w_in, l1_ret_norm, l1_w_out,
        l1_ffn_norm, l1_ffn_w_up, l1_ffn_conv_w, l1_ffn_conv_b, l1_ffn_w_down, l1_ple_proj, l1_ple_gate_norm,
        l1_ple_gate, final_norm, loss_target, m_l0_attn_norm, m_l0_w_in, m_l0_gdn_conv, m_l0_gdn_A_log,
        m_l0_gdn_dt_bias, m_l0_gdn_norm, m_l0_mla_q_norm, m_l0_mla_w_uq, m_l0_mla_kv_norm, m_l0_mla_w_ukv, m_l0_w_out,
        m_l0_ffn_norm, m_l0_ffn_w_up, m_l0_ffn_conv_w, m_l0_ffn_conv_b, m_l0_ffn_w_down, m_l0_ple_proj,
        m_l0_ple_gate_norm, m_l0_ple_gate, m_l1_attn_norm, m_l1_w_in, m_l1_ret_norm, m_l1_w_out, m_l1_ffn_norm,
        m_l1_ffn_w_up, m_l1_ffn_conv_w, m_l1_ffn_conv_b, m_l1_ffn_w_down, m_l1_ple_proj, m_l1_ple_gate_norm,
        m_l1_ple_gate, m_final_norm, v_l0_attn_norm, v_l0_w_in, v_l0_gdn_conv, v_l0_gdn_A_log, v_l0_gdn_dt_bias,
        v_l0_gdn_norm, v_l0_mla_q_norm, v_l0_mla_w_uq, v_l0_mla_kv_norm, v_l0_mla_w_ukv, v_l0_w_out, v_l0_ffn_norm,
        v_l0_ffn_w_up, v_l0_ffn_conv_w, v_l0_ffn_conv_b, v_l0_ffn_w_down, v_l0_ple_proj, v_l0_ple_gate_norm,
        v_l0_ple_gate, v_l1_attn_norm, v_l1_w_in, v_l1_ret_norm, v_l1_w_out, v_l1_ffn_norm, v_l1_ffn_w_up,
        v_l1_ffn_conv_w, v_l1_ffn_conv_b, v_l1_ffn_w_down, v_l1_ple_proj, v_l1_ple_gate_norm, v_l1_ple_gate,
        v_final_norm):
    given = locals()
    a = {k: given[k] for k in INPUTS}
    x_i, y_i, c_i = _place()
    chip = 2 * x_i + y_i
    shard_shapes = {k: a[k].shape for k in WEIGHTS}

    slots = {k: cast_to_slot(a[k], chip, "cast_" + k) for k in BIG}
    W = {}
    placed = []
    for k in SMALL_SHARDED:
        r, c = shard_shapes[k]
        mine = jnp.where(c_i == 0, a[k], jnp.zeros_like(a[k]))
        placed.append(lax.dynamic_update_slice(jnp.zeros((r, N_CHIPS * c), F32), mine, (0, chip * c)))
    full_small = _unpack_small(all_reduce_small(_pack_small(placed), "gather_small_weights"),
                               [p_.shape for p_ in placed])
    for k in SMALL:
        W[k] = a[k]
    W.update(dict(zip(SMALL_SHARDED, full_small)))

    loss_part, grad_x, grads, G = local_step(a["x"][0], a["p"][:, 0], a["positions"][0], a["loss_target"][0], slots, W,
                                             Exchange(chip, c_i))
    loss = lax.psum(loss_part, ("x", "y", "c"))
    deltas, new_m, new_v = {}, {}, {}
    for k in BIG:
        deltas[k], new_m[k], new_v[k] = adamw(a[k], grads[k], a["m_" + k], a["v_" + k], "adamw_" + k)

    small_full = [G[k].reshape(-1) for k in SMALL]
    summed = _unpack_small(all_reduce_small(_pack_small(small_full), "reduce_small_grads"),
                           [G[k].shape for k in SMALL])
    for k, g in zip(SMALL, summed):
        if k in SMALL_SHARDED:
            r, c = shard_shapes[k]
            g = lax.dynamic_slice(g.reshape(r, N_CHIPS * c), (0, chip * c), (r, c))
        grads[k] = g.reshape(shard_shapes[k])
    packed = [_pack_small([d[k] for k in SMALL]) for d in (
        {k: a[k] for k in SMALL}, grads, {k: a["m_" + k] for k in SMALL}, {k: a["v_" + k] for k in SMALL})]
    outs = adamw(*packed, "adamw_small")
    shapes = [shard_shapes[k] for k in SMALL]
    for d, rows in zip((deltas, new_m, new_v), outs):
        d.update(dict(zip(SMALL, _unpack_small(rows, shapes))))

    return (loss, grad_x[None], *[grads[k] for k in WEIGHTS], *[deltas[k] for k in WEIGHTS],
            *[new_m[k] for k in WEIGHTS], *[new_v[k] for k in WEIGHTS])
```

```python
import functools
import math

import numpy as np
import jax
import jax.numpy as jnp
from jax import lax
from jax.experimental import pallas as pl
from jax.experimental.pallas import tpu as pltpu

F32, BF16 = jnp.float32, jnp.bfloat16
HI = lax.Precision.HIGHEST
MESH = pl.DeviceIdType.MESH

NORM_EPS = 1e-6
ROPE_THETA = 10000.0
D_MODEL = 2048
PLE_DIM = 256
GDN_HEADS, GDN_DK, GDN_DV, GDN_CONV = 8, 128, 128, 4
MLA_HEADS, MLA_Q_RANK, MLA_KV_RANK, MLA_NOPE, MLA_ROPE, MLA_V = 8, 512, 512, 128, 64, 128
RET_HEADS, RET_DK, RET_DV = 8, 256, 512
D_FF, FFN_CONV = 5632, 3
ADAM_LR, ADAM_B1, ADAM_B2, ADAM_EPS, ADAM_WD, ADAM_STEP = 0.001, 0.9, 0.999, 1e-08, 0.01, 10

LANES = 128
SUBLANES = 8
CHUNK = 128
N_CHIPS = 4
VMEM_LIMIT_MB = 56

ZIN_QKV, ZIN_Z, ZIN_CQ, ZIN_CKV, ZIN_KR, ZIN_AB, ZIN_W = 0, 3072, 4096, 4608, 5120, 5248, 5376


class Hosted:
    def __init__(self, inputs, out_shapes, aliases, n_sems, start, finish):
        self.inputs, self.out_shapes, self.aliases, self.n_sems = list(inputs), list(out_shapes), dict(aliases), n_sems
        self.start, self.finish = start, finish


def _pcall(body, *, name, out_shape, grid=(), in_specs=None, out_specs=None, scratch_shapes=(), dims=None,
           hosted=None):
    params = dict(vmem_limit_bytes=VMEM_LIMIT_MB << 20)
    if dims is not None:
        params["dimension_semantics"] = dims
    if hosted is None:
        return pl.pallas_call(body, name=name, out_shape=out_shape, grid=grid, in_specs=in_specs, out_specs=out_specs,
                              scratch_shapes=list(scratch_shapes), compiler_params=pltpu.CompilerParams(**params))
    single = not isinstance(out_shape, (list, tuple))
    out_shape = [out_shape] if single else list(out_shape)
    out_specs = [out_specs] if single else list(out_specs)
    n_in, n_out, n_scr = len(in_specs), len(out_shape), len(scratch_shapes)
    h_in, h_out = len(hosted.inputs), len(hosted.out_shapes)
    hbm = pl.BlockSpec(memory_space=pltpu.HBM)

    def hosting_body(*refs):
        ins, h_ins = refs[:n_in], refs[n_in:n_in + h_in]
        o0 = n_in + h_in
        outs, h_outs = refs[o0:o0 + n_out], refs[o0 + n_out:o0 + n_out + h_out]
        s0 = o0 + n_out + h_out
        scr, (send_sems, recv_sems) = refs[s0:s0 + n_scr], refs[s0 + n_scr:]
        ids = [pl.program_id(d) for d in range(len(grid))]
        first = functools.reduce(lambda u, v: u & v, [i == 0 for i in ids])
        last = functools.reduce(lambda u, v: u & v, [i == g - 1 for i, g in zip(ids, grid)])

        @pl.when(first)
        def _():
            hosted.start(h_ins, h_outs, send_sems, recv_sems)

        body(*ins, *outs, *scr)

        @pl.when(last)
        def _():
            hosted.finish(h_ins, h_outs, send_sems, recv_sems)

    params["dimension_semantics"] = ("arbitrary",) * len(grid)
    call = pl.pallas_call(
        hosting_body, name=name, out_shape=out_shape + hosted.out_shapes, grid=grid,
        in_specs=list(in_specs) + [hbm] * h_in, out_specs=out_specs + [hbm] * h_out,
        scratch_shapes=list(scratch_shapes) + [pltpu.SemaphoreType.DMA((hosted.n_sems,)),
                                               pltpu.SemaphoreType.DMA((hosted.n_sems,))],
        input_output_aliases={n_in + i: n_out + o for i, o in hosted.aliases.items()},
        compiler_params=pltpu.CompilerParams(**params))

    def run(*args):
        res = call(*args, *hosted.inputs)
        main = res[:n_out]
        return (main[0] if single else main), list(res[n_out:])

    return run


def _tile(n, target, mult=LANES):
    best = None
    for t in range(mult, min(n, target) + 1, mult):
        if n % t == 0:
            best = t
    return best or n


_DN = {"nn": (((1,), (0,)), ((), ())), "nt": (((1,), (1,)), ((), ())), "tn": (((0,), (0,)), ((), ()))}


def matmul(a, b, mode, out_dtype, *, name, add=None, b_shards=1, out_shards=1, tm=512, tn=1024, tk=2048,
           hosted=None, norm_bwd=None, b_shard0=0):
    bs = b.shape[-2:]
    if mode == "nn":
        (M, K), (K2, N) = a.shape, (bs[0], bs[1] * b_shards)
    elif mode == "nt":
        (M, K), (N, K2) = a.shape, (bs[0], bs[1] * b_shards)
    else:
        (K, M), (K2, N) = a.shape, bs
    assert K == K2, (name, a.shape, b.shape)
    n_sh = N // max(b_shards if mode == "nn" else 1, out_shards)
    k_sh = K // (b_shards if mode == "nt" else 1)
    tm, tn, tk = _tile(M, tm), _tile(n_sh, tn), _tile(k_sh, tk)
    nk = K // tk
    nbn, nbk = n_sh // tn, k_sh // tk
    dn = _DN[mode]
    has_add = add is not None
    a_bytes, b_bytes = a.size * a.dtype.itemsize, b.size * b.dtype.itemsize
    i_outer = nk > 1 or a_bytes + (M // tm) * b_bytes <= b_bytes + (N // tn) * a_bytes

    def ij(g0, g1):
        return (g0, g1) if i_outer else (g1, g0)

    fused_norm = norm_bwd is not None
    if fused_norm:
        assert tn == N and out_shards == 1 and not has_add and hosted is None, name
        i_outer = True

    def body(*refs):
        a_ref, b_ref = refs[:2]
        add_ref = refs[2] if has_add else None
        o_ref = refs[3 if has_add else 2]
        part = lax.dot_general(a_ref[...].astype(BF16), b_ref[...].astype(BF16), dn, preferred_element_type=F32)

        def finish(r):
            if fused_norm:
                h_ref, g_ref, dh_ref, o32_ref, o16_ref, dg_ref = refs[2:8]
                _, vjp = jax.vjp(_rms, h_ref[...], g_ref[...])
                dx, dg = vjp(r)
                out = dx + dh_ref[...]
                o32_ref[...] = out
                o16_ref[...] = out.astype(BF16)
                dg_ref[...] += dg
                return
            if has_add:
                r = r + add_ref[...]
            o_ref[...] = r.astype(out_dtype)

        if fused_norm:
            @pl.when((pl.program_id(0) == 0) & (pl.program_id(2) == 0))
            def _():
                refs[7][...] = jnp.zeros_like(refs[7])

        if nk == 1:
            finish(part)
            return
        acc = refs[-1]
        k = pl.program_id(2)

        @pl.when(k == 0)
        def _():
            acc[...] = part

        @pl.when(k > 0)
        def _():
            acc[...] += part

        @pl.when(k == nk - 1)
        def _():
            finish(acc[...])

    def spec(block, fn):
        return pl.BlockSpec(block, lambda g0, g1, k: fn(*ij(g0, g1), k))

    if mode == "tn":
        a_spec = spec((tk, tm), lambda i, j, k: (k, i))
    else:
        a_spec = spec((tm, tk), lambda i, j, k: (i, k))
    if mode == "nt":
        if b_shards > 1:
            b_spec = spec((None, tn, tk), lambda i, j, k: (b_shard0 + k // nbk, j, k % nbk))
        else:
            b_spec = spec((tn, tk), lambda i, j, k: (j, k))
    elif b_shards > 1:
        b_spec = spec((None, tk, tn), lambda i, j, k: (j // nbn, k, j % nbn))
    else:
        b_spec = spec((tk, tn), lambda i, j, k: (k, j))
    in_specs = [a_spec, b_spec]
    args = [a, b]
    if has_add:
        in_specs.append(spec((tm, tn), lambda i, j, k: (i, j)))
        args.append(add)
    if out_shards > 1:
        out_spec = spec((None, tm, tn), lambda i, j, k: (j // nbn, i, j % nbn))
        out_shape = jax.ShapeDtypeStruct((out_shards, M, n_sh), out_dtype)
    else:
        out_spec = spec((tm, tn), lambda i, j, k: (i, j))
        out_shape = jax.ShapeDtypeStruct((M, N), out_dtype)
    gi, gj = M // tm, N // tn
    if fused_norm:
        h, gain, dh = norm_bwd
        row_blk = spec((tm, N), lambda i, j, k: (i, 0))
        gain_blk = spec((1, N), lambda i, j, k: (0, 0))
        return _pcall(body, name=name, grid=(gi, 1, nk), in_specs=in_specs + [row_blk, gain_blk, row_blk],
                      out_specs=[row_blk, row_blk, gain_blk],
                      out_shape=[jax.ShapeDtypeStruct((M, N), F32), jax.ShapeDtypeStruct((M, N), BF16),
                                 jax.ShapeDtypeStruct((1, N), F32)],
                      scratch_shapes=[pltpu.VMEM((tm, tn), F32)] if nk > 1 else [],
                      dims=("arbitrary", "arbitrary", "arbitrary"))(a, b, h, gain, dh)
    return _pcall(body, name=name, out_shape=out_shape, grid=(gi, gj, nk) if i_outer else (gj, gi, nk),
                  in_specs=in_specs, out_specs=out_spec,
                  scratch_shapes=[pltpu.VMEM((tm, tn), F32)] if nk > 1 else [],
                  dims=("parallel", "parallel", "arbitrary"), hosted=hosted)(*args)


def _row_spec(tr, w, c):
    return pl.BlockSpec((tr, w), lambda i: (i, c))


def _full_spec(arr):
    return pl.BlockSpec(arr.shape, lambda i: (0,) * arr.ndim)


def rowwise(fn, rows, params, nd_rows, outs, *, name, tr=256):
    S = rows[0][0].shape[0]
    tr = min(tr, S)
    n_in = len(rows) + len(params) + len(nd_rows)

    def body(*refs):
        res = fn(*[x[...] for x in refs[:n_in]])
        for o_ref, v in zip(refs[n_in:], res):
            o_ref[...] = v.astype(o_ref.dtype)

    return _pcall(body, name=name, grid=(S // tr,),
                  in_specs=([_row_spec(tr, w, c) for (_, w, c) in rows] + [_full_spec(q) for q in params]
                            + [_row_spec(tr, w, c) for (_, w, c) in nd_rows]),
                  out_specs=[_row_spec(tr, w, 0) for (w, _) in outs],
                  out_shape=[jax.ShapeDtypeStruct((S, w), dt) for (w, dt) in outs],
                  dims=("parallel",))(*[r[0] for r in rows], *params, *[r[0] for r in nd_rows])


def rowwise_bwd(fn, rows, params, nd_rows, cts, d_dtypes, *, name, adds=None, tr=256):
    S = rows[0][0].shape[0]
    tr = min(tr, S)
    n_r, n_p, n_n, n_c = len(rows), len(params), len(nd_rows), len(cts)
    adds = adds or [None] * n_r
    add_list = [a for a in adds if a is not None]
    n_a = len(add_list)
    d_dtypes = [dt if isinstance(dt, (list, tuple)) else (dt,) for dt in d_dtypes]
    n_d = sum(len(dt) for dt in d_dtypes)

    def body(*refs):
        it = iter(refs)
        r = [next(it)[...] for _ in range(n_r)]
        p = [next(it)[...] for _ in range(n_p)]
        nd = [next(it)[...] for _ in range(n_n)]
        c = [next(it)[...] for _ in range(n_c)]
        ad = [next(it)[...] for _ in range(n_a)]
        d_row_refs = [[next(it) for _ in dts] for dts in d_dtypes]
        d_par_refs = [next(it) for _ in range(n_p)]
        outs, vjp = jax.vjp(lambda *dp: fn(*dp, *nd), *r, *p)
        g = vjp(tuple(ci.astype(o.dtype) for ci, o in zip(c, outs)))
        ai = 0
        for k in range(n_r):
            gk = g[k].astype(F32)
            if adds[k] is not None:
                gk = gk + ad[ai].astype(F32)
                ai += 1
            for ref in d_row_refs[k]:
                ref[...] = gk.astype(ref.dtype)

        @pl.when(pl.program_id(0) == 0)
        def _():
            for ref in d_par_refs:
                ref[...] = jnp.zeros_like(ref)

        for k in range(n_p):
            d_par_refs[k][...] += g[n_r + k].astype(F32)

    in_specs = ([_row_spec(tr, w, c) for (_, w, c) in rows] + [_full_spec(q) for q in params]
                + [_row_spec(tr, w, c) for (_, w, c) in nd_rows] + [_row_spec(tr, w, c) for (_, w, c) in cts]
                + [_row_spec(tr, w, c) for (_, w, c) in add_list])
    out_specs = ([_row_spec(tr, w, 0) for (_, w, _), dts in zip(rows, d_dtypes) for _ in dts]
                 + [_full_spec(q) for q in params])
    out_shape = ([jax.ShapeDtypeStruct((S, w), dt) for (_, w, _), dts in zip(rows, d_dtypes) for dt in dts]
                 + [jax.ShapeDtypeStruct(q.shape, F32) for q in params])
    res = _pcall(body, name=name, grid=(S // tr,), in_specs=in_specs, out_specs=out_specs, out_shape=out_shape,
                 dims=("arbitrary",))(*[r[0] for r in rows], *params, *[r[0] for r in nd_rows],
                                      *[r[0] for r in cts], *[r[0] for r in add_list])
    d_rows, i = [], 0
    for dts in d_dtypes:
        d_rows.append(res[i] if len(dts) == 1 else tuple(res[i:i + len(dts)]))
        i += len(dts)
    return d_rows, res[n_d:]


def _rms(x, g):
    x = x.astype(F32)
    return x * lax.rsqrt(jnp.mean(x * x, axis=-1, keepdims=True) + NORM_EPS) * g


def _fn_rms(x, g):
    return (_rms(x, g),)


def _sigmoid(x):
    return 1.0 / (1.0 + jnp.exp(-x))


def _silu(x):
    return x * _sigmoid(x)


def _softplus(x):
    return jnp.maximum(x, 0.0) + jnp.log(1.0 + jnp.exp(-jnp.abs(x)))


def _fn_ple(h, pp, gl):
    return (h.astype(F32) + pp.astype(F32) * _sigmoid(gl.astype(F32)),)


def _fn_ple_terms(pp, gl):
    return (pp.astype(F32) * _sigmoid(gl.astype(F32)),)


def _rot_half_matrix():
    half = MLA_ROPE // 2
    r = lax.broadcasted_iota(jnp.int32, (LANES, LANES), 0)
    c = lax.broadcasted_iota(jnp.int32, (LANES, LANES), 1)
    plus = (c == r + half) & (r < half)
    minus = (r == c + half) & (c < half)
    return jnp.where(plus, 1.0, 0.0) - jnp.where(minus, 1.0, 0.0)


def _rope_pad(x, cosp, sinp):
    return x * cosp + jnp.dot(x, _rot_half_matrix(), precision=HI, preferred_element_type=F32) * sinp


def _fn_mla_pre(cq, ckv, kr, qn_w, kvn_w, cosp, sinp):
    return (_rms(cq, qn_w), _rms(ckv, kvn_w), _rope_pad(kr.astype(F32), cosp, sinp))


def _fn_rope_q(q, cosp, sinp):
    q = q.astype(F32)
    parts = []
    for h in range(MLA_HEADS):
        base = 2 * LANES * h
        parts.append(q[:, base:base + LANES])
        parts.append(_rope_pad(q[:, base + LANES:base + 2 * LANES], cosp, sinp))
    return (jnp.concatenate(parts, axis=1),)


def _fn_ret_gate(g, on, w):
    return (_silu(g.astype(F32)) * (on.astype(F32) * w),)


def _shift_down(cur, halo, s):
    if s == 0:
        return cur
    r = pltpu.roll(cur, s, 0)
    hs = pltpu.roll(halo, s, 0)
    row = lax.broadcasted_iota(jnp.int32, hs.shape, 0)
    first = jnp.where(row < s, hs, r[:SUBLANES])
    return jnp.concatenate([first, r[SUBLANES:]], axis=0)


def _shift_up(cur, halo, s):
    if s == 0:
        return cur
    n = cur.shape[0]
    r = pltpu.roll(cur, n - s, 0)
    hs = pltpu.roll(halo, SUBLANES - s, 0)
    row = lax.broadcasted_iota(jnp.int32, hs.shape, 0)
    last = jnp.where(row >= SUBLANES - s, hs, r[n - SUBLANES:])
    return jnp.concatenate([r[:n - SUBLANES], last], axis=0)


def _prev_halo_spec(tr, tw, col):
    return pl.BlockSpec((SUBLANES, tw), lambda c, i: (jnp.maximum(i * (tr // SUBLANES) - 1, 0), col(c)))


def _conv_taps(cur, halo, w_ref, width):
    taps = [_shift_down(cur, halo, width - 1 - j) for j in range(width)]
    y = taps[0] * w_ref[0:1, :]
    for j in range(1, width):
        y = y + taps[j] * w_ref[j:j + 1, :]
    return y, taps


def gdn_conv_fwd(zin, w, *, tr=512, tw=512):
    S = zin.shape[0]
    tr = min(tr, S)
    width, C = w.shape

    def body(cur_ref, halo_ref, w_ref, o_ref):
        i = pl.program_id(1)
        halo = halo_ref[...] * (i > 0).astype(F32)
        y, _ = _conv_taps(cur_ref[...], halo, w_ref, width)
        o_ref[...] = _silu(y)

    return _pcall(body, name="gdn_conv_fwd", grid=(C // tw, S // tr),
                  in_specs=[pl.BlockSpec((tr, tw), lambda c, i: (i, c)), _prev_halo_spec(tr, tw, lambda c: c),
                            pl.BlockSpec((width, tw), lambda c, i: (0, c))],
                  out_specs=pl.BlockSpec((tr, tw), lambda c, i: (i, c)),
                  out_shape=jax.ShapeDtypeStruct((S, C), F32), dims=("parallel", "arbitrary"))(zin, zin, w)


def gdn_conv_bwd(zin, w, dy, *, tr=512, tw=512):
    S = zin.shape[0]
    tr = min(tr, S)
    width, C = w.shape
    n_i = S // tr

    def body(cur_ref, halo_ref, w_ref, dy_ref, dx_ref, dw_ref, carry):
        i = pl.program_id(1)
        halo = halo_ref[...] * (i < n_i - 1).astype(F32)
        y, taps = _conv_taps(cur_ref[...], halo, w_ref, width)
        sg = _sigmoid(y)
        da = dy_ref[...] * (sg * (1.0 + y * (1.0 - sg)))

        @pl.when(i == 0)
        def _():
            dw_ref[...] = jnp.zeros_like(dw_ref)
            carry[...] = jnp.zeros_like(carry)

        nxt = carry[...]
        dx = da * w_ref[width - 1:width, :]
        for s in range(1, width):
            dx = dx + _shift_up(da, nxt, s) * w_ref[width - 1 - s:width - s, :]
        dx_ref[...] = dx.astype(dx_ref.dtype)
        carry[...] = da[:SUBLANES]
        for j in range(width):
            dw_ref[j:j + 1, :] += jnp.sum(da * taps[j], axis=0, keepdims=True)

    rows = pl.BlockSpec((tr, tw), lambda c, i: (n_i - 1 - i, c))
    prev = pl.BlockSpec((SUBLANES, tw), lambda c, i: (jnp.maximum((n_i - 1 - i) * (tr // SUBLANES) - 1, 0), c))
    return _pcall(body, name="gdn_conv_bwd", grid=(C // tw, n_i),
                  in_specs=[rows, prev, pl.BlockSpec((width, tw), lambda c, i: (0, c)), rows],
                  out_specs=[rows, pl.BlockSpec((width, tw), lambda c, i: (0, c))],
                  out_shape=[jax.ShapeDtypeStruct((S, C), BF16), jax.ShapeDtypeStruct((width, C), F32)],
                  scratch_shapes=[pltpu.VMEM((SUBLANES, tw), F32)],
                  dims=("arbitrary", "arbitrary"))(zin, zin, w, dy)


def ffn_conv_fwd(u, w, b, *, tr=1024, tw=256):
    S, C2 = u.shape
    tr = min(tr, S)
    width = w.shape[0]
    half = C2 // 2
    nc = half // tw

    def body(g_ref, gh_ref, u_ref, uh_ref, wg_ref, wu_ref, bg_ref, bu_ref, o_ref):
        i = pl.program_id(1)
        live = (i > 0).astype(F32)
        yg, _ = _conv_taps(g_ref[...], gh_ref[...] * live, wg_ref, width)
        yu, _ = _conv_taps(u_ref[...], uh_ref[...] * live, wu_ref, width)
        o_ref[...] = (_silu(yg + bg_ref[...]) * (yu + bu_ref[...])).astype(o_ref.dtype)

    return _pcall(body, name="ffn_conv_fwd", grid=(nc, S // tr),
                  in_specs=[pl.BlockSpec((tr, tw), lambda c, i: (i, c)), _prev_halo_spec(tr, tw, lambda c: c),
                            pl.BlockSpec((tr, tw), lambda c, i: (i, c + nc)),
                            _prev_halo_spec(tr, tw, lambda c: c + nc),
                            pl.BlockSpec((width, tw), lambda c, i: (0, c)),
                            pl.BlockSpec((width, tw), lambda c, i: (0, c + nc)),
                            pl.BlockSpec((1, tw), lambda c, i: (0, c)), pl.BlockSpec((1, tw), lambda c, i: (0, c + nc))],
                  out_specs=pl.BlockSpec((tr, tw), lambda c, i: (i, c)),
                  out_shape=jax.ShapeDtypeStruct((S, half), BF16),
                  dims=("parallel", "arbitrary"))(u, u, u, u, w, w, b, b)


def ffn_conv_bwd(u, w, b, df, *, tr=512, tw=512):
    S, C2 = u.shape
    tr = min(tr, S)
    width = w.shape[0]
    half = C2 // 2
    nc = half // tw
    n_i = S // tr

    def body(g_ref, gh_ref, u_ref, uh_ref, wg_ref, wu_ref, bg_ref, bu_ref, df_ref, dug_ref, duu_ref, dw_ref, db_ref,
             carry):
        i = pl.program_id(1)
        live = (i < n_i - 1).astype(F32)
        yg, gt = _conv_taps(g_ref[...], gh_ref[...] * live, wg_ref, width)
        yu, ut = _conv_taps(u_ref[...], uh_ref[...] * live, wu_ref, width)
        yg = yg + bg_ref[...]
        yu = yu + bu_ref[...]
        sg = _sigmoid(yg)
        dfv = df_ref[...].astype(F32)
        dcs = (dfv * yu * (sg * (1.0 + yg * (1.0 - sg))), dfv * (yg * sg))

        @pl.when(i == 0)
        def _():
            dw_ref[...] = jnp.zeros_like(dw_ref)
            db_ref[...] = jnp.zeros_like(db_ref)
            carry[...] = jnp.zeros_like(carry)

        for t, (dc, taps, w_ref, du_ref) in enumerate(zip(dcs, (gt, ut), (wg_ref, wu_ref), (dug_ref, duu_ref))):
            halo = carry[t]
            du = dc * w_ref[width - 1:width, :]
            for s in range(1, width):
                du = du + _shift_up(dc, halo, s) * w_ref[width - 1 - s:width - s, :]
            du_ref[...] = du.astype(du_ref.dtype)
            carry[t] = dc[:SUBLANES]
            db_ref[t] += jnp.sum(dc, axis=0, keepdims=True)
            for j in range(width):
                dw_ref[t, j:j + 1, :] += jnp.sum(dc * taps[j], axis=0, keepdims=True)

    def prev(col):
        return pl.BlockSpec((SUBLANES, tw),
                            lambda c, i: (jnp.maximum((n_i - 1 - i) * (tr // SUBLANES) - 1, 0), col(c)))

    rows = lambda col: pl.BlockSpec((tr, tw), lambda c, i: (n_i - 1 - i, col(c)))
    du_g, du_u, dw, db = _pcall(
        body, name="ffn_conv_bwd", grid=(nc, n_i),
        in_specs=[rows(lambda c: c), prev(lambda c: c), rows(lambda c: c + nc), prev(lambda c: c + nc),
                  pl.BlockSpec((width, tw), lambda c, i: (0, c)), pl.BlockSpec((width, tw), lambda c, i: (0, c + nc)),
                  pl.BlockSpec((1, tw), lambda c, i: (0, c)), pl.BlockSpec((1, tw), lambda c, i: (0, c + nc)),
                  rows(lambda c: c)],
        out_specs=[rows(lambda c: c), rows(lambda c: c), pl.BlockSpec((2, width, tw), lambda c, i: (0, 0, c)),
                   pl.BlockSpec((2, 1, tw), lambda c, i: (0, 0, c))],
        out_shape=[jax.ShapeDtypeStruct((S, half), BF16), jax.ShapeDtypeStruct((S, half), BF16),
                   jax.ShapeDtypeStruct((2, width, half), F32), jax.ShapeDtypeStruct((2, 1, half), F32)],
        scratch_shapes=[pltpu.VMEM((2, SUBLANES, tw), F32)],
        dims=("arbitrary", "arbitrary"))(u, u, u, u, w, w, b, b, df)
    return du_g, du_u, jnp.concatenate([dw[0], dw[1]], axis=1), jnp.concatenate([db[0], db[1]], axis=1)


_MODE_OF = {v: k for k, v in _DN.items()}


def _bf16_dot(a, b, mode):
    return lax.dot_general(a.astype(BF16), b.astype(BF16), _DN[mode], preferred_element_type=F32)


@functools.partial(jax.custom_vjp, nondiff_argnums=(2,))
def _bdot_mode(a, b, mode):
    return _bf16_dot(a, b, mode)


def _bdot_fwd(a, b, mode):
    return _bf16_dot(a, b, mode), (a, b)


def _bdot_bwd(mode, res, ct):
    a, b = res
    if mode == "nn":
        da, db = _bf16_dot(ct, b, "nt"), _bf16_dot(a, ct, "tn")
    elif mode == "nt":
        da, db = _bf16_dot(ct, b, "nn"), _bf16_dot(ct, a, "tn")
    else:
        da, db = _bf16_dot(b, ct, "nt"), _bf16_dot(a, ct, "nn")
    return da.astype(a.dtype), db.astype(b.dtype)


_bdot_mode.defvjp(_bdot_fwd, _bdot_bwd)


def _bdot(a, b, dn=_DN["nn"]):
    return _bdot_mode(a, b, _MODE_OF[dn])


def _hi_lo(x):
    hi = x.astype(BF16)
    return hi, (x - hi.astype(F32)).astype(BF16)


def _dot3_raw(a, b, mode):
    a1, a2 = _hi_lo(a)
    b1, b2 = _hi_lo(b)
    dot = lambda p, q: lax.dot_general(p, q, _DN[mode], preferred_element_type=F32)
    return dot(a1, b1) + (dot(a1, b2) + dot(a2, b1))


@functools.partial(jax.custom_vjp, nondiff_argnums=(2,))
def _dot3(a, b, mode="nn"):
    return _dot3_raw(a, b, mode)


def _dot3_fwd(a, b, mode):
    return _dot3_raw(a, b, mode), (a, b)


def _dot3_bwd(mode, res, ct):
    a, b = res
    if mode == "nn":
        return _dot3_raw(ct, b, "nt"), _dot3_raw(a, ct, "tn")
    if mode == "nt":
        return _dot3_raw(ct, b, "nn"), _dot3_raw(ct, a, "tn")
    return _dot3_raw(b, ct, "nt"), _dot3_raw(a, ct, "nn")


_dot3.defvjp(_dot3_fwd, _dot3_bwd)


@functools.partial(jax.custom_vjp, nondiff_argnums=(2,))
def _gdot(a, b, mode="nn"):
    return _bf16_dot(a, b, mode)


def _gdot_fwd(a, b, mode):
    return _bf16_dot(a, b, mode), (a.astype(BF16), b.astype(BF16))


def _ct_dot(p, q, mode, ct_first):
    ct, r = (p, q) if ct_first else (q, p)
    c1, c2 = _hi_lo(ct)
    dot = lambda c: lax.dot_general(*((c, r) if ct_first else (r, c)), _DN[mode], preferred_element_type=F32)
    return dot(c1) + dot(c2)


def _gdot_bwd(mode, res, ct):
    a, b = res
    if mode == "nn":
        return _ct_dot(ct, b, "nt", True), _ct_dot(a, ct, "tn", False)
    if mode == "nt":
        return _ct_dot(ct, b, "nn", True), _ct_dot(ct, a, "tn", True)
    return _ct_dot(b, ct, "nt", False), _ct_dot(a, ct, "nn", False)


_gdot.defvjp(_gdot_fwd, _gdot_bwd)


def _split_dot(ones, x):
    x1 = x.astype(BF16)
    r1 = x - x1.astype(F32)
    x2 = r1.astype(BF16)
    x3 = (r1 - x2.astype(F32)).astype(BF16)
    m = ones.astype(BF16)
    dot = lambda p: lax.dot_general(m, p, _DN["nn"], preferred_element_type=F32)
    return dot(x1) + dot(x2) + dot(x3)


@jax.custom_vjp
def _tri_cumsum(x, lower, upper):
    return _split_dot(lower, x)


def _tri_cumsum_fwd(x, lower, upper):
    return _split_dot(lower, x), (lower, upper)


def _tri_cumsum_bwd(res, ct):
    lower, upper = res
    return _split_dot(upper, ct), jnp.zeros_like(lower), jnp.zeros_like(upper)


_tri_cumsum.defvjp(_tri_cumsum_fwd, _tri_cumsum_bwd)


def _tri_masks(n):
    r = lax.broadcasted_iota(jnp.int32, (n, n), 0)
    c = lax.broadcasted_iota(jnp.int32, (n, n), 1)
    return r >= c, r > c


def _gdn_chunk(q, k, v, z, ab, a_row, dt_row, norm_w, state, sel_a, sel_b):
    C = q.shape[0]
    incl, strict = _tri_masks(C)
    lower = jnp.where(incl, 1.0, 0.0)
    qn = q * lax.rsqrt(jnp.sum(q * q, axis=-1, keepdims=True) + NORM_EPS) * (GDN_DK ** -0.5)
    kn = k * lax.rsqrt(jnp.sum(k * k, axis=-1, keepdims=True) + NORM_EPS)
    g = jnp.sum(-jnp.exp(a_row) * _softplus(ab + dt_row) * sel_a, axis=-1, keepdims=True)
    beta = jnp.sum(_sigmoid(ab) * sel_b, axis=-1, keepdims=True)
    gb = jnp.broadcast_to(g, (C, C))
    g_col = _tri_cumsum(gb, lower, jnp.where(strict, 0.0, 1.0))
    g_row = g_col.T
    g_last = jnp.sum(gb, axis=0, keepdims=True)
    gamma = jnp.where(incl, jnp.exp(jnp.where(incl, g_col - g_row, 0.0)), 0.0)
    e_col = jnp.exp(g_col)
    kb = kn * beta
    a_mat = jnp.where(strict, _gdot(kb, kn, "nt") * gamma, 0.0)
    x = jnp.concatenate([v * beta, kb * e_col], axis=1)
    pw = -a_mat
    steps = int(math.log2(C))
    for it in range(steps):
        x = x + _dot3(pw, x, "nn")
        if it < steps - 1:
            pw = _dot3(pw, pw, "nn")
    u, w = x[:, :GDN_DV], x[:, GDN_DV:]
    attn = _gdot(qn, kn, "nt") * gamma
    q_dec = qn * e_col
    k_dec = kn * jnp.exp(g_last - g_col)
    v_new = u - _gdot(w, state, "nn")
    o = _gdot(q_dec, state, "nn") + _gdot(attn, v_new, "nn")
    state_new = state * jnp.exp(jnp.broadcast_to(g_last, state.shape)) + _gdot(k_dec, v_new, "tn")
    y = _rms(o, norm_w) * _silu(z)
    return y, state_new


def _head_selectors(h):
    lane = lax.broadcasted_iota(jnp.int32, (1, LANES), 1)
    return jnp.where(lane == h, 1.0, 0.0), jnp.where(lane == h + GDN_HEADS, 1.0, 0.0)


GDN_HPS = 4
GDN_W = GDN_HPS * LANES


def _gdn_in_specs(rev, nc):
    def n_(n):
        return nc - 1 - n if rev else n
    G = GDN_HEADS // GDN_HPS
    blk = lambda off: pl.BlockSpec((CHUNK, GDN_W), lambda n, h: (n_(n), off + h))
    row = pl.BlockSpec((1, LANES), lambda n, h: (0, 0))
    return n_, [blk(0), blk(G), blk(2 * G), blk(ZIN_Z // GDN_W),
                pl.BlockSpec((CHUNK, LANES), lambda n, h: (n_(n), ZIN_AB // LANES)), row, row, row]


def _lanes(ref, j):
    return ref[:, j * LANES:(j + 1) * LANES]


def _hosting(call, hosted, *args):
    res = call(*args)
    return res if hosted is not None else (res, [])


def gdn_fwd(qkv, zin, a_row, dt_row, norm_w, hosted=None):
    S = qkv.shape[0]
    nc = S // CHUNK
    H = GDN_HEADS
    _, in_specs = _gdn_in_specs(False, nc)

    def body(q_ref, k_ref, v_ref, z_ref, ab_ref, a_ref, dt_ref, nw_ref, y_ref, st_ref, state):
        n, g = pl.program_id(0), pl.program_id(1)
        @pl.when((n == 0) & (g == 0))
        def _():
            state[...] = jnp.zeros_like(state)

        res = []
        for j in range(GDN_HPS):
            h = g * GDN_HPS + j
            st = state[h]
            sel_a, sel_b = _head_selectors(h)
            res.append((st,) + _gdn_chunk(_lanes(q_ref, j), _lanes(k_ref, j), _lanes(v_ref, j), _lanes(z_ref, j),
                                          ab_ref[...], a_ref[...], dt_ref[...], nw_ref[...], st, sel_a, sel_b))
        for j, (st, y, st_new) in enumerate(res):
            st_ref[j] = st
            y_ref[:, j * LANES:(j + 1) * LANES] = y.astype(y_ref.dtype)
            state[g * GDN_HPS + j] = st_new

    call = _pcall(body, name="gdn_fwd", grid=(nc, H // GDN_HPS), in_specs=in_specs,
                  out_specs=[pl.BlockSpec((CHUNK, GDN_W), lambda n, h: (n, h)),
                             pl.BlockSpec((GDN_HPS, None, GDN_DK, GDN_DV), lambda n, h: (h, n, 0, 0))],
                  out_shape=[jax.ShapeDtypeStruct((S, H * GDN_DV), BF16),
                             jax.ShapeDtypeStruct((H, nc, GDN_DK, GDN_DV), F32)],
                  scratch_shapes=[pltpu.VMEM((H, GDN_DK, GDN_DV), F32)],
                  dims=("arbitrary", "arbitrary"), hosted=hosted)
    return _hosting(call, hosted, qkv, qkv, qkv, zin, zin, a_row, dt_row, norm_w)


def gdn_bwd(qkv, zin, a_row, dt_row, norm_w, states, dy, dy_col0, hosted=None):
    S = qkv.shape[0]
    nc = S // CHUNK
    H = GDN_HEADS
    n_, in_specs = _gdn_in_specs(True, nc)
    assert dy_col0 % GDN_HPS == 0
    in_specs = in_specs + [pl.BlockSpec((GDN_HPS, None, GDN_DK, GDN_DV), lambda n, h: (h, n_(n), 0, 0)),
                           pl.BlockSpec((CHUNK, GDN_W), lambda n, h: (n_(n), dy_col0 // GDN_HPS + h))]

    def body(q_ref, k_ref, v_ref, z_ref, ab_ref, a_ref, dt_ref, nw_ref, st_ref, dy_ref,
             dq_ref, dk_ref, dv_ref, dz_ref, dab_ref, da_ref, ddt_ref, dnw_ref, dstate):
        n, g = pl.program_id(0), pl.program_id(1)

        @pl.when((n == 0) & (g == 0))
        def _():
            da_ref[...] = jnp.zeros_like(da_ref)
            ddt_ref[...] = jnp.zeros_like(ddt_ref)
            dnw_ref[...] = jnp.zeros_like(dnw_ref)
            dstate[...] = jnp.zeros_like(dstate)

        @pl.when(g == 0)
        def _():
            dab_ref[...] = jnp.zeros_like(dab_ref)

        res = []
        for j in range(GDN_HPS):
            h = g * GDN_HPS + j
            sel_a, sel_b = _head_selectors(h)
            _, vjp = jax.vjp(lambda *a, sa=sel_a, sb=sel_b: _gdn_chunk(*a, sa, sb), _lanes(q_ref, j), _lanes(k_ref, j),
                             _lanes(v_ref, j), _lanes(z_ref, j), ab_ref[...], a_ref[...], dt_ref[...], nw_ref[...],
                             st_ref[j])
            res.append(vjp((_lanes(dy_ref, j).astype(F32), dstate[h])))
        for j, (dq, dk, dv, dz, dab, da, ddt, dnw, dst) in enumerate(res):
            cols = slice(j * LANES, (j + 1) * LANES)
            dq_ref[:, cols] = dq
            dk_ref[:, cols] = dk
            dv_ref[:, cols] = dv
            dz_ref[:, cols] = dz.astype(dz_ref.dtype)
            dstate[g * GDN_HPS + j] = dst
        dab_ref[...] += sum(r[4] for r in res)
        da_ref[...] += sum(r[5] for r in res)
        ddt_ref[...] += sum(r[6] for r in res)
        dnw_ref[...] += sum(r[7] for r in res)

    blk = pl.BlockSpec((CHUNK, GDN_W), lambda n, h: (n_(n), h))
    row = pl.BlockSpec((1, LANES), lambda n, h: (0, 0))
    wide = jax.ShapeDtypeStruct((S, H * LANES), F32)
    call = _pcall(body, name="gdn_bwd", grid=(nc, H // GDN_HPS), in_specs=in_specs,
                  out_specs=[blk, blk, blk, blk, pl.BlockSpec((CHUNK, LANES), lambda n, h: (n_(n), 0)), row, row, row],
                  out_shape=[wide, wide, wide, jax.ShapeDtypeStruct((S, H * LANES), BF16),
                             jax.ShapeDtypeStruct((S, LANES), F32)] + [jax.ShapeDtypeStruct((1, LANES), F32)] * 3,
                  scratch_shapes=[pltpu.VMEM((H, GDN_DK, GDN_DV), F32)],
                  dims=("arbitrary", "arbitrary"), hosted=hosted)
    return _hosting(call, hosted, qkv, qkv, qkv, zin, zin, a_row, dt_row, norm_w, states, dy)


def _rope_full(x, cos, sin):
    x1, x2 = x[:, :RET_DK // 2], x[:, RET_DK // 2:]
    return jnp.concatenate([x1 * cos - x2 * sin, x2 * cos + x1 * sin], axis=1)


RET_CHUNK = 512


def _ret_chunk(q, k, v, cos, sin, lg, state):
    C = q.shape[0]
    incl, _ = _tri_masks(C)
    qr = _rope_full(q, cos, sin)
    kr = _rope_full(k, cos, sin) * (RET_DK ** -0.5)
    r = lax.broadcasted_iota(jnp.int32, (C, C), 0)
    c = lax.broadcasted_iota(jnp.int32, (C, C), 1)
    dist = jnp.where(incl, (r - c).astype(F32), 0.0)
    lg1 = lg[:, :1]
    decay = jnp.where(incl, jnp.exp(dist * lg1), 0.0)
    pos = lax.broadcasted_iota(jnp.int32, (C, 1), 0).astype(F32)
    xi = jnp.exp((pos + 1.0) * lg1)
    zeta = jnp.exp((C - 1.0 - pos) * lg1)
    inner = _bdot(_bdot(qr, kr, _DN["nt"]) * decay, v)
    cross = _bdot(qr * xi, state)
    state_new = state * jnp.exp(C * lg1) + _bdot(kr * zeta, v, _DN["tn"])
    o = inner + cross
    mu = jnp.mean(o, axis=-1, keepdims=True)
    var = jnp.mean(jnp.square(o - mu), axis=-1, keepdims=True)
    return (o - mu) * lax.rsqrt(var + NORM_EPS), state_new


def _ret_log_gamma():
    lg = np.log1p(-np.power(2.0, -5.0 - np.arange(RET_HEADS, dtype=np.float64))).astype(np.float32)
    return jnp.asarray(np.broadcast_to(lg[:, None, None], (RET_HEADS, 1, LANES)).copy())


def _ret_in_specs(rev, nc):
    def n_(n):
        return nc - 1 - n if rev else n
    H = RET_HEADS
    return n_, [pl.BlockSpec((RET_CHUNK, RET_DK),lambda n, h: (n_(n), h)),
                pl.BlockSpec((RET_CHUNK, RET_DK),lambda n, h: (n_(n), H + h)),
                pl.BlockSpec((RET_CHUNK, RET_DV),lambda n, h: (n_(n), 2 * H * RET_DK // RET_DV + h)),
                pl.BlockSpec((RET_CHUNK, LANES), lambda n, h: (n_(n), 0)),
                pl.BlockSpec((RET_CHUNK, LANES), lambda n, h: (n_(n), 0)),
                pl.BlockSpec((None, 1, LANES), lambda n, h: (h, 0, 0))]


def ret_fwd(zz, cos, sin, hosted=None):
    S = zz.shape[0]
    nc = S // RET_CHUNK
    H = RET_HEADS
    _, in_specs = _ret_in_specs(False, nc)

    def body(q_ref, k_ref, v_ref, cos_ref, sin_ref, lg_ref, o_ref, st_ref, state):
        n, h = pl.program_id(0), pl.program_id(1)

        @pl.when(n == 0)
        def _():
            state[h] = jnp.zeros((RET_DK, RET_DV), F32)

        st = state[h]
        st_ref[...] = st
        o, st_new = _ret_chunk(q_ref[...], k_ref[...], v_ref[...], cos_ref[...], sin_ref[...], lg_ref[...], st)
        o_ref[...] = o
        state[h] = st_new

    call = _pcall(body, name="ret_fwd", grid=(nc, H), in_specs=in_specs,
                  out_specs=[pl.BlockSpec((RET_CHUNK, RET_DV),lambda n, h: (n, h)),
                             pl.BlockSpec((None, None, RET_DK, RET_DV), lambda n, h: (h, n, 0, 0))],
                  out_shape=[jax.ShapeDtypeStruct((S, H * RET_DV), F32),
                             jax.ShapeDtypeStruct((H, nc, RET_DK, RET_DV), F32)],
                  scratch_shapes=[pltpu.VMEM((H, RET_DK, RET_DV), F32)],
                  dims=("arbitrary", "arbitrary"), hosted=hosted)
    return _hosting(call, hosted, zz, zz, zz, cos, sin, _ret_log_gamma())


def ret_bwd(zz, cos, sin, states, do):
    S = zz.shape[0]
    nc = S // RET_CHUNK
    H = RET_HEADS
    n_, in_specs = _ret_in_specs(True, nc)
    in_specs = in_specs + [pl.BlockSpec((None, None, RET_DK, RET_DV), lambda n, h: (h, n_(n), 0, 0)),
                           pl.BlockSpec((RET_CHUNK, RET_DV),lambda n, h: (n_(n), h))]

    def body(q_ref, k_ref, v_ref, cos_ref, sin_ref, lg_ref, st_ref, do_ref, dq_ref, dk_ref, dv_ref, dstate):
        n, h = pl.program_id(0), pl.program_id(1)

        @pl.when(n == 0)
        def _():
            dstate[h] = jnp.zeros((RET_DK, RET_DV), F32)

        cos, sin, lg = cos_ref[...], sin_ref[...], lg_ref[...]
        _, vjp = jax.vjp(lambda q, k, v, st: _ret_chunk(q, k, v, cos, sin, lg, st),
                         q_ref[...], k_ref[...], v_ref[...], st_ref[...])
        dq, dk, dv, dst = vjp((do_ref[...], dstate[h]))
        dq_ref[...] = dq.astype(dq_ref.dtype)
        dk_ref[...] = dk.astype(dk_ref.dtype)
        dv_ref[...] = dv.astype(dv_ref.dtype)
        dstate[h] = dst

    return _pcall(body, name="ret_bwd", grid=(nc, H), in_specs=in_specs,
                  out_specs=[pl.BlockSpec((RET_CHUNK, RET_DK),lambda n, h: (n_(n), h)),
                             pl.BlockSpec((RET_CHUNK, RET_DK),lambda n, h: (n_(n), h)),
                             pl.BlockSpec((RET_CHUNK, RET_DV),lambda n, h: (n_(n), h))],
                  out_shape=[jax.ShapeDtypeStruct((S, H * RET_DK), BF16), jax.ShapeDtypeStruct((S, H * RET_DK), BF16),
                             jax.ShapeDtypeStruct((S, H * RET_DV), BF16)],
                  scratch_shapes=[pltpu.VMEM((H, RET_DK, RET_DV), F32)],
                  dims=("arbitrary", "arbitrary"))(zz, zz, zz, cos, sin, _ret_log_gamma(), states, do)


MLA_SCALE = (MLA_NOPE + MLA_ROPE) ** -0.5
NEG = -1e30


def _mla_scores(q, kn, kpe, diagonal):
    s = (lax.dot_general(q[:, :LANES], kn, _DN["nt"], preferred_element_type=F32)
         + lax.dot_general(q[:, LANES:], kpe, _DN["nt"], preferred_element_type=F32)) * MLA_SCALE
    if diagonal:
        row = lax.broadcasted_iota(jnp.int32, s.shape, 0)
        col = lax.broadcasted_iota(jnp.int32, s.shape, 1)
        s = jnp.where(col <= row, s, NEG)
    return s


def _on_and_below_diagonal(i, j, step):
    @pl.when(j < i)
    def _():
        step(False)

    @pl.when(j == i)
    def _():
        step(True)


FLASH_T = 1024


def flash_fwd(qr, kv, kpe, *, t=FLASH_T, hosted=None):
    S = qr.shape[0]
    t = min(t, S)
    nb = S // t
    H = MLA_HEADS

    def body(q_ref, kn_ref, v_ref, kpe_ref, o_ref, lse_ref, m_s, l_s, acc):
        i, j = pl.program_id(1), pl.program_id(2)

        @pl.when(j == 0)
        def _():
            m_s[...] = jnp.full_like(m_s, NEG)
            l_s[...] = jnp.zeros_like(l_s)
            acc[...] = jnp.zeros_like(acc)

        def step(diagonal):
            s = _mla_scores(q_ref[...], kn_ref[...], kpe_ref[...], diagonal)
            m_new = jnp.maximum(m_s[...], jnp.max(s, axis=-1, keepdims=True))
            p = jnp.exp(s - m_new)
            alpha = jnp.exp(m_s[...] - m_new)
            l_s[...] = alpha * l_s[...] + jnp.sum(p, axis=-1, keepdims=True)
            acc[...] = alpha * acc[...] + _bdot(p, v_ref[...])
            m_s[...] = m_new

        _on_and_below_diagonal(i, j, step)

        @pl.when(j == nb - 1)
        def _():
            o_ref[...] = (acc[...] / l_s[...]).astype(o_ref.dtype)
            lse_ref[...] = m_s[...] + jnp.log(l_s[...])

    kmap = lambda off: (lambda h, i, j: (jnp.minimum(j, i), off + h))
    call = _pcall(body, name="mla_flash_fwd", grid=(H, nb, nb),
                  in_specs=[pl.BlockSpec((t, 2 * LANES), lambda h, i, j: (i, h)),
                            pl.BlockSpec((t, LANES), kmap(0)), pl.BlockSpec((t, LANES), kmap(H)),
                            pl.BlockSpec((t, LANES), lambda h, i, j: (jnp.minimum(j, i), 0))],
                  out_specs=[pl.BlockSpec((t, LANES), lambda h, i, j: (i, h)),
                             pl.BlockSpec((None, t, 1), lambda h, i, j: (h, i, 0))],
                  out_shape=[jax.ShapeDtypeStruct((S, H * MLA_V), BF16), jax.ShapeDtypeStruct((H, S, 1), F32)],
                  scratch_shapes=[pltpu.VMEM((t, 1), F32), pltpu.VMEM((t, 1), F32), pltpu.VMEM((t, MLA_V), F32)],
                  dims=("parallel", "parallel", "arbitrary"), hosted=hosted)
    return _hosting(call, hosted, qr, kv, kv, kpe)


def _mla_p_ds(q, kn, v, kpe, do, o, lse, diagonal):
    p = jnp.exp(_mla_scores(q, kn, kpe, diagonal) - lse)
    dof = do.astype(F32)
    delta = jnp.sum(dof * o.astype(F32), axis=-1, keepdims=True)
    dp = lax.dot_general(do.astype(BF16), v, _DN["nt"], preferred_element_type=F32)
    ds = p * (dp - delta) * MLA_SCALE
    return p, ds


def flash_bwd(qr, kv, kpe, o, lse, dy, dy_col0, *, t=FLASH_T, hosted=None):
    S = qr.shape[0]
    t = min(t, S)
    nb = S // t
    H = MLA_HEADS

    def body(q_ref, kn_ref, v_ref, kpe_ref, o_ref, lse_ref, do_ref, dq_ref, dkn_ref, dv_ref, dkpe_ref, acc, akn, av):
        h, i, j = pl.program_id(0), pl.program_id(1), pl.program_id(2)

        @pl.when((h == 0) & (i == 0) & (j == 0))
        def _():
            dkpe_ref[...] = jnp.zeros_like(dkpe_ref)

        @pl.when((i == 0) & (j == 0))
        def _():
            akn[...] = jnp.zeros_like(akn)
            av[...] = jnp.zeros_like(av)

        @pl.when(j == 0)
        def _():
            acc[...] = jnp.zeros_like(acc)

        def step(diagonal):
            q = q_ref[...]
            p, ds = _mla_p_ds(q, kn_ref[...], v_ref[...], kpe_ref[...], do_ref[...], o_ref[...], lse_ref[...],
                              diagonal)
            acc[...] += jnp.concatenate([_bdot(ds, kn_ref[...]), _bdot(ds, kpe_ref[...])], axis=1)
            av[j] += _bdot(p, do_ref[...], _DN["tn"])
            akn[j] += _bdot(ds, q[:, :LANES], _DN["tn"])
            rows = pl.ds(pl.multiple_of(j * t, t), t)
            dkpe_ref[rows, :] += _bdot(ds, q[:, LANES:], _DN["tn"])

        _on_and_below_diagonal(i, j, step)

        @pl.when(j == nb - 1)
        def _():
            dq_ref[...] = acc[...]

        @pl.when((i == nb - 1) & (j == nb - 1))
        def _():
            dkn_ref[...] = akn[...].reshape(S, LANES).astype(dkn_ref.dtype)
            dv_ref[...] = av[...].reshape(S, LANES).astype(dv_ref.dtype)

    kmap = lambda off: (lambda h, i, j: (jnp.minimum(j, i), off + h))
    head_col = pl.BlockSpec((S, LANES), lambda h, i, j: (0, h))
    call = _pcall(body, name="mla_flash_bwd", grid=(H, nb, nb),
                  in_specs=[pl.BlockSpec((t, 2 * LANES), lambda h, i, j: (i, h)),
                            pl.BlockSpec((t, LANES), kmap(0)), pl.BlockSpec((t, LANES), kmap(H)),
                            pl.BlockSpec((t, LANES), lambda h, i, j: (jnp.minimum(j, i), 0)),
                            pl.BlockSpec((t, LANES), lambda h, i, j: (i, h)),
                            pl.BlockSpec((None, t, 1), lambda h, i, j: (h, i, 0)),
                            pl.BlockSpec((t, LANES), lambda h, i, j: (i, dy_col0 + h))],
                  out_specs=[pl.BlockSpec((t, 2 * LANES), lambda h, i, j: (i, h)), head_col, head_col,
                             pl.BlockSpec((S, LANES), lambda h, i, j: (0, 0))],
                  out_shape=[jax.ShapeDtypeStruct((S, H * 2 * LANES), F32), jax.ShapeDtypeStruct((S, H * LANES), BF16),
                             jax.ShapeDtypeStruct((S, H * LANES), BF16), jax.ShapeDtypeStruct((S, LANES), F32)],
                  scratch_shapes=[pltpu.VMEM((t, 2 * LANES), F32), pltpu.VMEM((nb, t, LANES), F32),
                                  pltpu.VMEM((nb, t, LANES), F32)],
                  dims=("arbitrary", "arbitrary", "arbitrary"), hosted=hosted)
    (dq, dkn, dv, dkpe), extra = _hosting(call, hosted, qr, kv, kv, kpe, o, lse, dy)
    return (dq, jnp.concatenate([dkn, dv], axis=1), dkpe), extra


def loss_head(h, target, g, *, tr=256):
    S, D = h.shape
    tr = min(tr, S)

    def body(h_ref, t_ref, g_ref, loss_ref, dh_ref, dg_ref):
        tgt = t_ref[...]

        def f(hh, gg):
            err = jnp.square(_rms(hh, gg) - tgt)
            per_row = jnp.sum(err, axis=-1, keepdims=True) * (0.5 / D)
            return jnp.sum(per_row, axis=0, keepdims=True)

        val, vjp = jax.vjp(f, h_ref[...], g_ref[...])
        dh, dg = vjp(jnp.ones((1, 1), F32))
        dh_ref[...] = dh

        @pl.when(pl.program_id(0) == 0)
        def _():
            loss_ref[...] = jnp.zeros_like(loss_ref)
            dg_ref[...] = jnp.zeros_like(dg_ref)

        loss_ref[...] += jnp.broadcast_to(val, loss_ref.shape)
        dg_ref[...] += dg

    return _pcall(body, name="loss_head", grid=(S // tr,),
                  in_specs=[_row_spec(tr, D, 0), _row_spec(tr, D, 0), _full_spec(g)],
                  out_specs=[pl.BlockSpec((1, LANES), lambda i: (0, 0)), _row_spec(tr, D, 0), _full_spec(g)],
                  out_shape=[jax.ShapeDtypeStruct((1, LANES), F32), jax.ShapeDtypeStruct((S, D), F32),
                             jax.ShapeDtypeStruct(g.shape, F32)],
                  dims=("arbitrary",))(h, target, g)


def _rope_tables(positions, dim):
    inv_freq = ROPE_THETA ** (-jnp.arange(0, dim, 2, dtype=F32) / dim)
    ang = positions.astype(F32)[:, None] * inv_freq
    return jnp.cos(ang), jnp.sin(ang)


def _pad_cols(w, n):
    return jnp.pad(w, ((0, 0), (0, n - w.shape[1])))


def _prep_w_in0(w):
    return jnp.concatenate([w[:, :4096], w[:, 4112:5136], _pad_cols(w[:, 5136:5200], LANES),
                            _pad_cols(w[:, 4096:4112], LANES)], axis=1)


def _unprep_w_in0(g):
    return jnp.concatenate([g[:, :4096], g[:, ZIN_AB:ZIN_AB + 16], g[:, ZIN_CQ:ZIN_KR], g[:, ZIN_KR:ZIN_KR + MLA_ROPE]],
                           axis=1)


def _prep_w_uq(w):
    w = w.reshape(MLA_Q_RANK, MLA_HEADS, MLA_NOPE + MLA_ROPE)
    w = jnp.pad(w, ((0, 0), (0, 0), (0, 2 * LANES - MLA_NOPE - MLA_ROPE)))
    return w.reshape(MLA_Q_RANK, MLA_HEADS * 2 * LANES)


def _unprep_w_uq(g):
    g = g.reshape(MLA_Q_RANK, MLA_HEADS, 2 * LANES)[:, :, :MLA_NOPE + MLA_ROPE]
    return g.reshape(MLA_Q_RANK, MLA_HEADS * (MLA_NOPE + MLA_ROPE))


def _prep_w_ukv(w):
    w = w.reshape(MLA_KV_RANK, MLA_HEADS, 2, LANES)
    return jnp.transpose(w, (0, 2, 1, 3)).reshape(MLA_KV_RANK, 2 * MLA_HEADS * LANES)


def _unprep_w_ukv(g):
    g = g.reshape(MLA_KV_RANK, 2, MLA_HEADS, LANES)
    return jnp.transpose(g, (0, 2, 1, 3)).reshape(MLA_KV_RANK, 2 * MLA_HEADS * LANES)


def _row(v, n=None):
    v = v.reshape(1, -1).astype(F32)
    return v if n is None else _pad_cols(v, n)


def _ffn_fwd(h, norm_g, w_up, conv_w, conv_b, w_down, tag, hosted=None):
    (hn,) = rowwise(_fn_rms, [(h, D_MODEL, 0)], [norm_g], [], [(D_MODEL, BF16)], name=f"{tag}_ffn_norm")
    u = matmul(hn, w_up, "nn", F32, b_shards=N_CHIPS, **TILES["wide_nn"], name=f"{tag}_ffn_up", hosted=hosted)
    u, got = u if hosted is not None else (u, [])
    f = ffn_conv_fwd(u, conv_w, conv_b)
    h_out = matmul(f, w_down, "nn", F32, add=h, tm=512, tn=1024, tk=8192, name=f"{tag}_ffn_down")
    return h_out, (hn, u, f), got


def _ffn_bwd(dh, dh16, h, norm_g, w_up, conv_w, conv_b, w_down, saved, tag, make_hosted=None):
    hn, u, f = saved
    df = matmul(dh16, w_down, "nt", BF16, tm=512, tn=2816, name=f"{tag}_ffn_down_dx")
    g_down = matmul(f, dh16, "tn", BF16, **TILES["dw"], name=f"{tag}_ffn_down_dw")
    du_g, du_u, g_conv_w, g_conv_b = ffn_conv_bwd(u, conv_w, conv_b, df)
    pair = N_CHIPS // 2
    g_up = jnp.concatenate([matmul(hn, du_, "tn", BF16, out_shards=pair, tm=1024, tn=1408, tk=4096,
                                   name=f"{tag}_ffn_up_dw_{part}") for part, du_ in (("gate", du_g), ("up", du_u))])
    hosted = make_hosted(dict(ffn_w_up=g_up, ffn_w_down=g_down)) if make_hosted else None
    dhn_gate = matmul(du_g, w_up, "nt", F32, b_shards=pair, tm=512, tn=2048, tk=2816, name=f"{tag}_ffn_up_dx_gate")
    dhn = matmul(du_u, w_up, "nt", F32, b_shards=pair, b_shard0=pair, add=dhn_gate, tm=512, tn=2048, tk=2816,
                 name=f"{tag}_ffn_up_dx_up", hosted=hosted)
    dhn, got = dhn if hosted is not None else (dhn, [])
    ((dh_in, dh_in16),), (g_norm,) = rowwise_bwd(_fn_rms, [(h, D_MODEL, 0)], [norm_g], [], [(dhn, D_MODEL, 0)],
                                                 [(F32, BF16)], adds=[(dh, D_MODEL, 0)], name=f"{tag}_ffn_norm_bwd")
    return dh_in, dh_in16, dict(ffn_norm=g_norm, ffn_w_up=g_up, ffn_conv_w=g_conv_w, ffn_conv_b=g_conv_b,
                                ffn_w_down=g_down), got


TILES = {"wide_nn": dict(tm=512, tn=3072, tk=2048),
         "square": dict(tm=512, tn=2048, tk=2048),
         "dw": dict(tm=512, tn=2048, tk=4096)}


def _ple_fwd(h, p_i, w_proj, gate_g, w_gate, tag):
    (hg,) = rowwise(_fn_rms, [(h, D_MODEL, 0)], [gate_g], [], [(D_MODEL, BF16)], name=f"{tag}_ple_norm")
    gl = matmul(hg, w_gate, "nn", F32, **TILES["square"], name=f"{tag}_ple_gate")
    pp = matmul(p_i, w_proj, "nn", F32, b_shards=N_CHIPS, name=f"{tag}_ple_proj")
    (h_out,) = rowwise(_fn_ple, [(h, D_MODEL, 0), (pp, D_MODEL, 0), (gl, D_MODEL, 0)], [], [], [(D_MODEL, F32)],
                       name=f"{tag}_ple_add")
    return h_out, (hg, gl, pp)


def _ple_bwd(dh, h, p_i, w_proj, gate_g, w_gate, saved, tag):
    hg, gl, pp = saved
    (dpp, dgl), _ = rowwise_bwd(_fn_ple_terms, [(pp, D_MODEL, 0), (gl, D_MODEL, 0)], [], [], [(dh, D_MODEL, 0)],
                                [BF16, BF16], name=f"{tag}_ple_add_bwd")
    g_proj = matmul(p_i, dpp, "tn", BF16, out_shards=N_CHIPS, name=f"{tag}_ple_proj_dw")
    g_gate = matmul(hg, dgl, "tn", BF16, **TILES["dw"], name=f"{tag}_ple_gate_dw")
    dh_in, dh_in16, g_norm = matmul(dgl, w_gate, "nt", F32, tm=256, tn=2048, tk=2048, name=f"{tag}_ple_gate_dx",
                                    norm_bwd=(h, gate_g, dh))
    return dh_in, dh_in16, dict(ple_proj=g_proj, ple_gate_norm=g_norm, ple_gate=g_gate)


def local_step(x, p, positions, target, slots, W, ex=None):
    S = x.shape[0]
    G = {}
    p0, p1 = p[0].astype(BF16), p[1].astype(BF16)
    W = dict(W)

    def use(names, bufs):
        for k, b in zip(names, bufs):
            r, c = b.shape[1:]
            W[k] = b.reshape(N_CHIPS * r, c) if k in ROW_SHARDED else (b if k in KEPT_SHARDED else _cols_to_full(b))

    def by_chip(names):
        out = []
        for k in names:
            r, c = slots[k].shape[1:]
            out.append(G[k].reshape(N_CHIPS, r, c) if k in ROW_SHARDED
                       else (G[k] if k in KEPT_SHARDED else _full_to_cols(G[k])))
        return out

    first = [slots[k] for k in GATHER_FIRST]
    use(GATHER_FIRST, ex.gather(first) if ex else first)

    cm, sm = _rope_tables(positions, MLA_ROPE)
    zeros = jnp.zeros((S, LANES - MLA_ROPE), F32)
    cosp = jnp.concatenate([cm, cm, zeros], axis=1)
    sinp = jnp.concatenate([sm, sm, zeros], axis=1)
    cr, sr = _rope_tables(positions, RET_DK)

    w_in0 = _prep_w_in0(W["l0_w_in"])
    a_row = _row(W["l0_gdn_A_log"], LANES)
    dt_row = _row(W["l0_gdn_dt_bias"], LANES)
    gdn_nw = _row(W["l0_gdn_norm"])
    n = {k: _row(W[k]) for k in ("l0_attn_norm", "l0_mla_q_norm", "l0_mla_kv_norm", "l0_ffn_norm",
                                 "l0_ple_gate_norm", "l1_attn_norm", "l1_ret_norm", "l1_ffn_norm",
                                 "l1_ple_gate_norm", "final_norm", "l0_ffn_conv_b", "l1_ffn_conv_b")}

    (hn0,) = rowwise(_fn_rms, [(x, D_MODEL, 0)], [n["l0_attn_norm"]], [], [(D_MODEL, BF16)], name="l0_attn_norm")
    zin = matmul(hn0, w_in0, "nn", F32, tm=512, tn=1792, name="l0_w_in")
    qkv = gdn_conv_fwd(zin, W["l0_gdn_conv"])
    layer0 = [slots[k] for k in GATHER_L0]
    (y_a, gdn_states), got = gdn_fwd(qkv, zin, a_row, dt_row, gdn_nw, hosted=hosted_gather(layer0) if ex else None)
    use(GATHER_L0, got if ex else layer0)
    w_uq = _prep_w_uq(W["l0_mla_w_uq"])
    w_ukv = _prep_w_ukv(W["l0_mla_w_ukv"])
    mla_rows = [(zin, MLA_Q_RANK, ZIN_CQ // MLA_Q_RANK), (zin, MLA_KV_RANK, ZIN_CKV // MLA_KV_RANK),
                (zin, LANES, ZIN_KR // LANES)]
    mla_nd = [(cosp, LANES, 0), (sinp, LANES, 0)]
    cqn, ckvn, kpe = rowwise(_fn_mla_pre, mla_rows, [n["l0_mla_q_norm"], n["l0_mla_kv_norm"]], mla_nd,
                             [(MLA_Q_RANK, BF16), (MLA_KV_RANK, BF16), (LANES, BF16)], name="mla_pre")
    q_lin = matmul(cqn, w_uq, "nn", F32, name="mla_w_uq")
    kv = matmul(ckvn, w_ukv, "nn", BF16, name="mla_w_ukv")
    (qr,) = rowwise(_fn_rope_q, [(q_lin, 2048, 0)], [], mla_nd, [(2048, BF16)],
                    name="mla_rope_q")
    layer1 = [slots[k] for k in GATHER_L1]
    (y_b, lse), got = flash_fwd(qr, kv, kpe, hosted=hosted_gather(layer1) if ex else None)
    use(GATHER_L1, got if ex else layer1)
    y_ab = jnp.concatenate([y_a, y_b], axis=1)
    h1 = matmul(y_ab, W["l0_w_out"], "nn", F32, add=x, **TILES["square"], name="l0_w_out")
    ffn_late = [slots[k] for k in GATHER_FFN]
    h2, ffn0, got = _ffn_fwd(h1, n["l0_ffn_norm"], W["l0_ffn_w_up"], W["l0_ffn_conv_w"], n["l0_ffn_conv_b"],
                             W["l0_ffn_w_down"], "l0", hosted=hosted_gather(ffn_late) if ex else None)
    use(GATHER_FFN, got if ex else ffn_late)
    h3, ple0 = _ple_fwd(h2, p0, W["l0_ple_proj"], n["l0_ple_gate_norm"], W["l0_ple_gate"], "l0")

    (hn1,) = rowwise(_fn_rms, [(h3, D_MODEL, 0)], [n["l1_attn_norm"]], [], [(D_MODEL, BF16)], name="l1_attn_norm")
    late = [slots[k] for k in GATHER_L1_IN]
    zz = matmul(hn1, W["l1_w_in"], "nn", F32, b_shards=N_CHIPS, **TILES["wide_nn"], name="l1_w_in",
                hosted=hosted_gather(late) if ex else None)
    zz, got = zz if ex else (zz, late)
    use(GATHER_L1_IN, got)
    (o_ret, ret_states), _ = ret_fwd(zz, cr, sr)
    gate_rows = [(zz, 4096, 2), (o_ret, 4096, 0)]
    (yg,) = rowwise(_fn_ret_gate, gate_rows, [n["l1_ret_norm"]], [], [(4096, BF16)], name="ret_gate")
    h4 = matmul(yg, W["l1_w_out"], "nn", F32, add=h3, tm=512, tn=2048, tk=4096, name="l1_w_out")
    h5, ffn1, _ = _ffn_fwd(h4, n["l1_ffn_norm"], W["l1_ffn_w_up"], W["l1_ffn_conv_w"], n["l1_ffn_conv_b"],
                           W["l1_ffn_w_down"], "l1")
    h6, ple1 = _ple_fwd(h5, p1, W["l1_ple_proj"], n["l1_ple_gate_norm"], W["l1_ple_gate"], "l1")

    loss_vec, dh, G["final_norm"] = loss_head(h6, target, n["final_norm"])

    dh, dh16, g = _ple_bwd(dh, h5, p1, W["l1_ple_proj"], n["l1_ple_gate_norm"], W["l1_ple_gate"], ple1, "l1")
    G.update({"l1_" + k: v for k, v in g.items()})
    dh, dh16, g, _ = _ffn_bwd(dh, dh16, h4, n["l1_ffn_norm"], W["l1_ffn_w_up"], W["l1_ffn_conv_w"],
                              n["l1_ffn_conv_b"], W["l1_ffn_w_down"], ffn1, "l1")
    G.update({"l1_" + k: v for k, v in g.items()})

    dyg = matmul(dh16, W["l1_w_out"], "nt", F32, tm=512, tn=4096, name="l1_w_out_dx")
    G["l1_w_out"] = matmul(yg, dh16, "tn", BF16, **TILES["dw"], name="l1_w_out_dw")
    (dg, do_ret), (G["l1_ret_norm"],) = rowwise_bwd(_fn_ret_gate, gate_rows, [n["l1_ret_norm"]], [],
                                                   [(dyg, 4096, 0)], [BF16, F32], name="ret_gate_bwd")
    dq, dk, dv = ret_bwd(zz, cr, sr, ret_states, do_ret)
    dzz = jnp.concatenate([dq, dk, dv, dg], axis=1)
    G["l1_w_in"] = matmul(hn1, dzz, "tn", BF16, out_shards=N_CHIPS, tm=1024, tn=1536, tk=4096, name="l1_w_in_dw")
    sums, landed = {}, {}
    grads_l1 = by_chip(REDUCE_L1)
    dhn = matmul(dzz, W["l1_w_in"], "nt", F32, b_shards=N_CHIPS, tm=1024, tn=1024, tk=3072, name="l1_w_in_dx",
                 hosted=hosted_swap(grads_l1) if ex else None)
    if ex:
        dhn, swapped = dhn
        sums.update(zip(REDUCE_L1, ex.pair_sums(REDUCE_L1, grads_l1, swapped)))
    (dh,), (G["l1_attn_norm"],) = rowwise_bwd(_fn_rms, [(h3, D_MODEL, 0)], [n["l1_attn_norm"]], [],
                                             [(dhn, D_MODEL, 0)], [F32], adds=[(dh, D_MODEL, 0)],
                                             name="l1_attn_norm_bwd")

    dh, dh16, g = _ple_bwd(dh, h2, p0, W["l0_ple_proj"], n["l0_ple_gate_norm"], W["l0_ple_gate"], ple0, "l0")
    G.update({"l0_" + k: v for k, v in g.items()})
    mid = REDUCE_DQ + REDUCE_L0

    def swap_mid(g_ffn):
        G.update({"l0_" + k: v for k, v in g_ffn.items()})
        return hosted_swap(by_chip(mid))

    dh, dh16, g, swapped = _ffn_bwd(dh, dh16, h1, n["l0_ffn_norm"], W["l0_ffn_w_up"], W["l0_ffn_conv_w"],
                                    n["l0_ffn_conv_b"], W["l0_ffn_w_down"], ffn0, "l0",
                                    make_hosted=swap_mid if ex else None)
    G.update({"l0_" + k: v for k, v in g.items()})
    if ex:
        sums.update(zip(mid, ex.pair_sums(mid, by_chip(mid), swapped)))

    dy_ab = matmul(dh16, W["l0_w_out"], "nt", F32, **TILES["square"], name="l0_w_out_dx")
    G["l0_w_out"] = matmul(y_ab, dh16, "tn", BF16, **TILES["dw"], name="l0_w_out_dw")
    (dq, dk, dv, dz, dab, g_a, g_dt, G["l0_gdn_norm"]), got = gdn_bwd(
        qkv, zin, a_row, dt_row, gdn_nw, gdn_states, dy_ab, 0,
        hosted=hosted_scatter([sums[k] for k in REDUCE_L1]) if ex else None)
    landed.update(zip(REDUCE_L1, got))
    G["l0_gdn_A_log"], G["l0_gdn_dt_bias"] = g_a[:, :GDN_HEADS], g_dt[:, :GDN_HEADS]
    dqkv, G["l0_gdn_conv"] = gdn_conv_bwd(zin, W["l0_gdn_conv"], jnp.concatenate([dq, dk, dv], axis=1))
    (dqr, dkv, dkpe), got = flash_bwd(qr, kv, kpe, y_b, lse, dy_ab, MLA_HEADS,
                                      hosted=hosted_scatter([sums[k] for k in mid]) if ex else None)
    landed.update(zip(mid, got))
    (dq_lin,), _ = rowwise_bwd(_fn_rope_q, [(q_lin, 2048, 0)], [], mla_nd, [(dqr, 2048, 0)], [BF16],
                               name="mla_rope_q_bwd")
    G["l0_mla_w_uq"] = _unprep_w_uq(matmul(cqn, dq_lin, "tn", BF16, name="mla_w_uq_dw"))
    dcqn = matmul(dq_lin, w_uq, "nt", F32, name="mla_w_uq_dx")
    G["l0_mla_w_ukv"] = _unprep_w_ukv(matmul(ckvn, dkv, "tn", BF16, name="mla_w_ukv_dw"))
    dckvn = matmul(dkv, w_ukv, "nt", F32, name="mla_w_ukv_dx")
    (dcq, dckv, dkr), (G["l0_mla_q_norm"], G["l0_mla_kv_norm"]) = rowwise_bwd(
        _fn_mla_pre, mla_rows, [n["l0_mla_q_norm"], n["l0_mla_kv_norm"]], mla_nd,
        [(dcqn, MLA_Q_RANK, 0), (dckvn, MLA_KV_RANK, 0), (dkpe, LANES, 0)], [BF16, BF16, BF16], name="mla_pre_bwd")
    dzin = jnp.concatenate([dqkv, dz, dcq, dckv, dkr, dab.astype(BF16)], axis=1)
    early = REDUCE_L1 + REDUCE_DQ + REDUCE_L0
    reduced = {}
    g_in = matmul(hn0, dzin, "tn", BF16, tm=512, tn=1792, tk=4096, name="l0_w_in_dw",
                  hosted=hosted_join(ex.halves(early, sums, landed)) if ex else None)
    if ex:
        g_in, joined = g_in
        reduced.update(zip(early, joined))
    G["l0_w_in"] = _unprep_w_in0(g_in)
    if ex:
        sums.update(zip(REDUCE_LAST, ex.pair_sums(REDUCE_LAST, by_chip(REDUCE_LAST))))
    dhn = matmul(dzin, w_in0, "nt", F32, tm=512, tn=2048, tk=5376, name="l0_w_in_dx",
                 hosted=hosted_scatter([sums[k] for k in REDUCE_LAST]) if ex else None)
    if ex:
        dhn, got = dhn
        landed.update(zip(REDUCE_LAST, got))
    (grad_x,), (G["l0_attn_norm"],) = rowwise_bwd(_fn_rms, [(x, D_MODEL, 0)], [n["l0_attn_norm"]], [],
                                                 [(dhn, D_MODEL, 0)], [F32], adds=[(dh, D_MODEL, 0)],
                                                 name="l0_attn_norm_bwd")
    small = {k: G[k] for k in SMALL}
    if not ex:
        return loss_vec[0, 0], grad_x, dict(zip(BIG, by_chip(BIG))), small
    reduced.update(zip(REDUCE_LAST, pair_join_halves(ex.halves(REDUCE_LAST, sums, landed))))
    return loss_vec[0, 0], grad_x, reduced, small


HBM = pl.BlockSpec(memory_space=pltpu.HBM)
VMEM = pl.BlockSpec(memory_space=pltpu.VMEM)


def _place():
    return lax.axis_index("x"), lax.axis_index("y"), lax.axis_index("c")


def _other_chips(x, y):
    return [(1 - x, y), (x, 1 - y), (1 - x, 1 - y)]


def _comm_call(body, *, name, out_shape, in_specs, out_specs, scratch_shapes):
    return pl.pallas_call(body, name=name, out_shape=out_shape, in_specs=in_specs, out_specs=out_specs,
                          scratch_shapes=list(scratch_shapes),
                          compiler_params=pltpu.CompilerParams(vmem_limit_bytes=VMEM_LIMIT_MB << 20))


def _inplace_comm_call(body, bufs, *, name, n_sems):
    n = len(bufs)
    return pl.pallas_call(body, name=name, out_shape=[jax.ShapeDtypeStruct(b.shape, b.dtype) for b in bufs],
                          in_specs=[HBM] * n, out_specs=[HBM] * n, input_output_aliases={i: i for i in range(n)},
                          scratch_shapes=[pltpu.SemaphoreType.DMA((n_sems,)), pltpu.SemaphoreType.DMA((n_sems,))],
                          compiler_params=pltpu.CompilerParams(vmem_limit_bytes=VMEM_LIMIT_MB << 20))(*bufs)


def all_gather_chips(bufs):
    n_sems, start, finish = _gather_phase(len(bufs))
    n = len(bufs)

    def body(*refs):
        outs, send_sems, recv_sems = refs[n:2 * n], refs[2 * n], refs[2 * n + 1]
        start(None, outs, send_sems, recv_sems)
        finish(None, outs, send_sems, recv_sems)

    return _inplace_comm_call(body, bufs, name="all_gather_chips", n_sems=n_sems)


def _gather_phase(n):
    def plan(outs, send_sems, recv_sems):
        x, y, c = _place()

        def copy(w, k, chip, hc, to):
            half = outs[w].shape[1] // 2
            rows = outs[w].at[2 * chip[0] + chip[1], pl.ds(hc * half, half), :]
            return pltpu.make_async_remote_copy(src_ref=rows, dst_ref=rows, send_sem=send_sems.at[6 * w + k],
                                                recv_sem=recv_sems.at[6 * w + k], device_id=to, device_id_type=MESH)

        first = [[copy(w, k, (x, y), c, (*chip, c)) for k, chip in enumerate(_other_chips(x, y))] for w in range(n)]
        passed = [[copy(w, 3 + k, chip, c, (x, y, 1 - c)) for k, chip in enumerate(_other_chips(x, y))]
                  for w in range(n)]
        return copy, first, passed, (x, y, c)

    def start(_, outs, send_sems, recv_sems):
        _, first, _, _ = plan(outs, send_sems, recv_sems)
        for w in range(n):
            for cp in first[w]:
                cp.start()

    def finish(_, outs, send_sems, recv_sems):
        copy, first, passed, (x, y, c) = plan(outs, send_sems, recv_sems)
        chips = _other_chips(x, y)
        for w in range(n):
            for k, chip in enumerate(chips):
                copy(w, k, chip, c, (x, y, c)).wait_recv()
                passed[w][k].start()
        for w in range(n):
            for k, chip in enumerate(chips):
                copy(w, 3 + k, chip, 1 - c, (x, y, c)).wait_recv()
        for w in range(n):
            for cp in first[w] + passed[w]:
                cp.wait_send()

    return 6 * n, start, finish


def hosted_gather(bufs):
    n_sems, start, finish = _gather_phase(len(bufs))
    return Hosted(bufs, [jax.ShapeDtypeStruct(b.shape, b.dtype) for b in bufs], {i: i for i in range(len(bufs))},
                  n_sems, start, finish)


def pair_swap_halves(gs):
    n = len(gs)
    n_sems, start, finish = _swap_phase(n)

    def body(*refs):
        g_refs, o_refs, send_sems, recv_sems = refs[:n], refs[n:2 * n], refs[2 * n], refs[2 * n + 1]
        start(g_refs, o_refs, send_sems, recv_sems)
        finish(g_refs, o_refs, send_sems, recv_sems)

    return _comm_call(body, name="pair_swap_halves", out_shape=_swap_shapes(gs), in_specs=[HBM] * n, out_specs=[HBM] * n,
                      scratch_shapes=[pltpu.SemaphoreType.DMA((n_sems,)), pltpu.SemaphoreType.DMA((n_sems,))])(*gs)


def _swap_shapes(gs):
    return [jax.ShapeDtypeStruct((N_CHIPS, g.shape[1] // 2, g.shape[2]), g.dtype) for g in gs]


def _swap_phase(n):
    def copies(g_refs, o_refs, send_sems, recv_sems):
        x, y, c = _place()
        out = []
        for w in range(n):
            half = g_refs[w].shape[1] // 2
            out.append(pltpu.make_async_remote_copy(
                src_ref=g_refs[w].at[:, pl.ds((1 - c) * half, half), :], dst_ref=o_refs[w], send_sem=send_sems.at[w],
                recv_sem=recv_sems.at[w], device_id=(x, y, 1 - c), device_id_type=MESH))
        return out

    def start(*refs):
        for cp in copies(*refs):
            cp.start()

    def finish(*refs):
        for cp in copies(*refs):
            cp.wait()

    return n, start, finish


def hosted_swap(gs):
    n_sems, start, finish = _swap_phase(len(gs))
    return Hosted(gs, _swap_shapes(gs), {}, n_sems, start, finish)


def scatter_chips(ps):
    n = len(ps)
    n_sems, start, finish = _scatter_phase(n)

    def body(*refs):
        p_refs, o_refs, send_sems, recv_sems = refs[:n], refs[n:2 * n], refs[2 * n], refs[2 * n + 1]
        start(p_refs, o_refs, send_sems, recv_sems)
        finish(p_refs, o_refs, send_sems, recv_sems)

    return _comm_call(body, name="scatter_chips", out_shape=_scatter_shapes(ps), in_specs=[HBM] * n, out_specs=[HBM] * n,
                      scratch_shapes=[pltpu.SemaphoreType.DMA((n_sems,)), pltpu.SemaphoreType.DMA((n_sems,))])(*ps)


def _scatter_shapes(ps):
    return [jax.ShapeDtypeStruct((3,) + p.shape[1:], p.dtype) for p in ps]


def _scatter_phase(n):
    def copies(p_refs, o_refs, send_sems, recv_sems):
        x, y, c = _place()
        return [pltpu.make_async_remote_copy(src_ref=p_refs[w].at[2 * chip[0] + chip[1]], dst_ref=o_refs[w].at[k],
                                             send_sem=send_sems.at[3 * w + k], recv_sem=recv_sems.at[3 * w + k],
                                             device_id=(*chip, c), device_id_type=MESH)
                for w in range(n) for k, chip in enumerate(_other_chips(x, y))]

    def start(*refs):
        for cp in copies(*refs):
            cp.start()

    def finish(*refs):
        for cp in copies(*refs):
            cp.wait()

    return 3 * n, start, finish


def hosted_scatter(ps):
    n_sems, start, finish = _scatter_phase(len(ps))
    return Hosted(ps, _scatter_shapes(ps), {}, n_sems, start, finish)


def pair_join_halves(rs):
    n = len(rs)
    n_sems, start, finish = _join_phase(n)

    def body(*refs):
        outs, send_sems, recv_sems = refs[n:2 * n], refs[2 * n], refs[2 * n + 1]
        start(None, outs, send_sems, recv_sems)
        finish(None, outs, send_sems, recv_sems)

    return _inplace_comm_call(body, rs, name="pair_join_halves", n_sems=n_sems)


def _join_phase(n):
    def copies(_, outs, send_sems, recv_sems):
        x, y, c = _place()
        out = []
        for w in range(n):
            half = outs[w].shape[0] // 2
            rows = outs[w].at[pl.ds(c * half, half), :]
            out.append(pltpu.make_async_remote_copy(src_ref=rows, dst_ref=rows, send_sem=send_sems.at[w],
                                                    recv_sem=recv_sems.at[w], device_id=(x, y, 1 - c),
                                                    device_id_type=MESH))
        return out

    def start(*refs):
        for cp in copies(*refs):
            cp.start()

    def finish(*refs):
        for cp in copies(*refs):
            cp.wait()

    return n, start, finish


def hosted_join(rs):
    n_sems, start, finish = _join_phase(len(rs))
    return Hosted(rs, [jax.ShapeDtypeStruct(r.shape, r.dtype) for r in rs], {i: i for i in range(len(rs))},
                  n_sems, start, finish)


def all_reduce_small(v, name):
    n, L = v.shape
    n_dev = 8

    def body(v_ref, out_ref, buf, send_sems, recv_sems):
        x, y, c = _place()
        me = 4 * x + 2 * y + c
        buf[me] = v_ref[...]

        def copy(k, slot, peer):
            return pltpu.make_async_remote_copy(src_ref=v_ref, dst_ref=buf.at[slot], send_sem=send_sems.at[k],
                                                recv_sem=recv_sems.at[slot],
                                                device_id=(peer // 4, (peer // 2) % 2, peer % 2), device_id_type=MESH)

        sends = [copy(k - 1, me, (me + k) % n_dev) for k in range(1, n_dev)]
        for cp in sends:
            cp.start()
        for k in range(1, n_dev):
            src = (me + k) % n_dev
            copy(0, src, src).wait_recv()
        for cp in sends:
            cp.wait_send()
        acc = buf[0]
        for s in range(1, n_dev):
            acc = acc + buf[s]
        out_ref[...] = acc

    return _comm_call(body, name=name, out_shape=jax.ShapeDtypeStruct((n, L), v.dtype), in_specs=[VMEM], out_specs=VMEM,
                      scratch_shapes=[pltpu.VMEM((n_dev, n, L), v.dtype), pltpu.SemaphoreType.DMA((n_dev - 1,)),
                                      pltpu.SemaphoreType.DMA((n_dev,))])(v)


BF16_ROWS = 16
STREAM_BLOCK_BYTES = 4 << 20


def _rows_tile(n, row_bytes, budget=1 << 20, mult=SUBLANES):
    best = mult if n % mult == 0 else n
    for t in range(mult, n + 1, mult):
        if n % t == 0 and t * row_bytes <= budget:
            best = t
    return best


def _scalars(*vals):
    return jnp.stack([jnp.asarray(v, jnp.int32) for v in vals])


def cast_to_slot(w, chip, name):
    r, c = w.shape
    tb = _rows_tile(r, c * 4, budget=STREAM_BLOCK_BYTES, mult=BF16_ROWS)

    def body(s_ref, w_ref, o_ref):
        o_ref[...] = w_ref[...].astype(BF16)

    spec = pltpu.PrefetchScalarGridSpec(
        num_scalar_prefetch=1, grid=(r // tb,), in_specs=[pl.BlockSpec((tb, c), lambda i, s: (i, 0))],
        out_specs=pl.BlockSpec((None, tb, c), lambda i, s: (s[0], i, 0)))
    return pl.pallas_call(body, name=name, grid_spec=spec, out_shape=jax.ShapeDtypeStruct((N_CHIPS, r, c), BF16),
                          compiler_params=pltpu.CompilerParams(dimension_semantics=("parallel",)))(_scalars(chip), w)


def pair_add(g, got, c, name):
    _, r, w = g.shape
    half = r // 2
    tb = _rows_tile(half, w * 4, budget=STREAM_BLOCK_BYTES, mult=BF16_ROWS)
    nb = half // tb

    def body(c_ref, g_ref, got_ref, o_ref):
        o_ref[...] = (g_ref[...].astype(F32) + got_ref[...].astype(F32)).astype(o_ref.dtype)

    spec = pltpu.PrefetchScalarGridSpec(
        num_scalar_prefetch=1, grid=(N_CHIPS, nb),
        in_specs=[pl.BlockSpec((None, tb, w), lambda s, i, c_ref: (s, c_ref[0] * nb + i, 0)),
                  pl.BlockSpec((None, tb, w), lambda s, i, c_ref: (s, i, 0))],
        out_specs=pl.BlockSpec((None, tb, w), lambda s, i, c_ref: (s, i, 0)))
    return pl.pallas_call(body, name=name, grid_spec=spec, out_shape=jax.ShapeDtypeStruct((N_CHIPS, half, w), BF16),
                          compiler_params=pltpu.CompilerParams(dimension_semantics=("parallel", "parallel")))(
        _scalars(c), g, got)


def chip_add(p, got, chip, c, name):
    _, h, w = p.shape
    tb = _rows_tile(h, w * 4, budget=STREAM_BLOCK_BYTES, mult=BF16_ROWS)
    nb = h // tb

    def body(s_ref, p_ref, got_ref, o_ref):
        acc = p_ref[...].astype(F32)
        for k in range(3):
            acc = acc + got_ref[k].astype(F32)
        o_ref[...] = acc

    spec = pltpu.PrefetchScalarGridSpec(
        num_scalar_prefetch=1, grid=(nb,),
        in_specs=[pl.BlockSpec((None, tb, w), lambda i, s: (s[0], i, 0)),
                  pl.BlockSpec((3, tb, w), lambda i, s: (0, i, 0))],
        out_specs=pl.BlockSpec((tb, w), lambda i, s: (s[1] * nb + i, 0)))
    return pl.pallas_call(body, name=name, grid_spec=spec, out_shape=jax.ShapeDtypeStruct((2 * h, w), F32),
                          compiler_params=pltpu.CompilerParams(dimension_semantics=("parallel",)))(
        _scalars(chip, c), p, got)


def adamw(w, g, m, v, name, pass_grad=False):
    r, c = w.shape
    tr = _rows_tile(r, c * 4, budget=STREAM_BLOCK_BYTES // 2)
    n_out = 4 if pass_grad else 3

    def body(w_ref, g_ref, m_ref, v_ref, d_ref, m_out, v_out, *g_out):
        gg = g_ref[...]
        if pass_grad:
            g_out[0][...] = gg
        m2 = ADAM_B1 * m_ref[...] + (1.0 - ADAM_B1) * gg
        v2 = ADAM_B2 * v_ref[...] + (1.0 - ADAM_B2) * jnp.square(gg)
        m_hat = m2 / (1.0 - ADAM_B1 ** ADAM_STEP)
        v_hat = v2 / (1.0 - ADAM_B2 ** ADAM_STEP)
        d_ref[...] = -ADAM_LR * (m_hat / (jnp.sqrt(v_hat) + ADAM_EPS) + ADAM_WD * w_ref[...])
        m_out[...] = m2
        v_out[...] = v2

    blk = pl.BlockSpec((tr, c), lambda i: (i, 0))
    return _pcall(body, name=name, grid=(r // tr,), in_specs=[blk] * 4, out_specs=[blk] * n_out,
                  out_shape=[jax.ShapeDtypeStruct((r, c), F32)] * n_out, dims=("parallel",))(w, g, m, v)


WEIGHTS = ["l0_attn_norm", "l0_w_in", "l0_gdn_conv", "l0_gdn_A_log", "l0_gdn_dt_bias", "l0_gdn_norm", "l0_mla_q_norm",
           "l0_mla_w_uq", "l0_mla_kv_norm", "l0_mla_w_ukv", "l0_w_out", "l0_ffn_norm", "l0_ffn_w_up", "l0_ffn_conv_w",
           "l0_ffn_conv_b", "l0_ffn_w_down", "l0_ple_proj", "l0_ple_gate_norm", "l0_ple_gate", "l1_attn_norm",
           "l1_w_in", "l1_ret_norm", "l1_w_out", "l1_ffn_norm", "l1_ffn_w_up", "l1_ffn_conv_w", "l1_ffn_conv_b",
           "l1_ffn_w_down", "l1_ple_proj", "l1_ple_gate_norm", "l1_ple_gate", "final_norm"]
COL_SHARDED = ["l0_w_in", "l0_mla_w_uq", "l0_mla_w_ukv", "l0_ffn_w_up", "l0_ple_proj", "l1_w_in", "l1_ffn_w_up",
               "l1_ple_proj"]
ROW_SHARDED = ["l0_w_out", "l0_ffn_w_down", "l0_ple_gate", "l1_w_out", "l1_ffn_w_down", "l1_ple_gate"]
BIG = [k for k in WEIGHTS if k in COL_SHARDED or k in ROW_SHARDED]
SMALL_SHARDED = ["l0_gdn_conv", "l0_ffn_conv_w", "l1_ffn_conv_w"]
SMALL = [k for k in WEIGHTS if k not in BIG]
KEPT_SHARDED = ["l0_ffn_w_up", "l0_ple_proj", "l1_w_in", "l1_ffn_w_up", "l1_ple_proj"]
GATHER_FIRST = ["l0_w_in"]
GATHER_L0 = ["l0_mla_w_uq", "l0_mla_w_ukv", "l0_w_out", "l0_ffn_w_up", "l0_ffn_w_down", "l0_ple_proj", "l0_ple_gate",
             "l1_w_out"]
GATHER_L1 = ["l1_w_in"]
GATHER_FFN = ["l1_ffn_w_down", "l1_ple_proj", "l1_ple_gate"]
GATHER_L1_IN = ["l1_ffn_w_up"]
REDUCE_L1 = [k for k in BIG if k.startswith("l1_")]
REDUCE_DQ = ["l0_ffn_w_up"]
REDUCE_L0 = ["l0_ffn_w_down", "l0_ple_proj", "l0_ple_gate"]
REDUCE_LAST = ["l0_w_in", "l0_mla_w_uq", "l0_mla_w_ukv", "l0_w_out"]


class Exchange:
    def __init__(self, chip, core):
        self.chip, self.core = chip, core

    def gather(self, bufs):
        return all_gather_chips(bufs)

    def pair_sums(self, names, grads, swapped=None):
        swapped = pair_swap_halves(grads) if swapped is None else swapped
        return [pair_add(g, got, self.core, "rs_pair_add_" + k) for k, g, got in zip(names, grads, swapped)]

    def halves(self, names, sums, landed):
        return [chip_add(sums[k], landed[k], self.chip, self.core, "rs_chip_add_" + k) for k in names]


def _cols_to_full(s):
    j, k, n = s.shape
    return jnp.transpose(s, (1, 0, 2)).reshape(k, j * n)


def _full_to_cols(g):
    k, n4 = g.shape
    return jnp.transpose(g.reshape(k, N_CHIPS, n4 // N_CHIPS), (1, 0, 2))


def _pack_small(vals):
    flat = jnp.concatenate([v.astype(F32).reshape(-1) for v in vals])
    align = SUBLANES * LANES
    flat = jnp.pad(flat, (0, -flat.shape[0] % align))
    return flat.reshape(-1, LANES)


def _unpack_small(rows, shapes):
    flat = rows.reshape(-1)
    out, off = [], 0
    for shp in shapes:
        n = int(np.prod(shp))
        out.append(flat[off:off + n].reshape(shp))
        off += n
    return out


INPUTS = (["x", "p", "positions"] + WEIGHTS + ["loss_target"] + ["m_" + k for k in WEIGHTS]
          + ["v_" + k for k in WEIGHTS])


def kernel(
        x, p, positions, l0_attn_norm, l0_w_in, l0_gdn_conv, l0_gdn_A_log, l0_gdn_dt_bias, l0_gdn_norm, l0_mla_q_norm,
        l0_mla_w_uq, l0_mla_kv_norm, l0_mla_w_ukv, l0_w_out, l0_ffn_norm, l0_ffn_w_up, l0_ffn_conv_w, l0_ffn_conv_b,
        l0_ffn_w_down, l0_ple_proj, l0_ple_gate_norm, l0_ple_gate, l1_attn_norm, l1_w_in, l1_ret_norm, l1_w_out,
        l1_ffn_norm, l1_ffn_w_up, l1_ffn_conv_w, l1_ffn_conv_b, l1_ffn_w_down, l1_ple_proj, l1_ple_gate_norm,
        l1_ple_gate, final_norm, loss_target, m_l0_attn_norm, m_l0_w_in, m_l0_gdn_conv, m_l0_gdn_A_log,
        m_l0_gdn_dt_bias, m_l0_gdn_norm, m_l0_mla_q_norm, m_l0_mla_w_uq, m_l0_mla_kv_norm, m_l0_mla_w_ukv, m_l0_w_out,
        m_l0_ffn_norm, m_l0_ffn_w_up, m_l0_ffn_conv_w, m_l0_ffn_conv_b, m_l0_ffn_w_down, m_l0_ple_proj,
        m_l0_ple_gate_norm, m_l0_ple_gate, m_l1_attn_norm, m_l1_w_in, m_l1_ret_norm, m_l1_w_out, m_l1_ffn_norm,
        m_l1_ffn_w_up, m_l1_ffn_conv_w, m_l1_ffn_conv_b, m_l1_ffn_w_down, m_l1_ple_proj, m_l1_ple_gate_norm,
        m_l1_ple_gate, m_final_norm, v_l0_attn_norm, v_l0_w_in, v_l0_gdn_conv, v_l0_gdn_A_log, v_l0_gdn_dt_bias,
        v_l0_gdn_norm, v_l0_mla_q_norm, v_l0_mla_w_uq, v_l0_mla_kv_norm, v_l0_mla_w_ukv, v_l0_w_out, v_l0_ffn_norm,
        v_l0_ffn_w_up, v_l0_ffn_conv_w, v_l0_ffn_conv_b, v_l0_ffn_w_down, v_l0_ple_proj, v_l0_ple_gate_norm,
        v_l0_ple_gate, v_l1_attn_norm, v_l1_w_in, v_l1_ret_norm, v_l1_w_out, v_l1_ffn_norm, v_l1_ffn_w_up,
        v_l1_ffn_conv_w, v_l1_ffn_conv_b, v_l1_ffn_w_down, v_l1_ple_proj, v_l1_ple_gate_norm, v_l1_ple_gate,
        v_final_norm):
    given = locals()
    a = {k: given[k] for k in INPUTS}
    x_i, y_i, c_i = _place()
    chip = 2 * x_i + y_i
    shard_shapes = {k: a[k].shape for k in WEIGHTS}

    slots = {k: cast_to_slot(a[k], chip, "cast_" + k) for k in BIG}
    W = {}
    placed = []
    for k in SMALL_SHARDED:
        r, c = shard_shapes[k]
        mine = jnp.where(c_i == 0, a[k], jnp.zeros_like(a[k]))
        placed.append(lax.dynamic_update_slice(jnp.zeros((r, N_CHIPS * c), F32), mine, (0, chip * c)))
    full_small = _unpack_small(all_reduce_small(_pack_small(placed), "gather_small_weights"),
                               [p_.shape for p_ in placed])
    for k in SMALL:
        W[k] = a[k]
    W.update(dict(zip(SMALL_SHARDED, full_small)))

    loss_part, grad_x, grads, G = local_step(a["x"][0], a["p"][:, 0], a["positions"][0], a["loss_target"][0], slots, W,
                                             Exchange(chip, c_i))
    loss = lax.psum(loss_part, ("x", "y", "c"))
    deltas, new_m, new_v = {}, {}, {}
    for k in BIG:
        deltas[k], new_m[k], new_v[k], grads[k] = adamw(a[k], grads[k], a["m_" + k], a["v_" + k], "adamw_" + k,
                                                        pass_grad=True)

    small_full = [G[k].reshape(-1) for k in SMALL]
    summed = _unpack_small(all_reduce_small(_pack_small(small_full), "reduce_small_grads"),
                           [G[k].shape for k in SMALL])
    for k, g in zip(SMALL, summed):
        if k in SMALL_SHARDED:
            r, c = shard_shapes[k]
            g = lax.dynamic_slice(g.reshape(r, N_CHIPS * c), (0, chip * c), (r, c))
        grads[k] = g.reshape(shard_shapes[k])
    packed = [_pack_small([d[k] for k in SMALL]) for d in (
        {k: a[k] for k in SMALL}, grads, {k: a["m_" + k] for k in SMALL}, {k: a["v_" + k] for k in SMALL})]
    outs = adamw(*packed, "adamw_small")
    shapes = [shard_shapes[k] for k in SMALL]
    for d, rows in zip((deltas, new_m, new_v), outs):
        d.update(dict(zip(SMALL, _unpack_small(rows, shapes))))

    return (loss, grad_x[None], *[grads[k] for k in WEIGHTS], *[deltas[k] for k in WEIGHTS],
            *[new_m[k] for k in WEIGHTS], *[new_v[k] for k in WEIGHTS])
```

```python
import functools
import math

import numpy as np
import jax
import jax.numpy as jnp
from jax import lax
from jax.experimental import pallas as pl
from jax.experimental.pallas import tpu as pltpu

F32, BF16 = jnp.float32, jnp.bfloat16
HI = lax.Precision.HIGHEST
MESH = pl.DeviceIdType.MESH

NORM_EPS = 1e-6
ROPE_THETA = 10000.0
D_MODEL = 2048
PLE_DIM = 256
GDN_HEADS, GDN_DK, GDN_DV, GDN_CONV = 8, 128, 128, 4
MLA_HEADS, MLA_Q_RANK, MLA_KV_RANK, MLA_NOPE, MLA_ROPE, MLA_V = 8, 512, 512, 128, 64, 128
RET_HEADS, RET_DK, RET_DV = 8, 256, 512
D_FF, FFN_CONV = 5632, 3
ADAM_LR, ADAM_B1, ADAM_B2, ADAM_EPS, ADAM_WD, ADAM_STEP = 0.001, 0.9, 0.999, 1e-08, 0.01, 10

LANES = 128
SUBLANES = 8
CHUNK = 128
N_CHIPS = 4
VMEM_LIMIT_MB = 56

ZIN_QKV, ZIN_Z, ZIN_CQ, ZIN_CKV, ZIN_KR, ZIN_AB, ZIN_W = 0, 3072, 4096, 4608, 5120, 5248, 5376


class Hosted:
    def __init__(self, inputs, out_shapes, aliases, n_sems, start, finish):
        self.inputs, self.out_shapes, self.aliases, self.n_sems = list(inputs), list(out_shapes), dict(aliases), n_sems
        self.start, self.finish = start, finish


def _pcall(body, *, name, out_shape, grid=(), in_specs=None, out_specs=None, scratch_shapes=(), dims=None,
           hosted=None):
    params = dict(vmem_limit_bytes=VMEM_LIMIT_MB << 20)
    if dims is not None:
        params["dimension_semantics"] = dims
    if hosted is None:
        return pl.pallas_call(body, name=name, out_shape=out_shape, grid=grid, in_specs=in_specs, out_specs=out_specs,
                              scratch_shapes=list(scratch_shapes), compiler_params=pltpu.CompilerParams(**params))
    single = not isinstance(out_shape, (list, tuple))
    out_shape = [out_shape] if single else list(out_shape)
    out_specs = [out_specs] if single else list(out_specs)
    n_in, n_out, n_scr = len(in_specs), len(out_shape), len(scratch_shapes)
    h_in, h_out = len(hosted.inputs), len(hosted.out_shapes)
    hbm = pl.BlockSpec(memory_space=pltpu.HBM)

    def hosting_body(*refs):
        ins, h_ins = refs[:n_in], refs[n_in:n_in + h_in]
        o0 = n_in + h_in
        outs, h_outs = refs[o0:o0 + n_out], refs[o0 + n_out:o0 + n_out + h_out]
        s0 = o0 + n_out + h_out
        scr, (send_sems, recv_sems) = refs[s0:s0 + n_scr], refs[s0 + n_scr:]
        ids = [pl.program_id(d) for d in range(len(grid))]
        first = functools.reduce(lambda u, v: u & v, [i == 0 for i in ids])
        last = functools.reduce(lambda u, v: u & v, [i == g - 1 for i, g in zip(ids, grid)])

        @pl.when(first)
        def _():
            hosted.start(h_ins, h_outs, send_sems, recv_sems)

        body(*ins, *outs, *scr)

        @pl.when(last)
        def _():
            hosted.finish(h_ins, h_outs, send_sems, recv_sems)

    params["dimension_semantics"] = ("arbitrary",) * len(grid)
    call = pl.pallas_call(
        hosting_body, name=name, out_shape=out_shape + hosted.out_shapes, grid=grid,
        in_specs=list(in_specs) + [hbm] * h_in, out_specs=out_specs + [hbm] * h_out,
        scratch_shapes=list(scratch_shapes) + [pltpu.SemaphoreType.DMA((hosted.n_sems,)),
                                               pltpu.SemaphoreType.DMA((hosted.n_sems,))],
        input_output_aliases={n_in + i: n_out + o for i, o in hosted.aliases.items()},
        compiler_params=pltpu.CompilerParams(**params))

    def run(*args):
        res = call(*args, *hosted.inputs)
        main = res[:n_out]
        return (main[0] if single else main), list(res[n_out:])

    return run


def _tile(n, target, mult=LANES):
    best = None
    for t in range(mult, min(n, target) + 1, mult):
        if n % t == 0:
            best = t
    return best or n


_DN = {"nn": (((1,), (0,)), ((), ())), "nt": (((1,), (1,)), ((), ())), "tn": (((0,), (0,)), ((), ()))}


def matmul(a, b, mode, out_dtype, *, name, add=None, b_shards=1, out_shards=1, tm=512, tn=1024, tk=2048,
           hosted=None, norm_bwd=None, b_shard0=0):
    bs = b.shape[-2:]
    if mode == "nn":
        (M, K), (K2, N) = a.shape, (bs[0], bs[1] * b_shards)
    elif mode == "nt":
        (M, K), (N, K2) = a.shape, (bs[0], bs[1] * b_shards)
    else:
        (K, M), (K2, N) = a.shape, bs
    assert K == K2, (name, a.shape, b.shape)
    n_sh = N // max(b_shards if mode == "nn" else 1, out_shards)
    k_sh = K // (b_shards if mode == "nt" else 1)
    tm, tn, tk = _tile(M, tm), _tile(n_sh, tn), _tile(k_sh, tk)
    nk = K // tk
    nbn, nbk = n_sh // tn, k_sh // tk
    dn = _DN[mode]
    has_add = add is not None
    a_bytes, b_bytes = a.size * a.dtype.itemsize, b.size * b.dtype.itemsize
    i_outer = nk > 1 or a_bytes + (M // tm) * b_bytes <= b_bytes + (N // tn) * a_bytes

    def ij(g0, g1):
        return (g0, g1) if i_outer else (g1, g0)

    fused_norm = norm_bwd is not None
    if fused_norm:
        assert tn == N and out_shards == 1 and not has_add and hosted is None, name
        i_outer = True

    def body(*refs):
        a_ref, b_ref = refs[:2]
        add_ref = refs[2] if has_add else None
        o_ref = refs[3 if has_add else 2]
        part = lax.dot_general(a_ref[...].astype(BF16), b_ref[...].astype(BF16), dn, preferred_element_type=F32)

        def finish(r):
            if fused_norm:
                h_ref, g_ref, dh_ref, o32_ref, o16_ref, dg_ref = refs[2:8]
                _, vjp = jax.vjp(_rms, h_ref[...], g_ref[...])
                dx, dg = vjp(r)
                out = dx + dh_ref[...]
                o32_ref[...] = out
                o16_ref[...] = out.astype(BF16)
                dg_ref[...] += dg
                return
            if has_add:
                r = r + add_ref[...]
            o_ref[...] = r.astype(out_dtype)

        if fused_norm:
            @pl.when((pl.program_id(0) == 0) & (pl.program_id(2) == 0))
            def _():
                refs[7][...] = jnp.zeros_like(refs[7])

        if nk == 1:
            finish(part)
            return
        acc = refs[-1]
        k = pl.program_id(2)

        @pl.when(k == 0)
        def _():
            acc[...] = part

        @pl.when(k > 0)
        def _():
            acc[...] += part

        @pl.when(k == nk - 1)
        def _():
            finish(acc[...])

    def spec(block, fn):
        return pl.BlockSpec(block, lambda g0, g1, k: fn(*ij(g0, g1), k))

    if mode == "tn":
        a_spec = spec((tk, tm), lambda i, j, k: (k, i))
    else:
        a_spec = spec((tm, tk), lambda i, j, k: (i, k))
    if mode == "nt":
        if b_shards > 1:
            b_spec = spec((None, tn, tk), lambda i, j, k: (b_shard0 + k // nbk, j, k % nbk))
        else:
            b_spec = spec((tn, tk), lambda i, j, k: (j, k))
    elif b_shards > 1:
        b_spec = spec((None, tk, tn), lambda i, j, k: (j // nbn, k, j % nbn))
    else:
        b_spec = spec((tk, tn), lambda i, j, k: (k, j))
    in_specs = [a_spec, b_spec]
    args = [a, b]
    if has_add:
        in_specs.append(spec((tm, tn), lambda i, j, k: (i, j)))
        args.append(add)
    if out_shards > 1:
        out_spec = spec((None, tm, tn), lambda i, j, k: (j // nbn, i, j % nbn))
        out_shape = jax.ShapeDtypeStruct((out_shards, M, n_sh), out_dtype)
    else:
        out_spec = spec((tm, tn), lambda i, j, k: (i, j))
        out_shape = jax.ShapeDtypeStruct((M, N), out_dtype)
    gi, gj = M // tm, N // tn
    if fused_norm:
        h, gain, dh = norm_bwd
        row_blk = spec((tm, N), lambda i, j, k: (i, 0))
        gain_blk = spec((1, N), lambda i, j, k: (0, 0))
        return _pcall(body, name=name, grid=(gi, 1, nk), in_specs=in_specs + [row_blk, gain_blk, row_blk],
                      out_specs=[row_blk, row_blk, gain_blk],
                      out_shape=[jax.ShapeDtypeStruct((M, N), F32), jax.ShapeDtypeStruct((M, N), BF16),
                                 jax.ShapeDtypeStruct((1, N), F32)],
                      scratch_shapes=[pltpu.VMEM((tm, tn), F32)] if nk > 1 else [],
                      dims=("arbitrary", "arbitrary", "arbitrary"))(a, b, h, gain, dh)
    return _pcall(body, name=name, out_shape=out_shape, grid=(gi, gj, nk) if i_outer else (gj, gi, nk),
                  in_specs=in_specs, out_specs=out_spec,
                  scratch_shapes=[pltpu.VMEM((tm, tn), F32)] if nk > 1 else [],
                  dims=("parallel", "parallel", "arbitrary"), hosted=hosted)(*args)


def _row_spec(tr, w, c):
    return pl.BlockSpec((tr, w), lambda i: (i, c))


def _full_spec(arr):
    return pl.BlockSpec(arr.shape, lambda i: (0,) * arr.ndim)


def rowwise(fn, rows, params, nd_rows, outs, *, name, tr=256):
    S = rows[0][0].shape[0]
    tr = min(tr, S)
    n_in = len(rows) + len(params) + len(nd_rows)

    def body(*refs):
        res = fn(*[x[...] for x in refs[:n_in]])
        for o_ref, v in zip(refs[n_in:], res):
            o_ref[...] = v.astype(o_ref.dtype)

    return _pcall(body, name=name, grid=(S // tr,),
                  in_specs=([_row_spec(tr, w, c) for (_, w, c) in rows] + [_full_spec(q) for q in params]
                            + [_row_spec(tr, w, c) for (_, w, c) in nd_rows]),
                  out_specs=[_row_spec(tr, w, 0) for (w, _) in outs],
                  out_shape=[jax.ShapeDtypeStruct((S, w), dt) for (w, dt) in outs],
                  dims=("parallel",))(*[r[0] for r in rows], *params, *[r[0] for r in nd_rows])


def rowwise_bwd(fn, rows, params, nd_rows, cts, d_dtypes, *, name, adds=None, tr=256):
    S = rows[0][0].shape[0]
    tr = min(tr, S)
    n_r, n_p, n_n, n_c = len(rows), len(params), len(nd_rows), len(cts)
    adds = adds or [None] * n_r
    add_list = [a for a in adds if a is not None]
    n_a = len(add_list)
    d_dtypes = [dt if isinstance(dt, (list, tuple)) else (dt,) for dt in d_dtypes]
    n_d = sum(len(dt) for dt in d_dtypes)

    def body(*refs):
        it = iter(refs)
        r = [next(it)[...] for _ in range(n_r)]
        p = [next(it)[...] for _ in range(n_p)]
        nd = [next(it)[...] for _ in range(n_n)]
        c = [next(it)[...] for _ in range(n_c)]
        ad = [next(it)[...] for _ in range(n_a)]
        d_row_refs = [[next(it) for _ in dts] for dts in d_dtypes]
        d_par_refs = [next(it) for _ in range(n_p)]
        outs, vjp = jax.vjp(lambda *dp: fn(*dp, *nd), *r, *p)
        g = vjp(tuple(ci.astype(o.dtype) for ci, o in zip(c, outs)))
        ai = 0
        for k in range(n_r):
            gk = g[k].astype(F32)
            if adds[k] is not None:
                gk = gk + ad[ai].astype(F32)
                ai += 1
            for ref in d_row_refs[k]:
                ref[...] = gk.astype(ref.dtype)

        @pl.when(pl.program_id(0) == 0)
        def _():
            for ref in d_par_refs:
                ref[...] = jnp.zeros_like(ref)

        for k in range(n_p):
            d_par_refs[k][...] += g[n_r + k].astype(F32)

    in_specs = ([_row_spec(tr, w, c) for (_, w, c) in rows] + [_full_spec(q) for q in params]
                + [_row_spec(tr, w, c) for (_, w, c) in nd_rows] + [_row_spec(tr, w, c) for (_, w, c) in cts]
                + [_row_spec(tr, w, c) for (_, w, c) in add_list])
    out_specs = ([_row_spec(tr, w, 0) for (_, w, _), dts in zip(rows, d_dtypes) for _ in dts]
                 + [_full_spec(q) for q in params])
    out_shape = ([jax.ShapeDtypeStruct((S, w), dt) for (_, w, _), dts in zip(rows, d_dtypes) for dt in dts]
                 + [jax.ShapeDtypeStruct(q.shape, F32) for q in params])
    res = _pcall(body, name=name, grid=(S // tr,), in_specs=in_specs, out_specs=out_specs, out_shape=out_shape,
                 dims=("arbitrary",))(*[r[0] for r in rows], *params, *[r[0] for r in nd_rows],
                                      *[r[0] for r in cts], *[r[0] for r in add_list])
    d_rows, i = [], 0
    for dts in d_dtypes:
        d_rows.append(res[i] if len(dts) == 1 else tuple(res[i:i + len(dts)]))
        i += len(dts)
    return d_rows, res[n_d:]


def _rms(x, g):
    x = x.astype(F32)
    return x * lax.rsqrt(jnp.mean(x * x, axis=-1, keepdims=True) + NORM_EPS) * g


def _fn_rms(x, g):
    return (_rms(x, g),)


def _sigmoid(x):
    return 1.0 / (1.0 + jnp.exp(-x))


def _silu(x):
    return x * _sigmoid(x)


def _softplus(x):
    return jnp.maximum(x, 0.0) + jnp.log(1.0 + jnp.exp(-jnp.abs(x)))


def _fn_ple(h, pp, gl):
    return (h.astype(F32) + pp.astype(F32) * _sigmoid(gl.astype(F32)),)


def _fn_ple_terms(pp, gl):
    return (pp.astype(F32) * _sigmoid(gl.astype(F32)),)


def _rot_half_matrix():
    half = MLA_ROPE // 2
    r = lax.broadcasted_iota(jnp.int32, (LANES, LANES), 0)
    c = lax.broadcasted_iota(jnp.int32, (LANES, LANES), 1)
    plus = (c == r + half) & (r < half)
    minus = (r == c + half) & (c < half)
    return jnp.where(plus, 1.0, 0.0) - jnp.where(minus, 1.0, 0.0)


def _rope_pad(x, cosp, sinp):
    return x * cosp + jnp.dot(x, _rot_half_matrix(), precision=HI, preferred_element_type=F32) * sinp


def _fn_mla_pre(cq, ckv, kr, qn_w, kvn_w, cosp, sinp):
    return (_rms(cq, qn_w), _rms(ckv, kvn_w), _rope_pad(kr.astype(F32), cosp, sinp))


def _fn_rope_q(q, cosp, sinp):
    q = q.astype(F32)
    parts = []
    for h in range(MLA_HEADS):
        base = 2 * LANES * h
        parts.append(q[:, base:base + LANES])
        parts.append(_rope_pad(q[:, base + LANES:base + 2 * LANES], cosp, sinp))
    return (jnp.concatenate(parts, axis=1),)


def _fn_ret_gate(g, on, w):
    return (_silu(g.astype(F32)) * (on.astype(F32) * w),)


def _shift_down(cur, halo, s):
    if s == 0:
        return cur
    r = pltpu.roll(cur, s, 0)
    hs = pltpu.roll(halo, s, 0)
    row = lax.broadcasted_iota(jnp.int32, hs.shape, 0)
    first = jnp.where(row < s, hs, r[:SUBLANES])
    return jnp.concatenate([first, r[SUBLANES:]], axis=0)


def _shift_up(cur, halo, s):
    if s == 0:
        return cur
    n = cur.shape[0]
    r = pltpu.roll(cur, n - s, 0)
    hs = pltpu.roll(halo, SUBLANES - s, 0)
    row = lax.broadcasted_iota(jnp.int32, hs.shape, 0)
    last = jnp.where(row >= SUBLANES - s, hs, r[n - SUBLANES:])
    return jnp.concatenate([r[:n - SUBLANES], last], axis=0)


def _prev_halo_spec(tr, tw, col):
    return pl.BlockSpec((SUBLANES, tw), lambda c, i: (jnp.maximum(i * (tr // SUBLANES) - 1, 0), col(c)))


def _conv_taps(cur, halo, w_ref, width):
    taps = [_shift_down(cur, halo, width - 1 - j) for j in range(width)]
    y = taps[0] * w_ref[0:1, :]
    for j in range(1, width):
        y = y + taps[j] * w_ref[j:j + 1, :]
    return y, taps


def gdn_conv_fwd(zin, w, *, tr=512, tw=512):
    S = zin.shape[0]
    tr = min(tr, S)
    width, C = w.shape

    def body(cur_ref, halo_ref, w_ref, o_ref):
        i = pl.program_id(1)
        halo = halo_ref[...] * (i > 0).astype(F32)
        y, _ = _conv_taps(cur_ref[...], halo, w_ref, width)
        o_ref[...] = _silu(y)

    return _pcall(body, name="gdn_conv_fwd", grid=(C // tw, S // tr),
                  in_specs=[pl.BlockSpec((tr, tw), lambda c, i: (i, c)), _prev_halo_spec(tr, tw, lambda c: c),
                            pl.BlockSpec((width, tw), lambda c, i: (0, c))],
                  out_specs=pl.BlockSpec((tr, tw), lambda c, i: (i, c)),
                  out_shape=jax.ShapeDtypeStruct((S, C), F32), dims=("parallel", "arbitrary"))(zin, zin, w)


def gdn_conv_bwd(zin, w, dy, *, tr=512, tw=512):
    S = zin.shape[0]
    tr = min(tr, S)
    width, C = w.shape
    n_i = S // tr

    def body(cur_ref, halo_ref, w_ref, dy_ref, dx_ref, dw_ref, carry):
        i = pl.program_id(1)
        halo = halo_ref[...] * (i < n_i - 1).astype(F32)
        y, taps = _conv_taps(cur_ref[...], halo, w_ref, width)
        sg = _sigmoid(y)
        da = dy_ref[...] * (sg * (1.0 + y * (1.0 - sg)))

        @pl.when(i == 0)
        def _():
            dw_ref[...] = jnp.zeros_like(dw_ref)
            carry[...] = jnp.zeros_like(carry)

        nxt = carry[...]
        dx = da * w_ref[width - 1:width, :]
        for s in range(1, width):
            dx = dx + _shift_up(da, nxt, s) * w_ref[width - 1 - s:width - s, :]
        dx_ref[...] = dx.astype(dx_ref.dtype)
        carry[...] = da[:SUBLANES]
        for j in range(width):
            dw_ref[j:j + 1, :] += jnp.sum(da * taps[j], axis=0, keepdims=True)

    rows = pl.BlockSpec((tr, tw), lambda c, i: (n_i - 1 - i, c))
    prev = pl.BlockSpec((SUBLANES, tw), lambda c, i: (jnp.maximum((n_i - 1 - i) * (tr // SUBLANES) - 1, 0), c))
    return _pcall(body, name="gdn_conv_bwd", grid=(C // tw, n_i),
                  in_specs=[rows, prev, pl.BlockSpec((width, tw), lambda c, i: (0, c)), rows],
                  out_specs=[rows, pl.BlockSpec((width, tw), lambda c, i: (0, c))],
                  out_shape=[jax.ShapeDtypeStruct((S, C), BF16), jax.ShapeDtypeStruct((width, C), F32)],
                  scratch_shapes=[pltpu.VMEM((SUBLANES, tw), F32)],
                  dims=("arbitrary", "arbitrary"))(zin, zin, w, dy)


def ffn_conv_fwd(u, w, b, *, tr=1024, tw=256):
    S, C2 = u.shape
    tr = min(tr, S)
    width = w.shape[0]
    half = C2 // 2
    nc = half // tw

    def body(g_ref, gh_ref, u_ref, uh_ref, wg_ref, wu_ref, bg_ref, bu_ref, o_ref):
        i = pl.program_id(1)
        live = (i > 0).astype(F32)
        yg, _ = _conv_taps(g_ref[...], gh_ref[...] * live, wg_ref, width)
        yu, _ = _conv_taps(u_ref[...], uh_ref[...] * live, wu_ref, width)
        o_ref[...] = (_silu(yg + bg_ref[...]) * (yu + bu_ref[...])).astype(o_ref.dtype)

    return _pcall(body, name="ffn_conv_fwd", grid=(nc, S // tr),
                  in_specs=[pl.BlockSpec((tr, tw), lambda c, i: (i, c)), _prev_halo_spec(tr, tw, lambda c: c),
                            pl.BlockSpec((tr, tw), lambda c, i: (i, c + nc)),
                            _prev_halo_spec(tr, tw, lambda c: c + nc),
                            pl.BlockSpec((width, tw), lambda c, i: (0, c)),
                            pl.BlockSpec((width, tw), lambda c, i: (0, c + nc)),
                            pl.BlockSpec((1, tw), lambda c, i: (0, c)), pl.BlockSpec((1, tw), lambda c, i: (0, c + nc))],
                  out_specs=pl.BlockSpec((tr, tw), lambda c, i: (i, c)),
                  out_shape=jax.ShapeDtypeStruct((S, half), BF16),
                  dims=("parallel", "arbitrary"))(u, u, u, u, w, w, b, b)


def ffn_conv_bwd(u, w, b, df, *, tr=512, tw=512):
    S, C2 = u.shape
    tr = min(tr, S)
    width = w.shape[0]
    half = C2 // 2
    nc = half // tw
    n_i = S // tr

    def body(g_ref, gh_ref, u_ref, uh_ref, wg_ref, wu_ref, bg_ref, bu_ref, df_ref, dug_ref, duu_ref, dw_ref, db_ref,
             carry):
        i = pl.program_id(1)
        live = (i < n_i - 1).astype(F32)
        yg, gt = _conv_taps(g_ref[...], gh_ref[...] * live, wg_ref, width)
        yu, ut = _conv_taps(u_ref[...], uh_ref[...] * live, wu_ref, width)
        yg = yg + bg_ref[...]
        yu = yu + bu_ref[...]
        sg = _sigmoid(yg)
        dfv = df_ref[...].astype(F32)
        dcs = (dfv * yu * (sg * (1.0 + yg * (1.0 - sg))), dfv * (yg * sg))

        @pl.when(i == 0)
        def _():
            dw_ref[...] = jnp.zeros_like(dw_ref)
            db_ref[...] = jnp.zeros_like(db_ref)
            carry[...] = jnp.zeros_like(carry)

        for t, (dc, taps, w_ref, du_ref) in enumerate(zip(dcs, (gt, ut), (wg_ref, wu_ref), (dug_ref, duu_ref))):
            halo = carry[t]
            du = dc * w_ref[width - 1:width, :]
            for s in range(1, width):
                du = du + _shift_up(dc, halo, s) * w_ref[width - 1 - s:width - s, :]
            du_ref[...] = du.astype(du_ref.dtype)
            carry[t] = dc[:SUBLANES]
            db_ref[t] += jnp.sum(dc, axis=0, keepdims=True)
            for j in range(width):
                dw_ref[t, j:j + 1, :] += jnp.sum(dc * taps[j], axis=0, keepdims=True)

    def prev(col):
        return pl.BlockSpec((SUBLANES, tw),
                            lambda c, i: (jnp.maximum((n_i - 1 - i) * (tr // SUBLANES) - 1, 0), col(c)))

    rows = lambda col: pl.BlockSpec((tr, tw), lambda c, i: (n_i - 1 - i, col(c)))
    du_g, du_u, dw, db = _pcall(
        body, name="ffn_conv_bwd", grid=(nc, n_i),
        in_specs=[rows(lambda c: c), prev(lambda c: c), rows(lambda c: c + nc), prev(lambda c: c + nc),
                  pl.BlockSpec((width, tw), lambda c, i: (0, c)), pl.BlockSpec((width, tw), lambda c, i: (0, c + nc)),
                  pl.BlockSpec((1, tw), lambda c, i: (0, c)), pl.BlockSpec((1, tw), lambda c, i: (0, c + nc)),
                  rows(lambda c: c)],
        out_specs=[rows(lambda c: c), rows(lambda c: c), pl.BlockSpec((2, width, tw), lambda c, i: (0, 0, c)),
                   pl.BlockSpec((2, 1, tw), lambda c, i: (0, 0, c))],
        out_shape=[jax.ShapeDtypeStruct((S, half), BF16), jax.ShapeDtypeStruct((S, half), BF16),
                   jax.ShapeDtypeStruct((2, width, half), F32), jax.ShapeDtypeStruct((2, 1, half), F32)],
        scratch_shapes=[pltpu.VMEM((2, SUBLANES, tw), F32)],
        dims=("arbitrary", "arbitrary"))(u, u, u, u, w, w, b, b, df)
    return du_g, du_u, jnp.concatenate([dw[0], dw[1]], axis=1), jnp.concatenate([db[0], db[1]], axis=1)


_MODE_OF = {v: k for k, v in _DN.items()}


def _bf16_dot(a, b, mode):
    return lax.dot_general(a.astype(BF16), b.astype(BF16), _DN[mode], preferred_element_type=F32)


@functools.partial(jax.custom_vjp, nondiff_argnums=(2,))
def _bdot_mode(a, b, mode):
    return _bf16_dot(a, b, mode)


def _bdot_fwd(a, b, mode):
    return _bf16_dot(a, b, mode), (a, b)


def _bdot_bwd(mode, res, ct):
    a, b = res
    if mode == "nn":
        da, db = _bf16_dot(ct, b, "nt"), _bf16_dot(a, ct, "tn")
    elif mode == "nt":
        da, db = _bf16_dot(ct, b, "nn"), _bf16_dot(ct, a, "tn")
    else:
        da, db = _bf16_dot(b, ct, "nt"), _bf16_dot(a, ct, "nn")
    return da.astype(a.dtype), db.astype(b.dtype)


_bdot_mode.defvjp(_bdot_fwd, _bdot_bwd)


def _bdot(a, b, dn=_DN["nn"]):
    return _bdot_mode(a, b, _MODE_OF[dn])


def _hi_lo(x):
    hi = x.astype(BF16)
    return hi, (x - hi.astype(F32)).astype(BF16)


def _dot3_raw(a, b, mode):
    a1, a2 = _hi_lo(a)
    b1, b2 = _hi_lo(b)
    dot = lambda p, q: lax.dot_general(p, q, _DN[mode], preferred_element_type=F32)
    return dot(a1, b1) + (dot(a1, b2) + dot(a2, b1))


@functools.partial(jax.custom_vjp, nondiff_argnums=(2,))
def _dot3(a, b, mode="nn"):
    return _dot3_raw(a, b, mode)


def _dot3_fwd(a, b, mode):
    return _dot3_raw(a, b, mode), (a, b)


def _dot3_bwd(mode, res, ct):
    a, b = res
    if mode == "nn":
        return _dot3_raw(ct, b, "nt"), _dot3_raw(a, ct, "tn")
    if mode == "nt":
        return _dot3_raw(ct, b, "nn"), _dot3_raw(ct, a, "tn")
    return _dot3_raw(b, ct, "nt"), _dot3_raw(a, ct, "nn")


_dot3.defvjp(_dot3_fwd, _dot3_bwd)


@functools.partial(jax.custom_vjp, nondiff_argnums=(2,))
def _gdot(a, b, mode="nn"):
    return _bf16_dot(a, b, mode)


def _gdot_fwd(a, b, mode):
    return _bf16_dot(a, b, mode), (a.astype(BF16), b.astype(BF16))


def _ct_dot(p, q, mode, ct_first):
    ct, r = (p, q) if ct_first else (q, p)
    c1, c2 = _hi_lo(ct)
    dot = lambda c: lax.dot_general(*((c, r) if ct_first else (r, c)), _DN[mode], preferred_element_type=F32)
    return dot(c1) + dot(c2)


def _gdot_bwd(mode, res, ct):
    a, b = res
    if mode == "nn":
        return _ct_dot(ct, b, "nt", True), _ct_dot(a, ct, "tn", False)
    if mode == "nt":
        return _ct_dot(ct, b, "nn", True), _ct_dot(ct, a, "tn", True)
    return _ct_dot(b, ct, "nt", False), _ct_dot(a, ct, "nn", False)


_gdot.defvjp(_gdot_fwd, _gdot_bwd)


def _split_dot(ones, x):
    x1 = x.astype(BF16)
    r1 = x - x1.astype(F32)
    x2 = r1.astype(BF16)
    x3 = (r1 - x2.astype(F32)).astype(BF16)
    m = ones.astype(BF16)
    dot = lambda p: lax.dot_general(m, p, _DN["nn"], preferred_element_type=F32)
    return dot(x1) + dot(x2) + dot(x3)


@jax.custom_vjp
def _tri_cumsum(x, lower, upper):
    return _split_dot(lower, x)


def _tri_cumsum_fwd(x, lower, upper):
    return _split_dot(lower, x), (lower, upper)


def _tri_cumsum_bwd(res, ct):
    lower, upper = res
    return _split_dot(upper, ct), jnp.zeros_like(lower), jnp.zeros_like(upper)


_tri_cumsum.defvjp(_tri_cumsum_fwd, _tri_cumsum_bwd)


def _tri_masks(n):
    r = lax.broadcasted_iota(jnp.int32, (n, n), 0)
    c = lax.broadcasted_iota(jnp.int32, (n, n), 1)
    return r >= c, r > c


def _gdn_chunk(q, k, v, z, ab, a_row, dt_row, norm_w, state, sel_a, sel_b):
    C = q.shape[0]
    incl, strict = _tri_masks(C)
    lower = jnp.where(incl, 1.0, 0.0)
    qn = q * lax.rsqrt(jnp.sum(q * q, axis=-1, keepdims=True) + NORM_EPS) * (GDN_DK ** -0.5)
    kn = k * lax.rsqrt(jnp.sum(k * k, axis=-1, keepdims=True) + NORM_EPS)
    g = jnp.sum(-jnp.exp(a_row) * _softplus(ab + dt_row) * sel_a, axis=-1, keepdims=True)
    beta = jnp.sum(_sigmoid(ab) * sel_b, axis=-1, keepdims=True)
    gb = jnp.broadcast_to(g, (C, C))
    g_col = _tri_cumsum(gb, lower, jnp.where(strict, 0.0, 1.0))
    g_row = g_col.T
    g_last = jnp.sum(gb, axis=0, keepdims=True)
    gamma = jnp.where(incl, jnp.exp(jnp.where(incl, g_col - g_row, 0.0)), 0.0)
    e_col = jnp.exp(g_col)
    kb = kn * beta
    a_mat = jnp.where(strict, _gdot(kb, kn, "nt") * gamma, 0.0)
    x = jnp.concatenate([v * beta, kb * e_col], axis=1)
    pw = -a_mat
    steps = int(math.log2(C))
    for it in range(steps):
        x = x + _dot3(pw, x, "nn")
        if it < steps - 1:
            pw = _dot3(pw, pw, "nn")
    u, w = x[:, :GDN_DV], x[:, GDN_DV:]
    attn = _gdot(qn, kn, "nt") * gamma
    q_dec = qn * e_col
    k_dec = kn * jnp.exp(g_last - g_col)
    v_new = u - _gdot(w, state, "nn")
    o = _gdot(q_dec, state, "nn") + _gdot(attn, v_new, "nn")
    state_new = state * jnp.exp(jnp.broadcast_to(g_last, state.shape)) + _gdot(k_dec, v_new, "tn")
    y = _rms(o, norm_w) * _silu(z)
    return y, state_new


def _head_selectors(h):
    lane = lax.broadcasted_iota(jnp.int32, (1, LANES), 1)
    return jnp.where(lane == h, 1.0, 0.0), jnp.where(lane == h + GDN_HEADS, 1.0, 0.0)


GDN_HPS = 4
GDN_W = GDN_HPS * LANES


def _gdn_in_specs(rev, nc):
    def n_(n):
        return nc - 1 - n if rev else n
    G = GDN_HEADS // GDN_HPS
    blk = lambda off: pl.BlockSpec((CHUNK, GDN_W), lambda n, h: (n_(n), off + h))
    row = pl.BlockSpec((1, LANES), lambda n, h: (0, 0))
    return n_, [blk(0), blk(G), blk(2 * G), blk(ZIN_Z // GDN_W),
                pl.BlockSpec((CHUNK, LANES), lambda n, h: (n_(n), ZIN_AB // LANES)), row, row, row]


def _lanes(ref, j):
    return ref[:, j * LANES:(j + 1) * LANES]


def _hosting(call, hosted, *args):
    res = call(*args)
    return res if hosted is not None else (res, [])


def gdn_fwd(qkv, zin, a_row, dt_row, norm_w, hosted=None):
    S = qkv.shape[0]
    nc = S // CHUNK
    H = GDN_HEADS
    _, in_specs = _gdn_in_specs(False, nc)

    def body(q_ref, k_ref, v_ref, z_ref, ab_ref, a_ref, dt_ref, nw_ref, y_ref, st_ref, state):
        n, g = pl.program_id(0), pl.program_id(1)
        @pl.when((n == 0) & (g == 0))
        def _():
            state[...] = jnp.zeros_like(state)

        res = []
        for j in range(GDN_HPS):
            h = g * GDN_HPS + j
            st = state[h]
            sel_a, sel_b = _head_selectors(h)
            res.append((st,) + _gdn_chunk(_lanes(q_ref, j), _lanes(k_ref, j), _lanes(v_ref, j), _lanes(z_ref, j),
                                          ab_ref[...], a_ref[...], dt_ref[...], nw_ref[...], st, sel_a, sel_b))
        for j, (st, y, st_new) in enumerate(res):
            st_ref[j] = st
            y_ref[:, j * LANES:(j + 1) * LANES] = y.astype(y_ref.dtype)
            state[g * GDN_HPS + j] = st_new

    call = _pcall(body, name="gdn_fwd", grid=(nc, H // GDN_HPS), in_specs=in_specs,
                  out_specs=[pl.BlockSpec((CHUNK, GDN_W), lambda n, h: (n, h)),
                             pl.BlockSpec((GDN_HPS, None, GDN_DK, GDN_DV), lambda n, h: (h, n, 0, 0))],
                  out_shape=[jax.ShapeDtypeStruct((S, H * GDN_DV), BF16),
                             jax.ShapeDtypeStruct((H, nc, GDN_DK, GDN_DV), F32)],
                  scratch_shapes=[pltpu.VMEM((H, GDN_DK, GDN_DV), F32)],
                  dims=("arbitrary", "arbitrary"), hosted=hosted)
    return _hosting(call, hosted, qkv, qkv, qkv, zin, zin, a_row, dt_row, norm_w)


def gdn_bwd(qkv, zin, a_row, dt_row, norm_w, states, dy, dy_col0, hosted=None):
    S = qkv.shape[0]
    nc = S // CHUNK
    H = GDN_HEADS
    n_, in_specs = _gdn_in_specs(True, nc)
    assert dy_col0 % GDN_HPS == 0
    in_specs = in_specs + [pl.BlockSpec((GDN_HPS, None, GDN_DK, GDN_DV), lambda n, h: (h, n_(n), 0, 0)),
                           pl.BlockSpec((CHUNK, GDN_W), lambda n, h: (n_(n), dy_col0 // GDN_HPS + h))]

    def body(q_ref, k_ref, v_ref, z_ref, ab_ref, a_ref, dt_ref, nw_ref, st_ref, dy_ref,
             dq_ref, dk_ref, dv_ref, dz_ref, dab_ref, da_ref, ddt_ref, dnw_ref, dstate):
        n, g = pl.program_id(0), pl.program_id(1)

        @pl.when((n == 0) & (g == 0))
        def _():
            da_ref[...] = jnp.zeros_like(da_ref)
            ddt_ref[...] = jnp.zeros_like(ddt_ref)
            dnw_ref[...] = jnp.zeros_like(dnw_ref)
            dstate[...] = jnp.zeros_like(dstate)

        @pl.when(g == 0)
        def _():
            dab_ref[...] = jnp.zeros_like(dab_ref)

        res = []
        for j in range(GDN_HPS):
            h = g * GDN_HPS + j
            sel_a, sel_b = _head_selectors(h)
            _, vjp = jax.vjp(lambda *a, sa=sel_a, sb=sel_b: _gdn_chunk(*a, sa, sb), _lanes(q_ref, j), _lanes(k_ref, j),
                             _lanes(v_ref, j), _lanes(z_ref, j), ab_ref[...], a_ref[...], dt_ref[...], nw_ref[...],
                             st_ref[j])
            res.append(vjp((_lanes(dy_ref, j).astype(F32), dstate[h])))
        for j, (dq, dk, dv, dz, dab, da, ddt, dnw, dst) in enumerate(res):
            cols = slice(j * LANES, (j + 1) * LANES)
            dq_ref[:, cols] = dq
            dk_ref[:, cols] = dk
            dv_ref[:, cols] = dv
            dz_ref[:, cols] = dz.astype(dz_ref.dtype)
            dstate[g * GDN_HPS + j] = dst
        dab_ref[...] += sum(r[4] for r in res)
        da_ref[...] += sum(r[5] for r in res)
        ddt_ref[...] += sum(r[6] for r in res)
        dnw_ref[...] += sum(r[7] for r in res)

    blk = pl.BlockSpec((CHUNK, GDN_W), lambda n, h: (n_(n), h))
    row = pl.BlockSpec((1, LANES), lambda n, h: (0, 0))
    wide = jax.ShapeDtypeStruct((S, H * LANES), F32)
    call = _pcall(body, name="gdn_bwd", grid=(nc, H // GDN_HPS), in_specs=in_specs,
                  out_specs=[blk, blk, blk, blk, pl.BlockSpec((CHUNK, LANES), lambda n, h: (n_(n), 0)), row, row, row],
                  out_shape=[wide, wide, wide, jax.ShapeDtypeStruct((S, H * LANES), BF16),
                             jax.ShapeDtypeStruct((S, LANES), F32)] + [jax.ShapeDtypeStruct((1, LANES), F32)] * 3,
                  scratch_shapes=[pltpu.VMEM((H, GDN_DK, GDN_DV), F32)],
                  dims=("arbitrary", "arbitrary"), hosted=hosted)
    return _hosting(call, hosted, qkv, qkv, qkv, zin, zin, a_row, dt_row, norm_w, states, dy)


def _rope_full(x, cos, sin):
    x1, x2 = x[:, :RET_DK // 2], x[:, RET_DK // 2:]
    return jnp.concatenate([x1 * cos - x2 * sin, x2 * cos + x1 * sin], axis=1)


RET_CHUNK = 512


def _ret_chunk(q, k, v, cos, sin, lg, state):
    C = q.shape[0]
    incl, _ = _tri_masks(C)
    qr = _rope_full(q, cos, sin)
    kr = _rope_full(k, cos, sin) * (RET_DK ** -0.5)
    r = lax.broadcasted_iota(jnp.int32, (C, C), 0)
    c = lax.broadcasted_iota(jnp.int32, (C, C), 1)
    dist = jnp.where(incl, (r - c).astype(F32), 0.0)
    lg1 = lg[:, :1]
    decay = jnp.where(incl, jnp.exp(dist * lg1), 0.0)
    pos = lax.broadcasted_iota(jnp.int32, (C, 1), 0).astype(F32)
    xi = jnp.exp((pos + 1.0) * lg1)
    zeta = jnp.exp((C - 1.0 - pos) * lg1)
    inner = _bdot(_bdot(qr, kr, _DN["nt"]) * decay, v)
    cross = _bdot(qr * xi, state)
    state_new = state * jnp.exp(C * lg1) + _bdot(kr * zeta, v, _DN["tn"])
    o = inner + cross
    mu = jnp.mean(o, axis=-1, keepdims=True)
    var = jnp.mean(jnp.square(o - mu), axis=-1, keepdims=True)
    return (o - mu) * lax.rsqrt(var + NORM_EPS), state_new


def _ret_log_gamma():
    lg = np.log1p(-np.power(2.0, -5.0 - np.arange(RET_HEADS, dtype=np.float64))).astype(np.float32)
    return jnp.asarray(np.broadcast_to(lg[:, None, None], (RET_HEADS, 1, LANES)).copy())


def _ret_in_specs(rev, nc):
    def n_(n):
        return nc - 1 - n if rev else n
    H = RET_HEADS
    return n_, [pl.BlockSpec((RET_CHUNK, RET_DK),lambda n, h: (n_(n), h)),
                pl.BlockSpec((RET_CHUNK, RET_DK),lambda n, h: (n_(n), H + h)),
                pl.BlockSpec((RET_CHUNK, RET_DV),lambda n, h: (n_(n), 2 * H * RET_DK // RET_DV + h)),
                pl.BlockSpec((RET_CHUNK, LANES), lambda n, h: (n_(n), 0)),
                pl.BlockSpec((RET_CHUNK, LANES), lambda n, h: (n_(n), 0)),
                pl.BlockSpec((None, 1, LANES), lambda n, h: (h, 0, 0))]


def ret_fwd(zz, cos, sin, hosted=None):
    S = zz.shape[0]
    nc = S // RET_CHUNK
    H = RET_HEADS
    _, in_specs = _ret_in_specs(False, nc)

    def body(q_ref, k_ref, v_ref, cos_ref, sin_ref, lg_ref, o_ref, st_ref, state):
        n, h = pl.program_id(0), pl.program_id(1)

        @pl.when(n == 0)
        def _():
            state[h] = jnp.zeros((RET_DK, RET_DV), F32)

        st = state[h]
        st_ref[...] = st
        o, st_new = _ret_chunk(q_ref[...], k_ref[...], v_ref[...], cos_ref[...], sin_ref[...], lg_ref[...], st)
        o_ref[...] = o
        state[h] = st_new

    call = _pcall(body, name="ret_fwd", grid=(nc, H), in_specs=in_specs,
                  out_specs=[pl.BlockSpec((RET_CHUNK, RET_DV),lambda n, h: (n, h)),
                             pl.BlockSpec((None, None, RET_DK, RET_DV), lambda n, h: (h, n, 0, 0))],
                  out_shape=[jax.ShapeDtypeStruct((S, H * RET_DV), F32),
                             jax.ShapeDtypeStruct((H, nc, RET_DK, RET_DV), F32)],
                  scratch_shapes=[pltpu.VMEM((H, RET_DK, RET_DV), F32)],
                  dims=("arbitrary", "arbitrary"), hosted=hosted)
    return _hosting(call, hosted, zz, zz, zz, cos, sin, _ret_log_gamma())


def ret_bwd(zz, cos, sin, states, do):
    S = zz.shape[0]
    nc = S // RET_CHUNK
    H = RET_HEADS
    n_, in_specs = _ret_in_specs(True, nc)
    in_specs = in_specs + [pl.BlockSpec((None, None, RET_DK, RET_DV), lambda n, h: (h, n_(n), 0, 0)),
                           pl.BlockSpec((RET_CHUNK, RET_DV),lambda n, h: (n_(n), h))]

    def body(q_ref, k_ref, v_ref, cos_ref, sin_ref, lg_ref, st_ref, do_ref, dq_ref, dk_ref, dv_ref, dstate):
        n, h = pl.program_id(0), pl.program_id(1)

        @pl.when(n == 0)
        def _():
            dstate[h] = jnp.zeros((RET_DK, RET_DV), F32)

        cos, sin, lg = cos_ref[...], sin_ref[...], lg_ref[...]
        _, vjp = jax.vjp(lambda q, k, v, st: _ret_chunk(q, k, v, cos, sin, lg, st),
                         q_ref[...], k_ref[...], v_ref[...], st_ref[...])
        dq, dk, dv, dst = vjp((do_ref[...], dstate[h]))
        dq_ref[...] = dq.astype(dq_ref.dtype)
        dk_ref[...] = dk.astype(dk_ref.dtype)
        dv_ref[...] = dv.astype(dv_ref.dtype)
        dstate[h] = dst

    return _pcall(body, name="ret_bwd", grid=(nc, H), in_specs=in_specs,
                  out_specs=[pl.BlockSpec((RET_CHUNK, RET_DK),lambda n, h: (n_(n), h)),
                             pl.BlockSpec((RET_CHUNK, RET_DK),lambda n, h: (n_(n), h)),
                             pl.BlockSpec((RET_CHUNK, RET_DV),lambda n, h: (n_(n), h))],
                  out_shape=[jax.ShapeDtypeStruct((S, H * RET_DK), BF16), jax.ShapeDtypeStruct((S, H * RET_DK), BF16),
                             jax.ShapeDtypeStruct((S, H * RET_DV), BF16)],
                  scratch_shapes=[pltpu.VMEM((H, RET_DK, RET_DV), F32)],
                  dims=("arbitrary", "arbitrary"))(zz, zz, zz, cos, sin, _ret_log_gamma(), states, do)


MLA_SCALE = (MLA_NOPE + MLA_ROPE) ** -0.5
NEG = -1e30


def _mla_scores(q, kn, kpe, diagonal):
    s = (lax.dot_general(q[:, :LANES], kn, _DN["nt"], preferred_element_type=F32)
         + lax.dot_general(q[:, LANES:], kpe, _DN["nt"], preferred_element_type=F32)) * MLA_SCALE
    if diagonal:
        row = lax.broadcasted_iota(jnp.int32, s.shape, 0)
        col = lax.broadcasted_iota(jnp.int32, s.shape, 1)
        s = jnp.where(col <= row, s, NEG)
    return s


def _on_and_below_diagonal(i, j, step):
    @pl.when(j < i)
    def _():
        step(False)

    @pl.when(j == i)
    def _():
        step(True)


FLASH_T = 1024


def flash_fwd(qr, kv, kpe, *, t=FLASH_T, hosted=None):
    S = qr.shape[0]
    t = min(t, S)
    nb = S // t
    H = MLA_HEADS

    def body(q_ref, kn_ref, v_ref, kpe_ref, o_ref, lse_ref, m_s, l_s, acc):
        i, j = pl.program_id(1), pl.program_id(2)

        @pl.when(j == 0)
        def _():
            m_s[...] = jnp.full_like(m_s, NEG)
            l_s[...] = jnp.zeros_like(l_s)
            acc[...] = jnp.zeros_like(acc)

        def step(diagonal):
            s = _mla_scores(q_ref[...], kn_ref[...], kpe_ref[...], diagonal)
            m_new = jnp.maximum(m_s[...], jnp.max(s, axis=-1, keepdims=True))
            p = jnp.exp(s - m_new)
            alpha = jnp.exp(m_s[...] - m_new)
            l_s[...] = alpha * l_s[...] + jnp.sum(p, axis=-1, keepdims=True)
            acc[...] = alpha * acc[...] + _bdot(p, v_ref[...])
            m_s[...] = m_new

        _on_and_below_diagonal(i, j, step)

        @pl.when(j == nb - 1)
        def _():
            o_ref[...] = (acc[...] / l_s[...]).astype(o_ref.dtype)
            lse_ref[...] = m_s[...] + jnp.log(l_s[...])

    kmap = lambda off: (lambda h, i, j: (jnp.minimum(j, i), off + h))
    call = _pcall(body, name="mla_flash_fwd", grid=(H, nb, nb),
                  in_specs=[pl.BlockSpec((t, 2 * LANES), lambda h, i, j: (i, h)),
                            pl.BlockSpec((t, LANES), kmap(0)), pl.BlockSpec((t, LANES), kmap(H)),
                            pl.BlockSpec((t, LANES), lambda h, i, j: (jnp.minimum(j, i), 0))],
                  out_specs=[pl.BlockSpec((t, LANES), lambda h, i, j: (i, h)),
                             pl.BlockSpec((None, t, 1), lambda h, i, j: (h, i, 0))],
                  out_shape=[jax.ShapeDtypeStruct((S, H * MLA_V), BF16), jax.ShapeDtypeStruct((H, S, 1), F32)],
                  scratch_shapes=[pltpu.VMEM((t, 1), F32), pltpu.VMEM((t, 1), F32), pltpu.VMEM((t, MLA_V), F32)],
                  dims=("parallel", "parallel", "arbitrary"), hosted=hosted)
    return _hosting(call, hosted, qr, kv, kv, kpe)


def _mla_p_ds(q, kn, v, kpe, do, o, lse, diagonal):
    p = jnp.exp(_mla_scores(q, kn, kpe, diagonal) - lse)
    dof = do.astype(F32)
    delta = jnp.sum(dof * o.astype(F32), axis=-1, keepdims=True)
    dp = lax.dot_general(do.astype(BF16), v, _DN["nt"], preferred_element_type=F32)
    ds = p * (dp - delta) * MLA_SCALE
    return p, ds


def flash_bwd(qr, kv, kpe, o, lse, dy, dy_col0, *, t=FLASH_T, hosted=None):
    S = qr.shape[0]
    t = min(t, S)
    nb = S // t
    H = MLA_HEADS

    def body(q_ref, kn_ref, v_ref, kpe_ref, o_ref, lse_ref, do_ref, dq_ref, dkn_ref, dv_ref, dkpe_ref, acc, akn, av):
        h, i, j = pl.program_id(0), pl.program_id(1), pl.program_id(2)

        @pl.when((h == 0) & (i == 0) & (j == 0))
        def _():
            dkpe_ref[...] = jnp.zeros_like(dkpe_ref)

        @pl.when((i == 0) & (j == 0))
        def _():
            akn[...] = jnp.zeros_like(akn)
            av[...] = jnp.zeros_like(av)

        @pl.when(j == 0)
        def _():
            acc[...] = jnp.zeros_like(acc)

        def step(diagonal):
            q = q_ref[...]
            p, ds = _mla_p_ds(q, kn_ref[...], v_ref[...], kpe_ref[...], do_ref[...], o_ref[...], lse_ref[...],
                              diagonal)
            acc[...] += jnp.concatenate([_bdot(ds, kn_ref[...]), _bdot(ds, kpe_ref[...])], axis=1)
            av[j] += _bdot(p, do_ref[...], _DN["tn"])
            akn[j] += _bdot(ds, q[:, :LANES], _DN["tn"])
            rows = pl.ds(pl.multiple_of(j * t, t), t)
            dkpe_ref[rows, :] += _bdot(ds, q[:, LANES:], _DN["tn"])

        _on_and_below_diagonal(i, j, step)

        @pl.when(j == nb - 1)
        def _():
            dq_ref[...] = acc[...]

        @pl.when((i == nb - 1) & (j == nb - 1))
        def _():
            dkn_ref[...] = akn[...].reshape(S, LANES).astype(dkn_ref.dtype)
            dv_ref[...] = av[...].reshape(S, LANES).astype(dv_ref.dtype)

    kmap = lambda off: (lambda h, i, j: (jnp.minimum(j, i), off + h))
    head_col = pl.BlockSpec((S, LANES), lambda h, i, j: (0, h))
    call = _pcall(body, name="mla_flash_bwd", grid=(H, nb, nb),
                  in_specs=[pl.BlockSpec((t, 2 * LANES), lambda h, i, j: (i, h)),
                            pl.BlockSpec((t, LANES), kmap(0)), pl.BlockSpec((t, LANES), kmap(H)),
                            pl.BlockSpec((t, LANES), lambda h, i, j: (jnp.minimum(j, i), 0)),
                            pl.BlockSpec((t, LANES), lambda h, i, j: (i, h)),
                            pl.BlockSpec((None, t, 1), lambda h, i, j: (h, i, 0)),
                            pl.BlockSpec((t, LANES), lambda h, i, j: (i, dy_col0 + h))],
                  out_specs=[pl.BlockSpec((t, 2 * LANES), lambda h, i, j: (i, h)), head_col, head_col,
                             pl.BlockSpec((S, LANES), lambda h, i, j: (0, 0))],
                  out_shape=[jax.ShapeDtypeStruct((S, H * 2 * LANES), F32), jax.ShapeDtypeStruct((S, H * LANES), BF16),
                             jax.ShapeDtypeStruct((S, H * LANES), BF16), jax.ShapeDtypeStruct((S, LANES), F32)],
                  scratch_shapes=[pltpu.VMEM((t, 2 * LANES), F32), pltpu.VMEM((nb, t, LANES), F32),
                                  pltpu.VMEM((nb, t, LANES), F32)],
                  dims=("arbitrary", "arbitrary", "arbitrary"), hosted=hosted)
    (dq, dkn, dv, dkpe), extra = _hosting(call, hosted, qr, kv, kv, kpe, o, lse, dy)
    return (dq, jnp.concatenate([dkn, dv], axis=1), dkpe), extra


def loss_head(h, target, g, *, tr=256):
    S, D = h.shape
    tr = min(tr, S)

    def body(h_ref, t_ref, g_ref, loss_ref, dh_ref, dg_ref):
        tgt = t_ref[...]

        def f(hh, gg):
            err = jnp.square(_rms(hh, gg) - tgt)
            per_row = jnp.sum(err, axis=-1, keepdims=True) * (0.5 / D)
            return jnp.sum(per_row, axis=0, keepdims=True)

        val, vjp = jax.vjp(f, h_ref[...], g_ref[...])
        dh, dg = vjp(jnp.ones((1, 1), F32))
        dh_ref[...] = dh

        @pl.when(pl.program_id(0) == 0)
        def _():
            loss_ref[...] = jnp.zeros_like(loss_ref)
            dg_ref[...] = jnp.zeros_like(dg_ref)

        loss_ref[...] += jnp.broadcast_to(val, loss_ref.shape)
        dg_ref[...] += dg

    return _pcall(body, name="loss_head", grid=(S // tr,),
                  in_specs=[_row_spec(tr, D, 0), _row_spec(tr, D, 0), _full_spec(g)],
                  out_specs=[pl.BlockSpec((1, LANES), lambda i: (0, 0)), _row_spec(tr, D, 0), _full_spec(g)],
                  out_shape=[jax.ShapeDtypeStruct((1, LANES), F32), jax.ShapeDtypeStruct((S, D), F32),
                             jax.ShapeDtypeStruct(g.shape, F32)],
                  dims=("arbitrary",))(h, target, g)


def _rope_tables(positions, dim):
    inv_freq = ROPE_THETA ** (-jnp.arange(0, dim, 2, dtype=F32) / dim)
    ang = positions.astype(F32)[:, None] * inv_freq
    return jnp.cos(ang), jnp.sin(ang)


def _pad_cols(w, n):
    return jnp.pad(w, ((0, 0), (0, n - w.shape[1])))


def _prep_w_in0(w):
    return jnp.concatenate([w[:, :4096], w[:, 4112:5136], _pad_cols(w[:, 5136:5200], LANES),
                            _pad_cols(w[:, 4096:4112], LANES)], axis=1)


def _unprep_w_in0(g):
    return jnp.concatenate([g[:, :4096], g[:, ZIN_AB:ZIN_AB + 16], g[:, ZIN_CQ:ZIN_KR], g[:, ZIN_KR:ZIN_KR + MLA_ROPE]],
                           axis=1)


def _prep_w_uq(w):
    w = w.reshape(MLA_Q_RANK, MLA_HEADS, MLA_NOPE + MLA_ROPE)
    w = jnp.pad(w, ((0, 0), (0, 0), (0, 2 * LANES - MLA_NOPE - MLA_ROPE)))
    return w.reshape(MLA_Q_RANK, MLA_HEADS * 2 * LANES)


def _unprep_w_uq(g):
    g = g.reshape(MLA_Q_RANK, MLA_HEADS, 2 * LANES)[:, :, :MLA_NOPE + MLA_ROPE]
    return g.reshape(MLA_Q_RANK, MLA_HEADS * (MLA_NOPE + MLA_ROPE))


def _prep_w_ukv(w):
    w = w.reshape(MLA_KV_RANK, MLA_HEADS, 2, LANES)
    return jnp.transpose(w, (0, 2, 1, 3)).reshape(MLA_KV_RANK, 2 * MLA_HEADS * LANES)


def _unprep_w_ukv(g):
    g = g.reshape(MLA_KV_RANK, 2, MLA_HEADS, LANES)
    return jnp.transpose(g, (0, 2, 1, 3)).reshape(MLA_KV_RANK, 2 * MLA_HEADS * LANES)


def _row(v, n=None):
    v = v.reshape(1, -1).astype(F32)
    return v if n is None else _pad_cols(v, n)


def _ffn_fwd(h, norm_g, w_up, conv_w, conv_b, w_down, tag, hosted=None):
    (hn,) = rowwise(_fn_rms, [(h, D_MODEL, 0)], [norm_g], [], [(D_MODEL, BF16)], name=f"{tag}_ffn_norm")
    u = matmul(hn, w_up, "nn", F32, b_shards=N_CHIPS, **TILES["wide_nn"], name=f"{tag}_ffn_up", hosted=hosted)
    u, got = u if hosted is not None else (u, [])
    f = ffn_conv_fwd(u, conv_w, conv_b)
    h_out = matmul(f, w_down, "nn", F32, add=h, tm=512, tn=1024, tk=8192, name=f"{tag}_ffn_down")
    return h_out, (hn, u, f), got


def _ffn_bwd(dh, dh16, h, norm_g, w_up, conv_w, conv_b, w_down, saved, tag, make_hosted=None):
    hn, u, f = saved
    df = matmul(dh16, w_down, "nt", BF16, tm=512, tn=2816, name=f"{tag}_ffn_down_dx")
    g_down = matmul(f, dh16, "tn", BF16, **TILES["dw"], name=f"{tag}_ffn_down_dw")
    du_g, du_u, g_conv_w, g_conv_b = ffn_conv_bwd(u, conv_w, conv_b, df)
    pair = N_CHIPS // 2
    g_up = jnp.concatenate([matmul(hn, du_, "tn", BF16, out_shards=pair, tm=1024, tn=1408, tk=4096,
                                   name=f"{tag}_ffn_up_dw_{part}") for part, du_ in (("gate", du_g), ("up", du_u))])
    hosted = make_hosted(dict(ffn_w_up=g_up, ffn_w_down=g_down)) if make_hosted else None
    dhn_gate = matmul(du_g, w_up, "nt", F32, b_shards=pair, tm=512, tn=2048, tk=2816, name=f"{tag}_ffn_up_dx_gate")
    dhn = matmul(du_u, w_up, "nt", F32, b_shards=pair, b_shard0=pair, add=dhn_gate, tm=512, tn=2048, tk=2816,
                 name=f"{tag}_ffn_up_dx_up", hosted=hosted)
    dhn, got = dhn if hosted is not None else (dhn, [])
    ((dh_in, dh_in16),), (g_norm,) = rowwise_bwd(_fn_rms, [(h, D_MODEL, 0)], [norm_g], [], [(dhn, D_MODEL, 0)],
                                                 [(F32, BF16)], adds=[(dh, D_MODEL, 0)], name=f"{tag}_ffn_norm_bwd")
    return dh_in, dh_in16, dict(ffn_norm=g_norm, ffn_w_up=g_up, ffn_conv_w=g_conv_w, ffn_conv_b=g_conv_b,
                                ffn_w_down=g_down), got


TILES = {"wide_nn": dict(tm=512, tn=3072, tk=2048),
         "square": dict(tm=512, tn=2048, tk=2048),
         "dw": dict(tm=512, tn=2048, tk=4096)}


def _ple_fwd(h, p_i, w_proj, gate_g, w_gate, tag):
    (hg,) = rowwise(_fn_rms, [(h, D_MODEL, 0)], [gate_g], [], [(D_MODEL, BF16)], name=f"{tag}_ple_norm")
    gl = matmul(hg, w_gate, "nn", F32, **TILES["square"], name=f"{tag}_ple_gate")
    pp = matmul(p_i, w_proj, "nn", F32, b_shards=N_CHIPS, name=f"{tag}_ple_proj")
    (h_out,) = rowwise(_fn_ple, [(h, D_MODEL, 0), (pp, D_MODEL, 0), (gl, D_MODEL, 0)], [], [], [(D_MODEL, F32)],
                       name=f"{tag}_ple_add")
    return h_out, (hg, gl, pp)


def _ple_bwd(dh, h, p_i, w_proj, gate_g, w_gate, saved, tag):
    hg, gl, pp = saved
    (dpp, dgl), _ = rowwise_bwd(_fn_ple_terms, [(pp, D_MODEL, 0), (gl, D_MODEL, 0)], [], [], [(dh, D_MODEL, 0)],
                                [BF16, BF16], name=f"{tag}_ple_add_bwd")
    g_proj = matmul(p_i, dpp, "tn", BF16, out_shards=N_CHIPS, name=f"{tag}_ple_proj_dw")
    g_gate = matmul(hg, dgl, "tn", BF16, **TILES["dw"], name=f"{tag}_ple_gate_dw")
    dh_in, dh_in16, g_norm = matmul(dgl, w_gate, "nt", F32, tm=256, tn=2048, tk=2048, name=f"{tag}_ple_gate_dx",
                                    norm_bwd=(h, gate_g, dh))
    return dh_in, dh_in16, dict(ple_proj=g_proj, ple_gate_norm=g_norm, ple_gate=g_gate)


def local_step(x, p, positions, target, slots, W, ex=None):
    S = x.shape[0]
    G = {}
    p0, p1 = p[0].astype(BF16), p[1].astype(BF16)
    W = dict(W)

    def use(names, bufs):
        for k, b in zip(names, bufs):
            r, c = b.shape[1:]
            W[k] = b.reshape(N_CHIPS * r, c) if k in ROW_SHARDED else (b if k in KEPT_SHARDED else _cols_to_full(b))

    def by_chip(names):
        out = []
        for k in names:
            r, c = slots[k].shape[1:]
            out.append(G[k].reshape(N_CHIPS, r, c) if k in ROW_SHARDED
                       else (G[k] if k in KEPT_SHARDED else _full_to_cols(G[k])))
        return out

    first = [slots[k] for k in GATHER_FIRST]
    use(GATHER_FIRST, ex.gather(first) if ex else first)

    cm, sm = _rope_tables(positions, MLA_ROPE)
    zeros = jnp.zeros((S, LANES - MLA_ROPE), F32)
    cosp = jnp.concatenate([cm, cm, zeros], axis=1)
    sinp = jnp.concatenate([sm, sm, zeros], axis=1)
    cr, sr = _rope_tables(positions, RET_DK)

    w_in0 = _prep_w_in0(W["l0_w_in"])
    a_row = _row(W["l0_gdn_A_log"], LANES)
    dt_row = _row(W["l0_gdn_dt_bias"], LANES)
    gdn_nw = _row(W["l0_gdn_norm"])
    n = {k: _row(W[k]) for k in ("l0_attn_norm", "l0_mla_q_norm", "l0_mla_kv_norm", "l0_ffn_norm",
                                 "l0_ple_gate_norm", "l1_attn_norm", "l1_ret_norm", "l1_ffn_norm",
                                 "l1_ple_gate_norm", "final_norm", "l0_ffn_conv_b", "l1_ffn_conv_b")}

    (hn0,) = rowwise(_fn_rms, [(x, D_MODEL, 0)], [n["l0_attn_norm"]], [], [(D_MODEL, BF16)], name="l0_attn_norm")
    zin = matmul(hn0, w_in0, "nn", F32, tm=512, tn=1792, name="l0_w_in")
    qkv = gdn_conv_fwd(zin, W["l0_gdn_conv"])
    layer0 = [slots[k] for k in GATHER_L0]
    (y_a, gdn_states), got = gdn_fwd(qkv, zin, a_row, dt_row, gdn_nw, hosted=hosted_gather(layer0) if ex else None)
    use(GATHER_L0, got if ex else layer0)
    w_uq = _prep_w_uq(W["l0_mla_w_uq"])
    w_ukv = _prep_w_ukv(W["l0_mla_w_ukv"])
    mla_rows = [(zin, MLA_Q_RANK, ZIN_CQ // MLA_Q_RANK), (zin, MLA_KV_RANK, ZIN_CKV // MLA_KV_RANK),
                (zin, LANES, ZIN_KR // LANES)]
    mla_nd = [(cosp, LANES, 0), (sinp, LANES, 0)]
    cqn, ckvn, kpe = rowwise(_fn_mla_pre, mla_rows, [n["l0_mla_q_norm"], n["l0_mla_kv_norm"]], mla_nd,
                             [(MLA_Q_RANK, BF16), (MLA_KV_RANK, BF16), (LANES, BF16)], name="mla_pre")
    q_lin = matmul(cqn, w_uq, "nn", F32, name="mla_w_uq")
    kv = matmul(ckvn, w_ukv, "nn", BF16, name="mla_w_ukv")
    (qr,) = rowwise(_fn_rope_q, [(q_lin, 2048, 0)], [], mla_nd, [(2048, BF16)],
                    name="mla_rope_q")
    layer1 = [slots[k] for k in GATHER_L1]
    (y_b, lse), got = flash_fwd(qr, kv, kpe, hosted=hosted_gather(layer1) if ex else None)
    use(GATHER_L1, got if ex else layer1)
    y_ab = jnp.concatenate([y_a, y_b], axis=1)
    h1 = matmul(y_ab, W["l0_w_out"], "nn", F32, add=x, **TILES["square"], name="l0_w_out")
    ffn_late = [slots[k] for k in GATHER_FFN]
    h2, ffn0, got = _ffn_fwd(h1, n["l0_ffn_norm"], W["l0_ffn_w_up"], W["l0_ffn_conv_w"], n["l0_ffn_conv_b"],
                             W["l0_ffn_w_down"], "l0", hosted=hosted_gather(ffn_late) if ex else None)
    use(GATHER_FFN, got if ex else ffn_late)
    h3, ple0 = _ple_fwd(h2, p0, W["l0_ple_proj"], n["l0_ple_gate_norm"], W["l0_ple_gate"], "l0")

    (hn1,) = rowwise(_fn_rms, [(h3, D_MODEL, 0)], [n["l1_attn_norm"]], [], [(D_MODEL, BF16)], name="l1_attn_norm")
    late = [slots[k] for k in GATHER_L1_IN]
    zz = matmul(hn1, W["l1_w_in"], "nn", F32, b_shards=N_CHIPS, **TILES["wide_nn"], name="l1_w_in",
                hosted=hosted_gather(late) if ex else None)
    zz, got = zz if ex else (zz, late)
    use(GATHER_L1_IN, got)
    (o_ret, ret_states), _ = ret_fwd(zz, cr, sr)
    gate_rows = [(zz, 4096, 2), (o_ret, 4096, 0)]
    (yg,) = rowwise(_fn_ret_gate, gate_rows, [n["l1_ret_norm"]], [], [(4096, BF16)], name="ret_gate")
    h4 = matmul(yg, W["l1_w_out"], "nn", F32, add=h3, tm=512, tn=2048, tk=4096, name="l1_w_out")
    h5, ffn1, _ = _ffn_fwd(h4, n["l1_ffn_norm"], W["l1_ffn_w_up"], W["l1_ffn_conv_w"], n["l1_ffn_conv_b"],
                           W["l1_ffn_w_down"], "l1")
    h6, ple1 = _ple_fwd(h5, p1, W["l1_ple_proj"], n["l1_ple_gate_norm"], W["l1_ple_gate"], "l1")

    loss_vec, dh, G["final_norm"] = loss_head(h6, target, n["final_norm"])

    dh, dh16, g = _ple_bwd(dh, h5, p1, W["l1_ple_proj"], n["l1_ple_gate_norm"], W["l1_ple_gate"], ple1, "l1")
    G.update({"l1_" + k: v for k, v in g.items()})
    dh, dh16, g, _ = _ffn_bwd(dh, dh16, h4, n["l1_ffn_norm"], W["l1_ffn_w_up"], W["l1_ffn_conv_w"],
                              n["l1_ffn_conv_b"], W["l1_ffn_w_down"], ffn1, "l1")
    G.update({"l1_" + k: v for k, v in g.items()})

    dyg = matmul(dh16, W["l1_w_out"], "nt", F32, tm=512, tn=4096, name="l1_w_out_dx")
    G["l1_w_out"] = matmul(yg, dh16, "tn", BF16, **TILES["dw"], name="l1_w_out_dw")
    (dg, do_ret), (G["l1_ret_norm"],) = rowwise_bwd(_fn_ret_gate, gate_rows, [n["l1_ret_norm"]], [],
                                                   [(dyg, 4096, 0)], [BF16, F32], name="ret_gate_bwd")
    dq, dk, dv = ret_bwd(zz, cr, sr, ret_states, do_ret)
    dzz = jnp.concatenate([dq, dk, dv, dg], axis=1)
    G["l1_w_in"] = matmul(hn1, dzz, "tn", BF16, out_shards=N_CHIPS, tm=1024, tn=1536, tk=4096, name="l1_w_in_dw")
    sums, landed = {}, {}
    grads_l1 = by_chip(REDUCE_L1)
    dhn = matmul(dzz, W["l1_w_in"], "nt", F32, b_shards=N_CHIPS, tm=1024, tn=1024, tk=3072, name="l1_w_in_dx",
                 hosted=hosted_swap(grads_l1) if ex else None)
    if ex:
        dhn, swapped = dhn
        sums.update(zip(REDUCE_L1, ex.pair_sums(REDUCE_L1, grads_l1, swapped)))
    (dh,), (G["l1_attn_norm"],) = rowwise_bwd(_fn_rms, [(h3, D_MODEL, 0)], [n["l1_attn_norm"]], [],
                                             [(dhn, D_MODEL, 0)], [F32], adds=[(dh, D_MODEL, 0)],
                                             name="l1_attn_norm_bwd")

    dh, dh16, g = _ple_bwd(dh, h2, p0, W["l0_ple_proj"], n["l0_ple_gate_norm"], W["l0_ple_gate"], ple0, "l0")
    G.update({"l0_" + k: v for k, v in g.items()})
    mid = REDUCE_DQ + REDUCE_L0

    def swap_mid(g_ffn):
        G.update({"l0_" + k: v for k, v in g_ffn.items()})
        return hosted_swap(by_chip(mid))

    dh, dh16, g, swapped = _ffn_bwd(dh, dh16, h1, n["l0_ffn_norm"], W["l0_ffn_w_up"], W["l0_ffn_conv_w"],
                                    n["l0_ffn_conv_b"], W["l0_ffn_w_down"], ffn0, "l0",
                                    make_hosted=swap_mid if ex else None)
    G.update({"l0_" + k: v for k, v in g.items()})
    if ex:
        sums.update(zip(mid, ex.pair_sums(mid, by_chip(mid), swapped)))

    dy_ab = matmul(dh16, W["l0_w_out"], "nt", F32, **TILES["square"], name="l0_w_out_dx")
    G["l0_w_out"] = matmul(y_ab, dh16, "tn", BF16, **TILES["dw"], name="l0_w_out_dw")
    (dq, dk, dv, dz, dab, g_a, g_dt, G["l0_gdn_norm"]), got = gdn_bwd(
        qkv, zin, a_row, dt_row, gdn_nw, gdn_states, dy_ab, 0,
        hosted=hosted_scatter([sums[k] for k in REDUCE_L1]) if ex else None)
    landed.update(zip(REDUCE_L1, got))
    G["l0_gdn_A_log"], G["l0_gdn_dt_bias"] = g_a[:, :GDN_HEADS], g_dt[:, :GDN_HEADS]
    dqkv, G["l0_gdn_conv"] = gdn_conv_bwd(zin, W["l0_gdn_conv"], jnp.concatenate([dq, dk, dv], axis=1))
    (dqr, dkv, dkpe), got = flash_bwd(qr, kv, kpe, y_b, lse, dy_ab, MLA_HEADS,
                                      hosted=hosted_scatter([sums[k] for k in mid]) if ex else None)
    landed.update(zip(mid, got))
    (dq_lin,), _ = rowwise_bwd(_fn_rope_q, [(q_lin, 2048, 0)], [], mla_nd, [(dqr, 2048, 0)], [BF16],
                               name="mla_rope_q_bwd")
    G["l0_mla_w_uq"] = _unprep_w_uq(matmul(cqn, dq_lin, "tn", BF16, name="mla_w_uq_dw"))
    dcqn = matmul(dq_lin, w_uq, "nt", F32, name="mla_w_uq_dx")
    G["l0_mla_w_ukv"] = _unprep_w_ukv(matmul(ckvn, dkv, "tn", BF16, name="mla_w_ukv_dw"))
    dckvn = matmul(dkv, w_ukv, "nt", F32, name="mla_w_ukv_dx")
    (dcq, dckv, dkr), (G["l0_mla_q_norm"], G["l0_mla_kv_norm"]) = rowwise_bwd(
        _fn_mla_pre, mla_rows, [n["l0_mla_q_norm"], n["l0_mla_kv_norm"]], mla_nd,
        [(dcqn, MLA_Q_RANK, 0), (dckvn, MLA_KV_RANK, 0), (dkpe, LANES, 0)], [BF16, BF16, BF16], name="mla_pre_bwd")
    dzin = jnp.concatenate([dqkv, dz, dcq, dckv, dkr, dab.astype(BF16)], axis=1)
    early = REDUCE_L1 + REDUCE_DQ + REDUCE_L0
    reduced = {}
    g_in = matmul(hn0, dzin, "tn", BF16, tm=512, tn=1792, tk=4096, name="l0_w_in_dw",
                  hosted=hosted_join(ex.halves(early, sums, landed)) if ex else None)
    if ex:
        g_in, joined = g_in
        reduced.update(zip(early, joined))
    G["l0_w_in"] = _unprep_w_in0(g_in)
    if ex:
        sums.update(zip(REDUCE_LAST, ex.pair_sums(REDUCE_LAST, by_chip(REDUCE_LAST))))
    dhn = matmul(dzin, w_in0, "nt", F32, tm=512, tn=2048, tk=5376, name="l0_w_in_dx",
                 hosted=hosted_scatter([sums[k] for k in REDUCE_LAST]) if ex else None)
    if ex:
        dhn, got = dhn
        landed.update(zip(REDUCE_LAST, got))
    (grad_x,), (G["l0_attn_norm"],) = rowwise_bwd(_fn_rms, [(x, D_MODEL, 0)], [n["l0_attn_norm"]], [],
                                                 [(dhn, D_MODEL, 0)], [F32], adds=[(dh, D_MODEL, 0)],
                                                 name="l0_attn_norm_bwd")
    small = {k: G[k] for k in SMALL}
    if not ex:
        return loss_vec[0, 0], grad_x, dict(zip(BIG, by_chip(BIG))), small
    reduced.update(zip(REDUCE_LAST, pair_join_halves(ex.halves(REDUCE_LAST, sums, landed))))
    return loss_vec[0, 0], grad_x, reduced, small


HBM = pl.BlockSpec(memory_space=pltpu.HBM)
VMEM = pl.BlockSpec(memory_space=pltpu.VMEM)


def _place():
    return lax.axis_index("x"), lax.axis_index("y"), lax.axis_index("c")


def _other_chips(x, y):
    return [(1 - x, y), (x, 1 - y), (1 - x, 1 - y)]


def _comm_call(body, *, name, out_shape, in_specs, out_specs, scratch_shapes):
    return pl.pallas_call(body, name=name, out_shape=out_shape, in_specs=in_specs, out_specs=out_specs,
                          scratch_shapes=list(scratch_shapes),
                          compiler_params=pltpu.CompilerParams(vmem_limit_bytes=VMEM_LIMIT_MB << 20))


def _inplace_comm_call(body, bufs, *, name, n_sems):
    n = len(bufs)
    return pl.pallas_call(body, name=name, out_shape=[jax.ShapeDtypeStruct(b.shape, b.dtype) for b in bufs],
                          in_specs=[HBM] * n, out_specs=[HBM] * n, input_output_aliases={i: i for i in range(n)},
                          scratch_shapes=[pltpu.SemaphoreType.DMA((n_sems,)), pltpu.SemaphoreType.DMA((n_sems,))],
                          compiler_params=pltpu.CompilerParams(vmem_limit_bytes=VMEM_LIMIT_MB << 20))(*bufs)


def all_gather_chips(bufs):
    n_sems, start, finish = _gather_phase(len(bufs))
    n = len(bufs)

    def body(*refs):
        outs, send_sems, recv_sems = refs[n:2 * n], refs[2 * n], refs[2 * n + 1]
        start(None, outs, send_sems, recv_sems)
        finish(None, outs, send_sems, recv_sems)

    return _inplace_comm_call(body, bufs, name="all_gather_chips", n_sems=n_sems)


def _gather_phase(n):
    def plan(outs, send_sems, recv_sems):
        x, y, c = _place()

        def copy(w, k, chip, hc, to):
            half = outs[w].shape[1] // 2
            rows = outs[w].at[2 * chip[0] + chip[1], pl.ds(hc * half, half), :]
            return pltpu.make_async_remote_copy(src_ref=rows, dst_ref=rows, send_sem=send_sems.at[6 * w + k],
                                                recv_sem=recv_sems.at[6 * w + k], device_id=to, device_id_type=MESH)

        first = [[copy(w, k, (x, y), c, (*chip, c)) for k, chip in enumerate(_other_chips(x, y))] for w in range(n)]
        passed = [[copy(w, 3 + k, chip, c, (x, y, 1 - c)) for k, chip in enumerate(_other_chips(x, y))]
                  for w in range(n)]
        return copy, first, passed, (x, y, c)

    def start(_, outs, send_sems, recv_sems):
        _, first, _, _ = plan(outs, send_sems, recv_sems)
        for w in range(n):
            for cp in first[w]:
                cp.start()

    def finish(_, outs, send_sems, recv_sems):
        copy, first, passed, (x, y, c) = plan(outs, send_sems, recv_sems)
        chips = _other_chips(x, y)
        for w in range(n):
            for k, chip in enumerate(chips):
                copy(w, k, chip, c, (x, y, c)).wait_recv()
                passed[w][k].start()
        for w in range(n):
            for k, chip in enumerate(chips):
                copy(w, 3 + k, chip, 1 - c, (x, y, c)).wait_recv()
        for w in range(n):
            for cp in first[w] + passed[w]:
                cp.wait_send()

    return 6 * n, start, finish


def hosted_gather(bufs):
    n_sems, start, finish = _gather_phase(len(bufs))
    return Hosted(bufs, [jax.ShapeDtypeStruct(b.shape, b.dtype) for b in bufs], {i: i for i in range(len(bufs))},
                  n_sems, start, finish)


def pair_swap_halves(gs):
    n = len(gs)
    n_sems, start, finish = _swap_phase(n)

    def body(*refs):
        g_refs, o_refs, send_sems, recv_sems = refs[:n], refs[n:2 * n], refs[2 * n], refs[2 * n + 1]
        start(g_refs, o_refs, send_sems, recv_sems)
        finish(g_refs, o_refs, send_sems, recv_sems)

    return _comm_call(body, name="pair_swap_halves", out_shape=_swap_shapes(gs), in_specs=[HBM] * n, out_specs=[HBM] * n,
                      scratch_shapes=[pltpu.SemaphoreType.DMA((n_sems,)), pltpu.SemaphoreType.DMA((n_sems,))])(*gs)


def _swap_shapes(gs):
    return [jax.ShapeDtypeStruct((N_CHIPS, g.shape[1] // 2, g.shape[2]), g.dtype) for g in gs]


def _swap_phase(n):
    def copies(g_refs, o_refs, send_sems, recv_sems):
        x, y, c = _place()
        out = []
        for w in range(n):
            half = g_refs[w].shape[1] // 2
            out.append(pltpu.make_async_remote_copy(
                src_ref=g_refs[w].at[:, pl.ds((1 - c) * half, half), :], dst_ref=o_refs[w], send_sem=send_sems.at[w],
                recv_sem=recv_sems.at[w], device_id=(x, y, 1 - c), device_id_type=MESH))
        return out

    def start(*refs):
        for cp in copies(*refs):
            cp.start()

    def finish(*refs):
        for cp in copies(*refs):
            cp.wait()

    return n, start, finish


def hosted_swap(gs):
    n_sems, start, finish = _swap_phase(len(gs))
    return Hosted(gs, _swap_shapes(gs), {}, n_sems, start, finish)


def scatter_chips(ps):
    n = len(ps)
    n_sems, start, finish = _scatter_phase(n)

    def body(*refs):
        p_refs, o_refs, send_sems, recv_sems = refs[:n], refs[n:2 * n], refs[2 * n], refs[2 * n + 1]
        start(p_refs, o_refs, send_sems, recv_sems)
        finish(p_refs, o_refs, send_sems, recv_sems)

    return _comm_call(body, name="scatter_chips", out_shape=_scatter_shapes(ps), in_specs=[HBM] * n, out_specs=[HBM] * n,
                      scratch_shapes=[pltpu.SemaphoreType.DMA((n_sems,)), pltpu.SemaphoreType.DMA((n_sems,))])(*ps)


def _scatter_shapes(ps):
    return [jax.ShapeDtypeStruct((3,) + p.shape[1:], p.dtype) for p in ps]


def _scatter_phase(n):
    def copies(p_refs, o_refs, send_sems, recv_sems):
        x, y, c = _place()
        return [pltpu.make_async_remote_copy(src_ref=p_refs[w].at[2 * chip[0] + chip[1]], dst_ref=o_refs[w].at[k],
                                             send_sem=send_sems.at[3 * w + k], recv_sem=recv_sems.at[3 * w + k],
                                             device_id=(*chip, c), device_id_type=MESH)
                for w in range(n) for k, chip in enumerate(_other_chips(x, y))]

    def start(*refs):
        for cp in copies(*refs):
            cp.start()

    def finish(*refs):
        for cp in copies(*refs):
            cp.wait()

    return 3 * n, start, finish


def hosted_scatter(ps):
    n_sems, start, finish = _scatter_phase(len(ps))
    return Hosted(ps, _scatter_shapes(ps), {}, n_sems, start, finish)


def pair_join_halves(rs):
    n = len(rs)
    n_sems, start, finish = _join_phase(n)

    def body(*refs):
        outs, send_sems, recv_sems = refs[n:2 * n], refs[2 * n], refs[2 * n + 1]
        start(None, outs, send_sems, recv_sems)
        finish(None, outs, send_sems, recv_sems)

    return _inplace_comm_call(body, rs, name="pair_join_halves", n_sems=n_sems)


def _join_phase(n):
    def copies(_, outs, send_sems, recv_sems):
        x, y, c = _place()
        out = []
        for w in range(n):
            half = outs[w].shape[0] // 2
            rows = outs[w].at[pl.ds(c * half, half), :]
            out.append(pltpu.make_async_remote_copy(src_ref=rows, dst_ref=rows, send_sem=send_sems.at[w],
                                                    recv_sem=recv_sems.at[w], device_id=(x, y, 1 - c),
                                                    device_id_type=MESH))
        return out

    def start(*refs):
        for cp in copies(*refs):
            cp.start()

    def finish(*refs):
        for cp in copies(*refs):
            cp.wait()

    return n, start, finish


def hosted_join(rs):
    n_sems, start, finish = _join_phase(len(rs))
    return Hosted(rs, [jax.ShapeDtypeStruct(r.shape, r.dtype) for r in rs], {i: i for i in range(len(rs))},
                  n_sems, start, finish)


def all_reduce_small(v, name):
    n, L = v.shape
    n_dev = 8

    def body(v_ref, out_ref, buf, send_sems, recv_sems):
        x, y, c = _place()
        me = 4 * x + 2 * y + c
        buf[me] = v_ref[...]

        def copy(k, slot, peer):
            return pltpu.make_async_remote_copy(src_ref=v_ref, dst_ref=buf.at[slot], send_sem=send_sems.at[k],
                                                recv_sem=recv_sems.at[slot],
                                                device_id=(peer // 4, (peer // 2) % 2, peer % 2), device_id_type=MESH)

        sends = [copy(k - 1, me, (me + k) % n_dev) for k in range(1, n_dev)]
        for cp in sends:
            cp.start()
        for k in range(1, n_dev):
            src = (me + k) % n_dev
            copy(0, src, src).wait_recv()
        for cp in sends:
            cp.wait_send()
        acc = buf[0]
        for s in range(1, n_dev):
            acc = acc + buf[s]
        out_ref[...] = acc

    return _comm_call(body, name=name, out_shape=jax.ShapeDtypeStruct((n, L), v.dtype), in_specs=[VMEM], out_specs=VMEM,
                      scratch_shapes=[pltpu.VMEM((n_dev, n, L), v.dtype), pltpu.SemaphoreType.DMA((n_dev - 1,)),
                                      pltpu.SemaphoreType.DMA((n_dev,))])(v)


BF16_ROWS = 16
STREAM_BLOCK_BYTES = 4 << 20


def _rows_tile(n, row_bytes, budget=1 << 20, mult=SUBLANES):
    best = mult if n % mult == 0 else n
    for t in range(mult, n + 1, mult):
        if n % t == 0 and t * row_bytes <= budget:
            best = t
    return best


def _scalars(*vals):
    return jnp.stack([jnp.asarray(v, jnp.int32) for v in vals])


def cast_to_slot(w, chip, name):
    r, c = w.shape
    tb = _rows_tile(r, c * 4, budget=STREAM_BLOCK_BYTES, mult=BF16_ROWS)

    def body(s_ref, w_ref, o_ref):
        o_ref[...] = w_ref[...].astype(BF16)

    spec = pltpu.PrefetchScalarGridSpec(
        num_scalar_prefetch=1, grid=(r // tb,), in_specs=[pl.BlockSpec((tb, c), lambda i, s: (i, 0))],
        out_specs=pl.BlockSpec((None, tb, c), lambda i, s: (s[0], i, 0)))
    return pl.pallas_call(body, name=name, grid_spec=spec, out_shape=jax.ShapeDtypeStruct((N_CHIPS, r, c), BF16),
                          compiler_params=pltpu.CompilerParams(dimension_semantics=("parallel",)))(_scalars(chip), w)


def pair_add(g, got, c, name):
    _, r, w = g.shape
    half = r // 2
    tb = _rows_tile(half, w * 4, budget=STREAM_BLOCK_BYTES, mult=BF16_ROWS)
    nb = half // tb

    def body(c_ref, g_ref, got_ref, o_ref):
        o_ref[...] = (g_ref[...].astype(F32) + got_ref[...].astype(F32)).astype(o_ref.dtype)

    spec = pltpu.PrefetchScalarGridSpec(
        num_scalar_prefetch=1, grid=(N_CHIPS, nb),
        in_specs=[pl.BlockSpec((None, tb, w), lambda s, i, c_ref: (s, c_ref[0] * nb + i, 0)),
                  pl.BlockSpec((None, tb, w), lambda s, i, c_ref: (s, i, 0))],
        out_specs=pl.BlockSpec((None, tb, w), lambda s, i, c_ref: (s, i, 0)))
    return pl.pallas_call(body, name=name, grid_spec=spec, out_shape=jax.ShapeDtypeStruct((N_CHIPS, half, w), BF16),
                          compiler_params=pltpu.CompilerParams(dimension_semantics=("parallel", "parallel")))(
        _scalars(c), g, got)


def chip_add(p, got, chip, c, name):
    _, h, w = p.shape
    tb = _rows_tile(h, w * 4, budget=STREAM_BLOCK_BYTES, mult=BF16_ROWS)
    nb = h // tb

    def body(s_ref, p_ref, got_ref, o_ref):
        acc = p_ref[...].astype(F32)
        for k in range(3):
            acc = acc + got_ref[k].astype(F32)
        o_ref[...] = acc

    spec = pltpu.PrefetchScalarGridSpec(
        num_scalar_prefetch=1, grid=(nb,),
        in_specs=[pl.BlockSpec((None, tb, w), lambda i, s: (s[0], i, 0)),
                  pl.BlockSpec((3, tb, w), lambda i, s: (0, i, 0))],
        out_specs=pl.BlockSpec((tb, w), lambda i, s: (s[1] * nb + i, 0)))
    return pl.pallas_call(body, name=name, grid_spec=spec, out_shape=jax.ShapeDtypeStruct((2 * h, w), F32),
                          compiler_params=pltpu.CompilerParams(dimension_semantics=("parallel",)))(
        _scalars(chip, c), p, got)


def adamw(w, g, m, v, name, pass_grad=False):
    r, c = w.shape
    tr = _rows_tile(r, c * 4, budget=STREAM_BLOCK_BYTES // 2)
    n_out = 4 if pass_grad else 3

    def body(w_ref, g_ref, m_ref, v_ref, d_ref, m_out, v_out, *g_out):
        gg = g_ref[...]
        if pass_grad:
            g_out[0][...] = gg
        m2 = ADAM_B1 * m_ref[...] + (1.0 - ADAM_B1) * gg
        v2 = ADAM_B2 * v_ref[...] + (1.0 - ADAM_B2) * jnp.square(gg)
        m_hat = m2 / (1.0 - ADAM_B1 ** ADAM_STEP)
        v_hat = v2 / (1.0 - ADAM_B2 ** ADAM_STEP)
        d_ref[...] = -ADAM_LR * (m_hat / (jnp.sqrt(v_hat) + ADAM_EPS) + ADAM_WD * w_ref[...])
        m_out[...] = m2
        v_out[...] = v2

    blk = pl.BlockSpec((tr, c), lambda i: (i, 0))
    return _pcall(body, name=name, grid=(r // tr,), in_specs=[blk] * 4, out_specs=[blk] * n_out,
                  out_shape=[jax.ShapeDtypeStruct((r, c), F32)] * n_out, dims=("parallel",))(w, g, m, v)


WEIGHTS = ["l0_attn_norm", "l0_w_in", "l0_gdn_conv", "l0_gdn_A_log", "l0_gdn_dt_bias", "l0_gdn_norm", "l0_mla_q_norm",
           "l0_mla_w_uq", "l0_mla_kv_norm", "l0_mla_w_ukv", "l0_w_out", "l0_ffn_norm", "l0_ffn_w_up", "l0_ffn_conv_w",
           "l0_ffn_conv_b", "l0_ffn_w_down", "l0_ple_proj", "l0_ple_gate_norm", "l0_ple_gate", "l1_attn_norm",
           "l1_w_in", "l1_ret_norm", "l1_w_out", "l1_ffn_norm", "l1_ffn_w_up", "l1_ffn_conv_w", "l1_ffn_conv_b",
           "l1_ffn_w_down", "l1_ple_proj", "l1_ple_gate_norm", "l1_ple_gate", "final_norm"]
COL_SHARDED = ["l0_w_in", "l0_mla_w_uq", "l0_mla_w_ukv", "l0_ffn_w_up", "l0_ple_proj", "l1_w_in", "l1_ffn_w_up",
               "l1_ple_proj"]
ROW_SHARDED = ["l0_w_out", "l0_ffn_w_down", "l0_ple_gate", "l1_w_out", "l1_ffn_w_down", "l1_ple_gate"]
BIG = [k for k in WEIGHTS if k in COL_SHARDED or k in ROW_SHARDED]
SMALL_SHARDED = ["l0_gdn_conv", "l0_ffn_conv_w", "l1_ffn_conv_w"]
SMALL = [k for k in WEIGHTS if k not in BIG]
KEPT_SHARDED = ["l0_ffn_w_up", "l0_ple_proj", "l1_w_in", "l1_ffn_w_up", "l1_ple_proj"]
GATHER_FIRST = ["l0_w_in"]
GATHER_L0 = ["l0_mla_w_uq", "l0_mla_w_ukv", "l0_w_out", "l0_ffn_w_up", "l0_ffn_w_down", "l0_ple_proj", "l0_ple_gate",
             "l1_w_out"]
GATHER_L1 = ["l1_ffn_w_down", "l1_ple_proj", "l1_ple_gate"]
GATHER_FFN = ["l1_w_in"]
GATHER_L1_IN = ["l1_ffn_w_up"]
REDUCE_L1 = [k for k in BIG if k.startswith("l1_")]
REDUCE_DQ = ["l0_ffn_w_up"]
REDUCE_L0 = ["l0_ffn_w_down", "l0_ple_proj", "l0_ple_gate"]
REDUCE_LAST = ["l0_w_in", "l0_mla_w_uq", "l0_mla_w_ukv", "l0_w_out"]


class Exchange:
    def __init__(self, chip, core):
        self.chip, self.core = chip, core

    def gather(self, bufs):
        return all_gather_chips(bufs)

    def pair_sums(self, names, grads, swapped=None):
        swapped = pair_swap_halves(grads) if swapped is None else swapped
        return [pair_add(g, got, self.core, "rs_pair_add_" + k) for k, g, got in zip(names, grads, swapped)]

    def halves(self, names, sums, landed):
        return [chip_add(sums[k], landed[k], self.chip, self.core, "rs_chip_add_" + k) for k in names]


def _cols_to_full(s):
    j, k, n = s.shape
    return jnp.transpose(s, (1, 0, 2)).reshape(k, j * n)


def _full_to_cols(g):
    k, n4 = g.shape
    return jnp.transpose(g.reshape(k, N_CHIPS, n4 // N_CHIPS), (1, 0, 2))


def _pack_small(vals):
    flat = jnp.concatenate([v.astype(F32).reshape(-1) for v in vals])
    align = SUBLANES * LANES
    flat = jnp.pad(flat, (0, -flat.shape[0] % align))
    return flat.reshape(-1, LANES)


def _unpack_small(rows, shapes):
    flat = rows.reshape(-1)
    out, off = [], 0
    for shp in shapes:
        n = int(np.prod(shp))
        out.append(flat[off:off + n].reshape(shp))
        off += n
    return out


INPUTS = (["x", "p", "positions"] + WEIGHTS + ["loss_target"] + ["m_" + k for k in WEIGHTS]
          + ["v_" + k for k in WEIGHTS])


def kernel(
        x, p, positions, l0_attn_norm, l0_w_in, l0_gdn_conv, l0_gdn_A_log, l0_gdn_dt_bias, l0_gdn_norm, l0_mla_q_norm,
        l0_mla_w_uq, l0_mla_kv_norm, l0_mla_w_ukv, l0_w_out, l0_ffn_norm, l0_ffn_w_up, l0_ffn_conv_w, l0_ffn_conv_b,
        l0_ffn_w_down, l0_ple_proj, l0_ple_gate_norm, l0_ple_gate, l1_attn_norm, l1_w_in, l1_ret_norm, l1_w_out,
        l1_ffn_norm, l1_ffn_w_up, l1_ffn_conv_w, l1_ffn_conv_b, l1_ffn_w_down, l1_ple_proj, l1_ple_gate_norm,
        l1_ple_gate, final_norm, loss_target, m_l0_attn_norm, m_l0_w_in, m_l0_gdn_conv, m_l0_gdn_A_log,
        m_l0_gdn_dt_bias, m_l0_gdn_norm, m_l0_mla_q_norm, m_l0_mla_w_uq, m_l0_mla_kv_norm, m_l0_mla_w_ukv, m_l0_w_out,
        m_l0_ffn_norm, m_l0_ffn_w_up, m_l0_ffn_conv_w, m_l0_ffn_conv_b, m_l0_ffn_w_down, m_l0_ple_proj,
        m_l0_ple_gate_norm, m_l0_ple_gate, m_l1_attn_norm, m_l1_w_in, m_l1_ret_norm, m_l1_w_out, m_l1_ffn_norm,
        m_l1_ffn_w_up, m_l1_ffn_conv_w, m_l1_ffn_conv_b, m_l1_ffn_w_down, m_l1_ple_proj, m_l1_ple_gate_norm,
        m_l1_ple_gate, m_final_norm, v_l0_attn_norm, v_l0_w_in, v_l0_gdn_conv, v_l0_gdn_A_log, v_l0_gdn_dt_bias,
        v_l0_gdn_norm, v_l0_mla_q_norm, v_l0_mla_w_uq, v_l0_mla_kv_norm, v_l0_mla_w_ukv, v_l0_w_out, v_l0_ffn_norm,
        v_l0_ffn_w_up, v_l0_ffn_conv_w, v_l0_ffn_conv_b, v_l0_ffn_w_down, v_l0_ple_proj, v_l0_ple_gate_norm,
        v_l0_ple_gate, v_l1_attn_norm, v_l1_w_in, v_l1_ret_norm, v_l1_w_out, v_l1_ffn_norm, v_l1_ffn_w_up,
        v_l1_ffn_conv_w, v_l1_ffn_conv_b, v_l1_ffn_w_down, v_l1_ple_proj, v_l1_ple_gate_norm, v_l1_ple_gate,
        v_final_norm):
    given = locals()
    a = {k: given[k] for k in INPUTS}
    x_i, y_i, c_i = _place()
    chip = 2 * x_i + y_i
    shard_shapes = {k: a[k].shape for k in WEIGHTS}

    slots = {k: cast_to_slot(a[k], chip, "cast_" + k) for k in BIG}
    W = {}
    placed = []
    for k in SMALL_SHARDED:
        r, c = shard_shapes[k]
        mine = jnp.where(c_i == 0, a[k], jnp.zeros_like(a[k]))
        placed.append(lax.dynamic_update_slice(jnp.zeros((r, N_CHIPS * c), F32), mine, (0, chip * c)))
    full_small = _unpack_small(all_reduce_small(_pack_small(placed), "gather_small_weights"),
                               [p_.shape for p_ in placed])
    for k in SMALL:
        W[k] = a[k]
    W.update(dict(zip(SMALL_SHARDED, full_small)))

    loss_part, grad_x, grads, G = local_step(a["x"][0], a["p"][:, 0], a["positions"][0], a["loss_target"][0], slots, W,
                                             Exchange(chip, c_i))
    loss = lax.psum(loss_part, ("x", "y", "c"))
    deltas, new_m, new_v = {}, {}, {}
    for k in BIG:
        deltas[k], new_m[k], new_v[k], grads[k] = adamw(a[k], grads[k], a["m_" + k], a["v_" + k], "adamw_" + k,
                                                        pass_grad=True)

    small_full = [G[k].reshape(-1) for k in SMALL]
    summed = _unpack_small(all_reduce_small(_pack_small(small_full), "reduce_small_grads"),
                           [G[k].shape for k in SMALL])
    for k, g in zip(SMALL, summed):
        if k in SMALL_SHARDED:
            r, c = shard_shapes[k]
            g = lax.dynamic_slice(g.reshape(r, N_CHIPS * c), (0, chip * c), (r, c))
        grads[k] = g.reshape(shard_shapes[k])
    packed = [_pack_small([d[k] for k in SMALL]) for d in (
        {k: a[k] for k in SMALL}, grads, {k: a["m_" + k] for k in SMALL}, {k: a["v_" + k] for k in SMALL})]
    outs = adamw(*packed, "adamw_small")
    shapes = [shard_shapes[k] for k in SMALL]
    for d, rows in zip((deltas, new_m, new_v), outs):
        d.update(dict(zip(SMALL, _unpack_small(rows, shapes))))

    return (loss, grad_x[None], *[grads[k] for k in WEIGHTS], *[deltas[k] for k in WEIGHTS],
            *[new_m[k] for k in WEIGHTS], *[new_v[k] for k in WEIGHTS])
```

```python
import functools
import math

import numpy as np
import jax
import jax.numpy as jnp
from jax import lax
from jax.experimental import pallas as pl
from jax.experimental.pallas import tpu as pltpu

F32, BF16 = jnp.float32, jnp.bfloat16
HI = lax.Precision.HIGHEST
MESH = pl.DeviceIdType.MESH

NORM_EPS = 1e-6
ROPE_THETA = 10000.0
D_MODEL = 2048
PLE_DIM = 256
GDN_HEADS, GDN_DK, GDN_DV, GDN_CONV = 8, 128, 128, 4
MLA_HEADS, MLA_Q_RANK, MLA_KV_RANK, MLA_NOPE, MLA_ROPE, MLA_V = 8, 512, 512, 128, 64, 128
RET_HEADS, RET_DK, RET_DV = 8, 256, 512
D_FF, FFN_CONV = 5632, 3
ADAM_LR, ADAM_B1, ADAM_B2, ADAM_EPS, ADAM_WD, ADAM_STEP = 0.001, 0.9, 0.999, 1e-08, 0.01, 10

LANES = 128
SUBLANES = 8
CHUNK = 128
N_CHIPS = 4
VMEM_LIMIT_MB = 56

ZIN_QKV, ZIN_Z, ZIN_CQ, ZIN_CKV, ZIN_KR, ZIN_AB, ZIN_W = 0, 3072, 4096, 4608, 5120, 5248, 5376


class Hosted:
    def __init__(self, inputs, out_shapes, aliases, n_sems, start, finish):
        self.inputs, self.out_shapes, self.aliases, self.n_sems = list(inputs), list(out_shapes), dict(aliases), n_sems
        self.start, self.finish = start, finish


def _pcall(body, *, name, out_shape, grid=(), in_specs=None, out_specs=None, scratch_shapes=(), dims=None,
           hosted=None):
    params = dict(vmem_limit_bytes=VMEM_LIMIT_MB << 20)
    if dims is not None:
        params["dimension_semantics"] = dims
    if hosted is None:
        return pl.pallas_call(body, name=name, out_shape=out_shape, grid=grid, in_specs=in_specs, out_specs=out_specs,
                              scratch_shapes=list(scratch_shapes), compiler_params=pltpu.CompilerParams(**params))
    single = not isinstance(out_shape, (list, tuple))
    out_shape = [out_shape] if single else list(out_shape)
    out_specs = [out_specs] if single else list(out_specs)
    n_in, n_out, n_scr = len(in_specs), len(out_shape), len(scratch_shapes)
    h_in, h_out = len(hosted.inputs), len(hosted.out_shapes)
    hbm = pl.BlockSpec(memory_space=pltpu.HBM)

    def hosting_body(*refs):
        ins, h_ins = refs[:n_in], refs[n_in:n_in + h_in]
        o0 = n_in + h_in
        outs, h_outs = refs[o0:o0 + n_out], refs[o0 + n_out:o0 + n_out + h_out]
        s0 = o0 + n_out + h_out
        scr, (send_sems, recv_sems) = refs[s0:s0 + n_scr], refs[s0 + n_scr:]
        ids = [pl.program_id(d) for d in range(len(grid))]
        first = functools.reduce(lambda u, v: u & v, [i == 0 for i in ids])
        last = functools.reduce(lambda u, v: u & v, [i == g - 1 for i, g in zip(ids, grid)])

        @pl.when(first)
        def _():
            hosted.start(h_ins, h_outs, send_sems, recv_sems)

        body(*ins, *outs, *scr)

        @pl.when(last)
        def _():
            hosted.finish(h_ins, h_outs, send_sems, recv_sems)

    params["dimension_semantics"] = ("arbitrary",) * len(grid)
    call = pl.pallas_call(
        hosting_body, name=name, out_shape=out_shape + hosted.out_shapes, grid=grid,
        in_specs=list(in_specs) + [hbm] * h_in, out_specs=out_specs + [hbm] * h_out,
        scratch_shapes=list(scratch_shapes) + [pltpu.SemaphoreType.DMA((hosted.n_sems,)),
                                               pltpu.SemaphoreType.DMA((hosted.n_sems,))],
        input_output_aliases={n_in + i: n_out + o for i, o in hosted.aliases.items()},
        compiler_params=pltpu.CompilerParams(**params))

    def run(*args):
        res = call(*args, *hosted.inputs)
        main = res[:n_out]
        return (main[0] if single else main), list(res[n_out:])

    return run


def _tile(n, target, mult=LANES):
    best = None
    for t in range(mult, min(n, target) + 1, mult):
        if n % t == 0:
            best = t
    return best or n


_DN = {"nn": (((1,), (0,)), ((), ())), "nt": (((1,), (1,)), ((), ())), "tn": (((0,), (0,)), ((), ()))}


def matmul(a, b, mode, out_dtype, *, name, add=None, b_shards=1, out_shards=1, tm=512, tn=1024, tk=2048,
           hosted=None, norm_bwd=None, b_shard0=0):
    bs = b.shape[-2:]
    if mode == "nn":
        (M, K), (K2, N) = a.shape, (bs[0], bs[1] * b_shards)
    elif mode == "nt":
        (M, K), (N, K2) = a.shape, (bs[0], bs[1] * b_shards)
    else:
        (K, M), (K2, N) = a.shape, bs
    assert K == K2, (name, a.shape, b.shape)
    n_sh = N // max(b_shards if mode == "nn" else 1, out_shards)
    k_sh = K // (b_shards if mode == "nt" else 1)
    tm, tn, tk = _tile(M, tm), _tile(n_sh, tn), _tile(k_sh, tk)
    nk = K // tk
    nbn, nbk = n_sh // tn, k_sh // tk
    dn = _DN[mode]
    has_add = add is not None
    a_bytes, b_bytes = a.size * a.dtype.itemsize, b.size * b.dtype.itemsize
    i_outer = nk > 1 or a_bytes + (M // tm) * b_bytes <= b_bytes + (N // tn) * a_bytes

    def ij(g0, g1):
        return (g0, g1) if i_outer else (g1, g0)

    fused_norm = norm_bwd is not None
    if fused_norm:
        assert tn == N and out_shards == 1 and not has_add and hosted is None, name
        i_outer = True

    def body(*refs):
        a_ref, b_ref = refs[:2]
        add_ref = refs[2] if has_add else None
        o_ref = refs[3 if has_add else 2]
        part = lax.dot_general(a_ref[...].astype(BF16), b_ref[...].astype(BF16), dn, preferred_element_type=F32)

        def finish(r):
            if fused_norm:
                h_ref, g_ref, dh_ref, o32_ref, o16_ref, dg_ref = refs[2:8]
                _, vjp = jax.vjp(_rms, h_ref[...], g_ref[...])
                dx, dg = vjp(r)
                out = dx + dh_ref[...]
                o32_ref[...] = out
                o16_ref[...] = out.astype(BF16)
                dg_ref[...] += dg
                return
            if has_add:
                r = r + add_ref[...]
            o_ref[...] = r.astype(out_dtype)

        if fused_norm:
            @pl.when((pl.program_id(0) == 0) & (pl.program_id(2) == 0))
            def _():
                refs[7][...] = jnp.zeros_like(refs[7])

        if nk == 1:
            finish(part)
            return
        acc = refs[-1]
        k = pl.program_id(2)

        @pl.when(k == 0)
        def _():
            acc[...] = part

        @pl.when(k > 0)
        def _():
            acc[...] += part

        @pl.when(k == nk - 1)
        def _():
            finish(acc[...])

    def spec(block, fn):
        return pl.BlockSpec(block, lambda g0, g1, k: fn(*ij(g0, g1), k))

    if mode == "tn":
        a_spec = spec((tk, tm), lambda i, j, k: (k, i))
    else:
        a_spec = spec((tm, tk), lambda i, j, k: (i, k))
    if mode == "nt":
        if b_shards > 1:
            b_spec = spec((None, tn, tk), lambda i, j, k: (b_shard0 + k // nbk, j, k % nbk))
        else:
            b_spec = spec((tn, tk), lambda i, j, k: (j, k))
    elif b_shards > 1:
        b_spec = spec((None, tk, tn), lambda i, j, k: (j // nbn, k, j % nbn))
    else:
        b_spec = spec((tk, tn), lambda i, j, k: (k, j))
    in_specs = [a_spec, b_spec]
    args = [a, b]
    if has_add:
        in_specs.append(spec((tm, tn), lambda i, j, k: (i, j)))
        args.append(add)
    if out_shards > 1:
        out_spec = spec((None, tm, tn), lambda i, j, k: (j // nbn, i, j % nbn))
        out_shape = jax.ShapeDtypeStruct((out_shards, M, n_sh), out_dtype)
    else:
        out_spec = spec((tm, tn), lambda i, j, k: (i, j))
        out_shape = jax.ShapeDtypeStruct((M, N), out_dtype)
    gi, gj = M // tm, N // tn
    if fused_norm:
        h, gain, dh = norm_bwd
        row_blk = spec((tm, N), lambda i, j, k: (i, 0))
        gain_blk = spec((1, N), lambda i, j, k: (0, 0))
        return _pcall(body, name=name, grid=(gi, 1, nk), in_specs=in_specs + [row_blk, gain_blk, row_blk],
                      out_specs=[row_blk, row_blk, gain_blk],
                      out_shape=[jax.ShapeDtypeStruct((M, N), F32), jax.ShapeDtypeStruct((M, N), BF16),
                                 jax.ShapeDtypeStruct((1, N), F32)],
                      scratch_shapes=[pltpu.VMEM((tm, tn), F32)] if nk > 1 else [],
                      dims=("arbitrary", "arbitrary", "arbitrary"))(a, b, h, gain, dh)
    return _pcall(body, name=name, out_shape=out_shape, grid=(gi, gj, nk) if i_outer else (gj, gi, nk),
                  in_specs=in_specs, out_specs=out_spec,
                  scratch_shapes=[pltpu.VMEM((tm, tn), F32)] if nk > 1 else [],
                  dims=("parallel", "parallel", "arbitrary"), hosted=hosted)(*args)


def _row_spec(tr, w, c):
    return pl.BlockSpec((tr, w), lambda i: (i, c))


def _full_spec(arr):
    return pl.BlockSpec(arr.shape, lambda i: (0,) * arr.ndim)


def rowwise(fn, rows, params, nd_rows, outs, *, name, tr=256):
    S = rows[0][0].shape[0]
    tr = min(tr, S)
    n_in = len(rows) + len(params) + len(nd_rows)

    def body(*refs):
        res = fn(*[x[...] for x in refs[:n_in]])
        for o_ref, v in zip(refs[n_in:], res):
            o_ref[...] = v.astype(o_ref.dtype)

    return _pcall(body, name=name, grid=(S // tr,),
                  in_specs=([_row_spec(tr, w, c) for (_, w, c) in rows] + [_full_spec(q) for q in params]
                            + [_row_spec(tr, w, c) for (_, w, c) in nd_rows]),
                  out_specs=[_row_spec(tr, w, 0) for (w, _) in outs],
                  out_shape=[jax.ShapeDtypeStruct((S, w), dt) for (w, dt) in outs],
                  dims=("parallel",))(*[r[0] for r in rows], *params, *[r[0] for r in nd_rows])


def rowwise_bwd(fn, rows, params, nd_rows, cts, d_dtypes, *, name, adds=None, tr=256):
    S = rows[0][0].shape[0]
    tr = min(tr, S)
    n_r, n_p, n_n, n_c = len(rows), len(params), len(nd_rows), len(cts)
    adds = adds or [None] * n_r
    add_list = [a for a in adds if a is not None]
    n_a = len(add_list)
    d_dtypes = [dt if isinstance(dt, (list, tuple)) else (dt,) for dt in d_dtypes]
    n_d = sum(len(dt) for dt in d_dtypes)

    def body(*refs):
        it = iter(refs)
        r = [next(it)[...] for _ in range(n_r)]
        p = [next(it)[...] for _ in range(n_p)]
        nd = [next(it)[...] for _ in range(n_n)]
        c = [next(it)[...] for _ in range(n_c)]
        ad = [next(it)[...] for _ in range(n_a)]
        d_row_refs = [[next(it) for _ in dts] for dts in d_dtypes]
        d_par_refs = [next(it) for _ in range(n_p)]
        outs, vjp = jax.vjp(lambda *dp: fn(*dp, *nd), *r, *p)
        g = vjp(tuple(ci.astype(o.dtype) for ci, o in zip(c, outs)))
        ai = 0
        for k in range(n_r):
            gk = g[k].astype(F32)
            if adds[k] is not None:
                gk = gk + ad[ai].astype(F32)
                ai += 1
            for ref in d_row_refs[k]:
                ref[...] = gk.astype(ref.dtype)

        @pl.when(pl.program_id(0) == 0)
        def _():
            for ref in d_par_refs:
                ref[...] = jnp.zeros_like(ref)

        for k in range(n_p):
            d_par_refs[k][...] += g[n_r + k].astype(F32)

    in_specs = ([_row_spec(tr, w, c) for (_, w, c) in rows] + [_full_spec(q) for q in params]
                + [_row_spec(tr, w, c) for (_, w, c) in nd_rows] + [_row_spec(tr, w, c) for (_, w, c) in cts]
                + [_row_spec(tr, w, c) for (_, w, c) in add_list])
    out_specs = ([_row_spec(tr, w, 0) for (_, w, _), dts in zip(rows, d_dtypes) for _ in dts]
                 + [_full_spec(q) for q in params])
    out_shape = ([jax.ShapeDtypeStruct((S, w), dt) for (_, w, _), dts in zip(rows, d_dtypes) for dt in dts]
                 + [jax.ShapeDtypeStruct(q.shape, F32) for q in params])
    res = _pcall(body, name=name, grid=(S // tr,), in_specs=in_specs, out_specs=out_specs, out_shape=out_shape,
                 dims=("arbitrary",))(*[r[0] for r in rows], *params, *[r[0] for r in nd_rows],
                                      *[r[0] for r in cts], *[r[0] for r in add_list])
    d_rows, i = [], 0
    for dts in d_dtypes:
        d_rows.append(res[i] if len(dts) == 1 else tuple(res[i:i + len(dts)]))
        i += len(dts)
    return d_rows, res[n_d:]


def _rms(x, g):
    x = x.astype(F32)
    return x * lax.rsqrt(jnp.mean(x * x, axis=-1, keepdims=True) + NORM_EPS) * g


def _fn_rms(x, g):
    return (_rms(x, g),)


def _sigmoid(x):
    return 1.0 / (1.0 + jnp.exp(-x))


def _silu(x):
    return x * _sigmoid(x)


def _softplus(x):
    return jnp.maximum(x, 0.0) + jnp.log(1.0 + jnp.exp(-jnp.abs(x)))


def _fn_ple(h, pp, gl):
    return (h.astype(F32) + pp.astype(F32) * _sigmoid(gl.astype(F32)),)


def _fn_ple_terms(pp, gl):
    return (pp.astype(F32) * _sigmoid(gl.astype(F32)),)


def _rot_half_matrix():
    half = MLA_ROPE // 2
    r = lax.broadcasted_iota(jnp.int32, (LANES, LANES), 0)
    c = lax.broadcasted_iota(jnp.int32, (LANES, LANES), 1)
    plus = (c == r + half) & (r < half)
    minus = (r == c + half) & (c < half)
    return jnp.where(plus, 1.0, 0.0) - jnp.where(minus, 1.0, 0.0)


def _rope_pad(x, cosp, sinp):
    return x * cosp + jnp.dot(x, _rot_half_matrix(), precision=HI, preferred_element_type=F32) * sinp


def _fn_mla_pre(cq, ckv, kr, qn_w, kvn_w, cosp, sinp):
    return (_rms(cq, qn_w), _rms(ckv, kvn_w), _rope_pad(kr.astype(F32), cosp, sinp))


def _fn_rope_q(q, cosp, sinp):
    q = q.astype(F32)
    parts = []
    for h in range(MLA_HEADS):
        base = 2 * LANES * h
        parts.append(q[:, base:base + LANES])
        parts.append(_rope_pad(q[:, base + LANES:base + 2 * LANES], cosp, sinp))
    return (jnp.concatenate(parts, axis=1),)


def _fn_ret_gate(g, on, w):
    return (_silu(g.astype(F32)) * (on.astype(F32) * w),)


def _shift_down(cur, halo, s):
    if s == 0:
        return cur
    r = pltpu.roll(cur, s, 0)
    hs = pltpu.roll(halo, s, 0)
    row = lax.broadcasted_iota(jnp.int32, hs.shape, 0)
    first = jnp.where(row < s, hs, r[:SUBLANES])
    return jnp.concatenate([first, r[SUBLANES:]], axis=0)


def _shift_up(cur, halo, s):
    if s == 0:
        return cur
    n = cur.shape[0]
    r = pltpu.roll(cur, n - s, 0)
    hs = pltpu.roll(halo, SUBLANES - s, 0)
    row = lax.broadcasted_iota(jnp.int32, hs.shape, 0)
    last = jnp.where(row >= SUBLANES - s, hs, r[n - SUBLANES:])
    return jnp.concatenate([r[:n - SUBLANES], last], axis=0)


def _prev_halo_spec(tr, tw, col):
    return pl.BlockSpec((SUBLANES, tw), lambda c, i: (jnp.maximum(i * (tr // SUBLANES) - 1, 0), col(c)))


def _conv_taps(cur, halo, w_ref, width):
    taps = [_shift_down(cur, halo, width - 1 - j) for j in range(width)]
    y = taps[0] * w_ref[0:1, :]
    for j in range(1, width):
        y = y + taps[j] * w_ref[j:j + 1, :]
    return y, taps


def gdn_conv_fwd(zin, w, *, tr=512, tw=512):
    S = zin.shape[0]
    tr = min(tr, S)
    width, C = w.shape

    def body(cur_ref, halo_ref, w_ref, o_ref):
        i = pl.program_id(1)
        halo = halo_ref[...] * (i > 0).astype(F32)
        y, _ = _conv_taps(cur_ref[...], halo, w_ref, width)
        o_ref[...] = _silu(y)

    return _pcall(body, name="gdn_conv_fwd", grid=(C // tw, S // tr),
                  in_specs=[pl.BlockSpec((tr, tw), lambda c, i: (i, c)), _prev_halo_spec(tr, tw, lambda c: c),
                            pl.BlockSpec((width, tw), lambda c, i: (0, c))],
                  out_specs=pl.BlockSpec((tr, tw), lambda c, i: (i, c)),
                  out_shape=jax.ShapeDtypeStruct((S, C), F32), dims=("parallel", "arbitrary"))(zin, zin, w)


def gdn_conv_bwd(zin, w, dy, *, tr=512, tw=512):
    S = zin.shape[0]
    tr = min(tr, S)
    width, C = w.shape
    n_i = S // tr

    def body(cur_ref, halo_ref, w_ref, dy_ref, dx_ref, dw_ref, carry):
        i = pl.program_id(1)
        halo = halo_ref[...] * (i < n_i - 1).astype(F32)
        y, taps = _conv_taps(cur_ref[...], halo, w_ref, width)
        sg = _sigmoid(y)
        da = dy_ref[...] * (sg * (1.0 + y * (1.0 - sg)))

        @pl.when(i == 0)
        def _():
            dw_ref[...] = jnp.zeros_like(dw_ref)
            carry[...] = jnp.zeros_like(carry)

        nxt = carry[...]
        dx = da * w_ref[width - 1:width, :]
        for s in range(1, width):
            dx = dx + _shift_up(da, nxt, s) * w_ref[width - 1 - s:width - s, :]
        dx_ref[...] = dx.astype(dx_ref.dtype)
        carry[...] = da[:SUBLANES]
        for j in range(width):
            dw_ref[j:j + 1, :] += jnp.sum(da * taps[j], axis=0, keepdims=True)

    rows = pl.BlockSpec((tr, tw), lambda c, i: (n_i - 1 - i, c))
    prev = pl.BlockSpec((SUBLANES, tw), lambda c, i: (jnp.maximum((n_i - 1 - i) * (tr // SUBLANES) - 1, 0), c))
    return _pcall(body, name="gdn_conv_bwd", grid=(C // tw, n_i),
                  in_specs=[rows, prev, pl.BlockSpec((width, tw), lambda c, i: (0, c)), rows],
                  out_specs=[rows, pl.BlockSpec((width, tw), lambda c, i: (0, c))],
                  out_shape=[jax.ShapeDtypeStruct((S, C), BF16), jax.ShapeDtypeStruct((width, C), F32)],
                  scratch_shapes=[pltpu.VMEM((SUBLANES, tw), F32)],
                  dims=("arbitrary", "arbitrary"))(zin, zin, w, dy)


def ffn_conv_fwd(u, w, b, *, tr=1024, tw=256):
    S, C2 = u.shape
    tr = min(tr, S)
    width = w.shape[0]
    half = C2 // 2
    nc = half // tw

    def body(g_ref, gh_ref, u_ref, uh_ref, wg_ref, wu_ref, bg_ref, bu_ref, o_ref):
        i = pl.program_id(1)
        live = (i > 0).astype(F32)
        yg, _ = _conv_taps(g_ref[...], gh_ref[...] * live, wg_ref, width)
        yu, _ = _conv_taps(u_ref[...], uh_ref[...] * live, wu_ref, width)
        o_ref[...] = (_silu(yg + bg_ref[...]) * (yu + bu_ref[...])).astype(o_ref.dtype)

    return _pcall(body, name="ffn_conv_fwd", grid=(nc, S // tr),
                  in_specs=[pl.BlockSpec((tr, tw), lambda c, i: (i, c)), _prev_halo_spec(tr, tw, lambda c: c),
                            pl.BlockSpec((tr, tw), lambda c, i: (i, c + nc)),
                            _prev_halo_spec(tr, tw, lambda c: c + nc),
                            pl.BlockSpec((width, tw), lambda c, i: (0, c)),
                            pl.BlockSpec((width, tw), lambda c, i: (0, c + nc)),
                            pl.BlockSpec((1, tw), lambda c, i: (0, c)), pl.BlockSpec((1, tw), lambda c, i: (0, c + nc))],
                  out_specs=pl.BlockSpec((tr, tw), lambda c, i: (i, c)),
                  out_shape=jax.ShapeDtypeStruct((S, half), BF16),
                  dims=("parallel", "arbitrary"))(u, u, u, u, w, w, b, b)


def ffn_conv_bwd(u, w, b, df, *, tr=512, tw=512):
    S, C2 = u.shape
    tr = min(tr, S)
    width = w.shape[0]
    half = C2 // 2
    nc = half // tw
    n_i = S // tr

    def body(g_ref, gh_ref, u_ref, uh_ref, wg_ref, wu_ref, bg_ref, bu_ref, df_ref, dug_ref, duu_ref, dw_ref, db_ref,
             carry):
        i = pl.program_id(1)
        live = (i < n_i - 1).astype(F32)
        yg, gt = _conv_taps(g_ref[...], gh_ref[...] * live, wg_ref, width)
        yu, ut = _conv_taps(u_ref[...], uh_ref[...] * live, wu_ref, width)
        yg = yg + bg_ref[...]
        yu = yu + bu_ref[...]
        sg = _sigmoid(yg)
        dfv = df_ref[...].astype(F32)
        dcs = (dfv * yu * (sg * (1.0 + yg * (1.0 - sg))), dfv * (yg * sg))

        @pl.when(i == 0)
        def _():
            dw_ref[...] = jnp.zeros_like(dw_ref)
            db_ref[...] = jnp.zeros_like(db_ref)
            carry[...] = jnp.zeros_like(carry)

        for t, (dc, taps, w_ref, du_ref) in enumerate(zip(dcs, (gt, ut), (wg_ref, wu_ref), (dug_ref, duu_ref))):
            halo = carry[t]
            du = dc * w_ref[width - 1:width, :]
            for s in range(1, width):
                du = du + _shift_up(dc, halo, s) * w_ref[width - 1 - s:width - s, :]
            du_ref[...] = du.astype(du_ref.dtype)
            carry[t] = dc[:SUBLANES]
            db_ref[t] += jnp.sum(dc, axis=0, keepdims=True)
            for j in range(width):
                dw_ref[t, j:j + 1, :] += jnp.sum(dc * taps[j], axis=0, keepdims=True)

    def prev(col):
        return pl.BlockSpec((SUBLANES, tw),
                            lambda c, i: (jnp.maximum((n_i - 1 - i) * (tr // SUBLANES) - 1, 0), col(c)))

    rows = lambda col: pl.BlockSpec((tr, tw), lambda c, i: (n_i - 1 - i, col(c)))
    du_g, du_u, dw, db = _pcall(
        body, name="ffn_conv_bwd", grid=(nc, n_i),
        in_specs=[rows(lambda c: c), prev(lambda c: c), rows(lambda c: c + nc), prev(lambda c: c + nc),
                  pl.BlockSpec((width, tw), lambda c, i: (0, c)), pl.BlockSpec((width, tw), lambda c, i: (0, c + nc)),
                  pl.BlockSpec((1, tw), lambda c, i: (0, c)), pl.BlockSpec((1, tw), lambda c, i: (0, c + nc)),
                  rows(lambda c: c)],
        out_specs=[rows(lambda c: c), rows(lambda c: c), pl.BlockSpec((2, width, tw), lambda c, i: (0, 0, c)),
                   pl.BlockSpec((2, 1, tw), lambda c, i: (0, 0, c))],
        out_shape=[jax.ShapeDtypeStruct((S, half), BF16), jax.ShapeDtypeStruct((S, half), BF16),
                   jax.ShapeDtypeStruct((2, width, half), F32), jax.ShapeDtypeStruct((2, 1, half), F32)],
        scratch_shapes=[pltpu.VMEM((2, SUBLANES, tw), F32)],
        dims=("arbitrary", "arbitrary"))(u, u, u, u, w, w, b, b, df)
    return du_g, du_u, jnp.concatenate([dw[0], dw[1]], axis=1), jnp.concatenate([db[0], db[1]], axis=1)


_MODE_OF = {v: k for k, v in _DN.items()}


def _bf16_dot(a, b, mode):
    return lax.dot_general(a.astype(BF16), b.astype(BF16), _DN[mode], preferred_element_type=F32)


@functools.partial(jax.custom_vjp, nondiff_argnums=(2,))
def _bdot_mode(a, b, mode):
    return _bf16_dot(a, b, mode)


def _bdot_fwd(a, b, mode):
    return _bf16_dot(a, b, mode), (a, b)


def _bdot_bwd(mode, res, ct):
    a, b = res
    if mode == "nn":
        da, db = _bf16_dot(ct, b, "nt"), _bf16_dot(a, ct, "tn")
    elif mode == "nt":
        da, db = _bf16_dot(ct, b, "nn"), _bf16_dot(ct, a, "tn")
    else:
        da, db = _bf16_dot(b, ct, "nt"), _bf16_dot(a, ct, "nn")
    return da.astype(a.dtype), db.astype(b.dtype)


_bdot_mode.defvjp(_bdot_fwd, _bdot_bwd)


def _bdot(a, b, dn=_DN["nn"]):
    return _bdot_mode(a, b, _MODE_OF[dn])


def _hi_lo(x):
    hi = x.astype(BF16)
    return hi, (x - hi.astype(F32)).astype(BF16)


def _dot3_raw(a, b, mode):
    a1, a2 = _hi_lo(a)
    b1, b2 = _hi_lo(b)
    dot = lambda p, q: lax.dot_general(p, q, _DN[mode], preferred_element_type=F32)
    return dot(a1, b1) + (dot(a1, b2) + dot(a2, b1))


@functools.partial(jax.custom_vjp, nondiff_argnums=(2,))
def _dot3(a, b, mode="nn"):
    return _dot3_raw(a, b, mode)


def _dot3_fwd(a, b, mode):
    return _dot3_raw(a, b, mode), (a, b)


def _dot3_bwd(mode, res, ct):
    a, b = res
    if mode == "nn":
        return _dot3_raw(ct, b, "nt"), _dot3_raw(a, ct, "tn")
    if mode == "nt":
        return _dot3_raw(ct, b, "nn"), _dot3_raw(ct, a, "tn")
    return _dot3_raw(b, ct, "nt"), _dot3_raw(a, ct, "nn")


_dot3.defvjp(_dot3_fwd, _dot3_bwd)


@functools.partial(jax.custom_vjp, nondiff_argnums=(2,))
def _gdot(a, b, mode="nn"):
    return _bf16_dot(a, b, mode)


def _gdot_fwd(a, b, mode):
    return _bf16_dot(a, b, mode), (a.astype(BF16), b.astype(BF16))


def _ct_dot(p, q, mode, ct_first):
    ct, r = (p, q) if ct_first else (q, p)
    c1, c2 = _hi_lo(ct)
    dot = lambda c: lax.dot_general(*((c, r) if ct_first else (r, c)), _DN[mode], preferred_element_type=F32)
    return dot(c1) + dot(c2)


def _gdot_bwd(mode, res, ct):
    a, b = res
    if mode == "nn":
        return _ct_dot(ct, b, "nt", True), _ct_dot(a, ct, "tn", False)
    if mode == "nt":
        return _ct_dot(ct, b, "nn", True), _ct_dot(ct, a, "tn", True)
    return _ct_dot(b, ct, "nt", False), _ct_dot(a, ct, "nn", False)


_gdot.defvjp(_gdot_fwd, _gdot_bwd)


def _split_dot(ones, x):
    x1 = x.astype(BF16)
    r1 = x - x1.astype(F32)
    x2 = r1.astype(BF16)
    x3 = (r1 - x2.astype(F32)).astype(BF16)
    m = ones.astype(BF16)
    dot = lambda p: lax.dot_general(m, p, _DN["nn"], preferred_element_type=F32)
    return dot(x1) + dot(x2) + dot(x3)


@jax.custom_vjp
def _tri_cumsum(x, lower, upper):
    return _split_dot(lower, x)


def _tri_cumsum_fwd(x, lower, upper):
    return _split_dot(lower, x), (lower, upper)


def _tri_cumsum_bwd(res, ct):
    lower, upper = res
    return _split_dot(upper, ct), jnp.zeros_like(lower), jnp.zeros_like(upper)


_tri_cumsum.defvjp(_tri_cumsum_fwd, _tri_cumsum_bwd)


def _tri_masks(n):
    r = lax.broadcasted_iota(jnp.int32, (n, n), 0)
    c = lax.broadcasted_iota(jnp.int32, (n, n), 1)
    return r >= c, r > c


def _gdn_chunk(q, k, v, z, ab, a_row, dt_row, norm_w, state, sel_a, sel_b):
    C = q.shape[0]
    incl, strict = _tri_masks(C)
    lower = jnp.where(incl, 1.0, 0.0)
    qn = q * lax.rsqrt(jnp.sum(q * q, axis=-1, keepdims=True) + NORM_EPS) * (GDN_DK ** -0.5)
    kn = k * lax.rsqrt(jnp.sum(k * k, axis=-1, keepdims=True) + NORM_EPS)
    g = jnp.sum(-jnp.exp(a_row) * _softplus(ab + dt_row) * sel_a, axis=-1, keepdims=True)
    beta = jnp.sum(_sigmoid(ab) * sel_b, axis=-1, keepdims=True)
    gb = jnp.broadcast_to(g, (C, C))
    g_col = _tri_cumsum(gb, lower, jnp.where(strict, 0.0, 1.0))
    g_row = g_col.T
    g_last = jnp.sum(gb, axis=0, keepdims=True)
    gamma = jnp.where(incl, jnp.exp(jnp.where(incl, g_col - g_row, 0.0)), 0.0)
    e_col = jnp.exp(g_col)
    kb = kn * beta
    a_mat = jnp.where(strict, _gdot(kb, kn, "nt") * gamma, 0.0)
    x = jnp.concatenate([v * beta, kb * e_col], axis=1)
    pw = -a_mat
    steps = int(math.log2(C))
    for it in range(steps):
        x = x + _dot3(pw, x, "nn")
        if it < steps - 1:
            pw = _dot3(pw, pw, "nn")
    u, w = x[:, :GDN_DV], x[:, GDN_DV:]
    attn = _gdot(qn, kn, "nt") * gamma
    q_dec = qn * e_col
    k_dec = kn * jnp.exp(g_last - g_col)
    v_new = u - _gdot(w, state, "nn")
    o = _gdot(q_dec, state, "nn") + _gdot(attn, v_new, "nn")
    state_new = state * jnp.exp(jnp.broadcast_to(g_last, state.shape)) + _gdot(k_dec, v_new, "tn")
    y = _rms(o, norm_w) * _silu(z)
    return y, state_new


def _head_selectors(h):
    lane = lax.broadcasted_iota(jnp.int32, (1, LANES), 1)
    return jnp.where(lane == h, 1.0, 0.0), jnp.where(lane == h + GDN_HEADS, 1.0, 0.0)


GDN_HPS = 8
GDN_W = GDN_HPS * LANES


def _gdn_in_specs(rev, nc):
    def n_(n):
        return nc - 1 - n if rev else n
    G = GDN_HEADS // GDN_HPS
    blk = lambda off: pl.BlockSpec((CHUNK, GDN_W), lambda n, h: (n_(n), off + h))
    row = pl.BlockSpec((1, LANES), lambda n, h: (0, 0))
    return n_, [blk(0), blk(G), blk(2 * G), blk(ZIN_Z // GDN_W),
                pl.BlockSpec((CHUNK, LANES), lambda n, h: (n_(n), ZIN_AB // LANES)), row, row, row]


def _lanes(ref, j):
    return ref[:, j * LANES:(j + 1) * LANES]


def _hosting(call, hosted, *args):
    res = call(*args)
    return res if hosted is not None else (res, [])


def gdn_fwd(qkv, zin, a_row, dt_row, norm_w, hosted=None):
    S = qkv.shape[0]
    nc = S // CHUNK
    H = GDN_HEADS
    _, in_specs = _gdn_in_specs(False, nc)

    def body(q_ref, k_ref, v_ref, z_ref, ab_ref, a_ref, dt_ref, nw_ref, y_ref, st_ref, state):
        n, g = pl.program_id(0), pl.program_id(1)
        @pl.when((n == 0) & (g == 0))
        def _():
            state[...] = jnp.zeros_like(state)

        res = []
        for j in range(GDN_HPS):
            h = g * GDN_HPS + j
            st = state[h]
            sel_a, sel_b = _head_selectors(h)
            res.append((st,) + _gdn_chunk(_lanes(q_ref, j), _lanes(k_ref, j), _lanes(v_ref, j), _lanes(z_ref, j),
                                          ab_ref[...], a_ref[...], dt_ref[...], nw_ref[...], st, sel_a, sel_b))
        for j, (st, y, st_new) in enumerate(res):
            st_ref[j] = st
            y_ref[:, j * LANES:(j + 1) * LANES] = y.astype(y_ref.dtype)
            state[g * GDN_HPS + j] = st_new

    call = _pcall(body, name="gdn_fwd", grid=(nc, H // GDN_HPS), in_specs=in_specs,
                  out_specs=[pl.BlockSpec((CHUNK, GDN_W), lambda n, h: (n, h)),
                             pl.BlockSpec((GDN_HPS, None, GDN_DK, GDN_DV), lambda n, h: (h, n, 0, 0))],
                  out_shape=[jax.ShapeDtypeStruct((S, H * GDN_DV), BF16),
                             jax.ShapeDtypeStruct((H, nc, GDN_DK, GDN_DV), F32)],
                  scratch_shapes=[pltpu.VMEM((H, GDN_DK, GDN_DV), F32)],
                  dims=("arbitrary", "arbitrary"), hosted=hosted)
    return _hosting(call, hosted, qkv, qkv, qkv, zin, zin, a_row, dt_row, norm_w)


def gdn_bwd(qkv, zin, a_row, dt_row, norm_w, states, dy, dy_col0, hosted=None):
    S = qkv.shape[0]
    nc = S // CHUNK
    H = GDN_HEADS
    n_, in_specs = _gdn_in_specs(True, nc)
    assert dy_col0 % GDN_HPS == 0
    in_specs = in_specs + [pl.BlockSpec((GDN_HPS, None, GDN_DK, GDN_DV), lambda n, h: (h, n_(n), 0, 0)),
                           pl.BlockSpec((CHUNK, GDN_W), lambda n, h: (n_(n), dy_col0 // GDN_HPS + h))]

    def body(q_ref, k_ref, v_ref, z_ref, ab_ref, a_ref, dt_ref, nw_ref, st_ref, dy_ref,
             dq_ref, dk_ref, dv_ref, dz_ref, dab_ref, da_ref, ddt_ref, dnw_ref, dstate):
        n, g = pl.program_id(0), pl.program_id(1)

        @pl.when((n == 0) & (g == 0))
        def _():
            da_ref[...] = jnp.zeros_like(da_ref)
            ddt_ref[...] = jnp.zeros_like(ddt_ref)
            dnw_ref[...] = jnp.zeros_like(dnw_ref)
            dstate[...] = jnp.zeros_like(dstate)

        @pl.when(g == 0)
        def _():
            dab_ref[...] = jnp.zeros_like(dab_ref)

        res = []
        for j in range(GDN_HPS):
            h = g * GDN_HPS + j
            sel_a, sel_b = _head_selectors(h)
            _, vjp = jax.vjp(lambda *a, sa=sel_a, sb=sel_b: _gdn_chunk(*a, sa, sb), _lanes(q_ref, j), _lanes(k_ref, j),
                             _lanes(v_ref, j), _lanes(z_ref, j), ab_ref[...], a_ref[...], dt_ref[...], nw_ref[...],
                             st_ref[j])
            res.append(vjp((_lanes(dy_ref, j).astype(F32), dstate[h])))
        for j, (dq, dk, dv, dz, dab, da, ddt, dnw, dst) in enumerate(res):
            cols = slice(j * LANES, (j + 1) * LANES)
            dq_ref[:, cols] = dq
            dk_ref[:, cols] = dk
            dv_ref[:, cols] = dv
            dz_ref[:, cols] = dz.astype(dz_ref.dtype)
            dstate[g * GDN_HPS + j] = dst
        dab_ref[...] += sum(r[4] for r in res)
        da_ref[...] += sum(r[5] for r in res)
        ddt_ref[...] += sum(r[6] for r in res)
        dnw_ref[...] += sum(r[7] for r in res)

    blk = pl.BlockSpec((CHUNK, GDN_W), lambda n, h: (n_(n), h))
    row = pl.BlockSpec((1, LANES), lambda n, h: (0, 0))
    wide = jax.ShapeDtypeStruct((S, H * LANES), F32)
    call = _pcall(body, name="gdn_bwd", grid=(nc, H // GDN_HPS), in_specs=in_specs,
                  out_specs=[blk, blk, blk, blk, pl.BlockSpec((CHUNK, LANES), lambda n, h: (n_(n), 0)), row, row, row],
                  out_shape=[wide, wide, wide, jax.ShapeDtypeStruct((S, H * LANES), BF16),
                             jax.ShapeDtypeStruct((S, LANES), F32)] + [jax.ShapeDtypeStruct((1, LANES), F32)] * 3,
                  scratch_shapes=[pltpu.VMEM((H, GDN_DK, GDN_DV), F32)],
                  dims=("arbitrary", "arbitrary"), hosted=hosted)
    return _hosting(call, hosted, qkv, qkv, qkv, zin, zin, a_row, dt_row, norm_w, states, dy)


def _rope_full(x, cos, sin):
    x1, x2 = x[:, :RET_DK // 2], x[:, RET_DK // 2:]
    return jnp.concatenate([x1 * cos - x2 * sin, x2 * cos + x1 * sin], axis=1)


RET_CHUNK = 512


def _ret_chunk(q, k, v, cos, sin, lg, state):
    C = q.shape[0]
    incl, _ = _tri_masks(C)
    qr = _rope_full(q, cos, sin)
    kr = _rope_full(k, cos, sin) * (RET_DK ** -0.5)
    r = lax.broadcasted_iota(jnp.int32, (C, C), 0)
    c = lax.broadcasted_iota(jnp.int32, (C, C), 1)
    dist = jnp.where(incl, (r - c).astype(F32), 0.0)
    lg1 = lg[:, :1]
    decay = jnp.where(incl, jnp.exp(dist * lg1), 0.0)
    pos = lax.broadcasted_iota(jnp.int32, (C, 1), 0).astype(F32)
    xi = jnp.exp((pos + 1.0) * lg1)
    zeta = jnp.exp((C - 1.0 - pos) * lg1)
    inner = _bdot(_bdot(qr, kr, _DN["nt"]) * decay, v)
    cross = _bdot(qr * xi, state)
    state_new = state * jnp.exp(C * lg1) + _bdot(kr * zeta, v, _DN["tn"])
    o = inner + cross
    mu = jnp.mean(o, axis=-1, keepdims=True)
    var = jnp.mean(jnp.square(o - mu), axis=-1, keepdims=True)
    return (o - mu) * lax.rsqrt(var + NORM_EPS), state_new


def _ret_log_gamma():
    lg = np.log1p(-np.power(2.0, -5.0 - np.arange(RET_HEADS, dtype=np.float64))).astype(np.float32)
    return jnp.asarray(np.broadcast_to(lg[:, None, None], (RET_HEADS, 1, LANES)).copy())


def _ret_in_specs(rev, nc):
    def n_(n):
        return nc - 1 - n if rev else n
    H = RET_HEADS
    return n_, [pl.BlockSpec((RET_CHUNK, RET_DK),lambda n, h: (n_(n), h)),
                pl.BlockSpec((RET_CHUNK, RET_DK),lambda n, h: (n_(n), H + h)),
                pl.BlockSpec((RET_CHUNK, RET_DV),lambda n, h: (n_(n), 2 * H * RET_DK // RET_DV + h)),
                pl.BlockSpec((RET_CHUNK, LANES), lambda n, h: (n_(n), 0)),
                pl.BlockSpec((RET_CHUNK, LANES), lambda n, h: (n_(n), 0)),
                pl.BlockSpec((None, 1, LANES), lambda n, h: (h, 0, 0))]


def ret_fwd(zz, cos, sin, hosted=None):
    S = zz.shape[0]
    nc = S // RET_CHUNK
    H = RET_HEADS
    _, in_specs = _ret_in_specs(False, nc)

    def body(q_ref, k_ref, v_ref, cos_ref, sin_ref, lg_ref, o_ref, st_ref, state):
        n, h = pl.program_id(0), pl.program_id(1)

        @pl.when(n == 0)
        def _():
            state[h] = jnp.zeros((RET_DK, RET_DV), F32)

        st = state[h]
        st_ref[...] = st
        o, st_new = _ret_chunk(q_ref[...], k_ref[...], v_ref[...], cos_ref[...], sin_ref[...], lg_ref[...], st)
        o_ref[...] = o
        state[h] = st_new

    call = _pcall(body, name="ret_fwd", grid=(nc, H), in_specs=in_specs,
                  out_specs=[pl.BlockSpec((RET_CHUNK, RET_DV),lambda n, h: (n, h)),
                             pl.BlockSpec((None, None, RET_DK, RET_DV), lambda n, h: (h, n, 0, 0))],
                  out_shape=[jax.ShapeDtypeStruct((S, H * RET_DV), F32),
                             jax.ShapeDtypeStruct((H, nc, RET_DK, RET_DV), F32)],
                  scratch_shapes=[pltpu.VMEM((H, RET_DK, RET_DV), F32)],
                  dims=("arbitrary", "arbitrary"), hosted=hosted)
    return _hosting(call, hosted, zz, zz, zz, cos, sin, _ret_log_gamma())


def ret_bwd(zz, cos, sin, states, do):
    S = zz.shape[0]
    nc = S // RET_CHUNK
    H = RET_HEADS
    n_, in_specs = _ret_in_specs(True, nc)
    in_specs = in_specs + [pl.BlockSpec((None, None, RET_DK, RET_DV), lambda n, h: (h, n_(n), 0, 0)),
                           pl.BlockSpec((RET_CHUNK, RET_DV),lambda n, h: (n_(n), h))]

    def body(q_ref, k_ref, v_ref, cos_ref, sin_ref, lg_ref, st_ref, do_ref, dq_ref, dk_ref, dv_ref, dstate):
        n, h = pl.program_id(0), pl.program_id(1)

        @pl.when(n == 0)
        def _():
            dstate[h] = jnp.zeros((RET_DK, RET_DV), F32)

        cos, sin, lg = cos_ref[...], sin_ref[...], lg_ref[...]
        _, vjp = jax.vjp(lambda q, k, v, st: _ret_chunk(q, k, v, cos, sin, lg, st),
                         q_ref[...], k_ref[...], v_ref[...], st_ref[...])
        dq, dk, dv, dst = vjp((do_ref[...], dstate[h]))
        dq_ref[...] = dq.astype(dq_ref.dtype)
        dk_ref[...] = dk.astype(dk_ref.dtype)
        dv_ref[...] = dv.astype(dv_ref.dtype)
        dstate[h] = dst

    return _pcall(body, name="ret_bwd", grid=(nc, H), in_specs=in_specs,
                  out_specs=[pl.BlockSpec((RET_CHUNK, RET_DK),lambda n, h: (n_(n), h)),
                             pl.BlockSpec((RET_CHUNK, RET_DK),lambda n, h: (n_(n), h)),
                             pl.BlockSpec((RET_CHUNK, RET_DV),lambda n, h: (n_(n), h))],
                  out_shape=[jax.ShapeDtypeStruct((S, H * RET_DK), BF16), jax.ShapeDtypeStruct((S, H * RET_DK), BF16),
                             jax.ShapeDtypeStruct((S, H * RET_DV), BF16)],
                  scratch_shapes=[pltpu.VMEM((H, RET_DK, RET_DV), F32)],
                  dims=("arbitrary", "arbitrary"))(zz, zz, zz, cos, sin, _ret_log_gamma(), states, do)


MLA_SCALE = (MLA_NOPE + MLA_ROPE) ** -0.5
NEG = -1e30


def _mla_scores(q, kn, kpe, diagonal):
    s = (lax.dot_general(q[:, :LANES], kn, _DN["nt"], preferred_element_type=F32)
         + lax.dot_general(q[:, LANES:], kpe, _DN["nt"], preferred_element_type=F32)) * MLA_SCALE
    if diagonal:
        row = lax.broadcasted_iota(jnp.int32, s.shape, 0)
        col = lax.broadcasted_iota(jnp.int32, s.shape, 1)
        s = jnp.where(col <= row, s, NEG)
    return s


def _on_and_below_diagonal(i, j, step):
    @pl.when(j < i)
    def _():
        step(False)

    @pl.when(j == i)
    def _():
        step(True)


FLASH_T = 1024


def flash_fwd(qr, kv, kpe, *, t=FLASH_T, hosted=None):
    S = qr.shape[0]
    t = min(t, S)
    nb = S // t
    H = MLA_HEADS

    def body(q_ref, kn_ref, v_ref, kpe_ref, o_ref, lse_ref, m_s, l_s, acc):
        i, j = pl.program_id(1), pl.program_id(2)

        @pl.when(j == 0)
        def _():
            m_s[...] = jnp.full_like(m_s, NEG)
            l_s[...] = jnp.zeros_like(l_s)
            acc[...] = jnp.zeros_like(acc)

        def step(diagonal):
            s = _mla_scores(q_ref[...], kn_ref[...], kpe_ref[...], diagonal)
            m_new = jnp.maximum(m_s[...], jnp.max(s, axis=-1, keepdims=True))
            p = jnp.exp(s - m_new)
            alpha = jnp.exp(m_s[...] - m_new)
            l_s[...] = alpha * l_s[...] + jnp.sum(p, axis=-1, keepdims=True)
            acc[...] = alpha * acc[...] + _bdot(p, v_ref[...])
            m_s[...] = m_new

        _on_and_below_diagonal(i, j, step)

        @pl.when(j == nb - 1)
        def _():
            o_ref[...] = (acc[...] / l_s[...]).astype(o_ref.dtype)
            lse_ref[...] = m_s[...] + jnp.log(l_s[...])

    kmap = lambda off: (lambda h, i, j: (jnp.minimum(j, i), off + h))
    call = _pcall(body, name="mla_flash_fwd", grid=(H, nb, nb),
                  in_specs=[pl.BlockSpec((t, 2 * LANES), lambda h, i, j: (i, h)),
                            pl.BlockSpec((t, LANES), kmap(0)), pl.BlockSpec((t, LANES), kmap(H)),
                            pl.BlockSpec((t, LANES), lambda h, i, j: (jnp.minimum(j, i), 0))],
                  out_specs=[pl.BlockSpec((t, LANES), lambda h, i, j: (i, h)),
                             pl.BlockSpec((None, t, 1), lambda h, i, j: (h, i, 0))],
                  out_shape=[jax.ShapeDtypeStruct((S, H * MLA_V), BF16), jax.ShapeDtypeStruct((H, S, 1), F32)],
                  scratch_shapes=[pltpu.VMEM((t, 1), F32), pltpu.VMEM((t, 1), F32), pltpu.VMEM((t, MLA_V), F32)],
                  dims=("parallel", "parallel", "arbitrary"), hosted=hosted)
    return _hosting(call, hosted, qr, kv, kv, kpe)


def _mla_p_ds(q, kn, v, kpe, do, o, lse, diagonal):
    p = jnp.exp(_mla_scores(q, kn, kpe, diagonal) - lse)
    dof = do.astype(F32)
    delta = jnp.sum(dof * o.astype(F32), axis=-1, keepdims=True)
    dp = lax.dot_general(do.astype(BF16), v, _DN["nt"], preferred_element_type=F32)
    ds = p * (dp - delta) * MLA_SCALE
    return p, ds


def flash_bwd(qr, kv, kpe, o, lse, dy, dy_col0, *, t=FLASH_T, hosted=None):
    S = qr.shape[0]
    t = min(t, S)
    nb = S // t
    H = MLA_HEADS

    def body(q_ref, kn_ref, v_ref, kpe_ref, o_ref, lse_ref, do_ref, dq_ref, dkn_ref, dv_ref, dkpe_ref, acc, akn, av):
        h, i, j = pl.program_id(0), pl.program_id(1), pl.program_id(2)

        @pl.when((h == 0) & (i == 0) & (j == 0))
        def _():
            dkpe_ref[...] = jnp.zeros_like(dkpe_ref)

        @pl.when((i == 0) & (j == 0))
        def _():
            akn[...] = jnp.zeros_like(akn)
            av[...] = jnp.zeros_like(av)

        @pl.when(j == 0)
        def _():
            acc[...] = jnp.zeros_like(acc)

        def step(diagonal):
            q = q_ref[...]
            p, ds = _mla_p_ds(q, kn_ref[...], v_ref[...], kpe_ref[...], do_ref[...], o_ref[...], lse_ref[...],
                              diagonal)
            acc[...] += jnp.concatenate([_bdot(ds, kn_ref[...]), _bdot(ds, kpe_ref[...])], axis=1)
            av[j] += _bdot(p, do_ref[...], _DN["tn"])
            akn[j] += _bdot(ds, q[:, :LANES], _DN["tn"])
            rows = pl.ds(pl.multiple_of(j * t, t), t)
            dkpe_ref[rows, :] += _bdot(ds, q[:, LANES:], _DN["tn"])

        _on_and_below_diagonal(i, j, step)

        @pl.when(j == nb - 1)
        def _():
            dq_ref[...] = acc[...]

        @pl.when((i == nb - 1) & (j == nb - 1))
        def _():
            dkn_ref[...] = akn[...].reshape(S, LANES).astype(dkn_ref.dtype)
            dv_ref[...] = av[...].reshape(S, LANES).astype(dv_ref.dtype)

    kmap = lambda off: (lambda h, i, j: (jnp.minimum(j, i), off + h))
    head_col = pl.BlockSpec((S, LANES), lambda h, i, j: (0, h))
    call = _pcall(body, name="mla_flash_bwd", grid=(H, nb, nb),
                  in_specs=[pl.BlockSpec((t, 2 * LANES), lambda h, i, j: (i, h)),
                            pl.BlockSpec((t, LANES), kmap(0)), pl.BlockSpec((t, LANES), kmap(H)),
                            pl.BlockSpec((t, LANES), lambda h, i, j: (jnp.minimum(j, i), 0)),
                            pl.BlockSpec((t, LANES), lambda h, i, j: (i, h)),
                            pl.BlockSpec((None, t, 1), lambda h, i, j: (h, i, 0)),
                            pl.BlockSpec((t, LANES), lambda h, i, j: (i, dy_col0 + h))],
                  out_specs=[pl.BlockSpec((t, 2 * LANES), lambda h, i, j: (i, h)), head_col, head_col,
                             pl.BlockSpec((S, LANES), lambda h, i, j: (0, 0))],
                  out_shape=[jax.ShapeDtypeStruct((S, H * 2 * LANES), F32), jax.ShapeDtypeStruct((S, H * LANES), BF16),
                             jax.ShapeDtypeStruct((S, H * LANES), BF16), jax.ShapeDtypeStruct((S, LANES), F32)],
                  scratch_shapes=[pltpu.VMEM((t, 2 * LANES), F32), pltpu.VMEM((nb, t, LANES), F32),
                                  pltpu.VMEM((nb, t, LANES), F32)],
                  dims=("arbitrary", "arbitrary", "arbitrary"), hosted=hosted)
    (dq, dkn, dv, dkpe), extra = _hosting(call, hosted, qr, kv, kv, kpe, o, lse, dy)
    return (dq, jnp.concatenate([dkn, dv], axis=1), dkpe), extra


def loss_head(h, target, g, *, tr=256):
    S, D = h.shape
    tr = min(tr, S)

    def body(h_ref, t_ref, g_ref, loss_ref, dh_ref, dg_ref):
        tgt = t_ref[...]

        def f(hh, gg):
            err = jnp.square(_rms(hh, gg) - tgt)
            per_row = jnp.sum(err, axis=-1, keepdims=True) * (0.5 / D)
            return jnp.sum(per_row, axis=0, keepdims=True)

        val, vjp = jax.vjp(f, h_ref[...], g_ref[...])
        dh, dg = vjp(jnp.ones((1, 1), F32))
        dh_ref[...] = dh

        @pl.when(pl.program_id(0) == 0)
        def _():
            loss_ref[...] = jnp.zeros_like(loss_ref)
            dg_ref[...] = jnp.zeros_like(dg_ref)

        loss_ref[...] += jnp.broadcast_to(val, loss_ref.shape)
        dg_ref[...] += dg

    return _pcall(body, name="loss_head", grid=(S // tr,),
                  in_specs=[_row_spec(tr, D, 0), _row_spec(tr, D, 0), _full_spec(g)],
                  out_specs=[pl.BlockSpec((1, LANES), lambda i: (0, 0)), _row_spec(tr, D, 0), _full_spec(g)],
                  out_shape=[jax.ShapeDtypeStruct((1, LANES), F32), jax.ShapeDtypeStruct((S, D), F32),
                             jax.ShapeDtypeStruct(g.shape, F32)],
                  dims=("arbitrary",))(h, target, g)


def _rope_tables(positions, dim):
    inv_freq = ROPE_THETA ** (-jnp.arange(0, dim, 2, dtype=F32) / dim)
    ang = positions.astype(F32)[:, None] * inv_freq
    return jnp.cos(ang), jnp.sin(ang)


def _pad_cols(w, n):
    return jnp.pad(w, ((0, 0), (0, n - w.shape[1])))


def _prep_w_in0(w):
    return jnp.concatenate([w[:, :4096], w[:, 4112:5136], _pad_cols(w[:, 5136:5200], LANES),
                            _pad_cols(w[:, 4096:4112], LANES)], axis=1)


def _unprep_w_in0(g):
    return jnp.concatenate([g[:, :4096], g[:, ZIN_AB:ZIN_AB + 16], g[:, ZIN_CQ:ZIN_KR], g[:, ZIN_KR:ZIN_KR + MLA_ROPE]],
                           axis=1)


def _prep_w_uq(w):
    w = w.reshape(MLA_Q_RANK, MLA_HEADS, MLA_NOPE + MLA_ROPE)
    w = jnp.pad(w, ((0, 0), (0, 0), (0, 2 * LANES - MLA_NOPE - MLA_ROPE)))
    return w.reshape(MLA_Q_RANK, MLA_HEADS * 2 * LANES)


def _unprep_w_uq(g):
    g = g.reshape(MLA_Q_RANK, MLA_HEADS, 2 * LANES)[:, :, :MLA_NOPE + MLA_ROPE]
    return g.reshape(MLA_Q_RANK, MLA_HEADS * (MLA_NOPE + MLA_ROPE))


def _prep_w_ukv(w):
    w = w.reshape(MLA_KV_RANK, MLA_HEADS, 2, LANES)
    return jnp.transpose(w, (0, 2, 1, 3)).reshape(MLA_KV_RANK, 2 * MLA_HEADS * LANES)


def _unprep_w_ukv(g):
    g = g.reshape(MLA_KV_RANK, 2, MLA_HEADS, LANES)
    return jnp.transpose(g, (0, 2, 1, 3)).reshape(MLA_KV_RANK, 2 * MLA_HEADS * LANES)


def _row(v, n=None):
    v = v.reshape(1, -1).astype(F32)
    return v if n is None else _pad_cols(v, n)


def _ffn_fwd(h, norm_g, w_up, conv_w, conv_b, w_down, tag, hosted=None):
    (hn,) = rowwise(_fn_rms, [(h, D_MODEL, 0)], [norm_g], [], [(D_MODEL, BF16)], name=f"{tag}_ffn_norm")
    u = matmul(hn, w_up, "nn", F32, b_shards=N_CHIPS, **TILES["wide_nn"], name=f"{tag}_ffn_up", hosted=hosted)
    u, got = u if hosted is not None else (u, [])
    f = ffn_conv_fwd(u, conv_w, conv_b)
    h_out = matmul(f, w_down, "nn", F32, add=h, tm=512, tn=1024, tk=8192, name=f"{tag}_ffn_down")
    return h_out, (hn, u, f), got


def _ffn_bwd(dh, dh16, h, norm_g, w_up, conv_w, conv_b, w_down, saved, tag, make_hosted=None):
    hn, u, f = saved
    df = matmul(dh16, w_down, "nt", BF16, tm=512, tn=2816, name=f"{tag}_ffn_down_dx")
    g_down = matmul(f, dh16, "tn", BF16, **TILES["dw"], name=f"{tag}_ffn_down_dw")
    du_g, du_u, g_conv_w, g_conv_b = ffn_conv_bwd(u, conv_w, conv_b, df)
    pair = N_CHIPS // 2
    g_up = jnp.concatenate([matmul(hn, du_, "tn", BF16, out_shards=pair, tm=1024, tn=1408, tk=4096,
                                   name=f"{tag}_ffn_up_dw_{part}") for part, du_ in (("gate", du_g), ("up", du_u))])
    hosted = make_hosted(dict(ffn_w_up=g_up, ffn_w_down=g_down)) if make_hosted else None
    dhn_gate = matmul(du_g, w_up, "nt", F32, b_shards=pair, tm=512, tn=2048, tk=2816, name=f"{tag}_ffn_up_dx_gate")
    dhn = matmul(du_u, w_up, "nt", F32, b_shards=pair, b_shard0=pair, add=dhn_gate, tm=512, tn=2048, tk=2816,
                 name=f"{tag}_ffn_up_dx_up", hosted=hosted)
    dhn, got = dhn if hosted is not None else (dhn, [])
    ((dh_in, dh_in16),), (g_norm,) = rowwise_bwd(_fn_rms, [(h, D_MODEL, 0)], [norm_g], [], [(dhn, D_MODEL, 0)],
                                                 [(F32, BF16)], adds=[(dh, D_MODEL, 0)], name=f"{tag}_ffn_norm_bwd")
    return dh_in, dh_in16, dict(ffn_norm=g_norm, ffn_w_up=g_up, ffn_conv_w=g_conv_w, ffn_conv_b=g_conv_b,
                                ffn_w_down=g_down), got


TILES = {"wide_nn": dict(tm=512, tn=3072, tk=2048),
         "square": dict(tm=512, tn=2048, tk=2048),
         "dw": dict(tm=512, tn=2048, tk=4096)}


def _ple_fwd(h, p_i, w_proj, gate_g, w_gate, tag):
    (hg,) = rowwise(_fn_rms, [(h, D_MODEL, 0)], [gate_g], [], [(D_MODEL, BF16)], name=f"{tag}_ple_norm")
    gl = matmul(hg, w_gate, "nn", F32, **TILES["square"], name=f"{tag}_ple_gate")
    pp = matmul(p_i, w_proj, "nn", F32, b_shards=N_CHIPS, name=f"{tag}_ple_proj")
    (h_out,) = rowwise(_fn_ple, [(h, D_MODEL, 0), (pp, D_MODEL, 0), (gl, D_MODEL, 0)], [], [], [(D_MODEL, F32)],
                       name=f"{tag}_ple_add")
    return h_out, (hg, gl, pp)


def _ple_bwd(dh, h, p_i, w_proj, gate_g, w_gate, saved, tag):
    hg, gl, pp = saved
    (dpp, dgl), _ = rowwise_bwd(_fn_ple_terms, [(pp, D_MODEL, 0), (gl, D_MODEL, 0)], [], [], [(dh, D_MODEL, 0)],
                                [BF16, BF16], name=f"{tag}_ple_add_bwd")
    g_proj = matmul(p_i, dpp, "tn", BF16, out_shards=N_CHIPS, name=f"{tag}_ple_proj_dw")
    g_gate = matmul(hg, dgl, "tn", BF16, **TILES["dw"], name=f"{tag}_ple_gate_dw")
    dh_in, dh_in16, g_norm = matmul(dgl, w_gate, "nt", F32, tm=256, tn=2048, tk=2048, name=f"{tag}_ple_gate_dx",
                                    norm_bwd=(h, gate_g, dh))
    return dh_in, dh_in16, dict(ple_proj=g_proj, ple_gate_norm=g_norm, ple_gate=g_gate)


def local_step(x, p, positions, target, slots, W, ex=None):
    S = x.shape[0]
    G = {}
    p0, p1 = p[0].astype(BF16), p[1].astype(BF16)
    W = dict(W)

    def use(names, bufs):
        for k, b in zip(names, bufs):
            r, c = b.shape[1:]
            W[k] = b.reshape(N_CHIPS * r, c) if k in ROW_SHARDED else (b if k in KEPT_SHARDED else _cols_to_full(b))

    def by_chip(names):
        out = []
        for k in names:
            r, c = slots[k].shape[1:]
            out.append(G[k].reshape(N_CHIPS, r, c) if k in ROW_SHARDED
                       else (G[k] if k in KEPT_SHARDED else _full_to_cols(G[k])))
        return out

    first = [slots[k] for k in GATHER_FIRST]
    use(GATHER_FIRST, ex.gather(first) if ex else first)

    cm, sm = _rope_tables(positions, MLA_ROPE)
    zeros = jnp.zeros((S, LANES - MLA_ROPE), F32)
    cosp = jnp.concatenate([cm, cm, zeros], axis=1)
    sinp = jnp.concatenate([sm, sm, zeros], axis=1)
    cr, sr = _rope_tables(positions, RET_DK)

    w_in0 = _prep_w_in0(W["l0_w_in"])
    a_row = _row(W["l0_gdn_A_log"], LANES)
    dt_row = _row(W["l0_gdn_dt_bias"], LANES)
    gdn_nw = _row(W["l0_gdn_norm"])
    n = {k: _row(W[k]) for k in ("l0_attn_norm", "l0_mla_q_norm", "l0_mla_kv_norm", "l0_ffn_norm",
                                 "l0_ple_gate_norm", "l1_attn_norm", "l1_ret_norm", "l1_ffn_norm",
                                 "l1_ple_gate_norm", "final_norm", "l0_ffn_conv_b", "l1_ffn_conv_b")}

    (hn0,) = rowwise(_fn_rms, [(x, D_MODEL, 0)], [n["l0_attn_norm"]], [], [(D_MODEL, BF16)], name="l0_attn_norm")
    zin = matmul(hn0, w_in0, "nn", F32, tm=512, tn=1792, name="l0_w_in")
    qkv = gdn_conv_fwd(zin, W["l0_gdn_conv"])
    layer0 = [slots[k] for k in GATHER_L0]
    (y_a, gdn_states), got = gdn_fwd(qkv, zin, a_row, dt_row, gdn_nw, hosted=hosted_gather(layer0) if ex else None)
    use(GATHER_L0, got if ex else layer0)
    w_uq = _prep_w_uq(W["l0_mla_w_uq"])
    w_ukv = _prep_w_ukv(W["l0_mla_w_ukv"])
    mla_rows = [(zin, MLA_Q_RANK, ZIN_CQ // MLA_Q_RANK), (zin, MLA_KV_RANK, ZIN_CKV // MLA_KV_RANK),
                (zin, LANES, ZIN_KR // LANES)]
    mla_nd = [(cosp, LANES, 0), (sinp, LANES, 0)]
    cqn, ckvn, kpe = rowwise(_fn_mla_pre, mla_rows, [n["l0_mla_q_norm"], n["l0_mla_kv_norm"]], mla_nd,
                             [(MLA_Q_RANK, BF16), (MLA_KV_RANK, BF16), (LANES, BF16)], name="mla_pre")
    q_lin = matmul(cqn, w_uq, "nn", F32, name="mla_w_uq")
    kv = matmul(ckvn, w_ukv, "nn", BF16, name="mla_w_ukv")
    (qr,) = rowwise(_fn_rope_q, [(q_lin, 2048, 0)], [], mla_nd, [(2048, BF16)],
                    name="mla_rope_q")
    layer1 = [slots[k] for k in GATHER_L1]
    (y_b, lse), got = flash_fwd(qr, kv, kpe, hosted=hosted_gather(layer1) if ex else None)
    use(GATHER_L1, got if ex else layer1)
    y_ab = jnp.concatenate([y_a, y_b], axis=1)
    h1 = matmul(y_ab, W["l0_w_out"], "nn", F32, add=x, **TILES["square"], name="l0_w_out")
    ffn_late = [slots[k] for k in GATHER_FFN]
    h2, ffn0, got = _ffn_fwd(h1, n["l0_ffn_norm"], W["l0_ffn_w_up"], W["l0_ffn_conv_w"], n["l0_ffn_conv_b"],
                             W["l0_ffn_w_down"], "l0", hosted=hosted_gather(ffn_late) if ex else None)
    use(GATHER_FFN, got if ex else ffn_late)
    h3, ple0 = _ple_fwd(h2, p0, W["l0_ple_proj"], n["l0_ple_gate_norm"], W["l0_ple_gate"], "l0")

    (hn1,) = rowwise(_fn_rms, [(h3, D_MODEL, 0)], [n["l1_attn_norm"]], [], [(D_MODEL, BF16)], name="l1_attn_norm")
    late = [slots[k] for k in GATHER_L1_IN]
    zz = matmul(hn1, W["l1_w_in"], "nn", F32, b_shards=N_CHIPS, **TILES["wide_nn"], name="l1_w_in",
                hosted=hosted_gather(late) if ex else None)
    zz, got = zz if ex else (zz, late)
    use(GATHER_L1_IN, got)
    (o_ret, ret_states), _ = ret_fwd(zz, cr, sr)
    gate_rows = [(zz, 4096, 2), (o_ret, 4096, 0)]
    (yg,) = rowwise(_fn_ret_gate, gate_rows, [n["l1_ret_norm"]], [], [(4096, BF16)], name="ret_gate")
    h4 = matmul(yg, W["l1_w_out"], "nn", F32, add=h3, tm=512, tn=2048, tk=4096, name="l1_w_out")
    h5, ffn1, _ = _ffn_fwd(h4, n["l1_ffn_norm"], W["l1_ffn_w_up"], W["l1_ffn_conv_w"], n["l1_ffn_conv_b"],
                           W["l1_ffn_w_down"], "l1")
    h6, ple1 = _ple_fwd(h5, p1, W["l1_ple_proj"], n["l1_ple_gate_norm"], W["l1_ple_gate"], "l1")

    loss_vec, dh, G["final_norm"] = loss_head(h6, target, n["final_norm"])

    dh, dh16, g = _ple_bwd(dh, h5, p1, W["l1_ple_proj"], n["l1_ple_gate_norm"], W["l1_ple_gate"], ple1, "l1")
    G.update({"l1_" + k: v for k, v in g.items()})
    dh, dh16, g, _ = _ffn_bwd(dh, dh16, h4, n["l1_ffn_norm"], W["l1_ffn_w_up"], W["l1_ffn_conv_w"],
                              n["l1_ffn_conv_b"], W["l1_ffn_w_down"], ffn1, "l1")
    G.update({"l1_" + k: v for k, v in g.items()})

    dyg = matmul(dh16, W["l1_w_out"], "nt", F32, tm=512, tn=4096, name="l1_w_out_dx")
    G["l1_w_out"] = matmul(yg, dh16, "tn", BF16, **TILES["dw"], name="l1_w_out_dw")
    (dg, do_ret), (G["l1_ret_norm"],) = rowwise_bwd(_fn_ret_gate, gate_rows, [n["l1_ret_norm"]], [],
                                                   [(dyg, 4096, 0)], [BF16, F32], name="ret_gate_bwd")
    dq, dk, dv = ret_bwd(zz, cr, sr, ret_states, do_ret)
    dzz = jnp.concatenate([dq, dk, dv, dg], axis=1)
    G["l1_w_in"] = matmul(hn1, dzz, "tn", BF16, out_shards=N_CHIPS, tm=1024, tn=1536, tk=4096, name="l1_w_in_dw")
    sums, landed = {}, {}
    grads_l1 = by_chip(REDUCE_L1)
    dhn = matmul(dzz, W["l1_w_in"], "nt", F32, b_shards=N_CHIPS, tm=1024, tn=1024, tk=3072, name="l1_w_in_dx",
                 hosted=hosted_swap(grads_l1) if ex else None)
    if ex:
        dhn, swapped = dhn
        sums.update(zip(REDUCE_L1, ex.pair_sums(REDUCE_L1, grads_l1, swapped)))
    (dh,), (G["l1_attn_norm"],) = rowwise_bwd(_fn_rms, [(h3, D_MODEL, 0)], [n["l1_attn_norm"]], [],
                                             [(dhn, D_MODEL, 0)], [F32], adds=[(dh, D_MODEL, 0)],
                                             name="l1_attn_norm_bwd")

    dh, dh16, g = _ple_bwd(dh, h2, p0, W["l0_ple_proj"], n["l0_ple_gate_norm"], W["l0_ple_gate"], ple0, "l0")
    G.update({"l0_" + k: v for k, v in g.items()})
    mid = REDUCE_DQ + REDUCE_L0

    def swap_mid(g_ffn):
        G.update({"l0_" + k: v for k, v in g_ffn.items()})
        return hosted_swap(by_chip(mid))

    dh, dh16, g, swapped = _ffn_bwd(dh, dh16, h1, n["l0_ffn_norm"], W["l0_ffn_w_up"], W["l0_ffn_conv_w"],
                                    n["l0_ffn_conv_b"], W["l0_ffn_w_down"], ffn0, "l0",
                                    make_hosted=swap_mid if ex else None)
    G.update({"l0_" + k: v for k, v in g.items()})
    if ex:
        sums.update(zip(mid, ex.pair_sums(mid, by_chip(mid), swapped)))

    dy_ab = matmul(dh16, W["l0_w_out"], "nt", F32, **TILES["square"], name="l0_w_out_dx")
    G["l0_w_out"] = matmul(y_ab, dh16, "tn", BF16, **TILES["dw"], name="l0_w_out_dw")
    (dq, dk, dv, dz, dab, g_a, g_dt, G["l0_gdn_norm"]), got = gdn_bwd(
        qkv, zin, a_row, dt_row, gdn_nw, gdn_states, dy_ab, 0,
        hosted=hosted_scatter([sums[k] for k in REDUCE_L1]) if ex else None)
    landed.update(zip(REDUCE_L1, got))
    G["l0_gdn_A_log"], G["l0_gdn_dt_bias"] = g_a[:, :GDN_HEADS], g_dt[:, :GDN_HEADS]
    dqkv, G["l0_gdn_conv"] = gdn_conv_bwd(zin, W["l0_gdn_conv"], jnp.concatenate([dq, dk, dv], axis=1))
    (dqr, dkv, dkpe), got = flash_bwd(qr, kv, kpe, y_b, lse, dy_ab, MLA_HEADS,
                                      hosted=hosted_scatter([sums[k] for k in mid]) if ex else None)
    landed.update(zip(mid, got))
    (dq_lin,), _ = rowwise_bwd(_fn_rope_q, [(q_lin, 2048, 0)], [], mla_nd, [(dqr, 2048, 0)], [BF16],
                               name="mla_rope_q_bwd")
    G["l0_mla_w_uq"] = _unprep_w_uq(matmul(cqn, dq_lin, "tn", BF16, name="mla_w_uq_dw"))
    dcqn = matmul(dq_lin, w_uq, "nt", F32, name="mla_w_uq_dx")
    G["l0_mla_w_ukv"] = _unprep_w_ukv(matmul(ckvn, dkv, "tn", BF16, name="mla_w_ukv_dw"))
    dckvn = matmul(dkv, w_ukv, "nt", F32, name="mla_w_ukv_dx")
    (dcq, dckv, dkr), (G["l0_mla_q_norm"], G["l0_mla_kv_norm"]) = rowwise_bwd(
        _fn_mla_pre, mla_rows, [n["l0_mla_q_norm"], n["l0_mla_kv_norm"]], mla_nd,
        [(dcqn, MLA_Q_RANK, 0), (dckvn, MLA_KV_RANK, 0), (dkpe, LANES, 0)], [BF16, BF16, BF16], name="mla_pre_bwd")
    dzin = jnp.concatenate([dqkv, dz, dcq, dckv, dkr, dab.astype(BF16)], axis=1)
    early = REDUCE_L1 + REDUCE_DQ + REDUCE_L0
    reduced = {}
    g_in = matmul(hn0, dzin, "tn", BF16, tm=512, tn=1792, tk=4096, name="l0_w_in_dw",
                  hosted=hosted_join(ex.halves(early, sums, landed)) if ex else None)
    if ex:
        g_in, joined = g_in
        reduced.update(zip(early, joined))
    G["l0_w_in"] = _unprep_w_in0(g_in)
    if ex:
        sums.update(zip(REDUCE_LAST, ex.pair_sums(REDUCE_LAST, by_chip(REDUCE_LAST))))
    dhn = matmul(dzin, w_in0, "nt", F32, tm=512, tn=2048, tk=5376, name="l0_w_in_dx",
                 hosted=hosted_scatter([sums[k] for k in REDUCE_LAST]) if ex else None)
    if ex:
        dhn, got = dhn
        landed.update(zip(REDUCE_LAST, got))
    (grad_x,), (G["l0_attn_norm"],) = rowwise_bwd(_fn_rms, [(x, D_MODEL, 0)], [n["l0_attn_norm"]], [],
                                                 [(dhn, D_MODEL, 0)], [F32], adds=[(dh, D_MODEL, 0)],
                                                 name="l0_attn_norm_bwd")
    small = {k: G[k] for k in SMALL}
    if not ex:
        return loss_vec[0, 0], grad_x, dict(zip(BIG, by_chip(BIG))), small
    reduced.update(zip(REDUCE_LAST, pair_join_halves(ex.halves(REDUCE_LAST, sums, landed))))
    return loss_vec[0, 0], grad_x, reduced, small


HBM = pl.BlockSpec(memory_space=pltpu.HBM)
VMEM = pl.BlockSpec(memory_space=pltpu.VMEM)


def _place():
    return lax.axis_index("x"), lax.axis_index("y"), lax.axis_index("c")


def _other_chips(x, y):
    return [(1 - x, y), (x, 1 - y), (1 - x, 1 - y)]


def _comm_call(body, *, name, out_shape, in_specs, out_specs, scratch_shapes):
    return pl.pallas_call(body, name=name, out_shape=out_shape, in_specs=in_specs, out_specs=out_specs,
                          scratch_shapes=list(scratch_shapes),
                          compiler_params=pltpu.CompilerParams(vmem_limit_bytes=VMEM_LIMIT_MB << 20))


def _inplace_comm_call(body, bufs, *, name, n_sems):
    n = len(bufs)
    return pl.pallas_call(body, name=name, out_shape=[jax.ShapeDtypeStruct(b.shape, b.dtype) for b in bufs],
                          in_specs=[HBM] * n, out_specs=[HBM] * n, input_output_aliases={i: i for i in range(n)},
                          scratch_shapes=[pltpu.SemaphoreType.DMA((n_sems,)), pltpu.SemaphoreType.DMA((n_sems,))],
                          compiler_params=pltpu.CompilerParams(vmem_limit_bytes=VMEM_LIMIT_MB << 20))(*bufs)


def all_gather_chips(bufs):
    n_sems, start, finish = _gather_phase(len(bufs))
    n = len(bufs)

    def body(*refs):
        outs, send_sems, recv_sems = refs[n:2 * n], refs[2 * n], refs[2 * n + 1]
        start(None, outs, send_sems, recv_sems)
        finish(None, outs, send_sems, recv_sems)

    return _inplace_comm_call(body, bufs, name="all_gather_chips", n_sems=n_sems)


def _gather_phase(n):
    def plan(outs, send_sems, recv_sems):
        x, y, c = _place()

        def copy(w, k, chip, hc, to):
            half = outs[w].shape[1] // 2
            rows = outs[w].at[2 * chip[0] + chip[1], pl.ds(hc * half, half), :]
            return pltpu.make_async_remote_copy(src_ref=rows, dst_ref=rows, send_sem=send_sems.at[6 * w + k],
                                                recv_sem=recv_sems.at[6 * w + k], device_id=to, device_id_type=MESH)

        first = [[copy(w, k, (x, y), c, (*chip, c)) for k, chip in enumerate(_other_chips(x, y))] for w in range(n)]
        passed = [[copy(w, 3 + k, chip, c, (x, y, 1 - c)) for k, chip in enumerate(_other_chips(x, y))]
                  for w in range(n)]
        return copy, first, passed, (x, y, c)

    def start(_, outs, send_sems, recv_sems):
        _, first, _, _ = plan(outs, send_sems, recv_sems)
        for w in range(n):
            for cp in first[w]:
                cp.start()

    def finish(_, outs, send_sems, recv_sems):
        copy, first, passed, (x, y, c) = plan(outs, send_sems, recv_sems)
        chips = _other_chips(x, y)
        for w in range(n):
            for k, chip in enumerate(chips):
                copy(w, k, chip, c, (x, y, c)).wait_recv()
                passed[w][k].start()
        for w in range(n):
            for k, chip in enumerate(chips):
                copy(w, 3 + k, chip, 1 - c, (x, y, c)).wait_recv()
        for w in range(n):
            for cp in first[w] + passed[w]:
                cp.wait_send()

    return 6 * n, start, finish


def hosted_gather(bufs):
    n_sems, start, finish = _gather_phase(len(bufs))
    return Hosted(bufs, [jax.ShapeDtypeStruct(b.shape, b.dtype) for b in bufs], {i: i for i in range(len(bufs))},
                  n_sems, start, finish)


def pair_swap_halves(gs):
    n = len(gs)
    n_sems, start, finish = _swap_phase(n)

    def body(*refs):
        g_refs, o_refs, send_sems, recv_sems = refs[:n], refs[n:2 * n], refs[2 * n], refs[2 * n + 1]
        start(g_refs, o_refs, send_sems, recv_sems)
        finish(g_refs, o_refs, send_sems, recv_sems)

    return _comm_call(body, name="pair_swap_halves", out_shape=_swap_shapes(gs), in_specs=[HBM] * n, out_specs=[HBM] * n,
                      scratch_shapes=[pltpu.SemaphoreType.DMA((n_sems,)), pltpu.SemaphoreType.DMA((n_sems,))])(*gs)


def _swap_shapes(gs):
    return [jax.ShapeDtypeStruct((N_CHIPS, g.shape[1] // 2, g.shape[2]), g.dtype) for g in gs]


def _swap_phase(n):
    def copies(g_refs, o_refs, send_sems, recv_sems):
        x, y, c = _place()
        out = []
        for w in range(n):
            half = g_refs[w].shape[1] // 2
            out.append(pltpu.make_async_remote_copy(
                src_ref=g_refs[w].at[:, pl.ds((1 - c) * half, half), :], dst_ref=o_refs[w], send_sem=send_sems.at[w],
                recv_sem=recv_sems.at[w], device_id=(x, y, 1 - c), device_id_type=MESH))
        return out

    def start(*refs):
        for cp in copies(*refs):
            cp.start()

    def finish(*refs):
        for cp in copies(*refs):
            cp.wait()

    return n, start, finish


def hosted_swap(gs):
    n_sems, start, finish = _swap_phase(len(gs))
    return Hosted(gs, _swap_shapes(gs), {}, n_sems, start, finish)


def scatter_chips(ps):
    n = len(ps)
    n_sems, start, finish = _scatter_phase(n)

    def body(*refs):
        p_refs, o_refs, send_sems, recv_sems = refs[:n], refs[n:2 * n], refs[2 * n], refs[2 * n + 1]
        start(p_refs, o_refs, send_sems, recv_sems)
        finish(p_refs, o_refs, send_sems, recv_sems)

    return _comm_call(body, name="scatter_chips", out_shape=_scatter_shapes(ps), in_specs=[HBM] * n, out_specs=[HBM] * n,
                      scratch_shapes=[pltpu.SemaphoreType.DMA((n_sems,)), pltpu.SemaphoreType.DMA((n_sems,))])(*ps)


def _scatter_shapes(ps):
    return [jax.ShapeDtypeStruct((3,) + p.shape[1:], p.dtype) for p in ps]


def _scatter_phase(n):
    def copies(p_refs, o_refs, send_sems, recv_sems):
        x, y, c = _place()
        return [pltpu.make_async_remote_copy(src_ref=p_refs[w].at[2 * chip[0] + chip[1]], dst_ref=o_refs[w].at[k],
                                             send_sem=send_sems.at[3 * w + k], recv_sem=recv_sems.at[3 * w + k],
                                             device_id=(*chip, c), device_id_type=MESH)
                for w in range(n) for k, chip in enumerate(_other_chips(x, y))]

    def start(*refs):
        for cp in copies(*refs):
            cp.start()

    def finish(*refs):
        for cp in copies(*refs):
            cp.wait()

    return 3 * n, start, finish


def hosted_scatter(ps):
    n_sems, start, finish = _scatter_phase(len(ps))
    return Hosted(ps, _scatter_shapes(ps), {}, n_sems, start, finish)


def pair_join_halves(rs):
    n = len(rs)
    n_sems, start, finish = _join_phase(n)

    def body(*refs):
        outs, send_sems, recv_sems = refs[n:2 * n], refs[2 * n], refs[2 * n + 1]
        start(None, outs, send_sems, recv_sems)
        finish(None, outs, send_sems, recv_sems)

    return _inplace_comm_call(body, rs, name="pair_join_halves", n_sems=n_sems)


def _join_phase(n):
    def copies(_, outs, send_sems, recv_sems):
        x, y, c = _place()
        out = []
        for w in range(n):
            half = outs[w].shape[0] // 2
            rows = outs[w].at[pl.ds(c * half, half), :]
            out.append(pltpu.make_async_remote_copy(src_ref=rows, dst_ref=rows, send_sem=send_sems.at[w],
                                                    recv_sem=recv_sems.at[w], device_id=(x, y, 1 - c),
                                                    device_id_type=MESH))
        return out

    def start(*refs):
        for cp in copies(*refs):
            cp.start()

    def finish(*refs):
        for cp in copies(*refs):
            cp.wait()

    return n, start, finish


def hosted_join(rs):
    n_sems, start, finish = _join_phase(len(rs))
    return Hosted(rs, [jax.ShapeDtypeStruct(r.shape, r.dtype) for r in rs], {i: i for i in range(len(rs))},
                  n_sems, start, finish)


def all_reduce_small(v, name):
    n, L = v.shape
    n_dev = 8

    def body(v_ref, out_ref, buf, send_sems, recv_sems):
        x, y, c = _place()
        me = 4 * x + 2 * y + c
        buf[me] = v_ref[...]

        def copy(k, slot, peer):
            return pltpu.make_async_remote_copy(src_ref=v_ref, dst_ref=buf.at[slot], send_sem=send_sems.at[k],
                                                recv_sem=recv_sems.at[slot],
                                                device_id=(peer // 4, (peer // 2) % 2, peer % 2), device_id_type=MESH)

        sends = [copy(k - 1, me, (me + k) % n_dev) for k in range(1, n_dev)]
        for cp in sends:
            cp.start()
        for k in range(1, n_dev):
            src = (me + k) % n_dev
            copy(0, src, src).wait_recv()
        for cp in sends:
            cp.wait_send()
        acc = buf[0]
        for s in range(1, n_dev):
            acc = acc + buf[s]
        out_ref[...] = acc

    return _comm_call(body, name=name, out_shape=jax.ShapeDtypeStruct((n, L), v.dtype), in_specs=[VMEM], out_specs=VMEM,
                      scratch_shapes=[pltpu.VMEM((n_dev, n, L), v.dtype), pltpu.SemaphoreType.DMA((n_dev - 1,)),
                                      pltpu.SemaphoreType.DMA((n_dev,))])(v)


BF16_ROWS = 16
STREAM_BLOCK_BYTES = 4 << 20


def _rows_tile(n, row_bytes, budget=1 << 20, mult=SUBLANES):
    best = mult if n % mult == 0 else n
    for t in range(mult, n + 1, mult):
        if n % t == 0 and t * row_bytes <= budget:
            best = t
    return best


def _scalars(*vals):
    return jnp.stack([jnp.asarray(v, jnp.int32) for v in vals])


def cast_to_slot(w, chip, name):
    r, c = w.shape
    tb = _rows_tile(r, c * 4, budget=STREAM_BLOCK_BYTES, mult=BF16_ROWS)

    def body(s_ref, w_ref, o_ref):
        o_ref[...] = w_ref[...].astype(BF16)

    spec = pltpu.PrefetchScalarGridSpec(
        num_scalar_prefetch=1, grid=(r // tb,), in_specs=[pl.BlockSpec((tb, c), lambda i, s: (i, 0))],
        out_specs=pl.BlockSpec((None, tb, c), lambda i, s: (s[0], i, 0)))
    return pl.pallas_call(body, name=name, grid_spec=spec, out_shape=jax.ShapeDtypeStruct((N_CHIPS, r, c), BF16),
                          compiler_params=pltpu.CompilerParams(dimension_semantics=("parallel",)))(_scalars(chip), w)


def pair_add(g, got, c, name):
    _, r, w = g.shape
    half = r // 2
    tb = _rows_tile(half, w * 4, budget=STREAM_BLOCK_BYTES, mult=BF16_ROWS)
    nb = half // tb

    def body(c_ref, g_ref, got_ref, o_ref):
        o_ref[...] = (g_ref[...].astype(F32) + got_ref[...].astype(F32)).astype(o_ref.dtype)

    spec = pltpu.PrefetchScalarGridSpec(
        num_scalar_prefetch=1, grid=(N_CHIPS, nb),
        in_specs=[pl.BlockSpec((None, tb, w), lambda s, i, c_ref: (s, c_ref[0] * nb + i, 0)),
                  pl.BlockSpec((None, tb, w), lambda s, i, c_ref: (s, i, 0))],
        out_specs=pl.BlockSpec((None, tb, w), lambda s, i, c_ref: (s, i, 0)))
    return pl.pallas_call(body, name=name, grid_spec=spec, out_shape=jax.ShapeDtypeStruct((N_CHIPS, half, w), BF16),
                          compiler_params=pltpu.CompilerParams(dimension_semantics=("parallel", "parallel")))(
        _scalars(c), g, got)


def chip_add(p, got, chip, c, name):
    _, h, w = p.shape
    tb = _rows_tile(h, w * 4, budget=STREAM_BLOCK_BYTES, mult=BF16_ROWS)
    nb = h // tb

    def body(s_ref, p_ref, got_ref, o_ref):
        acc = p_ref[...].astype(F32)
        for k in range(3):
            acc = acc + got_ref[k].astype(F32)
        o_ref[...] = acc

    spec = pltpu.PrefetchScalarGridSpec(
        num_scalar_prefetch=1, grid=(nb,),
        in_specs=[pl.BlockSpec((None, tb, w), lambda i, s: (s[0], i, 0)),
                  pl.BlockSpec((3, tb, w), lambda i, s: (0, i, 0))],
        out_specs=pl.BlockSpec((tb, w), lambda i, s: (s[1] * nb + i, 0)))
    return pl.pallas_call(body, name=name, grid_spec=spec, out_shape=jax.ShapeDtypeStruct((2 * h, w), F32),
                          compiler_params=pltpu.CompilerParams(dimension_semantics=("parallel",)))(
        _scalars(chip, c), p, got)


def adamw(w, g, m, v, name, pass_grad=False):
    r, c = w.shape
    tr = _rows_tile(r, c * 4, budget=STREAM_BLOCK_BYTES // 2)
    n_out = 4 if pass_grad else 3

    def body(w_ref, g_ref, m_ref, v_ref, d_ref, m_out, v_out, *g_out):
        gg = g_ref[...]
        if pass_grad:
            g_out[0][...] = gg
        m2 = ADAM_B1 * m_ref[...] + (1.0 - ADAM_B1) * gg
        v2 = ADAM_B2 * v_ref[...] + (1.0 - ADAM_B2) * jnp.square(gg)
        m_hat = m2 / (1.0 - ADAM_B1 ** ADAM_STEP)
        v_hat = v2 / (1.0 - ADAM_B2 ** ADAM_STEP)
        d_ref[...] = -ADAM_LR * (m_hat / (jnp.sqrt(v_hat) + ADAM_EPS) + ADAM_WD * w_ref[...])
        m_out[...] = m2
        v_out[...] = v2

    blk = pl.BlockSpec((tr, c), lambda i: (i, 0))
    return _pcall(body, name=name, grid=(r // tr,), in_specs=[blk] * 4, out_specs=[blk] * n_out,
                  out_shape=[jax.ShapeDtypeStruct((r, c), F32)] * n_out, dims=("parallel",))(w, g, m, v)


WEIGHTS = ["l0_attn_norm", "l0_w_in", "l0_gdn_conv", "l0_gdn_A_log", "l0_gdn_dt_bias", "l0_gdn_norm", "l0_mla_q_norm",
           "l0_mla_w_uq", "l0_mla_kv_norm", "l0_mla_w_ukv", "l0_w_out", "l0_ffn_norm", "l0_ffn_w_up", "l0_ffn_conv_w",
           "l0_ffn_conv_b", "l0_ffn_w_down", "l0_ple_proj", "l0_ple_gate_norm", "l0_ple_gate", "l1_attn_norm",
           "l1_w_in", "l1_ret_norm", "l1_w_out", "l1_ffn_norm", "l1_ffn_w_up", "l1_ffn_conv_w", "l1_ffn_conv_b",
           "l1_ffn_w_down", "l1_ple_proj", "l1_ple_gate_norm", "l1_ple_gate", "final_norm"]
COL_SHARDED = ["l0_w_in", "l0_mla_w_uq", "l0_mla_w_ukv", "l0_ffn_w_up", "l0_ple_proj", "l1_w_in", "l1_ffn_w_up",
               "l1_ple_proj"]
ROW_SHARDED = ["l0_w_out", "l0_ffn_w_down", "l0_ple_gate", "l1_w_out", "l1_ffn_w_down", "l1_ple_gate"]
BIG = [k for k in WEIGHTS if k in COL_SHARDED or k in ROW_SHARDED]
SMALL_SHARDED = ["l0_gdn_conv", "l0_ffn_conv_w", "l1_ffn_conv_w"]
SMALL = [k for k in WEIGHTS if k not in BIG]
KEPT_SHARDED = ["l0_ffn_w_up", "l0_ple_proj", "l1_w_in", "l1_ffn_w_up", "l1_ple_proj"]
GATHER_FIRST = ["l0_w_in"]
GATHER_L0 = ["l0_mla_w_uq", "l0_mla_w_ukv", "l0_w_out", "l0_ffn_w_up", "l0_ffn_w_down", "l0_ple_proj", "l0_ple_gate",
             "l1_w_out"]
GATHER_L1 = ["l1_w_in"]
GATHER_FFN = ["l1_ffn_w_down", "l1_ple_proj", "l1_ple_gate"]
GATHER_L1_IN = ["l1_ffn_w_up"]
REDUCE_L1 = [k for k in BIG if k.startswith("l1_")]
REDUCE_DQ = ["l0_ffn_w_up"]
REDUCE_L0 = ["l0_ffn_w_down", "l0_ple_proj", "l0_ple_gate"]
REDUCE_LAST = ["l0_w_in", "l0_mla_w_uq", "l0_mla_w_ukv", "l0_w_out"]


class Exchange:
    def __init__(self, chip, core):
        self.chip, self.core = chip, core

    def gather(self, bufs):
        return all_gather_chips(bufs)

    def pair_sums(self, names, grads, swapped=None):
        swapped = pair_swap_halves(grads) if swapped is None else swapped
        return [pair_add(g, got, self.core, "rs_pair_add_" + k) for k, g, got in zip(names, grads, swapped)]

    def halves(self, names, sums, landed):
        return [chip_add(sums[k], landed[k], self.chip, self.core, "rs_chip_add_" + k) for k in names]


def _cols_to_full(s):
    j, k, n = s.shape
    return jnp.transpose(s, (1, 0, 2)).reshape(k, j * n)


def _full_to_cols(g):
    k, n4 = g.shape
    return jnp.transpose(g.reshape(k, N_CHIPS, n4 // N_CHIPS), (1, 0, 2))


def _pack_small(vals):
    flat = jnp.concatenate([v.astype(F32).reshape(-1) for v in vals])
    align = SUBLANES * LANES
    flat = jnp.pad(flat, (0, -flat.shape[0] % align))
    return flat.reshape(-1, LANES)


def _unpack_small(rows, shapes):
    flat = rows.reshape(-1)
    out, off = [], 0
    for shp in shapes:
        n = int(np.prod(shp))
        out.append(flat[off:off + n].reshape(shp))
        off += n
    return out


INPUTS = (["x", "p", "positions"] + WEIGHTS + ["loss_target"] + ["m_" + k for k in WEIGHTS]
          + ["v_" + k for k in WEIGHTS])


def kernel(
        x, p, positions, l0_attn_norm, l0_w_in, l0_gdn_conv, l0_gdn_A_log, l0_gdn_dt_bias, l0_gdn_norm, l0_mla_q_norm,
        l0_mla_w_uq, l0_mla_kv_norm, l0_mla_w_ukv, l0_w_out, l0_ffn_norm, l0_ffn_w_up, l0_ffn_conv_w, l0_ffn_conv_b,
        l0_ffn_w_down, l0_ple_proj, l0_ple_gate_norm, l0_ple_gate, l1_attn_norm, l1_w_in, l1_ret_norm, l1_w_out,
        l1_ffn_norm, l1_ffn_w_up, l1_ffn_conv_w, l1_ffn_conv_b, l1_ffn_w_down, l1_ple_proj, l1_ple_gate_norm,
        l1_ple_gate, final_norm, loss_target, m_l0_attn_norm, m_l0_w_in, m_l0_gdn_conv, m_l0_gdn_A_log,
        m_l0_gdn_dt_bias, m_l0_gdn_norm, m_l0_mla_q_norm, m_l0_mla_w_uq, m_l0_mla_kv_norm, m_l0_mla_w_ukv, m_l0_w_out,
        m_l0_ffn_norm, m_l0_ffn_w_up, m_l0_ffn_conv_w, m_l0_ffn_conv_b, m_l0_ffn_w_down, m_l0_ple_proj,
        m_l0_ple_gate_norm, m_l0_ple_gate, m_l1_attn_norm, m_l1_w_in, m_l1_ret_norm, m_l1_w_out, m_l1_ffn_norm,
        m_l1_ffn_w_up, m_l1_ffn_conv_w, m_l1_ffn_conv_b, m_l1_ffn_w_down, m_l1_ple_proj, m_l1_ple_gate_norm,
        m_l1_ple_gate, m_final_norm, v_l0_attn_norm, v_l0_w_in, v_l0_gdn_conv, v_l0_gdn_A_log, v_l0_gdn_dt_bias,
        v_l0_gdn_norm, v_l0_mla_q_norm, v_l0_mla_w_uq, v_l0_mla_kv_norm, v_l0_mla_w_ukv, v_l0_w_out, v_l0_ffn_norm,
        v_l0_ffn_w_up, v_l0_ffn_conv_w, v_l0_ffn_conv_b, v_l0_ffn_w_down, v_l0_ple_proj, v_l0_ple_gate_norm,
        v_l0_ple_gate, v_l1_attn_norm, v_l1_w_in, v_l1_ret_norm, v_l1_w_out, v_l1_ffn_norm, v_l1_ffn_w_up,
        v_l1_ffn_conv_w, v_l1_ffn_conv_b, v_l1_ffn_w_down, v_l1_ple_proj, v_l1_ple_gate_norm, v_l1_ple_gate,
        v_final_norm):
    given = locals()
    a = {k: given[k] for k in INPUTS}
    x_i, y_i, c_i = _place()
    chip = 2 * x_i + y_i
    shard_shapes = {k: a[k].shape for k in WEIGHTS}

    slots = {k: cast_to_slot(a[k], chip, "cast_" + k) for k in BIG}
    W = {}
    placed = []
    for k in SMALL_SHARDED:
        r, c = shard_shapes[k]
        mine = jnp.where(c_i == 0, a[k], jnp.zeros_like(a[k]))
        placed.append(lax.dynamic_update_slice(jnp.zeros((r, N_CHIPS * c), F32), mine, (0, chip * c)))
    full_small = _unpack_small(all_reduce_small(_pack_small(placed), "gather_small_weights"),
                               [p_.shape for p_ in placed])
    for k in SMALL:
        W[k] = a[k]
    W.update(dict(zip(SMALL_SHARDED, full_small)))

    loss_part, grad_x, grads, G = local_step(a["x"][0], a["p"][:, 0], a["positions"][0], a["loss_target"][0], slots, W,
                                             Exchange(chip, c_i))
    loss = lax.psum(loss_part, ("x", "y", "c"))
    deltas, new_m, new_v = {}, {}, {}
    for k in BIG:
        deltas[k], new_m[k], new_v[k], grads[k] = adamw(a[k], grads[k], a["m_" + k], a["v_" + k], "adamw_" + k,
                                                        pass_grad=True)

    small_full = [G[k].reshape(-1) for k in SMALL]
    summed = _unpack_small(all_reduce_small(_pack_small(small_full), "reduce_small_grads"),
                           [G[k].shape for k in SMALL])
    for k, g in zip(SMALL, summed):
        if k in SMALL_SHARDED:
            r, c = shard_shapes[k]
            g = lax.dynamic_slice(g.reshape(r, N_CHIPS * c), (0, chip * c), (r, c))
        grads[k] = g.reshape(shard_shapes[k])
    packed = [_pack_small([d[k] for k in SMALL]) for d in (
        {k: a[k] for k in SMALL}, grads, {k: a["m_" + k] for k in SMALL}, {k: a["v_" + k] for k in SMALL})]
    outs = adamw(*packed, "adamw_small")
    shapes = [shard_shapes[k] for k in SMALL]
    for d, rows in zip((deltas, new_m, new_v), outs):
        d.update(dict(zip(SMALL, _unpack_small(rows, shapes))))

    return (loss, grad_x[None], *[grads[k] for k in WEIGHTS], *[deltas[k] for k in WEIGHTS],
            *[new_m[k] for k in WEIGHTS], *[new_v[k] for k in WEIGHTS])
```
